```python
import math
import jax, jax.numpy as jnp
from jax import lax
import numpy as np

D_MODEL = 2048
BATCH = 4
SEQ = 4096
DEPTH = 4

N_MIXERS = 3
PLE_DIM = 256
NORM_EPS = 1e-6
MAX_POS_OFFSET = 1024

A_HEAD_DIM = 128
A_HEADS = D_MODEL // (2 * A_HEAD_DIM)
A_V_DIM = 2 * A_HEAD_DIM
A_WIDTH = A_HEADS * A_V_DIM
A_QK_WIDTH = A_HEADS * 2 * A_HEAD_DIM
A_IN_WIDTH = 2 * A_QK_WIDTH + 2 * A_WIDTH
ROT_DIM = A_HEAD_DIM // 4
ROPE_THETA = 500000.0
Q_BLOCK = 128
SUBLN_EPS = 1e-5

B_HEAD_DIM = 64
B_HEADS = D_MODEL // B_HEAD_DIM
B_DECAY_LORA = max(32, int(round(1.8 * D_MODEL ** 0.5 / 32)) * 32)
B_AAA_LORA = max(32, int(round(1.8 * D_MODEL ** 0.5 / 32)) * 32)
B_GN_EPS = 64e-5

C_HEAD_DIM = 128
C_K_HEADS = D_MODEL // C_HEAD_DIM
C_V_HEADS = 2 * C_K_HEADS
C_QK_WIDTH = C_K_HEADS * C_HEAD_DIM
C_V_WIDTH = C_V_HEADS * C_HEAD_DIM
C_CONV_CH = 2 * C_QK_WIDTH + C_V_WIDTH
C_IN_WIDTH = C_CONV_CH + C_V_WIDTH + 2 * C_V_HEADS
C_CONV_WIDTH = 4
C_CHUNK = 64

N_A_LAYERS = (DEPTH + 2) // 3
N_B_LAYERS = (DEPTH + 1) // 3
N_C_LAYERS = DEPTH // 3

kernel_name = 'hybrid_diffattn_rwkv7_gdn_trunk'


def rms_norm(x, g, eps=NORM_EPS):
    xf = x.astype(jnp.float32)
    y = xf * lax.rsqrt(jnp.mean(xf * xf, axis=-1, keepdims=True) + eps)
    return (y * g.astype(jnp.float32)).astype(x.dtype)


def l2_normalize(x, eps=1e-6):
    xf = x.astype(jnp.float32)
    return xf * lax.rsqrt(jnp.sum(xf * xf, axis=-1, keepdims=True) + eps)


def partial_rope(t, cos, sin):
    half = ROT_DIM // 2
    c = cos[:, :, None, None, :]
    s = sin[:, :, None, None, :]
    t1, t2, rest = t[..., :half], t[..., half:ROT_DIM], t[..., ROT_DIM:]
    return jnp.concatenate([t1 * c - t2 * s, t2 * c + t1 * s, rest.astype(jnp.float32)], axis=-1)


def diff_attention(h, positions, w_in, lam, subln_g, w_out, lam_init):
    B, T, _ = h.shape
    nb = T // Q_BLOCK
    q, k, v, z = jnp.split(h @ w_in, [A_QK_WIDTH, 2 * A_QK_WIDTH, 2 * A_QK_WIDTH + A_WIDTH], axis=-1)
    q = q.reshape(B, T, A_HEADS, 2, A_HEAD_DIM)
    k = k.reshape(B, T, A_HEADS, 2, A_HEAD_DIM)
    v = v.reshape(B, T, A_HEADS, A_V_DIM)
    inv_freq = ROPE_THETA ** (-jnp.arange(0, ROT_DIM, 2, dtype=jnp.float32) / ROT_DIM)
    ang = positions.astype(jnp.float32)[..., None] * inv_freq
    cos, sin = jnp.cos(ang), jnp.sin(ang)
    q = partial_rope(q.astype(jnp.float32), cos, sin) * (A_HEAD_DIM ** -0.5)
    k = partial_rope(k.astype(jnp.float32), cos, sin)
    lam32 = lam.astype(jnp.float32)
    lam_full = jnp.exp(jnp.sum(lam32[0] * lam32[1])) - jnp.exp(jnp.sum(lam32[2] * lam32[3])) + lam_init
    q_blocks = q.reshape(B, nb, Q_BLOCK, A_HEADS, 2, A_HEAD_DIM).transpose(1, 0, 3, 4, 2, 5)
    k_t = k.transpose(0, 2, 3, 1, 4)
    v_t = v.astype(jnp.float32).transpose(0, 2, 1, 3)
    key_pos = jnp.arange(T)

    def one_block(args):
        q_blk, blk = args
        s = jnp.einsum('bhcqd,bhckd->bhcqk', q_blk, k_t)
        q_pos = blk * Q_BLOCK + jnp.arange(Q_BLOCK)
        causal = key_pos[None, :] <= q_pos[:, None]
        p_attn = jax.nn.softmax(jnp.where(causal, s, -jnp.inf), axis=-1)
        diff_map = p_attn[:, :, 0] - lam_full * p_attn[:, :, 1]
        return jnp.einsum('bhqk,bhkv->bhqv', diff_map, v_t)

    o = lax.map(one_block, (q_blocks, jnp.arange(nb)))
    o = o.transpose(1, 0, 3, 2, 4).reshape(B, T, A_HEADS, A_V_DIM)
    o = rms_norm(o, subln_g, SUBLN_EPS) * (1.0 - lam_init)
    o = o.reshape(B, T, A_WIDTH) * jax.nn.silu(z.astype(jnp.float32))
    return (o @ w_out).astype(h.dtype)


def rwkv7_time_mix(h, mu, w_rkvg, w0, w_w1, w_w2, a0, w_a1, w_a2, k_k, k_a, r_k, ln_w, ln_b, w_out):
    B, T, D = h.shape
    H, N = B_HEADS, B_HEAD_DIM
    xx = jnp.pad(h, ((0, 0), (1, 0), (0, 0)))[:, :-1] - h
    x_r, x_w, x_k, x_v, x_a, x_g = [h + xx * mu[c] for c in range(6)]

    def heads(t):
        return t.reshape(B, T, H, N).astype(jnp.float32)

    r = heads(x_r @ w_rkvg[0])
    k = heads(x_k @ w_rkvg[1])
    v = heads(x_v @ w_rkvg[2])
    g = x_g @ w_rkvg[3]
    log_w = -jax.nn.softplus(-(w0 + jnp.tanh(x_w @ w_w1) @ w_w2).astype(jnp.float32)) - 0.5
    decay = heads(jnp.exp(-jnp.exp(log_w)))
    a = heads(jax.nn.sigmoid((a0 + (x_a @ w_a1) @ w_a2).astype(jnp.float32)))
    kk = l2_normalize(k * k_k.reshape(H, N))
    k = k * (1.0 + (a - 1.0) * k_a.reshape(H, N))

    def step(S, inp):
        r_t, w_t, k_t, v_t, kk_t, b_t = inp
        sa = jnp.einsum('bhvk,bhk->bhv', S, kk_t)
        S = S * w_t[:, :, None, :] - sa[..., None] * b_t[:, :, None, :] + v_t[..., None] * k_t[:, :, None, :]
        return S, jnp.einsum('bhvk,bhk->bhv', S, r_t)

    xs = tuple(jnp.swapaxes(t, 0, 1) for t in (r, decay, k, v, kk, kk * a))
    S0 = jnp.zeros((B, H, N, N), jnp.float32)
    _, o = lax.scan(step, S0, xs)
    o = jnp.swapaxes(o, 0, 1)
    mean = jnp.mean(o, axis=-1, keepdims=True)
    var = jnp.mean(jnp.square(o - mean), axis=-1, keepdims=True)
    o = (o - mean) * lax.rsqrt(var + B_GN_EPS) * ln_w.reshape(H, N) + ln_b.reshape(H, N)
    o = o + jnp.sum(r * k * r_k, axis=-1, keepdims=True) * v
    o = o.reshape(B, T, D) * jax.nn.silu(g.astype(jnp.float32))
    return (o @ w_out).astype(h.dtype)


def causal_conv(x, w):
    K = w.shape[0]
    T = x.shape[1]
    xp = jnp.pad(x, ((0, 0), (K - 1, 0), (0, 0)))
    return sum(xp[:, j:j + T] * w[j] for j in range(K))


def chunk_gated_delta_rule(q, k, v, g, beta):
    B, T, H, Dk = q.shape
    Dv = v.shape[-1]
    C = C_CHUNK
    N = T // C

    def chunks(t):
        return t.astype(jnp.float32).reshape(B, N, C, H, -1).transpose(1, 0, 3, 2, 4)

    qc, kc, vc = chunks(q), chunks(k), chunks(v)
    gc = jnp.cumsum(g.astype(jnp.float32).reshape(B, N, C, H).transpose(1, 0, 3, 2), axis=-1)
    bc = beta.astype(jnp.float32).reshape(B, N, C, H).transpose(1, 0, 3, 2)
    tri = jnp.tril(jnp.ones((C, C), dtype=bool))
    strict = jnp.tril(jnp.ones((C, C), dtype=bool), k=-1)
    decay = jnp.exp(jnp.where(tri, gc[..., :, None] - gc[..., None, :], -jnp.inf))
    kb = kc * bc[..., None]
    A = jnp.where(strict, jnp.einsum('nbhcd,nbhsd->nbhcs', kb, kc) * decay, 0.0)
    rhs = jnp.concatenate([vc * bc[..., None], kb * jnp.exp(gc)[..., None]], axis=-1)
    sol = lax.linalg.triangular_solve(A, rhs, left_side=True, lower=True, unit_diagonal=True)
    u, w = sol[..., :Dv], sol[..., Dv:]
    intra = jnp.where(tri, jnp.einsum('nbhcd,nbhsd->nbhcs', qc, kc) * decay, 0.0)

    def step(S, inp):
        q_, k_, u_, w_, g_, att = inp
        v_new = u_ - jnp.einsum('bhck,bhkv->bhcv', w_, S)
        o = jnp.einsum('bhck,bhkv->bhcv', q_ * jnp.exp(g_)[..., None], S) + jnp.einsum('bhcs,bhsv->bhcv', att, v_new)
        g_last = g_[..., -1:]
        S = S * jnp.exp(g_last)[..., None] + jnp.einsum('bhck,bhcv->bhkv', k_ * jnp.exp(g_last - g_)[..., None], v_new)
        return S, o

    S0 = jnp.zeros((B, H, Dk, Dv), jnp.float32)
    _, o = lax.scan(step, S0, (qc, kc, u, w, gc, intra))
    return o.transpose(1, 0, 3, 2, 4).reshape(B, T, H, Dv)


def gated_deltanet(h, w_in, conv_w, A_log, dt_bias, norm_g, w_out):
    B, T, _ = h.shape
    qkv, z, b, a = jnp.split(h @ w_in, [C_CONV_CH, C_CONV_CH + C_V_WIDTH, C_CONV_CH + C_V_WIDTH + C_V_HEADS], axis=-1)
    qkv = jax.nn.silu(causal_conv(qkv, conv_w))
    q, k, v = jnp.split(qkv, [C_QK_WIDTH, 2 * C_QK_WIDTH], axis=-1)
    rep = C_V_HEADS // C_K_HEADS
    q = jnp.repeat(l2_normalize(q.reshape(B, T, C_K_HEADS, C_HEAD_DIM)) * (C_HEAD_DIM ** -0.5), rep, axis=2)
    k = jnp.repeat(l2_normalize(k.reshape(B, T, C_K_HEADS, C_HEAD_DIM)), rep, axis=2)
    v = v.reshape(B, T, C_V_HEADS, C_HEAD_DIM)
    beta = jax.nn.sigmoid(b.astype(jnp.float32))
    g = -jnp.exp(A_log.astype(jnp.float32)) * jax.nn.softplus((a + dt_bias).astype(jnp.float32))
    o = chunk_gated_delta_rule(q, k, v, g, beta)
    o = rms_norm(o, norm_g) * jax.nn.silu(z.reshape(B, T, C_V_HEADS, C_HEAD_DIM).astype(jnp.float32))
    return (o.reshape(B, T, C_V_WIDTH) @ w_out).astype(h.dtype)


def setup_inputs(seed: int = 0) -> dict:
    key = jax.random.key(seed)
    keys = iter(jax.random.split(key, 40))

    def normal(shape, scale):
        return jax.random.normal(next(keys), shape, jnp.float32) * scale

    def gain(shape):
        return 1.0 + normal(shape, 0.02)

    d = D_MODEL
    x = normal((BATCH, SEQ, d), 1.0)
    p = normal((DEPTH, BATCH, SEQ, PLE_DIM), 1.0)
    positions = jnp.arange(SEQ, dtype=jnp.int32)[None, :] + jax.random.randint(next(keys), (BATCH, 1), 0, MAX_POS_OFFSET, dtype=jnp.int32)
    norm_g = gain((DEPTH, d))
    pe_norm_g = gain((DEPTH, d))
    pe_w_gate = normal((DEPTH, d, d), d ** -0.5)
    pe_w_proj = normal((DEPTH, PLE_DIM, d), PLE_DIM ** -0.5)
    final_norm_g = gain((d,))
    a_w_in = normal((N_A_LAYERS, d, A_IN_WIDTH), d ** -0.5)
    a_lam = normal((N_A_LAYERS, 4, A_HEAD_DIM), 0.1)
    a_subln_g = gain((N_A_LAYERS, A_V_DIM))
    a_w_out = normal((N_A_LAYERS, A_WIDTH, d), A_WIDTH ** -0.5)
    b_mu = jax.random.uniform(next(keys), (N_B_LAYERS, 6, d), jnp.float32)
    b_w_rkvg = normal((N_B_LAYERS, 4, d, d), d ** -0.5)
    b_w0 = jax.random.uniform(next(keys), (N_B_LAYERS, d), jnp.float32, -6.0, 1.0)
    b_w_w1 = normal((N_B_LAYERS, d, B_DECAY_LORA), d ** -0.5)
    b_w_w2 = normal((N_B_LAYERS, B_DECAY_LORA, d), 0.1 * B_DECAY_LORA ** -0.5)
    b_a0 = normal((N_B_LAYERS, d), 0.1)
    b_w_a1 = normal((N_B_LAYERS, d, B_AAA_LORA), d ** -0.5)
    b_w_a2 = normal((N_B_LAYERS, B_AAA_LORA, d), B_AAA_LORA ** -0.5)
    b_k_k = 0.85 + normal((N_B_LAYERS, d), 0.02)
    b_k_a = 1.0 + normal((N_B_LAYERS, d), 0.02)
    b_r_k = normal((N_B_LAYERS, B_HEADS, B_HEAD_DIM), 0.1)
    b_ln_w = gain((N_B_LAYERS, d))
    b_ln_b = normal((N_B_LAYERS, d), 0.01)
    b_w_out = normal((N_B_LAYERS, d, d), d ** -0.5)
    c_w_in = normal((N_C_LAYERS, d, C_IN_WIDTH), d ** -0.5)
    c_conv_w = normal((N_C_LAYERS, C_CONV_WIDTH, C_CONV_CH), C_CONV_WIDTH ** -0.5)
    c_A_log = jnp.log(jax.random.uniform(next(keys), (N_C_LAYERS, C_V_HEADS), jnp.float32, 1.0, 16.0))
    dt = jnp.exp(jax.random.uniform(next(keys), (N_C_LAYERS, C_V_HEADS), jnp.float32, math.log(1e-3), math.log(1e-1)))
    c_dt_bias = dt + jnp.log(-jnp.expm1(-dt))
    c_norm_g = gain((N_C_LAYERS, C_HEAD_DIM))
    c_w_out = normal((N_C_LAYERS, C_V_WIDTH, d), C_V_WIDTH ** -0.5)
    return {'x': x, 'p': p, 'positions': positions, 'norm_g': norm_g, 'pe_norm_g': pe_norm_g,
            'pe_w_gate': pe_w_gate, 'pe_w_proj': pe_w_proj, 'final_norm_g': final_norm_g,
            'a_w_in': a_w_in, 'a_lam': a_lam, 'a_subln_g': a_subln_g, 'a_w_out': a_w_out,
            'b_mu': b_mu, 'b_w_rkvg': b_w_rkvg, 'b_w0': b_w0, 'b_w_w1': b_w_w1, 'b_w_w2': b_w_w2,
            'b_a0': b_a0, 'b_w_a1': b_w_a1, 'b_w_a2': b_w_a2, 'b_k_k': b_k_k, 'b_k_a': b_k_a,
            'b_r_k': b_r_k, 'b_ln_w': b_ln_w, 'b_ln_b': b_ln_b, 'b_w_out': b_w_out,
            'c_w_in': c_w_in, 'c_conv_w': c_conv_w, 'c_A_log': c_A_log, 'c_dt_bias': c_dt_bias,
            'c_norm_g': c_norm_g, 'c_w_out': c_w_out}


def reference(x, p, positions, norm_g, pe_norm_g, pe_w_gate, pe_w_proj, final_norm_g,
              a_w_in, a_lam, a_subln_g, a_w_out,
              b_mu, b_w_rkvg, b_w0, b_w_w1, b_w_w2, b_a0, b_w_a1, b_w_a2, b_k_k, b_k_a,
              b_r_k, b_ln_w, b_ln_b, b_w_out,
              c_w_in, c_conv_w, c_A_log, c_dt_bias, c_norm_g, c_w_out):
    for i in range(DEPTH):
        kind = i % N_MIXERS
        j = i // N_MIXERS
        hn = rms_norm(x, norm_g[i])
        if kind == 0:
            lam_init = 0.8 - 0.6 * math.exp(-0.3 * i)
            y = diff_attention(hn, positions, a_w_in[j], a_lam[j], a_subln_g[j], a_w_out[j], lam_init)
        elif kind == 1:
            y = rwkv7_time_mix(hn, b_mu[j], b_w_rkvg[j], b_w0[j], b_w_w1[j], b_w_w2[j], b_a0[j],
                               b_w_a1[j], b_w_a2[j], b_k_k[j], b_k_a[j], b_r_k[j], b_ln_w[j],
                               b_ln_b[j], b_w_out[j])
        else:
            y = gated_deltanet(hn, c_w_in[j], c_conv_w[j], c_A_log[j], c_dt_bias[j], c_norm_g[j], c_w_out[j])
        x = x + y.astype(x.dtype)
        gate = jax.nn.sigmoid((rms_norm(x, pe_norm_g[i]) @ pe_w_gate[i]).astype(jnp.float32))
        x = x + (gate * (p[i] @ pe_w_proj[i]).astype(jnp.float32)).astype(x.dtype)
    return rms_norm(x, final_norm_g)
```

```python
import functools
import math

import jax
import jax.numpy as jnp
from jax import lax
from jax.experimental import pallas as pl
from jax.experimental.pallas import tpu as pltpu

F32 = jnp.float32
BF16 = jnp.bfloat16

N_MIXERS = 3
NORM_EPS = 1e-6
LANES = 128
VMEM_LIMIT = 48 * 1024 * 1024

A_HEAD_DIM = 128
A_V_DIM = 2 * A_HEAD_DIM
ROT_DIM = A_HEAD_DIM // 4
ROPE_THETA = 500000.0
SUBLN_EPS = 1e-5
ATTN_BLOCK = 256

B_HEAD_DIM = 64
B_GN_EPS = 64e-5
LORA_PAD = 128

C_HEAD_DIM = 128
C_CONV_WIDTH = 4
CHUNK = 64
SCAN_ROWS = 512

NN = (((1,), (0,)), ((), ()))
NT = (((1,), (1,)), ((), ()))


def _dot(a, b, dims=NN):
    return lax.dot_general(a, b, dims, preferred_element_type=F32)


def _split2(x):
    hi = x.astype(BF16)
    lo = (x - hi.astype(F32)).astype(BF16)
    return hi, lo


def _split3(x):
    hi = x.astype(BF16)
    r = x - hi.astype(F32)
    mid = r.astype(BF16)
    lo = (r - mid.astype(F32)).astype(BF16)
    return hi, mid, lo


def _dot3(a, b, dims=NN):
    ah, al = _split2(a)
    bh, bl = _split2(b)
    return _dot(ah, bh, dims) + (_dot(ah, bl, dims) + _dot(al, bh, dims))


def _dot_xl(a, b_exact, dims=NN):
    h, m, l = _split3(a)
    return _dot(h, b_exact, dims) + (_dot(m, b_exact, dims) + _dot(l, b_exact, dims))


def _dot_xr(a_exact, b, dims=NN):
    h, m, l = _split3(b)
    return _dot(a_exact, h, dims) + (_dot(a_exact, m, dims) + _dot(a_exact, l, dims))


def _iota(shape, dim):
    return lax.broadcasted_iota(jnp.int32, shape, dim)


def _silu(x):
    return x * jax.nn.sigmoid(x)


def _softplus(x):
    return jnp.maximum(x, 0.0) + jnp.log(1.0 + jnp.exp(-jnp.abs(x)))


def _params(*sem):
    return pltpu.CompilerParams(dimension_semantics=sem, vmem_limit_bytes=VMEM_LIMIT)


def _inv_unit_lower(a, nilpotency):
    n = a.shape[0]
    eye = (_iota((n, n), 0) == _iota((n, n), 1)).astype(F32)
    p = -a
    t = eye + p
    for _ in range(int(math.log2(nilpotency)) - 1):
        p = _dot3(p, p)
        t = t + _dot3(t, p)
    return t


def _rmsnorm_kernel(x_ref, g_ref, o_ref, *, eps):
    x = x_ref[...]
    y = x * lax.rsqrt(jnp.mean(x * x, axis=-1, keepdims=True) + eps)
    o_ref[...] = (y * g_ref[...]).astype(o_ref.dtype)


def rmsnorm(x, g, out_dtype, *, eps=NORM_EPS, tm=512):
    m, d = x.shape
    tm = min(tm, m)
    return pl.pallas_call(
        functools.partial(_rmsnorm_kernel, eps=eps),
        out_shape=jax.ShapeDtypeStruct((m, d), out_dtype),
        grid=(m // tm,),
        in_specs=[pl.BlockSpec((tm, d), lambda i: (i, 0)), pl.BlockSpec((1, d), lambda i: (0, 0))],
        out_specs=pl.BlockSpec((tm, d), lambda i: (i, 0)),
        compiler_params=_params("parallel"),
        name="rmsnorm",
    )(x, g.reshape(1, d))


def _mm_kernel(a_ref, w_ref, *rest, epilogue):
    o_ref = rest[-1]
    acc = _dot(a_ref[...], w_ref[...])
    if epilogue is not None:
        acc = epilogue(acc, *rest[:-1])
    o_ref[...] = acc.astype(o_ref.dtype)


def matmul(a, w, out_dtype, *, tm=1024, tn=1024, extra=(), extra_specs=(), epilogue=None, name="matmul"):
    m, k = a.shape
    n = w.shape[1]
    tm, tn = min(tm, m), min(tn, n)
    return pl.pallas_call(
        functools.partial(_mm_kernel, epilogue=epilogue),
        out_shape=jax.ShapeDtypeStruct((m, n), out_dtype),
        grid=(n // tn, m // tm),
        in_specs=[pl.BlockSpec((tm, k), lambda j, i: (i, 0)),
                  pl.BlockSpec((k, tn), lambda j, i: (0, j))] + list(extra_specs),
        out_specs=pl.BlockSpec((tm, tn), lambda j, i: (i, j)),
        compiler_params=_params("parallel", "parallel"),
        name=name,
    )(a, w, *extra)


def matmul_residual(a, w, res, *, tm=1024, tn=1024, name="matmul_residual"):
    tm, tn = min(tm, a.shape[0]), min(tn, w.shape[1])
    return matmul(a, w, F32, tm=tm, tn=tn, extra=(res,),
                  extra_specs=(pl.BlockSpec((tm, tn), lambda j, i: (i, j)),),
                  epilogue=lambda acc, r_ref: r_ref[...] + acc, name=name)


def _ple_kernel(h_ref, wg_ref, p_ref, wp_ref, x_ref, o_ref):
    gate = jax.nn.sigmoid(_dot(h_ref[...], wg_ref[...]))
    proj = _dot(p_ref[...].astype(BF16), wp_ref[...])
    o_ref[...] = x_ref[...] + gate * proj


def per_layer_embedding(x, hn, w_gate, p, w_proj, *, tm=1024, tn=1024):
    m, d = x.shape
    pd = p.shape[1]
    tm, tn = min(tm, m), min(tn, d)
    return pl.pallas_call(
        _ple_kernel,
        out_shape=jax.ShapeDtypeStruct((m, d), F32),
        grid=(d // tn, m // tm),
        in_specs=[pl.BlockSpec((tm, d), lambda j, i: (i, 0)),
                  pl.BlockSpec((d, tn), lambda j, i: (0, j)),
                  pl.BlockSpec((tm, pd), lambda j, i: (i, 0)),
                  pl.BlockSpec((pd, tn), lambda j, i: (0, j)),
                  pl.BlockSpec((tm, tn), lambda j, i: (i, j))],
        out_specs=pl.BlockSpec((tm, tn), lambda j, i: (i, j)),
        compiler_params=_params("parallel", "parallel"),
        name="per_layer_embedding",
    )(hn, w_gate, p, w_proj, x)


def _rope_table_kernel(pos_ref, freq_ref, cos_ref, sin_lo_ref, sin_hi_ref):
    half = ROT_DIM // 2
    ang = pos_ref[...].astype(F32) * freq_ref[...]
    lane = _iota(ang.shape, 1)
    c, s = jnp.cos(ang), jnp.sin(ang)
    cos_ref[...] = jnp.where(lane < ROT_DIM, c, 1.0)
    sin_lo_ref[...] = jnp.where(lane < half, -s, 0.0)
    sin_hi_ref[...] = jnp.where((lane >= half) & (lane < ROT_DIM), s, 0.0)


def rope_tables(positions, *, tm=1024):
    m = positions.size
    tm = min(tm, m)
    inv_freq = ROPE_THETA ** (-jnp.arange(0, ROT_DIM, 2, dtype=F32) / ROT_DIM)
    freq_row = jnp.concatenate([inv_freq, inv_freq, jnp.zeros((LANES - ROT_DIM,), F32)]).reshape(1, LANES)
    shape = jax.ShapeDtypeStruct((m, LANES), F32)
    spec = pl.BlockSpec((tm, LANES), lambda i: (i, 0))
    return pl.pallas_call(
        _rope_table_kernel,
        out_shape=(shape, shape, shape),
        grid=(m // tm,),
        in_specs=[pl.BlockSpec((tm, 1), lambda i: (i, 0)), pl.BlockSpec((1, LANES), lambda i: (0, 0))],
        out_specs=(spec, spec, spec),
        compiler_params=_params("parallel"),
        name="rope_tables",
    )(positions.reshape(m, 1), freq_row)


def _attn_in_kernel(a_ref, w_ref, cos_ref, sin_lo_ref, sin_hi_ref, o_ref, *, n_q_blocks):
    j = pl.program_id(0)
    acc = _dot(a_ref[...], w_ref[...])
    half = ROT_DIM // 2

    @pl.when(j < 2 * n_q_blocks)
    def _():
        scale = jnp.where(j < n_q_blocks, A_HEAD_DIM ** -0.5, 1.0).astype(F32)
        cos, sin_lo, sin_hi = cos_ref[...], sin_lo_ref[...], sin_hi_ref[...]
        for g in range(acc.shape[1] // LANES):
            x = acc[:, g * LANES:(g + 1) * LANES]
            y = x * cos + pltpu.roll(x, LANES - half, 1) * sin_lo + pltpu.roll(x, half, 1) * sin_hi
            o_ref[:, g * LANES:(g + 1) * LANES] = (y * scale).astype(o_ref.dtype)

    @pl.when(j >= 2 * n_q_blocks)
    def _():
        o_ref[...] = acc.astype(o_ref.dtype)


def attn_in_proj(hn, w_in, tables, qk_width, *, tm=1024, tn=1024):
    m, k = hn.shape
    n = w_in.shape[1]
    tm, tn = min(tm, m), min(tn, qk_width)
    tspec = pl.BlockSpec((tm, LANES), lambda j, i: (i, 0))
    return pl.pallas_call(
        functools.partial(_attn_in_kernel, n_q_blocks=qk_width // tn),
        out_shape=jax.ShapeDtypeStruct((m, n), BF16),
        grid=(n // tn, m // tm),
        in_specs=[pl.BlockSpec((tm, k), lambda j, i: (i, 0)),
                  pl.BlockSpec((k, tn), lambda j, i: (0, j)), tspec, tspec, tspec],
        out_specs=pl.BlockSpec((tm, tn), lambda j, i: (i, j)),
        compiler_params=_params("parallel", "parallel"),
        name="attn_in_proj",
    )(hn, w_in, *tables)


def _diff_attn_kernel(lam_ref, q_ref, k_ref, v_ref, z_ref, g_ref, o_ref, m_ref, l_ref, acc_ref, *, blk, lam_init):
    i = pl.program_id(2)
    lam = lam_ref[...]
    lam_full = (jnp.exp(jnp.sum(lam[0:1] * lam[1:2], axis=-1, keepdims=True))
                - jnp.exp(jnp.sum(lam[2:3] * lam[3:4], axis=-1, keepdims=True)) + lam_init)
    m_ref[...] = jnp.full(m_ref.shape, -jnp.inf, F32)
    l_ref[...] = jnp.zeros(l_ref.shape, F32)
    acc_ref[...] = jnp.zeros(acc_ref.shape, F32)
    q = q_ref[...]
    d = A_HEAD_DIM

    def step(j, masked):
        start = pl.multiple_of(j * blk, blk)
        kb = k_ref[pl.ds(start, blk), :]
        vb = v_ref[pl.ds(start, blk), :]
        for c in range(2):
            s = _dot(q[:, c * d:(c + 1) * d], kb[:, c * d:(c + 1) * d], NT)
            if masked:
                s = jnp.where(_iota(s.shape, 0) >= _iota(s.shape, 1), s, -jnp.inf)
            m_prev = m_ref[c]
            m_new = jnp.maximum(m_prev, jnp.max(s, axis=-1, keepdims=True))
            alpha = jnp.exp(m_prev - m_new)
            p = jnp.exp(s - m_new)
            l_ref[c] = alpha * l_ref[c] + jnp.sum(p, axis=-1, keepdims=True)
            acc_ref[c] = alpha * acc_ref[c] + _dot(p.astype(BF16), vb)
            m_ref[c] = m_new

    def body(j, carry):
        step(j, False)
        return carry

    lax.fori_loop(0, i, body, 0)
    step(i, True)
    o = acc_ref[0] / l_ref[0] - lam_full * (acc_ref[1] / l_ref[1])
    o = o * lax.rsqrt(jnp.mean(o * o, axis=-1, keepdims=True) + SUBLN_EPS) * g_ref[...]
    o = o * (1.0 - lam_init)
    o_ref[...] = (o * _silu(z_ref[...].astype(F32))).astype(o_ref.dtype)


def diff_attention_core(qkvz, lam, subln_g, batch, seq, heads, lam_init):
    m = qkvz.shape[0]
    blk = min(ATTN_BLOCK, seq)
    nq = seq // blk
    w = A_V_DIM
    return pl.pallas_call(
        functools.partial(_diff_attn_kernel, blk=blk, lam_init=lam_init),
        out_shape=jax.ShapeDtypeStruct((m, heads * w), BF16),
        grid=(batch, heads, nq),
        in_specs=[pl.BlockSpec((4, A_HEAD_DIM), lambda b, h, i: (0, 0)),
                  pl.BlockSpec((blk, w), lambda b, h, i: (b * nq + i, h)),
                  pl.BlockSpec((seq, w), lambda b, h, i: (b, heads + h)),
                  pl.BlockSpec((seq, w), lambda b, h, i: (b, 2 * heads + h)),
                  pl.BlockSpec((blk, w), lambda b, h, i: (b * nq + i, 3 * heads + h)),
                  pl.BlockSpec((1, w), lambda b, h, i: (0, 0))],
        out_specs=pl.BlockSpec((blk, w), lambda b, h, i: (b * nq + i, h)),
        scratch_shapes=[pltpu.VMEM((2, blk, 1), F32), pltpu.VMEM((2, blk, 1), F32),
                        pltpu.VMEM((2, blk, w), F32)],
        compiler_params=_params("parallel", "parallel", "parallel"),
        name="diff_attention",
    )(lam, qkvz, qkvz, qkvz, qkvz, subln_g.reshape(1, w))


def diff_attention_layer(x, hn, tables, w_in, lam, subln_g, w_out, batch, seq, lam_init):
    d = x.shape[1]
    heads = d // A_V_DIM
    qkvz = attn_in_proj(hn, w_in.astype(BF16), tables, heads * 2 * A_HEAD_DIM)
    o = diff_attention_core(qkvz, lam, subln_g, batch, seq, heads, lam_init)
    return matmul_residual(o, w_out.astype(BF16), x, name="attn_out_proj")


def _gdn_in_kernel(a_ref, w_ref, cw_ref, o_ref, tail_ref, *, n_conv_blocks, rows_per_seq):
    j = pl.program_id(0)
    i = pl.program_id(1)
    acc = _dot(a_ref[...], w_ref[...])
    tm = acc.shape[0]

    @pl.when(j < n_conv_blocks)
    def _():
        @pl.when((i * tm) % rows_per_seq == 0)
        def _():
            tail_ref[...] = jnp.zeros(tail_ref.shape, F32)

        tail = tail_ref[...]
        sub = _iota(tail.shape, 0)

        def shifted(s):
            xs = pltpu.roll(acc, s, 0)
            head = jnp.where(sub < s, pltpu.roll(tail, s, 0), xs[:8])
            return jnp.concatenate([head, xs[8:]], axis=0)

        cw = cw_ref[...]
        last = C_CONV_WIDTH - 1
        y = shifted(last) * cw[0:1]
        for t in range(1, last):
            y = y + shifted(last - t) * cw[t:t + 1]
        y = y + acc * cw[last:last + 1]
        tail_ref[...] = acc[tm - 8:]
        o_ref[...] = _silu(y).astype(o_ref.dtype)

    @pl.when(j >= n_conv_blocks)
    def _():
        o_ref[...] = acc.astype(o_ref.dtype)


def gdn_in_proj(hn, w, conv_w, seq, *, tm=1024, tn=1024):
    m, k = hn.shape
    n = w.shape[1]
    conv_ch = conv_w.shape[1]
    tm, tn = min(tm, seq), min(tn, n)
    ncb = conv_ch // tn
    return pl.pallas_call(
        functools.partial(_gdn_in_kernel, n_conv_blocks=ncb, rows_per_seq=seq),
        out_shape=jax.ShapeDtypeStruct((m, n), BF16),
        grid=(n // tn, m // tm),
        in_specs=[pl.BlockSpec((tm, k), lambda j, i: (i, 0)),
                  pl.BlockSpec((k, tn), lambda j, i: (0, j)),
                  pl.BlockSpec((C_CONV_WIDTH, tn), lambda j, i: (0, jnp.minimum(j, ncb - 1)))],
        out_specs=pl.BlockSpec((tm, tn), lambda j, i: (i, j)),
        scratch_shapes=[pltpu.VMEM((8, tn), F32)],
        compiler_params=_params("arbitrary", "arbitrary"),
        name="gdn_in_proj",
    )(hn, w, conv_w)


def _pair_masks():
    n = 2 * CHUNK
    r, c = _iota((n, n), 0), _iota((n, n), 1)
    same = (r // CHUNK) == (c // CHUNK)
    return same & (r >= c), same & (r > c), r == c


def _gdn_prep_kernel(q_ref, k_ref, v_ref, b_ref, a_ref, alog_ref, dtb_ref,
                     w_ref, qg_ref, u_ref, att_ref, kdt_ref, dec_ref):
    h = pl.program_id(1)
    c = CHUNK
    n2 = 2 * c
    dk = C_HEAD_DIM
    tri, strict, diag = _pair_masks()
    tri_b = tri.astype(BF16)
    eye = diag.astype(BF16)
    sel = (_iota((LANES, dk), 0) == h).astype(BF16)
    first = (_iota((n2, dk), 1) == 0).astype(BF16)
    upper_half = _iota((n2, dk), 0) < c
    beta_all = jax.nn.sigmoid(b_ref[...])
    g_all = -jnp.exp(alog_ref[...]) * _softplus(a_ref[...] + dtb_ref[...])
    for n in range(q_ref.shape[0] // n2):
        rows = slice(n * n2, (n + 1) * n2)
        qf = q_ref[rows, :].astype(F32)
        kf = k_ref[rows, :].astype(F32)
        vf = v_ref[rows, :].astype(F32)
        qn = qf * lax.rsqrt(jnp.sum(qf * qf, axis=-1, keepdims=True) + 1e-6) * (dk ** -0.5)
        kn = kf * lax.rsqrt(jnp.sum(kf * kf, axis=-1, keepdims=True) + 1e-6)
        beta = _dot_xl(beta_all[rows], sel)
        g = _dot_xl(g_all[rows], sel)
        gc = _dot_xr(tri_b, g)
        gc_row = _dot_xr(first, gc, NT)
        decay = jnp.where(tri, jnp.exp(jnp.where(tri, gc - gc_row, 0.0)), 0.0)
        kb = kn * beta
        kn_b = kn.astype(BF16)
        a_mat = jnp.where(strict, _dot(kb.astype(BF16), kn_b, NT) * decay, 0.0)
        att = _dot(qn.astype(BF16), kn_b, NT) * decay
        t_inv = _inv_unit_lower(a_mat, c)
        egc = jnp.exp(gc)
        sol = _dot3(t_inv, jnp.concatenate([vf * beta, kb * egc], axis=1))
        g_last = jnp.where(upper_half, gc[c - 1:c, :], gc[n2 - 1:n2, :])
        kdec = kn * jnp.exp(g_last - gc)
        u_ref[rows, :] = sol[:, :dk]
        w_ref[rows, :] = sol[:, dk:].astype(BF16)
        qg_ref[rows, :] = (qn * egc).astype(BF16)
        att_ref[rows, :] = att.astype(BF16)
        kdt_ref[:, rows] = _dot(eye, kdec.astype(BF16), NT).astype(BF16)
        dec_ref[2 * n:2 * n + 1, :] = jnp.exp(gc[c - 1:c, :])
        dec_ref[2 * n + 1:2 * n + 2, :] = jnp.exp(gc[n2 - 1:n2, :])


def _gdn_scan_kernel(w_ref, qg_ref, u_ref, att_ref, kdt_ref, dec_ref, z_ref, g_ref, o_ref, s_ref, *, group):
    c = CHUNK
    dk = C_HEAD_DIM
    zeros = jnp.zeros((c, dk), BF16)

    @pl.when(pl.program_id(2) == 0)
    def _():
        s_ref[...] = jnp.zeros(s_ref.shape, F32)

    for n in range(w_ref.shape[1] // c):
        rows = slice(n * c, (n + 1) * c)
        pair = slice((n // 2) * 2 * c, (n // 2 + 1) * 2 * c)
        for gi in range(group):
            s = s_ref[gi]
            lhs = jnp.concatenate([w_ref[gi, rows, :], qg_ref[gi, rows, :]], axis=0)
            ms = _dot(lhs, s.astype(BF16))
            v_new = (u_ref[gi, rows, :] - ms[:c]).astype(BF16)
            v_pad = jnp.concatenate([v_new, zeros] if n % 2 == 0 else [zeros, v_new], axis=0)
            o = ms[c:] + _dot(att_ref[gi, rows, :], v_pad)
            s_ref[gi] = s * dec_ref[gi, n:n + 1, :] + _dot(kdt_ref[gi, :, pair], v_pad)
            o = o * lax.rsqrt(jnp.mean(o * o, axis=-1, keepdims=True) + NORM_EPS) * g_ref[...]
            z = z_ref[rows, gi * dk:(gi + 1) * dk].astype(F32)
            o_ref[rows, gi * dk:(gi + 1) * dk] = (o * _silu(z)).astype(o_ref.dtype)


def gated_deltanet_layer(x, hn, w_in, conv_w, a_log, dt_bias, norm_g, w_out, batch, seq):
    m, d = x.shape
    dk = C_HEAD_DIM
    k_heads = d // dk
    v_heads = 2 * k_heads
    conv_ch = 2 * k_heads * dk + v_heads * dk
    main_w = conv_ch + v_heads * dk
    qkvz = gdn_in_proj(hn, w_in[:, :main_w].astype(BF16), conv_w, seq)
    pad = jnp.zeros((d, LANES - v_heads), F32)
    w_ba = jnp.concatenate([w_in[:, main_w:main_w + v_heads], pad, w_in[:, main_w + v_heads:], pad], axis=1)
    ba = matmul(hn, w_ba.astype(BF16), F32, tn=2 * LANES, name="gdn_ba_proj")
    row_pad = jnp.zeros((LANES - v_heads,), F32)
    alog_row = jnp.concatenate([a_log, row_pad]).reshape(1, LANES)
    dtb_row = jnp.concatenate([dt_bias, row_pad]).reshape(1, LANES)

    rows = min(SCAN_ROWS, seq)
    nr = seq // rows
    ncr = rows // CHUNK
    hv = v_heads
    bh_t = lambda dt, last: jax.ShapeDtypeStruct((batch, hv, seq, last), dt)
    blk4 = lambda last: pl.BlockSpec((None, None, rows, last), lambda b, h, i: (b, h, i, 0))
    qoff, koff, voff = 0, k_heads, 2 * k_heads
    w_, qg_, u_, att_, kdt_, dec_ = pl.pallas_call(
        _gdn_prep_kernel,
        out_shape=(bh_t(BF16, dk), bh_t(BF16, dk), bh_t(F32, dk), bh_t(BF16, 2 * CHUNK),
                   jax.ShapeDtypeStruct((batch, hv, dk, seq), BF16),
                   jax.ShapeDtypeStruct((batch, hv, seq // CHUNK, dk), F32)),
        grid=(batch, hv, nr),
        in_specs=[pl.BlockSpec((rows, dk), lambda b, h, i: (b * nr + i, qoff + h // 2)),
                  pl.BlockSpec((rows, dk), lambda b, h, i: (b * nr + i, koff + h // 2)),
                  pl.BlockSpec((rows, dk), lambda b, h, i: (b * nr + i, voff + h)),
                  pl.BlockSpec((rows, LANES), lambda b, h, i: (b * nr + i, 0)),
                  pl.BlockSpec((rows, LANES), lambda b, h, i: (b * nr + i, 1)),
                  pl.BlockSpec((1, LANES), lambda b, h, i: (0, 0)),
                  pl.BlockSpec((1, LANES), lambda b, h, i: (0, 0))],
        out_specs=(blk4(dk), blk4(dk), blk4(dk), blk4(2 * CHUNK),
                   pl.BlockSpec((None, None, dk, rows), lambda b, h, i: (b, h, 0, i)),
                   pl.BlockSpec((None, None, ncr, dk), lambda b, h, i: (b, h, i, 0))),
        compiler_params=_params("parallel", "parallel", "parallel"),
        name="gdn_prep",
    )(qkvz, qkvz, qkvz, ba, ba, alog_row, dtb_row)

    group = 4
    zoff = conv_ch // (group * dk)
    gblk = lambda last: pl.BlockSpec((None, group, rows, last), lambda b, h, i: (b, h, i, 0))
    o = pl.pallas_call(
        functools.partial(_gdn_scan_kernel, group=group),
        out_shape=jax.ShapeDtypeStruct((m, hv * dk), BF16),
        grid=(batch, hv // group, nr),
        in_specs=[gblk(dk), gblk(dk), gblk(dk), gblk(2 * CHUNK),
                  pl.BlockSpec((None, group, dk, rows), lambda b, h, i: (b, h, 0, i)),
                  pl.BlockSpec((None, group, ncr, dk), lambda b, h, i: (b, h, i, 0)),
                  pl.BlockSpec((rows, group * dk), lambda b, h, i: (b * nr + i, zoff + h)),
                  pl.BlockSpec((1, dk), lambda b, h, i: (0, 0))],
        out_specs=pl.BlockSpec((rows, group * dk), lambda b, h, i: (b * nr + i, h)),
        scratch_shapes=[pltpu.VMEM((group, dk, dk), F32)],
        compiler_params=_params("parallel", "parallel", "arbitrary"),
        name="gdn_scan",
    )(w_, qg_, u_, att_, kdt_, dec_, qkvz, norm_g.reshape(1, dk))
    return matmul_residual(o, w_out.astype(BF16), x, tm=512, name="gdn_out_proj")


def _rwkv_mix_kernel(x_ref, g_ref, mu_ref, o_ref, tail_ref, *, rows_per_seq):
    i = pl.program_id(0)
    x = x_ref[...]
    tm = x.shape[0]
    hn = x * lax.rsqrt(jnp.mean(x * x, axis=-1, keepdims=True) + NORM_EPS) * g_ref[...]

    @pl.when((i * tm) % rows_per_seq == 0)
    def _():
        tail_ref[...] = jnp.zeros(tail_ref.shape, F32)

    prev = jnp.where(_iota(hn.shape, 0) == 0, tail_ref[7:8, :], pltpu.roll(hn, 1, 0))
    tail_ref[...] = hn[tm - 8:]
    xx = prev - hn
    for c in range(o_ref.shape[0]):
        o_ref[c] = (hn + xx * mu_ref[c:c + 1, :]).astype(o_ref.dtype)


def rwkv_token_mix(x, norm_g, mu, seq, *, tm=256):
    m, d = x.shape
    tm = min(tm, seq)
    nmix = mu.shape[0]
    return pl.pallas_call(
        functools.partial(_rwkv_mix_kernel, rows_per_seq=seq),
        out_shape=jax.ShapeDtypeStruct((nmix, m, d), BF16),
        grid=(m // tm,),
        in_specs=[pl.BlockSpec((tm, d), lambda i: (i, 0)),
                  pl.BlockSpec((1, d), lambda i: (0, 0)),
                  pl.BlockSpec((nmix, d), lambda i: (0, 0))],
        out_specs=pl.BlockSpec((nmix, tm, d), lambda i: (0, i, 0)),
        scratch_shapes=[pltpu.VMEM((8, d), F32)],
        compiler_params=_params("arbitrary"),
        name="rwkv_token_mix",
    )(x, norm_g.reshape(1, d), mu)


def _grouped_mm_kernel(a_ref, w_ref, o_ref):
    o_ref[...] = _dot(a_ref[...], w_ref[...]).astype(o_ref.dtype)


def grouped_matmul(a, w, out_dtype, *, tm=1024, tn=1024):
    g, k, n = w.shape
    m = a.shape[1]
    tm, tn = min(tm, m), min(tn, n)
    return pl.pallas_call(
        _grouped_mm_kernel,
        out_shape=jax.ShapeDtypeStruct((g, m, n), out_dtype),
        grid=(g, n // tn, m // tm),
        in_specs=[pl.BlockSpec((None, tm, k), lambda c, j, i: (c, i, 0)),
                  pl.BlockSpec((None, k, tn), lambda c, j, i: (c, 0, j))],
        out_specs=pl.BlockSpec((None, tm, tn), lambda c, j, i: (c, i, j)),
        compiler_params=_params("parallel", "parallel", "parallel"),
        name="rwkv_rkvg_proj",
    )(a, w)


def _rwkv_lora_kernel(xw_ref, xa_ref, w1_ref, w2_ref, a1_ref, a2_ref, w0_ref, a0_ref, ld_ref, a_ref):
    hw = jnp.tanh(_dot(xw_ref[...], w1_ref[...])).astype(BF16)
    lw = w0_ref[...] + _dot(hw, w2_ref[...])
    log_w = -_softplus(-lw) - 0.5
    ld_ref[...] = -jnp.exp(log_w)
    ha = _dot(xa_ref[...], a1_ref[...]).astype(BF16)
    a_ref[...] = jax.nn.sigmoid(a0_ref[...] + _dot(ha, a2_ref[...]))


def rwkv_lora(xs, w1, w2, a1, a2, w0, a0, *, tm=512):
    _, m, d = xs.shape
    tm = min(tm, m)
    r = w1.shape[1]
    row = pl.BlockSpec((1, d), lambda i: (0, 0))
    shape = jax.ShapeDtypeStruct((m, d), F32)
    return pl.pallas_call(
        _rwkv_lora_kernel,
        out_shape=(shape, shape),
        grid=(m // tm,),
        in_specs=[pl.BlockSpec((None, tm, d), lambda i: (4, i, 0)),
                  pl.BlockSpec((None, tm, d), lambda i: (5, i, 0)),
                  pl.BlockSpec((d, r), lambda i: (0, 0)), pl.BlockSpec((r, d), lambda i: (0, 0)),
                  pl.BlockSpec((d, r), lambda i: (0, 0)), pl.BlockSpec((r, d), lambda i: (0, 0)),
                  row, row],
        out_specs=(pl.BlockSpec((tm, d), lambda i: (i, 0)), pl.BlockSpec((tm, d), lambda i: (i, 0))),
        compiler_params=_params("parallel"),
        name="rwkv_lora",
    )(xs, xs, w1, w2, a1, a2, w0.reshape(1, d), a0.reshape(1, d))


def _rwkv_prep_kernel(r_ref, k_ref, v_ref, ld_ref, a_ref, kk_ref, ka_ref, rk_ref,
                      wt_ref, rt_ref, u_ref, aro_ref, kbt_ref, vb_ref, dec_ref, bonus_ref):
    c = CHUNK
    n2 = 2 * c
    n_h = B_HEAD_DIM
    lane_head = _iota((1, LANES), 1) // n_h
    seg = (_iota((LANES, LANES), 0) // n_h == _iota((LANES, LANES), 1) // n_h).astype(BF16)
    tri, strict, diag = _pair_masks()
    tri_b = tri.astype(BF16)
    eye = diag.astype(BF16)
    upper_half = _iota((n2, LANES), 0) < c

    r = r_ref[...]
    k = k_ref[...]
    v = v_ref[...]
    a = a_ref[...]
    kk_raw = k * kk_ref[...]
    kk = kk_raw * lax.rsqrt(_dot_xl(kk_raw * kk_raw, seg) + 1e-6)
    k2 = k * (1.0 + (a - 1.0) * ka_ref[...])
    bb = kk * a
    bonus_ref[...] = _dot_xl(r * k2 * rk_ref[...], seg) * v
    for n in range(r.shape[0] // n2):
        rows = slice(n * n2, (n + 1) * n2)
        ld = ld_ref[rows, :]
        cs = _dot_xr(tri_b, ld)
        cs_last = jnp.where(upper_half, cs[c - 1:c, :], cs[n2 - 1:n2, :])
        w_inv = jnp.exp(-cs)
        w_end = jnp.exp(cs_last - cs)
        rt = r[rows] * jnp.exp(cs)
        kkt = kk[rows] * jnp.exp(cs - ld)
        k2c, bbc, vc = k2[rows], bb[rows], v[rows]
        kbar = k2c * w_inv
        bbar = bbc * w_inv
        wt = jnp.zeros((n2, LANES), F32)
        ut = jnp.zeros((n2, LANES), F32)
        for hh in range(2):
            own = lane_head == hh
            kkt_m = jnp.where(own, kkt, 0.0)
            rt_m = jnp.where(own, rt, 0.0)
            a_kb = jnp.where(strict, _dot3(kkt_m, bbar, NT), 0.0)
            a_kk = jnp.where(strict, _dot3(kkt_m, kbar, NT), 0.0)
            a_rk = jnp.where(tri, _dot3(rt_m, kbar, NT), 0.0)
            a_rb = jnp.where(tri, _dot3(rt_m, bbar, NT), 0.0)
            t_inv = _inv_unit_lower(a_kb, c)
            wt = jnp.where(own, _dot3(t_inv, kkt), wt)
            ut = jnp.where(own, _dot3(t_inv, _dot3(a_kk, vc)), ut)
            aro_ref[hh, rows, :] = jnp.concatenate([a_rk, -a_rb], axis=1).astype(BF16)
        wt_ref[rows, :] = wt.astype(BF16)
        rt_ref[rows, :] = rt.astype(BF16)
        u_ref[rows, :] = ut
        kw, bw = (k2c * w_end).astype(BF16), (-(bbc * w_end)).astype(BF16)
        for half in range(2):
            hs = slice(half * c, (half + 1) * c)
            kb_end = jnp.concatenate([kw[hs], bw[hs]], axis=0)
            cols = slice((2 * n + half) * n2, (2 * n + half + 1) * n2)
            kbt_ref[:, cols] = _dot(eye, kb_end, NT).astype(BF16)
        vb_ref[rows, :] = vc.astype(BF16)
        dec_ref[2 * n:2 * n + 1, :] = jnp.exp(cs[c - 1:c, :])
        dec_ref[2 * n + 1:2 * n + 2, :] = jnp.exp(cs[n2 - 1:n2, :])


def _rwkv_scan_kernel(wt_ref, rt_ref, u_ref, aro_ref, kbt_ref, vb_ref, dec_ref, bonus_ref, gate_ref,
                      lnw_ref, lnb_ref, o_ref, s_ref, *, group):
    c = CHUNK
    n2 = 2 * c
    n_h = B_HEAD_DIM
    lane_head = _iota((1, LANES), 1) // n_h
    r_i, c_i = _iota((LANES, LANES), 0), _iota((LANES, LANES), 1)
    same_head = r_i // n_h == c_i // n_h
    avg = jnp.where(same_head, 1.0 / n_h, 0.0).astype(BF16)
    eye = (r_i == c_i).astype(BF16)
    zeros = jnp.zeros((c, LANES), BF16)

    @pl.when(pl.program_id(2) == 0)
    def _():
        s_ref[...] = jnp.zeros(s_ref.shape, F32)

    for n in range(u_ref.shape[1] // c):
        rows = slice(n * c, (n + 1) * c)
        for gi in range(group):
            s = s_ref[gi]
            lhs = jnp.concatenate([wt_ref[gi, rows, :], rt_ref[gi, rows, :]], axis=0)
            ms = _dot(lhs, s.astype(BF16))
            sa = (u_ref[gi, rows, :] + ms[:c]).astype(BF16)
            vb = vb_ref[gi, rows, :]
            x_pad = jnp.concatenate([vb, zeros, sa, zeros] if n % 2 == 0 else [zeros, vb, zeros, sa], axis=0)
            o0 = _dot(aro_ref[gi, 0, rows, :], x_pad)
            o1 = _dot(aro_ref[gi, 1, rows, :], x_pad)
            o = ms[c:] + jnp.where(lane_head == 0, o0, o1)
            upd = _dot(kbt_ref[gi, :, n * n2:(n + 1) * n2], jnp.concatenate([vb, sa], axis=0))
            dec_col = _dot_xr(eye, jnp.broadcast_to(dec_ref[gi, n:n + 1, :], (LANES, LANES)), NT)
            s_ref[gi] = s * dec_col + jnp.where(same_head, upd, 0.0)
            mean = _dot_xl(o, avg)
            dlt = o - mean
            var = _dot_xl(dlt * dlt, avg)
            cols = slice(gi * LANES, (gi + 1) * LANES)
            y = dlt * lax.rsqrt(var + B_GN_EPS) * lnw_ref[:, cols] + lnb_ref[:, cols]
            y = y + bonus_ref[rows, cols]
            o_ref[rows, cols] = (y * _silu(gate_ref[rows, cols])).astype(o_ref.dtype)


def rwkv7_layer(x, norm_g, mu, w_rkvg, w0, w_w1, w_w2, a0, w_a1, w_a2, k_k, k_a, r_k, ln_w, ln_b, w_out,
                batch, seq):
    m, d = x.shape
    heads = d // B_HEAD_DIM
    pairs = heads // 2
    order = jnp.array([0, 2, 3, 5, 1, 4])
    xs = rwkv_token_mix(x, norm_g, mu[order], seq)
    rkvg = grouped_matmul(xs, w_rkvg.astype(BF16), F32)
    lora = w_w1.shape[1]
    padc = lambda w: jnp.pad(w, ((0, 0), (0, LORA_PAD - lora))).astype(BF16)
    padr = lambda w: jnp.pad(w, ((0, LORA_PAD - lora), (0, 0))).astype(BF16)
    ld, a = rwkv_lora(xs, padc(w_w1), padr(w_w2), padc(w_a1), padr(w_a2), w0, a0)

    rows = min(SCAN_ROWS, seq)
    nr = seq // rows
    ncr = rows // CHUNK
    col = lambda g: pl.BlockSpec((None, rows, LANES), lambda b, p, i, g=g: (g, b * nr + i, p))
    flat = pl.BlockSpec((rows, LANES), lambda b, p, i: (b * nr + i, p))
    prow = pl.BlockSpec((1, LANES), lambda b, p, i: (0, p))
    bp = lambda rws, last, dt: jax.ShapeDtypeStruct((batch, pairs, rws, last), dt)
    pblk = lambda rws, last: pl.BlockSpec((None, None, rws, last), lambda b, p, i: (b, p, i, 0))
    wt_, rt_, u_, aro_, kbt_, vb_, dec_, bonus = pl.pallas_call(
        _rwkv_prep_kernel,
        out_shape=(bp(seq, LANES, BF16), bp(seq, LANES, BF16), bp(seq, LANES, F32),
                   jax.ShapeDtypeStruct((batch, pairs, 2, seq, 4 * CHUNK), BF16),
                   bp(LANES, 2 * seq, BF16), bp(seq, LANES, BF16), bp(seq // CHUNK, LANES, F32),
                   jax.ShapeDtypeStruct((m, d), F32)),
        grid=(batch, pairs, nr),
        in_specs=[col(0), col(1), col(2), flat, flat, prow, prow, prow],
        out_specs=(pblk(rows, LANES), pblk(rows, LANES), pblk(rows, LANES),
                   pl.BlockSpec((None, None, 2, rows, 4 * CHUNK), lambda b, p, i: (b, p, 0, i, 0)),
                   pl.BlockSpec((None, None, LANES, 2 * rows), lambda b, p, i: (b, p, 0, i)),
                   pblk(rows, LANES), pblk(ncr, LANES), flat),
        compiler_params=_params("parallel", "parallel", "parallel"),
        name="rwkv_prep",
    )(rkvg, rkvg, rkvg, ld, a, k_k.reshape(1, d), k_a.reshape(1, d), r_k.reshape(1, d))

    group = 4
    gw = group * LANES
    gblk = lambda rws, last: pl.BlockSpec((None, group, rws, last), lambda b, p, i: (b, p, i, 0))
    gflat = pl.BlockSpec((rows, gw), lambda b, p, i: (b * nr + i, p))
    grow = pl.BlockSpec((1, gw), lambda b, p, i: (0, p))
    o = pl.pallas_call(
        functools.partial(_rwkv_scan_kernel, group=group),
        out_shape=jax.ShapeDtypeStruct((m, d), BF16),
        grid=(batch, pairs // group, nr),
        in_specs=[gblk(rows, LANES), gblk(rows, LANES), gblk(rows, LANES),
                  pl.BlockSpec((None, group, 2, rows, 4 * CHUNK), lambda b, p, i: (b, p, 0, i, 0)),
                  pl.BlockSpec((None, group, LANES, 2 * rows), lambda b, p, i: (b, p, 0, i)),
                  gblk(rows, LANES), gblk(ncr, LANES), gflat,
                  pl.BlockSpec((None, rows, gw), lambda b, p, i: (3, b * nr + i, p)),
                  grow, grow],
        out_specs=gflat,
        scratch_shapes=[pltpu.VMEM((group, LANES, LANES), F32)],
        compiler_params=_params("parallel", "parallel", "arbitrary"),
        name="rwkv_scan",
    )(wt_, rt_, u_, aro_, kbt_, vb_, dec_, bonus, rkvg, ln_w.reshape(1, d), ln_b.reshape(1, d))
    return matmul_residual(o, w_out.astype(BF16), x, name="rwkv_out_proj")


def kernel(x, p, positions, norm_g, pe_norm_g, pe_w_gate, pe_w_proj, final_norm_g, a_w_in, a_lam, a_subln_g, a_w_out, b_mu, b_w_rkvg, b_w0, b_w_w1, b_w_w2, b_a0, b_w_a1, b_w_a2, b_k_k, b_k_a, b_r_k, b_ln_w, b_ln_b, b_w_out, c_w_in, c_conv_w, c_A_log, c_dt_bias, c_norm_g, c_w_out):
    batch, seq, d = x.shape
    depth = p.shape[0]
    m = batch * seq
    xf = x.reshape(m, d)
    tables = rope_tables(positions)
    for i in range(depth):
        kind = i % N_MIXERS
        j = i // N_MIXERS
        if kind == 0:
            lam_init = 0.8 - 0.6 * math.exp(-0.3 * i)
            hn = rmsnorm(xf, norm_g[i], BF16)
            xf = diff_attention_layer(xf, hn, tables, a_w_in[j], a_lam[j], a_subln_g[j], a_w_out[j],
                                      batch, seq, lam_init)
        elif kind == 1:
            xf = rwkv7_layer(xf, norm_g[i], b_mu[j], b_w_rkvg[j], b_w0[j], b_w_w1[j], b_w_w2[j], b_a0[j],
                             b_w_a1[j], b_w_a2[j], b_k_k[j], b_k_a[j], b_r_k[j], b_ln_w[j], b_ln_b[j],
                             b_w_out[j], batch, seq)
        else:
            hn = rmsnorm(xf, norm_g[i], BF16)
            xf = gated_deltanet_layer(xf, hn, c_w_in[j], c_conv_w[j], c_A_log[j], c_dt_bias[j], c_norm_g[j],
                                      c_w_out[j], batch, seq)
        hn2 = rmsnorm(xf, pe_norm_g[i], BF16)
        xf = per_layer_embedding(xf, hn2, pe_w_gate[i].astype(BF16), p[i].reshape(m, -1),
                                 pe_w_proj[i].astype(BF16))
    return rmsnorm(xf, final_norm_g, F32).reshape(batch, seq, d)
```

```python
import functools
import math

import jax
import jax.numpy as jnp
from jax import lax
from jax.experimental import pallas as pl
from jax.experimental.pallas import tpu as pltpu

F32 = jnp.float32
BF16 = jnp.bfloat16

N_MIXERS = 3
NORM_EPS = 1e-6
LANES = 128
VMEM_LIMIT = 48 * 1024 * 1024

A_HEAD_DIM = 128
A_V_DIM = 2 * A_HEAD_DIM
ROT_DIM = A_HEAD_DIM // 4
ROPE_THETA = 500000.0
SUBLN_EPS = 1e-5
ATTN_BLOCK = 256

B_HEAD_DIM = 64
B_GN_EPS = 64e-5
LORA_PAD = 128

C_HEAD_DIM = 128
C_CONV_WIDTH = 4
CHUNK = 64
SCAN_ROWS = 512

NN = (((1,), (0,)), ((), ()))
NT = (((1,), (1,)), ((), ()))


def _dot(a, b, dims=NN):
    return lax.dot_general(a, b, dims, preferred_element_type=F32)


def _split2(x):
    hi = x.astype(BF16)
    lo = (x - hi.astype(F32)).astype(BF16)
    return hi, lo


def _split3(x):
    hi = x.astype(BF16)
    r = x - hi.astype(F32)
    mid = r.astype(BF16)
    lo = (r - mid.astype(F32)).astype(BF16)
    return hi, mid, lo


def _dot3(a, b, dims=NN):
    ah, al = _split2(a)
    bh, bl = _split2(b)
    return _dot(ah, bh, dims) + (_dot(ah, bl, dims) + _dot(al, bh, dims))


def _dot_xl(a, b_exact, dims=NN):
    h, m, l = _split3(a)
    return _dot(h, b_exact, dims) + (_dot(m, b_exact, dims) + _dot(l, b_exact, dims))


def _dot_xr(a_exact, b, dims=NN):
    h, m, l = _split3(b)
    return _dot(a_exact, h, dims) + (_dot(a_exact, m, dims) + _dot(a_exact, l, dims))


def _iota(shape, dim):
    return lax.broadcasted_iota(jnp.int32, shape, dim)


def _silu(x):
    return x * jax.nn.sigmoid(x)


def _softplus(x):
    return jnp.maximum(x, 0.0) + jnp.log(1.0 + jnp.exp(-jnp.abs(x)))


def _params(*sem):
    return pltpu.CompilerParams(dimension_semantics=sem, vmem_limit_bytes=VMEM_LIMIT)


def _inv_unit_lower(a, nilpotency):
    n = a.shape[0]
    eye = (_iota((n, n), 0) == _iota((n, n), 1)).astype(F32)
    p = -a
    t = eye + p
    for _ in range(int(math.log2(nilpotency)) - 1):
        p = _dot3(p, p)
        t = t + _dot3(t, p)
    return t


def _rmsnorm_kernel(x_ref, g_ref, o_ref, *, eps):
    x = x_ref[...]
    y = x * lax.rsqrt(jnp.mean(x * x, axis=-1, keepdims=True) + eps)
    o_ref[...] = (y * g_ref[...]).astype(o_ref.dtype)


def rmsnorm(x, g, out_dtype, *, eps=NORM_EPS, tm=512):
    m, d = x.shape
    tm = min(tm, m)
    return pl.pallas_call(
        functools.partial(_rmsnorm_kernel, eps=eps),
        out_shape=jax.ShapeDtypeStruct((m, d), out_dtype),
        grid=(m // tm,),
        in_specs=[pl.BlockSpec((tm, d), lambda i: (i, 0)), pl.BlockSpec((1, d), lambda i: (0, 0))],
        out_specs=pl.BlockSpec((tm, d), lambda i: (i, 0)),
        compiler_params=_params("parallel"),
        name="rmsnorm",
    )(x, g.reshape(1, d))


def _mm_kernel(a_ref, w_ref, *rest, epilogue):
    o_ref = rest[-1]
    acc = _dot(a_ref[...], w_ref[...])
    if epilogue is not None:
        acc = epilogue(acc, *rest[:-1])
    o_ref[...] = acc.astype(o_ref.dtype)


def matmul(a, w, out_dtype, *, tm=1024, tn=1024, extra=(), extra_specs=(), epilogue=None, name="matmul"):
    m, k = a.shape
    n = w.shape[1]
    tm, tn = min(tm, m), min(tn, n)
    return pl.pallas_call(
        functools.partial(_mm_kernel, epilogue=epilogue),
        out_shape=jax.ShapeDtypeStruct((m, n), out_dtype),
        grid=(n // tn, m // tm),
        in_specs=[pl.BlockSpec((tm, k), lambda j, i: (i, 0)),
                  pl.BlockSpec((k, tn), lambda j, i: (0, j))] + list(extra_specs),
        out_specs=pl.BlockSpec((tm, tn), lambda j, i: (i, j)),
        compiler_params=_params("parallel", "parallel"),
        name=name,
    )(a, w, *extra)


def matmul_residual(a, w, res, *, tm=1024, tn=1024, name="matmul_residual"):
    tm, tn = min(tm, a.shape[0]), min(tn, w.shape[1])
    return matmul(a, w, F32, tm=tm, tn=tn, extra=(res,),
                  extra_specs=(pl.BlockSpec((tm, tn), lambda j, i: (i, j)),),
                  epilogue=lambda acc, r_ref: r_ref[...] + acc, name=name)


def _ple_kernel(h_ref, wg_ref, p_ref, wp_ref, x_ref, o_ref):
    gate = jax.nn.sigmoid(_dot(h_ref[...], wg_ref[...]))
    proj = _dot(p_ref[...].astype(BF16), wp_ref[...])
    o_ref[...] = x_ref[...] + gate * proj


def per_layer_embedding(x, hn, w_gate, p, w_proj, *, tm=1024, tn=1024):
    m, d = x.shape
    pd = p.shape[1]
    tm, tn = min(tm, m), min(tn, d)
    return pl.pallas_call(
        _ple_kernel,
        out_shape=jax.ShapeDtypeStruct((m, d), F32),
        grid=(d // tn, m // tm),
        in_specs=[pl.BlockSpec((tm, d), lambda j, i: (i, 0)),
                  pl.BlockSpec((d, tn), lambda j, i: (0, j)),
                  pl.BlockSpec((tm, pd), lambda j, i: (i, 0)),
                  pl.BlockSpec((pd, tn), lambda j, i: (0, j)),
                  pl.BlockSpec((tm, tn), lambda j, i: (i, j))],
        out_specs=pl.BlockSpec((tm, tn), lambda j, i: (i, j)),
        compiler_params=_params("parallel", "parallel"),
        name="per_layer_embedding",
    )(hn, w_gate, p, w_proj, x)


def _rope_table_kernel(pos_ref, freq_ref, cos_ref, sin_lo_ref, sin_hi_ref):
    half = ROT_DIM // 2
    ang = pos_ref[...].astype(F32) * freq_ref[...]
    lane = _iota(ang.shape, 1)
    c, s = jnp.cos(ang), jnp.sin(ang)
    cos_ref[...] = jnp.where(lane < ROT_DIM, c, 1.0)
    sin_lo_ref[...] = jnp.where(lane < half, -s, 0.0)
    sin_hi_ref[...] = jnp.where((lane >= half) & (lane < ROT_DIM), s, 0.0)


def rope_tables(positions, *, tm=1024):
    m = positions.size
    tm = min(tm, m)
    inv_freq = ROPE_THETA ** (-jnp.arange(0, ROT_DIM, 2, dtype=F32) / ROT_DIM)
    freq_row = jnp.concatenate([inv_freq, inv_freq, jnp.zeros((LANES - ROT_DIM,), F32)]).reshape(1, LANES)
    shape = jax.ShapeDtypeStruct((m, LANES), F32)
    spec = pl.BlockSpec((tm, LANES), lambda i: (i, 0))
    return pl.pallas_call(
        _rope_table_kernel,
        out_shape=(shape, shape, shape),
        grid=(m // tm,),
        in_specs=[pl.BlockSpec((tm, 1), lambda i: (i, 0)), pl.BlockSpec((1, LANES), lambda i: (0, 0))],
        out_specs=(spec, spec, spec),
        compiler_params=_params("parallel"),
        name="rope_tables",
    )(positions.reshape(m, 1), freq_row)


def _attn_in_kernel(a_ref, w_ref, cos_ref, sin_lo_ref, sin_hi_ref, o_ref, *, n_q_blocks):
    j = pl.program_id(0)
    acc = _dot(a_ref[...], w_ref[...])
    half = ROT_DIM // 2

    @pl.when(j < 2 * n_q_blocks)
    def _():
        scale = jnp.where(j < n_q_blocks, A_HEAD_DIM ** -0.5, 1.0).astype(F32)
        cos, sin_lo, sin_hi = cos_ref[...], sin_lo_ref[...], sin_hi_ref[...]
        for g in range(acc.shape[1] // LANES):
            x = acc[:, g * LANES:(g + 1) * LANES]
            y = x * cos + pltpu.roll(x, LANES - half, 1) * sin_lo + pltpu.roll(x, half, 1) * sin_hi
            o_ref[:, g * LANES:(g + 1) * LANES] = (y * scale).astype(o_ref.dtype)

    @pl.when(j >= 2 * n_q_blocks)
    def _():
        o_ref[...] = acc.astype(o_ref.dtype)


def attn_in_proj(hn, w_in, tables, qk_width, *, tm=1024, tn=1024):
    m, k = hn.shape
    n = w_in.shape[1]
    tm, tn = min(tm, m), min(tn, qk_width)
    tspec = pl.BlockSpec((tm, LANES), lambda j, i: (i, 0))
    return pl.pallas_call(
        functools.partial(_attn_in_kernel, n_q_blocks=qk_width // tn),
        out_shape=jax.ShapeDtypeStruct((m, n), BF16),
        grid=(n // tn, m // tm),
        in_specs=[pl.BlockSpec((tm, k), lambda j, i: (i, 0)),
                  pl.BlockSpec((k, tn), lambda j, i: (0, j)), tspec, tspec, tspec],
        out_specs=pl.BlockSpec((tm, tn), lambda j, i: (i, j)),
        compiler_params=_params("parallel", "parallel"),
        name="attn_in_proj",
    )(hn, w_in, *tables)


def _diff_attn_kernel(lam_ref, q_ref, k_ref, v_ref, z_ref, g_ref, o_ref, m_ref, l_ref, acc_ref, *, blk, lam_init):
    i = pl.program_id(2)
    lam = lam_ref[...]
    lam_full = (jnp.exp(jnp.sum(lam[0:1] * lam[1:2], axis=-1, keepdims=True))
                - jnp.exp(jnp.sum(lam[2:3] * lam[3:4], axis=-1, keepdims=True)) + lam_init)
    m_ref[...] = jnp.full(m_ref.shape, -jnp.inf, F32)
    l_ref[...] = jnp.zeros(l_ref.shape, F32)
    acc_ref[...] = jnp.zeros(acc_ref.shape, F32)
    q = q_ref[...]
    d = A_HEAD_DIM

    def step(j, masked):
        start = pl.multiple_of(j * blk, blk)
        kb = k_ref[pl.ds(start, blk), :]
        vb = v_ref[pl.ds(start, blk), :]
        for c in range(2):
            s = _dot(q[:, c * d:(c + 1) * d], kb[:, c * d:(c + 1) * d], NT)
            if masked:
                s = jnp.where(_iota(s.shape, 0) >= _iota(s.shape, 1), s, -jnp.inf)
            m_prev = m_ref[c]
            m_new = jnp.maximum(m_prev, jnp.max(s, axis=-1, keepdims=True))
            alpha = jnp.exp(m_prev - m_new)
            p = jnp.exp(s - m_new)
            l_ref[c] = alpha * l_ref[c] + jnp.sum(p, axis=-1, keepdims=True)
            acc_ref[c] = alpha * acc_ref[c] + _dot(p.astype(BF16), vb)
            m_ref[c] = m_new

    def body(j, carry):
        step(j, False)
        return carry

    lax.fori_loop(0, i, body, 0)
    step(i, True)
    o = acc_ref[0] / l_ref[0] - lam_full * (acc_ref[1] / l_ref[1])
    o = o * lax.rsqrt(jnp.mean(o * o, axis=-1, keepdims=True) + SUBLN_EPS) * g_ref[...]
    o = o * (1.0 - lam_init)
    o_ref[...] = (o * _silu(z_ref[...].astype(F32))).astype(o_ref.dtype)


def diff_attention_core(qkvz, lam, subln_g, batch, seq, heads, lam_init):
    m = qkvz.shape[0]
    blk = min(ATTN_BLOCK, seq)
    nq = seq // blk
    w = A_V_DIM
    return pl.pallas_call(
        functools.partial(_diff_attn_kernel, blk=blk, lam_init=lam_init),
        out_shape=jax.ShapeDtypeStruct((m, heads * w), BF16),
        grid=(batch, heads, nq),
        in_specs=[pl.BlockSpec((4, A_HEAD_DIM), lambda b, h, i: (0, 0)),
                  pl.BlockSpec((blk, w), lambda b, h, i: (b * nq + i, h)),
                  pl.BlockSpec((seq, w), lambda b, h, i: (b, heads + h)),
                  pl.BlockSpec((seq, w), lambda b, h, i: (b, 2 * heads + h)),
                  pl.BlockSpec((blk, w), lambda b, h, i: (b * nq + i, 3 * heads + h)),
                  pl.BlockSpec((1, w), lambda b, h, i: (0, 0))],
        out_specs=pl.BlockSpec((blk, w), lambda b, h, i: (b * nq + i, h)),
        scratch_shapes=[pltpu.VMEM((2, blk, 1), F32), pltpu.VMEM((2, blk, 1), F32),
                        pltpu.VMEM((2, blk, w), F32)],
        compiler_params=_params("parallel", "parallel", "parallel"),
        name="diff_attention",
    )(lam, qkvz, qkvz, qkvz, qkvz, subln_g.reshape(1, w))


def diff_attention_layer(x, hn, tables, w_in, lam, subln_g, w_out, batch, seq, lam_init):
    d = x.shape[1]
    heads = d // A_V_DIM
    qkvz = attn_in_proj(hn, w_in.astype(BF16), tables, heads * 2 * A_HEAD_DIM)
    o = diff_attention_core(qkvz, lam, subln_g, batch, seq, heads, lam_init)
    return matmul_residual(o, w_out.astype(BF16), x, name="attn_out_proj")


def _gdn_in_kernel(a_ref, w_ref, cw_ref, o_ref, tail_ref, *, n_conv_blocks, rows_per_seq):
    j = pl.program_id(0)
    i = pl.program_id(1)
    acc = _dot(a_ref[...], w_ref[...])
    tm = acc.shape[0]

    @pl.when(j < n_conv_blocks)
    def _():
        @pl.when((i * tm) % rows_per_seq == 0)
        def _():
            tail_ref[...] = jnp.zeros(tail_ref.shape, F32)

        tail = tail_ref[...]
        sub = _iota(tail.shape, 0)

        def shifted(s):
            xs = pltpu.roll(acc, s, 0)
            head = jnp.where(sub < s, pltpu.roll(tail, s, 0), xs[:8])
            return jnp.concatenate([head, xs[8:]], axis=0)

        cw = cw_ref[...]
        last = C_CONV_WIDTH - 1
        y = shifted(last) * cw[0:1]
        for t in range(1, last):
            y = y + shifted(last - t) * cw[t:t + 1]
        y = y + acc * cw[last:last + 1]
        tail_ref[...] = acc[tm - 8:]
        o_ref[...] = _silu(y).astype(o_ref.dtype)

    @pl.when(j >= n_conv_blocks)
    def _():
        o_ref[...] = acc.astype(o_ref.dtype)


def gdn_in_proj(hn, w, conv_w, seq, *, tm=1024, tn=1024):
    m, k = hn.shape
    n = w.shape[1]
    conv_ch = conv_w.shape[1]
    tm, tn = min(tm, seq), min(tn, n)
    ncb = conv_ch // tn
    return pl.pallas_call(
        functools.partial(_gdn_in_kernel, n_conv_blocks=ncb, rows_per_seq=seq),
        out_shape=jax.ShapeDtypeStruct((m, n), BF16),
        grid=(n // tn, m // tm),
        in_specs=[pl.BlockSpec((tm, k), lambda j, i: (i, 0)),
                  pl.BlockSpec((k, tn), lambda j, i: (0, j)),
                  pl.BlockSpec((C_CONV_WIDTH, tn), lambda j, i: (0, jnp.minimum(j, ncb - 1)))],
        out_specs=pl.BlockSpec((tm, tn), lambda j, i: (i, j)),
        scratch_shapes=[pltpu.VMEM((8, tn), F32)],
        compiler_params=_params("arbitrary", "arbitrary"),
        name="gdn_in_proj",
    )(hn, w, conv_w)


def _pair_masks():
    n = 2 * CHUNK
    r, c = _iota((n, n), 0), _iota((n, n), 1)
    same = (r // CHUNK) == (c // CHUNK)
    return same & (r >= c), same & (r > c), r == c


def _dot_x2l(a, b_exact, dims=NN):
    h, l = _split2(a)
    return _dot(h, b_exact, dims) + _dot(l, b_exact, dims)


def _dot_x2r(a_exact, b, dims=NN):
    h, l = _split2(b)
    return _dot(a_exact, h, dims) + _dot(a_exact, l, dims)


def _inverse_stages(p_ref, t_ref, count, nilpotency):
    for _ in range(int(math.log2(nilpotency)) - 1):
        for i in range(count):
            p_ref[i] = _dot(p_ref[i], p_ref[i]).astype(BF16)
        for i in range(count):
            t = t_ref[i]
            t_ref[i] = t + _dot(t.astype(BF16), p_ref[i])


def _gdn_prep_kernel(q_ref, k_ref, v_ref, b_ref, a_ref, alog_ref, dtb_ref,
                     w_ref, qg_ref, u_ref, att_ref, kdt_ref, dec_ref,
                     g_s, dec_s, kn_s, kb_s, rhs_s, p_s, t_s):
    h = pl.program_id(1)
    c = CHUNK
    n2 = 2 * c
    dk = C_HEAD_DIM
    npairs = q_ref.shape[0] // n2
    tri, strict, diag = _pair_masks()
    tri_b = tri.astype(BF16)
    eye_b = diag.astype(BF16)
    eye_f = diag.astype(F32)
    sel = (_iota((LANES, dk), 0) == h).astype(BF16)
    first = (_iota((n2, dk), 1) == 0).astype(BF16)
    upper_half = _iota((n2, dk), 0) < c
    g_all = -jnp.exp(alog_ref[...]) * _softplus(a_ref[...] + dtb_ref[...])
    pairs = [slice(n * n2, (n + 1) * n2) for n in range(npairs)]

    for n, rows in enumerate(pairs):
        g_s[n] = _dot_x2l(g_all[rows], sel)
    for n, rows in enumerate(pairs):
        g_s[n] = _dot_x2r(tri_b, g_s[n])
    for n, rows in enumerate(pairs):
        gc = g_s[n]
        gc_row = _dot_x2r(first, gc, NT)
        dec_s[n] = jnp.where(tri, jnp.exp(jnp.where(tri, gc - gc_row, 0.0)), 0.0)
    for n, rows in enumerate(pairs):
        gc = g_s[n]
        qf = q_ref[rows, :].astype(F32)
        kf = k_ref[rows, :].astype(F32)
        vf = v_ref[rows, :].astype(F32)
        qn = qf * lax.rsqrt(jnp.sum(qf * qf, axis=-1, keepdims=True) + 1e-6) * (dk ** -0.5)
        kn = kf * lax.rsqrt(jnp.sum(kf * kf, axis=-1, keepdims=True) + 1e-6)
        beta = _dot(jax.nn.sigmoid(b_ref[rows, :]).astype(BF16), sel)
        egc = jnp.exp(gc)
        kb = kn * beta
        kn_b = kn.astype(BF16)
        qn_b = qn.astype(BF16)
        kn_s[n] = kn_b
        kb_s[n] = kb.astype(BF16)
        rhs_s[n] = jnp.concatenate([vf * beta, kb * egc], axis=1).astype(BF16)
        g_last = jnp.where(upper_half, gc[c - 1:c, :], gc[n2 - 1:n2, :])
        kdec = kn * jnp.exp(g_last - gc)
        qg_ref[rows, :] = (qn * egc).astype(BF16)
        att_ref[rows, :] = (_dot(qn_b, kn_b, NT) * dec_s[n]).astype(BF16)
        kdt_ref[:, rows] = _dot(eye_b, kdec.astype(BF16), NT).astype(BF16)
        dec_ref[2 * n:2 * n + 1, :] = jnp.exp(gc[c - 1:c, :])
        dec_ref[2 * n + 1:2 * n + 2, :] = jnp.exp(gc[n2 - 1:n2, :])
    for n, rows in enumerate(pairs):
        neg_a = jnp.where(strict, -(_dot(kb_s[n], kn_s[n], NT) * dec_s[n]), 0.0)
        p_s[n] = neg_a.astype(BF16)
        t_s[n] = eye_f + neg_a
    _inverse_stages(p_s, t_s, npairs, c)
    for n, rows in enumerate(pairs):
        sol = _dot(t_s[n].astype(BF16), rhs_s[n])
        u_ref[rows, :] = sol[:, :dk]
        w_ref[rows, :] = sol[:, dk:].astype(BF16)


def _gdn_scan_kernel(w_ref, qg_ref, u_ref, att_ref, kdt_ref, dec_ref, z_ref, g_ref, o_ref, s_ref, *, group):
    c = CHUNK
    dk = C_HEAD_DIM
    zeros = jnp.zeros((c, dk), BF16)

    @pl.when(pl.program_id(2) == 0)
    def _():
        s_ref[...] = jnp.zeros(s_ref.shape, F32)

    for n in range(w_ref.shape[1] // c):
        rows = slice(n * c, (n + 1) * c)
        pair = slice((n // 2) * 2 * c, (n // 2 + 1) * 2 * c)
        for gi in range(group):
            s = s_ref[gi]
            lhs = jnp.concatenate([w_ref[gi, rows, :], qg_ref[gi, rows, :]], axis=0)
            ms = _dot(lhs, s.astype(BF16))
            v_new = (u_ref[gi, rows, :] - ms[:c]).astype(BF16)
            v_pad = jnp.concatenate([v_new, zeros] if n % 2 == 0 else [zeros, v_new], axis=0)
            o = ms[c:] + _dot(att_ref[gi, rows, :], v_pad)
            s_ref[gi] = s * dec_ref[gi, n:n + 1, :] + _dot(kdt_ref[gi, :, pair], v_pad)
            o = o * lax.rsqrt(jnp.mean(o * o, axis=-1, keepdims=True) + NORM_EPS) * g_ref[...]
            z = z_ref[rows, gi * dk:(gi + 1) * dk].astype(F32)
            o_ref[rows, gi * dk:(gi + 1) * dk] = (o * _silu(z)).astype(o_ref.dtype)


def gated_deltanet_layer(x, hn, w_in, conv_w, a_log, dt_bias, norm_g, w_out, batch, seq):
    m, d = x.shape
    dk = C_HEAD_DIM
    k_heads = d // dk
    v_heads = 2 * k_heads
    conv_ch = 2 * k_heads * dk + v_heads * dk
    main_w = conv_ch + v_heads * dk
    qkvz = gdn_in_proj(hn, w_in[:, :main_w].astype(BF16), conv_w, seq)
    pad = jnp.zeros((d, LANES - v_heads), F32)
    w_ba = jnp.concatenate([w_in[:, main_w:main_w + v_heads], pad, w_in[:, main_w + v_heads:], pad], axis=1)
    ba = matmul(hn, w_ba.astype(BF16), F32, tn=2 * LANES, name="gdn_ba_proj")
    row_pad = jnp.zeros((LANES - v_heads,), F32)
    alog_row = jnp.concatenate([a_log, row_pad]).reshape(1, LANES)
    dtb_row = jnp.concatenate([dt_bias, row_pad]).reshape(1, LANES)

    rows = min(SCAN_ROWS, seq)
    nr = seq // rows
    ncr = rows // CHUNK
    npair = rows // (2 * CHUNK)
    hv = v_heads
    bh_t = lambda dt, last: jax.ShapeDtypeStruct((batch, hv, seq, last), dt)
    blk4 = lambda last: pl.BlockSpec((None, None, rows, last), lambda b, h, i: (b, h, i, 0))
    qoff, koff, voff = 0, k_heads, 2 * k_heads
    w_, qg_, u_, att_, kdt_, dec_ = pl.pallas_call(
        _gdn_prep_kernel,
        out_shape=(bh_t(BF16, dk), bh_t(BF16, dk), bh_t(F32, dk), bh_t(BF16, 2 * CHUNK),
                   jax.ShapeDtypeStruct((batch, hv, dk, seq), BF16),
                   jax.ShapeDtypeStruct((batch, hv, seq // CHUNK, dk), F32)),
        grid=(batch, hv, nr),
        in_specs=[pl.BlockSpec((rows, dk), lambda b, h, i: (b * nr + i, qoff + h // 2)),
                  pl.BlockSpec((rows, dk), lambda b, h, i: (b * nr + i, koff + h // 2)),
                  pl.BlockSpec((rows, dk), lambda b, h, i: (b * nr + i, voff + h)),
                  pl.BlockSpec((rows, LANES), lambda b, h, i: (b * nr + i, 0)),
                  pl.BlockSpec((rows, LANES), lambda b, h, i: (b * nr + i, 1)),
                  pl.BlockSpec((1, LANES), lambda b, h, i: (0, 0)),
                  pl.BlockSpec((1, LANES), lambda b, h, i: (0, 0))],
        out_specs=(blk4(dk), blk4(dk), blk4(dk), blk4(2 * CHUNK),
                   pl.BlockSpec((None, None, dk, rows), lambda b, h, i: (b, h, 0, i)),
                   pl.BlockSpec((None, None, ncr, dk), lambda b, h, i: (b, h, i, 0))),
        scratch_shapes=[pltpu.VMEM((npair, dk, dk), F32), pltpu.VMEM((npair, dk, dk), F32),
                        pltpu.VMEM((npair, dk, dk), BF16), pltpu.VMEM((npair, dk, dk), BF16),
                        pltpu.VMEM((npair, dk, 2 * dk), BF16), pltpu.VMEM((npair, dk, dk), BF16),
                        pltpu.VMEM((npair, dk, dk), F32)],
        compiler_params=_params("parallel", "parallel", "parallel"),
        name="gdn_prep",
    )(qkvz, qkvz, qkvz, ba, ba, alog_row, dtb_row)

    group = 8
    zoff = conv_ch // (group * dk)
    gblk = lambda last: pl.BlockSpec((None, group, rows, last), lambda b, h, i: (b, h, i, 0))
    o = pl.pallas_call(
        functools.partial(_gdn_scan_kernel, group=group),
        out_shape=jax.ShapeDtypeStruct((m, hv * dk), BF16),
        grid=(batch, hv // group, nr),
        in_specs=[gblk(dk), gblk(dk), gblk(dk), gblk(2 * CHUNK),
                  pl.BlockSpec((None, group, dk, rows), lambda b, h, i: (b, h, 0, i)),
                  pl.BlockSpec((None, group, ncr, dk), lambda b, h, i: (b, h, i, 0)),
                  pl.BlockSpec((rows, group * dk), lambda b, h, i: (b * nr + i, zoff + h)),
                  pl.BlockSpec((1, dk), lambda b, h, i: (0, 0))],
        out_specs=pl.BlockSpec((rows, group * dk), lambda b, h, i: (b * nr + i, h)),
        scratch_shapes=[pltpu.VMEM((group, dk, dk), F32)],
        compiler_params=_params("parallel", "parallel", "arbitrary"),
        name="gdn_scan",
    )(w_, qg_, u_, att_, kdt_, dec_, qkvz, norm_g.reshape(1, dk))
    return matmul_residual(o, w_out.astype(BF16), x, tm=512, name="gdn_out_proj")


def _rwkv_mix_kernel(x_ref, g_ref, mu_ref, o_ref, tail_ref, *, rows_per_seq):
    i = pl.program_id(0)
    x = x_ref[...]
    tm = x.shape[0]
    hn = x * lax.rsqrt(jnp.mean(x * x, axis=-1, keepdims=True) + NORM_EPS) * g_ref[...]

    @pl.when((i * tm) % rows_per_seq == 0)
    def _():
        tail_ref[...] = jnp.zeros(tail_ref.shape, F32)

    prev = jnp.where(_iota(hn.shape, 0) == 0, tail_ref[7:8, :], pltpu.roll(hn, 1, 0))
    tail_ref[...] = hn[tm - 8:]
    xx = prev - hn
    for c in range(o_ref.shape[0]):
        o_ref[c] = (hn + xx * mu_ref[c:c + 1, :]).astype(o_ref.dtype)


def rwkv_token_mix(x, norm_g, mu, seq, *, tm=256):
    m, d = x.shape
    tm = min(tm, seq)
    nmix = mu.shape[0]
    return pl.pallas_call(
        functools.partial(_rwkv_mix_kernel, rows_per_seq=seq),
        out_shape=jax.ShapeDtypeStruct((nmix, m, d), BF16),
        grid=(m // tm,),
        in_specs=[pl.BlockSpec((tm, d), lambda i: (i, 0)),
                  pl.BlockSpec((1, d), lambda i: (0, 0)),
                  pl.BlockSpec((nmix, d), lambda i: (0, 0))],
        out_specs=pl.BlockSpec((nmix, tm, d), lambda i: (0, i, 0)),
        scratch_shapes=[pltpu.VMEM((8, d), F32)],
        compiler_params=_params("arbitrary"),
        name="rwkv_token_mix",
    )(x, norm_g.reshape(1, d), mu)


def _grouped_mm_kernel(a_ref, w_ref, o_ref):
    o_ref[...] = _dot(a_ref[...], w_ref[...]).astype(o_ref.dtype)


def grouped_matmul(a, w, out_dtype, *, tm=1024, tn=1024):
    g, k, n = w.shape
    m = a.shape[1]
    tm, tn = min(tm, m), min(tn, n)
    return pl.pallas_call(
        _grouped_mm_kernel,
        out_shape=jax.ShapeDtypeStruct((g, m, n), out_dtype),
        grid=(g, n // tn, m // tm),
        in_specs=[pl.BlockSpec((None, tm, k), lambda c, j, i: (c, i, 0)),
                  pl.BlockSpec((None, k, tn), lambda c, j, i: (c, 0, j))],
        out_specs=pl.BlockSpec((None, tm, tn), lambda c, j, i: (c, i, j)),
        compiler_params=_params("parallel", "parallel", "parallel"),
        name="rwkv_rkvg_proj",
    )(a, w)


def _rwkv_lora_kernel(xw_ref, xa_ref, w1_ref, w2_ref, a1_ref, a2_ref, w0_ref, a0_ref, ld_ref, a_ref):
    hw = jnp.tanh(_dot(xw_ref[...], w1_ref[...])).astype(BF16)
    lw = w0_ref[...] + _dot(hw, w2_ref[...])
    log_w = -_softplus(-lw) - 0.5
    ld_ref[...] = -jnp.exp(log_w)
    ha = _dot(xa_ref[...], a1_ref[...]).astype(BF16)
    a_ref[...] = jax.nn.sigmoid(a0_ref[...] + _dot(ha, a2_ref[...]))


def rwkv_lora(xs, w1, w2, a1, a2, w0, a0, *, tm=512):
    _, m, d = xs.shape
    tm = min(tm, m)
    r = w1.shape[1]
    row = pl.BlockSpec((1, d), lambda i: (0, 0))
    shape = jax.ShapeDtypeStruct((m, d), F32)
    return pl.pallas_call(
        _rwkv_lora_kernel,
        out_shape=(shape, shape),
        grid=(m // tm,),
        in_specs=[pl.BlockSpec((None, tm, d), lambda i: (4, i, 0)),
                  pl.BlockSpec((None, tm, d), lambda i: (5, i, 0)),
                  pl.BlockSpec((d, r), lambda i: (0, 0)), pl.BlockSpec((r, d), lambda i: (0, 0)),
                  pl.BlockSpec((d, r), lambda i: (0, 0)), pl.BlockSpec((r, d), lambda i: (0, 0)),
                  row, row],
        out_specs=(pl.BlockSpec((tm, d), lambda i: (i, 0)), pl.BlockSpec((tm, d), lambda i: (i, 0))),
        compiler_params=_params("parallel"),
        name="rwkv_lora",
    )(xs, xs, w1, w2, a1, a2, w0.reshape(1, d), a0.reshape(1, d))


def _rwkv_prep_kernel(r_ref, k_ref, v_ref, ld_ref, a_ref, kk_ref, ka_ref, rk_ref,
                      wt_ref, rt_ref, u_ref, aro_ref, kbt_ref, vb_ref, dec_ref, bonus_ref,
                      kkt_s, kbar_s, bbar_s, akk_s, p_s, t_s):
    c = CHUNK
    n2 = 2 * c
    n_h = B_HEAD_DIM
    npairs = r_ref.shape[0] // n2
    lane_head = _iota((1, LANES), 1) // n_h
    seg = (_iota((LANES, LANES), 0) // n_h == _iota((LANES, LANES), 1) // n_h).astype(BF16)
    tri, strict, diag = _pair_masks()
    tri_b = tri.astype(BF16)
    eye_b = diag.astype(BF16)
    eye_f = diag.astype(F32)
    upper_half = _iota((n2, LANES), 0) < c
    pairs = [slice(n * n2, (n + 1) * n2) for n in range(npairs)]

    r = r_ref[...]
    k = k_ref[...]
    v = v_ref[...]
    a = a_ref[...]
    kk_raw = k * kk_ref[...]
    kk = kk_raw * lax.rsqrt(_dot_xl(kk_raw * kk_raw, seg) + 1e-6)
    k2 = k * (1.0 + (a - 1.0) * ka_ref[...])
    bb = kk * a
    bonus_ref[...] = _dot_xl(r * k2 * rk_ref[...], seg) * v
    vb_ref[...] = v.astype(BF16)
    for n, rows in enumerate(pairs):
        ld = ld_ref[rows, :]
        cs = _dot_xr(tri_b, ld)
        cs_last = jnp.where(upper_half, cs[c - 1:c, :], cs[n2 - 1:n2, :])
        w_inv = jnp.exp(-cs)
        w_end = jnp.exp(cs_last - cs)
        rt_ref[rows, :] = (r[rows] * jnp.exp(cs)).astype(BF16)
        kkt_s[n] = (kk[rows] * jnp.exp(cs - ld)).astype(BF16)
        k2c, bbc = k2[rows], bb[rows]
        kbar_s[n] = (k2c * w_inv).astype(BF16)
        bbar_s[n] = (bbc * w_inv).astype(BF16)
        kw, bw = (k2c * w_end).astype(BF16), (-(bbc * w_end)).astype(BF16)
        for half in range(2):
            hs = slice(half * c, (half + 1) * c)
            kb_end = jnp.concatenate([kw[hs], bw[hs]], axis=0)
            cols = slice((2 * n + half) * n2, (2 * n + half + 1) * n2)
            kbt_ref[:, cols] = _dot(eye_b, kb_end, NT).astype(BF16)
        dec_ref[2 * n:2 * n + 1, :] = jnp.exp(cs[c - 1:c, :])
        dec_ref[2 * n + 1:2 * n + 2, :] = jnp.exp(cs[n2 - 1:n2, :])
    zero_b = jnp.zeros((n2, LANES), BF16)
    for n, rows in enumerate(pairs):
        for hh in range(2):
            own = lane_head == hh
            i = 2 * n + hh
            kkt_m = jnp.where(own, kkt_s[n], zero_b)
            rt_m = jnp.where(own, rt_ref[rows, :], zero_b)
            neg_a = jnp.where(strict, -_dot(kkt_m, bbar_s[n], NT), 0.0)
            p_s[i] = neg_a.astype(BF16)
            t_s[i] = eye_f + neg_a
            akk_s[i] = jnp.where(strict, _dot(kkt_m, kbar_s[n], NT), 0.0).astype(BF16)
            a_rk = jnp.where(tri, _dot(rt_m, kbar_s[n], NT), 0.0)
            a_rb = jnp.where(tri, _dot(rt_m, bbar_s[n], NT), 0.0)
            aro_ref[hh, rows, :] = jnp.concatenate([a_rk, -a_rb], axis=1).astype(BF16)
    _inverse_stages(p_s, t_s, 2 * npairs, c)
    for n, rows in enumerate(pairs):
        for hh in range(2):
            i = 2 * n + hh
            p_s[i] = _dot(akk_s[i], vb_ref[rows, :]).astype(BF16)
    for n, rows in enumerate(pairs):
        t0, t1 = t_s[2 * n].astype(BF16), t_s[2 * n + 1].astype(BF16)
        own0 = lane_head == 0
        wt_ref[rows, :] = jnp.where(own0, _dot(t0, kkt_s[n]), _dot(t1, kkt_s[n])).astype(BF16)
        u_ref[rows, :] = jnp.where(own0, _dot(t0, p_s[2 * n]), _dot(t1, p_s[2 * n + 1]))


def _rwkv_scan_kernel(wt_ref, rt_ref, u_ref, aro_ref, kbt_ref, vb_ref, dec_ref, bonus_ref, gate_ref,
                      lnw_ref, lnb_ref, o_ref, s_ref, *, group):
    c = CHUNK
    n2 = 2 * c
    n_h = B_HEAD_DIM
    lane_head = _iota((1, LANES), 1) // n_h
    r_i, c_i = _iota((LANES, LANES), 0), _iota((LANES, LANES), 1)
    same_head = r_i // n_h == c_i // n_h
    avg = jnp.where(same_head, 1.0 / n_h, 0.0).astype(BF16)
    eye = (r_i == c_i).astype(BF16)
    zeros = jnp.zeros((c, LANES), BF16)

    @pl.when(pl.program_id(2) == 0)
    def _():
        s_ref[...] = jnp.zeros(s_ref.shape, F32)

    for n in range(u_ref.shape[1] // c):
        rows = slice(n * c, (n + 1) * c)
        for gi in range(group):
            s = s_ref[gi]
            lhs = jnp.concatenate([wt_ref[gi, rows, :], rt_ref[gi, rows, :]], axis=0)
            ms = _dot(lhs, s.astype(BF16))
            sa = (u_ref[gi, rows, :] + ms[:c]).astype(BF16)
            vb = vb_ref[gi, rows, :]
            x_pad = jnp.concatenate([vb, zeros, sa, zeros] if n % 2 == 0 else [zeros, vb, zeros, sa], axis=0)
            o0 = _dot(aro_ref[gi, 0, rows, :], x_pad)
            o1 = _dot(aro_ref[gi, 1, rows, :], x_pad)
            o = ms[c:] + jnp.where(lane_head == 0, o0, o1)
            upd = _dot(kbt_ref[gi, :, n * n2:(n + 1) * n2], jnp.concatenate([vb, sa], axis=0))
            dec_col = _dot_xr(eye, jnp.broadcast_to(dec_ref[gi, n:n + 1, :], (LANES, LANES)), NT)
            s_ref[gi] = s * dec_col + jnp.where(same_head, upd, 0.0)
            mean = _dot_xl(o, avg)
            dlt = o - mean
            var = _dot_xl(dlt * dlt, avg)
            cols = slice(gi * LANES, (gi + 1) * LANES)
            y = dlt * lax.rsqrt(var + B_GN_EPS) * lnw_ref[:, cols] + lnb_ref[:, cols]
            y = y + bonus_ref[rows, cols]
            o_ref[rows, cols] = (y * _silu(gate_ref[rows, cols])).astype(o_ref.dtype)


def rwkv7_layer(x, norm_g, mu, w_rkvg, w0, w_w1, w_w2, a0, w_a1, w_a2, k_k, k_a, r_k, ln_w, ln_b, w_out,
                batch, seq):
    m, d = x.shape
    heads = d // B_HEAD_DIM
    pairs = heads // 2
    order = jnp.array([0, 2, 3, 5, 1, 4])
    xs = rwkv_token_mix(x, norm_g, mu[order], seq)
    rkvg = grouped_matmul(xs, w_rkvg.astype(BF16), F32)
    lora = w_w1.shape[1]
    padc = lambda w: jnp.pad(w, ((0, 0), (0, LORA_PAD - lora))).astype(BF16)
    padr = lambda w: jnp.pad(w, ((0, LORA_PAD - lora), (0, 0))).astype(BF16)
    ld, a = rwkv_lora(xs, padc(w_w1), padr(w_w2), padc(w_a1), padr(w_a2), w0, a0)

    rows = min(SCAN_ROWS, seq)
    nr = seq // rows
    ncr = rows // CHUNK
    npair = rows // (2 * CHUNK)
    col = lambda g: pl.BlockSpec((None, rows, LANES), lambda b, p, i, g=g: (g, b * nr + i, p))
    flat = pl.BlockSpec((rows, LANES), lambda b, p, i: (b * nr + i, p))
    prow = pl.BlockSpec((1, LANES), lambda b, p, i: (0, p))
    bp = lambda rws, last, dt: jax.ShapeDtypeStruct((batch, pairs, rws, last), dt)
    pblk = lambda rws, last: pl.BlockSpec((None, None, rws, last), lambda b, p, i: (b, p, i, 0))
    wt_, rt_, u_, aro_, kbt_, vb_, dec_, bonus = pl.pallas_call(
        _rwkv_prep_kernel,
        out_shape=(bp(seq, LANES, BF16), bp(seq, LANES, BF16), bp(seq, LANES, F32),
                   jax.ShapeDtypeStruct((batch, pairs, 2, seq, 4 * CHUNK), BF16),
                   bp(LANES, 2 * seq, BF16), bp(seq, LANES, BF16), bp(seq // CHUNK, LANES, F32),
                   jax.ShapeDtypeStruct((m, d), F32)),
        grid=(batch, pairs, nr),
        in_specs=[col(0), col(1), col(2), flat, flat, prow, prow, prow],
        out_specs=(pblk(rows, LANES), pblk(rows, LANES), pblk(rows, LANES),
                   pl.BlockSpec((None, None, 2, rows, 4 * CHUNK), lambda b, p, i: (b, p, 0, i, 0)),
                   pl.BlockSpec((None, None, LANES, 2 * rows), lambda b, p, i: (b, p, 0, i)),
                   pblk(rows, LANES), pblk(ncr, LANES), flat),
        scratch_shapes=[pltpu.VMEM((npair, LANES, LANES), BF16), pltpu.VMEM((npair, LANES, LANES), BF16),
                        pltpu.VMEM((npair, LANES, LANES), BF16), pltpu.VMEM((2 * npair, LANES, LANES), BF16),
                        pltpu.VMEM((2 * npair, LANES, LANES), BF16), pltpu.VMEM((2 * npair, LANES, LANES), F32)],
        compiler_params=_params("parallel", "parallel", "parallel"),
        name="rwkv_prep",
    )(rkvg, rkvg, rkvg, ld, a, k_k.reshape(1, d), k_a.reshape(1, d), r_k.reshape(1, d))

    group = 8
    gw = group * LANES
    gblk = lambda rws, last: pl.BlockSpec((None, group, rws, last), lambda b, p, i: (b, p, i, 0))
    gflat = pl.BlockSpec((rows, gw), lambda b, p, i: (b * nr + i, p))
    grow = pl.BlockSpec((1, gw), lambda b, p, i: (0, p))
    o = pl.pallas_call(
        functools.partial(_rwkv_scan_kernel, group=group),
        out_shape=jax.ShapeDtypeStruct((m, d), BF16),
        grid=(batch, pairs // group, nr),
        in_specs=[gblk(rows, LANES), gblk(rows, LANES), gblk(rows, LANES),
                  pl.BlockSpec((None, group, 2, rows, 4 * CHUNK), lambda b, p, i: (b, p, 0, i, 0)),
                  pl.BlockSpec((None, group, LANES, 2 * rows), lambda b, p, i: (b, p, 0, i)),
                  gblk(rows, LANES), gblk(ncr, LANES), gflat,
                  pl.BlockSpec((None, rows, gw), lambda b, p, i: (3, b * nr + i, p)),
                  grow, grow],
        out_specs=gflat,
        scratch_shapes=[pltpu.VMEM((group, LANES, LANES), F32)],
        compiler_params=_params("parallel", "parallel", "arbitrary"),
        name="rwkv_scan",
    )(wt_, rt_, u_, aro_, kbt_, vb_, dec_, bonus, rkvg, ln_w.reshape(1, d), ln_b.reshape(1, d))
    return matmul_residual(o, w_out.astype(BF16), x, name="rwkv_out_proj")


def kernel(x, p, positions, norm_g, pe_norm_g, pe_w_gate, pe_w_proj, final_norm_g, a_w_in, a_lam, a_subln_g, a_w_out, b_mu, b_w_rkvg, b_w0, b_w_w1, b_w_w2, b_a0, b_w_a1, b_w_a2, b_k_k, b_k_a, b_r_k, b_ln_w, b_ln_b, b_w_out, c_w_in, c_conv_w, c_A_log, c_dt_bias, c_norm_g, c_w_out):
    batch, seq, d = x.shape
    depth = p.shape[0]
    m = batch * seq
    xf = x.reshape(m, d)
    tables = rope_tables(positions)
    for i in range(depth):
        kind = i % N_MIXERS
        j = i // N_MIXERS
        if kind == 0:
            lam_init = 0.8 - 0.6 * math.exp(-0.3 * i)
            hn = rmsnorm(xf, norm_g[i], BF16)
            xf = diff_attention_layer(xf, hn, tables, a_w_in[j], a_lam[j], a_subln_g[j], a_w_out[j],
                                      batch, seq, lam_init)
        elif kind == 1:
            xf = rwkv7_layer(xf, norm_g[i], b_mu[j], b_w_rkvg[j], b_w0[j], b_w_w1[j], b_w_w2[j], b_a0[j],
                             b_w_a1[j], b_w_a2[j], b_k_k[j], b_k_a[j], b_r_k[j], b_ln_w[j], b_ln_b[j],
                             b_w_out[j], batch, seq)
        else:
            hn = rmsnorm(xf, norm_g[i], BF16)
            xf = gated_deltanet_layer(xf, hn, c_w_in[j], c_conv_w[j], c_A_log[j], c_dt_bias[j], c_norm_g[j],
                                      c_w_out[j], batch, seq)
        hn2 = rmsnorm(xf, pe_norm_g[i], BF16)
        xf = per_layer_embedding(xf, hn2, pe_w_gate[i].astype(BF16), p[i].reshape(m, -1),
                                 pe_w_proj[i].astype(BF16))
    return rmsnorm(xf, final_norm_g, F32).reshape(batch, seq, d)
```

```python
import functools
import math

import jax
import jax.numpy as jnp
from jax import lax
from jax.experimental import pallas as pl
from jax.experimental.pallas import tpu as pltpu

F32 = jnp.float32
BF16 = jnp.bfloat16

N_MIXERS = 3
NORM_EPS = 1e-6
LANES = 128
VMEM_LIMIT = 48 * 1024 * 1024

A_HEAD_DIM = 128
A_V_DIM = 2 * A_HEAD_DIM
ROT_DIM = A_HEAD_DIM // 4
ROPE_THETA = 500000.0
SUBLN_EPS = 1e-5
ATTN_BLOCK = 256

B_HEAD_DIM = 64
B_GN_EPS = 64e-5
LORA_PAD = 128

C_HEAD_DIM = 128
C_CONV_WIDTH = 4
CHUNK = 64
SCAN_ROWS = 512

NN = (((1,), (0,)), ((), ()))
NT = (((1,), (1,)), ((), ()))


def _dot(a, b, dims=NN):
    return lax.dot_general(a, b, dims, preferred_element_type=F32)


def _split2(x):
    hi = x.astype(BF16)
    lo = (x - hi.astype(F32)).astype(BF16)
    return hi, lo


def _split3(x):
    hi = x.astype(BF16)
    r = x - hi.astype(F32)
    mid = r.astype(BF16)
    lo = (r - mid.astype(F32)).astype(BF16)
    return hi, mid, lo


def _dot3(a, b, dims=NN):
    ah, al = _split2(a)
    bh, bl = _split2(b)
    return _dot(ah, bh, dims) + (_dot(ah, bl, dims) + _dot(al, bh, dims))


def _dot_xl(a, b_exact, dims=NN):
    h, m, l = _split3(a)
    return _dot(h, b_exact, dims) + (_dot(m, b_exact, dims) + _dot(l, b_exact, dims))


def _dot_xr(a_exact, b, dims=NN):
    h, m, l = _split3(b)
    return _dot(a_exact, h, dims) + (_dot(a_exact, m, dims) + _dot(a_exact, l, dims))


def _iota(shape, dim):
    return lax.broadcasted_iota(jnp.int32, shape, dim)


def _silu(x):
    return x * jax.nn.sigmoid(x)


def _softplus(x):
    return jnp.maximum(x, 0.0) + jnp.log(1.0 + jnp.exp(-jnp.abs(x)))


def _params(*sem):
    return pltpu.CompilerParams(dimension_semantics=sem, vmem_limit_bytes=VMEM_LIMIT)


def _inv_unit_lower(a, nilpotency):
    n = a.shape[0]
    eye = (_iota((n, n), 0) == _iota((n, n), 1)).astype(F32)
    p = -a
    t = eye + p
    for _ in range(int(math.log2(nilpotency)) - 1):
        p = _dot3(p, p)
        t = t + _dot3(t, p)
    return t


def _rmsnorm_kernel(x_ref, g_ref, o_ref, *, eps):
    x = x_ref[...]
    y = x * lax.rsqrt(jnp.mean(x * x, axis=-1, keepdims=True) + eps)
    o_ref[...] = (y * g_ref[...]).astype(o_ref.dtype)


def rmsnorm(x, g, out_dtype, *, eps=NORM_EPS, tm=512):
    m, d = x.shape
    tm = min(tm, m)
    return pl.pallas_call(
        functools.partial(_rmsnorm_kernel, eps=eps),
        out_shape=jax.ShapeDtypeStruct((m, d), out_dtype),
        grid=(m // tm,),
        in_specs=[pl.BlockSpec((tm, d), lambda i: (i, 0)), pl.BlockSpec((1, d), lambda i: (0, 0))],
        out_specs=pl.BlockSpec((tm, d), lambda i: (i, 0)),
        compiler_params=_params("parallel"),
        name="rmsnorm",
    )(x, g.reshape(1, d))


def _mm_kernel(a_ref, w_ref, *rest, epilogue):
    o_ref = rest[-1]
    acc = _dot(a_ref[...], w_ref[...])
    if epilogue is not None:
        acc = epilogue(acc, *rest[:-1])
    o_ref[...] = acc.astype(o_ref.dtype)


def matmul(a, w, out_dtype, *, tm=1024, tn=1024, extra=(), extra_specs=(), epilogue=None, name="matmul"):
    m, k = a.shape
    n = w.shape[1]
    tm, tn = min(tm, m), min(tn, n)
    return pl.pallas_call(
        functools.partial(_mm_kernel, epilogue=epilogue),
        out_shape=jax.ShapeDtypeStruct((m, n), out_dtype),
        grid=(n // tn, m // tm),
        in_specs=[pl.BlockSpec((tm, k), lambda j, i: (i, 0)),
                  pl.BlockSpec((k, tn), lambda j, i: (0, j))] + list(extra_specs),
        out_specs=pl.BlockSpec((tm, tn), lambda j, i: (i, j)),
        compiler_params=_params("parallel", "parallel"),
        name=name,
    )(a, w, *extra)


def matmul_residual(a, w, res, *, tm=1024, tn=1024, name="matmul_residual"):
    tm, tn = min(tm, a.shape[0]), min(tn, w.shape[1])
    return matmul(a, w, F32, tm=tm, tn=tn, extra=(res,),
                  extra_specs=(pl.BlockSpec((tm, tn), lambda j, i: (i, j)),),
                  epilogue=lambda acc, r_ref: r_ref[...] + acc, name=name)


def _ple_kernel(h_ref, wg_ref, p_ref, wp_ref, x_ref, o_ref):
    gate = jax.nn.sigmoid(_dot(h_ref[...], wg_ref[...]))
    proj = _dot(p_ref[...].astype(BF16), wp_ref[...])
    o_ref[...] = x_ref[...] + gate * proj


def per_layer_embedding(x, hn, w_gate, p, w_proj, *, tm=1024, tn=1024):
    m, d = x.shape
    pd = p.shape[1]
    tm, tn = min(tm, m), min(tn, d)
    return pl.pallas_call(
        _ple_kernel,
        out_shape=jax.ShapeDtypeStruct((m, d), F32),
        grid=(d // tn, m // tm),
        in_specs=[pl.BlockSpec((tm, d), lambda j, i: (i, 0)),
                  pl.BlockSpec((d, tn), lambda j, i: (0, j)),
                  pl.BlockSpec((tm, pd), lambda j, i: (i, 0)),
                  pl.BlockSpec((pd, tn), lambda j, i: (0, j)),
                  pl.BlockSpec((tm, tn), lambda j, i: (i, j))],
        out_specs=pl.BlockSpec((tm, tn), lambda j, i: (i, j)),
        compiler_params=_params("parallel", "parallel"),
        name="per_layer_embedding",
    )(hn, w_gate, p, w_proj, x)


def _rope_table_kernel(pos_ref, freq_ref, cos_ref, sin_lo_ref, sin_hi_ref):
    half = ROT_DIM // 2
    ang = pos_ref[...].astype(F32) * freq_ref[...]
    lane = _iota(ang.shape, 1)
    c, s = jnp.cos(ang), jnp.sin(ang)
    cos_ref[...] = jnp.where(lane < ROT_DIM, c, 1.0)
    sin_lo_ref[...] = jnp.where(lane < half, -s, 0.0)
    sin_hi_ref[...] = jnp.where((lane >= half) & (lane < ROT_DIM), s, 0.0)


def rope_tables(positions, *, tm=1024):
    m = positions.size
    tm = min(tm, m)
    inv_freq = ROPE_THETA ** (-jnp.arange(0, ROT_DIM, 2, dtype=F32) / ROT_DIM)
    freq_row = jnp.concatenate([inv_freq, inv_freq, jnp.zeros((LANES - ROT_DIM,), F32)]).reshape(1, LANES)
    shape = jax.ShapeDtypeStruct((m, LANES), F32)
    spec = pl.BlockSpec((tm, LANES), lambda i: (i, 0))
    return pl.pallas_call(
        _rope_table_kernel,
        out_shape=(shape, shape, shape),
        grid=(m // tm,),
        in_specs=[pl.BlockSpec((tm, 1), lambda i: (i, 0)), pl.BlockSpec((1, LANES), lambda i: (0, 0))],
        out_specs=(spec, spec, spec),
        compiler_params=_params("parallel"),
        name="rope_tables",
    )(positions.reshape(m, 1), freq_row)


def _attn_in_kernel(a_ref, w_ref, cos_ref, sin_lo_ref, sin_hi_ref, o_ref, *, n_q_blocks):
    j = pl.program_id(0)
    acc = _dot(a_ref[...], w_ref[...])
    half = ROT_DIM // 2

    @pl.when(j < 2 * n_q_blocks)
    def _():
        scale = jnp.where(j < n_q_blocks, A_HEAD_DIM ** -0.5, 1.0).astype(F32)
        cos, sin_lo, sin_hi = cos_ref[...], sin_lo_ref[...], sin_hi_ref[...]
        for g in range(acc.shape[1] // LANES):
            x = acc[:, g * LANES:(g + 1) * LANES]
            y = x * cos + pltpu.roll(x, LANES - half, 1) * sin_lo + pltpu.roll(x, half, 1) * sin_hi
            o_ref[:, g * LANES:(g + 1) * LANES] = (y * scale).astype(o_ref.dtype)

    @pl.when(j >= 2 * n_q_blocks)
    def _():
        o_ref[...] = acc.astype(o_ref.dtype)


def attn_in_proj(hn, w_in, tables, qk_width, *, tm=1024, tn=1024):
    m, k = hn.shape
    n = w_in.shape[1]
    tm, tn = min(tm, m), min(tn, qk_width)
    tspec = pl.BlockSpec((tm, LANES), lambda j, i: (i, 0))
    return pl.pallas_call(
        functools.partial(_attn_in_kernel, n_q_blocks=qk_width // tn),
        out_shape=jax.ShapeDtypeStruct((m, n), BF16),
        grid=(n // tn, m // tm),
        in_specs=[pl.BlockSpec((tm, k), lambda j, i: (i, 0)),
                  pl.BlockSpec((k, tn), lambda j, i: (0, j)), tspec, tspec, tspec],
        out_specs=pl.BlockSpec((tm, tn), lambda j, i: (i, j)),
        compiler_params=_params("parallel", "parallel"),
        name="attn_in_proj",
    )(hn, w_in, *tables)


def _attn_vt_kernel(w_ref, a_ref, o_ref):
    o_ref[...] = _dot(w_ref[...], a_ref[...], NT).astype(o_ref.dtype)


def attn_v_proj_t(hn, w_t, batch, seq, blk, *, tn=1024):
    m, k = hn.shape
    n = w_t.shape[0]
    tn = min(tn, n)
    nk = seq // blk
    return pl.pallas_call(
        _attn_vt_kernel,
        out_shape=jax.ShapeDtypeStruct((batch, nk, n, blk), BF16),
        grid=(n // tn, m // blk),
        in_specs=[pl.BlockSpec((tn, k), lambda j, i: (j, 0)),
                  pl.BlockSpec((blk, k), lambda j, i: (i, 0))],
        out_specs=pl.BlockSpec((None, None, tn, blk), lambda j, i: (i // nk, i % nk, j, 0)),
        compiler_params=_params("parallel", "parallel"),
        name="attn_v_proj_t",
    )(w_t, hn)


def _diff_attn_kernel(lam_ref, q_ref, k_ref, vt_ref, z_ref, g_ref, o_ref, m_ref, l_ref, acc_ref, *, blk, lam_init):
    i = pl.program_id(2)
    lam = lam_ref[...]
    lam_full = (jnp.exp(jnp.sum(lam[0:1] * lam[1:2], axis=-1, keepdims=True))
                - jnp.exp(jnp.sum(lam[2:3] * lam[3:4], axis=-1, keepdims=True)) + lam_init)
    m_ref[...] = jnp.full(m_ref.shape, -jnp.inf, F32)
    l_ref[...] = jnp.zeros(l_ref.shape, F32)
    acc_ref[...] = jnp.zeros(acc_ref.shape, F32)
    q = q_ref[...]
    d = A_HEAD_DIM

    def step(j, masked):
        start = pl.multiple_of(j * blk, blk)
        kb = k_ref[pl.ds(start, blk), :]
        vt = vt_ref[j]
        for c in range(2):
            s = _dot(kb[:, c * d:(c + 1) * d], q[:, c * d:(c + 1) * d], NT)
            if masked:
                s = jnp.where(_iota(s.shape, 0) <= _iota(s.shape, 1), s, -jnp.inf)
            m_prev = m_ref[c]
            m_new = jnp.maximum(m_prev, jnp.max(s, axis=0, keepdims=True))
            alpha = jnp.exp(m_prev - m_new)
            p = jnp.exp(s - m_new)
            l_ref[c] = alpha * l_ref[c] + jnp.sum(p, axis=0, keepdims=True)
            acc_ref[c] = alpha * acc_ref[c] + _dot(vt, p.astype(BF16))
            m_ref[c] = m_new

    def body(jj, carry):
        step(2 * jj, False)
        step(2 * jj + 1, False)
        return carry

    lax.fori_loop(0, i // 2, body, 0)

    @pl.when(i % 2 == 1)
    def _():
        step(i - 1, False)

    step(i, True)
    o = acc_ref[0] / l_ref[0] - lam_full * (acc_ref[1] / l_ref[1])
    o = o * lax.rsqrt(jnp.mean(o * o, axis=0, keepdims=True) + SUBLN_EPS) * g_ref[...]
    o = (o * (1.0 - lam_init)).T
    o_ref[...] = (o * _silu(z_ref[...].astype(F32))).astype(o_ref.dtype)


def diff_attention_core(qkz, vt, lam, subln_g, batch, seq, heads, lam_init):
    m = qkz.shape[0]
    blk = vt.shape[-1]
    nq = seq // blk
    w = A_V_DIM
    return pl.pallas_call(
        functools.partial(_diff_attn_kernel, blk=blk, lam_init=lam_init),
        out_shape=jax.ShapeDtypeStruct((m, heads * w), BF16),
        grid=(batch, heads, nq),
        in_specs=[pl.BlockSpec((4, A_HEAD_DIM), lambda b, h, i: (0, 0)),
                  pl.BlockSpec((blk, w), lambda b, h, i: (b * nq + i, h)),
                  pl.BlockSpec((seq, w), lambda b, h, i: (b, heads + h)),
                  pl.BlockSpec((None, nq, w, blk), lambda b, h, i: (b, 0, h, 0)),
                  pl.BlockSpec((blk, w), lambda b, h, i: (b * nq + i, 2 * heads + h)),
                  pl.BlockSpec((w, 1), lambda b, h, i: (0, 0))],
        out_specs=pl.BlockSpec((blk, w), lambda b, h, i: (b * nq + i, h)),
        scratch_shapes=[pltpu.VMEM((2, 1, blk), F32), pltpu.VMEM((2, 1, blk), F32),
                        pltpu.VMEM((2, w, blk), F32)],
        compiler_params=_params("parallel", "parallel", "parallel"),
        name="diff_attention",
    )(lam, qkz, qkz, vt, qkz, subln_g.reshape(w, 1))


def diff_attention_layer(x, hn, tables, w_in, lam, subln_g, w_out, batch, seq, lam_init):
    d = x.shape[1]
    heads = d // A_V_DIM
    qk_w = heads * 2 * A_HEAD_DIM
    v_w = heads * A_V_DIM
    blk = min(ATTN_BLOCK, seq)
    w_qkz = jnp.concatenate([w_in[:, :2 * qk_w], w_in[:, 2 * qk_w + v_w:]], axis=1).astype(BF16)
    w_vt = w_in[:, 2 * qk_w:2 * qk_w + v_w].T.astype(BF16)
    qkz = attn_in_proj(hn, w_qkz, tables, qk_w)
    vt = attn_v_proj_t(hn, w_vt, batch, seq, blk)
    o = diff_attention_core(qkz, vt, lam, subln_g, batch, seq, heads, lam_init)
    return matmul_residual(o, w_out.astype(BF16), x, name="attn_out_proj")


def _gdn_in_kernel(a_ref, w_ref, cw_ref, o_ref, tail_ref, *, n_conv_blocks, rows_per_seq):
    j = pl.program_id(0)
    i = pl.program_id(1)
    acc = _dot(a_ref[...], w_ref[...])
    tm = acc.shape[0]

    @pl.when(j < n_conv_blocks)
    def _():
        @pl.when((i * tm) % rows_per_seq == 0)
        def _():
            tail_ref[...] = jnp.zeros(tail_ref.shape, F32)

        tail = tail_ref[...]
        sub = _iota(tail.shape, 0)

        def shifted(s):
            xs = pltpu.roll(acc, s, 0)
            head = jnp.where(sub < s, pltpu.roll(tail, s, 0), xs[:8])
            return jnp.concatenate([head, xs[8:]], axis=0)

        cw = cw_ref[...]
        last = C_CONV_WIDTH - 1
        y = shifted(last) * cw[0:1]
        for t in range(1, last):
            y = y + shifted(last - t) * cw[t:t + 1]
        y = y + acc * cw[last:last + 1]
        tail_ref[...] = acc[tm - 8:]
        o_ref[...] = _silu(y).astype(o_ref.dtype)

    @pl.when(j >= n_conv_blocks)
    def _():
        o_ref[...] = acc.astype(o_ref.dtype)


def gdn_in_proj(hn, w, conv_w, seq, *, tm=1024, tn=1024):
    m, k = hn.shape
    n = w.shape[1]
    conv_ch = conv_w.shape[1]
    tm, tn = min(tm, seq), min(tn, n)
    ncb = conv_ch // tn
    return pl.pallas_call(
        functools.partial(_gdn_in_kernel, n_conv_blocks=ncb, rows_per_seq=seq),
        out_shape=jax.ShapeDtypeStruct((m, n), BF16),
        grid=(n // tn, m // tm),
        in_specs=[pl.BlockSpec((tm, k), lambda j, i: (i, 0)),
                  pl.BlockSpec((k, tn), lambda j, i: (0, j)),
                  pl.BlockSpec((C_CONV_WIDTH, tn), lambda j, i: (0, jnp.minimum(j, ncb - 1)))],
        out_specs=pl.BlockSpec((tm, tn), lambda j, i: (i, j)),
        scratch_shapes=[pltpu.VMEM((8, tn), F32)],
        compiler_params=_params("arbitrary", "arbitrary"),
        name="gdn_in_proj",
    )(hn, w, conv_w)


def _pair_masks():
    n = 2 * CHUNK
    r, c = _iota((n, n), 0), _iota((n, n), 1)
    same = (r // CHUNK) == (c // CHUNK)
    return same & (r >= c), same & (r > c), r == c


def _dot_x2l(a, b_exact, dims=NN):
    h, l = _split2(a)
    return _dot(h, b_exact, dims) + _dot(l, b_exact, dims)


def _dot_x2r(a_exact, b, dims=NN):
    h, l = _split2(b)
    return _dot(a_exact, h, dims) + _dot(a_exact, l, dims)


def _inverse_stages(p_ref, t_ref, count, nilpotency):
    for _ in range(int(math.log2(nilpotency)) - 1):
        for i in range(count):
            p_ref[i] = _dot(p_ref[i], p_ref[i]).astype(BF16)
        for i in range(count):
            t = t_ref[i]
            t_ref[i] = t + _dot(t.astype(BF16), p_ref[i])


def _gdn_prep_kernel(q_ref, k_ref, v_ref, b_ref, a_ref, alog_ref, dtb_ref,
                     w_ref, qg_ref, u_ref, att_ref, kdt_ref, dec_ref,
                     g_s, dec_s, kn_s, kb_s, rhs_s, p_s, t_s):
    h = pl.program_id(1)
    c = CHUNK
    n2 = 2 * c
    dk = C_HEAD_DIM
    npairs = q_ref.shape[0] // n2
    tri, strict, diag = _pair_masks()
    tri_b = tri.astype(BF16)
    eye_b = diag.astype(BF16)
    eye_f = diag.astype(F32)
    sel = (_iota((LANES, dk), 0) == h).astype(BF16)
    first = (_iota((n2, dk), 1) == 0).astype(BF16)
    upper_half = _iota((n2, dk), 0) < c
    g_all = -jnp.exp(alog_ref[...]) * _softplus(a_ref[...] + dtb_ref[...])
    pairs = [slice(n * n2, (n + 1) * n2) for n in range(npairs)]

    for n, rows in enumerate(pairs):
        g_s[n] = _dot_x2l(g_all[rows], sel)
    for n, rows in enumerate(pairs):
        g_s[n] = _dot_x2r(tri_b, g_s[n])
    for n, rows in enumerate(pairs):
        gc = g_s[n]
        gc_row = _dot_x2r(first, gc, NT)
        dec_s[n] = jnp.where(tri, jnp.exp(jnp.where(tri, gc - gc_row, 0.0)), 0.0)
    for n, rows in enumerate(pairs):
        gc = g_s[n]
        qf = q_ref[rows, :].astype(F32)
        kf = k_ref[rows, :].astype(F32)
        vf = v_ref[rows, :].astype(F32)
        qn = qf * lax.rsqrt(jnp.sum(qf * qf, axis=-1, keepdims=True) + 1e-6) * (dk ** -0.5)
        kn = kf * lax.rsqrt(jnp.sum(kf * kf, axis=-1, keepdims=True) + 1e-6)
        beta = _dot(jax.nn.sigmoid(b_ref[rows, :]).astype(BF16), sel)
        egc = jnp.exp(gc)
        kb = kn * beta
        kn_b = kn.astype(BF16)
        qn_b = qn.astype(BF16)
        kn_s[n] = kn_b
        kb_s[n] = kb.astype(BF16)
        rhs_s[n] = jnp.concatenate([vf * beta, kb * egc], axis=1).astype(BF16)
        g_last = jnp.where(upper_half, gc[c - 1:c, :], gc[n2 - 1:n2, :])
        kdec = kn * jnp.exp(g_last - gc)
        qg_ref[rows, :] = (qn * egc).astype(BF16)
        att_ref[rows, :] = (_dot(qn_b, kn_b, NT) * dec_s[n]).astype(BF16)
        kdt_ref[:, rows] = _dot(eye_b, kdec.astype(BF16), NT).astype(BF16)
        dec_ref[2 * n:2 * n + 1, :] = jnp.exp(gc[c - 1:c, :])
        dec_ref[2 * n + 1:2 * n + 2, :] = jnp.exp(gc[n2 - 1:n2, :])
    for n, rows in enumerate(pairs):
        neg_a = jnp.where(strict, -(_dot(kb_s[n], kn_s[n], NT) * dec_s[n]), 0.0)
        p_s[n] = neg_a.astype(BF16)
        t_s[n] = eye_f + neg_a
    _inverse_stages(p_s, t_s, npairs, c)
    for n, rows in enumerate(pairs):
        sol = _dot(t_s[n].astype(BF16), rhs_s[n])
        u_ref[rows, :] = sol[:, :dk]
        w_ref[rows, :] = sol[:, dk:].astype(BF16)


def _gdn_scan_kernel(w_ref, qg_ref, u_ref, att_ref, kdt_ref, dec_ref, z_ref, g_ref, o_ref, s_ref, *, group):
    c = CHUNK
    dk = C_HEAD_DIM
    zeros = jnp.zeros((c, dk), BF16)

    @pl.when(pl.program_id(2) == 0)
    def _():
        s_ref[...] = jnp.zeros(s_ref.shape, F32)

    for n in range(w_ref.shape[1] // c):
        rows = slice(n * c, (n + 1) * c)
        pair = slice((n // 2) * 2 * c, (n // 2 + 1) * 2 * c)
        for gi in range(group):
            s = s_ref[gi]
            lhs = jnp.concatenate([w_ref[gi, rows, :], qg_ref[gi, rows, :]], axis=0)
            ms = _dot(lhs, s.astype(BF16))
            v_new = (u_ref[gi, rows, :] - ms[:c]).astype(BF16)
            v_pad = jnp.concatenate([v_new, zeros] if n % 2 == 0 else [zeros, v_new], axis=0)
            o = ms[c:] + _dot(att_ref[gi, rows, :], v_pad)
            s_ref[gi] = s * dec_ref[gi, n:n + 1, :] + _dot(kdt_ref[gi, :, pair], v_pad)
            o = o * lax.rsqrt(jnp.mean(o * o, axis=-1, keepdims=True) + NORM_EPS) * g_ref[...]
            z = z_ref[rows, gi * dk:(gi + 1) * dk].astype(F32)
            o_ref[rows, gi * dk:(gi + 1) * dk] = (o * _silu(z)).astype(o_ref.dtype)


def gated_deltanet_layer(x, hn, w_in, conv_w, a_log, dt_bias, norm_g, w_out, batch, seq):
    m, d = x.shape
    dk = C_HEAD_DIM
    k_heads = d // dk
    v_heads = 2 * k_heads
    conv_ch = 2 * k_heads * dk + v_heads * dk
    main_w = conv_ch + v_heads * dk
    qkvz = gdn_in_proj(hn, w_in[:, :main_w].astype(BF16), conv_w, seq)
    pad = jnp.zeros((d, LANES - v_heads), F32)
    w_ba = jnp.concatenate([w_in[:, main_w:main_w + v_heads], pad, w_in[:, main_w + v_heads:], pad], axis=1)
    ba = matmul(hn, w_ba.astype(BF16), F32, tn=2 * LANES, name="gdn_ba_proj")
    row_pad = jnp.zeros((LANES - v_heads,), F32)
    alog_row = jnp.concatenate([a_log, row_pad]).reshape(1, LANES)
    dtb_row = jnp.concatenate([dt_bias, row_pad]).reshape(1, LANES)

    rows = min(SCAN_ROWS, seq)
    nr = seq // rows
    ncr = rows // CHUNK
    npair = rows // (2 * CHUNK)
    hv = v_heads
    bh_t = lambda dt, last: jax.ShapeDtypeStruct((batch, hv, seq, last), dt)
    blk4 = lambda last: pl.BlockSpec((None, None, rows, last), lambda b, h, i: (b, h, i, 0))
    qoff, koff, voff = 0, k_heads, 2 * k_heads
    w_, qg_, u_, att_, kdt_, dec_ = pl.pallas_call(
        _gdn_prep_kernel,
        out_shape=(bh_t(BF16, dk), bh_t(BF16, dk), bh_t(F32, dk), bh_t(BF16, 2 * CHUNK),
                   jax.ShapeDtypeStruct((batch, hv, dk, seq), BF16),
                   jax.ShapeDtypeStruct((batch, hv, seq // CHUNK, dk), F32)),
        grid=(batch, hv, nr),
        in_specs=[pl.BlockSpec((rows, dk), lambda b, h, i: (b * nr + i, qoff + h // 2)),
                  pl.BlockSpec((rows, dk), lambda b, h, i: (b * nr + i, koff + h // 2)),
                  pl.BlockSpec((rows, dk), lambda b, h, i: (b * nr + i, voff + h)),
                  pl.BlockSpec((rows, LANES), lambda b, h, i: (b * nr + i, 0)),
                  pl.BlockSpec((rows, LANES), lambda b, h, i: (b * nr + i, 1)),
                  pl.BlockSpec((1, LANES), lambda b, h, i: (0, 0)),
                  pl.BlockSpec((1, LANES), lambda b, h, i: (0, 0))],
        out_specs=(blk4(dk), blk4(dk), blk4(dk), blk4(2 * CHUNK),
                   pl.BlockSpec((None, None, dk, rows), lambda b, h, i: (b, h, 0, i)),
                   pl.BlockSpec((None, None, ncr, dk), lambda b, h, i: (b, h, i, 0))),
        scratch_shapes=[pltpu.VMEM((npair, dk, dk), F32), pltpu.VMEM((npair, dk, dk), F32),
                        pltpu.VMEM((npair, dk, dk), BF16), pltpu.VMEM((npair, dk, dk), BF16),
                        pltpu.VMEM((npair, dk, 2 * dk), BF16), pltpu.VMEM((npair, dk, dk), BF16),
                        pltpu.VMEM((npair, dk, dk), F32)],
        compiler_params=_params("parallel", "parallel", "parallel"),
        name="gdn_prep",
    )(qkvz, qkvz, qkvz, ba, ba, alog_row, dtb_row)

    group = 8
    zoff = conv_ch // (group * dk)
    gblk = lambda last: pl.BlockSpec((None, group, rows, last), lambda b, h, i: (b, h, i, 0))
    o = pl.pallas_call(
        functools.partial(_gdn_scan_kernel, group=group),
        out_shape=jax.ShapeDtypeStruct((m, hv * dk), BF16),
        grid=(batch, hv // group, nr),
        in_specs=[gblk(dk), gblk(dk), gblk(dk), gblk(2 * CHUNK),
                  pl.BlockSpec((None, group, dk, rows), lambda b, h, i: (b, h, 0, i)),
                  pl.BlockSpec((None, group, ncr, dk), lambda b, h, i: (b, h, i, 0)),
                  pl.BlockSpec((rows, group * dk), lambda b, h, i: (b * nr + i, zoff + h)),
                  pl.BlockSpec((1, dk), lambda b, h, i: (0, 0))],
        out_specs=pl.BlockSpec((rows, group * dk), lambda b, h, i: (b * nr + i, h)),
        scratch_shapes=[pltpu.VMEM((group, dk, dk), F32)],
        compiler_params=_params("parallel", "parallel", "arbitrary"),
        name="gdn_scan",
    )(w_, qg_, u_, att_, kdt_, dec_, qkvz, norm_g.reshape(1, dk))
    return matmul_residual(o, w_out.astype(BF16), x, tm=512, name="gdn_out_proj")


def _rwkv_mix_kernel(x_ref, g_ref, mu_ref, o_ref, tail_ref, *, rows_per_seq):
    i = pl.program_id(0)
    x = x_ref[...]
    tm = x.shape[0]
    hn = x * lax.rsqrt(jnp.mean(x * x, axis=-1, keepdims=True) + NORM_EPS) * g_ref[...]

    @pl.when((i * tm) % rows_per_seq == 0)
    def _():
        tail_ref[...] = jnp.zeros(tail_ref.shape, F32)

    prev = jnp.where(_iota(hn.shape, 0) == 0, tail_ref[7:8, :], pltpu.roll(hn, 1, 0))
    tail_ref[...] = hn[tm - 8:]
    xx = prev - hn
    for c in range(o_ref.shape[0]):
        o_ref[c] = (hn + xx * mu_ref[c:c + 1, :]).astype(o_ref.dtype)


def rwkv_token_mix(x, norm_g, mu, seq, *, tm=256):
    m, d = x.shape
    tm = min(tm, seq)
    nmix = mu.shape[0]
    return pl.pallas_call(
        functools.partial(_rwkv_mix_kernel, rows_per_seq=seq),
        out_shape=jax.ShapeDtypeStruct((nmix, m, d), BF16),
        grid=(m // tm,),
        in_specs=[pl.BlockSpec((tm, d), lambda i: (i, 0)),
                  pl.BlockSpec((1, d), lambda i: (0, 0)),
                  pl.BlockSpec((nmix, d), lambda i: (0, 0))],
        out_specs=pl.BlockSpec((nmix, tm, d), lambda i: (0, i, 0)),
        scratch_shapes=[pltpu.VMEM((8, d), F32)],
        compiler_params=_params("arbitrary"),
        name="rwkv_token_mix",
    )(x, norm_g.reshape(1, d), mu)


def _grouped_mm_kernel(a_ref, w_ref, o_ref):
    o_ref[...] = _dot(a_ref[...], w_ref[...]).astype(o_ref.dtype)


def grouped_matmul(a, w, out_dtype, *, tm=1024, tn=1024):
    g, k, n = w.shape
    m = a.shape[1]
    tm, tn = min(tm, m), min(tn, n)
    return pl.pallas_call(
        _grouped_mm_kernel,
        out_shape=jax.ShapeDtypeStruct((g, m, n), out_dtype),
        grid=(g, n // tn, m // tm),
        in_specs=[pl.BlockSpec((None, tm, k), lambda c, j, i: (c, i, 0)),
                  pl.BlockSpec((None, k, tn), lambda c, j, i: (c, 0, j))],
        out_specs=pl.BlockSpec((None, tm, tn), lambda c, j, i: (c, i, j)),
        compiler_params=_params("parallel", "parallel", "parallel"),
        name="rwkv_rkvg_proj",
    )(a, w)


def _rwkv_lora_kernel(xw_ref, xa_ref, w1_ref, w2_ref, a1_ref, a2_ref, w0_ref, a0_ref, ld_ref, a_ref):
    hw = jnp.tanh(_dot(xw_ref[...], w1_ref[...])).astype(BF16)
    lw = w0_ref[...] + _dot(hw, w2_ref[...])
    log_w = -_softplus(-lw) - 0.5
    ld_ref[...] = -jnp.exp(log_w)
    ha = _dot(xa_ref[...], a1_ref[...]).astype(BF16)
    a_ref[...] = jax.nn.sigmoid(a0_ref[...] + _dot(ha, a2_ref[...]))


def rwkv_lora(xs, w1, w2, a1, a2, w0, a0, *, tm=512):
    _, m, d = xs.shape
    tm = min(tm, m)
    r = w1.shape[1]
    row = pl.BlockSpec((1, d), lambda i: (0, 0))
    shape = jax.ShapeDtypeStruct((m, d), F32)
    return pl.pallas_call(
        _rwkv_lora_kernel,
        out_shape=(shape, shape),
        grid=(m // tm,),
        in_specs=[pl.BlockSpec((None, tm, d), lambda i: (4, i, 0)),
                  pl.BlockSpec((None, tm, d), lambda i: (5, i, 0)),
                  pl.BlockSpec((d, r), lambda i: (0, 0)), pl.BlockSpec((r, d), lambda i: (0, 0)),
                  pl.BlockSpec((d, r), lambda i: (0, 0)), pl.BlockSpec((r, d), lambda i: (0, 0)),
                  row, row],
        out_specs=(pl.BlockSpec((tm, d), lambda i: (i, 0)), pl.BlockSpec((tm, d), lambda i: (i, 0))),
        compiler_params=_params("parallel"),
        name="rwkv_lora",
    )(xs, xs, w1, w2, a1, a2, w0.reshape(1, d), a0.reshape(1, d))


def _rwkv_prep_kernel(r_ref, k_ref, v_ref, ld_ref, a_ref, kk_ref, ka_ref, rk_ref,
                      wt_ref, rt_ref, u_ref, aro_ref, kbt_ref, vb_ref, dec_ref, bonus_ref,
                      kkt_s, kbar_s, bbar_s, akk_s, p_s, t_s):
    c = CHUNK
    n2 = 2 * c
    n_h = B_HEAD_DIM
    npairs = r_ref.shape[0] // n2
    lane_head = _iota((1, LANES), 1) // n_h
    seg = (_iota((LANES, LANES), 0) // n_h == _iota((LANES, LANES), 1) // n_h).astype(BF16)
    tri, strict, diag = _pair_masks()
    tri_b = tri.astype(BF16)
    eye_b = diag.astype(BF16)
    eye_f = diag.astype(F32)
    upper_half = _iota((n2, LANES), 0) < c
    pairs = [slice(n * n2, (n + 1) * n2) for n in range(npairs)]

    r = r_ref[...]
    k = k_ref[...]
    v = v_ref[...]
    a = a_ref[...]
    kk_raw = k * kk_ref[...]
    kk = kk_raw * lax.rsqrt(_dot_xl(kk_raw * kk_raw, seg) + 1e-6)
    k2 = k * (1.0 + (a - 1.0) * ka_ref[...])
    bb = kk * a
    bonus_ref[...] = _dot_xl(r * k2 * rk_ref[...], seg) * v
    vb_ref[...] = v.astype(BF16)
    for n, rows in enumerate(pairs):
        ld = ld_ref[rows, :]
        cs = _dot_xr(tri_b, ld)
        cs_last = jnp.where(upper_half, cs[c - 1:c, :], cs[n2 - 1:n2, :])
        w_inv = jnp.exp(-cs)
        w_end = jnp.exp(cs_last - cs)
        rt_ref[rows, :] = (r[rows] * jnp.exp(cs)).astype(BF16)
        kkt_s[n] = (kk[rows] * jnp.exp(cs - ld)).astype(BF16)
        k2c, bbc = k2[rows], bb[rows]
        kbar_s[n] = (k2c * w_inv).astype(BF16)
        bbar_s[n] = (bbc * w_inv).astype(BF16)
        kw, bw = (k2c * w_end).astype(BF16), (-(bbc * w_end)).astype(BF16)
        for half in range(2):
            hs = slice(half * c, (half + 1) * c)
            kb_end = jnp.concatenate([kw[hs], bw[hs]], axis=0)
            cols = slice((2 * n + half) * n2, (2 * n + half + 1) * n2)
            kbt_ref[:, cols] = _dot(eye_b, kb_end, NT).astype(BF16)
        dec_ref[2 * n:2 * n + 1, :] = jnp.exp(cs[c - 1:c, :])
        dec_ref[2 * n + 1:2 * n + 2, :] = jnp.exp(cs[n2 - 1:n2, :])
    zero_b = jnp.zeros((n2, LANES), BF16)
    for n, rows in enumerate(pairs):
        for hh in range(2):
            own = lane_head == hh
            i = 2 * n + hh
            kkt_m = jnp.where(own, kkt_s[n], zero_b)
            rt_m = jnp.where(own, rt_ref[rows, :], zero_b)
            neg_a = jnp.where(strict, -_dot(kkt_m, bbar_s[n], NT), 0.0)
            p_s[i] = neg_a.astype(BF16)
            t_s[i] = eye_f + neg_a
            akk_s[i] = jnp.where(strict, _dot(kkt_m, kbar_s[n], NT), 0.0).astype(BF16)
            a_rk = jnp.where(tri, _dot(rt_m, kbar_s[n], NT), 0.0)
            a_rb = jnp.where(tri, _dot(rt_m, bbar_s[n], NT), 0.0)
            aro_ref[hh, rows, :] = jnp.concatenate([a_rk, -a_rb], axis=1).astype(BF16)
    _inverse_stages(p_s, t_s, 2 * npairs, c)
    for n, rows in enumerate(pairs):
        for hh in range(2):
            i = 2 * n + hh
            p_s[i] = _dot(akk_s[i], vb_ref[rows, :]).astype(BF16)
    for n, rows in enumerate(pairs):
        t0, t1 = t_s[2 * n].astype(BF16), t_s[2 * n + 1].astype(BF16)
        own0 = lane_head == 0
        wt_ref[rows, :] = jnp.where(own0, _dot(t0, kkt_s[n]), _dot(t1, kkt_s[n])).astype(BF16)
        u_ref[rows, :] = jnp.where(own0, _dot(t0, p_s[2 * n]), _dot(t1, p_s[2 * n + 1]))


def _rwkv_scan_kernel(wt_ref, rt_ref, u_ref, aro_ref, kbt_ref, vb_ref, dec_ref, bonus_ref, gate_ref,
                      lnw_ref, lnb_ref, o_ref, s_ref, *, group):
    c = CHUNK
    n2 = 2 * c
    n_h = B_HEAD_DIM
    lane_head = _iota((1, LANES), 1) // n_h
    r_i, c_i = _iota((LANES, LANES), 0), _iota((LANES, LANES), 1)
    same_head = r_i // n_h == c_i // n_h
    avg = jnp.where(same_head, 1.0 / n_h, 0.0).astype(BF16)
    eye = (r_i == c_i).astype(BF16)
    zeros = jnp.zeros((c, LANES), BF16)

    @pl.when(pl.program_id(2) == 0)
    def _():
        s_ref[...] = jnp.zeros(s_ref.shape, F32)

    for n in range(u_ref.shape[1] // c):
        rows = slice(n * c, (n + 1) * c)
        for gi in range(group):
            s = s_ref[gi]
            lhs = jnp.concatenate([wt_ref[gi, rows, :], rt_ref[gi, rows, :]], axis=0)
            ms = _dot(lhs, s.astype(BF16))
            sa = (u_ref[gi, rows, :] + ms[:c]).astype(BF16)
            vb = vb_ref[gi, rows, :]
            x_pad = jnp.concatenate([vb, zeros, sa, zeros] if n % 2 == 0 else [zeros, vb, zeros, sa], axis=0)
            o0 = _dot(aro_ref[gi, 0, rows, :], x_pad)
            o1 = _dot(aro_ref[gi, 1, rows, :], x_pad)
            o = ms[c:] + jnp.where(lane_head == 0, o0, o1)
            upd = _dot(kbt_ref[gi, :, n * n2:(n + 1) * n2], jnp.concatenate([vb, sa], axis=0))
            dec_col = _dot_xr(eye, jnp.broadcast_to(dec_ref[gi, n:n + 1, :], (LANES, LANES)), NT)
            s_ref[gi] = s * dec_col + jnp.where(same_head, upd, 0.0)
            mean = _dot_xl(o, avg)
            dlt = o - mean
            var = _dot_xl(dlt * dlt, avg)
            cols = slice(gi * LANES, (gi + 1) * LANES)
            y = dlt * lax.rsqrt(var + B_GN_EPS) * lnw_ref[:, cols] + lnb_ref[:, cols]
            y = y + bonus_ref[rows, cols]
            o_ref[rows, cols] = (y * _silu(gate_ref[rows, cols])).astype(o_ref.dtype)


def rwkv7_layer(x, norm_g, mu, w_rkvg, w0, w_w1, w_w2, a0, w_a1, w_a2, k_k, k_a, r_k, ln_w, ln_b, w_out,
                batch, seq):
    m, d = x.shape
    heads = d // B_HEAD_DIM
    pairs = heads // 2
    order = jnp.array([0, 2, 3, 5, 1, 4])
    xs = rwkv_token_mix(x, norm_g, mu[order], seq)
    rkvg = grouped_matmul(xs, w_rkvg.astype(BF16), F32)
    lora = w_w1.shape[1]
    padc = lambda w: jnp.pad(w, ((0, 0), (0, LORA_PAD - lora))).astype(BF16)
    padr = lambda w: jnp.pad(w, ((0, LORA_PAD - lora), (0, 0))).astype(BF16)
    ld, a = rwkv_lora(xs, padc(w_w1), padr(w_w2), padc(w_a1), padr(w_a2), w0, a0)

    rows = min(SCAN_ROWS, seq)
    nr = seq // rows
    ncr = rows // CHUNK
    npair = rows // (2 * CHUNK)
    col = lambda g: pl.BlockSpec((None, rows, LANES), lambda b, p, i, g=g: (g, b * nr + i, p))
    flat = pl.BlockSpec((rows, LANES), lambda b, p, i: (b * nr + i, p))
    prow = pl.BlockSpec((1, LANES), lambda b, p, i: (0, p))
    bp = lambda rws, last, dt: jax.ShapeDtypeStruct((batch, pairs, rws, last), dt)
    pblk = lambda rws, last: pl.BlockSpec((None, None, rws, last), lambda b, p, i: (b, p, i, 0))
    wt_, rt_, u_, aro_, kbt_, vb_, dec_, bonus = pl.pallas_call(
        _rwkv_prep_kernel,
        out_shape=(bp(seq, LANES, BF16), bp(seq, LANES, BF16), bp(seq, LANES, F32),
                   jax.ShapeDtypeStruct((batch, pairs, 2, seq, 4 * CHUNK), BF16),
                   bp(LANES, 2 * seq, BF16), bp(seq, LANES, BF16), bp(seq // CHUNK, LANES, F32),
                   jax.ShapeDtypeStruct((m, d), F32)),
        grid=(batch, pairs, nr),
        in_specs=[col(0), col(1), col(2), flat, flat, prow, prow, prow],
        out_specs=(pblk(rows, LANES), pblk(rows, LANES), pblk(rows, LANES),
                   pl.BlockSpec((None, None, 2, rows, 4 * CHUNK), lambda b, p, i: (b, p, 0, i, 0)),
                   pl.BlockSpec((None, None, LANES, 2 * rows), lambda b, p, i: (b, p, 0, i)),
                   pblk(rows, LANES), pblk(ncr, LANES), flat),
        scratch_shapes=[pltpu.VMEM((npair, LANES, LANES), BF16), pltpu.VMEM((npair, LANES, LANES), BF16),
                        pltpu.VMEM((npair, LANES, LANES), BF16), pltpu.VMEM((2 * npair, LANES, LANES), BF16),
                        pltpu.VMEM((2 * npair, LANES, LANES), BF16), pltpu.VMEM((2 * npair, LANES, LANES), F32)],
        compiler_params=_params("parallel", "parallel", "parallel"),
        name="rwkv_prep",
    )(rkvg, rkvg, rkvg, ld, a, k_k.reshape(1, d), k_a.reshape(1, d), r_k.reshape(1, d))

    group = 8
    gw = group * LANES
    gblk = lambda rws, last: pl.BlockSpec((None, group, rws, last), lambda b, p, i: (b, p, i, 0))
    gflat = pl.BlockSpec((rows, gw), lambda b, p, i: (b * nr + i, p))
    grow = pl.BlockSpec((1, gw), lambda b, p, i: (0, p))
    o = pl.pallas_call(
        functools.partial(_rwkv_scan_kernel, group=group),
        out_shape=jax.ShapeDtypeStruct((m, d), BF16),
        grid=(batch, pairs // group, nr),
        in_specs=[gblk(rows, LANES), gblk(rows, LANES), gblk(rows, LANES),
                  pl.BlockSpec((None, group, 2, rows, 4 * CHUNK), lambda b, p, i: (b, p, 0, i, 0)),
                  pl.BlockSpec((None, group, LANES, 2 * rows), lambda b, p, i: (b, p, 0, i)),
                  gblk(rows, LANES), gblk(ncr, LANES), gflat,
                  pl.BlockSpec((None, rows, gw), lambda b, p, i: (3, b * nr + i, p)),
                  grow, grow],
        out_specs=gflat,
        scratch_shapes=[pltpu.VMEM((group, LANES, LANES), F32)],
        compiler_params=_params("parallel", "parallel", "arbitrary"),
        name="rwkv_scan",
    )(wt_, rt_, u_, aro_, kbt_, vb_, dec_, bonus, rkvg, ln_w.reshape(1, d), ln_b.reshape(1, d))
    return matmul_residual(o, w_out.astype(BF16), x, name="rwkv_out_proj")


def kernel(x, p, positions, norm_g, pe_norm_g, pe_w_gate, pe_w_proj, final_norm_g, a_w_in, a_lam, a_subln_g, a_w_out, b_mu, b_w_rkvg, b_w0, b_w_w1, b_w_w2, b_a0, b_w_a1, b_w_a2, b_k_k, b_k_a, b_r_k, b_ln_w, b_ln_b, b_w_out, c_w_in, c_conv_w, c_A_log, c_dt_bias, c_norm_g, c_w_out):
    batch, seq, d = x.shape
    depth = p.shape[0]
    m = batch * seq
    xf = x.reshape(m, d)
    tables = rope_tables(positions)
    for i in range(depth):
        kind = i % N_MIXERS
        j = i // N_MIXERS
        if kind == 0:
            lam_init = 0.8 - 0.6 * math.exp(-0.3 * i)
            hn = rmsnorm(xf, norm_g[i], BF16)
            xf = diff_attention_layer(xf, hn, tables, a_w_in[j], a_lam[j], a_subln_g[j], a_w_out[j],
                                      batch, seq, lam_init)
        elif kind == 1:
            xf = rwkv7_layer(xf, norm_g[i], b_mu[j], b_w_rkvg[j], b_w0[j], b_w_w1[j], b_w_w2[j], b_a0[j],
                             b_w_a1[j], b_w_a2[j], b_k_k[j], b_k_a[j], b_r_k[j], b_ln_w[j], b_ln_b[j],
                             b_w_out[j], batch, seq)
        else:
            hn = rmsnorm(xf, norm_g[i], BF16)
            xf = gated_deltanet_layer(xf, hn, c_w_in[j], c_conv_w[j], c_A_log[j], c_dt_bias[j], c_norm_g[j],
                                      c_w_out[j], batch, seq)
        hn2 = rmsnorm(xf, pe_norm_g[i], BF16)
        xf = per_layer_embedding(xf, hn2, pe_w_gate[i].astype(BF16), p[i].reshape(m, -1),
                                 pe_w_proj[i].astype(BF16))
    return rmsnorm(xf, final_norm_g, F32).reshape(batch, seq, d)
```

```python
import functools
import math

import jax
import jax.numpy as jnp
from jax import lax
from jax.experimental import pallas as pl
from jax.experimental.pallas import tpu as pltpu

F32 = jnp.float32
BF16 = jnp.bfloat16

N_MIXERS = 3
NORM_EPS = 1e-6
LANES = 128
VMEM_LIMIT = 48 * 1024 * 1024

A_HEAD_DIM = 128
A_V_DIM = 2 * A_HEAD_DIM
ROT_DIM = A_HEAD_DIM // 4
ROPE_THETA = 500000.0
SUBLN_EPS = 1e-5
ATTN_BLOCK = 256

B_HEAD_DIM = 64
B_GN_EPS = 64e-5
LORA_PAD = 128

C_HEAD_DIM = 128
C_CONV_WIDTH = 4
CHUNK = 64
SCAN_ROWS = 512

NN = (((1,), (0,)), ((), ()))
NT = (((1,), (1,)), ((), ()))


def _dot(a, b, dims=NN):
    return lax.dot_general(a, b, dims, preferred_element_type=F32)


def _split2(x):
    hi = x.astype(BF16)
    lo = (x - hi.astype(F32)).astype(BF16)
    return hi, lo


def _split3(x):
    hi = x.astype(BF16)
    r = x - hi.astype(F32)
    mid = r.astype(BF16)
    lo = (r - mid.astype(F32)).astype(BF16)
    return hi, mid, lo


def _dot3(a, b, dims=NN):
    ah, al = _split2(a)
    bh, bl = _split2(b)
    return _dot(ah, bh, dims) + (_dot(ah, bl, dims) + _dot(al, bh, dims))


def _dot_xl(a, b_exact, dims=NN):
    h, m, l = _split3(a)
    return _dot(h, b_exact, dims) + (_dot(m, b_exact, dims) + _dot(l, b_exact, dims))


def _dot_xr(a_exact, b, dims=NN):
    h, m, l = _split3(b)
    return _dot(a_exact, h, dims) + (_dot(a_exact, m, dims) + _dot(a_exact, l, dims))


def _iota(shape, dim):
    return lax.broadcasted_iota(jnp.int32, shape, dim)


def _silu(x):
    return x * jax.nn.sigmoid(x)


def _softplus(x):
    return jnp.maximum(x, 0.0) + jnp.log(1.0 + jnp.exp(-jnp.abs(x)))


def _params(*sem):
    return pltpu.CompilerParams(dimension_semantics=sem, vmem_limit_bytes=VMEM_LIMIT)


def _inv_unit_lower(a, nilpotency):
    n = a.shape[0]
    eye = (_iota((n, n), 0) == _iota((n, n), 1)).astype(F32)
    p = -a
    t = eye + p
    for _ in range(int(math.log2(nilpotency)) - 1):
        p = _dot3(p, p)
        t = t + _dot3(t, p)
    return t


def _rmsnorm_kernel(x_ref, g_ref, o_ref, *, eps):
    x = x_ref[...]
    y = x * lax.rsqrt(jnp.mean(x * x, axis=-1, keepdims=True) + eps)
    o_ref[...] = (y * g_ref[...]).astype(o_ref.dtype)


def rmsnorm(x, g, out_dtype, *, eps=NORM_EPS, tm=512):
    m, d = x.shape
    tm = min(tm, m)
    return pl.pallas_call(
        functools.partial(_rmsnorm_kernel, eps=eps),
        out_shape=jax.ShapeDtypeStruct((m, d), out_dtype),
        grid=(m // tm,),
        in_specs=[pl.BlockSpec((tm, d), lambda i: (i, 0)), pl.BlockSpec((1, d), lambda i: (0, 0))],
        out_specs=pl.BlockSpec((tm, d), lambda i: (i, 0)),
        compiler_params=_params("parallel"),
        name="rmsnorm",
    )(x, g.reshape(1, d))


def _mm_kernel(a_ref, w_ref, *rest, epilogue):
    o_ref = rest[-1]
    acc = _dot(a_ref[...], w_ref[...])
    if epilogue is not None:
        acc = epilogue(acc, *rest[:-1])
    o_ref[...] = acc.astype(o_ref.dtype)


def matmul(a, w, out_dtype, *, tm=1024, tn=1024, extra=(), extra_specs=(), epilogue=None, name="matmul"):
    m, k = a.shape
    n = w.shape[1]
    tm, tn = min(tm, m), min(tn, n)
    return pl.pallas_call(
        functools.partial(_mm_kernel, epilogue=epilogue),
        out_shape=jax.ShapeDtypeStruct((m, n), out_dtype),
        grid=(n // tn, m // tm),
        in_specs=[pl.BlockSpec((tm, k), lambda j, i: (i, 0)),
                  pl.BlockSpec((k, tn), lambda j, i: (0, j))] + list(extra_specs),
        out_specs=pl.BlockSpec((tm, tn), lambda j, i: (i, j)),
        compiler_params=_params("parallel", "parallel"),
        name=name,
    )(a, w, *extra)


def matmul_residual(a, w, res, *, tm=1024, tn=1024, name="matmul_residual"):
    tm, tn = min(tm, a.shape[0]), min(tn, w.shape[1])
    return matmul(a, w, F32, tm=tm, tn=tn, extra=(res,),
                  extra_specs=(pl.BlockSpec((tm, tn), lambda j, i: (i, j)),),
                  epilogue=lambda acc, r_ref: r_ref[...] + acc, name=name)


def _ple_kernel(h_ref, wg_ref, p_ref, wp_ref, x_ref, o_ref):
    gate = jax.nn.sigmoid(_dot(h_ref[...], wg_ref[...]))
    proj = _dot(p_ref[...].astype(BF16), wp_ref[...])
    o_ref[...] = x_ref[...] + gate * proj


def per_layer_embedding(x, hn, w_gate, p, w_proj, *, tm=1024, tn=1024):
    m, d = x.shape
    pd = p.shape[1]
    tm, tn = min(tm, m), min(tn, d)
    return pl.pallas_call(
        _ple_kernel,
        out_shape=jax.ShapeDtypeStruct((m, d), F32),
        grid=(d // tn, m // tm),
        in_specs=[pl.BlockSpec((tm, d), lambda j, i: (i, 0)),
                  pl.BlockSpec((d, tn), lambda j, i: (0, j)),
                  pl.BlockSpec((tm, pd), lambda j, i: (i, 0)),
                  pl.BlockSpec((pd, tn), lambda j, i: (0, j)),
                  pl.BlockSpec((tm, tn), lambda j, i: (i, j))],
        out_specs=pl.BlockSpec((tm, tn), lambda j, i: (i, j)),
        compiler_params=_params("parallel", "parallel"),
        name="per_layer_embedding",
    )(hn, w_gate, p, w_proj, x)


def _rope_table_kernel(pos_ref, freq_ref, cos_ref, sin_lo_ref, sin_hi_ref):
    half = ROT_DIM // 2
    ang = pos_ref[...].astype(F32) * freq_ref[...]
    lane = _iota(ang.shape, 1)
    c, s = jnp.cos(ang), jnp.sin(ang)
    cos_ref[...] = jnp.where(lane < ROT_DIM, c, 1.0)
    sin_lo_ref[...] = jnp.where(lane < half, -s, 0.0)
    sin_hi_ref[...] = jnp.where((lane >= half) & (lane < ROT_DIM), s, 0.0)


def rope_tables(positions, *, tm=1024):
    m = positions.size
    tm = min(tm, m)
    inv_freq = ROPE_THETA ** (-jnp.arange(0, ROT_DIM, 2, dtype=F32) / ROT_DIM)
    freq_row = jnp.concatenate([inv_freq, inv_freq, jnp.zeros((LANES - ROT_DIM,), F32)]).reshape(1, LANES)
    shape = jax.ShapeDtypeStruct((m, LANES), F32)
    spec = pl.BlockSpec((tm, LANES), lambda i: (i, 0))
    return pl.pallas_call(
        _rope_table_kernel,
        out_shape=(shape, shape, shape),
        grid=(m // tm,),
        in_specs=[pl.BlockSpec((tm, 1), lambda i: (i, 0)), pl.BlockSpec((1, LANES), lambda i: (0, 0))],
        out_specs=(spec, spec, spec),
        compiler_params=_params("parallel"),
        name="rope_tables",
    )(positions.reshape(m, 1), freq_row)


def _attn_in_kernel(a_ref, w_ref, cos_ref, sin_lo_ref, sin_hi_ref, o_ref, *, n_q_blocks):
    j = pl.program_id(0)
    acc = _dot(a_ref[...], w_ref[...])
    half = ROT_DIM // 2

    @pl.when(j < 2 * n_q_blocks)
    def _():
        scale = jnp.where(j < n_q_blocks, A_HEAD_DIM ** -0.5, 1.0).astype(F32)
        cos, sin_lo, sin_hi = cos_ref[...], sin_lo_ref[...], sin_hi_ref[...]
        for g in range(acc.shape[1] // LANES):
            x = acc[:, g * LANES:(g + 1) * LANES]
            y = x * cos + pltpu.roll(x, LANES - half, 1) * sin_lo + pltpu.roll(x, half, 1) * sin_hi
            o_ref[:, g * LANES:(g + 1) * LANES] = (y * scale).astype(o_ref.dtype)

    @pl.when(j >= 2 * n_q_blocks)
    def _():
        o_ref[...] = acc.astype(o_ref.dtype)


def attn_in_proj(hn, w_in, tables, qk_width, *, tm=1024, tn=1024):
    m, k = hn.shape
    n = w_in.shape[1]
    tm, tn = min(tm, m), min(tn, qk_width)
    tspec = pl.BlockSpec((tm, LANES), lambda j, i: (i, 0))
    return pl.pallas_call(
        functools.partial(_attn_in_kernel, n_q_blocks=qk_width // tn),
        out_shape=jax.ShapeDtypeStruct((m, n), BF16),
        grid=(n // tn, m // tm),
        in_specs=[pl.BlockSpec((tm, k), lambda j, i: (i, 0)),
                  pl.BlockSpec((k, tn), lambda j, i: (0, j)), tspec, tspec, tspec],
        out_specs=pl.BlockSpec((tm, tn), lambda j, i: (i, j)),
        compiler_params=_params("parallel", "parallel"),
        name="attn_in_proj",
    )(hn, w_in, *tables)


def _attn_vt_kernel(w_ref, a_ref, o_ref):
    o_ref[...] = _dot(w_ref[...], a_ref[...], NT).astype(o_ref.dtype)


def attn_v_proj_t(hn, w_t, batch, seq, blk, *, tn=1024):
    m, k = hn.shape
    n = w_t.shape[0]
    tn = min(tn, n)
    nk = seq // blk
    return pl.pallas_call(
        _attn_vt_kernel,
        out_shape=jax.ShapeDtypeStruct((batch, nk, n, blk), BF16),
        grid=(n // tn, m // blk),
        in_specs=[pl.BlockSpec((tn, k), lambda j, i: (j, 0)),
                  pl.BlockSpec((blk, k), lambda j, i: (i, 0))],
        out_specs=pl.BlockSpec((None, None, tn, blk), lambda j, i: (i // nk, i % nk, j, 0)),
        compiler_params=_params("parallel", "parallel"),
        name="attn_v_proj_t",
    )(w_t, hn)


def _diff_attn_kernel(lam_ref, q_ref, k_ref, vt_ref, z_ref, g_ref, o_ref, m_ref, l_ref, acc_ref, *, blk, lam_init):
    i = pl.program_id(2)
    lam = lam_ref[...]
    lam_full = (jnp.exp(jnp.sum(lam[0:1] * lam[1:2], axis=-1, keepdims=True))
                - jnp.exp(jnp.sum(lam[2:3] * lam[3:4], axis=-1, keepdims=True)) + lam_init)
    m_ref[...] = jnp.full(m_ref.shape, -jnp.inf, F32)
    l_ref[...] = jnp.zeros(l_ref.shape, F32)
    acc_ref[...] = jnp.zeros(acc_ref.shape, F32)
    q = q_ref[...]
    d = A_HEAD_DIM

    def step(j, masked):
        start = pl.multiple_of(j * blk, blk)
        kb = k_ref[pl.ds(start, blk), :]
        vt = vt_ref[j]
        for c in range(2):
            s = _dot(kb[:, c * d:(c + 1) * d], q[:, c * d:(c + 1) * d], NT)
            if masked:
                s = jnp.where(_iota(s.shape, 0) <= _iota(s.shape, 1), s, -jnp.inf)
            m_prev = m_ref[c]
            m_new = jnp.maximum(m_prev, jnp.max(s, axis=0, keepdims=True))
            alpha = jnp.exp(m_prev - m_new)
            p = jnp.exp(s - m_new)
            l_ref[c] = alpha * l_ref[c] + jnp.sum(p, axis=0, keepdims=True)
            acc_ref[c] = alpha * acc_ref[c] + _dot(vt, p.astype(BF16))
            m_ref[c] = m_new

    def body(jj, carry):
        step(2 * jj, False)
        step(2 * jj + 1, False)
        return carry

    lax.fori_loop(0, i // 2, body, 0)

    @pl.when(i % 2 == 1)
    def _():
        step(i - 1, False)

    step(i, True)
    o = acc_ref[0] / l_ref[0] - lam_full * (acc_ref[1] / l_ref[1])
    o = o * lax.rsqrt(jnp.mean(o * o, axis=0, keepdims=True) + SUBLN_EPS) * g_ref[...]
    o = (o * (1.0 - lam_init)).T
    o_ref[...] = (o * _silu(z_ref[...].astype(F32))).astype(o_ref.dtype)


def diff_attention_core(qkz, vt, lam, subln_g, batch, seq, heads, lam_init):
    m = qkz.shape[0]
    blk = vt.shape[-1]
    nq = seq // blk
    w = A_V_DIM
    return pl.pallas_call(
        functools.partial(_diff_attn_kernel, blk=blk, lam_init=lam_init),
        out_shape=jax.ShapeDtypeStruct((m, heads * w), BF16),
        grid=(batch, heads, nq),
        in_specs=[pl.BlockSpec((4, A_HEAD_DIM), lambda b, h, i: (0, 0)),
                  pl.BlockSpec((blk, w), lambda b, h, i: (b * nq + i, h)),
                  pl.BlockSpec((seq, w), lambda b, h, i: (b, heads + h)),
                  pl.BlockSpec((None, nq, w, blk), lambda b, h, i: (b, 0, h, 0)),
                  pl.BlockSpec((blk, w), lambda b, h, i: (b * nq + i, 2 * heads + h)),
                  pl.BlockSpec((w, 1), lambda b, h, i: (0, 0))],
        out_specs=pl.BlockSpec((blk, w), lambda b, h, i: (b * nq + i, h)),
        scratch_shapes=[pltpu.VMEM((2, 1, blk), F32), pltpu.VMEM((2, 1, blk), F32),
                        pltpu.VMEM((2, w, blk), F32)],
        compiler_params=_params("parallel", "parallel", "parallel"),
        name="diff_attention",
    )(lam, qkz, qkz, vt, qkz, subln_g.reshape(w, 1))


def diff_attention_layer(x, hn, tables, w_in, lam, subln_g, w_out, batch, seq, lam_init):
    d = x.shape[1]
    heads = d // A_V_DIM
    qk_w = heads * 2 * A_HEAD_DIM
    v_w = heads * A_V_DIM
    blk = min(ATTN_BLOCK, seq)
    w_qkz = jnp.concatenate([w_in[:, :2 * qk_w], w_in[:, 2 * qk_w + v_w:]], axis=1).astype(BF16)
    w_vt = w_in[:, 2 * qk_w:2 * qk_w + v_w].T.astype(BF16)
    qkz = attn_in_proj(hn, w_qkz, tables, qk_w)
    vt = attn_v_proj_t(hn, w_vt, batch, seq, blk)
    o = diff_attention_core(qkz, vt, lam, subln_g, batch, seq, heads, lam_init)
    return matmul_residual(o, w_out.astype(BF16), x, name="attn_out_proj")


def _gdn_in_kernel(a_ref, w_ref, cw_ref, o_ref, tail_ref, *, n_conv_blocks, rows_per_seq):
    j = pl.program_id(0)
    i = pl.program_id(1)
    acc = _dot(a_ref[...], w_ref[...])
    tm = acc.shape[0]

    @pl.when(j < n_conv_blocks)
    def _():
        @pl.when((i * tm) % rows_per_seq == 0)
        def _():
            tail_ref[...] = jnp.zeros(tail_ref.shape, F32)

        tail = tail_ref[...]
        sub = _iota(tail.shape, 0)

        def shifted(s):
            xs = pltpu.roll(acc, s, 0)
            head = jnp.where(sub < s, pltpu.roll(tail, s, 0), xs[:8])
            return jnp.concatenate([head, xs[8:]], axis=0)

        cw = cw_ref[...]
        last = C_CONV_WIDTH - 1
        y = shifted(last) * cw[0:1]
        for t in range(1, last):
            y = y + shifted(last - t) * cw[t:t + 1]
        y = y + acc * cw[last:last + 1]
        tail_ref[...] = acc[tm - 8:]
        o_ref[...] = _silu(y).astype(o_ref.dtype)

    @pl.when(j >= n_conv_blocks)
    def _():
        o_ref[...] = acc.astype(o_ref.dtype)


def gdn_in_proj(hn, w, conv_w, seq, *, tm=1024, tn=1024):
    m, k = hn.shape
    n = w.shape[1]
    conv_ch = conv_w.shape[1]
    tm, tn = min(tm, seq), min(tn, n)
    ncb = conv_ch // tn
    return pl.pallas_call(
        functools.partial(_gdn_in_kernel, n_conv_blocks=ncb, rows_per_seq=seq),
        out_shape=jax.ShapeDtypeStruct((m, n), BF16),
        grid=(n // tn, m // tm),
        in_specs=[pl.BlockSpec((tm, k), lambda j, i: (i, 0)),
                  pl.BlockSpec((k, tn), lambda j, i: (0, j)),
                  pl.BlockSpec((C_CONV_WIDTH, tn), lambda j, i: (0, jnp.minimum(j, ncb - 1)))],
        out_specs=pl.BlockSpec((tm, tn), lambda j, i: (i, j)),
        scratch_shapes=[pltpu.VMEM((8, tn), F32)],
        compiler_params=_params("arbitrary", "arbitrary"),
        name="gdn_in_proj",
    )(hn, w, conv_w)


def _pair_masks():
    n = 2 * CHUNK
    r, c = _iota((n, n), 0), _iota((n, n), 1)
    same = (r // CHUNK) == (c // CHUNK)
    return same & (r >= c), same & (r > c), r == c


def _dot_x2l(a, b_exact, dims=NN):
    h, l = _split2(a)
    return _dot(h, b_exact, dims) + _dot(l, b_exact, dims)


def _dot_x2r(a_exact, b, dims=NN):
    h, l = _split2(b)
    return _dot(a_exact, h, dims) + _dot(a_exact, l, dims)


def _inverse_stages(p_ref, t_ref, count, nilpotency):
    for _ in range(int(math.log2(nilpotency)) - 1):
        for i in range(count):
            p_ref[i] = _dot(p_ref[i], p_ref[i]).astype(BF16)
        for i in range(count):
            t = t_ref[i]
            t_ref[i] = t + _dot(t.astype(BF16), p_ref[i])


def _gdn_prep_kernel(q_ref, k_ref, v_ref, b_ref, a_ref, alog_ref, dtb_ref,
                     w_ref, qg_ref, u_ref, att_ref, kdt_ref, dec_ref,
                     g_s, dec_s, kn_s, kb_s, rhs_s, p_s, t_s):
    h = pl.program_id(1)
    c = CHUNK
    n2 = 2 * c
    dk = C_HEAD_DIM
    npairs = q_ref.shape[0] // n2
    tri, strict, diag = _pair_masks()
    tri_b = tri.astype(BF16)
    eye_b = diag.astype(BF16)
    eye_f = diag.astype(F32)
    sel = (_iota((LANES, dk), 0) == h).astype(BF16)
    first = (_iota((n2, dk), 1) == 0).astype(BF16)
    upper_half = _iota((n2, dk), 0) < c
    g_all = -jnp.exp(alog_ref[...]) * _softplus(a_ref[...] + dtb_ref[...])
    pairs = [slice(n * n2, (n + 1) * n2) for n in range(npairs)]

    for n, rows in enumerate(pairs):
        g_s[n] = _dot_x2l(g_all[rows], sel)
    for n, rows in enumerate(pairs):
        g_s[n] = _dot_x2r(tri_b, g_s[n])
    for n, rows in enumerate(pairs):
        gc = g_s[n]
        gc_row = _dot_x2r(first, gc, NT)
        dec_s[n] = jnp.where(tri, jnp.exp(jnp.where(tri, gc - gc_row, 0.0)), 0.0)
    for n, rows in enumerate(pairs):
        gc = g_s[n]
        qf = q_ref[rows, :].astype(F32)
        kf = k_ref[rows, :].astype(F32)
        vf = v_ref[rows, :].astype(F32)
        qn = qf * lax.rsqrt(jnp.sum(qf * qf, axis=-1, keepdims=True) + 1e-6) * (dk ** -0.5)
        kn = kf * lax.rsqrt(jnp.sum(kf * kf, axis=-1, keepdims=True) + 1e-6)
        beta = _dot(jax.nn.sigmoid(b_ref[rows, :]).astype(BF16), sel)
        egc = jnp.exp(gc)
        kb = kn * beta
        kn_b = kn.astype(BF16)
        qn_b = qn.astype(BF16)
        kn_s[n] = kn_b
        kb_s[n] = kb.astype(BF16)
        rhs_s[n] = jnp.concatenate([vf * beta, kb * egc], axis=1).astype(BF16)
        g_last = jnp.where(upper_half, gc[c - 1:c, :], gc[n2 - 1:n2, :])
        kdec = kn * jnp.exp(g_last - gc)
        qg_ref[rows, :] = (qn * egc).astype(BF16)
        att_ref[rows, :] = (_dot(qn_b, kn_b, NT) * dec_s[n]).astype(BF16)
        kdt_ref[:, rows] = _dot(eye_b, kdec.astype(BF16), NT).astype(BF16)
        dec_ref[2 * n:2 * n + 1, :] = jnp.exp(gc[c - 1:c, :])
        dec_ref[2 * n + 1:2 * n + 2, :] = jnp.exp(gc[n2 - 1:n2, :])
    for n, rows in enumerate(pairs):
        neg_a = jnp.where(strict, -(_dot(kb_s[n], kn_s[n], NT) * dec_s[n]), 0.0)
        p_s[n] = neg_a.astype(BF16)
        t_s[n] = eye_f + neg_a
    _inverse_stages(p_s, t_s, npairs, c)
    for n, rows in enumerate(pairs):
        sol = _dot(t_s[n].astype(BF16), rhs_s[n])
        u_ref[rows, :] = sol[:, :dk]
        w_ref[rows, :] = sol[:, dk:].astype(BF16)


def _gdn_scan_kernel(w_ref, qg_ref, u_ref, att_ref, kdt_ref, dec_ref, z_ref, g_ref, o_ref,
                     s_ref, ms_s, vp_s, *, group):
    c = CHUNK
    dk = C_HEAD_DIM
    zeros = jnp.zeros((c, dk), BF16)

    @pl.when(pl.program_id(2) == 0)
    def _():
        s_ref[...] = jnp.zeros(s_ref.shape, F32)

    for n in range(w_ref.shape[1] // c):
        rows = slice(n * c, (n + 1) * c)
        pair = slice((n // 2) * 2 * c, (n // 2 + 1) * 2 * c)
        for gi in range(group):
            lhs = jnp.concatenate([w_ref[gi, rows, :], qg_ref[gi, rows, :]], axis=0)
            ms_s[gi] = _dot(lhs, s_ref[gi].astype(BF16))
        for gi in range(group):
            v_new = (u_ref[gi, rows, :] - ms_s[gi, :c, :]).astype(BF16)
            vp_s[gi] = jnp.concatenate([v_new, zeros] if n % 2 == 0 else [zeros, v_new], axis=0)
        for gi in range(group):
            s_ref[gi] = s_ref[gi] * dec_ref[gi, n:n + 1, :] + _dot(kdt_ref[gi, :, pair], vp_s[gi])
        for gi in range(group):
            o = ms_s[gi, c:, :] + _dot(att_ref[gi, rows, :], vp_s[gi])
            o = o * lax.rsqrt(jnp.mean(o * o, axis=-1, keepdims=True) + NORM_EPS) * g_ref[...]
            z = z_ref[rows, gi * dk:(gi + 1) * dk].astype(F32)
            o_ref[rows, gi * dk:(gi + 1) * dk] = (o * _silu(z)).astype(o_ref.dtype)


def gated_deltanet_layer(x, hn, w_in, conv_w, a_log, dt_bias, norm_g, w_out, batch, seq):
    m, d = x.shape
    dk = C_HEAD_DIM
    k_heads = d // dk
    v_heads = 2 * k_heads
    conv_ch = 2 * k_heads * dk + v_heads * dk
    main_w = conv_ch + v_heads * dk
    qkvz = gdn_in_proj(hn, w_in[:, :main_w].astype(BF16), conv_w, seq)
    pad = jnp.zeros((d, LANES - v_heads), F32)
    w_ba = jnp.concatenate([w_in[:, main_w:main_w + v_heads], pad, w_in[:, main_w + v_heads:], pad], axis=1)
    ba = matmul(hn, w_ba.astype(BF16), F32, tn=2 * LANES, name="gdn_ba_proj")
    row_pad = jnp.zeros((LANES - v_heads,), F32)
    alog_row = jnp.concatenate([a_log, row_pad]).reshape(1, LANES)
    dtb_row = jnp.concatenate([dt_bias, row_pad]).reshape(1, LANES)

    rows = min(SCAN_ROWS, seq)
    nr = seq // rows
    ncr = rows // CHUNK
    npair = rows // (2 * CHUNK)
    hv = v_heads
    bh_t = lambda dt, last: jax.ShapeDtypeStruct((batch, hv, seq, last), dt)
    blk4 = lambda last: pl.BlockSpec((None, None, rows, last), lambda b, h, i: (b, h, i, 0))
    qoff, koff, voff = 0, k_heads, 2 * k_heads
    w_, qg_, u_, att_, kdt_, dec_ = pl.pallas_call(
        _gdn_prep_kernel,
        out_shape=(bh_t(BF16, dk), bh_t(BF16, dk), bh_t(F32, dk), bh_t(BF16, 2 * CHUNK),
                   jax.ShapeDtypeStruct((batch, hv, dk, seq), BF16),
                   jax.ShapeDtypeStruct((batch, hv, seq // CHUNK, dk), F32)),
        grid=(batch, hv, nr),
        in_specs=[pl.BlockSpec((rows, dk), lambda b, h, i: (b * nr + i, qoff + h // 2)),
                  pl.BlockSpec((rows, dk), lambda b, h, i: (b * nr + i, koff + h // 2)),
                  pl.BlockSpec((rows, dk), lambda b, h, i: (b * nr + i, voff + h)),
                  pl.BlockSpec((rows, LANES), lambda b, h, i: (b * nr + i, 0)),
                  pl.BlockSpec((rows, LANES), lambda b, h, i: (b * nr + i, 1)),
                  pl.BlockSpec((1, LANES), lambda b, h, i: (0, 0)),
                  pl.BlockSpec((1, LANES), lambda b, h, i: (0, 0))],
        out_specs=(blk4(dk), blk4(dk), blk4(dk), blk4(2 * CHUNK),
                   pl.BlockSpec((None, None, dk, rows), lambda b, h, i: (b, h, 0, i)),
                   pl.BlockSpec((None, None, ncr, dk), lambda b, h, i: (b, h, i, 0))),
        scratch_shapes=[pltpu.VMEM((npair, dk, dk), F32), pltpu.VMEM((npair, dk, dk), F32),
                        pltpu.VMEM((npair, dk, dk), BF16), pltpu.VMEM((npair, dk, dk), BF16),
                        pltpu.VMEM((npair, dk, 2 * dk), BF16), pltpu.VMEM((npair, dk, dk), BF16),
                        pltpu.VMEM((npair, dk, dk), F32)],
        compiler_params=_params("parallel", "parallel", "parallel"),
        name="gdn_prep",
    )(qkvz, qkvz, qkvz, ba, ba, alog_row, dtb_row)

    group = 8
    zoff = conv_ch // (group * dk)
    gblk = lambda last: pl.BlockSpec((None, group, rows, last), lambda b, h, i: (b, h, i, 0))
    o = pl.pallas_call(
        functools.partial(_gdn_scan_kernel, group=group),
        out_shape=jax.ShapeDtypeStruct((m, hv * dk), BF16),
        grid=(batch, hv // group, nr),
        in_specs=[gblk(dk), gblk(dk), gblk(dk), gblk(2 * CHUNK),
                  pl.BlockSpec((None, group, dk, rows), lambda b, h, i: (b, h, 0, i)),
                  pl.BlockSpec((None, group, ncr, dk), lambda b, h, i: (b, h, i, 0)),
                  pl.BlockSpec((rows, group * dk), lambda b, h, i: (b * nr + i, zoff + h)),
                  pl.BlockSpec((1, dk), lambda b, h, i: (0, 0))],
        out_specs=pl.BlockSpec((rows, group * dk), lambda b, h, i: (b * nr + i, h)),
        scratch_shapes=[pltpu.VMEM((group, dk, dk), F32), pltpu.VMEM((group, 2 * CHUNK, dk), F32),
                        pltpu.VMEM((group, 2 * CHUNK, dk), BF16)],
        compiler_params=_params("parallel", "parallel", "arbitrary"),
        name="gdn_scan",
    )(w_, qg_, u_, att_, kdt_, dec_, qkvz, norm_g.reshape(1, dk))
    return matmul_residual(o, w_out.astype(BF16), x, tm=512, name="gdn_out_proj")


def _rwkv_mix_kernel(x_ref, g_ref, mu_ref, o_ref, tail_ref, *, rows_per_seq):
    i = pl.program_id(0)
    x = x_ref[...]
    tm = x.shape[0]
    hn = x * lax.rsqrt(jnp.mean(x * x, axis=-1, keepdims=True) + NORM_EPS) * g_ref[...]

    @pl.when((i * tm) % rows_per_seq == 0)
    def _():
        tail_ref[...] = jnp.zeros(tail_ref.shape, F32)

    prev = jnp.where(_iota(hn.shape, 0) == 0, tail_ref[7:8, :], pltpu.roll(hn, 1, 0))
    tail_ref[...] = hn[tm - 8:]
    xx = prev - hn
    for c in range(o_ref.shape[0]):
        o_ref[c] = (hn + xx * mu_ref[c:c + 1, :]).astype(o_ref.dtype)


def rwkv_token_mix(x, norm_g, mu, seq, *, tm=256):
    m, d = x.shape
    tm = min(tm, seq)
    nmix = mu.shape[0]
    return pl.pallas_call(
        functools.partial(_rwkv_mix_kernel, rows_per_seq=seq),
        out_shape=jax.ShapeDtypeStruct((nmix, m, d), BF16),
        grid=(m // tm,),
        in_specs=[pl.BlockSpec((tm, d), lambda i: (i, 0)),
                  pl.BlockSpec((1, d), lambda i: (0, 0)),
                  pl.BlockSpec((nmix, d), lambda i: (0, 0))],
        out_specs=pl.BlockSpec((nmix, tm, d), lambda i: (0, i, 0)),
        scratch_shapes=[pltpu.VMEM((8, d), F32)],
        compiler_params=_params("arbitrary"),
        name="rwkv_token_mix",
    )(x, norm_g.reshape(1, d), mu)


def _grouped_mm_kernel(a_ref, w_ref, o_ref):
    o_ref[...] = _dot(a_ref[...], w_ref[...]).astype(o_ref.dtype)


def grouped_matmul(a, w, out_dtype, *, tm=1024, tn=1024):
    g, k, n = w.shape
    m = a.shape[1]
    tm, tn = min(tm, m), min(tn, n)
    return pl.pallas_call(
        _grouped_mm_kernel,
        out_shape=jax.ShapeDtypeStruct((g, m, n), out_dtype),
        grid=(g, n // tn, m // tm),
        in_specs=[pl.BlockSpec((None, tm, k), lambda c, j, i: (c, i, 0)),
                  pl.BlockSpec((None, k, tn), lambda c, j, i: (c, 0, j))],
        out_specs=pl.BlockSpec((None, tm, tn), lambda c, j, i: (c, i, j)),
        compiler_params=_params("parallel", "parallel", "parallel"),
        name="rwkv_rkvg_proj",
    )(a, w)


def _rwkv_lora_kernel(xw_ref, xa_ref, w1_ref, w2_ref, a1_ref, a2_ref, w0_ref, a0_ref, ld_ref, a_ref):
    hw = jnp.tanh(_dot(xw_ref[...], w1_ref[...])).astype(BF16)
    lw = w0_ref[...] + _dot(hw, w2_ref[...])
    log_w = -_softplus(-lw) - 0.5
    ld_ref[...] = -jnp.exp(log_w)
    ha = _dot(xa_ref[...], a1_ref[...]).astype(BF16)
    a_ref[...] = jax.nn.sigmoid(a0_ref[...] + _dot(ha, a2_ref[...]))


def rwkv_lora(xs, w1, w2, a1, a2, w0, a0, *, tm=512):
    _, m, d = xs.shape
    tm = min(tm, m)
    r = w1.shape[1]
    row = pl.BlockSpec((1, d), lambda i: (0, 0))
    shape = jax.ShapeDtypeStruct((m, d), F32)
    return pl.pallas_call(
        _rwkv_lora_kernel,
        out_shape=(shape, shape),
        grid=(m // tm,),
        in_specs=[pl.BlockSpec((None, tm, d), lambda i: (4, i, 0)),
                  pl.BlockSpec((None, tm, d), lambda i: (5, i, 0)),
                  pl.BlockSpec((d, r), lambda i: (0, 0)), pl.BlockSpec((r, d), lambda i: (0, 0)),
                  pl.BlockSpec((d, r), lambda i: (0, 0)), pl.BlockSpec((r, d), lambda i: (0, 0)),
                  row, row],
        out_specs=(pl.BlockSpec((tm, d), lambda i: (i, 0)), pl.BlockSpec((tm, d), lambda i: (i, 0))),
        compiler_params=_params("parallel"),
        name="rwkv_lora",
    )(xs, xs, w1, w2, a1, a2, w0.reshape(1, d), a0.reshape(1, d))


def _rwkv_prep_kernel(r_ref, k_ref, v_ref, ld_ref, a_ref, kk_ref, ka_ref, rk_ref,
                      wt_ref, rt_ref, u_ref, aro_ref, kbt_ref, vb_ref, dec_ref, bonus_ref,
                      kkt_s, kbar_s, bbar_s, akk_s, p_s, t_s, drow_s):
    c = CHUNK
    n2 = 2 * c
    n_h = B_HEAD_DIM
    npairs = r_ref.shape[0] // n2
    lane_head = _iota((1, LANES), 1) // n_h
    seg = (_iota((LANES, LANES), 0) // n_h == _iota((LANES, LANES), 1) // n_h).astype(BF16)
    tri, strict, diag = _pair_masks()
    tri_b = tri.astype(BF16)
    eye_b = diag.astype(BF16)
    eye_f = diag.astype(F32)
    upper_half = _iota((n2, LANES), 0) < c
    pairs = [slice(n * n2, (n + 1) * n2) for n in range(npairs)]

    r = r_ref[...]
    k = k_ref[...]
    v = v_ref[...]
    a = a_ref[...]
    kk_raw = k * kk_ref[...]
    kk = kk_raw * lax.rsqrt(_dot_xl(kk_raw * kk_raw, seg) + 1e-6)
    k2 = k * (1.0 + (a - 1.0) * ka_ref[...])
    bb = kk * a
    bonus_ref[...] = _dot_xl(r * k2 * rk_ref[...], seg) * v
    vb_ref[...] = v.astype(BF16)
    for n, rows in enumerate(pairs):
        ld = ld_ref[rows, :]
        cs = _dot_xr(tri_b, ld)
        cs_last = jnp.where(upper_half, cs[c - 1:c, :], cs[n2 - 1:n2, :])
        w_inv = jnp.exp(-cs)
        w_end = jnp.exp(cs_last - cs)
        rt_ref[rows, :] = (r[rows] * jnp.exp(cs)).astype(BF16)
        kkt_s[n] = (kk[rows] * jnp.exp(cs - ld)).astype(BF16)
        k2c, bbc = k2[rows], bb[rows]
        kbar_s[n] = (k2c * w_inv).astype(BF16)
        bbar_s[n] = (bbc * w_inv).astype(BF16)
        kw, bw = (k2c * w_end).astype(BF16), (-(bbc * w_end)).astype(BF16)
        for half in range(2):
            hs = slice(half * c, (half + 1) * c)
            kb_end = jnp.concatenate([kw[hs], bw[hs]], axis=0)
            cols = slice((2 * n + half) * n2, (2 * n + half + 1) * n2)
            kbt_ref[:, cols] = _dot(eye_b, kb_end, NT).astype(BF16)
        drow_s[2 * n:2 * n + 1, :] = jnp.exp(cs[c - 1:c, :])
        drow_s[2 * n + 1:2 * n + 2, :] = jnp.exp(cs[n2 - 1:n2, :])
    drow_s[2 * npairs:, :] = jnp.zeros((LANES - 2 * npairs, LANES), F32)
    dec_ref[...] = _dot_xr(eye_b, drow_s[...], NT)
    zero_b = jnp.zeros((n2, LANES), BF16)
    for n, rows in enumerate(pairs):
        for hh in range(2):
            own = lane_head == hh
            i = 2 * n + hh
            kkt_m = jnp.where(own, kkt_s[n], zero_b)
            rt_m = jnp.where(own, rt_ref[rows, :], zero_b)
            neg_a = jnp.where(strict, -_dot(kkt_m, bbar_s[n], NT), 0.0)
            p_s[i] = neg_a.astype(BF16)
            t_s[i] = eye_f + neg_a
            akk_s[i] = jnp.where(strict, _dot(kkt_m, kbar_s[n], NT), 0.0).astype(BF16)
            a_rk = jnp.where(tri, _dot(rt_m, kbar_s[n], NT), 0.0)
            a_rb = jnp.where(tri, _dot(rt_m, bbar_s[n], NT), 0.0)
            aro_ref[hh, rows, :] = jnp.concatenate([a_rk, -a_rb], axis=1).astype(BF16)
    _inverse_stages(p_s, t_s, 2 * npairs, c)
    for n, rows in enumerate(pairs):
        for hh in range(2):
            i = 2 * n + hh
            p_s[i] = _dot(akk_s[i], vb_ref[rows, :]).astype(BF16)
    for n, rows in enumerate(pairs):
        t0, t1 = t_s[2 * n].astype(BF16), t_s[2 * n + 1].astype(BF16)
        own0 = lane_head == 0
        wt_ref[rows, :] = jnp.where(own0, _dot(t0, kkt_s[n]), _dot(t1, kkt_s[n])).astype(BF16)
        u_ref[rows, :] = jnp.where(own0, _dot(t0, p_s[2 * n]), _dot(t1, p_s[2 * n + 1]))


def _rwkv_scan_kernel(wt_ref, rt_ref, u_ref, aro_ref, kbt_ref, vb_ref, dec_ref, bonus_ref, gate_ref,
                      lnw_ref, lnb_ref, o_ref, s_ref, ms_s, xc_s, xp_s, *, group):
    c = CHUNK
    n2 = 2 * c
    n_h = B_HEAD_DIM
    first_head = _iota((1, LANES), 1) < n_h
    same_head = _iota((LANES, LANES), 0) // n_h == _iota((LANES, LANES), 1) // n_h
    zeros = jnp.zeros((c, LANES), BF16)

    def head_mean(x):
        tot = jnp.sum(x, axis=-1, keepdims=True)
        lo = jnp.sum(jnp.where(first_head, x, 0.0), axis=-1, keepdims=True)
        return jnp.where(first_head, lo, tot - lo) * (1.0 / n_h)

    @pl.when(pl.program_id(2) == 0)
    def _():
        s_ref[...] = jnp.zeros(s_ref.shape, F32)

    for n in range(u_ref.shape[1] // c):
        rows = slice(n * c, (n + 1) * c)
        for gi in range(group):
            lhs = jnp.concatenate([wt_ref[gi, rows, :], rt_ref[gi, rows, :]], axis=0)
            ms_s[gi] = _dot(lhs, s_ref[gi].astype(BF16))
        for gi in range(group):
            sa = (u_ref[gi, rows, :] + ms_s[gi, :c, :]).astype(BF16)
            vb = vb_ref[gi, rows, :]
            xc_s[gi] = jnp.concatenate([vb, sa], axis=0)
            xp_s[gi] = jnp.concatenate([vb, zeros, sa, zeros] if n % 2 == 0 else [zeros, vb, zeros, sa], axis=0)
        for gi in range(group):
            upd = _dot(kbt_ref[gi, :, n * n2:(n + 1) * n2], xc_s[gi])
            dec_col = jnp.broadcast_to(dec_ref[gi, :, n:n + 1], (LANES, LANES))
            s_ref[gi] = s_ref[gi] * dec_col + jnp.where(same_head, upd, 0.0)
        for gi in range(group):
            both = _dot(jnp.concatenate([aro_ref[gi, 0, rows, :], aro_ref[gi, 1, rows, :]], axis=0), xp_s[gi])
            o = ms_s[gi, c:, :] + jnp.where(first_head, both[:c], both[c:])
            mean = head_mean(o)
            dlt = o - mean
            var = head_mean(dlt * dlt)
            cols = slice(gi * LANES, (gi + 1) * LANES)
            y = dlt * lax.rsqrt(var + B_GN_EPS) * lnw_ref[:, cols] + lnb_ref[:, cols]
            y = y + bonus_ref[rows, cols]
            o_ref[rows, cols] = (y * _silu(gate_ref[rows, cols])).astype(o_ref.dtype)


def rwkv7_layer(x, norm_g, mu, w_rkvg, w0, w_w1, w_w2, a0, w_a1, w_a2, k_k, k_a, r_k, ln_w, ln_b, w_out,
                batch, seq):
    m, d = x.shape
    heads = d // B_HEAD_DIM
    pairs = heads // 2
    order = jnp.array([0, 2, 3, 5, 1, 4])
    xs = rwkv_token_mix(x, norm_g, mu[order], seq)
    rkvg = grouped_matmul(xs, w_rkvg.astype(BF16), F32)
    lora = w_w1.shape[1]
    padc = lambda w: jnp.pad(w, ((0, 0), (0, LORA_PAD - lora))).astype(BF16)
    padr = lambda w: jnp.pad(w, ((0, LORA_PAD - lora), (0, 0))).astype(BF16)
    ld, a = rwkv_lora(xs, padc(w_w1), padr(w_w2), padc(w_a1), padr(w_a2), w0, a0)

    rows = min(SCAN_ROWS, seq)
    nr = seq // rows
    ncr = rows // CHUNK
    npair = rows // (2 * CHUNK)
    col = lambda g: pl.BlockSpec((None, rows, LANES), lambda b, p, i, g=g: (g, b * nr + i, p))
    flat = pl.BlockSpec((rows, LANES), lambda b, p, i: (b * nr + i, p))
    prow = pl.BlockSpec((1, LANES), lambda b, p, i: (0, p))
    bp = lambda rws, last, dt: jax.ShapeDtypeStruct((batch, pairs, rws, last), dt)
    pblk = lambda rws, last: pl.BlockSpec((None, None, rws, last), lambda b, p, i: (b, p, i, 0))
    wt_, rt_, u_, aro_, kbt_, vb_, dec_, bonus = pl.pallas_call(
        _rwkv_prep_kernel,
        out_shape=(bp(seq, LANES, BF16), bp(seq, LANES, BF16), bp(seq, LANES, F32),
                   jax.ShapeDtypeStruct((batch, pairs, 2, seq, 4 * CHUNK), BF16),
                   bp(LANES, 2 * seq, BF16), bp(seq, LANES, BF16), bp(nr * LANES, LANES, F32),
                   jax.ShapeDtypeStruct((m, d), F32)),
        grid=(batch, pairs, nr),
        in_specs=[col(0), col(1), col(2), flat, flat, prow, prow, prow],
        out_specs=(pblk(rows, LANES), pblk(rows, LANES), pblk(rows, LANES),
                   pl.BlockSpec((None, None, 2, rows, 4 * CHUNK), lambda b, p, i: (b, p, 0, i, 0)),
                   pl.BlockSpec((None, None, LANES, 2 * rows), lambda b, p, i: (b, p, 0, i)),
                   pblk(rows, LANES), pblk(LANES, LANES), flat),
        scratch_shapes=[pltpu.VMEM((npair, LANES, LANES), BF16), pltpu.VMEM((npair, LANES, LANES), BF16),
                        pltpu.VMEM((npair, LANES, LANES), BF16), pltpu.VMEM((2 * npair, LANES, LANES), BF16),
                        pltpu.VMEM((2 * npair, LANES, LANES), BF16), pltpu.VMEM((2 * npair, LANES, LANES), F32),
                        pltpu.VMEM((LANES, LANES), F32)],
        compiler_params=_params("parallel", "parallel", "parallel"),
        name="rwkv_prep",
    )(rkvg, rkvg, rkvg, ld, a, k_k.reshape(1, d), k_a.reshape(1, d), r_k.reshape(1, d))

    group = 8
    gw = group * LANES
    gblk = lambda rws, last: pl.BlockSpec((None, group, rws, last), lambda b, p, i: (b, p, i, 0))
    gflat = pl.BlockSpec((rows, gw), lambda b, p, i: (b * nr + i, p))
    grow = pl.BlockSpec((1, gw), lambda b, p, i: (0, p))
    o = pl.pallas_call(
        functools.partial(_rwkv_scan_kernel, group=group),
        out_shape=jax.ShapeDtypeStruct((m, d), BF16),
        grid=(batch, pairs // group, nr),
        in_specs=[gblk(rows, LANES), gblk(rows, LANES), gblk(rows, LANES),
                  pl.BlockSpec((None, group, 2, rows, 4 * CHUNK), lambda b, p, i: (b, p, 0, i, 0)),
                  pl.BlockSpec((None, group, LANES, 2 * rows), lambda b, p, i: (b, p, 0, i)),
                  gblk(rows, LANES), gblk(LANES, LANES), gflat,
                  pl.BlockSpec((None, rows, gw), lambda b, p, i: (3, b * nr + i, p)),
                  grow, grow],
        out_specs=gflat,
        scratch_shapes=[pltpu.VMEM((group, LANES, LANES), F32), pltpu.VMEM((group, 2 * CHUNK, LANES), F32),
                        pltpu.VMEM((group, 2 * CHUNK, LANES), BF16), pltpu.VMEM((group, 4 * CHUNK, LANES), BF16)],
        compiler_params=_params("parallel", "parallel", "arbitrary"),
        name="rwkv_scan",
    )(wt_, rt_, u_, aro_, kbt_, vb_, dec_, bonus, rkvg, ln_w.reshape(1, d), ln_b.reshape(1, d))
    return matmul_residual(o, w_out.astype(BF16), x, name="rwkv_out_proj")


def kernel(x, p, positions, norm_g, pe_norm_g, pe_w_gate, pe_w_proj, final_norm_g, a_w_in, a_lam, a_subln_g, a_w_out, b_mu, b_w_rkvg, b_w0, b_w_w1, b_w_w2, b_a0, b_w_a1, b_w_a2, b_k_k, b_k_a, b_r_k, b_ln_w, b_ln_b, b_w_out, c_w_in, c_conv_w, c_A_log, c_dt_bias, c_norm_g, c_w_out):
    batch, seq, d = x.shape
    depth = p.shape[0]
    m = batch * seq
    xf = x.reshape(m, d)
    tables = rope_tables(positions)
    for i in range(depth):
        kind = i % N_MIXERS
        j = i // N_MIXERS
        if kind == 0:
            lam_init = 0.8 - 0.6 * math.exp(-0.3 * i)
            hn = rmsnorm(xf, norm_g[i], BF16)
            xf = diff_attention_layer(xf, hn, tables, a_w_in[j], a_lam[j], a_subln_g[j], a_w_out[j],
                                      batch, seq, lam_init)
        elif kind == 1:
            xf = rwkv7_layer(xf, norm_g[i], b_mu[j], b_w_rkvg[j], b_w0[j], b_w_w1[j], b_w_w2[j], b_a0[j],
                             b_w_a1[j], b_w_a2[j], b_k_k[j], b_k_a[j], b_r_k[j], b_ln_w[j], b_ln_b[j],
                             b_w_out[j], batch, seq)
        else:
            hn = rmsnorm(xf, norm_g[i], BF16)
            xf = gated_deltanet_layer(xf, hn, c_w_in[j], c_conv_w[j], c_A_log[j], c_dt_bias[j], c_norm_g[j],
                                      c_w_out[j], batch, seq)
        hn2 = rmsnorm(xf, pe_norm_g[i], BF16)
        xf = per_layer_embedding(xf, hn2, pe_w_gate[i].astype(BF16), p[i].reshape(m, -1),
                                 pe_w_proj[i].astype(BF16))
    return rmsnorm(xf, final_norm_g, F32).reshape(batch, seq, d)
```

```python
import functools
import math

import jax
import jax.numpy as jnp
from jax import lax
from jax.experimental import pallas as pl
from jax.experimental.pallas import tpu as pltpu

F32 = jnp.float32
BF16 = jnp.bfloat16

N_MIXERS = 3
NORM_EPS = 1e-6
LANES = 128
VMEM_LIMIT = 48 * 1024 * 1024

A_HEAD_DIM = 128
A_V_DIM = 2 * A_HEAD_DIM
ROT_DIM = A_HEAD_DIM // 4
ROPE_THETA = 500000.0
SUBLN_EPS = 1e-5
ATTN_BLOCK = 256

B_HEAD_DIM = 64
B_GN_EPS = 64e-5
LORA_PAD = 128

C_HEAD_DIM = 128
C_CONV_WIDTH = 4
CHUNK = 64
SCAN_ROWS = 512
PREP_ROWS = 1024

NN = (((1,), (0,)), ((), ()))
NT = (((1,), (1,)), ((), ()))


def _dot(a, b, dims=NN):
    return lax.dot_general(a, b, dims, preferred_element_type=F32)


def _split2(x):
    hi = x.astype(BF16)
    lo = (x - hi.astype(F32)).astype(BF16)
    return hi, lo


def _split3(x):
    hi = x.astype(BF16)
    r = x - hi.astype(F32)
    mid = r.astype(BF16)
    lo = (r - mid.astype(F32)).astype(BF16)
    return hi, mid, lo


def _dot3(a, b, dims=NN):
    ah, al = _split2(a)
    bh, bl = _split2(b)
    return _dot(ah, bh, dims) + (_dot(ah, bl, dims) + _dot(al, bh, dims))


def _dot_xl(a, b_exact, dims=NN):
    h, m, l = _split3(a)
    return _dot(h, b_exact, dims) + (_dot(m, b_exact, dims) + _dot(l, b_exact, dims))


def _dot_xr(a_exact, b, dims=NN):
    h, m, l = _split3(b)
    return _dot(a_exact, h, dims) + (_dot(a_exact, m, dims) + _dot(a_exact, l, dims))


def _iota(shape, dim):
    return lax.broadcasted_iota(jnp.int32, shape, dim)


def _silu(x):
    return x * jax.nn.sigmoid(x)


def _softplus(x):
    return jnp.maximum(x, 0.0) + jnp.log(1.0 + jnp.exp(-jnp.abs(x)))


def _params(*sem):
    return pltpu.CompilerParams(dimension_semantics=sem, vmem_limit_bytes=VMEM_LIMIT)


def _inv_unit_lower(a, nilpotency):
    n = a.shape[0]
    eye = (_iota((n, n), 0) == _iota((n, n), 1)).astype(F32)
    p = -a
    t = eye + p
    for _ in range(int(math.log2(nilpotency)) - 1):
        p = _dot3(p, p)
        t = t + _dot3(t, p)
    return t


def _rmsnorm_kernel(x_ref, g_ref, o_ref, *, eps):
    x = x_ref[...]
    y = x * lax.rsqrt(jnp.mean(x * x, axis=-1, keepdims=True) + eps)
    o_ref[...] = (y * g_ref[...]).astype(o_ref.dtype)


def rmsnorm(x, g, out_dtype, *, eps=NORM_EPS, tm=512):
    m, d = x.shape
    tm = min(tm, m)
    return pl.pallas_call(
        functools.partial(_rmsnorm_kernel, eps=eps),
        out_shape=jax.ShapeDtypeStruct((m, d), out_dtype),
        grid=(m // tm,),
        in_specs=[pl.BlockSpec((tm, d), lambda i: (i, 0)), pl.BlockSpec((1, d), lambda i: (0, 0))],
        out_specs=pl.BlockSpec((tm, d), lambda i: (i, 0)),
        compiler_params=_params("parallel"),
        name="rmsnorm",
    )(x, g.reshape(1, d))


def _mm_kernel(a_ref, w_ref, *rest, epilogue):
    o_ref = rest[-1]
    acc = _dot(a_ref[...], w_ref[...])
    if epilogue is not None:
        acc = epilogue(acc, *rest[:-1])
    o_ref[...] = acc.astype(o_ref.dtype)


def matmul(a, w, out_dtype, *, tm=1024, tn=1024, extra=(), extra_specs=(), epilogue=None, name="matmul"):
    m, k = a.shape
    n = w.shape[1]
    tm, tn = min(tm, m), min(tn, n)
    return pl.pallas_call(
        functools.partial(_mm_kernel, epilogue=epilogue),
        out_shape=jax.ShapeDtypeStruct((m, n), out_dtype),
        grid=(n // tn, m // tm),
        in_specs=[pl.BlockSpec((tm, k), lambda j, i: (i, 0)),
                  pl.BlockSpec((k, tn), lambda j, i: (0, j))] + list(extra_specs),
        out_specs=pl.BlockSpec((tm, tn), lambda j, i: (i, j)),
        compiler_params=_params("parallel", "parallel"),
        name=name,
    )(a, w, *extra)


def matmul_residual(a, w, res, *, tm=1024, tn=1024, name="matmul_residual"):
    tm, tn = min(tm, a.shape[0]), min(tn, w.shape[1])
    return matmul(a, w, F32, tm=tm, tn=tn, extra=(res,),
                  extra_specs=(pl.BlockSpec((tm, tn), lambda j, i: (i, j)),),
                  epilogue=lambda acc, r_ref: r_ref[...] + acc, name=name)


def _ple_kernel(h_ref, wg_ref, p_ref, wp_ref, x_ref, o_ref):
    gate = jax.nn.sigmoid(_dot(h_ref[...], wg_ref[...]))
    proj = _dot(p_ref[...].astype(BF16), wp_ref[...])
    o_ref[...] = x_ref[...] + gate * proj


def per_layer_embedding(x, hn, w_gate, p, w_proj, *, tm=1024, tn=1024):
    m, d = x.shape
    pd = p.shape[1]
    tm, tn = min(tm, m), min(tn, d)
    return pl.pallas_call(
        _ple_kernel,
        out_shape=jax.ShapeDtypeStruct((m, d), F32),
        grid=(d // tn, m // tm),
        in_specs=[pl.BlockSpec((tm, d), lambda j, i: (i, 0)),
                  pl.BlockSpec((d, tn), lambda j, i: (0, j)),
                  pl.BlockSpec((tm, pd), lambda j, i: (i, 0)),
                  pl.BlockSpec((pd, tn), lambda j, i: (0, j)),
                  pl.BlockSpec((tm, tn), lambda j, i: (i, j))],
        out_specs=pl.BlockSpec((tm, tn), lambda j, i: (i, j)),
        compiler_params=_params("parallel", "parallel"),
        name="per_layer_embedding",
    )(hn, w_gate, p, w_proj, x)


def _rope_table_kernel(pos_ref, freq_ref, cos_ref, sin_lo_ref, sin_hi_ref):
    half = ROT_DIM // 2
    ang = pos_ref[...].astype(F32) * freq_ref[...]
    lane = _iota(ang.shape, 1)
    c, s = jnp.cos(ang), jnp.sin(ang)
    cos_ref[...] = jnp.where(lane < ROT_DIM, c, 1.0)
    sin_lo_ref[...] = jnp.where(lane < half, -s, 0.0)
    sin_hi_ref[...] = jnp.where((lane >= half) & (lane < ROT_DIM), s, 0.0)


def rope_tables(positions, *, tm=1024):
    m = positions.size
    tm = min(tm, m)
    inv_freq = ROPE_THETA ** (-jnp.arange(0, ROT_DIM, 2, dtype=F32) / ROT_DIM)
    freq_row = jnp.concatenate([inv_freq, inv_freq, jnp.zeros((LANES - ROT_DIM,), F32)]).reshape(1, LANES)
    shape = jax.ShapeDtypeStruct((m, LANES), F32)
    spec = pl.BlockSpec((tm, LANES), lambda i: (i, 0))
    return pl.pallas_call(
        _rope_table_kernel,
        out_shape=(shape, shape, shape),
        grid=(m // tm,),
        in_specs=[pl.BlockSpec((tm, 1), lambda i: (i, 0)), pl.BlockSpec((1, LANES), lambda i: (0, 0))],
        out_specs=(spec, spec, spec),
        compiler_params=_params("parallel"),
        name="rope_tables",
    )(positions.reshape(m, 1), freq_row)


def _attn_in_kernel(a_ref, w_ref, cos_ref, sin_lo_ref, sin_hi_ref, o_ref, *, n_q_blocks):
    j = pl.program_id(0)
    acc = _dot(a_ref[...], w_ref[...])
    half = ROT_DIM // 2

    @pl.when(j < 2 * n_q_blocks)
    def _():
        scale = jnp.where(j < n_q_blocks, A_HEAD_DIM ** -0.5, 1.0).astype(F32)
        cos, sin_lo, sin_hi = cos_ref[...], sin_lo_ref[...], sin_hi_ref[...]
        for g in range(acc.shape[1] // LANES):
            x = acc[:, g * LANES:(g + 1) * LANES]
            y = x * cos + pltpu.roll(x, LANES - half, 1) * sin_lo + pltpu.roll(x, half, 1) * sin_hi
            o_ref[:, g * LANES:(g + 1) * LANES] = (y * scale).astype(o_ref.dtype)

    @pl.when(j >= 2 * n_q_blocks)
    def _():
        o_ref[...] = acc.astype(o_ref.dtype)


def attn_in_proj(hn, w_in, tables, qk_width, *, tm=1024, tn=1024):
    m, k = hn.shape
    n = w_in.shape[1]
    tm, tn = min(tm, m), min(tn, qk_width)
    tspec = pl.BlockSpec((tm, LANES), lambda j, i: (i, 0))
    return pl.pallas_call(
        functools.partial(_attn_in_kernel, n_q_blocks=qk_width // tn),
        out_shape=jax.ShapeDtypeStruct((m, n), BF16),
        grid=(n // tn, m // tm),
        in_specs=[pl.BlockSpec((tm, k), lambda j, i: (i, 0)),
                  pl.BlockSpec((k, tn), lambda j, i: (0, j)), tspec, tspec, tspec],
        out_specs=pl.BlockSpec((tm, tn), lambda j, i: (i, j)),
        compiler_params=_params("parallel", "parallel"),
        name="attn_in_proj",
    )(hn, w_in, *tables)


def _attn_vt_kernel(w_ref, a_ref, o_ref):
    o_ref[...] = _dot(w_ref[...], a_ref[...], NT).astype(o_ref.dtype)


def attn_v_proj_t(hn, w_t, batch, seq, blk, *, tn=1024):
    m, k = hn.shape
    n = w_t.shape[0]
    tn = min(tn, n)
    nk = seq // blk
    return pl.pallas_call(
        _attn_vt_kernel,
        out_shape=jax.ShapeDtypeStruct((batch, nk, n, blk), BF16),
        grid=(n // tn, m // blk),
        in_specs=[pl.BlockSpec((tn, k), lambda j, i: (j, 0)),
                  pl.BlockSpec((blk, k), lambda j, i: (i, 0))],
        out_specs=pl.BlockSpec((None, None, tn, blk), lambda j, i: (i // nk, i % nk, j, 0)),
        compiler_params=_params("parallel", "parallel"),
        name="attn_v_proj_t",
    )(w_t, hn)


def _diff_attn_kernel(lam_ref, q_ref, k_ref, vt_ref, z_ref, g_ref, o_ref, m_ref, l_ref, acc_ref, *, blk, lam_init):
    i = pl.program_id(2)
    lam = lam_ref[...]
    lam_full = (jnp.exp(jnp.sum(lam[0:1] * lam[1:2], axis=-1, keepdims=True))
                - jnp.exp(jnp.sum(lam[2:3] * lam[3:4], axis=-1, keepdims=True)) + lam_init)
    m_ref[...] = jnp.full(m_ref.shape, -jnp.inf, F32)
    l_ref[...] = jnp.zeros(l_ref.shape, F32)
    acc_ref[...] = jnp.zeros(acc_ref.shape, F32)
    q = q_ref[...]
    d = A_HEAD_DIM

    def step(j, masked):
        start = pl.multiple_of(j * blk, blk)
        kb = k_ref[pl.ds(start, blk), :]
        vt = vt_ref[j]
        for c in range(2):
            s = _dot(kb[:, c * d:(c + 1) * d], q[:, c * d:(c + 1) * d], NT)
            if masked:
                s = jnp.where(_iota(s.shape, 0) <= _iota(s.shape, 1), s, -jnp.inf)
            m_prev = m_ref[c]
            m_new = jnp.maximum(m_prev, jnp.max(s, axis=0, keepdims=True))
            alpha = jnp.exp(m_prev - m_new)
            p = jnp.exp(s - m_new)
            l_ref[c] = alpha * l_ref[c] + jnp.sum(p, axis=0, keepdims=True)
            acc_ref[c] = alpha * acc_ref[c] + _dot(vt, p.astype(BF16))
            m_ref[c] = m_new

    def body(jj, carry):
        step(2 * jj, False)
        step(2 * jj + 1, False)
        return carry

    lax.fori_loop(0, i // 2, body, 0)

    @pl.when(i % 2 == 1)
    def _():
        step(i - 1, False)

    step(i, True)
    o = acc_ref[0] / l_ref[0] - lam_full * (acc_ref[1] / l_ref[1])
    o = o * lax.rsqrt(jnp.mean(o * o, axis=0, keepdims=True) + SUBLN_EPS) * g_ref[...]
    o = (o * (1.0 - lam_init)).T
    o_ref[...] = (o * _silu(z_ref[...].astype(F32))).astype(o_ref.dtype)


def diff_attention_core(qkz, vt, lam, subln_g, batch, seq, heads, lam_init):
    m = qkz.shape[0]
    blk = vt.shape[-1]
    nq = seq // blk
    w = A_V_DIM
    return pl.pallas_call(
        functools.partial(_diff_attn_kernel, blk=blk, lam_init=lam_init),
        out_shape=jax.ShapeDtypeStruct((m, heads * w), BF16),
        grid=(batch, heads, nq),
        in_specs=[pl.BlockSpec((4, A_HEAD_DIM), lambda b, h, i: (0, 0)),
                  pl.BlockSpec((blk, w), lambda b, h, i: (b * nq + i, h)),
                  pl.BlockSpec((seq, w), lambda b, h, i: (b, heads + h)),
                  pl.BlockSpec((None, nq, w, blk), lambda b, h, i: (b, 0, h, 0)),
                  pl.BlockSpec((blk, w), lambda b, h, i: (b * nq + i, 2 * heads + h)),
                  pl.BlockSpec((w, 1), lambda b, h, i: (0, 0))],
        out_specs=pl.BlockSpec((blk, w), lambda b, h, i: (b * nq + i, h)),
        scratch_shapes=[pltpu.VMEM((2, 1, blk), F32), pltpu.VMEM((2, 1, blk), F32),
                        pltpu.VMEM((2, w, blk), F32)],
        compiler_params=_params("parallel", "parallel", "parallel"),
        name="diff_attention",
    )(lam, qkz, qkz, vt, qkz, subln_g.reshape(w, 1))


def diff_attention_layer(x, hn, tables, w_in, lam, subln_g, w_out, batch, seq, lam_init):
    d = x.shape[1]
    heads = d // A_V_DIM
    qk_w = heads * 2 * A_HEAD_DIM
    v_w = heads * A_V_DIM
    blk = min(ATTN_BLOCK, seq)
    w_qkz = jnp.concatenate([w_in[:, :2 * qk_w], w_in[:, 2 * qk_w + v_w:]], axis=1).astype(BF16)
    w_vt = w_in[:, 2 * qk_w:2 * qk_w + v_w].T.astype(BF16)
    qkz = attn_in_proj(hn, w_qkz, tables, qk_w)
    vt = attn_v_proj_t(hn, w_vt, batch, seq, blk)
    o = diff_attention_core(qkz, vt, lam, subln_g, batch, seq, heads, lam_init)
    return matmul_residual(o, w_out.astype(BF16), x, name="attn_out_proj")


def _gdn_in_kernel(a_ref, w_ref, cw_ref, o_ref, tail_ref, *, n_conv_blocks, rows_per_seq):
    j = pl.program_id(0)
    i = pl.program_id(1)
    acc = _dot(a_ref[...], w_ref[...])
    tm = acc.shape[0]

    @pl.when(j < n_conv_blocks)
    def _():
        @pl.when((i * tm) % rows_per_seq == 0)
        def _():
            tail_ref[...] = jnp.zeros(tail_ref.shape, F32)

        tail = tail_ref[...]
        sub = _iota(tail.shape, 0)

        def shifted(s):
            xs = pltpu.roll(acc, s, 0)
            head = jnp.where(sub < s, pltpu.roll(tail, s, 0), xs[:8])
            return jnp.concatenate([head, xs[8:]], axis=0)

        cw = cw_ref[...]
        last = C_CONV_WIDTH - 1
        y = shifted(last) * cw[0:1]
        for t in range(1, last):
            y = y + shifted(last - t) * cw[t:t + 1]
        y = y + acc * cw[last:last + 1]
        tail_ref[...] = acc[tm - 8:]
        o_ref[...] = _silu(y).astype(o_ref.dtype)

    @pl.when(j >= n_conv_blocks)
    def _():
        o_ref[...] = acc.astype(o_ref.dtype)


def gdn_in_proj(hn, w, conv_w, seq, *, tm=1024, tn=1024):
    m, k = hn.shape
    n = w.shape[1]
    conv_ch = conv_w.shape[1]
    tm, tn = min(tm, seq), min(tn, n)
    ncb = conv_ch // tn
    return pl.pallas_call(
        functools.partial(_gdn_in_kernel, n_conv_blocks=ncb, rows_per_seq=seq),
        out_shape=jax.ShapeDtypeStruct((m, n), BF16),
        grid=(n // tn, m // tm),
        in_specs=[pl.BlockSpec((tm, k), lambda j, i: (i, 0)),
                  pl.BlockSpec((k, tn), lambda j, i: (0, j)),
                  pl.BlockSpec((C_CONV_WIDTH, tn), lambda j, i: (0, jnp.minimum(j, ncb - 1)))],
        out_specs=pl.BlockSpec((tm, tn), lambda j, i: (i, j)),
        scratch_shapes=[pltpu.VMEM((8, tn), F32)],
        compiler_params=_params("arbitrary", "arbitrary"),
        name="gdn_in_proj",
    )(hn, w, conv_w)


def _pair_masks():
    n = 2 * CHUNK
    r, c = _iota((n, n), 0), _iota((n, n), 1)
    same = (r // CHUNK) == (c // CHUNK)
    return same & (r >= c), same & (r > c), r == c


def _dot_x2l(a, b_exact, dims=NN):
    h, l = _split2(a)
    return _dot(h, b_exact, dims) + _dot(l, b_exact, dims)


def _dot_x2r(a_exact, b, dims=NN):
    h, l = _split2(b)
    return _dot(a_exact, h, dims) + _dot(a_exact, l, dims)


def _inverse_stages(p_ref, t_ref, count, nilpotency):
    for _ in range(int(math.log2(nilpotency)) - 1):
        for i in range(count):
            p_ref[i] = _dot(p_ref[i], p_ref[i]).astype(BF16)
        for i in range(count):
            t = t_ref[i]
            t_ref[i] = t + _dot(t.astype(BF16), p_ref[i])


def _gdn_gates_kernel(a_ref, w_ref, alog_ref, dtb_ref, beta_ref, gc_ref):
    acc = _dot(a_ref[...], w_ref[...])
    beta_ref[...] = jax.nn.sigmoid(acc[:, :LANES]).astype(beta_ref.dtype)
    g = -jnp.exp(alog_ref[...]) * _softplus(acc[:, LANES:] + dtb_ref[...])
    tri_b = _pair_masks()[0].astype(BF16)
    n2 = 2 * CHUNK
    for n in range(g.shape[0] // n2):
        rows = slice(n * n2, (n + 1) * n2)
        gc_ref[rows, :] = _dot_xr(tri_b, g[rows])


def gdn_gates(hn, w_ba, alog_row, dtb_row, *, tm=512):
    m, k = hn.shape
    tm = min(tm, m)
    row = pl.BlockSpec((1, LANES), lambda i: (0, 0))
    out = pl.BlockSpec((tm, LANES), lambda i: (i, 0))
    return pl.pallas_call(
        _gdn_gates_kernel,
        out_shape=(jax.ShapeDtypeStruct((m, LANES), BF16), jax.ShapeDtypeStruct((m, LANES), F32)),
        grid=(m // tm,),
        in_specs=[pl.BlockSpec((tm, k), lambda i: (i, 0)), pl.BlockSpec((k, 2 * LANES), lambda i: (0, 0)), row, row],
        out_specs=(out, out),
        compiler_params=_params("parallel"),
        name="gdn_gates",
    )(hn, w_ba, alog_row, dtb_row)


def _gdn_prep_kernel(q_ref, k_ref, v_ref, beta_ref, gc_ref,
                     w_ref, qg_ref, u_ref, att_ref, kdt_ref, dec_ref,
                     g_s, dec_s, kn_s, kb_s, rhs_s, p_s, t_s):
    h = pl.program_id(1)
    c = CHUNK
    n2 = 2 * c
    dk = C_HEAD_DIM
    npairs = q_ref.shape[0] // n2
    tri, strict, diag = _pair_masks()
    eye_b = diag.astype(BF16)
    eye_f = diag.astype(F32)
    sel = (_iota((LANES, dk), 0) == h).astype(BF16)
    own_lane = _iota((n2, LANES), 1) == h
    own_b = own_lane.astype(BF16)
    upper_half = _iota((n2, dk), 0) < c
    pairs = [slice(n * n2, (n + 1) * n2) for n in range(npairs)]

    for n, rows in enumerate(pairs):
        gc_all = gc_ref[rows, :]
        gc = jnp.broadcast_to(jnp.sum(jnp.where(own_lane, gc_all, 0.0), axis=-1, keepdims=True), (n2, dk))
        gc_row = _dot_x2r(own_b, gc_all, NT)
        g_s[n] = gc
        dec_s[n] = jnp.where(tri, jnp.exp(jnp.where(tri, gc - gc_row, 0.0)), 0.0)
    for n, rows in enumerate(pairs):
        gc = g_s[n]
        qf = q_ref[rows, :].astype(F32)
        kf = k_ref[rows, :].astype(F32)
        vf = v_ref[rows, :].astype(F32)
        qn = qf * lax.rsqrt(jnp.sum(qf * qf, axis=-1, keepdims=True) + 1e-6) * (dk ** -0.5)
        kn = kf * lax.rsqrt(jnp.sum(kf * kf, axis=-1, keepdims=True) + 1e-6)
        beta = _dot(beta_ref[rows, :], sel)
        egc = jnp.exp(gc)
        kb = kn * beta
        kn_b = kn.astype(BF16)
        qn_b = qn.astype(BF16)
        kn_s[n] = kn_b
        kb_s[n] = kb.astype(BF16)
        rhs_s[n] = jnp.concatenate([vf * beta, kb * egc], axis=1).astype(BF16)
        g_last = jnp.where(upper_half, gc[c - 1:c, :], gc[n2 - 1:n2, :])
        kdec = kn * jnp.exp(g_last - gc)
        qg_ref[rows, :] = (qn * egc).astype(BF16)
        att_ref[rows, :] = (_dot(qn_b, kn_b, NT) * dec_s[n]).astype(BF16)
        kdt_ref[:, rows] = _dot(eye_b, kdec.astype(BF16), NT).astype(BF16)
        dec_ref[2 * n:2 * n + 1, :] = jnp.exp(gc[c - 1:c, :])
        dec_ref[2 * n + 1:2 * n + 2, :] = jnp.exp(gc[n2 - 1:n2, :])
    for n, rows in enumerate(pairs):
        neg_a = jnp.where(strict, -(_dot(kb_s[n], kn_s[n], NT) * dec_s[n]), 0.0)
        p_s[n] = neg_a.astype(BF16)
        t_s[n] = eye_f + neg_a
    _inverse_stages(p_s, t_s, npairs, c)
    for n, rows in enumerate(pairs):
        sol = _dot(t_s[n].astype(BF16), rhs_s[n])
        u_ref[rows, :] = sol[:, :dk]
        w_ref[rows, :] = sol[:, dk:].astype(BF16)


def _gdn_scan_kernel(w_ref, qg_ref, u_ref, att_ref, kdt_ref, dec_ref, z_ref, g_ref, o_ref,
                     s_ref, ms_s, vp_s, *, group):
    c = CHUNK
    dk = C_HEAD_DIM
    zeros = jnp.zeros((c, dk), BF16)

    @pl.when(pl.program_id(2) == 0)
    def _():
        s_ref[...] = jnp.zeros(s_ref.shape, F32)

    for n in range(w_ref.shape[1] // c):
        rows = slice(n * c, (n + 1) * c)
        pair = slice((n // 2) * 2 * c, (n // 2 + 1) * 2 * c)
        for gi in range(group):
            lhs = jnp.concatenate([w_ref[gi, rows, :], qg_ref[gi, rows, :]], axis=0)
            ms_s[gi] = _dot(lhs, s_ref[gi].astype(BF16))
        for gi in range(group):
            v_new = (u_ref[gi, rows, :] - ms_s[gi, :c, :]).astype(BF16)
            vp_s[gi] = jnp.concatenate([v_new, zeros] if n % 2 == 0 else [zeros, v_new], axis=0)
        for gi in range(group):
            s_ref[gi] = s_ref[gi] * dec_ref[gi, n:n + 1, :] + _dot(kdt_ref[gi, :, pair], vp_s[gi])
        for gi in range(group):
            o = ms_s[gi, c:, :] + _dot(att_ref[gi, rows, :], vp_s[gi])
            o = o * lax.rsqrt(jnp.mean(o * o, axis=-1, keepdims=True) + NORM_EPS) * g_ref[...]
            z = z_ref[rows, gi * dk:(gi + 1) * dk].astype(F32)
            o_ref[rows, gi * dk:(gi + 1) * dk] = (o * _silu(z)).astype(o_ref.dtype)


def gated_deltanet_layer(x, hn, w_in, conv_w, a_log, dt_bias, norm_g, w_out, batch, seq):
    m, d = x.shape
    dk = C_HEAD_DIM
    k_heads = d // dk
    v_heads = 2 * k_heads
    conv_ch = 2 * k_heads * dk + v_heads * dk
    main_w = conv_ch + v_heads * dk
    qkvz = gdn_in_proj(hn, w_in[:, :main_w].astype(BF16), conv_w, seq)
    pad = jnp.zeros((d, LANES - v_heads), F32)
    w_ba = jnp.concatenate([w_in[:, main_w:main_w + v_heads], pad, w_in[:, main_w + v_heads:], pad], axis=1)
    row_pad = jnp.zeros((LANES - v_heads,), F32)
    alog_row = jnp.concatenate([a_log, row_pad]).reshape(1, LANES)
    dtb_row = jnp.concatenate([dt_bias, row_pad]).reshape(1, LANES)
    beta_all, gc_all = gdn_gates(hn, w_ba.astype(BF16), alog_row, dtb_row)

    rows = min(PREP_ROWS, seq)
    nr = seq // rows
    ncr = rows // CHUNK
    npair = rows // (2 * CHUNK)
    hv = v_heads
    bh_t = lambda dt, last: jax.ShapeDtypeStruct((batch, hv, seq, last), dt)
    blk4 = lambda last: pl.BlockSpec((None, None, rows, last), lambda b, h, i: (b, h, i, 0))
    qoff, koff, voff = 0, k_heads, 2 * k_heads
    w_, qg_, u_, att_, kdt_, dec_ = pl.pallas_call(
        _gdn_prep_kernel,
        out_shape=(bh_t(BF16, dk), bh_t(BF16, dk), bh_t(F32, dk), bh_t(BF16, 2 * CHUNK),
                   jax.ShapeDtypeStruct((batch, hv, dk, seq), BF16),
                   jax.ShapeDtypeStruct((batch, hv, seq // CHUNK, dk), F32)),
        grid=(batch, hv, nr),
        in_specs=[pl.BlockSpec((rows, dk), lambda b, h, i: (b * nr + i, qoff + h // 2)),
                  pl.BlockSpec((rows, dk), lambda b, h, i: (b * nr + i, koff + h // 2)),
                  pl.BlockSpec((rows, dk), lambda b, h, i: (b * nr + i, voff + h)),
                  pl.BlockSpec((rows, LANES), lambda b, h, i: (b * nr + i, 0)),
                  pl.BlockSpec((rows, LANES), lambda b, h, i: (b * nr + i, 0))],
        out_specs=(blk4(dk), blk4(dk), blk4(dk), blk4(2 * CHUNK),
                   pl.BlockSpec((None, None, dk, rows), lambda b, h, i: (b, h, 0, i)),
                   pl.BlockSpec((None, None, ncr, dk), lambda b, h, i: (b, h, i, 0))),
        scratch_shapes=[pltpu.VMEM((npair, dk, dk), F32), pltpu.VMEM((npair, dk, dk), F32),
                        pltpu.VMEM((npair, dk, dk), BF16), pltpu.VMEM((npair, dk, dk), BF16),
                        pltpu.VMEM((npair, dk, 2 * dk), BF16), pltpu.VMEM((npair, dk, dk), BF16),
                        pltpu.VMEM((npair, dk, dk), F32)],
        compiler_params=_params("parallel", "parallel", "parallel"),
        name="gdn_prep",
    )(qkvz, qkvz, qkvz, beta_all, gc_all)

    rows = min(SCAN_ROWS, seq)
    nr = seq // rows
    ncr = rows // CHUNK
    group = 8
    zoff = conv_ch // (group * dk)
    gblk = lambda last: pl.BlockSpec((None, group, rows, last), lambda b, h, i: (b, h, i, 0))
    o = pl.pallas_call(
        functools.partial(_gdn_scan_kernel, group=group),
        out_shape=jax.ShapeDtypeStruct((m, hv * dk), BF16),
        grid=(batch, hv // group, nr),
        in_specs=[gblk(dk), gblk(dk), gblk(dk), gblk(2 * CHUNK),
                  pl.BlockSpec((None, group, dk, rows), lambda b, h, i: (b, h, 0, i)),
                  pl.BlockSpec((None, group, ncr, dk), lambda b, h, i: (b, h, i, 0)),
                  pl.BlockSpec((rows, group * dk), lambda b, h, i: (b * nr + i, zoff + h)),
                  pl.BlockSpec((1, dk), lambda b, h, i: (0, 0))],
        out_specs=pl.BlockSpec((rows, group * dk), lambda b, h, i: (b * nr + i, h)),
        scratch_shapes=[pltpu.VMEM((group, dk, dk), F32), pltpu.VMEM((group, 2 * CHUNK, dk), F32),
                        pltpu.VMEM((group, 2 * CHUNK, dk), BF16)],
        compiler_params=_params("parallel", "parallel", "arbitrary"),
        name="gdn_scan",
    )(w_, qg_, u_, att_, kdt_, dec_, qkvz, norm_g.reshape(1, dk))
    return matmul_residual(o, w_out.astype(BF16), x, tm=512, name="gdn_out_proj")


def _rwkv_mix_kernel(x_ref, g_ref, mu_ref, o_ref, tail_ref, *, rows_per_seq):
    i = pl.program_id(0)
    x = x_ref[...]
    tm = x.shape[0]
    hn = x * lax.rsqrt(jnp.mean(x * x, axis=-1, keepdims=True) + NORM_EPS) * g_ref[...]

    @pl.when((i * tm) % rows_per_seq == 0)
    def _():
        tail_ref[...] = jnp.zeros(tail_ref.shape, F32)

    prev = jnp.where(_iota(hn.shape, 0) == 0, tail_ref[7:8, :], pltpu.roll(hn, 1, 0))
    tail_ref[...] = hn[tm - 8:]
    xx = prev - hn
    for c in range(o_ref.shape[0]):
        o_ref[c] = (hn + xx * mu_ref[c:c + 1, :]).astype(o_ref.dtype)


def rwkv_token_mix(x, norm_g, mu, seq, *, tm=256):
    m, d = x.shape
    tm = min(tm, seq)
    nmix = mu.shape[0]
    return pl.pallas_call(
        functools.partial(_rwkv_mix_kernel, rows_per_seq=seq),
        out_shape=jax.ShapeDtypeStruct((nmix, m, d), BF16),
        grid=(m // tm,),
        in_specs=[pl.BlockSpec((tm, d), lambda i: (i, 0)),
                  pl.BlockSpec((1, d), lambda i: (0, 0)),
                  pl.BlockSpec((nmix, d), lambda i: (0, 0))],
        out_specs=pl.BlockSpec((nmix, tm, d), lambda i: (0, i, 0)),
        scratch_shapes=[pltpu.VMEM((8, d), F32)],
        compiler_params=_params("arbitrary"),
        name="rwkv_token_mix",
    )(x, norm_g.reshape(1, d), mu)


def _grouped_mm_kernel(a_ref, w_ref, o_ref):
    o_ref[...] = _dot(a_ref[...], w_ref[...]).astype(o_ref.dtype)


def grouped_matmul(a, w, out_dtype, *, tm=1024, tn=1024):
    g, k, n = w.shape
    m = a.shape[1]
    tm, tn = min(tm, m), min(tn, n)
    return pl.pallas_call(
        _grouped_mm_kernel,
        out_shape=jax.ShapeDtypeStruct((g, m, n), out_dtype),
        grid=(g, n // tn, m // tm),
        in_specs=[pl.BlockSpec((None, tm, k), lambda c, j, i: (c, i, 0)),
                  pl.BlockSpec((None, k, tn), lambda c, j, i: (c, 0, j))],
        out_specs=pl.BlockSpec((None, tm, tn), lambda c, j, i: (c, i, j)),
        compiler_params=_params("parallel", "parallel", "parallel"),
        name="rwkv_rkvg_proj",
    )(a, w)


def _rwkv_lora_kernel(xw_ref, xa_ref, w1_ref, w2_ref, a1_ref, a2_ref, w0_ref, a0_ref, ld_ref, a_ref):
    hw = jnp.tanh(_dot(xw_ref[...], w1_ref[...])).astype(BF16)
    lw = w0_ref[...] + _dot(hw, w2_ref[...])
    log_w = -_softplus(-lw) - 0.5
    ld_ref[...] = -jnp.exp(log_w)
    ha = _dot(xa_ref[...], a1_ref[...]).astype(BF16)
    a_ref[...] = jax.nn.sigmoid(a0_ref[...] + _dot(ha, a2_ref[...]))


def rwkv_lora(xs, w1, w2, a1, a2, w0, a0, *, tm=512):
    _, m, d = xs.shape
    tm = min(tm, m)
    r = w1.shape[1]
    row = pl.BlockSpec((1, d), lambda i: (0, 0))
    shape = jax.ShapeDtypeStruct((m, d), F32)
    return pl.pallas_call(
        _rwkv_lora_kernel,
        out_shape=(shape, shape),
        grid=(m // tm,),
        in_specs=[pl.BlockSpec((None, tm, d), lambda i: (4, i, 0)),
                  pl.BlockSpec((None, tm, d), lambda i: (5, i, 0)),
                  pl.BlockSpec((d, r), lambda i: (0, 0)), pl.BlockSpec((r, d), lambda i: (0, 0)),
                  pl.BlockSpec((d, r), lambda i: (0, 0)), pl.BlockSpec((r, d), lambda i: (0, 0)),
                  row, row],
        out_specs=(pl.BlockSpec((tm, d), lambda i: (i, 0)), pl.BlockSpec((tm, d), lambda i: (i, 0))),
        compiler_params=_params("parallel"),
        name="rwkv_lora",
    )(xs, xs, w1, w2, a1, a2, w0.reshape(1, d), a0.reshape(1, d))


def _rwkv_prep_kernel(r_ref, k_ref, v_ref, ld_ref, a_ref, kk_ref, ka_ref, rk_ref,
                      wt_ref, rt_ref, u_ref, aro_ref, kbt_ref, vb_ref, dec_ref, bonus_ref,
                      kkt_s, kbar_s, bbar_s, akk_s, p_s, t_s, drow_s):
    c = CHUNK
    n2 = 2 * c
    n_h = B_HEAD_DIM
    npairs = r_ref.shape[0] // n2
    lane_head = _iota((1, LANES), 1) // n_h
    seg = (_iota((LANES, LANES), 0) // n_h == _iota((LANES, LANES), 1) // n_h).astype(BF16)
    tri, strict, diag = _pair_masks()
    tri_b = tri.astype(BF16)
    eye_b = diag.astype(BF16)
    eye_f = diag.astype(F32)
    upper_half = _iota((n2, LANES), 0) < c
    pairs = [slice(n * n2, (n + 1) * n2) for n in range(npairs)]

    r = r_ref[...]
    k = k_ref[...]
    v = v_ref[...]
    a = a_ref[...]
    kk_raw = k * kk_ref[...]
    kk = kk_raw * lax.rsqrt(_dot_xl(kk_raw * kk_raw, seg) + 1e-6)
    k2 = k * (1.0 + (a - 1.0) * ka_ref[...])
    bb = kk * a
    bonus_ref[...] = _dot_xl(r * k2 * rk_ref[...], seg) * v
    vb_ref[...] = v.astype(BF16)
    for n, rows in enumerate(pairs):
        ld = ld_ref[rows, :]
        cs = _dot_xr(tri_b, ld)
        cs_last = jnp.where(upper_half, cs[c - 1:c, :], cs[n2 - 1:n2, :])
        w_inv = jnp.exp(-cs)
        w_end = jnp.exp(cs_last - cs)
        rt_ref[rows, :] = (r[rows] * jnp.exp(cs)).astype(BF16)
        kkt_s[n] = (kk[rows] * jnp.exp(cs - ld)).astype(BF16)
        k2c, bbc = k2[rows], bb[rows]
        kbar_s[n] = (k2c * w_inv).astype(BF16)
        bbar_s[n] = (bbc * w_inv).astype(BF16)
        kw, bw = (k2c * w_end).astype(BF16), (-(bbc * w_end)).astype(BF16)
        for half in range(2):
            hs = slice(half * c, (half + 1) * c)
            kb_end = jnp.concatenate([kw[hs], bw[hs]], axis=0)
            cols = slice((2 * n + half) * n2, (2 * n + half + 1) * n2)
            kbt_ref[:, cols] = _dot(eye_b, kb_end, NT).astype(BF16)
        drow_s[2 * n:2 * n + 1, :] = jnp.exp(cs[c - 1:c, :])
        drow_s[2 * n + 1:2 * n + 2, :] = jnp.exp(cs[n2 - 1:n2, :])
    drow_s[2 * npairs:, :] = jnp.zeros((LANES - 2 * npairs, LANES), F32)
    dec_ref[...] = _dot_xr(eye_b, drow_s[...], NT)
    zero_b = jnp.zeros((n2, LANES), BF16)
    for n, rows in enumerate(pairs):
        for hh in range(2):
            own = lane_head == hh
            i = 2 * n + hh
            kkt_m = jnp.where(own, kkt_s[n], zero_b)
            rt_m = jnp.where(own, rt_ref[rows, :], zero_b)
            neg_a = jnp.where(strict, -_dot(kkt_m, bbar_s[n], NT), 0.0)
            p_s[i] = neg_a.astype(BF16)
            t_s[i] = eye_f + neg_a
            akk_s[i] = jnp.where(strict, _dot(kkt_m, kbar_s[n], NT), 0.0).astype(BF16)
            a_rk = jnp.where(tri, _dot(rt_m, kbar_s[n], NT), 0.0)
            a_rb = jnp.where(tri, _dot(rt_m, bbar_s[n], NT), 0.0)
            aro_ref[hh, rows, :] = jnp.concatenate([a_rk, -a_rb], axis=1).astype(BF16)
    _inverse_stages(p_s, t_s, 2 * npairs, c)
    for n, rows in enumerate(pairs):
        for hh in range(2):
            i = 2 * n + hh
            p_s[i] = _dot(akk_s[i], vb_ref[rows, :]).astype(BF16)
    for n, rows in enumerate(pairs):
        t0, t1 = t_s[2 * n].astype(BF16), t_s[2 * n + 1].astype(BF16)
        own0 = lane_head == 0
        wt_ref[rows, :] = jnp.where(own0, _dot(t0, kkt_s[n]), _dot(t1, kkt_s[n])).astype(BF16)
        u_ref[rows, :] = jnp.where(own0, _dot(t0, p_s[2 * n]), _dot(t1, p_s[2 * n + 1]))


def _rwkv_scan_kernel(wt_ref, rt_ref, u_ref, aro_ref, kbt_ref, vb_ref, dec_ref, bonus_ref, gate_ref,
                      lnw_ref, lnb_ref, o_ref, s_ref, ms_s, xc_s, xp_s, *, group):
    c = CHUNK
    n2 = 2 * c
    n_h = B_HEAD_DIM
    first_head = _iota((1, LANES), 1) < n_h
    same_head = _iota((LANES, LANES), 0) // n_h == _iota((LANES, LANES), 1) // n_h
    zeros = jnp.zeros((c, LANES), BF16)

    def head_mean(x):
        tot = jnp.sum(x, axis=-1, keepdims=True)
        lo = jnp.sum(jnp.where(first_head, x, 0.0), axis=-1, keepdims=True)
        return jnp.where(first_head, lo, tot - lo) * (1.0 / n_h)

    @pl.when(pl.program_id(2) == 0)
    def _():
        s_ref[...] = jnp.zeros(s_ref.shape, F32)

    for n in range(u_ref.shape[1] // c):
        rows = slice(n * c, (n + 1) * c)
        for gi in range(group):
            lhs = jnp.concatenate([wt_ref[gi, rows, :], rt_ref[gi, rows, :]], axis=0)
            ms_s[gi] = _dot(lhs, s_ref[gi].astype(BF16))
        for gi in range(group):
            sa = (u_ref[gi, rows, :] + ms_s[gi, :c, :]).astype(BF16)
            vb = vb_ref[gi, rows, :]
            xc_s[gi] = jnp.concatenate([vb, sa], axis=0)
            xp_s[gi] = jnp.concatenate([vb, zeros, sa, zeros] if n % 2 == 0 else [zeros, vb, zeros, sa], axis=0)
        for gi in range(group):
            upd = _dot(kbt_ref[gi, :, n * n2:(n + 1) * n2], xc_s[gi])
            dec_col = jnp.broadcast_to(dec_ref[gi, :, n:n + 1], (LANES, LANES))
            s_ref[gi] = s_ref[gi] * dec_col + jnp.where(same_head, upd, 0.0)
        for gi in range(group):
            both = _dot(jnp.concatenate([aro_ref[gi, 0, rows, :], aro_ref[gi, 1, rows, :]], axis=0), xp_s[gi])
            o = ms_s[gi, c:, :] + jnp.where(first_head, both[:c], both[c:])
            mean = head_mean(o)
            dlt = o - mean
            var = head_mean(dlt * dlt)
            cols = slice(gi * LANES, (gi + 1) * LANES)
            y = dlt * lax.rsqrt(var + B_GN_EPS) * lnw_ref[:, cols] + lnb_ref[:, cols]
            y = y + bonus_ref[rows, cols]
            o_ref[rows, cols] = (y * _silu(gate_ref[rows, cols])).astype(o_ref.dtype)


def rwkv7_layer(x, norm_g, mu, w_rkvg, w0, w_w1, w_w2, a0, w_a1, w_a2, k_k, k_a, r_k, ln_w, ln_b, w_out,
                batch, seq):
    m, d = x.shape
    heads = d // B_HEAD_DIM
    pairs = heads // 2
    order = jnp.array([0, 2, 3, 5, 1, 4])
    xs = rwkv_token_mix(x, norm_g, mu[order], seq)
    rkvg = grouped_matmul(xs, w_rkvg.astype(BF16), F32)
    lora = w_w1.shape[1]
    padc = lambda w: jnp.pad(w, ((0, 0), (0, LORA_PAD - lora))).astype(BF16)
    padr = lambda w: jnp.pad(w, ((0, LORA_PAD - lora), (0, 0))).astype(BF16)
    ld, a = rwkv_lora(xs, padc(w_w1), padr(w_w2), padc(w_a1), padr(w_a2), w0, a0)

    rows = min(SCAN_ROWS, seq)
    nr = seq // rows
    ncr = rows // CHUNK
    npair = rows // (2 * CHUNK)
    col = lambda g: pl.BlockSpec((None, rows, LANES), lambda b, p, i, g=g: (g, b * nr + i, p))
    flat = pl.BlockSpec((rows, LANES), lambda b, p, i: (b * nr + i, p))
    prow = pl.BlockSpec((1, LANES), lambda b, p, i: (0, p))
    bp = lambda rws, last, dt: jax.ShapeDtypeStruct((batch, pairs, rws, last), dt)
    pblk = lambda rws, last: pl.BlockSpec((None, None, rws, last), lambda b, p, i: (b, p, i, 0))
    wt_, rt_, u_, aro_, kbt_, vb_, dec_, bonus = pl.pallas_call(
        _rwkv_prep_kernel,
        out_shape=(bp(seq, LANES, BF16), bp(seq, LANES, BF16), bp(seq, LANES, F32),
                   jax.ShapeDtypeStruct((batch, pairs, 2, seq, 4 * CHUNK), BF16),
                   bp(LANES, 2 * seq, BF16), bp(seq, LANES, BF16), bp(nr * LANES, LANES, F32),
                   jax.ShapeDtypeStruct((m, d), F32)),
        grid=(batch, pairs, nr),
        in_specs=[col(0), col(1), col(2), flat, flat, prow, prow, prow],
        out_specs=(pblk(rows, LANES), pblk(rows, LANES), pblk(rows, LANES),
                   pl.BlockSpec((None, None, 2, rows, 4 * CHUNK), lambda b, p, i: (b, p, 0, i, 0)),
                   pl.BlockSpec((None, None, LANES, 2 * rows), lambda b, p, i: (b, p, 0, i)),
                   pblk(rows, LANES), pblk(LANES, LANES), flat),
        scratch_shapes=[pltpu.VMEM((npair, LANES, LANES), BF16), pltpu.VMEM((npair, LANES, LANES), BF16),
                        pltpu.VMEM((npair, LANES, LANES), BF16), pltpu.VMEM((2 * npair, LANES, LANES), BF16),
                        pltpu.VMEM((2 * npair, LANES, LANES), BF16), pltpu.VMEM((2 * npair, LANES, LANES), F32),
                        pltpu.VMEM((LANES, LANES), F32)],
        compiler_params=_params("parallel", "parallel", "parallel"),
        name="rwkv_prep",
    )(rkvg, rkvg, rkvg, ld, a, k_k.reshape(1, d), k_a.reshape(1, d), r_k.reshape(1, d))

    group = 8
    gw = group * LANES
    gblk = lambda rws, last: pl.BlockSpec((None, group, rws, last), lambda b, p, i: (b, p, i, 0))
    gflat = pl.BlockSpec((rows, gw), lambda b, p, i: (b * nr + i, p))
    grow = pl.BlockSpec((1, gw), lambda b, p, i: (0, p))
    o = pl.pallas_call(
        functools.partial(_rwkv_scan_kernel, group=group),
        out_shape=jax.ShapeDtypeStruct((m, d), BF16),
        grid=(batch, pairs // group, nr),
        in_specs=[gblk(rows, LANES), gblk(rows, LANES), gblk(rows, LANES),
                  pl.BlockSpec((None, group, 2, rows, 4 * CHUNK), lambda b, p, i: (b, p, 0, i, 0)),
                  pl.BlockSpec((None, group, LANES, 2 * rows), lambda b, p, i: (b, p, 0, i)),
                  gblk(rows, LANES), gblk(LANES, LANES), gflat,
                  pl.BlockSpec((None, rows, gw), lambda b, p, i: (3, b * nr + i, p)),
                  grow, grow],
        out_specs=gflat,
        scratch_shapes=[pltpu.VMEM((group, LANES, LANES), F32), pltpu.VMEM((group, 2 * CHUNK, LANES), F32),
                        pltpu.VMEM((group, 2 * CHUNK, LANES), BF16), pltpu.VMEM((group, 4 * CHUNK, LANES), BF16)],
        compiler_params=_params("parallel", "parallel", "arbitrary"),
        name="rwkv_scan",
    )(wt_, rt_, u_, aro_, kbt_, vb_, dec_, bonus, rkvg, ln_w.reshape(1, d), ln_b.reshape(1, d))
    return matmul_residual(o, w_out.astype(BF16), x, name="rwkv_out_proj")


def kernel(x, p, positions, norm_g, pe_norm_g, pe_w_gate, pe_w_proj, final_norm_g, a_w_in, a_lam, a_subln_g, a_w_out, b_mu, b_w_rkvg, b_w0, b_w_w1, b_w_w2, b_a0, b_w_a1, b_w_a2, b_k_k, b_k_a, b_r_k, b_ln_w, b_ln_b, b_w_out, c_w_in, c_conv_w, c_A_log, c_dt_bias, c_norm_g, c_w_out):
    batch, seq, d = x.shape
    depth = p.shape[0]
    m = batch * seq
    xf = x.reshape(m, d)
    tables = rope_tables(positions)
    for i in range(depth):
        kind = i % N_MIXERS
        j = i // N_MIXERS
        if kind == 0:
            lam_init = 0.8 - 0.6 * math.exp(-0.3 * i)
            hn = rmsnorm(xf, norm_g[i], BF16)
            xf = diff_attention_layer(xf, hn, tables, a_w_in[j], a_lam[j], a_subln_g[j], a_w_out[j],
                                      batch, seq, lam_init)
        elif kind == 1:
            xf = rwkv7_layer(xf, norm_g[i], b_mu[j], b_w_rkvg[j], b_w0[j], b_w_w1[j], b_w_w2[j], b_a0[j],
                             b_w_a1[j], b_w_a2[j], b_k_k[j], b_k_a[j], b_r_k[j], b_ln_w[j], b_ln_b[j],
                             b_w_out[j], batch, seq)
        else:
            hn = rmsnorm(xf, norm_g[i], BF16)
            xf = gated_deltanet_layer(xf, hn, c_w_in[j], c_conv_w[j], c_A_log[j], c_dt_bias[j], c_norm_g[j],
                                      c_w_out[j], batch, seq)
        hn2 = rmsnorm(xf, pe_norm_g[i], BF16)
        xf = per_layer_embedding(xf, hn2, pe_w_gate[i].astype(BF16), p[i].reshape(m, -1),
                                 pe_w_proj[i].astype(BF16))
    return rmsnorm(xf, final_norm_g, F32).reshape(batch, seq, d)
```

```python
import functools
import math

import jax
import jax.numpy as jnp
from jax import lax
from jax.experimental import pallas as pl
from jax.experimental.pallas import tpu as pltpu

F32 = jnp.float32
BF16 = jnp.bfloat16

N_MIXERS = 3
NORM_EPS = 1e-6
LANES = 128
VMEM_LIMIT = 48 * 1024 * 1024

A_HEAD_DIM = 128
A_V_DIM = 2 * A_HEAD_DIM
ROT_DIM = A_HEAD_DIM // 4
ROPE_THETA = 500000.0
SUBLN_EPS = 1e-5
ATTN_BLOCK = 256
ONES_ROWS = 16

B_HEAD_DIM = 64
B_GN_EPS = 64e-5
LORA_PAD = 128

C_HEAD_DIM = 128
C_CONV_WIDTH = 4
CHUNK = 64
SCAN_ROWS = 512
PREP_ROWS = 1024

NN = (((1,), (0,)), ((), ()))
NT = (((1,), (1,)), ((), ()))


def _dot(a, b, dims=NN):
    return lax.dot_general(a, b, dims, preferred_element_type=F32)


def _split2(x):
    hi = x.astype(BF16)
    lo = (x - hi.astype(F32)).astype(BF16)
    return hi, lo


def _split3(x):
    hi = x.astype(BF16)
    r = x - hi.astype(F32)
    mid = r.astype(BF16)
    lo = (r - mid.astype(F32)).astype(BF16)
    return hi, mid, lo


def _dot3(a, b, dims=NN):
    ah, al = _split2(a)
    bh, bl = _split2(b)
    return _dot(ah, bh, dims) + (_dot(ah, bl, dims) + _dot(al, bh, dims))


def _dot_xl(a, b_exact, dims=NN):
    h, m, l = _split3(a)
    return _dot(h, b_exact, dims) + (_dot(m, b_exact, dims) + _dot(l, b_exact, dims))


def _dot_xr(a_exact, b, dims=NN):
    h, m, l = _split3(b)
    return _dot(a_exact, h, dims) + (_dot(a_exact, m, dims) + _dot(a_exact, l, dims))


def _iota(shape, dim):
    return lax.broadcasted_iota(jnp.int32, shape, dim)


def _silu(x):
    return x * jax.nn.sigmoid(x)


def _softplus(x):
    return jnp.maximum(x, 0.0) + jnp.log(1.0 + jnp.exp(-jnp.abs(x)))


def _params(*sem):
    return pltpu.CompilerParams(dimension_semantics=sem, vmem_limit_bytes=VMEM_LIMIT)


def _inv_unit_lower(a, nilpotency):
    n = a.shape[0]
    eye = (_iota((n, n), 0) == _iota((n, n), 1)).astype(F32)
    p = -a
    t = eye + p
    for _ in range(int(math.log2(nilpotency)) - 1):
        p = _dot3(p, p)
        t = t + _dot3(t, p)
    return t


def _rmsnorm_kernel(x_ref, g_ref, o_ref, *, eps):
    x = x_ref[...]
    y = x * lax.rsqrt(jnp.mean(x * x, axis=-1, keepdims=True) + eps)
    o_ref[...] = (y * g_ref[...]).astype(o_ref.dtype)


def rmsnorm(x, g, out_dtype, *, eps=NORM_EPS, tm=512):
    m, d = x.shape
    tm = min(tm, m)
    return pl.pallas_call(
        functools.partial(_rmsnorm_kernel, eps=eps),
        out_shape=jax.ShapeDtypeStruct((m, d), out_dtype),
        grid=(m // tm,),
        in_specs=[pl.BlockSpec((tm, d), lambda i: (i, 0)), pl.BlockSpec((1, d), lambda i: (0, 0))],
        out_specs=pl.BlockSpec((tm, d), lambda i: (i, 0)),
        compiler_params=_params("parallel"),
        name="rmsnorm",
    )(x, g.reshape(1, d))


def _mm_kernel(a_ref, w_ref, *rest, epilogue):
    o_ref = rest[-1]
    acc = _dot(a_ref[...], w_ref[...])
    if epilogue is not None:
        acc = epilogue(acc, *rest[:-1])
    o_ref[...] = acc.astype(o_ref.dtype)


def matmul(a, w, out_dtype, *, tm=1024, tn=1024, extra=(), extra_specs=(), epilogue=None, name="matmul"):
    m, k = a.shape
    n = w.shape[1]
    tm, tn = min(tm, m), min(tn, n)
    return pl.pallas_call(
        functools.partial(_mm_kernel, epilogue=epilogue),
        out_shape=jax.ShapeDtypeStruct((m, n), out_dtype),
        grid=(n // tn, m // tm),
        in_specs=[pl.BlockSpec((tm, k), lambda j, i: (i, 0)),
                  pl.BlockSpec((k, tn), lambda j, i: (0, j))] + list(extra_specs),
        out_specs=pl.BlockSpec((tm, tn), lambda j, i: (i, j)),
        compiler_params=_params("parallel", "parallel"),
        name=name,
    )(a, w, *extra)


def matmul_residual(a, w, res, *, tm=1024, tn=1024, name="matmul_residual"):
    tm, tn = min(tm, a.shape[0]), min(tn, w.shape[1])
    return matmul(a, w, F32, tm=tm, tn=tn, extra=(res,),
                  extra_specs=(pl.BlockSpec((tm, tn), lambda j, i: (i, j)),),
                  epilogue=lambda acc, r_ref: r_ref[...] + acc, name=name)


def _ple_kernel(h_ref, wg_ref, p_ref, wp_ref, x_ref, o_ref):
    gate = jax.nn.sigmoid(_dot(h_ref[...], wg_ref[...]))
    proj = _dot(p_ref[...].astype(BF16), wp_ref[...])
    o_ref[...] = x_ref[...] + gate * proj


def per_layer_embedding(x, hn, w_gate, p, w_proj, *, tm=1024, tn=1024):
    m, d = x.shape
    pd = p.shape[1]
    tm, tn = min(tm, m), min(tn, d)
    return pl.pallas_call(
        _ple_kernel,
        out_shape=jax.ShapeDtypeStruct((m, d), F32),
        grid=(d // tn, m // tm),
        in_specs=[pl.BlockSpec((tm, d), lambda j, i: (i, 0)),
                  pl.BlockSpec((d, tn), lambda j, i: (0, j)),
                  pl.BlockSpec((tm, pd), lambda j, i: (i, 0)),
                  pl.BlockSpec((pd, tn), lambda j, i: (0, j)),
                  pl.BlockSpec((tm, tn), lambda j, i: (i, j))],
        out_specs=pl.BlockSpec((tm, tn), lambda j, i: (i, j)),
        compiler_params=_params("parallel", "parallel"),
        name="per_layer_embedding",
    )(hn, w_gate, p, w_proj, x)


def _rope_table_kernel(pos_ref, freq_ref, cos_ref, sin_lo_ref, sin_hi_ref):
    half = ROT_DIM // 2
    ang = pos_ref[...].astype(F32) * freq_ref[...]
    lane = _iota(ang.shape, 1)
    c, s = jnp.cos(ang), jnp.sin(ang)
    cos_ref[...] = jnp.where(lane < ROT_DIM, c, 1.0)
    sin_lo_ref[...] = jnp.where(lane < half, -s, 0.0)
    sin_hi_ref[...] = jnp.where((lane >= half) & (lane < ROT_DIM), s, 0.0)


def rope_tables(positions, *, tm=1024):
    m = positions.size
    tm = min(tm, m)
    inv_freq = ROPE_THETA ** (-jnp.arange(0, ROT_DIM, 2, dtype=F32) / ROT_DIM)
    freq_row = jnp.concatenate([inv_freq, inv_freq, jnp.zeros((LANES - ROT_DIM,), F32)]).reshape(1, LANES)
    shape = jax.ShapeDtypeStruct((m, LANES), F32)
    spec = pl.BlockSpec((tm, LANES), lambda i: (i, 0))
    return pl.pallas_call(
        _rope_table_kernel,
        out_shape=(shape, shape, shape),
        grid=(m // tm,),
        in_specs=[pl.BlockSpec((tm, 1), lambda i: (i, 0)), pl.BlockSpec((1, LANES), lambda i: (0, 0))],
        out_specs=(spec, spec, spec),
        compiler_params=_params("parallel"),
        name="rope_tables",
    )(positions.reshape(m, 1), freq_row)


def _attn_in_kernel(a_ref, w_ref, cos_ref, sin_lo_ref, sin_hi_ref, o_ref, *, n_q_blocks):
    j = pl.program_id(0)
    acc = _dot(a_ref[...], w_ref[...])
    half = ROT_DIM // 2

    @pl.when(j < 2 * n_q_blocks)
    def _():
        scale = jnp.where(j < n_q_blocks, A_HEAD_DIM ** -0.5, 1.0).astype(F32)
        cos, sin_lo, sin_hi = cos_ref[...], sin_lo_ref[...], sin_hi_ref[...]
        for g in range(acc.shape[1] // LANES):
            x = acc[:, g * LANES:(g + 1) * LANES]
            y = x * cos + pltpu.roll(x, LANES - half, 1) * sin_lo + pltpu.roll(x, half, 1) * sin_hi
            o_ref[:, g * LANES:(g + 1) * LANES] = (y * scale).astype(o_ref.dtype)

    @pl.when(j >= 2 * n_q_blocks)
    def _():
        o_ref[...] = acc.astype(o_ref.dtype)


def attn_in_proj(hn, w_in, tables, qk_width, *, tm=1024, tn=1024):
    m, k = hn.shape
    n = w_in.shape[1]
    tm, tn = min(tm, m), min(tn, qk_width)
    tspec = pl.BlockSpec((tm, LANES), lambda j, i: (i, 0))
    return pl.pallas_call(
        functools.partial(_attn_in_kernel, n_q_blocks=qk_width // tn),
        out_shape=jax.ShapeDtypeStruct((m, n), BF16),
        grid=(n // tn, m // tm),
        in_specs=[pl.BlockSpec((tm, k), lambda j, i: (i, 0)),
                  pl.BlockSpec((k, tn), lambda j, i: (0, j)), tspec, tspec, tspec],
        out_specs=pl.BlockSpec((tm, tn), lambda j, i: (i, j)),
        compiler_params=_params("parallel", "parallel"),
        name="attn_in_proj",
    )(hn, w_in, *tables)


def _attn_vt_kernel(w_ref, a_ref, o_ref):
    o_ref[...] = _dot(w_ref[...], a_ref[...], NT).astype(o_ref.dtype)


def attn_v_proj_t(hn, w_t, batch, seq, blk, *, tn=1024):
    m, k = hn.shape
    n = w_t.shape[0]
    tn = min(tn, n)
    nk = seq // blk
    return pl.pallas_call(
        _attn_vt_kernel,
        out_shape=jax.ShapeDtypeStruct((batch, nk, n, blk), BF16),
        grid=(n // tn, m // blk),
        in_specs=[pl.BlockSpec((tn, k), lambda j, i: (j, 0)),
                  pl.BlockSpec((blk, k), lambda j, i: (i, 0))],
        out_specs=pl.BlockSpec((None, None, tn, blk), lambda j, i: (i // nk, i % nk, j, 0)),
        compiler_params=_params("parallel", "parallel"),
        name="attn_v_proj_t",
    )(w_t, hn)


def _diff_attn_kernel(lam_ref, q_ref, k_ref, vt_ref, z_ref, g_ref, o_ref, m_ref, acc_ref, s_ref, *, blk, lam_init):
    i = pl.program_id(2)
    lam = lam_ref[...]
    lam_full = (jnp.exp(jnp.sum(lam[0:1] * lam[1:2], axis=-1, keepdims=True))
                - jnp.exp(jnp.sum(lam[2:3] * lam[3:4], axis=-1, keepdims=True)) + lam_init)
    m_ref[...] = jnp.full(m_ref.shape, -jnp.inf, F32)
    acc_ref[...] = jnp.zeros(acc_ref.shape, F32)
    q = q_ref[...]
    d = A_HEAD_DIM
    w = A_V_DIM
    ones = jnp.ones((ONES_ROWS, blk), BF16)

    def scores(j, slot):
        start = pl.multiple_of(j * blk, blk)
        kb = k_ref[pl.ds(start, blk), :]
        for c in range(2):
            s_ref[slot, c] = _dot(kb[:, c * d:(c + 1) * d], q[:, c * d:(c + 1) * d], NT)

    def absorb(j, slot, masked):
        vt = jnp.concatenate([vt_ref[j], ones], axis=0)
        for c in range(2):
            s = s_ref[slot, c]
            if masked:
                s = jnp.where(_iota(s.shape, 0) <= _iota(s.shape, 1), s, -jnp.inf)
            m_prev = m_ref[c]
            m_new = jnp.maximum(m_prev, jnp.max(s, axis=0, keepdims=True))
            alpha = jnp.exp(m_prev - m_new)
            p = jnp.exp(s - m_new)
            acc_ref[c] = alpha * acc_ref[c] + _dot(vt, p.astype(BF16))
            m_ref[c] = m_new

    scores(0, 0)

    def body(jj, carry):
        scores(2 * jj + 1, 1)
        absorb(2 * jj, 0, False)
        scores(2 * jj + 2, 0)
        absorb(2 * jj + 1, 1, False)
        return carry

    lax.fori_loop(0, i // 2, body, 0)

    @pl.when(i % 2 == 1)
    def _():
        scores(i, 1)
        absorb(i - 1, 0, False)
        absorb(i, 1, True)

    @pl.when(i % 2 == 0)
    def _():
        absorb(i, 0, True)

    o = (acc_ref[0, :w, :] / acc_ref[0, w:w + 1, :]
         - lam_full * (acc_ref[1, :w, :] / acc_ref[1, w:w + 1, :]))
    o = o * lax.rsqrt(jnp.mean(o * o, axis=0, keepdims=True) + SUBLN_EPS) * g_ref[...]
    o = (o * (1.0 - lam_init)).T
    o_ref[...] = (o * _silu(z_ref[...].astype(F32))).astype(o_ref.dtype)


def diff_attention_core(qkz, vt, lam, subln_g, batch, seq, heads, lam_init):
    m = qkz.shape[0]
    blk = vt.shape[-1]
    nq = seq // blk
    w = A_V_DIM
    return pl.pallas_call(
        functools.partial(_diff_attn_kernel, blk=blk, lam_init=lam_init),
        out_shape=jax.ShapeDtypeStruct((m, heads * w), BF16),
        grid=(batch, heads, nq),
        in_specs=[pl.BlockSpec((4, A_HEAD_DIM), lambda b, h, i: (0, 0)),
                  pl.BlockSpec((blk, w), lambda b, h, i: (b * nq + i, h)),
                  pl.BlockSpec((seq, w), lambda b, h, i: (b, heads + h)),
                  pl.BlockSpec((None, nq, w, blk), lambda b, h, i: (b, 0, h, 0)),
                  pl.BlockSpec((blk, w), lambda b, h, i: (b * nq + i, 2 * heads + h)),
                  pl.BlockSpec((w, 1), lambda b, h, i: (0, 0))],
        out_specs=pl.BlockSpec((blk, w), lambda b, h, i: (b * nq + i, h)),
        scratch_shapes=[pltpu.VMEM((2, 1, blk), F32), pltpu.VMEM((2, w + ONES_ROWS, blk), F32),
                        pltpu.VMEM((2, 2, blk, blk), F32)],
        compiler_params=_params("parallel", "parallel", "parallel"),
        name="diff_attention",
    )(lam, qkz, qkz, vt, qkz, subln_g.reshape(w, 1))


def diff_attention_layer(x, hn, tables, w_in, lam, subln_g, w_out, batch, seq, lam_init):
    d = x.shape[1]
    heads = d // A_V_DIM
    qk_w = heads * 2 * A_HEAD_DIM
    v_w = heads * A_V_DIM
    blk = min(ATTN_BLOCK, seq)
    w_qkz = jnp.concatenate([w_in[:, :2 * qk_w], w_in[:, 2 * qk_w + v_w:]], axis=1).astype(BF16)
    w_vt = w_in[:, 2 * qk_w:2 * qk_w + v_w].T.astype(BF16)
    qkz = attn_in_proj(hn, w_qkz, tables, qk_w)
    vt = attn_v_proj_t(hn, w_vt, batch, seq, blk)
    o = diff_attention_core(qkz, vt, lam, subln_g, batch, seq, heads, lam_init)
    return matmul_residual(o, w_out.astype(BF16), x, name="attn_out_proj")


def _gdn_in_kernel(a_ref, w_ref, cw_ref, o_ref, tail_ref, *, n_conv_blocks, rows_per_seq):
    j = pl.program_id(0)
    i = pl.program_id(1)
    acc = _dot(a_ref[...], w_ref[...])
    tm = acc.shape[0]

    @pl.when(j < n_conv_blocks)
    def _():
        @pl.when((i * tm) % rows_per_seq == 0)
        def _():
            tail_ref[...] = jnp.zeros(tail_ref.shape, F32)

        tail = tail_ref[...]
        sub = _iota(tail.shape, 0)

        def shifted(s):
            xs = pltpu.roll(acc, s, 0)
            head = jnp.where(sub < s, pltpu.roll(tail, s, 0), xs[:8])
            return jnp.concatenate([head, xs[8:]], axis=0)

        cw = cw_ref[...]
        last = C_CONV_WIDTH - 1
        y = shifted(last) * cw[0:1]
        for t in range(1, last):
            y = y + shifted(last - t) * cw[t:t + 1]
        y = y + acc * cw[last:last + 1]
        tail_ref[...] = acc[tm - 8:]
        o_ref[...] = _silu(y).astype(o_ref.dtype)

    @pl.when(j >= n_conv_blocks)
    def _():
        o_ref[...] = acc.astype(o_ref.dtype)


def gdn_in_proj(hn, w, conv_w, seq, *, tm=1024, tn=1024):
    m, k = hn.shape
    n = w.shape[1]
    conv_ch = conv_w.shape[1]
    tm, tn = min(tm, seq), min(tn, n)
    ncb = conv_ch // tn
    return pl.pallas_call(
        functools.partial(_gdn_in_kernel, n_conv_blocks=ncb, rows_per_seq=seq),
        out_shape=jax.ShapeDtypeStruct((m, n), BF16),
        grid=(n // tn, m // tm),
        in_specs=[pl.BlockSpec((tm, k), lambda j, i: (i, 0)),
                  pl.BlockSpec((k, tn), lambda j, i: (0, j)),
                  pl.BlockSpec((C_CONV_WIDTH, tn), lambda j, i: (0, jnp.minimum(j, ncb - 1)))],
        out_specs=pl.BlockSpec((tm, tn), lambda j, i: (i, j)),
        scratch_shapes=[pltpu.VMEM((8, tn), F32)],
        compiler_params=_params("arbitrary", "arbitrary"),
        name="gdn_in_proj",
    )(hn, w, conv_w)


def _pair_masks():
    n = 2 * CHUNK
    r, c = _iota((n, n), 0), _iota((n, n), 1)
    same = (r // CHUNK) == (c // CHUNK)
    return same & (r >= c), same & (r > c), r == c


def _dot_x2l(a, b_exact, dims=NN):
    h, l = _split2(a)
    return _dot(h, b_exact, dims) + _dot(l, b_exact, dims)


def _dot_x2r(a_exact, b, dims=NN):
    h, l = _split2(b)
    return _dot(a_exact, h, dims) + _dot(a_exact, l, dims)


def _inverse_stages(p_ref, t_ref, count, nilpotency):
    for _ in range(int(math.log2(nilpotency)) - 1):
        for i in range(count):
            p_ref[i] = _dot(p_ref[i], p_ref[i]).astype(BF16)
        for i in range(count):
            t = t_ref[i]
            t_ref[i] = t + _dot(t.astype(BF16), p_ref[i])


def _gdn_gates_kernel(a_ref, w_ref, alog_ref, dtb_ref, beta_ref, gc_ref):
    acc = _dot(a_ref[...], w_ref[...])
    beta_ref[...] = jax.nn.sigmoid(acc[:, :LANES]).astype(beta_ref.dtype)
    g = -jnp.exp(alog_ref[...]) * _softplus(acc[:, LANES:] + dtb_ref[...])
    tri_b = _pair_masks()[0].astype(BF16)
    n2 = 2 * CHUNK
    for n in range(g.shape[0] // n2):
        rows = slice(n * n2, (n + 1) * n2)
        gc_ref[rows, :] = _dot_xr(tri_b, g[rows])


def gdn_gates(hn, w_ba, alog_row, dtb_row, *, tm=512):
    m, k = hn.shape
    tm = min(tm, m)
    row = pl.BlockSpec((1, LANES), lambda i: (0, 0))
    out = pl.BlockSpec((tm, LANES), lambda i: (i, 0))
    return pl.pallas_call(
        _gdn_gates_kernel,
        out_shape=(jax.ShapeDtypeStruct((m, LANES), BF16), jax.ShapeDtypeStruct((m, LANES), F32)),
        grid=(m // tm,),
        in_specs=[pl.BlockSpec((tm, k), lambda i: (i, 0)), pl.BlockSpec((k, 2 * LANES), lambda i: (0, 0)), row, row],
        out_specs=(out, out),
        compiler_params=_params("parallel"),
        name="gdn_gates",
    )(hn, w_ba, alog_row, dtb_row)


def _gdn_prep_kernel(q_ref, k_ref, v_ref, beta_ref, gc_ref,
                     w_ref, qg_ref, u_ref, att_ref, kdt_ref, dec_ref,
                     g_s, dec_s, kn_s, kb_s, rhs_s, p_s, t_s):
    h = pl.program_id(1)
    c = CHUNK
    n2 = 2 * c
    dk = C_HEAD_DIM
    npairs = q_ref.shape[0] // n2
    tri, strict, diag = _pair_masks()
    eye_b = diag.astype(BF16)
    eye_f = diag.astype(F32)
    sel = (_iota((LANES, dk), 0) == h).astype(BF16)
    own_lane = _iota((n2, LANES), 1) == h
    own_b = own_lane.astype(BF16)
    upper_half = _iota((n2, dk), 0) < c
    pairs = [slice(n * n2, (n + 1) * n2) for n in range(npairs)]

    for n, rows in enumerate(pairs):
        gc_all = gc_ref[rows, :]
        gc = jnp.broadcast_to(jnp.sum(jnp.where(own_lane, gc_all, 0.0), axis=-1, keepdims=True), (n2, dk))
        gc_row = _dot_x2r(own_b, gc_all, NT)
        g_s[n] = gc
        dec_s[n] = jnp.where(tri, jnp.exp(jnp.where(tri, gc - gc_row, 0.0)), 0.0)
    for n, rows in enumerate(pairs):
        gc = g_s[n]
        qf = q_ref[rows, :].astype(F32)
        kf = k_ref[rows, :].astype(F32)
        vf = v_ref[rows, :].astype(F32)
        qn = qf * lax.rsqrt(jnp.sum(qf * qf, axis=-1, keepdims=True) + 1e-6) * (dk ** -0.5)
        kn = kf * lax.rsqrt(jnp.sum(kf * kf, axis=-1, keepdims=True) + 1e-6)
        beta = _dot(beta_ref[rows, :], sel)
        egc = jnp.exp(gc)
        kb = kn * beta
        kn_b = kn.astype(BF16)
        qn_b = qn.astype(BF16)
        kn_s[n] = kn_b
        kb_s[n] = kb.astype(BF16)
        rhs_s[n] = jnp.concatenate([vf * beta, kb * egc], axis=1).astype(BF16)
        g_last = jnp.where(upper_half, gc[c - 1:c, :], gc[n2 - 1:n2, :])
        kdec = kn * jnp.exp(g_last - gc)
        qg_ref[rows, :] = (qn * egc).astype(BF16)
        att_ref[rows, :] = (_dot(qn_b, kn_b, NT) * dec_s[n]).astype(BF16)
        kdt_ref[:, rows] = _dot(eye_b, kdec.astype(BF16), NT).astype(BF16)
        dec_ref[2 * n:2 * n + 1, :] = jnp.exp(gc[c - 1:c, :])
        dec_ref[2 * n + 1:2 * n + 2, :] = jnp.exp(gc[n2 - 1:n2, :])
    for n, rows in enumerate(pairs):
        neg_a = jnp.where(strict, -(_dot(kb_s[n], kn_s[n], NT) * dec_s[n]), 0.0)
        p_s[n] = neg_a.astype(BF16)
        t_s[n] = eye_f + neg_a
    _inverse_stages(p_s, t_s, npairs, c)
    for n, rows in enumerate(pairs):
        sol = _dot(t_s[n].astype(BF16), rhs_s[n])
        u_ref[rows, :] = sol[:, :dk]
        w_ref[rows, :] = sol[:, dk:].astype(BF16)


def _gdn_scan_kernel(w_ref, qg_ref, u_ref, att_ref, kdt_ref, dec_ref, z_ref, g_ref, o_ref,
                     s_ref, ms_s, vp_s, *, group):
    c = CHUNK
    dk = C_HEAD_DIM
    zeros = jnp.zeros((c, dk), BF16)

    @pl.when(pl.program_id(2) == 0)
    def _():
        s_ref[...] = jnp.zeros(s_ref.shape, F32)

    for n in range(w_ref.shape[1] // c):
        rows = slice(n * c, (n + 1) * c)
        pair = slice((n // 2) * 2 * c, (n // 2 + 1) * 2 * c)
        for gi in range(group):
            lhs = jnp.concatenate([w_ref[gi, rows, :], qg_ref[gi, rows, :]], axis=0)
            ms_s[gi] = _dot(lhs, s_ref[gi].astype(BF16))
        for gi in range(group):
            v_new = (u_ref[gi, rows, :] - ms_s[gi, :c, :]).astype(BF16)
            vp_s[gi] = jnp.concatenate([v_new, zeros] if n % 2 == 0 else [zeros, v_new], axis=0)
        for gi in range(group):
            s_ref[gi] = s_ref[gi] * dec_ref[gi, n:n + 1, :] + _dot(kdt_ref[gi, :, pair], vp_s[gi])
        for gi in range(group):
            o = ms_s[gi, c:, :] + _dot(att_ref[gi, rows, :], vp_s[gi])
            o = o * lax.rsqrt(jnp.mean(o * o, axis=-1, keepdims=True) + NORM_EPS) * g_ref[...]
            z = z_ref[rows, gi * dk:(gi + 1) * dk].astype(F32)
            o_ref[rows, gi * dk:(gi + 1) * dk] = (o * _silu(z)).astype(o_ref.dtype)


def gated_deltanet_layer(x, hn, w_in, conv_w, a_log, dt_bias, norm_g, w_out, batch, seq):
    m, d = x.shape
    dk = C_HEAD_DIM
    k_heads = d // dk
    v_heads = 2 * k_heads
    conv_ch = 2 * k_heads * dk + v_heads * dk
    main_w = conv_ch + v_heads * dk
    qkvz = gdn_in_proj(hn, w_in[:, :main_w].astype(BF16), conv_w, seq)
    pad = jnp.zeros((d, LANES - v_heads), F32)
    w_ba = jnp.concatenate([w_in[:, main_w:main_w + v_heads], pad, w_in[:, main_w + v_heads:], pad], axis=1)
    row_pad = jnp.zeros((LANES - v_heads,), F32)
    alog_row = jnp.concatenate([a_log, row_pad]).reshape(1, LANES)
    dtb_row = jnp.concatenate([dt_bias, row_pad]).reshape(1, LANES)
    beta_all, gc_all = gdn_gates(hn, w_ba.astype(BF16), alog_row, dtb_row)

    rows = min(PREP_ROWS, seq)
    nr = seq // rows
    ncr = rows // CHUNK
    npair = rows // (2 * CHUNK)
    hv = v_heads
    bh_t = lambda dt, last: jax.ShapeDtypeStruct((batch, hv, seq, last), dt)
    blk4 = lambda last: pl.BlockSpec((None, None, rows, last), lambda b, h, i: (b, h, i, 0))
    qoff, koff, voff = 0, k_heads, 2 * k_heads
    w_, qg_, u_, att_, kdt_, dec_ = pl.pallas_call(
        _gdn_prep_kernel,
        out_shape=(bh_t(BF16, dk), bh_t(BF16, dk), bh_t(F32, dk), bh_t(BF16, 2 * CHUNK),
                   jax.ShapeDtypeStruct((batch, hv, dk, seq), BF16),
                   jax.ShapeDtypeStruct((batch, hv, seq // CHUNK, dk), F32)),
        grid=(batch, hv, nr),
        in_specs=[pl.BlockSpec((rows, dk), lambda b, h, i: (b * nr + i, qoff + h // 2)),
                  pl.BlockSpec((rows, dk), lambda b, h, i: (b * nr + i, koff + h // 2)),
                  pl.BlockSpec((rows, dk), lambda b, h, i: (b * nr + i, voff + h)),
                  pl.BlockSpec((rows, LANES), lambda b, h, i: (b * nr + i, 0)),
                  pl.BlockSpec((rows, LANES), lambda b, h, i: (b * nr + i, 0))],
        out_specs=(blk4(dk), blk4(dk), blk4(dk), blk4(2 * CHUNK),
                   pl.BlockSpec((None, None, dk, rows), lambda b, h, i: (b, h, 0, i)),
                   pl.BlockSpec((None, None, ncr, dk), lambda b, h, i: (b, h, i, 0))),
        scratch_shapes=[pltpu.VMEM((npair, dk, dk), F32), pltpu.VMEM((npair, dk, dk), F32),
                        pltpu.VMEM((npair, dk, dk), BF16), pltpu.VMEM((npair, dk, dk), BF16),
                        pltpu.VMEM((npair, dk, 2 * dk), BF16), pltpu.VMEM((npair, dk, dk), BF16),
                        pltpu.VMEM((npair, dk, dk), F32)],
        compiler_params=_params("parallel", "parallel", "parallel"),
        name="gdn_prep",
    )(qkvz, qkvz, qkvz, beta_all, gc_all)

    rows = min(SCAN_ROWS, seq)
    nr = seq // rows
    ncr = rows // CHUNK
    group = 8
    zoff = conv_ch // (group * dk)
    gblk = lambda last: pl.BlockSpec((None, group, rows, last), lambda b, h, i: (b, h, i, 0))
    o = pl.pallas_call(
        functools.partial(_gdn_scan_kernel, group=group),
        out_shape=jax.ShapeDtypeStruct((m, hv * dk), BF16),
        grid=(batch, hv // group, nr),
        in_specs=[gblk(dk), gblk(dk), gblk(dk), gblk(2 * CHUNK),
                  pl.BlockSpec((None, group, dk, rows), lambda b, h, i: (b, h, 0, i)),
                  pl.BlockSpec((None, group, ncr, dk), lambda b, h, i: (b, h, i, 0)),
                  pl.BlockSpec((rows, group * dk), lambda b, h, i: (b * nr + i, zoff + h)),
                  pl.BlockSpec((1, dk), lambda b, h, i: (0, 0))],
        out_specs=pl.BlockSpec((rows, group * dk), lambda b, h, i: (b * nr + i, h)),
        scratch_shapes=[pltpu.VMEM((group, dk, dk), F32), pltpu.VMEM((group, 2 * CHUNK, dk), F32),
                        pltpu.VMEM((group, 2 * CHUNK, dk), BF16)],
        compiler_params=_params("parallel", "parallel", "arbitrary"),
        name="gdn_scan",
    )(w_, qg_, u_, att_, kdt_, dec_, qkvz, norm_g.reshape(1, dk))
    return matmul_residual(o, w_out.astype(BF16), x, tm=512, name="gdn_out_proj")


def _rwkv_mix_kernel(x_ref, g_ref, mu_ref, o_ref, tail_ref, *, rows_per_seq):
    i = pl.program_id(0)
    x = x_ref[...]
    tm = x.shape[0]
    hn = x * lax.rsqrt(jnp.mean(x * x, axis=-1, keepdims=True) + NORM_EPS) * g_ref[...]

    @pl.when((i * tm) % rows_per_seq == 0)
    def _():
        tail_ref[...] = jnp.zeros(tail_ref.shape, F32)

    prev = jnp.where(_iota(hn.shape, 0) == 0, tail_ref[7:8, :], pltpu.roll(hn, 1, 0))
    tail_ref[...] = hn[tm - 8:]
    xx = prev - hn
    for c in range(o_ref.shape[0]):
        o_ref[c] = (hn + xx * mu_ref[c:c + 1, :]).astype(o_ref.dtype)


def rwkv_token_mix(x, norm_g, mu, seq, *, tm=256):
    m, d = x.shape
    tm = min(tm, seq)
    nmix = mu.shape[0]
    return pl.pallas_call(
        functools.partial(_rwkv_mix_kernel, rows_per_seq=seq),
        out_shape=jax.ShapeDtypeStruct((nmix, m, d), BF16),
        grid=(m // tm,),
        in_specs=[pl.BlockSpec((tm, d), lambda i: (i, 0)),
                  pl.BlockSpec((1, d), lambda i: (0, 0)),
                  pl.BlockSpec((nmix, d), lambda i: (0, 0))],
        out_specs=pl.BlockSpec((nmix, tm, d), lambda i: (0, i, 0)),
        scratch_shapes=[pltpu.VMEM((8, d), F32)],
        compiler_params=_params("arbitrary"),
        name="rwkv_token_mix",
    )(x, norm_g.reshape(1, d), mu)


def _grouped_mm_kernel(a_ref, w_ref, o_ref):
    o_ref[...] = _dot(a_ref[...], w_ref[...]).astype(o_ref.dtype)


def grouped_matmul(a, w, out_dtype, *, tm=1024, tn=1024):
    g, k, n = w.shape
    m = a.shape[1]
    tm, tn = min(tm, m), min(tn, n)
    return pl.pallas_call(
        _grouped_mm_kernel,
        out_shape=jax.ShapeDtypeStruct((g, m, n), out_dtype),
        grid=(g, n // tn, m // tm),
        in_specs=[pl.BlockSpec((None, tm, k), lambda c, j, i: (c, i, 0)),
                  pl.BlockSpec((None, k, tn), lambda c, j, i: (c, 0, j))],
        out_specs=pl.BlockSpec((None, tm, tn), lambda c, j, i: (c, i, j)),
        compiler_params=_params("parallel", "parallel", "parallel"),
        name="rwkv_rkvg_proj",
    )(a, w)


def _rwkv_lora_kernel(xw_ref, xa_ref, w1_ref, w2_ref, a1_ref, a2_ref, w0_ref, a0_ref, ld_ref, a_ref):
    hw = jnp.tanh(_dot(xw_ref[...], w1_ref[...])).astype(BF16)
    lw = w0_ref[...] + _dot(hw, w2_ref[...])
    log_w = -_softplus(-lw) - 0.5
    ld_ref[...] = -jnp.exp(log_w)
    ha = _dot(xa_ref[...], a1_ref[...]).astype(BF16)
    a_ref[...] = jax.nn.sigmoid(a0_ref[...] + _dot(ha, a2_ref[...]))


def rwkv_lora(xs, w1, w2, a1, a2, w0, a0, *, tm=512):
    _, m, d = xs.shape
    tm = min(tm, m)
    r = w1.shape[1]
    row = pl.BlockSpec((1, d), lambda i: (0, 0))
    shape = jax.ShapeDtypeStruct((m, d), F32)
    return pl.pallas_call(
        _rwkv_lora_kernel,
        out_shape=(shape, shape),
        grid=(m // tm,),
        in_specs=[pl.BlockSpec((None, tm, d), lambda i: (4, i, 0)),
                  pl.BlockSpec((None, tm, d), lambda i: (5, i, 0)),
                  pl.BlockSpec((d, r), lambda i: (0, 0)), pl.BlockSpec((r, d), lambda i: (0, 0)),
                  pl.BlockSpec((d, r), lambda i: (0, 0)), pl.BlockSpec((r, d), lambda i: (0, 0)),
                  row, row],
        out_specs=(pl.BlockSpec((tm, d), lambda i: (i, 0)), pl.BlockSpec((tm, d), lambda i: (i, 0))),
        compiler_params=_params("parallel"),
        name="rwkv_lora",
    )(xs, xs, w1, w2, a1, a2, w0.reshape(1, d), a0.reshape(1, d))


def _rwkv_prep_kernel(r_ref, k_ref, v_ref, ld_ref, a_ref, kk_ref, ka_ref, rk_ref,
                      wt_ref, rt_ref, u_ref, aro_ref, kbt_ref, vb_ref, dec_ref, bonus_ref,
                      kkt_s, kbar_s, bbar_s, akk_s, p_s, t_s, drow_s):
    c = CHUNK
    n2 = 2 * c
    n_h = B_HEAD_DIM
    npairs = r_ref.shape[0] // n2
    lane_head = _iota((1, LANES), 1) // n_h
    seg = (_iota((LANES, LANES), 0) // n_h == _iota((LANES, LANES), 1) // n_h).astype(BF16)
    tri, strict, diag = _pair_masks()
    tri_b = tri.astype(BF16)
    eye_b = diag.astype(BF16)
    eye_f = diag.astype(F32)
    upper_half = _iota((n2, LANES), 0) < c
    pairs = [slice(n * n2, (n + 1) * n2) for n in range(npairs)]

    r = r_ref[...]
    k = k_ref[...]
    v = v_ref[...]
    a = a_ref[...]
    kk_raw = k * kk_ref[...]
    kk = kk_raw * lax.rsqrt(_dot_xl(kk_raw * kk_raw, seg) + 1e-6)
    k2 = k * (1.0 + (a - 1.0) * ka_ref[...])
    bb = kk * a
    bonus_ref[...] = _dot_xl(r * k2 * rk_ref[...], seg) * v
    vb_ref[...] = v.astype(BF16)
    for n, rows in enumerate(pairs):
        ld = ld_ref[rows, :]
        cs = _dot_xr(tri_b, ld)
        cs_last = jnp.where(upper_half, cs[c - 1:c, :], cs[n2 - 1:n2, :])
        w_inv = jnp.exp(-cs)
        w_end = jnp.exp(cs_last - cs)
        rt_ref[rows, :] = (r[rows] * jnp.exp(cs)).astype(BF16)
        kkt_s[n] = (kk[rows] * jnp.exp(cs - ld)).astype(BF16)
        k2c, bbc = k2[rows], bb[rows]
        kbar_s[n] = (k2c * w_inv).astype(BF16)
        bbar_s[n] = (bbc * w_inv).astype(BF16)
        kw, bw = (k2c * w_end).astype(BF16), (-(bbc * w_end)).astype(BF16)
        for half in range(2):
            hs = slice(half * c, (half + 1) * c)
            kb_end = jnp.concatenate([kw[hs], bw[hs]], axis=0)
            cols = slice((2 * n + half) * n2, (2 * n + half + 1) * n2)
            kbt_ref[:, cols] = _dot(eye_b, kb_end, NT).astype(BF16)
        drow_s[2 * n:2 * n + 1, :] = jnp.exp(cs[c - 1:c, :])
        drow_s[2 * n + 1:2 * n + 2, :] = jnp.exp(cs[n2 - 1:n2, :])
    drow_s[2 * npairs:, :] = jnp.zeros((LANES - 2 * npairs, LANES), F32)
    dec_ref[...] = _dot_xr(eye_b, drow_s[...], NT)
    zero_b = jnp.zeros((n2, LANES), BF16)
    for n, rows in enumerate(pairs):
        for hh in range(2):
            own = lane_head == hh
            i = 2 * n + hh
            kkt_m = jnp.where(own, kkt_s[n], zero_b)
            rt_m = jnp.where(own, rt_ref[rows, :], zero_b)
            neg_a = jnp.where(strict, -_dot(kkt_m, bbar_s[n], NT), 0.0)
            p_s[i] = neg_a.astype(BF16)
            t_s[i] = eye_f + neg_a
            akk_s[i] = jnp.where(strict, _dot(kkt_m, kbar_s[n], NT), 0.0).astype(BF16)
            a_rk = jnp.where(tri, _dot(rt_m, kbar_s[n], NT), 0.0)
            a_rb = jnp.where(tri, _dot(rt_m, bbar_s[n], NT), 0.0)
            aro_ref[hh, rows, :] = jnp.concatenate([a_rk, -a_rb], axis=1).astype(BF16)
    _inverse_stages(p_s, t_s, 2 * npairs, c)
    for n, rows in enumerate(pairs):
        for hh in range(2):
            i = 2 * n + hh
            p_s[i] = _dot(akk_s[i], vb_ref[rows, :]).astype(BF16)
    for n, rows in enumerate(pairs):
        t0, t1 = t_s[2 * n].astype(BF16), t_s[2 * n + 1].astype(BF16)
        own0 = lane_head == 0
        wt_ref[rows, :] = jnp.where(own0, _dot(t0, kkt_s[n]), _dot(t1, kkt_s[n])).astype(BF16)
        u_ref[rows, :] = jnp.where(own0, _dot(t0, p_s[2 * n]), _dot(t1, p_s[2 * n + 1]))


def _rwkv_scan_kernel(wt_ref, rt_ref, u_ref, aro_ref, kbt_ref, vb_ref, dec_ref, bonus_ref, gate_ref,
                      lnw_ref, lnb_ref, o_ref, s_ref, ms_s, xc_s, xp_s, *, group):
    c = CHUNK
    n2 = 2 * c
    n_h = B_HEAD_DIM
    first_head = _iota((1, LANES), 1) < n_h
    same_head = _iota((LANES, LANES), 0) // n_h == _iota((LANES, LANES), 1) // n_h
    zeros = jnp.zeros((c, LANES), BF16)

    def head_mean(x):
        tot = jnp.sum(x, axis=-1, keepdims=True)
        lo = jnp.sum(jnp.where(first_head, x, 0.0), axis=-1, keepdims=True)
        return jnp.where(first_head, lo, tot - lo) * (1.0 / n_h)

    @pl.when(pl.program_id(2) == 0)
    def _():
        s_ref[...] = jnp.zeros(s_ref.shape, F32)

    for n in range(u_ref.shape[1] // c):
        rows = slice(n * c, (n + 1) * c)
        for gi in range(group):
            lhs = jnp.concatenate([wt_ref[gi, rows, :], rt_ref[gi, rows, :]], axis=0)
            ms_s[gi] = _dot(lhs, s_ref[gi].astype(BF16))
        for gi in range(group):
            sa = (u_ref[gi, rows, :] + ms_s[gi, :c, :]).astype(BF16)
            vb = vb_ref[gi, rows, :]
            xc_s[gi] = jnp.concatenate([vb, sa], axis=0)
            xp_s[gi] = jnp.concatenate([vb, zeros, sa, zeros] if n % 2 == 0 else [zeros, vb, zeros, sa], axis=0)
        for gi in range(group):
            upd = _dot(kbt_ref[gi, :, n * n2:(n + 1) * n2], xc_s[gi])
            dec_col = jnp.broadcast_to(dec_ref[gi, :, n:n + 1], (LANES, LANES))
            s_ref[gi] = s_ref[gi] * dec_col + jnp.where(same_head, upd, 0.0)
        for gi in range(group):
            both = _dot(jnp.concatenate([aro_ref[gi, 0, rows, :], aro_ref[gi, 1, rows, :]], axis=0), xp_s[gi])
            o = ms_s[gi, c:, :] + jnp.where(first_head, both[:c], both[c:])
            mean = head_mean(o)
            dlt = o - mean
            var = head_mean(dlt * dlt)
            cols = slice(gi * LANES, (gi + 1) * LANES)
            y = dlt * lax.rsqrt(var + B_GN_EPS) * lnw_ref[:, cols] + lnb_ref[:, cols]
            y = y + bonus_ref[rows, cols]
            o_ref[rows, cols] = (y * _silu(gate_ref[rows, cols])).astype(o_ref.dtype)


def rwkv7_layer(x, norm_g, mu, w_rkvg, w0, w_w1, w_w2, a0, w_a1, w_a2, k_k, k_a, r_k, ln_w, ln_b, w_out,
                batch, seq):
    m, d = x.shape
    heads = d // B_HEAD_DIM
    pairs = heads // 2
    order = jnp.array([0, 2, 3, 5, 1, 4])
    xs = rwkv_token_mix(x, norm_g, mu[order], seq)
    rkvg = grouped_matmul(xs, w_rkvg.astype(BF16), F32)
    lora = w_w1.shape[1]
    padc = lambda w: jnp.pad(w, ((0, 0), (0, LORA_PAD - lora))).astype(BF16)
    padr = lambda w: jnp.pad(w, ((0, LORA_PAD - lora), (0, 0))).astype(BF16)
    ld, a = rwkv_lora(xs, padc(w_w1), padr(w_w2), padc(w_a1), padr(w_a2), w0, a0)

    rows = min(SCAN_ROWS, seq)
    nr = seq // rows
    ncr = rows // CHUNK
    npair = rows // (2 * CHUNK)
    col = lambda g: pl.BlockSpec((None, rows, LANES), lambda b, p, i, g=g: (g, b * nr + i, p))
    flat = pl.BlockSpec((rows, LANES), lambda b, p, i: (b * nr + i, p))
    prow = pl.BlockSpec((1, LANES), lambda b, p, i: (0, p))
    bp = lambda rws, last, dt: jax.ShapeDtypeStruct((batch, pairs, rws, last), dt)
    pblk = lambda rws, last: pl.BlockSpec((None, None, rws, last), lambda b, p, i: (b, p, i, 0))
    wt_, rt_, u_, aro_, kbt_, vb_, dec_, bonus = pl.pallas_call(
        _rwkv_prep_kernel,
        out_shape=(bp(seq, LANES, BF16), bp(seq, LANES, BF16), bp(seq, LANES, F32),
                   jax.ShapeDtypeStruct((batch, pairs, 2, seq, 4 * CHUNK), BF16),
                   bp(LANES, 2 * seq, BF16), bp(seq, LANES, BF16), bp(nr * LANES, LANES, F32),
                   jax.ShapeDtypeStruct((m, d), F32)),
        grid=(batch, pairs, nr),
        in_specs=[col(0), col(1), col(2), flat, flat, prow, prow, prow],
        out_specs=(pblk(rows, LANES), pblk(rows, LANES), pblk(rows, LANES),
                   pl.BlockSpec((None, None, 2, rows, 4 * CHUNK), lambda b, p, i: (b, p, 0, i, 0)),
                   pl.BlockSpec((None, None, LANES, 2 * rows), lambda b, p, i: (b, p, 0, i)),
                   pblk(rows, LANES), pblk(LANES, LANES), flat),
        scratch_shapes=[pltpu.VMEM((npair, LANES, LANES), BF16), pltpu.VMEM((npair, LANES, LANES), BF16),
                        pltpu.VMEM((npair, LANES, LANES), BF16), pltpu.VMEM((2 * npair, LANES, LANES), BF16),
                        pltpu.VMEM((2 * npair, LANES, LANES), BF16), pltpu.VMEM((2 * npair, LANES, LANES), F32),
                        pltpu.VMEM((LANES, LANES), F32)],
        compiler_params=_params("parallel", "parallel", "parallel"),
        name="rwkv_prep",
    )(rkvg, rkvg, rkvg, ld, a, k_k.reshape(1, d), k_a.reshape(1, d), r_k.reshape(1, d))

    group = 8
    gw = group * LANES
    gblk = lambda rws, last: pl.BlockSpec((None, group, rws, last), lambda b, p, i: (b, p, i, 0))
    gflat = pl.BlockSpec((rows, gw), lambda b, p, i: (b * nr + i, p))
    grow = pl.BlockSpec((1, gw), lambda b, p, i: (0, p))
    o = pl.pallas_call(
        functools.partial(_rwkv_scan_kernel, group=group),
        out_shape=jax.ShapeDtypeStruct((m, d), BF16),
        grid=(batch, pairs // group, nr),
        in_specs=[gblk(rows, LANES), gblk(rows, LANES), gblk(rows, LANES),
                  pl.BlockSpec((None, group, 2, rows, 4 * CHUNK), lambda b, p, i: (b, p, 0, i, 0)),
                  pl.BlockSpec((None, group, LANES, 2 * rows), lambda b, p, i: (b, p, 0, i)),
                  gblk(rows, LANES), gblk(LANES, LANES), gflat,
                  pl.BlockSpec((None, rows, gw), lambda b, p, i: (3, b * nr + i, p)),
                  grow, grow],
        out_specs=gflat,
        scratch_shapes=[pltpu.VMEM((group, LANES, LANES), F32), pltpu.VMEM((group, 2 * CHUNK, LANES), F32),
                        pltpu.VMEM((group, 2 * CHUNK, LANES), BF16), pltpu.VMEM((group, 4 * CHUNK, LANES), BF16)],
        compiler_params=_params("parallel", "parallel", "arbitrary"),
        name="rwkv_scan",
    )(wt_, rt_, u_, aro_, kbt_, vb_, dec_, bonus, rkvg, ln_w.reshape(1, d), ln_b.reshape(1, d))
    return matmul_residual(o, w_out.astype(BF16), x, name="rwkv_out_proj")


def kernel(x, p, positions, norm_g, pe_norm_g, pe_w_gate, pe_w_proj, final_norm_g, a_w_in, a_lam, a_subln_g, a_w_out, b_mu, b_w_rkvg, b_w0, b_w_w1, b_w_w2, b_a0, b_w_a1, b_w_a2, b_k_k, b_k_a, b_r_k, b_ln_w, b_ln_b, b_w_out, c_w_in, c_conv_w, c_A_log, c_dt_bias, c_norm_g, c_w_out):
    batch, seq, d = x.shape
    depth = p.shape[0]
    m = batch * seq
    xf = x.reshape(m, d)
    tables = rope_tables(positions)
    for i in range(depth):
        kind = i % N_MIXERS
        j = i // N_MIXERS
        if kind == 0:
            lam_init = 0.8 - 0.6 * math.exp(-0.3 * i)
            hn = rmsnorm(xf, norm_g[i], BF16)
            xf = diff_attention_layer(xf, hn, tables, a_w_in[j], a_lam[j], a_subln_g[j], a_w_out[j],
                                      batch, seq, lam_init)
        elif kind == 1:
            xf = rwkv7_layer(xf, norm_g[i], b_mu[j], b_w_rkvg[j], b_w0[j], b_w_w1[j], b_w_w2[j], b_a0[j],
                             b_w_a1[j], b_w_a2[j], b_k_k[j], b_k_a[j], b_r_k[j], b_ln_w[j], b_ln_b[j],
                             b_w_out[j], batch, seq)
        else:
            hn = rmsnorm(xf, norm_g[i], BF16)
            xf = gated_deltanet_layer(xf, hn, c_w_in[j], c_conv_w[j], c_A_log[j], c_dt_bias[j], c_norm_g[j],
                                      c_w_out[j], batch, seq)
        hn2 = rmsnorm(xf, pe_norm_g[i], BF16)
        xf = per_layer_embedding(xf, hn2, pe_w_gate[i].astype(BF16), p[i].reshape(m, -1),
                                 pe_w_proj[i].astype(BF16))
    return rmsnorm(xf, final_norm_g, F32).reshape(batch, seq, d)
```

```python
import functools
import math

import jax
import jax.numpy as jnp
from jax import lax
from jax.experimental import pallas as pl
from jax.experimental.pallas import tpu as pltpu

F32 = jnp.float32
BF16 = jnp.bfloat16

N_MIXERS = 3
NORM_EPS = 1e-6
LANES = 128
VMEM_LIMIT = 48 * 1024 * 1024

A_HEAD_DIM = 128
A_V_DIM = 2 * A_HEAD_DIM
ROT_DIM = A_HEAD_DIM // 4
ROPE_THETA = 500000.0
SUBLN_EPS = 1e-5
ATTN_BLOCK = 256
ONES_ROWS = 16

B_HEAD_DIM = 64
B_GN_EPS = 64e-5
LORA_PAD = 128

C_HEAD_DIM = 128
C_CONV_WIDTH = 4
CHUNK = 64
SCAN_ROWS = 512
PREP_ROWS = 1024

NN = (((1,), (0,)), ((), ()))
NT = (((1,), (1,)), ((), ()))


def _dot(a, b, dims=NN):
    return lax.dot_general(a, b, dims, preferred_element_type=F32)


def _split2(x):
    hi = x.astype(BF16)
    lo = (x - hi.astype(F32)).astype(BF16)
    return hi, lo


def _split3(x):
    hi = x.astype(BF16)
    r = x - hi.astype(F32)
    mid = r.astype(BF16)
    lo = (r - mid.astype(F32)).astype(BF16)
    return hi, mid, lo


def _dot3(a, b, dims=NN):
    ah, al = _split2(a)
    bh, bl = _split2(b)
    return _dot(ah, bh, dims) + (_dot(ah, bl, dims) + _dot(al, bh, dims))


def _dot_xl(a, b_exact, dims=NN):
    h, m, l = _split3(a)
    return _dot(h, b_exact, dims) + (_dot(m, b_exact, dims) + _dot(l, b_exact, dims))


def _dot_xr(a_exact, b, dims=NN):
    h, m, l = _split3(b)
    return _dot(a_exact, h, dims) + (_dot(a_exact, m, dims) + _dot(a_exact, l, dims))


def _iota(shape, dim):
    return lax.broadcasted_iota(jnp.int32, shape, dim)


def _silu(x):
    return x * jax.nn.sigmoid(x)


def _softplus(x):
    return jnp.maximum(x, 0.0) + jnp.log(1.0 + jnp.exp(-jnp.abs(x)))


def _params(*sem):
    return pltpu.CompilerParams(dimension_semantics=sem, vmem_limit_bytes=VMEM_LIMIT)


def _inv_unit_lower(a, nilpotency):
    n = a.shape[0]
    eye = (_iota((n, n), 0) == _iota((n, n), 1)).astype(F32)
    p = -a
    t = eye + p
    for _ in range(int(math.log2(nilpotency)) - 1):
        p = _dot3(p, p)
        t = t + _dot3(t, p)
    return t


def _rmsnorm_kernel(x_ref, g_ref, o_ref, *, eps):
    x = x_ref[...]
    y = x * lax.rsqrt(jnp.mean(x * x, axis=-1, keepdims=True) + eps)
    o_ref[...] = (y * g_ref[...]).astype(o_ref.dtype)


def rmsnorm(x, g, out_dtype, *, eps=NORM_EPS, tm=512):
    m, d = x.shape
    tm = min(tm, m)
    return pl.pallas_call(
        functools.partial(_rmsnorm_kernel, eps=eps),
        out_shape=jax.ShapeDtypeStruct((m, d), out_dtype),
        grid=(m // tm,),
        in_specs=[pl.BlockSpec((tm, d), lambda i: (i, 0)), pl.BlockSpec((1, d), lambda i: (0, 0))],
        out_specs=pl.BlockSpec((tm, d), lambda i: (i, 0)),
        compiler_params=_params("parallel"),
        name="rmsnorm",
    )(x, g.reshape(1, d))


def _mm_kernel(a_ref, w_ref, *rest, epilogue):
    o_ref = rest[-1]
    acc = _dot(a_ref[...], w_ref[...])
    if epilogue is not None:
        acc = epilogue(acc, *rest[:-1])
    o_ref[...] = acc.astype(o_ref.dtype)


def matmul(a, w, out_dtype, *, tm=1024, tn=1024, extra=(), extra_specs=(), epilogue=None, name="matmul"):
    m, k = a.shape
    n = w.shape[1]
    tm, tn = min(tm, m), min(tn, n)
    return pl.pallas_call(
        functools.partial(_mm_kernel, epilogue=epilogue),
        out_shape=jax.ShapeDtypeStruct((m, n), out_dtype),
        grid=(n // tn, m // tm),
        in_specs=[pl.BlockSpec((tm, k), lambda j, i: (i, 0)),
                  pl.BlockSpec((k, tn), lambda j, i: (0, j))] + list(extra_specs),
        out_specs=pl.BlockSpec((tm, tn), lambda j, i: (i, j)),
        compiler_params=_params("parallel", "parallel"),
        name=name,
    )(a, w, *extra)


def matmul_residual(a, w, res, *, tm=1024, tn=1024, name="matmul_residual"):
    tm, tn = min(tm, a.shape[0]), min(tn, w.shape[1])
    return matmul(a, w, F32, tm=tm, tn=tn, extra=(res,),
                  extra_specs=(pl.BlockSpec((tm, tn), lambda j, i: (i, j)),),
                  epilogue=lambda acc, r_ref: r_ref[...] + acc, name=name)


def _ple_kernel(h_ref, wg_ref, p_ref, wp_ref, x_ref, o_ref):
    gate = jax.nn.sigmoid(_dot(h_ref[...], wg_ref[...]))
    proj = _dot(p_ref[...].astype(BF16), wp_ref[...])
    o_ref[...] = x_ref[...] + gate * proj


def per_layer_embedding(x, hn, w_gate, p, w_proj, *, tm=1024, tn=1024):
    m, d = x.shape
    pd = p.shape[1]
    tm, tn = min(tm, m), min(tn, d)
    return pl.pallas_call(
        _ple_kernel,
        out_shape=jax.ShapeDtypeStruct((m, d), F32),
        grid=(d // tn, m // tm),
        in_specs=[pl.BlockSpec((tm, d), lambda j, i: (i, 0)),
                  pl.BlockSpec((d, tn), lambda j, i: (0, j)),
                  pl.BlockSpec((tm, pd), lambda j, i: (i, 0)),
                  pl.BlockSpec((pd, tn), lambda j, i: (0, j)),
                  pl.BlockSpec((tm, tn), lambda j, i: (i, j))],
        out_specs=pl.BlockSpec((tm, tn), lambda j, i: (i, j)),
        compiler_params=_params("parallel", "parallel"),
        name="per_layer_embedding",
    )(hn, w_gate, p, w_proj, x)


def _rope_table_kernel(pos_ref, freq_ref, cos_ref, sin_lo_ref, sin_hi_ref):
    half = ROT_DIM // 2
    ang = pos_ref[...].astype(F32) * freq_ref[...]
    lane = _iota(ang.shape, 1)
    c, s = jnp.cos(ang), jnp.sin(ang)
    cos_ref[...] = jnp.where(lane < ROT_DIM, c, 1.0)
    sin_lo_ref[...] = jnp.where(lane < half, -s, 0.0)
    sin_hi_ref[...] = jnp.where((lane >= half) & (lane < ROT_DIM), s, 0.0)


def rope_tables(positions, *, tm=1024):
    m = positions.size
    tm = min(tm, m)
    inv_freq = ROPE_THETA ** (-jnp.arange(0, ROT_DIM, 2, dtype=F32) / ROT_DIM)
    freq_row = jnp.concatenate([inv_freq, inv_freq, jnp.zeros((LANES - ROT_DIM,), F32)]).reshape(1, LANES)
    shape = jax.ShapeDtypeStruct((m, LANES), F32)
    spec = pl.BlockSpec((tm, LANES), lambda i: (i, 0))
    return pl.pallas_call(
        _rope_table_kernel,
        out_shape=(shape, shape, shape),
        grid=(m // tm,),
        in_specs=[pl.BlockSpec((tm, 1), lambda i: (i, 0)), pl.BlockSpec((1, LANES), lambda i: (0, 0))],
        out_specs=(spec, spec, spec),
        compiler_params=_params("parallel"),
        name="rope_tables",
    )(positions.reshape(m, 1), freq_row)


def _attn_in_kernel(a_ref, w_ref, cos_ref, sin_lo_ref, sin_hi_ref, o_ref, *, n_q_blocks):
    j = pl.program_id(0)
    acc = _dot(a_ref[...], w_ref[...])
    half = ROT_DIM // 2

    @pl.when(j < 2 * n_q_blocks)
    def _():
        scale = jnp.where(j < n_q_blocks, A_HEAD_DIM ** -0.5, 1.0).astype(F32)
        cos, sin_lo, sin_hi = cos_ref[...], sin_lo_ref[...], sin_hi_ref[...]
        for g in range(acc.shape[1] // LANES):
            x = acc[:, g * LANES:(g + 1) * LANES]
            y = x * cos + pltpu.roll(x, LANES - half, 1) * sin_lo + pltpu.roll(x, half, 1) * sin_hi
            o_ref[:, g * LANES:(g + 1) * LANES] = (y * scale).astype(o_ref.dtype)

    @pl.when(j >= 2 * n_q_blocks)
    def _():
        o_ref[...] = acc.astype(o_ref.dtype)


def attn_in_proj(hn, w_in, tables, qk_width, *, tm=1024, tn=1024):
    m, k = hn.shape
    n = w_in.shape[1]
    tm, tn = min(tm, m), min(tn, qk_width)
    tspec = pl.BlockSpec((tm, LANES), lambda j, i: (i, 0))
    return pl.pallas_call(
        functools.partial(_attn_in_kernel, n_q_blocks=qk_width // tn),
        out_shape=jax.ShapeDtypeStruct((m, n), BF16),
        grid=(n // tn, m // tm),
        in_specs=[pl.BlockSpec((tm, k), lambda j, i: (i, 0)),
                  pl.BlockSpec((k, tn), lambda j, i: (0, j)), tspec, tspec, tspec],
        out_specs=pl.BlockSpec((tm, tn), lambda j, i: (i, j)),
        compiler_params=_params("parallel", "parallel"),
        name="attn_in_proj",
    )(hn, w_in, *tables)


def _attn_vt_kernel(w_ref, a_ref, o_ref):
    o_ref[...] = _dot(w_ref[...], a_ref[...], NT).astype(o_ref.dtype)


def attn_v_proj_t(hn, w_t, batch, seq, blk, *, tn=1024):
    m, k = hn.shape
    n = w_t.shape[0]
    tn = min(tn, n)
    nk = seq // blk
    return pl.pallas_call(
        _attn_vt_kernel,
        out_shape=jax.ShapeDtypeStruct((batch, nk, n, blk), BF16),
        grid=(n // tn, m // blk),
        in_specs=[pl.BlockSpec((tn, k), lambda j, i: (j, 0)),
                  pl.BlockSpec((blk, k), lambda j, i: (i, 0))],
        out_specs=pl.BlockSpec((None, None, tn, blk), lambda j, i: (i // nk, i % nk, j, 0)),
        compiler_params=_params("parallel", "parallel"),
        name="attn_v_proj_t",
    )(w_t, hn)


def _diff_attn_kernel(lam_ref, q_ref, k_ref, vt_ref, z_ref, g_ref, o_ref, m_ref, acc_ref, s_ref,
                      *, bq, bk, lam_init):
    i = pl.program_id(2)
    lam = lam_ref[...]
    lam_full = (jnp.exp(jnp.sum(lam[0:1] * lam[1:2], axis=-1, keepdims=True))
                - jnp.exp(jnp.sum(lam[2:3] * lam[3:4], axis=-1, keepdims=True)) + lam_init)
    m_ref[...] = jnp.full(m_ref.shape, -jnp.inf, F32)
    acc_ref[...] = jnp.zeros(acc_ref.shape, F32)
    q = q_ref[...]
    d = A_HEAD_DIM
    w = A_V_DIM
    ones = jnp.ones((ONES_ROWS, bk), BF16)

    def scores(j, slot):
        start = pl.multiple_of(j * bk, bk)
        kb = k_ref[pl.ds(start, bk), :]
        for c in range(2):
            s_ref[slot, c] = _dot(kb[:, c * d:(c + 1) * d], q[:, c * d:(c + 1) * d], NT)

    def absorb(j, slot, masked):
        vt = jnp.concatenate([vt_ref[j], ones], axis=0)
        for c in range(2):
            s = s_ref[slot, c]
            if masked:
                kv_pos = j * bk + _iota(s.shape, 0)
                q_pos = i * bq + _iota(s.shape, 1)
                s = jnp.where(kv_pos <= q_pos, s, -jnp.inf)
            m_prev = m_ref[c]
            m_new = jnp.maximum(m_prev, jnp.max(s, axis=0, keepdims=True))
            alpha = jnp.exp(m_prev - m_new)
            p = jnp.exp(s - m_new)
            acc_ref[c] = alpha * acc_ref[c] + _dot(vt, p.astype(BF16))
            m_ref[c] = m_new

    scores(0, 0)

    def body(jj, carry):
        scores(2 * jj + 1, 1)
        absorb(2 * jj, 0, False)
        scores(2 * jj + 2, 0)
        absorb(2 * jj + 1, 1, False)
        return carry

    lax.fori_loop(0, i, body, 0)
    scores(2 * i + 1, 1)
    absorb(2 * i, 0, True)
    absorb(2 * i + 1, 1, True)

    o = (acc_ref[0, :w, :] / acc_ref[0, w:w + 1, :]
         - lam_full * (acc_ref[1, :w, :] / acc_ref[1, w:w + 1, :]))
    o = o * lax.rsqrt(jnp.mean(o * o, axis=0, keepdims=True) + SUBLN_EPS) * g_ref[...]
    o = (o * (1.0 - lam_init)).T
    o_ref[...] = (o * _silu(z_ref[...].astype(F32))).astype(o_ref.dtype)


def diff_attention_core(qkz, vt, lam, subln_g, batch, seq, heads, lam_init):
    m = qkz.shape[0]
    bk = vt.shape[-1]
    bq = 2 * bk
    nq = seq // bq
    nk = seq // bk
    w = A_V_DIM
    return pl.pallas_call(
        functools.partial(_diff_attn_kernel, bq=bq, bk=bk, lam_init=lam_init),
        out_shape=jax.ShapeDtypeStruct((m, heads * w), BF16),
        grid=(batch, heads, nq),
        in_specs=[pl.BlockSpec((4, A_HEAD_DIM), lambda b, h, i: (0, 0)),
                  pl.BlockSpec((bq, w), lambda b, h, i: (b * nq + i, h)),
                  pl.BlockSpec((seq, w), lambda b, h, i: (b, heads + h)),
                  pl.BlockSpec((None, nk, w, bk), lambda b, h, i: (b, 0, h, 0)),
                  pl.BlockSpec((bq, w), lambda b, h, i: (b * nq + i, 2 * heads + h)),
                  pl.BlockSpec((w, 1), lambda b, h, i: (0, 0))],
        out_specs=pl.BlockSpec((bq, w), lambda b, h, i: (b * nq + i, h)),
        scratch_shapes=[pltpu.VMEM((2, 1, bq), F32), pltpu.VMEM((2, w + ONES_ROWS, bq), F32),
                        pltpu.VMEM((2, 2, bk, bq), F32)],
        compiler_params=_params("parallel", "parallel", "parallel"),
        name="diff_attention",
    )(lam, qkz, qkz, vt, qkz, subln_g.reshape(w, 1))


def diff_attention_layer(x, hn, tables, w_in, lam, subln_g, w_out, batch, seq, lam_init):
    d = x.shape[1]
    heads = d // A_V_DIM
    qk_w = heads * 2 * A_HEAD_DIM
    v_w = heads * A_V_DIM
    blk = min(ATTN_BLOCK, seq // 2)
    w_qkz = jnp.concatenate([w_in[:, :2 * qk_w], w_in[:, 2 * qk_w + v_w:]], axis=1).astype(BF16)
    w_vt = w_in[:, 2 * qk_w:2 * qk_w + v_w].T.astype(BF16)
    qkz = attn_in_proj(hn, w_qkz, tables, qk_w)
    vt = attn_v_proj_t(hn, w_vt, batch, seq, blk)
    o = diff_attention_core(qkz, vt, lam, subln_g, batch, seq, heads, lam_init)
    return matmul_residual(o, w_out.astype(BF16), x, name="attn_out_proj")


def _gdn_in_kernel(a_ref, w_ref, cw_ref, o_ref, tail_ref, *, n_conv_blocks, rows_per_seq):
    j = pl.program_id(0)
    i = pl.program_id(1)
    acc = _dot(a_ref[...], w_ref[...])
    tm = acc.shape[0]

    @pl.when(j < n_conv_blocks)
    def _():
        @pl.when((i * tm) % rows_per_seq == 0)
        def _():
            tail_ref[...] = jnp.zeros(tail_ref.shape, F32)

        tail = tail_ref[...]
        sub = _iota(tail.shape, 0)

        def shifted(s):
            xs = pltpu.roll(acc, s, 0)
            head = jnp.where(sub < s, pltpu.roll(tail, s, 0), xs[:8])
            return jnp.concatenate([head, xs[8:]], axis=0)

        cw = cw_ref[...]
        last = C_CONV_WIDTH - 1
        y = shifted(last) * cw[0:1]
        for t in range(1, last):
            y = y + shifted(last - t) * cw[t:t + 1]
        y = y + acc * cw[last:last + 1]
        tail_ref[...] = acc[tm - 8:]
        o_ref[...] = _silu(y).astype(o_ref.dtype)

    @pl.when(j >= n_conv_blocks)
    def _():
        o_ref[...] = acc.astype(o_ref.dtype)


def gdn_in_proj(hn, w, conv_w, seq, *, tm=1024, tn=1024):
    m, k = hn.shape
    n = w.shape[1]
    conv_ch = conv_w.shape[1]
    tm, tn = min(tm, seq), min(tn, n)
    ncb = conv_ch // tn
    return pl.pallas_call(
        functools.partial(_gdn_in_kernel, n_conv_blocks=ncb, rows_per_seq=seq),
        out_shape=jax.ShapeDtypeStruct((m, n), BF16),
        grid=(n // tn, m // tm),
        in_specs=[pl.BlockSpec((tm, k), lambda j, i: (i, 0)),
                  pl.BlockSpec((k, tn), lambda j, i: (0, j)),
                  pl.BlockSpec((C_CONV_WIDTH, tn), lambda j, i: (0, jnp.minimum(j, ncb - 1)))],
        out_specs=pl.BlockSpec((tm, tn), lambda j, i: (i, j)),
        scratch_shapes=[pltpu.VMEM((8, tn), F32)],
        compiler_params=_params("arbitrary", "arbitrary"),
        name="gdn_in_proj",
    )(hn, w, conv_w)


def _pair_masks():
    n = 2 * CHUNK
    r, c = _iota((n, n), 0), _iota((n, n), 1)
    same = (r // CHUNK) == (c // CHUNK)
    return same & (r >= c), same & (r > c), r == c


def _dot_x2l(a, b_exact, dims=NN):
    h, l = _split2(a)
    return _dot(h, b_exact, dims) + _dot(l, b_exact, dims)


def _dot_x2r(a_exact, b, dims=NN):
    h, l = _split2(b)
    return _dot(a_exact, h, dims) + _dot(a_exact, l, dims)


def _inverse_stages(p_ref, t_ref, count, nilpotency):
    for _ in range(int(math.log2(nilpotency)) - 1):
        for i in range(count):
            p_ref[i] = _dot(p_ref[i], p_ref[i]).astype(BF16)
        for i in range(count):
            t = t_ref[i]
            t_ref[i] = t + _dot(t.astype(BF16), p_ref[i])


def _gdn_gates_kernel(a_ref, w_ref, alog_ref, dtb_ref, beta_ref, gc_ref):
    acc = _dot(a_ref[...], w_ref[...])
    beta_ref[...] = jax.nn.sigmoid(acc[:, :LANES]).astype(beta_ref.dtype)
    g = -jnp.exp(alog_ref[...]) * _softplus(acc[:, LANES:] + dtb_ref[...])
    tri_b = _pair_masks()[0].astype(BF16)
    n2 = 2 * CHUNK
    for n in range(g.shape[0] // n2):
        rows = slice(n * n2, (n + 1) * n2)
        gc_ref[rows, :] = _dot_xr(tri_b, g[rows])


def gdn_gates(hn, w_ba, alog_row, dtb_row, *, tm=512):
    m, k = hn.shape
    tm = min(tm, m)
    row = pl.BlockSpec((1, LANES), lambda i: (0, 0))
    out = pl.BlockSpec((tm, LANES), lambda i: (i, 0))
    return pl.pallas_call(
        _gdn_gates_kernel,
        out_shape=(jax.ShapeDtypeStruct((m, LANES), BF16), jax.ShapeDtypeStruct((m, LANES), F32)),
        grid=(m // tm,),
        in_specs=[pl.BlockSpec((tm, k), lambda i: (i, 0)), pl.BlockSpec((k, 2 * LANES), lambda i: (0, 0)), row, row],
        out_specs=(out, out),
        compiler_params=_params("parallel"),
        name="gdn_gates",
    )(hn, w_ba, alog_row, dtb_row)


def _gdn_prep_kernel(q_ref, k_ref, v_ref, beta_ref, gc_ref,
                     w_ref, qg_ref, u_ref, att_ref, kdt_ref, dec_ref,
                     g_s, dec_s, kn_s, kb_s, rhs_s, p_s, t_s):
    h = pl.program_id(1)
    c = CHUNK
    n2 = 2 * c
    dk = C_HEAD_DIM
    npairs = q_ref.shape[0] // n2
    tri, strict, diag = _pair_masks()
    eye_b = diag.astype(BF16)
    eye_f = diag.astype(F32)
    sel = (_iota((LANES, dk), 0) == h).astype(BF16)
    own_lane = _iota((n2, LANES), 1) == h
    own_b = own_lane.astype(BF16)
    upper_half = _iota((n2, dk), 0) < c
    pairs = [slice(n * n2, (n + 1) * n2) for n in range(npairs)]

    for n, rows in enumerate(pairs):
        gc_all = gc_ref[rows, :]
        gc = jnp.broadcast_to(jnp.sum(jnp.where(own_lane, gc_all, 0.0), axis=-1, keepdims=True), (n2, dk))
        gc_row = _dot_x2r(own_b, gc_all, NT)
        g_s[n] = gc
        dec_s[n] = jnp.where(tri, jnp.exp(jnp.where(tri, gc - gc_row, 0.0)), 0.0)
    for n, rows in enumerate(pairs):
        gc = g_s[n]
        qf = q_ref[rows, :].astype(F32)
        kf = k_ref[rows, :].astype(F32)
        vf = v_ref[rows, :].astype(F32)
        qn = qf * lax.rsqrt(jnp.sum(qf * qf, axis=-1, keepdims=True) + 1e-6) * (dk ** -0.5)
        kn = kf * lax.rsqrt(jnp.sum(kf * kf, axis=-1, keepdims=True) + 1e-6)
        beta = _dot(beta_ref[rows, :], sel)
        egc = jnp.exp(gc)
        kb = kn * beta
        kn_b = kn.astype(BF16)
        qn_b = qn.astype(BF16)
        kn_s[n] = kn_b
        kb_s[n] = kb.astype(BF16)
        rhs_s[n] = jnp.concatenate([vf * beta, kb * egc], axis=1).astype(BF16)
        g_last = jnp.where(upper_half, gc[c - 1:c, :], gc[n2 - 1:n2, :])
        kdec = kn * jnp.exp(g_last - gc)
        qg_ref[rows, :] = (qn * egc).astype(BF16)
        att_ref[rows, :] = (_dot(qn_b, kn_b, NT) * dec_s[n]).astype(BF16)
        kdt_ref[:, rows] = _dot(eye_b, kdec.astype(BF16), NT).astype(BF16)
        dec_ref[2 * n:2 * n + 1, :] = jnp.exp(gc[c - 1:c, :])
        dec_ref[2 * n + 1:2 * n + 2, :] = jnp.exp(gc[n2 - 1:n2, :])
    for n, rows in enumerate(pairs):
        neg_a = jnp.where(strict, -(_dot(kb_s[n], kn_s[n], NT) * dec_s[n]), 0.0)
        p_s[n] = neg_a.astype(BF16)
        t_s[n] = eye_f + neg_a
    _inverse_stages(p_s, t_s, npairs, c)
    for n, rows in enumerate(pairs):
        sol = _dot(t_s[n].astype(BF16), rhs_s[n])
        u_ref[rows, :] = sol[:, :dk]
        w_ref[rows, :] = sol[:, dk:].astype(BF16)


def _gdn_scan_kernel(w_ref, qg_ref, u_ref, att_ref, kdt_ref, dec_ref, z_ref, g_ref, o_ref,
                     s_ref, ms_s, vp_s, *, group):
    c = CHUNK
    dk = C_HEAD_DIM
    zeros = jnp.zeros((c, dk), BF16)

    @pl.when(pl.program_id(2) == 0)
    def _():
        s_ref[...] = jnp.zeros(s_ref.shape, F32)

    for n in range(w_ref.shape[1] // c):
        rows = slice(n * c, (n + 1) * c)
        pair = slice((n // 2) * 2 * c, (n // 2 + 1) * 2 * c)
        for gi in range(group):
            lhs = jnp.concatenate([w_ref[gi, rows, :], qg_ref[gi, rows, :]], axis=0)
            ms_s[gi] = _dot(lhs, s_ref[gi].astype(BF16))
        for gi in range(group):
            v_new = (u_ref[gi, rows, :] - ms_s[gi, :c, :]).astype(BF16)
            vp_s[gi] = jnp.concatenate([v_new, zeros] if n % 2 == 0 else [zeros, v_new], axis=0)
        for gi in range(group):
            s_ref[gi] = s_ref[gi] * dec_ref[gi, n:n + 1, :] + _dot(kdt_ref[gi, :, pair], vp_s[gi])
        for gi in range(group):
            o = ms_s[gi, c:, :] + _dot(att_ref[gi, rows, :], vp_s[gi])
            o = o * lax.rsqrt(jnp.mean(o * o, axis=-1, keepdims=True) + NORM_EPS) * g_ref[...]
            z = z_ref[rows, gi * dk:(gi + 1) * dk].astype(F32)
            o_ref[rows, gi * dk:(gi + 1) * dk] = (o * _silu(z)).astype(o_ref.dtype)


def gated_deltanet_layer(x, hn, w_in, conv_w, a_log, dt_bias, norm_g, w_out, batch, seq):
    m, d = x.shape
    dk = C_HEAD_DIM
    k_heads = d // dk
    v_heads = 2 * k_heads
    conv_ch = 2 * k_heads * dk + v_heads * dk
    main_w = conv_ch + v_heads * dk
    qkvz = gdn_in_proj(hn, w_in[:, :main_w].astype(BF16), conv_w, seq)
    pad = jnp.zeros((d, LANES - v_heads), F32)
    w_ba = jnp.concatenate([w_in[:, main_w:main_w + v_heads], pad, w_in[:, main_w + v_heads:], pad], axis=1)
    row_pad = jnp.zeros((LANES - v_heads,), F32)
    alog_row = jnp.concatenate([a_log, row_pad]).reshape(1, LANES)
    dtb_row = jnp.concatenate([dt_bias, row_pad]).reshape(1, LANES)
    beta_all, gc_all = gdn_gates(hn, w_ba.astype(BF16), alog_row, dtb_row)

    rows = min(PREP_ROWS, seq)
    nr = seq // rows
    ncr = rows // CHUNK
    npair = rows // (2 * CHUNK)
    hv = v_heads
    bh_t = lambda dt, last: jax.ShapeDtypeStruct((batch, hv, seq, last), dt)
    blk4 = lambda last: pl.BlockSpec((None, None, rows, last), lambda b, h, i: (b, h, i, 0))
    qoff, koff, voff = 0, k_heads, 2 * k_heads
    w_, qg_, u_, att_, kdt_, dec_ = pl.pallas_call(
        _gdn_prep_kernel,
        out_shape=(bh_t(BF16, dk), bh_t(BF16, dk), bh_t(F32, dk), bh_t(BF16, 2 * CHUNK),
                   jax.ShapeDtypeStruct((batch, hv, dk, seq), BF16),
                   jax.ShapeDtypeStruct((batch, hv, seq // CHUNK, dk), F32)),
        grid=(batch, hv, nr),
        in_specs=[pl.BlockSpec((rows, dk), lambda b, h, i: (b * nr + i, qoff + h // 2)),
                  pl.BlockSpec((rows, dk), lambda b, h, i: (b * nr + i, koff + h // 2)),
                  pl.BlockSpec((rows, dk), lambda b, h, i: (b * nr + i, voff + h)),
                  pl.BlockSpec((rows, LANES), lambda b, h, i: (b * nr + i, 0)),
                  pl.BlockSpec((rows, LANES), lambda b, h, i: (b * nr + i, 0))],
        out_specs=(blk4(dk), blk4(dk), blk4(dk), blk4(2 * CHUNK),
                   pl.BlockSpec((None, None, dk, rows), lambda b, h, i: (b, h, 0, i)),
                   pl.BlockSpec((None, None, ncr, dk), lambda b, h, i: (b, h, i, 0))),
        scratch_shapes=[pltpu.VMEM((npair, dk, dk), F32), pltpu.VMEM((npair, dk, dk), F32),
                        pltpu.VMEM((npair, dk, dk), BF16), pltpu.VMEM((npair, dk, dk), BF16),
                        pltpu.VMEM((npair, dk, 2 * dk), BF16), pltpu.VMEM((npair, dk, dk), BF16),
                        pltpu.VMEM((npair, dk, dk), F32)],
        compiler_params=_params("parallel", "parallel", "parallel"),
        name="gdn_prep",
    )(qkvz, qkvz, qkvz, beta_all, gc_all)

    rows = min(SCAN_ROWS, seq)
    nr = seq // rows
    ncr = rows // CHUNK
    group = 8
    zoff = conv_ch // (group * dk)
    gblk = lambda last: pl.BlockSpec((None, group, rows, last), lambda b, h, i: (b, h, i, 0))
    o = pl.pallas_call(
        functools.partial(_gdn_scan_kernel, group=group),
        out_shape=jax.ShapeDtypeStruct((m, hv * dk), BF16),
        grid=(batch, hv // group, nr),
        in_specs=[gblk(dk), gblk(dk), gblk(dk), gblk(2 * CHUNK),
                  pl.BlockSpec((None, group, dk, rows), lambda b, h, i: (b, h, 0, i)),
                  pl.BlockSpec((None, group, ncr, dk), lambda b, h, i: (b, h, i, 0)),
                  pl.BlockSpec((rows, group * dk), lambda b, h, i: (b * nr + i, zoff + h)),
                  pl.BlockSpec((1, dk), lambda b, h, i: (0, 0))],
        out_specs=pl.BlockSpec((rows, group * dk), lambda b, h, i: (b * nr + i, h)),
        scratch_shapes=[pltpu.VMEM((group, dk, dk), F32), pltpu.VMEM((group, 2 * CHUNK, dk), F32),
                        pltpu.VMEM((group, 2 * CHUNK, dk), BF16)],
        compiler_params=_params("parallel", "parallel", "arbitrary"),
        name="gdn_scan",
    )(w_, qg_, u_, att_, kdt_, dec_, qkvz, norm_g.reshape(1, dk))
    return matmul_residual(o, w_out.astype(BF16), x, tm=512, name="gdn_out_proj")


def _rwkv_mix_kernel(x_ref, g_ref, mu_ref, o_ref, tail_ref, *, rows_per_seq):
    i = pl.program_id(0)
    x = x_ref[...]
    tm = x.shape[0]
    hn = x * lax.rsqrt(jnp.mean(x * x, axis=-1, keepdims=True) + NORM_EPS) * g_ref[...]

    @pl.when((i * tm) % rows_per_seq == 0)
    def _():
        tail_ref[...] = jnp.zeros(tail_ref.shape, F32)

    prev = jnp.where(_iota(hn.shape, 0) == 0, tail_ref[7:8, :], pltpu.roll(hn, 1, 0))
    tail_ref[...] = hn[tm - 8:]
    xx = prev - hn
    for c in range(o_ref.shape[0]):
        o_ref[c] = (hn + xx * mu_ref[c:c + 1, :]).astype(o_ref.dtype)


def rwkv_token_mix(x, norm_g, mu, seq, *, tm=256):
    m, d = x.shape
    tm = min(tm, seq)
    nmix = mu.shape[0]
    return pl.pallas_call(
        functools.partial(_rwkv_mix_kernel, rows_per_seq=seq),
        out_shape=jax.ShapeDtypeStruct((nmix, m, d), BF16),
        grid=(m // tm,),
        in_specs=[pl.BlockSpec((tm, d), lambda i: (i, 0)),
                  pl.BlockSpec((1, d), lambda i: (0, 0)),
                  pl.BlockSpec((nmix, d), lambda i: (0, 0))],
        out_specs=pl.BlockSpec((nmix, tm, d), lambda i: (0, i, 0)),
        scratch_shapes=[pltpu.VMEM((8, d), F32)],
        compiler_params=_params("arbitrary"),
        name="rwkv_token_mix",
    )(x, norm_g.reshape(1, d), mu)


def _grouped_mm_kernel(a_ref, w_ref, o_ref):
    o_ref[...] = _dot(a_ref[...], w_ref[...]).astype(o_ref.dtype)


def grouped_matmul(a, w, out_dtype, *, tm=1024, tn=1024):
    g, k, n = w.shape
    m = a.shape[1]
    tm, tn = min(tm, m), min(tn, n)
    return pl.pallas_call(
        _grouped_mm_kernel,
        out_shape=jax.ShapeDtypeStruct((g, m, n), out_dtype),
        grid=(g, n // tn, m // tm),
        in_specs=[pl.BlockSpec((None, tm, k), lambda c, j, i: (c, i, 0)),
                  pl.BlockSpec((None, k, tn), lambda c, j, i: (c, 0, j))],
        out_specs=pl.BlockSpec((None, tm, tn), lambda c, j, i: (c, i, j)),
        compiler_params=_params("parallel", "parallel", "parallel"),
        name="rwkv_rkvg_proj",
    )(a, w)


def _rwkv_lora_kernel(xw_ref, xa_ref, w1_ref, w2_ref, a1_ref, a2_ref, w0_ref, a0_ref, ld_ref, a_ref):
    hw = jnp.tanh(_dot(xw_ref[...], w1_ref[...])).astype(BF16)
    lw = w0_ref[...] + _dot(hw, w2_ref[...])
    log_w = -_softplus(-lw) - 0.5
    ld_ref[...] = -jnp.exp(log_w)
    ha = _dot(xa_ref[...], a1_ref[...]).astype(BF16)
    a_ref[...] = jax.nn.sigmoid(a0_ref[...] + _dot(ha, a2_ref[...]))


def rwkv_lora(xs, w1, w2, a1, a2, w0, a0, *, tm=512):
    _, m, d = xs.shape
    tm = min(tm, m)
    r = w1.shape[1]
    row = pl.BlockSpec((1, d), lambda i: (0, 0))
    shape = jax.ShapeDtypeStruct((m, d), F32)
    return pl.pallas_call(
        _rwkv_lora_kernel,
        out_shape=(shape, shape),
        grid=(m // tm,),
        in_specs=[pl.BlockSpec((None, tm, d), lambda i: (4, i, 0)),
                  pl.BlockSpec((None, tm, d), lambda i: (5, i, 0)),
                  pl.BlockSpec((d, r), lambda i: (0, 0)), pl.BlockSpec((r, d), lambda i: (0, 0)),
                  pl.BlockSpec((d, r), lambda i: (0, 0)), pl.BlockSpec((r, d), lambda i: (0, 0)),
                  row, row],
        out_specs=(pl.BlockSpec((tm, d), lambda i: (i, 0)), pl.BlockSpec((tm, d), lambda i: (i, 0))),
        compiler_params=_params("parallel"),
        name="rwkv_lora",
    )(xs, xs, w1, w2, a1, a2, w0.reshape(1, d), a0.reshape(1, d))


def _rwkv_prep_kernel(r_ref, k_ref, v_ref, ld_ref, a_ref, kk_ref, ka_ref, rk_ref,
                      wt_ref, rt_ref, u_ref, aro_ref, kbt_ref, vb_ref, dec_ref, bonus_ref,
                      kkt_s, kbar_s, bbar_s, akk_s, p_s, t_s, drow_s):
    c = CHUNK
    n2 = 2 * c
    n_h = B_HEAD_DIM
    npairs = r_ref.shape[0] // n2
    lane_head = _iota((1, LANES), 1) // n_h
    seg = (_iota((LANES, LANES), 0) // n_h == _iota((LANES, LANES), 1) // n_h).astype(BF16)
    tri, strict, diag = _pair_masks()
    tri_b = tri.astype(BF16)
    eye_b = diag.astype(BF16)
    eye_f = diag.astype(F32)
    upper_half = _iota((n2, LANES), 0) < c
    pairs = [slice(n * n2, (n + 1) * n2) for n in range(npairs)]

    r = r_ref[...]
    k = k_ref[...]
    v = v_ref[...]
    a = a_ref[...]
    kk_raw = k * kk_ref[...]
    kk = kk_raw * lax.rsqrt(_dot_xl(kk_raw * kk_raw, seg) + 1e-6)
    k2 = k * (1.0 + (a - 1.0) * ka_ref[...])
    bb = kk * a
    bonus_ref[...] = _dot_xl(r * k2 * rk_ref[...], seg) * v
    vb_ref[...] = v.astype(BF16)
    for n, rows in enumerate(pairs):
        ld = ld_ref[rows, :]
        cs = _dot_xr(tri_b, ld)
        cs_last = jnp.where(upper_half, cs[c - 1:c, :], cs[n2 - 1:n2, :])
        w_inv = jnp.exp(-cs)
        w_end = jnp.exp(cs_last - cs)
        rt_ref[rows, :] = (r[rows] * jnp.exp(cs)).astype(BF16)
        kkt_s[n] = (kk[rows] * jnp.exp(cs - ld)).astype(BF16)
        k2c, bbc = k2[rows], bb[rows]
        kbar_s[n] = (k2c * w_inv).astype(BF16)
        bbar_s[n] = (bbc * w_inv).astype(BF16)
        kw, bw = (k2c * w_end).astype(BF16), (-(bbc * w_end)).astype(BF16)
        for half in range(2):
            hs = slice(half * c, (half + 1) * c)
            kb_end = jnp.concatenate([kw[hs], bw[hs]], axis=0)
            cols = slice((2 * n + half) * n2, (2 * n + half + 1) * n2)
            kbt_ref[:, cols] = _dot(eye_b, kb_end, NT).astype(BF16)
        drow_s[2 * n:2 * n + 1, :] = jnp.exp(cs[c - 1:c, :])
        drow_s[2 * n + 1:2 * n + 2, :] = jnp.exp(cs[n2 - 1:n2, :])
    drow_s[2 * npairs:, :] = jnp.zeros((LANES - 2 * npairs, LANES), F32)
    dec_ref[...] = _dot_xr(eye_b, drow_s[...], NT)
    zero_b = jnp.zeros((n2, LANES), BF16)
    for n, rows in enumerate(pairs):
        for hh in range(2):
            own = lane_head == hh
            i = 2 * n + hh
            kkt_m = jnp.where(own, kkt_s[n], zero_b)
            rt_m = jnp.where(own, rt_ref[rows, :], zero_b)
            neg_a = jnp.where(strict, -_dot(kkt_m, bbar_s[n], NT), 0.0)
            p_s[i] = neg_a.astype(BF16)
            t_s[i] = eye_f + neg_a
            akk_s[i] = jnp.where(strict, _dot(kkt_m, kbar_s[n], NT), 0.0).astype(BF16)
            a_rk = jnp.where(tri, _dot(rt_m, kbar_s[n], NT), 0.0)
            a_rb = jnp.where(tri, _dot(rt_m, bbar_s[n], NT), 0.0)
            aro_ref[hh, rows, :] = jnp.concatenate([a_rk, -a_rb], axis=1).astype(BF16)
    _inverse_stages(p_s, t_s, 2 * npairs, c)
    for n, rows in enumerate(pairs):
        for hh in range(2):
            i = 2 * n + hh
            p_s[i] = _dot(akk_s[i], vb_ref[rows, :]).astype(BF16)
    for n, rows in enumerate(pairs):
        t0, t1 = t_s[2 * n].astype(BF16), t_s[2 * n + 1].astype(BF16)
        own0 = lane_head == 0
        wt_ref[rows, :] = jnp.where(own0, _dot(t0, kkt_s[n]), _dot(t1, kkt_s[n])).astype(BF16)
        u_ref[rows, :] = jnp.where(own0, _dot(t0, p_s[2 * n]), _dot(t1, p_s[2 * n + 1]))


def _rwkv_scan_kernel(wt_ref, rt_ref, u_ref, aro_ref, kbt_ref, vb_ref, dec_ref, bonus_ref, gate_ref,
                      lnw_ref, lnb_ref, o_ref, s_ref, ms_s, xc_s, xp_s, *, group):
    c = CHUNK
    n2 = 2 * c
    n_h = B_HEAD_DIM
    first_head = _iota((1, LANES), 1) < n_h
    same_head = _iota((LANES, LANES), 0) // n_h == _iota((LANES, LANES), 1) // n_h
    zeros = jnp.zeros((c, LANES), BF16)

    def head_mean(x):
        tot = jnp.sum(x, axis=-1, keepdims=True)
        lo = jnp.sum(jnp.where(first_head, x, 0.0), axis=-1, keepdims=True)
        return jnp.where(first_head, lo, tot - lo) * (1.0 / n_h)

    @pl.when(pl.program_id(2) == 0)
    def _():
        s_ref[...] = jnp.zeros(s_ref.shape, F32)

    for n in range(u_ref.shape[1] // c):
        rows = slice(n * c, (n + 1) * c)
        for gi in range(group):
            lhs = jnp.concatenate([wt_ref[gi, rows, :], rt_ref[gi, rows, :]], axis=0)
            ms_s[gi] = _dot(lhs, s_ref[gi].astype(BF16))
        for gi in range(group):
            sa = (u_ref[gi, rows, :] + ms_s[gi, :c, :]).astype(BF16)
            vb = vb_ref[gi, rows, :]
            xc_s[gi] = jnp.concatenate([vb, sa], axis=0)
            xp_s[gi] = jnp.concatenate([vb, zeros, sa, zeros] if n % 2 == 0 else [zeros, vb, zeros, sa], axis=0)
        for gi in range(group):
            upd = _dot(kbt_ref[gi, :, n * n2:(n + 1) * n2], xc_s[gi])
            dec_col = jnp.broadcast_to(dec_ref[gi, :, n:n + 1], (LANES, LANES))
            s_ref[gi] = s_ref[gi] * dec_col + jnp.where(same_head, upd, 0.0)
        for gi in range(group):
            both = _dot(jnp.concatenate([aro_ref[gi, 0, rows, :], aro_ref[gi, 1, rows, :]], axis=0), xp_s[gi])
            o = ms_s[gi, c:, :] + jnp.where(first_head, both[:c], both[c:])
            mean = head_mean(o)
            dlt = o - mean
            var = head_mean(dlt * dlt)
            cols = slice(gi * LANES, (gi + 1) * LANES)
            y = dlt * lax.rsqrt(var + B_GN_EPS) * lnw_ref[:, cols] + lnb_ref[:, cols]
            y = y + bonus_ref[rows, cols]
            o_ref[rows, cols] = (y * _silu(gate_ref[rows, cols])).astype(o_ref.dtype)


def rwkv7_layer(x, norm_g, mu, w_rkvg, w0, w_w1, w_w2, a0, w_a1, w_a2, k_k, k_a, r_k, ln_w, ln_b, w_out,
                batch, seq):
    m, d = x.shape
    heads = d // B_HEAD_DIM
    pairs = heads // 2
    order = jnp.array([0, 2, 3, 5, 1, 4])
    xs = rwkv_token_mix(x, norm_g, mu[order], seq)
    rkvg = grouped_matmul(xs, w_rkvg.astype(BF16), F32)
    lora = w_w1.shape[1]
    padc = lambda w: jnp.pad(w, ((0, 0), (0, LORA_PAD - lora))).astype(BF16)
    padr = lambda w: jnp.pad(w, ((0, LORA_PAD - lora), (0, 0))).astype(BF16)
    ld, a = rwkv_lora(xs, padc(w_w1), padr(w_w2), padc(w_a1), padr(w_a2), w0, a0)

    rows = min(SCAN_ROWS, seq)
    nr = seq // rows
    ncr = rows // CHUNK
    npair = rows // (2 * CHUNK)
    col = lambda g: pl.BlockSpec((None, rows, LANES), lambda b, p, i, g=g: (g, b * nr + i, p))
    flat = pl.BlockSpec((rows, LANES), lambda b, p, i: (b * nr + i, p))
    prow = pl.BlockSpec((1, LANES), lambda b, p, i: (0, p))
    bp = lambda rws, last, dt: jax.ShapeDtypeStruct((batch, pairs, rws, last), dt)
    pblk = lambda rws, last: pl.BlockSpec((None, None, rws, last), lambda b, p, i: (b, p, i, 0))
    wt_, rt_, u_, aro_, kbt_, vb_, dec_, bonus = pl.pallas_call(
        _rwkv_prep_kernel,
        out_shape=(bp(seq, LANES, BF16), bp(seq, LANES, BF16), bp(seq, LANES, F32),
                   jax.ShapeDtypeStruct((batch, pairs, 2, seq, 4 * CHUNK), BF16),
                   bp(LANES, 2 * seq, BF16), bp(seq, LANES, BF16), bp(nr * LANES, LANES, F32),
                   jax.ShapeDtypeStruct((m, d), F32)),
        grid=(batch, pairs, nr),
        in_specs=[col(0), col(1), col(2), flat, flat, prow, prow, prow],
        out_specs=(pblk(rows, LANES), pblk(rows, LANES), pblk(rows, LANES),
                   pl.BlockSpec((None, None, 2, rows, 4 * CHUNK), lambda b, p, i: (b, p, 0, i, 0)),
                   pl.BlockSpec((None, None, LANES, 2 * rows), lambda b, p, i: (b, p, 0, i)),
                   pblk(rows, LANES), pblk(LANES, LANES), flat),
        scratch_shapes=[pltpu.VMEM((npair, LANES, LANES), BF16), pltpu.VMEM((npair, LANES, LANES), BF16),
                        pltpu.VMEM((npair, LANES, LANES), BF16), pltpu.VMEM((2 * npair, LANES, LANES), BF16),
                        pltpu.VMEM((2 * npair, LANES, LANES), BF16), pltpu.VMEM((2 * npair, LANES, LANES), F32),
                        pltpu.VMEM((LANES, LANES), F32)],
        compiler_params=_params("parallel", "parallel", "parallel"),
        name="rwkv_prep",
    )(rkvg, rkvg, rkvg, ld, a, k_k.reshape(1, d), k_a.reshape(1, d), r_k.reshape(1, d))

    group = 8
    gw = group * LANES
    gblk = lambda rws, last: pl.BlockSpec((None, group, rws, last), lambda b, p, i: (b, p, i, 0))
    gflat = pl.BlockSpec((rows, gw), lambda b, p, i: (b * nr + i, p))
    grow = pl.BlockSpec((1, gw), lambda b, p, i: (0, p))
    o = pl.pallas_call(
        functools.partial(_rwkv_scan_kernel, group=group),
        out_shape=jax.ShapeDtypeStruct((m, d), BF16),
        grid=(batch, pairs // group, nr),
        in_specs=[gblk(rows, LANES), gblk(rows, LANES), gblk(rows, LANES),
                  pl.BlockSpec((None, group, 2, rows, 4 * CHUNK), lambda b, p, i: (b, p, 0, i, 0)),
                  pl.BlockSpec((None, group, LANES, 2 * rows), lambda b, p, i: (b, p, 0, i)),
                  gblk(rows, LANES), gblk(LANES, LANES), gflat,
                  pl.BlockSpec((None, rows, gw), lambda b, p, i: (3, b * nr + i, p)),
                  grow, grow],
        out_specs=gflat,
        scratch_shapes=[pltpu.VMEM((group, LANES, LANES), F32), pltpu.VMEM((group, 2 * CHUNK, LANES), F32),
                        pltpu.VMEM((group, 2 * CHUNK, LANES), BF16), pltpu.VMEM((group, 4 * CHUNK, LANES), BF16)],
        compiler_params=_params("parallel", "parallel", "arbitrary"),
        name="rwkv_scan",
    )(wt_, rt_, u_, aro_, kbt_, vb_, dec_, bonus, rkvg, ln_w.reshape(1, d), ln_b.reshape(1, d))
    return matmul_residual(o, w_out.astype(BF16), x, name="rwkv_out_proj")


def kernel(x, p, positions, norm_g, pe_norm_g, pe_w_gate, pe_w_proj, final_norm_g, a_w_in, a_lam, a_subln_g, a_w_out, b_mu, b_w_rkvg, b_w0, b_w_w1, b_w_w2, b_a0, b_w_a1, b_w_a2, b_k_k, b_k_a, b_r_k, b_ln_w, b_ln_b, b_w_out, c_w_in, c_conv_w, c_A_log, c_dt_bias, c_norm_g, c_w_out):
    batch, seq, d = x.shape
    depth = p.shape[0]
    m = batch * seq
    xf = x.reshape(m, d)
    tables = rope_tables(positions)
    for i in range(depth):
        kind = i % N_MIXERS
        j = i // N_MIXERS
        if kind == 0:
            lam_init = 0.8 - 0.6 * math.exp(-0.3 * i)
            hn = rmsnorm(xf, norm_g[i], BF16)
            xf = diff_attention_layer(xf, hn, tables, a_w_in[j], a_lam[j], a_subln_g[j], a_w_out[j],
                                      batch, seq, lam_init)
        elif kind == 1:
            xf = rwkv7_layer(xf, norm_g[i], b_mu[j], b_w_rkvg[j], b_w0[j], b_w_w1[j], b_w_w2[j], b_a0[j],
                             b_w_a1[j], b_w_a2[j], b_k_k[j], b_k_a[j], b_r_k[j], b_ln_w[j], b_ln_b[j],
                             b_w_out[j], batch, seq)
        else:
            hn = rmsnorm(xf, norm_g[i], BF16)
            xf = gated_deltanet_layer(xf, hn, c_w_in[j], c_conv_w[j], c_A_log[j], c_dt_bias[j], c_norm_g[j],
                                      c_w_out[j], batch, seq)
        hn2 = rmsnorm(xf, pe_norm_g[i], BF16)
        xf = per_layer_embedding(xf, hn2, pe_w_gate[i].astype(BF16), p[i].reshape(m, -1),
                                 pe_w_proj[i].astype(BF16))
    return rmsnorm(xf, final_norm_g, F32).reshape(batch, seq, d)
```

```python
import functools
import math

import jax
import jax.numpy as jnp
from jax import lax
from jax.experimental import pallas as pl
from jax.experimental.pallas import tpu as pltpu

F32 = jnp.float32
BF16 = jnp.bfloat16

N_MIXERS = 3
NORM_EPS = 1e-6
LANES = 128
VMEM_LIMIT = 48 * 1024 * 1024

A_HEAD_DIM = 128
A_V_DIM = 2 * A_HEAD_DIM
ROT_DIM = A_HEAD_DIM // 4
ROPE_THETA = 500000.0
SUBLN_EPS = 1e-5
ATTN_BLOCK = 256
ONES_ROWS = 16

B_HEAD_DIM = 64
B_GN_EPS = 64e-5
LORA_PAD = 128

C_HEAD_DIM = 128
C_CONV_WIDTH = 4
CHUNK = 64
SCAN_ROWS = 512
PREP_ROWS = 1024

NN = (((1,), (0,)), ((), ()))
NT = (((1,), (1,)), ((), ()))


def _dot(a, b, dims=NN):
    return lax.dot_general(a, b, dims, preferred_element_type=F32)


def _split2(x):
    hi = x.astype(BF16)
    lo = (x - hi.astype(F32)).astype(BF16)
    return hi, lo


def _split3(x):
    hi = x.astype(BF16)
    r = x - hi.astype(F32)
    mid = r.astype(BF16)
    lo = (r - mid.astype(F32)).astype(BF16)
    return hi, mid, lo


def _dot3(a, b, dims=NN):
    ah, al = _split2(a)
    bh, bl = _split2(b)
    return _dot(ah, bh, dims) + (_dot(ah, bl, dims) + _dot(al, bh, dims))


def _dot_xl(a, b_exact, dims=NN):
    h, m, l = _split3(a)
    return _dot(h, b_exact, dims) + (_dot(m, b_exact, dims) + _dot(l, b_exact, dims))


def _dot_xr(a_exact, b, dims=NN):
    h, m, l = _split3(b)
    return _dot(a_exact, h, dims) + (_dot(a_exact, m, dims) + _dot(a_exact, l, dims))


def _iota(shape, dim):
    return lax.broadcasted_iota(jnp.int32, shape, dim)


def _silu(x):
    return x * jax.nn.sigmoid(x)


def _softplus(x):
    return jnp.maximum(x, 0.0) + jnp.log(1.0 + jnp.exp(-jnp.abs(x)))


def _params(*sem):
    return pltpu.CompilerParams(dimension_semantics=sem, vmem_limit_bytes=VMEM_LIMIT)


def _inv_unit_lower(a, nilpotency):
    n = a.shape[0]
    eye = (_iota((n, n), 0) == _iota((n, n), 1)).astype(F32)
    p = -a
    t = eye + p
    for _ in range(int(math.log2(nilpotency)) - 1):
        p = _dot3(p, p)
        t = t + _dot3(t, p)
    return t


def _rmsnorm_kernel(x_ref, g_ref, o_ref, *, eps):
    x = x_ref[...]
    y = x * lax.rsqrt(jnp.mean(x * x, axis=-1, keepdims=True) + eps)
    o_ref[...] = (y * g_ref[...]).astype(o_ref.dtype)


def rmsnorm(x, g, out_dtype, *, eps=NORM_EPS, tm=512):
    m, d = x.shape
    tm = min(tm, m)
    return pl.pallas_call(
        functools.partial(_rmsnorm_kernel, eps=eps),
        out_shape=jax.ShapeDtypeStruct((m, d), out_dtype),
        grid=(m // tm,),
        in_specs=[pl.BlockSpec((tm, d), lambda i: (i, 0)), pl.BlockSpec((1, d), lambda i: (0, 0))],
        out_specs=pl.BlockSpec((tm, d), lambda i: (i, 0)),
        compiler_params=_params("parallel"),
        name="rmsnorm",
    )(x, g.reshape(1, d))


def _mm_kernel(a_ref, w_ref, *rest, epilogue):
    o_ref = rest[-1]
    acc = _dot(a_ref[...], w_ref[...])
    if epilogue is not None:
        acc = epilogue(acc, *rest[:-1])
    o_ref[...] = acc.astype(o_ref.dtype)


def matmul(a, w, out_dtype, *, tm=1024, tn=1024, extra=(), extra_specs=(), epilogue=None, name="matmul"):
    m, k = a.shape
    n = w.shape[1]
    tm, tn = min(tm, m), min(tn, n)
    return pl.pallas_call(
        functools.partial(_mm_kernel, epilogue=epilogue),
        out_shape=jax.ShapeDtypeStruct((m, n), out_dtype),
        grid=(n // tn, m // tm),
        in_specs=[pl.BlockSpec((tm, k), lambda j, i: (i, 0)),
                  pl.BlockSpec((k, tn), lambda j, i: (0, j))] + list(extra_specs),
        out_specs=pl.BlockSpec((tm, tn), lambda j, i: (i, j)),
        compiler_params=_params("parallel", "parallel"),
        name=name,
    )(a, w, *extra)


def matmul_residual(a, w, res, *, tm=1024, tn=1024, name="matmul_residual"):
    tm, tn = min(tm, a.shape[0]), min(tn, w.shape[1])
    return matmul(a, w, F32, tm=tm, tn=tn, extra=(res,),
                  extra_specs=(pl.BlockSpec((tm, tn), lambda j, i: (i, j)),),
                  epilogue=lambda acc, r_ref: r_ref[...] + acc, name=name)


def _ple_kernel(h_ref, wg_ref, p_ref, wp_ref, x_ref, o_ref):
    gate = jax.nn.sigmoid(_dot(h_ref[...], wg_ref[...]))
    proj = _dot(p_ref[...].astype(BF16), wp_ref[...])
    o_ref[...] = x_ref[...] + gate * proj


def per_layer_embedding(x, hn, w_gate, p, w_proj, *, tm=1024, tn=1024):
    m, d = x.shape
    pd = p.shape[1]
    tm, tn = min(tm, m), min(tn, d)
    return pl.pallas_call(
        _ple_kernel,
        out_shape=jax.ShapeDtypeStruct((m, d), F32),
        grid=(d // tn, m // tm),
        in_specs=[pl.BlockSpec((tm, d), lambda j, i: (i, 0)),
                  pl.BlockSpec((d, tn), lambda j, i: (0, j)),
                  pl.BlockSpec((tm, pd), lambda j, i: (i, 0)),
                  pl.BlockSpec((pd, tn), lambda j, i: (0, j)),
                  pl.BlockSpec((tm, tn), lambda j, i: (i, j))],
        out_specs=pl.BlockSpec((tm, tn), lambda j, i: (i, j)),
        compiler_params=_params("parallel", "parallel"),
        name="per_layer_embedding",
    )(hn, w_gate, p, w_proj, x)


def _rope_table_kernel(pos_ref, freq_ref, cos_ref, sin_lo_ref, sin_hi_ref):
    half = ROT_DIM // 2
    ang = pos_ref[...].astype(F32) * freq_ref[...]
    lane = _iota(ang.shape, 1)
    c, s = jnp.cos(ang), jnp.sin(ang)
    cos_ref[...] = jnp.where(lane < ROT_DIM, c, 1.0)
    sin_lo_ref[...] = jnp.where(lane < half, -s, 0.0)
    sin_hi_ref[...] = jnp.where((lane >= half) & (lane < ROT_DIM), s, 0.0)


def rope_tables(positions, *, tm=1024):
    m = positions.size
    tm = min(tm, m)
    inv_freq = ROPE_THETA ** (-jnp.arange(0, ROT_DIM, 2, dtype=F32) / ROT_DIM)
    freq_row = jnp.concatenate([inv_freq, inv_freq, jnp.zeros((LANES - ROT_DIM,), F32)]).reshape(1, LANES)
    shape = jax.ShapeDtypeStruct((m, LANES), F32)
    spec = pl.BlockSpec((tm, LANES), lambda i: (i, 0))
    return pl.pallas_call(
        _rope_table_kernel,
        out_shape=(shape, shape, shape),
        grid=(m // tm,),
        in_specs=[pl.BlockSpec((tm, 1), lambda i: (i, 0)), pl.BlockSpec((1, LANES), lambda i: (0, 0))],
        out_specs=(spec, spec, spec),
        compiler_params=_params("parallel"),
        name="rope_tables",
    )(positions.reshape(m, 1), freq_row)


def _attn_qk_kernel(a_ref, w_ref, cos_ref, sin_lo_ref, sin_hi_ref, o_ref, *, n_q_blocks):
    j = pl.program_id(0)
    acc = _dot(a_ref[...], w_ref[...])
    half = ROT_DIM // 2
    scale = jnp.where(j < n_q_blocks, A_HEAD_DIM ** -0.5, 1.0).astype(F32)
    cos, sin_lo, sin_hi = cos_ref[...], sin_lo_ref[...], sin_hi_ref[...]
    for g in range(acc.shape[1] // LANES):
        x = acc[:, g * LANES:(g + 1) * LANES]
        y = x * cos + pltpu.roll(x, LANES - half, 1) * sin_lo + pltpu.roll(x, half, 1) * sin_hi
        o_ref[:, g * LANES:(g + 1) * LANES] = (y * scale).astype(o_ref.dtype)


def attn_qk_proj(hn, w_qk, tables, *, tm=1024, tn=1024):
    m, k = hn.shape
    n = w_qk.shape[1]
    tm, tn = min(tm, m), min(tn, n // 2)
    tspec = pl.BlockSpec((tm, LANES), lambda j, i: (i, 0))
    return pl.pallas_call(
        functools.partial(_attn_qk_kernel, n_q_blocks=n // 2 // tn),
        out_shape=jax.ShapeDtypeStruct((m, n), BF16),
        grid=(n // tn, m // tm),
        in_specs=[pl.BlockSpec((tm, k), lambda j, i: (i, 0)),
                  pl.BlockSpec((k, tn), lambda j, i: (0, j)), tspec, tspec, tspec],
        out_specs=pl.BlockSpec((tm, tn), lambda j, i: (i, j)),
        compiler_params=_params("parallel", "parallel"),
        name="attn_qk_proj",
    )(hn, w_qk, *tables)


def _attn_vt_kernel(w_ref, a_ref, o_ref):
    o_ref[...] = _dot(w_ref[...], a_ref[...], NT).astype(o_ref.dtype)


def attn_v_proj_t(hn, w_t, batch, seq, blk, *, tn=1024):
    m, k = hn.shape
    n = w_t.shape[0]
    tn = min(tn, n)
    nk = seq // blk
    return pl.pallas_call(
        _attn_vt_kernel,
        out_shape=jax.ShapeDtypeStruct((batch, nk, n, blk), BF16),
        grid=(n // tn, m // blk),
        in_specs=[pl.BlockSpec((tn, k), lambda j, i: (j, 0)),
                  pl.BlockSpec((blk, k), lambda j, i: (i, 0))],
        out_specs=pl.BlockSpec((None, None, tn, blk), lambda j, i: (i // nk, i % nk, j, 0)),
        compiler_params=_params("parallel", "parallel"),
        name="attn_v_proj_t",
    )(w_t, hn)


def _diff_attn_kernel(lam_ref, q_ref, k_ref, vt_ref, z_ref, g_ref, o_ref, m_ref, acc_ref, s_ref,
                      *, bq, bk, lam_init):
    i = pl.program_id(2)
    lam = lam_ref[...]
    lam_full = (jnp.exp(jnp.sum(lam[0:1] * lam[1:2], axis=-1, keepdims=True))
                - jnp.exp(jnp.sum(lam[2:3] * lam[3:4], axis=-1, keepdims=True)) + lam_init)
    m_ref[...] = jnp.full(m_ref.shape, -jnp.inf, F32)
    acc_ref[...] = jnp.zeros(acc_ref.shape, F32)
    q = q_ref[...]
    d = A_HEAD_DIM
    w = A_V_DIM
    ones = jnp.ones((ONES_ROWS, bk), BF16)

    def scores(j, slot):
        start = pl.multiple_of(j * bk, bk)
        kb = k_ref[pl.ds(start, bk), :]
        for c in range(2):
            s_ref[slot, c] = _dot(kb[:, c * d:(c + 1) * d], q[:, c * d:(c + 1) * d], NT)

    def absorb(j, slot, masked):
        vt = jnp.concatenate([vt_ref[j], ones], axis=0)
        for c in range(2):
            s = s_ref[slot, c]
            if masked:
                kv_pos = j * bk + _iota(s.shape, 0)
                q_pos = i * bq + _iota(s.shape, 1)
                s = jnp.where(kv_pos <= q_pos, s, -jnp.inf)
            m_prev = m_ref[c]
            m_new = jnp.maximum(m_prev, jnp.max(s, axis=0, keepdims=True))
            alpha = jnp.exp(m_prev - m_new)
            p = jnp.exp(s - m_new)
            acc_ref[c] = alpha * acc_ref[c] + _dot(vt, p.astype(BF16))
            m_ref[c] = m_new

    scores(0, 0)

    def body(jj, carry):
        scores(2 * jj + 1, 1)
        absorb(2 * jj, 0, False)
        scores(2 * jj + 2, 0)
        absorb(2 * jj + 1, 1, False)
        return carry

    lax.fori_loop(0, i, body, 0)
    scores(2 * i + 1, 1)
    absorb(2 * i, 0, True)
    absorb(2 * i + 1, 1, True)

    o = (acc_ref[0, :w, :] / acc_ref[0, w:w + 1, :]
         - lam_full * (acc_ref[1, :w, :] / acc_ref[1, w:w + 1, :]))
    o = o * lax.rsqrt(jnp.mean(o * o, axis=0, keepdims=True) + SUBLN_EPS) * g_ref[...]
    o = (o * (1.0 - lam_init)).T
    o_ref[...] = (o * _silu(z_ref[...].astype(F32))).astype(o_ref.dtype)


def diff_attention_core(qk, z, vt, lam, subln_g, batch, seq, heads, lam_init):
    m = qk.shape[0]
    bk = vt.shape[-1]
    bq = 2 * bk
    nq = seq // bq
    nk = seq // bk
    w = A_V_DIM
    return pl.pallas_call(
        functools.partial(_diff_attn_kernel, bq=bq, bk=bk, lam_init=lam_init),
        out_shape=jax.ShapeDtypeStruct((m, heads * w), BF16),
        grid=(batch, heads, nq),
        in_specs=[pl.BlockSpec((4, A_HEAD_DIM), lambda b, h, i: (0, 0)),
                  pl.BlockSpec((bq, w), lambda b, h, i: (b * nq + i, h)),
                  pl.BlockSpec((seq, w), lambda b, h, i: (b, heads + h)),
                  pl.BlockSpec((None, nk, w, bk), lambda b, h, i: (b, 0, h, 0)),
                  pl.BlockSpec((bq, w), lambda b, h, i: (b * nq + i, h)),
                  pl.BlockSpec((w, 1), lambda b, h, i: (0, 0))],
        out_specs=pl.BlockSpec((bq, w), lambda b, h, i: (b * nq + i, h)),
        scratch_shapes=[pltpu.VMEM((2, 1, bq), F32), pltpu.VMEM((2, w + ONES_ROWS, bq), F32),
                        pltpu.VMEM((2, 2, bk, bq), F32)],
        compiler_params=_params("parallel", "parallel", "parallel"),
        name="diff_attention",
    )(lam, qk, qk, vt, z, subln_g.reshape(w, 1))


def diff_attention_layer(x, hn, tables, w_in, lam, subln_g, w_out, batch, seq, lam_init):
    d = x.shape[1]
    heads = d // A_V_DIM
    qk_w = heads * 2 * A_HEAD_DIM
    v_w = heads * A_V_DIM
    blk = min(ATTN_BLOCK, seq // 2)
    w_vt = w_in[:, 2 * qk_w:2 * qk_w + v_w].T.astype(BF16)
    qk = attn_qk_proj(hn, w_in[:, :2 * qk_w].astype(BF16), tables)
    z = matmul(hn, w_in[:, 2 * qk_w + v_w:].astype(BF16), BF16, name="attn_gate_proj")
    vt = attn_v_proj_t(hn, w_vt, batch, seq, blk)
    o = diff_attention_core(qk, z, vt, lam, subln_g, batch, seq, heads, lam_init)
    return matmul_residual(o, w_out.astype(BF16), x, name="attn_out_proj")


def _gdn_conv_kernel(a_ref, w_ref, cw_ref, o_ref, tail_ref, *, rows_per_seq):
    i = pl.program_id(1)
    acc = _dot(a_ref[...], w_ref[...])
    tm = acc.shape[0]

    @pl.when((i * tm) % rows_per_seq == 0)
    def _():
        tail_ref[...] = jnp.zeros(tail_ref.shape, F32)

    tail = tail_ref[...]
    sub = _iota(tail.shape, 0)

    def shifted(s):
        xs = pltpu.roll(acc, s, 0)
        head = jnp.where(sub < s, pltpu.roll(tail, s, 0), xs[:8])
        return jnp.concatenate([head, xs[8:]], axis=0)

    cw = cw_ref[...]
    last = C_CONV_WIDTH - 1
    y = shifted(last) * cw[0:1]
    for t in range(1, last):
        y = y + shifted(last - t) * cw[t:t + 1]
    y = y + acc * cw[last:last + 1]
    tail_ref[...] = acc[tm - 8:]
    o_ref[...] = _silu(y).astype(o_ref.dtype)


def gdn_conv_proj(hn, w, conv_w, seq, *, tm=1024, tn=1024):
    m, k = hn.shape
    n = w.shape[1]
    tm, tn = min(tm, seq), min(tn, n)
    return pl.pallas_call(
        functools.partial(_gdn_conv_kernel, rows_per_seq=seq),
        out_shape=jax.ShapeDtypeStruct((m, n), BF16),
        grid=(n // tn, m // tm),
        in_specs=[pl.BlockSpec((tm, k), lambda j, i: (i, 0)),
                  pl.BlockSpec((k, tn), lambda j, i: (0, j)),
                  pl.BlockSpec((C_CONV_WIDTH, tn), lambda j, i: (0, j))],
        out_specs=pl.BlockSpec((tm, tn), lambda j, i: (i, j)),
        scratch_shapes=[pltpu.VMEM((8, tn), F32)],
        compiler_params=_params("arbitrary", "arbitrary"),
        name="gdn_conv_proj",
    )(hn, w, conv_w)


def _pair_masks():
    n = 2 * CHUNK
    r, c = _iota((n, n), 0), _iota((n, n), 1)
    same = (r // CHUNK) == (c // CHUNK)
    return same & (r >= c), same & (r > c), r == c


def _dot_x2l(a, b_exact, dims=NN):
    h, l = _split2(a)
    return _dot(h, b_exact, dims) + _dot(l, b_exact, dims)


def _dot_x2r(a_exact, b, dims=NN):
    h, l = _split2(b)
    return _dot(a_exact, h, dims) + _dot(a_exact, l, dims)


def _inverse_stages(p_ref, t_ref, count, nilpotency):
    for _ in range(int(math.log2(nilpotency)) - 1):
        for i in range(count):
            p_ref[i] = _dot(p_ref[i], p_ref[i]).astype(BF16)
        for i in range(count):
            t = t_ref[i]
            t_ref[i] = t + _dot(t.astype(BF16), p_ref[i])


def _gdn_gates_kernel(a_ref, w_ref, alog_ref, dtb_ref, beta_ref, gc_ref):
    acc = _dot(a_ref[...], w_ref[...])
    beta_ref[...] = jax.nn.sigmoid(acc[:, :LANES]).astype(beta_ref.dtype)
    g = -jnp.exp(alog_ref[...]) * _softplus(acc[:, LANES:] + dtb_ref[...])
    tri_b = _pair_masks()[0].astype(BF16)
    n2 = 2 * CHUNK
    for n in range(g.shape[0] // n2):
        rows = slice(n * n2, (n + 1) * n2)
        gc_ref[rows, :] = _dot_xr(tri_b, g[rows])


def gdn_gates(hn, w_ba, alog_row, dtb_row, *, tm=512):
    m, k = hn.shape
    tm = min(tm, m)
    row = pl.BlockSpec((1, LANES), lambda i: (0, 0))
    out = pl.BlockSpec((tm, LANES), lambda i: (i, 0))
    return pl.pallas_call(
        _gdn_gates_kernel,
        out_shape=(jax.ShapeDtypeStruct((m, LANES), BF16), jax.ShapeDtypeStruct((m, LANES), F32)),
        grid=(m // tm,),
        in_specs=[pl.BlockSpec((tm, k), lambda i: (i, 0)), pl.BlockSpec((k, 2 * LANES), lambda i: (0, 0)), row, row],
        out_specs=(out, out),
        compiler_params=_params("parallel"),
        name="gdn_gates",
    )(hn, w_ba, alog_row, dtb_row)


def _gdn_prep_kernel(q_ref, k_ref, v_ref, beta_ref, gc_ref,
                     w_ref, qg_ref, u_ref, att_ref, kdt_ref, dec_ref,
                     g_s, dec_s, kn_s, kb_s, rhs_s, p_s, t_s):
    h = pl.program_id(1)
    c = CHUNK
    n2 = 2 * c
    dk = C_HEAD_DIM
    npairs = q_ref.shape[0] // n2
    tri, strict, diag = _pair_masks()
    eye_b = diag.astype(BF16)
    eye_f = diag.astype(F32)
    sel = (_iota((LANES, dk), 0) == h).astype(BF16)
    own_lane = _iota((n2, LANES), 1) == h
    own_b = own_lane.astype(BF16)
    upper_half = _iota((n2, dk), 0) < c
    pairs = [slice(n * n2, (n + 1) * n2) for n in range(npairs)]

    for n, rows in enumerate(pairs):
        gc_all = gc_ref[rows, :]
        gc = jnp.broadcast_to(jnp.sum(jnp.where(own_lane, gc_all, 0.0), axis=-1, keepdims=True), (n2, dk))
        gc_row = _dot_x2r(own_b, gc_all, NT)
        g_s[n] = gc
        dec_s[n] = jnp.where(tri, jnp.exp(jnp.where(tri, gc - gc_row, 0.0)), 0.0)
    for n, rows in enumerate(pairs):
        gc = g_s[n]
        qf = q_ref[rows, :].astype(F32)
        kf = k_ref[rows, :].astype(F32)
        vf = v_ref[rows, :].astype(F32)
        qn = qf * lax.rsqrt(jnp.sum(qf * qf, axis=-1, keepdims=True) + 1e-6) * (dk ** -0.5)
        kn = kf * lax.rsqrt(jnp.sum(kf * kf, axis=-1, keepdims=True) + 1e-6)
        beta = _dot(beta_ref[rows, :], sel)
        egc = jnp.exp(gc)
        kb = kn * beta
        kn_b = kn.astype(BF16)
        qn_b = qn.astype(BF16)
        kn_s[n] = kn_b
        kb_s[n] = kb.astype(BF16)
        rhs_s[n] = jnp.concatenate([vf * beta, kb * egc], axis=1).astype(BF16)
        g_last = jnp.where(upper_half, gc[c - 1:c, :], gc[n2 - 1:n2, :])
        kdec = kn * jnp.exp(g_last - gc)
        qg_ref[rows, :] = (qn * egc).astype(BF16)
        att_ref[rows, :] = (_dot(qn_b, kn_b, NT) * dec_s[n]).astype(BF16)
        kdt_ref[:, rows] = _dot(eye_b, kdec.astype(BF16), NT).astype(BF16)
        dec_ref[2 * n:2 * n + 1, :] = jnp.exp(gc[c - 1:c, :])
        dec_ref[2 * n + 1:2 * n + 2, :] = jnp.exp(gc[n2 - 1:n2, :])
    for n, rows in enumerate(pairs):
        neg_a = jnp.where(strict, -(_dot(kb_s[n], kn_s[n], NT) * dec_s[n]), 0.0)
        p_s[n] = neg_a.astype(BF16)
        t_s[n] = eye_f + neg_a
    _inverse_stages(p_s, t_s, npairs, c)
    for n, rows in enumerate(pairs):
        sol = _dot(t_s[n].astype(BF16), rhs_s[n])
        u_ref[rows, :] = sol[:, :dk]
        w_ref[rows, :] = sol[:, dk:].astype(BF16)


def _gdn_scan_kernel(w_ref, qg_ref, u_ref, att_ref, kdt_ref, dec_ref, z_ref, g_ref, o_ref,
                     s_ref, ms_s, vp_s, *, group):
    c = CHUNK
    dk = C_HEAD_DIM
    zeros = jnp.zeros((c, dk), BF16)

    @pl.when(pl.program_id(2) == 0)
    def _():
        s_ref[...] = jnp.zeros(s_ref.shape, F32)

    for n in range(w_ref.shape[1] // c):
        rows = slice(n * c, (n + 1) * c)
        pair = slice((n // 2) * 2 * c, (n // 2 + 1) * 2 * c)
        for gi in range(group):
            lhs = jnp.concatenate([w_ref[gi, rows, :], qg_ref[gi, rows, :]], axis=0)
            ms_s[gi] = _dot(lhs, s_ref[gi].astype(BF16))
        for gi in range(group):
            v_new = (u_ref[gi, rows, :] - ms_s[gi, :c, :]).astype(BF16)
            vp_s[gi] = jnp.concatenate([v_new, zeros] if n % 2 == 0 else [zeros, v_new], axis=0)
        for gi in range(group):
            s_ref[gi] = s_ref[gi] * dec_ref[gi, n:n + 1, :] + _dot(kdt_ref[gi, :, pair], vp_s[gi])
        for gi in range(group):
            o = ms_s[gi, c:, :] + _dot(att_ref[gi, rows, :], vp_s[gi])
            o = o * lax.rsqrt(jnp.mean(o * o, axis=-1, keepdims=True) + NORM_EPS) * g_ref[...]
            z = z_ref[rows, gi * dk:(gi + 1) * dk].astype(F32)
            o_ref[rows, gi * dk:(gi + 1) * dk] = (o * _silu(z)).astype(o_ref.dtype)


def gated_deltanet_layer(x, hn, w_in, conv_w, a_log, dt_bias, norm_g, w_out, batch, seq):
    m, d = x.shape
    dk = C_HEAD_DIM
    k_heads = d // dk
    v_heads = 2 * k_heads
    conv_ch = 2 * k_heads * dk + v_heads * dk
    main_w = conv_ch + v_heads * dk
    qkv = gdn_conv_proj(hn, w_in[:, :conv_ch].astype(BF16), conv_w, seq)
    z = matmul(hn, w_in[:, conv_ch:main_w].astype(BF16), BF16, name="gdn_gate_proj")
    pad = jnp.zeros((d, LANES - v_heads), F32)
    w_ba = jnp.concatenate([w_in[:, main_w:main_w + v_heads], pad, w_in[:, main_w + v_heads:], pad], axis=1)
    row_pad = jnp.zeros((LANES - v_heads,), F32)
    alog_row = jnp.concatenate([a_log, row_pad]).reshape(1, LANES)
    dtb_row = jnp.concatenate([dt_bias, row_pad]).reshape(1, LANES)
    beta_all, gc_all = gdn_gates(hn, w_ba.astype(BF16), alog_row, dtb_row)

    rows = min(PREP_ROWS, seq)
    nr = seq // rows
    ncr = rows // CHUNK
    npair = rows // (2 * CHUNK)
    hv = v_heads
    bh_t = lambda dt, last: jax.ShapeDtypeStruct((batch, hv, seq, last), dt)
    blk4 = lambda last: pl.BlockSpec((None, None, rows, last), lambda b, h, i: (b, h, i, 0))
    qoff, koff, voff = 0, k_heads, 2 * k_heads
    w_, qg_, u_, att_, kdt_, dec_ = pl.pallas_call(
        _gdn_prep_kernel,
        out_shape=(bh_t(BF16, dk), bh_t(BF16, dk), bh_t(F32, dk), bh_t(BF16, 2 * CHUNK),
                   jax.ShapeDtypeStruct((batch, hv, dk, seq), BF16),
                   jax.ShapeDtypeStruct((batch, hv, seq // CHUNK, dk), F32)),
        grid=(batch, hv, nr),
        in_specs=[pl.BlockSpec((rows, dk), lambda b, h, i: (b * nr + i, qoff + h // 2)),
                  pl.BlockSpec((rows, dk), lambda b, h, i: (b * nr + i, koff + h // 2)),
                  pl.BlockSpec((rows, dk), lambda b, h, i: (b * nr + i, voff + h)),
                  pl.BlockSpec((rows, LANES), lambda b, h, i: (b * nr + i, 0)),
                  pl.BlockSpec((rows, LANES), lambda b, h, i: (b * nr + i, 0))],
        out_specs=(blk4(dk), blk4(dk), blk4(dk), blk4(2 * CHUNK),
                   pl.BlockSpec((None, None, dk, rows), lambda b, h, i: (b, h, 0, i)),
                   pl.BlockSpec((None, None, ncr, dk), lambda b, h, i: (b, h, i, 0))),
        scratch_shapes=[pltpu.VMEM((npair, dk, dk), F32), pltpu.VMEM((npair, dk, dk), F32),
                        pltpu.VMEM((npair, dk, dk), BF16), pltpu.VMEM((npair, dk, dk), BF16),
                        pltpu.VMEM((npair, dk, 2 * dk), BF16), pltpu.VMEM((npair, dk, dk), BF16),
                        pltpu.VMEM((npair, dk, dk), F32)],
        compiler_params=_params("parallel", "parallel", "parallel"),
        name="gdn_prep",
    )(qkv, qkv, qkv, beta_all, gc_all)

    rows = min(SCAN_ROWS, seq)
    nr = seq // rows
    ncr = rows // CHUNK
    group = 8
    gblk = lambda last: pl.BlockSpec((None, group, rows, last), lambda b, h, i: (b, h, i, 0))
    o = pl.pallas_call(
        functools.partial(_gdn_scan_kernel, group=group),
        out_shape=jax.ShapeDtypeStruct((m, hv * dk), BF16),
        grid=(batch, hv // group, nr),
        in_specs=[gblk(dk), gblk(dk), gblk(dk), gblk(2 * CHUNK),
                  pl.BlockSpec((None, group, dk, rows), lambda b, h, i: (b, h, 0, i)),
                  pl.BlockSpec((None, group, ncr, dk), lambda b, h, i: (b, h, i, 0)),
                  pl.BlockSpec((rows, group * dk), lambda b, h, i: (b * nr + i, h)),
                  pl.BlockSpec((1, dk), lambda b, h, i: (0, 0))],
        out_specs=pl.BlockSpec((rows, group * dk), lambda b, h, i: (b * nr + i, h)),
        scratch_shapes=[pltpu.VMEM((group, dk, dk), F32), pltpu.VMEM((group, 2 * CHUNK, dk), F32),
                        pltpu.VMEM((group, 2 * CHUNK, dk), BF16)],
        compiler_params=_params("parallel", "parallel", "arbitrary"),
        name="gdn_scan",
    )(w_, qg_, u_, att_, kdt_, dec_, z, norm_g.reshape(1, dk))
    return matmul_residual(o, w_out.astype(BF16), x, tm=512, name="gdn_out_proj")


def _rwkv_mix_kernel(x_ref, g_ref, mu_ref, o_ref, tail_ref, *, rows_per_seq):
    i = pl.program_id(0)
    x = x_ref[...]
    tm = x.shape[0]
    hn = x * lax.rsqrt(jnp.mean(x * x, axis=-1, keepdims=True) + NORM_EPS) * g_ref[...]

    @pl.when((i * tm) % rows_per_seq == 0)
    def _():
        tail_ref[...] = jnp.zeros(tail_ref.shape, F32)

    prev = jnp.where(_iota(hn.shape, 0) == 0, tail_ref[7:8, :], pltpu.roll(hn, 1, 0))
    tail_ref[...] = hn[tm - 8:]
    xx = prev - hn
    for c in range(o_ref.shape[0]):
        o_ref[c] = (hn + xx * mu_ref[c:c + 1, :]).astype(o_ref.dtype)


def rwkv_token_mix(x, norm_g, mu, seq, *, tm=256):
    m, d = x.shape
    tm = min(tm, seq)
    nmix = mu.shape[0]
    return pl.pallas_call(
        functools.partial(_rwkv_mix_kernel, rows_per_seq=seq),
        out_shape=jax.ShapeDtypeStruct((nmix, m, d), BF16),
        grid=(m // tm,),
        in_specs=[pl.BlockSpec((tm, d), lambda i: (i, 0)),
                  pl.BlockSpec((1, d), lambda i: (0, 0)),
                  pl.BlockSpec((nmix, d), lambda i: (0, 0))],
        out_specs=pl.BlockSpec((nmix, tm, d), lambda i: (0, i, 0)),
        scratch_shapes=[pltpu.VMEM((8, d), F32)],
        compiler_params=_params("arbitrary"),
        name="rwkv_token_mix",
    )(x, norm_g.reshape(1, d), mu)


def _grouped_mm_kernel(a_ref, w_ref, o_ref):
    o_ref[...] = _dot(a_ref[...], w_ref[...]).astype(o_ref.dtype)


def grouped_matmul(a, w, out_dtype, *, tm=1024, tn=1024):
    g, k, n = w.shape
    m = a.shape[1]
    tm, tn = min(tm, m), min(tn, n)
    return pl.pallas_call(
        _grouped_mm_kernel,
        out_shape=jax.ShapeDtypeStruct((g, m, n), out_dtype),
        grid=(g, n // tn, m // tm),
        in_specs=[pl.BlockSpec((None, tm, k), lambda c, j, i: (c, i, 0)),
                  pl.BlockSpec((None, k, tn), lambda c, j, i: (c, 0, j))],
        out_specs=pl.BlockSpec((None, tm, tn), lambda c, j, i: (c, i, j)),
        compiler_params=_params("parallel", "parallel", "parallel"),
        name="rwkv_rkvg_proj",
    )(a, w)


def _rwkv_lora_kernel(xw_ref, xa_ref, w1_ref, w2_ref, a1_ref, a2_ref, w0_ref, a0_ref, ld_ref, a_ref):
    hw = jnp.tanh(_dot(xw_ref[...], w1_ref[...])).astype(BF16)
    lw = w0_ref[...] + _dot(hw, w2_ref[...])
    log_w = -_softplus(-lw) - 0.5
    ld_ref[...] = -jnp.exp(log_w)
    ha = _dot(xa_ref[...], a1_ref[...]).astype(BF16)
    a_ref[...] = jax.nn.sigmoid(a0_ref[...] + _dot(ha, a2_ref[...]))


def rwkv_lora(xs, w1, w2, a1, a2, w0, a0, *, tm=512):
    _, m, d = xs.shape
    tm = min(tm, m)
    r = w1.shape[1]
    row = pl.BlockSpec((1, d), lambda i: (0, 0))
    shape = jax.ShapeDtypeStruct((m, d), F32)
    return pl.pallas_call(
        _rwkv_lora_kernel,
        out_shape=(shape, shape),
        grid=(m // tm,),
        in_specs=[pl.BlockSpec((None, tm, d), lambda i: (4, i, 0)),
                  pl.BlockSpec((None, tm, d), lambda i: (5, i, 0)),
                  pl.BlockSpec((d, r), lambda i: (0, 0)), pl.BlockSpec((r, d), lambda i: (0, 0)),
                  pl.BlockSpec((d, r), lambda i: (0, 0)), pl.BlockSpec((r, d), lambda i: (0, 0)),
                  row, row],
        out_specs=(pl.BlockSpec((tm, d), lambda i: (i, 0)), pl.BlockSpec((tm, d), lambda i: (i, 0))),
        compiler_params=_params("parallel"),
        name="rwkv_lora",
    )(xs, xs, w1, w2, a1, a2, w0.reshape(1, d), a0.reshape(1, d))


def _head_sum(x, first_head):
    tot = jnp.sum(x, axis=-1, keepdims=True)
    lo = jnp.sum(jnp.where(first_head, x, 0.0), axis=-1, keepdims=True)
    return jnp.where(first_head, lo, tot - lo)


def _rwkv_prep_kernel(r_ref, k_ref, v_ref, ld_ref, a_ref, kk_ref, ka_ref, rk_ref,
                      wt_ref, rt_ref, u_ref, aro_ref, kbt_ref, vb_ref, dec_ref, bonus_ref,
                      kkt_s, kb_s, akk_s, p_s, t_s, drow_s, *, cps):
    c = CHUNK
    n2 = 2 * c
    n_h = B_HEAD_DIM
    npairs = r_ref.shape[0] // n2
    first_head = _iota((1, LANES), 1) < n_h
    tri, strict, diag = _pair_masks()
    tri_b = tri.astype(BF16)
    eye_b = diag.astype(BF16)
    eye_f = diag.astype(F32)
    upper_half = _iota((n2, LANES), 0) < c
    pairs = [slice(n * n2, (n + 1) * n2) for n in range(npairs)]

    for n, rows in enumerate(pairs):
        r, k, v, a = r_ref[rows, :], k_ref[rows, :], v_ref[rows, :], a_ref[rows, :]
        kk_raw = k * kk_ref[...]
        kk = kk_raw * lax.rsqrt(_head_sum(kk_raw * kk_raw, first_head) + 1e-6)
        k2c = k * (1.0 + (a - 1.0) * ka_ref[...])
        bbc = kk * a
        bonus_ref[rows, :] = _head_sum(r * k2c * rk_ref[...], first_head) * v
        vb_ref[rows, :] = v.astype(BF16)
        ld = ld_ref[rows, :]
        cs = _dot_x2r(tri_b, ld)
        cs_last = jnp.where(upper_half, cs[c - 1:c, :], cs[n2 - 1:n2, :])
        w_inv = jnp.exp(-cs)
        w_end = jnp.exp(cs_last - cs)
        rt_ref[rows, :] = (r * jnp.exp(cs)).astype(BF16)
        kkt_s[n] = (kk * jnp.exp(cs - ld)).astype(BF16)
        kb_s[n] = jnp.concatenate([k2c * w_inv, bbc * w_inv], axis=0).astype(BF16)
        kw, bw = (k2c * w_end).astype(BF16), (-(bbc * w_end)).astype(BF16)
        for half in range(2):
            hs = slice(half * c, (half + 1) * c)
            kb_end = jnp.concatenate([kw[hs], bw[hs]], axis=0)
            cols = slice((2 * n + half) * n2, (2 * n + half + 1) * n2)
            kbt_ref[:, cols] = _dot(eye_b, kb_end, NT).astype(BF16)
        blk, off = divmod(2 * n, cps)
        drow_s[blk, off:off + 1, :] = jnp.exp(cs[c - 1:c, :])
        drow_s[blk, off + 1:off + 2, :] = jnp.exp(cs[n2 - 1:n2, :])
    for blk in range(dec_ref.shape[0] // LANES):
        drow_s[blk, cps:, :] = jnp.zeros((LANES - cps, LANES), F32)
        dec_ref[blk * LANES:(blk + 1) * LANES, :] = _dot_xr(eye_b, drow_s[blk], NT)
    zero_b = jnp.zeros((n2, LANES), BF16)
    strict2 = jnp.concatenate([strict, strict], axis=1)
    tri2 = jnp.concatenate([tri, tri], axis=1)
    sign2 = jnp.where(_iota((n2, 2 * n2), 1) < n2, 1.0, -1.0)
    for n, rows in enumerate(pairs):
        kkt, rt = kkt_s[n], rt_ref[rows, :]
        lhs = jnp.concatenate([jnp.where(first_head, kkt, zero_b), jnp.where(first_head, rt, zero_b),
                               jnp.where(first_head, zero_b, kkt), jnp.where(first_head, zero_b, rt)], axis=0)
        gram = _dot(lhs, kb_s[n], NT)
        for hh in range(2):
            i = 2 * n + hh
            g_kk = jnp.where(strict2, gram[2 * hh * n2:(2 * hh + 1) * n2], 0.0)
            neg_a = -g_kk[:, n2:]
            p_s[i] = neg_a.astype(BF16)
            t_s[i] = eye_f + neg_a
            akk_s[i] = g_kk[:, :n2].astype(BF16)
            g_r = gram[(2 * hh + 1) * n2:(2 * hh + 2) * n2]
            aro_ref[hh, rows, :] = jnp.where(tri2, g_r * sign2, 0.0).astype(BF16)
    _inverse_stages(p_s, t_s, 2 * npairs, c)
    for n, rows in enumerate(pairs):
        av = _dot(jnp.concatenate([akk_s[2 * n], akk_s[2 * n + 1]], axis=0), vb_ref[rows, :])
        p_s[2 * n] = av[:n2].astype(BF16)
        p_s[2 * n + 1] = av[n2:].astype(BF16)
    for n, rows in enumerate(pairs):
        t0, t1 = t_s[2 * n].astype(BF16), t_s[2 * n + 1].astype(BF16)
        wt = _dot(jnp.concatenate([t0, t1], axis=0), kkt_s[n])
        wt_ref[rows, :] = jnp.where(first_head, wt[:n2], wt[n2:]).astype(BF16)
        u_ref[rows, :] = jnp.where(first_head, _dot(t0, p_s[2 * n]), _dot(t1, p_s[2 * n + 1]))


def _rwkv_scan_kernel(wt_ref, rt_ref, u_ref, aro_ref, kbt_ref, vb_ref, dec_ref, bonus_ref, gate_ref,
                      lnw_ref, lnb_ref, o_ref, s_ref, ms_s, xc_s, xp_s, *, group):
    c = CHUNK
    n2 = 2 * c
    n_h = B_HEAD_DIM
    first_head = _iota((1, LANES), 1) < n_h
    same_head = _iota((LANES, LANES), 0) // n_h == _iota((LANES, LANES), 1) // n_h
    zeros = jnp.zeros((c, LANES), BF16)

    def head_mean(x):
        return _head_sum(x, first_head) * (1.0 / n_h)

    @pl.when(pl.program_id(2) == 0)
    def _():
        s_ref[...] = jnp.zeros(s_ref.shape, F32)

    for n in range(u_ref.shape[1] // c):
        rows = slice(n * c, (n + 1) * c)
        for gi in range(group):
            lhs = jnp.concatenate([wt_ref[gi, rows, :], rt_ref[gi, rows, :]], axis=0)
            ms_s[gi] = _dot(lhs, s_ref[gi].astype(BF16))
        for gi in range(group):
            sa = (u_ref[gi, rows, :] + ms_s[gi, :c, :]).astype(BF16)
            vb = vb_ref[gi, rows, :]
            xc_s[gi] = jnp.concatenate([vb, sa], axis=0)
            xp_s[gi] = jnp.concatenate([vb, zeros, sa, zeros] if n % 2 == 0 else [zeros, vb, zeros, sa], axis=0)
        for gi in range(group):
            upd = _dot(kbt_ref[gi, :, n * n2:(n + 1) * n2], xc_s[gi])
            dec_col = jnp.broadcast_to(dec_ref[gi, :, n:n + 1], (LANES, LANES))
            s_ref[gi] = s_ref[gi] * dec_col + jnp.where(same_head, upd, 0.0)
        for gi in range(group):
            both = _dot(jnp.concatenate([aro_ref[gi, 0, rows, :], aro_ref[gi, 1, rows, :]], axis=0), xp_s[gi])
            o = ms_s[gi, c:, :] + jnp.where(first_head, both[:c], both[c:])
            mean = head_mean(o)
            dlt = o - mean
            var = head_mean(dlt * dlt)
            cols = slice(gi * LANES, (gi + 1) * LANES)
            y = dlt * lax.rsqrt(var + B_GN_EPS) * lnw_ref[:, cols] + lnb_ref[:, cols]
            y = y + bonus_ref[rows, cols]
            o_ref[rows, cols] = (y * _silu(gate_ref[rows, cols])).astype(o_ref.dtype)


def rwkv7_layer(x, norm_g, mu, w_rkvg, w0, w_w1, w_w2, a0, w_a1, w_a2, k_k, k_a, r_k, ln_w, ln_b, w_out,
                batch, seq):
    m, d = x.shape
    heads = d // B_HEAD_DIM
    pairs = heads // 2
    order = jnp.array([0, 2, 3, 5, 1, 4])
    xs = rwkv_token_mix(x, norm_g, mu[order], seq)
    rkvg = grouped_matmul(xs, w_rkvg.astype(BF16), F32)
    lora = w_w1.shape[1]
    padc = lambda w: jnp.pad(w, ((0, 0), (0, LORA_PAD - lora))).astype(BF16)
    padr = lambda w: jnp.pad(w, ((0, LORA_PAD - lora), (0, 0))).astype(BF16)
    ld, a = rwkv_lora(xs, padc(w_w1), padr(w_w2), padc(w_a1), padr(w_a2), w0, a0)

    srows = min(SCAN_ROWS, seq)
    rows = min(PREP_ROWS, seq)
    nr = seq // rows
    npair = rows // (2 * CHUNK)
    col = lambda g: pl.BlockSpec((None, rows, LANES), lambda b, p, i, g=g: (g, b * nr + i, p))
    flat = pl.BlockSpec((rows, LANES), lambda b, p, i: (b * nr + i, p))
    prow = pl.BlockSpec((1, LANES), lambda b, p, i: (0, p))
    bp = lambda rws, last, dt: jax.ShapeDtypeStruct((batch, pairs, rws, last), dt)
    pblk = lambda rws, last: pl.BlockSpec((None, None, rws, last), lambda b, p, i: (b, p, i, 0))
    wt_, rt_, u_, aro_, kbt_, vb_, dec_, bonus = pl.pallas_call(
        functools.partial(_rwkv_prep_kernel, cps=srows // CHUNK),
        out_shape=(bp(seq, LANES, BF16), bp(seq, LANES, BF16), bp(seq, LANES, F32),
                   jax.ShapeDtypeStruct((batch, pairs, 2, seq, 4 * CHUNK), BF16),
                   bp(LANES, 2 * seq, BF16), bp(seq, LANES, BF16), bp(seq // srows * LANES, LANES, F32),
                   jax.ShapeDtypeStruct((m, d), F32)),
        grid=(batch, pairs, nr),
        in_specs=[col(0), col(1), col(2), flat, flat, prow, prow, prow],
        out_specs=(pblk(rows, LANES), pblk(rows, LANES), pblk(rows, LANES),
                   pl.BlockSpec((None, None, 2, rows, 4 * CHUNK), lambda b, p, i: (b, p, 0, i, 0)),
                   pl.BlockSpec((None, None, LANES, 2 * rows), lambda b, p, i: (b, p, 0, i)),
                   pblk(rows, LANES), pblk(rows // srows * LANES, LANES), flat),
        scratch_shapes=[pltpu.VMEM((npair, LANES, LANES), BF16), pltpu.VMEM((npair, 2 * LANES, LANES), BF16),
                        pltpu.VMEM((2 * npair, LANES, LANES), BF16), pltpu.VMEM((2 * npair, LANES, LANES), BF16),
                        pltpu.VMEM((2 * npair, LANES, LANES), F32),
                        pltpu.VMEM((rows // srows, LANES, LANES), F32)],
        compiler_params=_params("parallel", "parallel", "parallel"),
        name="rwkv_prep",
    )(rkvg, rkvg, rkvg, ld, a, k_k.reshape(1, d), k_a.reshape(1, d), r_k.reshape(1, d))

    rows = srows
    nr = seq // rows
    group = 8
    gw = group * LANES
    gblk = lambda rws, last: pl.BlockSpec((None, group, rws, last), lambda b, p, i: (b, p, i, 0))
    gflat = pl.BlockSpec((rows, gw), lambda b, p, i: (b * nr + i, p))
    grow = pl.BlockSpec((1, gw), lambda b, p, i: (0, p))
    o = pl.pallas_call(
        functools.partial(_rwkv_scan_kernel, group=group),
        out_shape=jax.ShapeDtypeStruct((m, d), BF16),
        grid=(batch, pairs // group, nr),
        in_specs=[gblk(rows, LANES), gblk(rows, LANES), gblk(rows, LANES),
                  pl.BlockSpec((None, group, 2, rows, 4 * CHUNK), lambda b, p, i: (b, p, 0, i, 0)),
                  pl.BlockSpec((None, group, LANES, 2 * rows), lambda b, p, i: (b, p, 0, i)),
                  gblk(rows, LANES), gblk(LANES, LANES), gflat,
                  pl.BlockSpec((None, rows, gw), lambda b, p, i: (3, b * nr + i, p)),
                  grow, grow],
        out_specs=gflat,
        scratch_shapes=[pltpu.VMEM((group, LANES, LANES), F32), pltpu.VMEM((group, 2 * CHUNK, LANES), F32),
                        pltpu.VMEM((group, 2 * CHUNK, LANES), BF16), pltpu.VMEM((group, 4 * CHUNK, LANES), BF16)],
        compiler_params=_params("parallel", "parallel", "arbitrary"),
        name="rwkv_scan",
    )(wt_, rt_, u_, aro_, kbt_, vb_, dec_, bonus, rkvg, ln_w.reshape(1, d), ln_b.reshape(1, d))
    return matmul_residual(o, w_out.astype(BF16), x, name="rwkv_out_proj")


def kernel(x, p, positions, norm_g, pe_norm_g, pe_w_gate, pe_w_proj, final_norm_g, a_w_in, a_lam, a_subln_g, a_w_out, b_mu, b_w_rkvg, b_w0, b_w_w1, b_w_w2, b_a0, b_w_a1, b_w_a2, b_k_k, b_k_a, b_r_k, b_ln_w, b_ln_b, b_w_out, c_w_in, c_conv_w, c_A_log, c_dt_bias, c_norm_g, c_w_out):
    batch, seq, d = x.shape
    depth = p.shape[0]
    m = batch * seq
    xf = x.reshape(m, d)
    tables = rope_tables(positions)
    for i in range(depth):
        kind = i % N_MIXERS
        j = i // N_MIXERS
        if kind == 0:
            lam_init = 0.8 - 0.6 * math.exp(-0.3 * i)
            hn = rmsnorm(xf, norm_g[i], BF16)
            xf = diff_attention_layer(xf, hn, tables, a_w_in[j], a_lam[j], a_subln_g[j], a_w_out[j],
                                      batch, seq, lam_init)
        elif kind == 1:
            xf = rwkv7_layer(xf, norm_g[i], b_mu[j], b_w_rkvg[j], b_w0[j], b_w_w1[j], b_w_w2[j], b_a0[j],
                             b_w_a1[j], b_w_a2[j], b_k_k[j], b_k_a[j], b_r_k[j], b_ln_w[j], b_ln_b[j],
                             b_w_out[j], batch, seq)
        else:
            hn = rmsnorm(xf, norm_g[i], BF16)
            xf = gated_deltanet_layer(xf, hn, c_w_in[j], c_conv_w[j], c_A_log[j], c_dt_bias[j], c_norm_g[j],
                                      c_w_out[j], batch, seq)
        hn2 = rmsnorm(xf, pe_norm_g[i], BF16)
        xf = per_layer_embedding(xf, hn2, pe_w_gate[i].astype(BF16), p[i].reshape(m, -1),
                                 pe_w_proj[i].astype(BF16))
    return rmsnorm(xf, final_norm_g, F32).reshape(batch, seq, d)
```

```python
import functools
import math

import jax
import jax.numpy as jnp
from jax import lax
from jax.experimental import pallas as pl
from jax.experimental.pallas import tpu as pltpu

F32 = jnp.float32
BF16 = jnp.bfloat16

N_MIXERS = 3
NORM_EPS = 1e-6
LANES = 128
VMEM_LIMIT = 48 * 1024 * 1024

A_HEAD_DIM = 128
A_V_DIM = 2 * A_HEAD_DIM
ROT_DIM = A_HEAD_DIM // 4
ROPE_THETA = 500000.0
SUBLN_EPS = 1e-5
ATTN_BLOCK = 256
ONES_ROWS = 16

B_HEAD_DIM = 64
B_GN_EPS = 64e-5
LORA_PAD = 128

C_HEAD_DIM = 128
C_CONV_WIDTH = 4
CHUNK = 64
SCAN_ROWS = 512
PREP_ROWS = 1024

NN = (((1,), (0,)), ((), ()))
NT = (((1,), (1,)), ((), ()))


def _dot(a, b, dims=NN):
    return lax.dot_general(a, b, dims, preferred_element_type=F32)


def _split2(x):
    hi = x.astype(BF16)
    lo = (x - hi.astype(F32)).astype(BF16)
    return hi, lo


def _split3(x):
    hi = x.astype(BF16)
    r = x - hi.astype(F32)
    mid = r.astype(BF16)
    lo = (r - mid.astype(F32)).astype(BF16)
    return hi, mid, lo


def _dot3(a, b, dims=NN):
    ah, al = _split2(a)
    bh, bl = _split2(b)
    return _dot(ah, bh, dims) + (_dot(ah, bl, dims) + _dot(al, bh, dims))


def _dot_xl(a, b_exact, dims=NN):
    h, m, l = _split3(a)
    return _dot(h, b_exact, dims) + (_dot(m, b_exact, dims) + _dot(l, b_exact, dims))


def _dot_xr(a_exact, b, dims=NN):
    h, m, l = _split3(b)
    return _dot(a_exact, h, dims) + (_dot(a_exact, m, dims) + _dot(a_exact, l, dims))


def _iota(shape, dim):
    return lax.broadcasted_iota(jnp.int32, shape, dim)


def _silu(x):
    return x * jax.nn.sigmoid(x)


def _softplus(x):
    return jnp.maximum(x, 0.0) + jnp.log(1.0 + jnp.exp(-jnp.abs(x)))


def _params(*sem):
    return pltpu.CompilerParams(dimension_semantics=sem, vmem_limit_bytes=VMEM_LIMIT)


def _inv_unit_lower(a, nilpotency):
    n = a.shape[0]
    eye = (_iota((n, n), 0) == _iota((n, n), 1)).astype(F32)
    p = -a
    t = eye + p
    for _ in range(int(math.log2(nilpotency)) - 1):
        p = _dot3(p, p)
        t = t + _dot3(t, p)
    return t


def _rmsnorm_kernel(x_ref, g_ref, o_ref, *, eps):
    x = x_ref[...]
    y = x * lax.rsqrt(jnp.mean(x * x, axis=-1, keepdims=True) + eps)
    o_ref[...] = (y * g_ref[...]).astype(o_ref.dtype)


def rmsnorm(x, g, out_dtype, *, eps=NORM_EPS, tm=512):
    m, d = x.shape
    tm = min(tm, m)
    return pl.pallas_call(
        functools.partial(_rmsnorm_kernel, eps=eps),
        out_shape=jax.ShapeDtypeStruct((m, d), out_dtype),
        grid=(m // tm,),
        in_specs=[pl.BlockSpec((tm, d), lambda i: (i, 0)), pl.BlockSpec((1, d), lambda i: (0, 0))],
        out_specs=pl.BlockSpec((tm, d), lambda i: (i, 0)),
        compiler_params=_params("parallel"),
        name="rmsnorm",
    )(x, g.reshape(1, d))


def _mm_kernel(a_ref, w_ref, *rest, epilogue):
    o_ref = rest[-1]
    acc = _dot(a_ref[...], w_ref[...])
    if epilogue is not None:
        acc = epilogue(acc, *rest[:-1])
    o_ref[...] = acc.astype(o_ref.dtype)


def matmul(a, w, out_dtype, *, tm=1024, tn=1024, extra=(), extra_specs=(), epilogue=None, name="matmul"):
    m, k = a.shape
    n = w.shape[1]
    tm, tn = min(tm, m), min(tn, n)
    return pl.pallas_call(
        functools.partial(_mm_kernel, epilogue=epilogue),
        out_shape=jax.ShapeDtypeStruct((m, n), out_dtype),
        grid=(n // tn, m // tm),
        in_specs=[pl.BlockSpec((tm, k), lambda j, i: (i, 0)),
                  pl.BlockSpec((k, tn), lambda j, i: (0, j))] + list(extra_specs),
        out_specs=pl.BlockSpec((tm, tn), lambda j, i: (i, j)),
        compiler_params=_params("parallel", "parallel"),
        name=name,
    )(a, w, *extra)


def matmul_residual(a, w, res, *, tm=1024, tn=1024, name="matmul_residual"):
    tm, tn = min(tm, a.shape[0]), min(tn, w.shape[1])
    return matmul(a, w, F32, tm=tm, tn=tn, extra=(res,),
                  extra_specs=(pl.BlockSpec((tm, tn), lambda j, i: (i, j)),),
                  epilogue=lambda acc, r_ref: r_ref[...] + acc, name=name)


def _ple_kernel(h_ref, wg_ref, p_ref, wp_ref, x_ref, o_ref):
    gate = jax.nn.sigmoid(_dot(h_ref[...], wg_ref[...]))
    proj = _dot(p_ref[...].astype(BF16), wp_ref[...])
    o_ref[...] = x_ref[...] + gate * proj


def per_layer_embedding(x, hn, w_gate, p, w_proj, *, tm=1024, tn=1024):
    m, d = x.shape
    pd = p.shape[1]
    tm, tn = min(tm, m), min(tn, d)
    return pl.pallas_call(
        _ple_kernel,
        out_shape=jax.ShapeDtypeStruct((m, d), F32),
        grid=(d // tn, m // tm),
        in_specs=[pl.BlockSpec((tm, d), lambda j, i: (i, 0)),
                  pl.BlockSpec((d, tn), lambda j, i: (0, j)),
                  pl.BlockSpec((tm, pd), lambda j, i: (i, 0)),
                  pl.BlockSpec((pd, tn), lambda j, i: (0, j)),
                  pl.BlockSpec((tm, tn), lambda j, i: (i, j))],
        out_specs=pl.BlockSpec((tm, tn), lambda j, i: (i, j)),
        compiler_params=_params("parallel", "parallel"),
        name="per_layer_embedding",
    )(hn, w_gate, p, w_proj, x)


def _rope_table_kernel(pos_ref, freq_ref, cos_ref, sin_lo_ref, sin_hi_ref):
    half = ROT_DIM // 2
    ang = pos_ref[...].astype(F32) * freq_ref[...]
    lane = _iota(ang.shape, 1)
    c, s = jnp.cos(ang), jnp.sin(ang)
    cos_ref[...] = jnp.where(lane < ROT_DIM, c, 1.0)
    sin_lo_ref[...] = jnp.where(lane < half, -s, 0.0)
    sin_hi_ref[...] = jnp.where((lane >= half) & (lane < ROT_DIM), s, 0.0)


def rope_tables(positions, *, tm=1024):
    m = positions.size
    tm = min(tm, m)
    inv_freq = ROPE_THETA ** (-jnp.arange(0, ROT_DIM, 2, dtype=F32) / ROT_DIM)
    freq_row = jnp.concatenate([inv_freq, inv_freq, jnp.zeros((LANES - ROT_DIM,), F32)]).reshape(1, LANES)
    shape = jax.ShapeDtypeStruct((m, LANES), F32)
    spec = pl.BlockSpec((tm, LANES), lambda i: (i, 0))
    return pl.pallas_call(
        _rope_table_kernel,
        out_shape=(shape, shape, shape),
        grid=(m // tm,),
        in_specs=[pl.BlockSpec((tm, 1), lambda i: (i, 0)), pl.BlockSpec((1, LANES), lambda i: (0, 0))],
        out_specs=(spec, spec, spec),
        compiler_params=_params("parallel"),
        name="rope_tables",
    )(positions.reshape(m, 1), freq_row)


def _attn_qk_kernel(a_ref, w_ref, cos_ref, sin_lo_ref, sin_hi_ref, o_ref, *, n_q_blocks):
    j = pl.program_id(0)
    acc = _dot(a_ref[...], w_ref[...])
    half = ROT_DIM // 2
    scale = jnp.where(j < n_q_blocks, A_HEAD_DIM ** -0.5, 1.0).astype(F32)
    cos, sin_lo, sin_hi = cos_ref[...], sin_lo_ref[...], sin_hi_ref[...]
    for g in range(acc.shape[1] // LANES):
        x = acc[:, g * LANES:(g + 1) * LANES]
        y = x * cos + pltpu.roll(x, LANES - half, 1) * sin_lo + pltpu.roll(x, half, 1) * sin_hi
        o_ref[:, g * LANES:(g + 1) * LANES] = (y * scale).astype(o_ref.dtype)


def attn_qk_proj(hn, w_qk, tables, *, tm=1024, tn=1024):
    m, k = hn.shape
    n = w_qk.shape[1]
    tm, tn = min(tm, m), min(tn, n // 2)
    tspec = pl.BlockSpec((tm, LANES), lambda j, i: (i, 0))
    return pl.pallas_call(
        functools.partial(_attn_qk_kernel, n_q_blocks=n // 2 // tn),
        out_shape=jax.ShapeDtypeStruct((m, n), BF16),
        grid=(n // tn, m // tm),
        in_specs=[pl.BlockSpec((tm, k), lambda j, i: (i, 0)),
                  pl.BlockSpec((k, tn), lambda j, i: (0, j)), tspec, tspec, tspec],
        out_specs=pl.BlockSpec((tm, tn), lambda j, i: (i, j)),
        compiler_params=_params("parallel", "parallel"),
        name="attn_qk_proj",
    )(hn, w_qk, *tables)


def _attn_vt_kernel(w_ref, a_ref, o_ref):
    o_ref[...] = _dot(w_ref[...], a_ref[...], NT).astype(o_ref.dtype)


def attn_v_proj_t(hn, w_t, batch, seq, blk, *, tn=1024):
    m, k = hn.shape
    n = w_t.shape[0]
    tn = min(tn, n)
    nk = seq // blk
    return pl.pallas_call(
        _attn_vt_kernel,
        out_shape=jax.ShapeDtypeStruct((batch, nk, n, blk), BF16),
        grid=(n // tn, m // blk),
        in_specs=[pl.BlockSpec((tn, k), lambda j, i: (j, 0)),
                  pl.BlockSpec((blk, k), lambda j, i: (i, 0))],
        out_specs=pl.BlockSpec((None, None, tn, blk), lambda j, i: (i // nk, i % nk, j, 0)),
        compiler_params=_params("parallel", "parallel"),
        name="attn_v_proj_t",
    )(w_t, hn)


def _diff_attn_kernel(lam_ref, q_ref, k_ref, vt_ref, z_ref, g_ref, o_ref, m_ref, acc_ref, s_ref,
                      *, bq, bk, lam_init):
    i = pl.program_id(2)
    lam = lam_ref[...]
    lam_full = (jnp.exp(jnp.sum(lam[0:1] * lam[1:2], axis=-1, keepdims=True))
                - jnp.exp(jnp.sum(lam[2:3] * lam[3:4], axis=-1, keepdims=True)) + lam_init)
    m_ref[...] = jnp.full(m_ref.shape, -jnp.inf, F32)
    acc_ref[...] = jnp.zeros(acc_ref.shape, F32)
    q = q_ref[...]
    d = A_HEAD_DIM
    w = A_V_DIM
    ones = jnp.ones((ONES_ROWS, bk), BF16)

    def scores(j, slot):
        start = pl.multiple_of(j * bk, bk)
        kb = k_ref[pl.ds(start, bk), :]
        for c in range(2):
            s_ref[slot, c] = _dot(kb[:, c * d:(c + 1) * d], q[:, c * d:(c + 1) * d], NT)

    def absorb(j, slot, masked):
        vt = jnp.concatenate([vt_ref[j], ones], axis=0)
        for c in range(2):
            s = s_ref[slot, c]
            if masked:
                kv_pos = j * bk + _iota(s.shape, 0)
                q_pos = i * bq + _iota(s.shape, 1)
                s = jnp.where(kv_pos <= q_pos, s, -jnp.inf)
            m_prev = m_ref[c]
            m_new = jnp.maximum(m_prev, jnp.max(s, axis=0, keepdims=True))
            alpha = jnp.exp(m_prev - m_new)
            p = jnp.exp(s - m_new)
            acc_ref[c] = alpha * acc_ref[c] + _dot(vt, p.astype(BF16))
            m_ref[c] = m_new

    scores(0, 0)

    def body(jj, carry):
        scores(2 * jj + 1, 1)
        absorb(2 * jj, 0, False)
        scores(2 * jj + 2, 0)
        absorb(2 * jj + 1, 1, False)
        return carry

    lax.fori_loop(0, i, body, 0)
    scores(2 * i + 1, 1)
    absorb(2 * i, 0, True)
    absorb(2 * i + 1, 1, True)

    o = (acc_ref[0, :w, :] / acc_ref[0, w:w + 1, :]
         - lam_full * (acc_ref[1, :w, :] / acc_ref[1, w:w + 1, :]))
    o = o * lax.rsqrt(jnp.mean(o * o, axis=0, keepdims=True) + SUBLN_EPS) * g_ref[...]
    o = (o * (1.0 - lam_init)).T
    o_ref[...] = (o * _silu(z_ref[...].astype(F32))).astype(o_ref.dtype)


def diff_attention_core(qk, z, vt, lam, subln_g, batch, seq, heads, lam_init):
    m = qk.shape[0]
    bk = vt.shape[-1]
    bq = 2 * bk
    nq = seq // bq
    nk = seq // bk
    w = A_V_DIM
    return pl.pallas_call(
        functools.partial(_diff_attn_kernel, bq=bq, bk=bk, lam_init=lam_init),
        out_shape=jax.ShapeDtypeStruct((m, heads * w), BF16),
        grid=(batch, heads, nq),
        in_specs=[pl.BlockSpec((4, A_HEAD_DIM), lambda b, h, i: (0, 0)),
                  pl.BlockSpec((bq, w), lambda b, h, i: (b * nq + i, h)),
                  pl.BlockSpec((seq, w), lambda b, h, i: (b, heads + h)),
                  pl.BlockSpec((None, nk, w, bk), lambda b, h, i: (b, 0, h, 0)),
                  pl.BlockSpec((bq, w), lambda b, h, i: (b * nq + i, h)),
                  pl.BlockSpec((w, 1), lambda b, h, i: (0, 0))],
        out_specs=pl.BlockSpec((bq, w), lambda b, h, i: (b * nq + i, h)),
        scratch_shapes=[pltpu.VMEM((2, 1, bq), F32), pltpu.VMEM((2, w + ONES_ROWS, bq), F32),
                        pltpu.VMEM((2, 2, bk, bq), F32)],
        compiler_params=_params("parallel", "parallel", "parallel"),
        name="diff_attention",
    )(lam, qk, qk, vt, z, subln_g.reshape(w, 1))


def diff_attention_layer(x, hn, tables, w_in, lam, subln_g, w_out, batch, seq, lam_init):
    d = x.shape[1]
    heads = d // A_V_DIM
    qk_w = heads * 2 * A_HEAD_DIM
    v_w = heads * A_V_DIM
    blk = min(ATTN_BLOCK, seq // 2)
    w_vt = w_in[:, 2 * qk_w:2 * qk_w + v_w].T.astype(BF16)
    qk = attn_qk_proj(hn, w_in[:, :2 * qk_w].astype(BF16), tables)
    z = matmul(hn, w_in[:, 2 * qk_w + v_w:].astype(BF16), BF16, name="attn_gate_proj")
    vt = attn_v_proj_t(hn, w_vt, batch, seq, blk)
    o = diff_attention_core(qk, z, vt, lam, subln_g, batch, seq, heads, lam_init)
    return matmul_residual(o, w_out.astype(BF16), x, name="attn_out_proj")


def _gdn_conv_kernel(a_ref, w_ref, cw_ref, o_ref, tail_ref, *, rows_per_seq):
    i = pl.program_id(1)
    acc = _dot(a_ref[...], w_ref[...])
    tm = acc.shape[0]

    @pl.when((i * tm) % rows_per_seq == 0)
    def _():
        tail_ref[...] = jnp.zeros(tail_ref.shape, F32)

    tail = tail_ref[...]
    sub = _iota(tail.shape, 0)

    def shifted(s):
        xs = pltpu.roll(acc, s, 0)
        head = jnp.where(sub < s, pltpu.roll(tail, s, 0), xs[:8])
        return jnp.concatenate([head, xs[8:]], axis=0)

    cw = cw_ref[...]
    last = C_CONV_WIDTH - 1
    y = shifted(last) * cw[0:1]
    for t in range(1, last):
        y = y + shifted(last - t) * cw[t:t + 1]
    y = y + acc * cw[last:last + 1]
    tail_ref[...] = acc[tm - 8:]
    o_ref[...] = _silu(y).astype(o_ref.dtype)


def gdn_conv_proj(hn, w, conv_w, seq, *, tm=1024, tn=1024):
    m, k = hn.shape
    n = w.shape[1]
    tm, tn = min(tm, seq), min(tn, n)
    return pl.pallas_call(
        functools.partial(_gdn_conv_kernel, rows_per_seq=seq),
        out_shape=jax.ShapeDtypeStruct((m, n), BF16),
        grid=(n // tn, m // tm),
        in_specs=[pl.BlockSpec((tm, k), lambda j, i: (i, 0)),
                  pl.BlockSpec((k, tn), lambda j, i: (0, j)),
                  pl.BlockSpec((C_CONV_WIDTH, tn), lambda j, i: (0, j))],
        out_specs=pl.BlockSpec((tm, tn), lambda j, i: (i, j)),
        scratch_shapes=[pltpu.VMEM((8, tn), F32)],
        compiler_params=_params("arbitrary", "arbitrary"),
        name="gdn_conv_proj",
    )(hn, w, conv_w)


def _pair_masks():
    n = 2 * CHUNK
    r, c = _iota((n, n), 0), _iota((n, n), 1)
    same = (r // CHUNK) == (c // CHUNK)
    return same & (r >= c), same & (r > c), r == c


def _dot_x2l(a, b_exact, dims=NN):
    h, l = _split2(a)
    return _dot(h, b_exact, dims) + _dot(l, b_exact, dims)


def _dot_x2r(a_exact, b, dims=NN):
    h, l = _split2(b)
    return _dot(a_exact, h, dims) + _dot(a_exact, l, dims)


def _inverse_stages(p_ref, t_ref, count, nilpotency):
    for _ in range(int(math.log2(nilpotency)) - 1):
        for i in range(count):
            p_ref[i] = _dot(p_ref[i], p_ref[i]).astype(BF16)
        for i in range(count):
            t = t_ref[i]
            t_ref[i] = t + _dot(t.astype(BF16), p_ref[i])


def _gdn_gates_kernel(a_ref, w_ref, alog_ref, dtb_ref, beta_ref, gc_ref, gct_ref):
    acc = _dot(a_ref[...], w_ref[...])
    beta_ref[...] = jax.nn.sigmoid(acc[:, :LANES])
    g = -jnp.exp(alog_ref[...]) * _softplus(acc[:, LANES:] + dtb_ref[...])
    tri, _, diag = _pair_masks()
    tri_b, eye_b = tri.astype(BF16), diag.astype(BF16)
    n2 = 2 * CHUNK
    for n in range(g.shape[0] // n2):
        rows = slice(n * n2, (n + 1) * n2)
        gc = _dot_xr(tri_b, g[rows])
        gc_ref[rows, :] = gc
        gct_ref[:, rows] = _dot_xr(eye_b, gc, NT)


def gdn_gates(hn, w_ba, alog_row, dtb_row, *, tm=512):
    m, k = hn.shape
    tm = min(tm, m)
    row = pl.BlockSpec((1, LANES), lambda i: (0, 0))
    out = pl.BlockSpec((tm, LANES), lambda i: (i, 0))
    return pl.pallas_call(
        _gdn_gates_kernel,
        out_shape=(jax.ShapeDtypeStruct((m, LANES), F32), jax.ShapeDtypeStruct((m, LANES), F32),
                   jax.ShapeDtypeStruct((LANES, m), F32)),
        grid=(m // tm,),
        in_specs=[pl.BlockSpec((tm, k), lambda i: (i, 0)), pl.BlockSpec((k, 2 * LANES), lambda i: (0, 0)), row, row],
        out_specs=(out, out, pl.BlockSpec((LANES, tm), lambda i: (0, i))),
        compiler_params=_params("parallel"),
        name="gdn_gates",
    )(hn, w_ba, alog_row, dtb_row)


def _gdn_prep_kernel(q_ref, k_ref, v_ref, beta_ref, gc_ref, gct_ref,
                     w_ref, qg_ref, u_ref, att_ref, kdt_ref, dec_ref,
                     g_s, dec_s, kn_s, kb_s, rhs_s, p_s, t_s, qn_s, kdec_s):
    kh = pl.program_id(1)
    c = CHUNK
    n2 = 2 * c
    dk = C_HEAD_DIM
    nv = w_ref.shape[0]
    npairs = q_ref.shape[0] // n2
    tri, strict, diag = _pair_masks()
    eye_f = diag.astype(F32)
    pairs = [slice(n * n2, (n + 1) * n2) for n in range(npairs)]
    lane = _iota((n2, LANES), 1)

    for n, rows in enumerate(pairs):
        for e in range(nv):
            h = kh * nv + e
            gc = jnp.broadcast_to(jnp.sum(jnp.where(lane == h, gc_ref[rows, :], 0.0), axis=-1, keepdims=True),
                                  (n2, dk))
            grp = gct_ref[pl.ds(pl.multiple_of((h // 8) * 8, 8), 8), rows]
            own_row = jnp.sum(jnp.where(_iota(grp.shape, 0) == h % 8, grp, 0.0), axis=0, keepdims=True)
            gc_row = jnp.broadcast_to(own_row, (n2, n2))
            g_s[n * nv + e] = gc
            dec_s[n * nv + e] = jnp.where(tri, jnp.exp(jnp.where(tri, gc - gc_row, 0.0)), 0.0)
    slab = 32
    lane_slab = _iota((slab, LANES), 1)
    for n, rows in enumerate(pairs):
        for r0 in range(0, n2, slab):
            rs = slice(r0, r0 + slab)
            gs = slice(n * n2 + r0, n * n2 + r0 + slab)
            last = c - 1 if r0 < c else n2 - 1
            qf = q_ref[gs, :].astype(F32)
            kf = k_ref[gs, :].astype(F32)
            qn = qf * lax.rsqrt(jnp.sum(qf * qf, axis=-1, keepdims=True) + 1e-6) * (dk ** -0.5)
            kn = kf * lax.rsqrt(jnp.sum(kf * kf, axis=-1, keepdims=True) + 1e-6)
            kn_s[n, rs, :] = kn.astype(BF16)
            qn_s[n, rs, :] = qn.astype(BF16)
            for e in range(nv):
                i = n * nv + e
                gc = g_s[i, rs, :]
                vf = v_ref[gs, e * dk:(e + 1) * dk].astype(F32)
                beta = jnp.sum(jnp.where(lane_slab == kh * nv + e, beta_ref[gs, :], 0.0), axis=-1, keepdims=True)
                egc = jnp.exp(gc)
                kb = kn * beta
                kb_s[i, rs, :] = kb.astype(BF16)
                rhs_s[i, rs, :dk] = (vf * beta).astype(BF16)
                rhs_s[i, rs, dk:] = (kb * egc).astype(BF16)
                kdec_s[i, rs, :] = kn * jnp.exp(g_s[i, last:last + 1, :] - gc)
                qg_ref[e, gs, :] = (qn * egc).astype(BF16)
        for e in range(nv):
            i = n * nv + e
            dec_ref[e, 2 * n:2 * n + 1, :] = jnp.exp(g_s[i, c - 1:c, :])
            dec_ref[e, 2 * n + 1:2 * n + 2, :] = jnp.exp(g_s[i, n2 - 1:n2, :])
    for n, rows in enumerate(pairs):
        raw = _dot(qn_s[n], kn_s[n], NT)
        for e in range(nv):
            att_ref[e, rows, :] = (raw * dec_s[n * nv + e]).astype(BF16)
            kdt_ref[e, :, rows] = kdec_s[n * nv + e].T.astype(BF16)
    for n, rows in enumerate(pairs):
        for e in range(nv):
            i = n * nv + e
            neg_a = jnp.where(strict, -(_dot(kb_s[i], kn_s[n], NT) * dec_s[i]), 0.0)
            p_s[i] = neg_a.astype(BF16)
            t_s[i] = eye_f + neg_a
    _inverse_stages(p_s, t_s, npairs * nv, c)
    for n, rows in enumerate(pairs):
        for e in range(nv):
            sol = _dot(t_s[n * nv + e].astype(BF16), rhs_s[n * nv + e])
            u_ref[e, rows, :] = sol[:, :dk]
            w_ref[e, rows, :] = sol[:, dk:].astype(BF16)


def _gdn_scan_kernel(w_ref, qg_ref, u_ref, att_ref, kdt_ref, dec_ref, z_ref, g_ref, o_ref,
                     s_ref, ms_s, vp_s, *, group):
    c = CHUNK
    dk = C_HEAD_DIM
    zeros = jnp.zeros((c, dk), BF16)

    @pl.when(pl.program_id(2) == 0)
    def _():
        s_ref[...] = jnp.zeros(s_ref.shape, F32)

    for n in range(w_ref.shape[1] // c):
        rows = slice(n * c, (n + 1) * c)
        pair = slice((n // 2) * 2 * c, (n // 2 + 1) * 2 * c)
        for gi in range(group):
            lhs = jnp.concatenate([w_ref[gi, rows, :], qg_ref[gi, rows, :]], axis=0)
            ms_s[gi] = _dot(lhs, s_ref[gi].astype(BF16))
        for gi in range(group):
            v_new = (u_ref[gi, rows, :] - ms_s[gi, :c, :]).astype(BF16)
            vp_s[gi] = jnp.concatenate([v_new, zeros] if n % 2 == 0 else [zeros, v_new], axis=0)
        for gi in range(group):
            s_ref[gi] = s_ref[gi] * dec_ref[gi, n:n + 1, :] + _dot(kdt_ref[gi, :, pair], vp_s[gi])
        for gi in range(group):
            o = ms_s[gi, c:, :] + _dot(att_ref[gi, rows, :], vp_s[gi])
            o = o * lax.rsqrt(jnp.mean(o * o, axis=-1, keepdims=True) + NORM_EPS) * g_ref[...]
            z = z_ref[rows, gi * dk:(gi + 1) * dk].astype(F32)
            o_ref[rows, gi * dk:(gi + 1) * dk] = (o * _silu(z)).astype(o_ref.dtype)


def gated_deltanet_layer(x, hn, w_in, conv_w, a_log, dt_bias, norm_g, w_out, batch, seq):
    m, d = x.shape
    dk = C_HEAD_DIM
    k_heads = d // dk
    v_heads = 2 * k_heads
    conv_ch = 2 * k_heads * dk + v_heads * dk
    main_w = conv_ch + v_heads * dk
    qkv = gdn_conv_proj(hn, w_in[:, :conv_ch].astype(BF16), conv_w, seq)
    z = matmul(hn, w_in[:, conv_ch:main_w].astype(BF16), BF16, name="gdn_gate_proj")
    pad = jnp.zeros((d, LANES - v_heads), F32)
    w_ba = jnp.concatenate([w_in[:, main_w:main_w + v_heads], pad, w_in[:, main_w + v_heads:], pad], axis=1)
    row_pad = jnp.zeros((LANES - v_heads,), F32)
    alog_row = jnp.concatenate([a_log, row_pad]).reshape(1, LANES)
    dtb_row = jnp.concatenate([dt_bias, row_pad]).reshape(1, LANES)
    beta_all, gc_all, gc_t = gdn_gates(hn, w_ba.astype(BF16), alog_row, dtb_row)

    rows = min(PREP_ROWS, seq)
    nr = seq // rows
    ncr = rows // CHUNK
    hv = v_heads
    nv = v_heads // k_heads
    nchain = nv * rows // (2 * CHUNK)
    npair = rows // (2 * CHUNK)
    bh_t = lambda dt, last: jax.ShapeDtypeStruct((batch, hv, seq, last), dt)
    blk4 = lambda last: pl.BlockSpec((None, nv, rows, last), lambda b, h, i: (b, h, i, 0))
    qoff, koff, voff = 0, k_heads, 2 * k_heads // nv
    w_, qg_, u_, att_, kdt_, dec_ = pl.pallas_call(
        _gdn_prep_kernel,
        out_shape=(bh_t(BF16, dk), bh_t(BF16, dk), bh_t(F32, dk), bh_t(BF16, 2 * CHUNK),
                   jax.ShapeDtypeStruct((batch, hv, dk, seq), BF16),
                   jax.ShapeDtypeStruct((batch, hv, seq // CHUNK, dk), F32)),
        grid=(batch, k_heads, nr),
        in_specs=[pl.BlockSpec((rows, dk), lambda b, h, i: (b * nr + i, qoff + h)),
                  pl.BlockSpec((rows, dk), lambda b, h, i: (b * nr + i, koff + h)),
                  pl.BlockSpec((rows, nv * dk), lambda b, h, i: (b * nr + i, voff + h)),
                  pl.BlockSpec((rows, LANES), lambda b, h, i: (b * nr + i, 0)),
                  pl.BlockSpec((rows, LANES), lambda b, h, i: (b * nr + i, 0)),
                  pl.BlockSpec((LANES, rows), lambda b, h, i: (0, b * nr + i))],
        out_specs=(blk4(dk), blk4(dk), blk4(dk), blk4(2 * CHUNK),
                   pl.BlockSpec((None, nv, dk, rows), lambda b, h, i: (b, h, 0, i)),
                   pl.BlockSpec((None, nv, ncr, dk), lambda b, h, i: (b, h, i, 0))),
        scratch_shapes=[pltpu.VMEM((nchain, dk, dk), F32), pltpu.VMEM((nchain, dk, dk), F32),
                        pltpu.VMEM((npair, dk, dk), BF16), pltpu.VMEM((nchain, dk, dk), BF16),
                        pltpu.VMEM((nchain, dk, 2 * dk), BF16), pltpu.VMEM((nchain, dk, dk), BF16),
                        pltpu.VMEM((nchain, dk, dk), F32), pltpu.VMEM((npair, dk, dk), BF16),
                        pltpu.VMEM((nchain, dk, dk), F32)],
        compiler_params=_params("parallel", "parallel", "parallel"),
        name="gdn_prep",
    )(qkv, qkv, qkv, beta_all, gc_all, gc_t)

    rows = min(SCAN_ROWS, seq)
    nr = seq // rows
    ncr = rows // CHUNK
    group = 8
    gblk = lambda last: pl.BlockSpec((None, group, rows, last), lambda b, h, i: (b, h, i, 0))
    o = pl.pallas_call(
        functools.partial(_gdn_scan_kernel, group=group),
        out_shape=jax.ShapeDtypeStruct((m, hv * dk), BF16),
        grid=(batch, hv // group, nr),
        in_specs=[gblk(dk), gblk(dk), gblk(dk), gblk(2 * CHUNK),
                  pl.BlockSpec((None, group, dk, rows), lambda b, h, i: (b, h, 0, i)),
                  pl.BlockSpec((None, group, ncr, dk), lambda b, h, i: (b, h, i, 0)),
                  pl.BlockSpec((rows, group * dk), lambda b, h, i: (b * nr + i, h)),
                  pl.BlockSpec((1, dk), lambda b, h, i: (0, 0))],
        out_specs=pl.BlockSpec((rows, group * dk), lambda b, h, i: (b * nr + i, h)),
        scratch_shapes=[pltpu.VMEM((group, dk, dk), F32), pltpu.VMEM((group, 2 * CHUNK, dk), F32),
                        pltpu.VMEM((group, 2 * CHUNK, dk), BF16)],
        compiler_params=_params("parallel", "parallel", "arbitrary"),
        name="gdn_scan",
    )(w_, qg_, u_, att_, kdt_, dec_, z, norm_g.reshape(1, dk))
    return matmul_residual(o, w_out.astype(BF16), x, tm=512, name="gdn_out_proj")


def _rwkv_mix_kernel(x_ref, g_ref, mu_ref, o_ref, tail_ref, *, rows_per_seq):
    i = pl.program_id(0)
    x = x_ref[...]
    tm = x.shape[0]
    hn = x * lax.rsqrt(jnp.mean(x * x, axis=-1, keepdims=True) + NORM_EPS) * g_ref[...]

    @pl.when((i * tm) % rows_per_seq == 0)
    def _():
        tail_ref[...] = jnp.zeros(tail_ref.shape, F32)

    prev = jnp.where(_iota(hn.shape, 0) == 0, tail_ref[7:8, :], pltpu.roll(hn, 1, 0))
    tail_ref[...] = hn[tm - 8:]
    xx = prev - hn
    for c in range(o_ref.shape[0]):
        o_ref[c] = (hn + xx * mu_ref[c:c + 1, :]).astype(o_ref.dtype)


def rwkv_token_mix(x, norm_g, mu, seq, *, tm=256):
    m, d = x.shape
    tm = min(tm, seq)
    nmix = mu.shape[0]
    return pl.pallas_call(
        functools.partial(_rwkv_mix_kernel, rows_per_seq=seq),
        out_shape=jax.ShapeDtypeStruct((nmix, m, d), BF16),
        grid=(m // tm,),
        in_specs=[pl.BlockSpec((tm, d), lambda i: (i, 0)),
                  pl.BlockSpec((1, d), lambda i: (0, 0)),
                  pl.BlockSpec((nmix, d), lambda i: (0, 0))],
        out_specs=pl.BlockSpec((nmix, tm, d), lambda i: (0, i, 0)),
        scratch_shapes=[pltpu.VMEM((8, d), F32)],
        compiler_params=_params("arbitrary"),
        name="rwkv_token_mix",
    )(x, norm_g.reshape(1, d), mu)


def _grouped_mm_kernel(a_ref, w_ref, o_ref):
    o_ref[...] = _dot(a_ref[...], w_ref[...]).astype(o_ref.dtype)


def grouped_matmul(a, w, out_dtype, *, tm=1024, tn=1024):
    g, k, n = w.shape
    m = a.shape[1]
    tm, tn = min(tm, m), min(tn, n)
    return pl.pallas_call(
        _grouped_mm_kernel,
        out_shape=jax.ShapeDtypeStruct((g, m, n), out_dtype),
        grid=(g, n // tn, m // tm),
        in_specs=[pl.BlockSpec((None, tm, k), lambda c, j, i: (c, i, 0)),
                  pl.BlockSpec((None, k, tn), lambda c, j, i: (c, 0, j))],
        out_specs=pl.BlockSpec((None, tm, tn), lambda c, j, i: (c, i, j)),
        compiler_params=_params("parallel", "parallel", "parallel"),
        name="rwkv_rkvg_proj",
    )(a, w)


def _rwkv_lora_kernel(xw_ref, xa_ref, w1_ref, w2_ref, a1_ref, a2_ref, w0_ref, a0_ref, ld_ref, a_ref):
    hw = jnp.tanh(_dot(xw_ref[...], w1_ref[...])).astype(BF16)
    lw = w0_ref[...] + _dot(hw, w2_ref[...])
    log_w = -_softplus(-lw) - 0.5
    ld_ref[...] = -jnp.exp(log_w)
    ha = _dot(xa_ref[...], a1_ref[...]).astype(BF16)
    a_ref[...] = jax.nn.sigmoid(a0_ref[...] + _dot(ha, a2_ref[...]))


def rwkv_lora(xs, w1, w2, a1, a2, w0, a0, *, tm=512):
    _, m, d = xs.shape
    tm = min(tm, m)
    r = w1.shape[1]
    row = pl.BlockSpec((1, d), lambda i: (0, 0))
    shape = jax.ShapeDtypeStruct((m, d), F32)
    return pl.pallas_call(
        _rwkv_lora_kernel,
        out_shape=(shape, shape),
        grid=(m // tm,),
        in_specs=[pl.BlockSpec((None, tm, d), lambda i: (4, i, 0)),
                  pl.BlockSpec((None, tm, d), lambda i: (5, i, 0)),
                  pl.BlockSpec((d, r), lambda i: (0, 0)), pl.BlockSpec((r, d), lambda i: (0, 0)),
                  pl.BlockSpec((d, r), lambda i: (0, 0)), pl.BlockSpec((r, d), lambda i: (0, 0)),
                  row, row],
        out_specs=(pl.BlockSpec((tm, d), lambda i: (i, 0)), pl.BlockSpec((tm, d), lambda i: (i, 0))),
        compiler_params=_params("parallel"),
        name="rwkv_lora",
    )(xs, xs, w1, w2, a1, a2, w0.reshape(1, d), a0.reshape(1, d))


def _head_sum(x, first_head):
    tot = jnp.sum(x, axis=-1, keepdims=True)
    lo = jnp.sum(jnp.where(first_head, x, 0.0), axis=-1, keepdims=True)
    return jnp.where(first_head, lo, tot - lo)


def _rwkv_prep_kernel(r_ref, k_ref, v_ref, ld_ref, a_ref, kk_ref, ka_ref, rk_ref,
                      wt_ref, rt_ref, u_ref, aro_ref, kbt_ref, vb_ref, dec_ref, bonus_ref,
                      kkt_s, kb_s, akk_s, p_s, t_s, drow_s, *, cps):
    c = CHUNK
    n2 = 2 * c
    n_h = B_HEAD_DIM
    npairs = r_ref.shape[0] // n2
    first_head = _iota((1, LANES), 1) < n_h
    tri, strict, diag = _pair_masks()
    tri_b = tri.astype(BF16)
    eye_b = diag.astype(BF16)
    eye_f = diag.astype(F32)
    upper_half = _iota((n2, LANES), 0) < c
    pairs = [slice(n * n2, (n + 1) * n2) for n in range(npairs)]

    for n, rows in enumerate(pairs):
        r, k, v, a = r_ref[rows, :], k_ref[rows, :], v_ref[rows, :], a_ref[rows, :]
        kk_raw = k * kk_ref[...]
        kk = kk_raw * lax.rsqrt(_head_sum(kk_raw * kk_raw, first_head) + 1e-6)
        k2c = k * (1.0 + (a - 1.0) * ka_ref[...])
        bbc = kk * a
        bonus_ref[rows, :] = _head_sum(r * k2c * rk_ref[...], first_head) * v
        vb_ref[rows, :] = v.astype(BF16)
        ld = ld_ref[rows, :]
        cs = _dot_x2r(tri_b, ld)
        cs_last = jnp.where(upper_half, cs[c - 1:c, :], cs[n2 - 1:n2, :])
        w_inv = jnp.exp(-cs)
        w_end = jnp.exp(cs_last - cs)
        rt_ref[rows, :] = (r * jnp.exp(cs)).astype(BF16)
        kkt_s[n] = (kk * jnp.exp(cs - ld)).astype(BF16)
        kb_s[n] = jnp.concatenate([k2c * w_inv, bbc * w_inv], axis=0).astype(BF16)
        kw, bw = (k2c * w_end).astype(BF16), (-(bbc * w_end)).astype(BF16)
        for half in range(2):
            hs = slice(half * c, (half + 1) * c)
            kb_end = jnp.concatenate([kw[hs], bw[hs]], axis=0)
            cols = slice((2 * n + half) * n2, (2 * n + half + 1) * n2)
            kbt_ref[:, cols] = _dot(eye_b, kb_end, NT).astype(BF16)
        blk, off = divmod(2 * n, cps)
        drow_s[blk, off:off + 1, :] = jnp.exp(cs[c - 1:c, :])
        drow_s[blk, off + 1:off + 2, :] = jnp.exp(cs[n2 - 1:n2, :])
    for blk in range(dec_ref.shape[0] // LANES):
        drow_s[blk, cps:, :] = jnp.zeros((LANES - cps, LANES), F32)
        dec_ref[blk * LANES:(blk + 1) * LANES, :] = _dot_xr(eye_b, drow_s[blk], NT)
    zero_b = jnp.zeros((n2, LANES), BF16)
    strict2 = jnp.concatenate([strict, strict], axis=1)
    tri2 = jnp.concatenate([tri, tri], axis=1)
    sign2 = jnp.where(_iota((n2, 2 * n2), 1) < n2, 1.0, -1.0)
    for n, rows in enumerate(pairs):
        kkt, rt = kkt_s[n], rt_ref[rows, :]
        lhs = jnp.concatenate([jnp.where(first_head, kkt, zero_b), jnp.where(first_head, rt, zero_b),
                               jnp.where(first_head, zero_b, kkt), jnp.where(first_head, zero_b, rt)], axis=0)
        gram = _dot(lhs, kb_s[n], NT)
        for hh in range(2):
            i = 2 * n + hh
            g_kk = jnp.where(strict2, gram[2 * hh * n2:(2 * hh + 1) * n2], 0.0)
            neg_a = -g_kk[:, n2:]
            p_s[i] = neg_a.astype(BF16)
            t_s[i] = eye_f + neg_a
            akk_s[i] = g_kk[:, :n2].astype(BF16)
            g_r = gram[(2 * hh + 1) * n2:(2 * hh + 2) * n2]
            aro_ref[hh, rows, :] = jnp.where(tri2, g_r * sign2, 0.0).astype(BF16)
    _inverse_stages(p_s, t_s, 2 * npairs, c)
    for n, rows in enumerate(pairs):
        av = _dot(jnp.concatenate([akk_s[2 * n], akk_s[2 * n + 1]], axis=0), vb_ref[rows, :])
        p_s[2 * n] = av[:n2].astype(BF16)
        p_s[2 * n + 1] = av[n2:].astype(BF16)
    for n, rows in enumerate(pairs):
        t0, t1 = t_s[2 * n].astype(BF16), t_s[2 * n + 1].astype(BF16)
        wt = _dot(jnp.concatenate([t0, t1], axis=0), kkt_s[n])
        wt_ref[rows, :] = jnp.where(first_head, wt[:n2], wt[n2:]).astype(BF16)
        u_ref[rows, :] = jnp.where(first_head, _dot(t0, p_s[2 * n]), _dot(t1, p_s[2 * n + 1]))


def _rwkv_scan_kernel(wt_ref, rt_ref, u_ref, aro_ref, kbt_ref, vb_ref, dec_ref, bonus_ref, gate_ref,
                      lnw_ref, lnb_ref, o_ref, s_ref, ms_s, xc_s, xp_s, *, group):
    c = CHUNK
    n2 = 2 * c
    n_h = B_HEAD_DIM
    first_head = _iota((1, LANES), 1) < n_h
    same_head = _iota((LANES, LANES), 0) // n_h == _iota((LANES, LANES), 1) // n_h
    zeros = jnp.zeros((c, LANES), BF16)

    def head_mean(x):
        return _head_sum(x, first_head) * (1.0 / n_h)

    @pl.when(pl.program_id(2) == 0)
    def _():
        s_ref[...] = jnp.zeros(s_ref.shape, F32)

    for n in range(u_ref.shape[1] // c):
        rows = slice(n * c, (n + 1) * c)
        for gi in range(group):
            lhs = jnp.concatenate([wt_ref[gi, rows, :], rt_ref[gi, rows, :]], axis=0)
            ms_s[gi] = _dot(lhs, s_ref[gi].astype(BF16))
        for gi in range(group):
            sa = (u_ref[gi, rows, :] + ms_s[gi, :c, :]).astype(BF16)
            vb = vb_ref[gi, rows, :]
            xc_s[gi] = jnp.concatenate([vb, sa], axis=0)
            xp_s[gi] = jnp.concatenate([vb, zeros, sa, zeros] if n % 2 == 0 else [zeros, vb, zeros, sa], axis=0)
        for gi in range(group):
            upd = _dot(kbt_ref[gi, :, n * n2:(n + 1) * n2], xc_s[gi])
            dec_col = jnp.broadcast_to(dec_ref[gi, :, n:n + 1], (LANES, LANES))
            s_ref[gi] = s_ref[gi] * dec_col + jnp.where(same_head, upd, 0.0)
        for gi in range(group):
            both = _dot(jnp.concatenate([aro_ref[gi, 0, rows, :], aro_ref[gi, 1, rows, :]], axis=0), xp_s[gi])
            o = ms_s[gi, c:, :] + jnp.where(first_head, both[:c], both[c:])
            mean = head_mean(o)
            dlt = o - mean
            var = head_mean(dlt * dlt)
            cols = slice(gi * LANES, (gi + 1) * LANES)
            y = dlt * lax.rsqrt(var + B_GN_EPS) * lnw_ref[:, cols] + lnb_ref[:, cols]
            y = y + bonus_ref[rows, cols]
            o_ref[rows, cols] = (y * _silu(gate_ref[rows, cols])).astype(o_ref.dtype)


def rwkv7_layer(x, norm_g, mu, w_rkvg, w0, w_w1, w_w2, a0, w_a1, w_a2, k_k, k_a, r_k, ln_w, ln_b, w_out,
                batch, seq):
    m, d = x.shape
    heads = d // B_HEAD_DIM
    pairs = heads // 2
    order = jnp.array([0, 2, 3, 5, 1, 4])
    xs = rwkv_token_mix(x, norm_g, mu[order], seq)
    rkvg = grouped_matmul(xs, w_rkvg.astype(BF16), F32)
    lora = w_w1.shape[1]
    padc = lambda w: jnp.pad(w, ((0, 0), (0, LORA_PAD - lora))).astype(BF16)
    padr = lambda w: jnp.pad(w, ((0, LORA_PAD - lora), (0, 0))).astype(BF16)
    ld, a = rwkv_lora(xs, padc(w_w1), padr(w_w2), padc(w_a1), padr(w_a2), w0, a0)

    srows = min(SCAN_ROWS, seq)
    rows = min(PREP_ROWS, seq)
    nr = seq // rows
    npair = rows // (2 * CHUNK)
    col = lambda g: pl.BlockSpec((None, rows, LANES), lambda b, p, i, g=g: (g, b * nr + i, p))
    flat = pl.BlockSpec((rows, LANES), lambda b, p, i: (b * nr + i, p))
    prow = pl.BlockSpec((1, LANES), lambda b, p, i: (0, p))
    bp = lambda rws, last, dt: jax.ShapeDtypeStruct((batch, pairs, rws, last), dt)
    pblk = lambda rws, last: pl.BlockSpec((None, None, rws, last), lambda b, p, i: (b, p, i, 0))
    wt_, rt_, u_, aro_, kbt_, vb_, dec_, bonus = pl.pallas_call(
        functools.partial(_rwkv_prep_kernel, cps=srows // CHUNK),
        out_shape=(bp(seq, LANES, BF16), bp(seq, LANES, BF16), bp(seq, LANES, F32),
                   jax.ShapeDtypeStruct((batch, pairs, 2, seq, 4 * CHUNK), BF16),
                   bp(LANES, 2 * seq, BF16), bp(seq, LANES, BF16), bp(seq // srows * LANES, LANES, F32),
                   jax.ShapeDtypeStruct((m, d), F32)),
        grid=(batch, pairs, nr),
        in_specs=[col(0), col(1), col(2), flat, flat, prow, prow, prow],
        out_specs=(pblk(rows, LANES), pblk(rows, LANES), pblk(rows, LANES),
                   pl.BlockSpec((None, None, 2, rows, 4 * CHUNK), lambda b, p, i: (b, p, 0, i, 0)),
                   pl.BlockSpec((None, None, LANES, 2 * rows), lambda b, p, i: (b, p, 0, i)),
                   pblk(rows, LANES), pblk(rows // srows * LANES, LANES), flat),
        scratch_shapes=[pltpu.VMEM((npair, LANES, LANES), BF16), pltpu.VMEM((npair, 2 * LANES, LANES), BF16),
                        pltpu.VMEM((2 * npair, LANES, LANES), BF16), pltpu.VMEM((2 * npair, LANES, LANES), BF16),
                        pltpu.VMEM((2 * npair, LANES, LANES), F32),
                        pltpu.VMEM((rows // srows, LANES, LANES), F32)],
        compiler_params=_params("parallel", "parallel", "parallel"),
        name="rwkv_prep",
    )(rkvg, rkvg, rkvg, ld, a, k_k.reshape(1, d), k_a.reshape(1, d), r_k.reshape(1, d))

    rows = srows
    nr = seq // rows
    group = 8
    gw = group * LANES
    gblk = lambda rws, last: pl.BlockSpec((None, group, rws, last), lambda b, p, i: (b, p, i, 0))
    gflat = pl.BlockSpec((rows, gw), lambda b, p, i: (b * nr + i, p))
    grow = pl.BlockSpec((1, gw), lambda b, p, i: (0, p))
    o = pl.pallas_call(
        functools.partial(_rwkv_scan_kernel, group=group),
        out_shape=jax.ShapeDtypeStruct((m, d), BF16),
        grid=(batch, pairs // group, nr),
        in_specs=[gblk(rows, LANES), gblk(rows, LANES), gblk(rows, LANES),
                  pl.BlockSpec((None, group, 2, rows, 4 * CHUNK), lambda b, p, i: (b, p, 0, i, 0)),
                  pl.BlockSpec((None, group, LANES, 2 * rows), lambda b, p, i: (b, p, 0, i)),
                  gblk(rows, LANES), gblk(LANES, LANES), gflat,
                  pl.BlockSpec((None, rows, gw), lambda b, p, i: (3, b * nr + i, p)),
                  grow, grow],
        out_specs=gflat,
        scratch_shapes=[pltpu.VMEM((group, LANES, LANES), F32), pltpu.VMEM((group, 2 * CHUNK, LANES), F32),
                        pltpu.VMEM((group, 2 * CHUNK, LANES), BF16), pltpu.VMEM((group, 4 * CHUNK, LANES), BF16)],
        compiler_params=_params("parallel", "parallel", "arbitrary"),
        name="rwkv_scan",
    )(wt_, rt_, u_, aro_, kbt_, vb_, dec_, bonus, rkvg, ln_w.reshape(1, d), ln_b.reshape(1, d))
    return matmul_residual(o, w_out.astype(BF16), x, name="rwkv_out_proj")


def kernel(x, p, positions, norm_g, pe_norm_g, pe_w_gate, pe_w_proj, final_norm_g, a_w_in, a_lam, a_subln_g, a_w_out, b_mu, b_w_rkvg, b_w0, b_w_w1, b_w_w2, b_a0, b_w_a1, b_w_a2, b_k_k, b_k_a, b_r_k, b_ln_w, b_ln_b, b_w_out, c_w_in, c_conv_w, c_A_log, c_dt_bias, c_norm_g, c_w_out):
    batch, seq, d = x.shape
    depth = p.shape[0]
    m = batch * seq
    xf = x.reshape(m, d)
    tables = rope_tables(positions)
    for i in range(depth):
        kind = i % N_MIXERS
        j = i // N_MIXERS
        if kind == 0:
            lam_init = 0.8 - 0.6 * math.exp(-0.3 * i)
            hn = rmsnorm(xf, norm_g[i], BF16)
            xf = diff_attention_layer(xf, hn, tables, a_w_in[j], a_lam[j], a_subln_g[j], a_w_out[j],
                                      batch, seq, lam_init)
        elif kind == 1:
            xf = rwkv7_layer(xf, norm_g[i], b_mu[j], b_w_rkvg[j], b_w0[j], b_w_w1[j], b_w_w2[j], b_a0[j],
                             b_w_a1[j], b_w_a2[j], b_k_k[j], b_k_a[j], b_r_k[j], b_ln_w[j], b_ln_b[j],
                             b_w_out[j], batch, seq)
        else:
            hn = rmsnorm(xf, norm_g[i], BF16)
            xf = gated_deltanet_layer(xf, hn, c_w_in[j], c_conv_w[j], c_A_log[j], c_dt_bias[j], c_norm_g[j],
                                      c_w_out[j], batch, seq)
        hn2 = rmsnorm(xf, pe_norm_g[i], BF16)
        xf = per_layer_embedding(xf, hn2, pe_w_gate[i].astype(BF16), p[i].reshape(m, -1),
                                 pe_w_proj[i].astype(BF16))
    return rmsnorm(xf, final_norm_g, F32).reshape(batch, seq, d)
```

```python
import functools
import math

import jax
import jax.numpy as jnp
from jax import lax
from jax.experimental import pallas as pl
from jax.experimental.pallas import tpu as pltpu

F32 = jnp.float32
BF16 = jnp.bfloat16

N_MIXERS = 3
NORM_EPS = 1e-6
LANES = 128
VMEM_LIMIT = 48 * 1024 * 1024

A_HEAD_DIM = 128
A_V_DIM = 2 * A_HEAD_DIM
ROT_DIM = A_HEAD_DIM // 4
ROPE_THETA = 500000.0
SUBLN_EPS = 1e-5
ATTN_BLOCK = 256
ATTN_Q_BLOCKS = 2
ONES_ROWS = 16

B_HEAD_DIM = 64
B_GN_EPS = 64e-5
LORA_PAD = 128

C_HEAD_DIM = 128
C_CONV_WIDTH = 4
CHUNK = 64
SCAN_ROWS = 512
PREP_ROWS = 1024

NN = (((1,), (0,)), ((), ()))
NT = (((1,), (1,)), ((), ()))


def _dot(a, b, dims=NN):
    return lax.dot_general(a, b, dims, preferred_element_type=F32)


def _split2(x):
    hi = x.astype(BF16)
    lo = (x - hi.astype(F32)).astype(BF16)
    return hi, lo


def _split3(x):
    hi = x.astype(BF16)
    r = x - hi.astype(F32)
    mid = r.astype(BF16)
    lo = (r - mid.astype(F32)).astype(BF16)
    return hi, mid, lo


def _dot3(a, b, dims=NN):
    ah, al = _split2(a)
    bh, bl = _split2(b)
    return _dot(ah, bh, dims) + (_dot(ah, bl, dims) + _dot(al, bh, dims))


def _dot_xl(a, b_exact, dims=NN):
    h, m, l = _split3(a)
    return _dot(h, b_exact, dims) + (_dot(m, b_exact, dims) + _dot(l, b_exact, dims))


def _dot_xr(a_exact, b, dims=NN):
    h, m, l = _split3(b)
    return _dot(a_exact, h, dims) + (_dot(a_exact, m, dims) + _dot(a_exact, l, dims))


def _iota(shape, dim):
    return lax.broadcasted_iota(jnp.int32, shape, dim)


def _silu(x):
    return x * jax.nn.sigmoid(x)


def _softplus(x):
    return jnp.maximum(x, 0.0) + jnp.log(1.0 + jnp.exp(-jnp.abs(x)))


def _params(*sem):
    return pltpu.CompilerParams(dimension_semantics=sem, vmem_limit_bytes=VMEM_LIMIT)


def _inv_unit_lower(a, nilpotency):
    n = a.shape[0]
    eye = (_iota((n, n), 0) == _iota((n, n), 1)).astype(F32)
    p = -a
    t = eye + p
    for _ in range(int(math.log2(nilpotency)) - 1):
        p = _dot3(p, p)
        t = t + _dot3(t, p)
    return t


def _rmsnorm_kernel(x_ref, g_ref, o_ref, *, eps):
    x = x_ref[...]
    y = x * lax.rsqrt(jnp.mean(x * x, axis=-1, keepdims=True) + eps)
    o_ref[...] = (y * g_ref[...]).astype(o_ref.dtype)


def rmsnorm(x, g, out_dtype, *, eps=NORM_EPS, tm=512):
    m, d = x.shape
    tm = min(tm, m)
    return pl.pallas_call(
        functools.partial(_rmsnorm_kernel, eps=eps),
        out_shape=jax.ShapeDtypeStruct((m, d), out_dtype),
        grid=(m // tm,),
        in_specs=[pl.BlockSpec((tm, d), lambda i: (i, 0)), pl.BlockSpec((1, d), lambda i: (0, 0))],
        out_specs=pl.BlockSpec((tm, d), lambda i: (i, 0)),
        compiler_params=_params("parallel"),
        name="rmsnorm",
    )(x, g.reshape(1, d))


def _mm_kernel(a_ref, w_ref, *rest, epilogue):
    o_ref = rest[-1]
    acc = _dot(a_ref[...], w_ref[...])
    if epilogue is not None:
        acc = epilogue(acc, *rest[:-1])
    o_ref[...] = acc.astype(o_ref.dtype)


def matmul(a, w, out_dtype, *, tm=1024, tn=1024, extra=(), extra_specs=(), epilogue=None, name="matmul"):
    m, k = a.shape
    n = w.shape[1]
    tm, tn = min(tm, m), min(tn, n)
    return pl.pallas_call(
        functools.partial(_mm_kernel, epilogue=epilogue),
        out_shape=jax.ShapeDtypeStruct((m, n), out_dtype),
        grid=(n // tn, m // tm),
        in_specs=[pl.BlockSpec((tm, k), lambda j, i: (i, 0)),
                  pl.BlockSpec((k, tn), lambda j, i: (0, j))] + list(extra_specs),
        out_specs=pl.BlockSpec((tm, tn), lambda j, i: (i, j)),
        compiler_params=_params("parallel", "parallel"),
        name=name,
    )(a, w, *extra)


def matmul_residual(a, w, res, *, tm=1024, tn=1024, name="matmul_residual"):
    tm, tn = min(tm, a.shape[0]), min(tn, w.shape[1])
    return matmul(a, w, F32, tm=tm, tn=tn, extra=(res,),
                  extra_specs=(pl.BlockSpec((tm, tn), lambda j, i: (i, j)),),
                  epilogue=lambda acc, r_ref: r_ref[...] + acc, name=name)


def _ple_kernel(h_ref, wg_ref, p_ref, wp_ref, x_ref, o_ref):
    gate = jax.nn.sigmoid(_dot(h_ref[...], wg_ref[...]))
    proj = _dot(p_ref[...].astype(BF16), wp_ref[...])
    o_ref[...] = x_ref[...] + gate * proj


def per_layer_embedding(x, hn, w_gate, p, w_proj, *, tm=1024, tn=1024):
    m, d = x.shape
    pd = p.shape[1]
    tm, tn = min(tm, m), min(tn, d)
    return pl.pallas_call(
        _ple_kernel,
        out_shape=jax.ShapeDtypeStruct((m, d), F32),
        grid=(d // tn, m // tm),
        in_specs=[pl.BlockSpec((tm, d), lambda j, i: (i, 0)),
                  pl.BlockSpec((d, tn), lambda j, i: (0, j)),
                  pl.BlockSpec((tm, pd), lambda j, i: (i, 0)),
                  pl.BlockSpec((pd, tn), lambda j, i: (0, j)),
                  pl.BlockSpec((tm, tn), lambda j, i: (i, j))],
        out_specs=pl.BlockSpec((tm, tn), lambda j, i: (i, j)),
        compiler_params=_params("parallel", "parallel"),
        name="per_layer_embedding",
    )(hn, w_gate, p, w_proj, x)


def _rope_table_kernel(pos_ref, freq_ref, cos_ref, sin_lo_ref, sin_hi_ref):
    half = ROT_DIM // 2
    ang = pos_ref[...].astype(F32) * freq_ref[...]
    lane = _iota(ang.shape, 1)
    c, s = jnp.cos(ang), jnp.sin(ang)
    cos_ref[...] = jnp.where(lane < ROT_DIM, c, 1.0)
    sin_lo_ref[...] = jnp.where(lane < half, -s, 0.0)
    sin_hi_ref[...] = jnp.where((lane >= half) & (lane < ROT_DIM), s, 0.0)


def rope_tables(positions, *, tm=1024):
    m = positions.size
    tm = min(tm, m)
    inv_freq = ROPE_THETA ** (-jnp.arange(0, ROT_DIM, 2, dtype=F32) / ROT_DIM)
    freq_row = jnp.concatenate([inv_freq, inv_freq, jnp.zeros((LANES - ROT_DIM,), F32)]).reshape(1, LANES)
    shape = jax.ShapeDtypeStruct((m, LANES), F32)
    spec = pl.BlockSpec((tm, LANES), lambda i: (i, 0))
    return pl.pallas_call(
        _rope_table_kernel,
        out_shape=(shape, shape, shape),
        grid=(m // tm,),
        in_specs=[pl.BlockSpec((tm, 1), lambda i: (i, 0)), pl.BlockSpec((1, LANES), lambda i: (0, 0))],
        out_specs=(spec, spec, spec),
        compiler_params=_params("parallel"),
        name="rope_tables",
    )(positions.reshape(m, 1), freq_row)


def _attn_qk_kernel(a_ref, w_ref, cos_ref, sin_lo_ref, sin_hi_ref, o_ref, *, n_q_blocks):
    j = pl.program_id(0)
    acc = _dot(a_ref[...], w_ref[...])
    half = ROT_DIM // 2
    scale = jnp.where(j < n_q_blocks, A_HEAD_DIM ** -0.5, 1.0).astype(F32)
    cos, sin_lo, sin_hi = cos_ref[...], sin_lo_ref[...], sin_hi_ref[...]
    for g in range(acc.shape[1] // LANES):
        x = acc[:, g * LANES:(g + 1) * LANES]
        y = x * cos + pltpu.roll(x, LANES - half, 1) * sin_lo + pltpu.roll(x, half, 1) * sin_hi
        o_ref[:, g * LANES:(g + 1) * LANES] = (y * scale).astype(o_ref.dtype)


def attn_qk_proj(hn, w_qk, tables, *, tm=1024, tn=1024):
    m, k = hn.shape
    n = w_qk.shape[1]
    tm, tn = min(tm, m), min(tn, n // 2)
    tspec = pl.BlockSpec((tm, LANES), lambda j, i: (i, 0))
    return pl.pallas_call(
        functools.partial(_attn_qk_kernel, n_q_blocks=n // 2 // tn),
        out_shape=jax.ShapeDtypeStruct((m, n), BF16),
        grid=(n // tn, m // tm),
        in_specs=[pl.BlockSpec((tm, k), lambda j, i: (i, 0)),
                  pl.BlockSpec((k, tn), lambda j, i: (0, j)), tspec, tspec, tspec],
        out_specs=pl.BlockSpec((tm, tn), lambda j, i: (i, j)),
        compiler_params=_params("parallel", "parallel"),
        name="attn_qk_proj",
    )(hn, w_qk, *tables)


def _attn_vt_kernel(w_ref, a_ref, o_ref):
    o_ref[...] = _dot(w_ref[...], a_ref[...], NT).astype(o_ref.dtype)


def attn_v_proj_t(hn, w_t, batch, seq, blk, *, tn=1024):
    m, k = hn.shape
    n = w_t.shape[0]
    tn = min(tn, n)
    nk = seq // blk
    return pl.pallas_call(
        _attn_vt_kernel,
        out_shape=jax.ShapeDtypeStruct((batch, nk, n, blk), BF16),
        grid=(n // tn, m // blk),
        in_specs=[pl.BlockSpec((tn, k), lambda j, i: (j, 0)),
                  pl.BlockSpec((blk, k), lambda j, i: (i, 0))],
        out_specs=pl.BlockSpec((None, None, tn, blk), lambda j, i: (i // nk, i % nk, j, 0)),
        compiler_params=_params("parallel", "parallel"),
        name="attn_v_proj_t",
    )(w_t, hn)


def _diff_attn_kernel(lam_ref, q_ref, k_ref, vt_ref, z_ref, g_ref, o_ref, m_ref, acc_ref, s_ref,
                      *, bq, bk, lam_init):
    i = pl.program_id(2)
    lam = lam_ref[...]
    lam_full = (jnp.exp(jnp.sum(lam[0:1] * lam[1:2], axis=-1, keepdims=True))
                - jnp.exp(jnp.sum(lam[2:3] * lam[3:4], axis=-1, keepdims=True)) + lam_init)
    m_ref[...] = jnp.full(m_ref.shape, -jnp.inf, F32)
    acc_ref[...] = jnp.zeros(acc_ref.shape, F32)
    q = q_ref[...]
    d = A_HEAD_DIM
    w = A_V_DIM
    ones = jnp.ones((ONES_ROWS, bk), BF16)

    def scores(j, slot):
        start = pl.multiple_of(j * bk, bk)
        kb = k_ref[pl.ds(start, bk), :]
        for c in range(2):
            s_ref[slot, c] = _dot(kb[:, c * d:(c + 1) * d], q[:, c * d:(c + 1) * d], NT)

    def absorb(j, slot, masked):
        vt = jnp.concatenate([vt_ref[j], ones], axis=0)
        for c in range(2):
            s = s_ref[slot, c]
            if masked:
                kv_pos = j * bk + _iota(s.shape, 0)
                q_pos = i * bq + _iota(s.shape, 1)
                s = jnp.where(kv_pos <= q_pos, s, -jnp.inf)
            m_prev = m_ref[c]
            m_new = jnp.maximum(m_prev, jnp.max(s, axis=0, keepdims=True))
            alpha = jnp.exp(m_prev - m_new)
            p = jnp.exp(s - m_new)
            acc_ref[c] = alpha * acc_ref[c] + _dot(vt, p.astype(BF16))
            m_ref[c] = m_new

    r = bq // bk
    first_masked = r * i
    scores(0, 0)

    def body(jj, carry):
        scores(2 * jj + 1, 1)
        absorb(2 * jj, 0, False)
        scores(2 * jj + 2, 0)
        absorb(2 * jj + 1, 1, False)
        return carry

    lax.fori_loop(0, first_masked // 2, body, 0)
    for t in range(0, r, 2):
        scores(first_masked + t + 1, 1)
        absorb(first_masked + t, 0, True)
        if t + 2 < r:
            scores(first_masked + t + 2, 0)
        absorb(first_masked + t + 1, 1, True)

    o = (acc_ref[0, :w, :] / acc_ref[0, w:w + 1, :]
         - lam_full * (acc_ref[1, :w, :] / acc_ref[1, w:w + 1, :]))
    o = o * lax.rsqrt(jnp.mean(o * o, axis=0, keepdims=True) + SUBLN_EPS) * g_ref[...]
    o = (o * (1.0 - lam_init)).T
    o_ref[...] = (o * _silu(z_ref[...].astype(F32))).astype(o_ref.dtype)


def diff_attention_core(qk, z, vt, lam, subln_g, batch, seq, heads, lam_init):
    m = qk.shape[0]
    bk = vt.shape[-1]
    bq = ATTN_Q_BLOCKS * bk
    nq = seq // bq
    nk = seq // bk
    w = A_V_DIM
    return pl.pallas_call(
        functools.partial(_diff_attn_kernel, bq=bq, bk=bk, lam_init=lam_init),
        out_shape=jax.ShapeDtypeStruct((m, heads * w), BF16),
        grid=(batch, heads, nq),
        in_specs=[pl.BlockSpec((4, A_HEAD_DIM), lambda b, h, i: (0, 0)),
                  pl.BlockSpec((bq, w), lambda b, h, i: (b * nq + i, h)),
                  pl.BlockSpec((seq, w), lambda b, h, i: (b, heads + h)),
                  pl.BlockSpec((None, nk, w, bk), lambda b, h, i: (b, 0, h, 0)),
                  pl.BlockSpec((bq, w), lambda b, h, i: (b * nq + i, h)),
                  pl.BlockSpec((w, 1), lambda b, h, i: (0, 0))],
        out_specs=pl.BlockSpec((bq, w), lambda b, h, i: (b * nq + i, h)),
        scratch_shapes=[pltpu.VMEM((2, 1, bq), F32), pltpu.VMEM((2, w + ONES_ROWS, bq), F32),
                        pltpu.VMEM((2, 2, bk, bq), F32)],
        compiler_params=_params("parallel", "parallel", "parallel"),
        name="diff_attention",
    )(lam, qk, qk, vt, z, subln_g.reshape(w, 1))


def diff_attention_layer(x, hn, tables, w_in, lam, subln_g, w_out, batch, seq, lam_init):
    d = x.shape[1]
    heads = d // A_V_DIM
    qk_w = heads * 2 * A_HEAD_DIM
    v_w = heads * A_V_DIM
    blk = min(ATTN_BLOCK, seq // ATTN_Q_BLOCKS)
    w_vt = w_in[:, 2 * qk_w:2 * qk_w + v_w].T.astype(BF16)
    qk = attn_qk_proj(hn, w_in[:, :2 * qk_w].astype(BF16), tables)
    z = matmul(hn, w_in[:, 2 * qk_w + v_w:].astype(BF16), BF16, name="attn_gate_proj")
    vt = attn_v_proj_t(hn, w_vt, batch, seq, blk)
    o = diff_attention_core(qk, z, vt, lam, subln_g, batch, seq, heads, lam_init)
    return matmul_residual(o, w_out.astype(BF16), x, name="attn_out_proj")


def _gdn_conv_kernel(a_ref, w_ref, cw_ref, o_ref, tail_ref, *, rows_per_seq, sub):
    i = pl.program_id(1)
    tm = a_ref.shape[0]

    @pl.when((i * tm) % rows_per_seq == 0)
    def _():
        tail_ref[...] = jnp.zeros(tail_ref.shape, F32)

    a = a_ref[...]
    sub_iota = _iota((8, sub), 0)
    last = C_CONV_WIDTH - 1

    def epilogue(acc, cols):
        tail = tail_ref[:, cols]

        def shifted(s):
            xs = pltpu.roll(acc, s, 0)
            head = jnp.where(sub_iota < s, pltpu.roll(tail, s, 0), xs[:8])
            return jnp.concatenate([head, xs[8:]], axis=0)

        cw = cw_ref[:, cols]
        y = shifted(last) * cw[0:1]
        for t in range(1, last):
            y = y + shifted(last - t) * cw[t:t + 1]
        y = y + acc * cw[last:last + 1]
        tail_ref[:, cols] = acc[tm - 8:]
        o_ref[:, cols] = _silu(y).astype(o_ref.dtype)

    blocks = [slice(c0, c0 + sub) for c0 in range(0, w_ref.shape[1], sub)]
    acc = _dot(a, w_ref[:, blocks[0]])
    for c, cols in enumerate(blocks):
        nxt = _dot(a, w_ref[:, blocks[c + 1]]) if c + 1 < len(blocks) else None
        epilogue(acc, cols)
        acc = nxt


def gdn_conv_proj(hn, w, conv_w, seq, *, tm=1024, tn=2048, sub=256):
    m, k = hn.shape
    n = w.shape[1]
    tm, tn = min(tm, seq), min(tn, n)
    return pl.pallas_call(
        functools.partial(_gdn_conv_kernel, rows_per_seq=seq, sub=min(sub, tn)),
        out_shape=jax.ShapeDtypeStruct((m, n), BF16),
        grid=(n // tn, m // tm),
        in_specs=[pl.BlockSpec((tm, k), lambda j, i: (i, 0)),
                  pl.BlockSpec((k, tn), lambda j, i: (0, j)),
                  pl.BlockSpec((C_CONV_WIDTH, tn), lambda j, i: (0, j))],
        out_specs=pl.BlockSpec((tm, tn), lambda j, i: (i, j)),
        scratch_shapes=[pltpu.VMEM((8, tn), F32)],
        compiler_params=_params("arbitrary", "arbitrary"),
        name="gdn_conv_proj",
    )(hn, w, conv_w)


def _pair_masks():
    n = 2 * CHUNK
    r, c = _iota((n, n), 0), _iota((n, n), 1)
    same = (r // CHUNK) == (c // CHUNK)
    return same & (r >= c), same & (r > c), r == c


def _dot_x2l(a, b_exact, dims=NN):
    h, l = _split2(a)
    return _dot(h, b_exact, dims) + _dot(l, b_exact, dims)


def _dot_x2r(a_exact, b, dims=NN):
    h, l = _split2(b)
    return _dot(a_exact, h, dims) + _dot(a_exact, l, dims)


def _inverse_stages(p_ref, t_ref, count, nilpotency):
    for _ in range(int(math.log2(nilpotency)) - 1):
        for i in range(count):
            p_ref[i] = _dot(p_ref[i], p_ref[i]).astype(BF16)
        for i in range(count):
            t = t_ref[i]
            t_ref[i] = t + _dot(t.astype(BF16), p_ref[i])


def _gdn_gates_kernel(a_ref, w_ref, alog_ref, dtb_ref, beta_ref, gc_ref, gct_ref):
    acc = _dot(a_ref[...], w_ref[...])
    beta_ref[...] = jax.nn.sigmoid(acc[:, :LANES])
    g = -jnp.exp(alog_ref[...]) * _softplus(acc[:, LANES:] + dtb_ref[...])
    tri, _, diag = _pair_masks()
    tri_b, eye_b = tri.astype(BF16), diag.astype(BF16)
    n2 = 2 * CHUNK
    for n in range(g.shape[0] // n2):
        rows = slice(n * n2, (n + 1) * n2)
        gc = _dot_xr(tri_b, g[rows])
        gc_ref[rows, :] = gc
        gct_ref[:, rows] = _dot_xr(eye_b, gc, NT)


def gdn_gates(hn, w_ba, alog_row, dtb_row, *, tm=512):
    m, k = hn.shape
    tm = min(tm, m)
    row = pl.BlockSpec((1, LANES), lambda i: (0, 0))
    out = pl.BlockSpec((tm, LANES), lambda i: (i, 0))
    return pl.pallas_call(
        _gdn_gates_kernel,
        out_shape=(jax.ShapeDtypeStruct((m, LANES), F32), jax.ShapeDtypeStruct((m, LANES), F32),
                   jax.ShapeDtypeStruct((LANES, m), F32)),
        grid=(m // tm,),
        in_specs=[pl.BlockSpec((tm, k), lambda i: (i, 0)), pl.BlockSpec((k, 2 * LANES), lambda i: (0, 0)), row, row],
        out_specs=(out, out, pl.BlockSpec((LANES, tm), lambda i: (0, i))),
        compiler_params=_params("parallel"),
        name="gdn_gates",
    )(hn, w_ba, alog_row, dtb_row)


def _gdn_prep_kernel(q_ref, k_ref, v_ref, beta_ref, gc_ref, gct_ref,
                     w_ref, qg_ref, u_ref, att_ref, kdt_ref, dec_ref,
                     g_s, dec_s, kn_s, kb_s, rhs_s, p_s, t_s, qn_s, kdec_s):
    kh = pl.program_id(1)
    c = CHUNK
    n2 = 2 * c
    dk = C_HEAD_DIM
    nv = w_ref.shape[0]
    npairs = q_ref.shape[0] // n2
    tri, strict, diag = _pair_masks()
    eye_f = diag.astype(F32)
    pairs = [slice(n * n2, (n + 1) * n2) for n in range(npairs)]
    lane = _iota((n2, LANES), 1)

    for n, rows in enumerate(pairs):
        for e in range(nv):
            h = kh * nv + e
            gc = jnp.broadcast_to(jnp.sum(jnp.where(lane == h, gc_ref[rows, :], 0.0), axis=-1, keepdims=True),
                                  (n2, dk))
            grp = gct_ref[pl.ds(pl.multiple_of((h // 8) * 8, 8), 8), rows]
            own_row = jnp.sum(jnp.where(_iota(grp.shape, 0) == h % 8, grp, 0.0), axis=0, keepdims=True)
            gc_row = jnp.broadcast_to(own_row, (n2, n2))
            g_s[n * nv + e] = gc
            dec_s[n * nv + e] = jnp.where(tri, jnp.exp(jnp.where(tri, gc - gc_row, 0.0)), 0.0)
    slab = 32
    lane_slab = _iota((slab, LANES), 1)
    for n, rows in enumerate(pairs):
        for r0 in range(0, n2, slab):
            rs = slice(r0, r0 + slab)
            gs = slice(n * n2 + r0, n * n2 + r0 + slab)
            last = c - 1 if r0 < c else n2 - 1
            qf = q_ref[gs, :].astype(F32)
            kf = k_ref[gs, :].astype(F32)
            qn = qf * lax.rsqrt(jnp.sum(qf * qf, axis=-1, keepdims=True) + 1e-6) * (dk ** -0.5)
            kn = kf * lax.rsqrt(jnp.sum(kf * kf, axis=-1, keepdims=True) + 1e-6)
            kn_s[n, rs, :] = kn.astype(BF16)
            qn_s[n, rs, :] = qn.astype(BF16)
            for e in range(nv):
                i = n * nv + e
                gc = g_s[i, rs, :]
                vf = v_ref[gs, e * dk:(e + 1) * dk].astype(F32)
                beta = jnp.sum(jnp.where(lane_slab == kh * nv + e, beta_ref[gs, :], 0.0), axis=-1, keepdims=True)
                egc = jnp.exp(gc)
                kb = kn * beta
                kb_s[i, rs, :] = kb.astype(BF16)
                rhs_s[i, rs, :dk] = (vf * beta).astype(BF16)
                rhs_s[i, rs, dk:] = (kb * egc).astype(BF16)
                kdec_s[i, rs, :] = kn * jnp.exp(g_s[i, last:last + 1, :] - gc)
                qg_ref[e, gs, :] = (qn * egc).astype(BF16)
        for e in range(nv):
            i = n * nv + e
            dec_ref[e, 2 * n:2 * n + 1, :] = jnp.exp(g_s[i, c - 1:c, :])
            dec_ref[e, 2 * n + 1:2 * n + 2, :] = jnp.exp(g_s[i, n2 - 1:n2, :])
    for n, rows in enumerate(pairs):
        raw = _dot(qn_s[n], kn_s[n], NT)
        for e in range(nv):
            att_ref[e, rows, :] = (raw * dec_s[n * nv + e]).astype(BF16)
            kdt_ref[e, :, rows] = kdec_s[n * nv + e].T.astype(BF16)
    for n, rows in enumerate(pairs):
        for e in range(nv):
            i = n * nv + e
            neg_a = jnp.where(strict, -(_dot(kb_s[i], kn_s[n], NT) * dec_s[i]), 0.0)
            p_s[i] = neg_a.astype(BF16)
            t_s[i] = eye_f + neg_a
    _inverse_stages(p_s, t_s, npairs * nv, c)
    for n, rows in enumerate(pairs):
        for e in range(nv):
            sol = _dot(t_s[n * nv + e].astype(BF16), rhs_s[n * nv + e])
            u_ref[e, rows, :] = sol[:, :dk]
            w_ref[e, rows, :] = sol[:, dk:].astype(BF16)


def _gdn_scan_kernel(w_ref, qg_ref, u_ref, att_ref, kdt_ref, dec_ref, z_ref, g_ref, o_ref,
                     s_ref, ms_s, vp_s, *, group):
    c = CHUNK
    dk = C_HEAD_DIM
    zeros = jnp.zeros((c, dk), BF16)

    @pl.when(pl.program_id(2) == 0)
    def _():
        s_ref[...] = jnp.zeros(s_ref.shape, F32)

    for n in range(w_ref.shape[1] // c):
        rows = slice(n * c, (n + 1) * c)
        pair = slice((n // 2) * 2 * c, (n // 2 + 1) * 2 * c)
        for gi in range(group):
            lhs = jnp.concatenate([w_ref[gi, rows, :], qg_ref[gi, rows, :]], axis=0)
            ms_s[gi] = _dot(lhs, s_ref[gi].astype(BF16))
        for gi in range(group):
            v_new = (u_ref[gi, rows, :] - ms_s[gi, :c, :]).astype(BF16)
            vp_s[gi] = jnp.concatenate([v_new, zeros] if n % 2 == 0 else [zeros, v_new], axis=0)
        for gi in range(group):
            s_ref[gi] = s_ref[gi] * dec_ref[gi, n:n + 1, :] + _dot(kdt_ref[gi, :, pair], vp_s[gi])
        for gi in range(group):
            o = ms_s[gi, c:, :] + _dot(att_ref[gi, rows, :], vp_s[gi])
            o = o * lax.rsqrt(jnp.mean(o * o, axis=-1, keepdims=True) + NORM_EPS) * g_ref[...]
            z = z_ref[rows, gi * dk:(gi + 1) * dk].astype(F32)
            o_ref[rows, gi * dk:(gi + 1) * dk] = (o * _silu(z)).astype(o_ref.dtype)


def gated_deltanet_layer(x, hn, w_in, conv_w, a_log, dt_bias, norm_g, w_out, batch, seq):
    m, d = x.shape
    dk = C_HEAD_DIM
    k_heads = d // dk
    v_heads = 2 * k_heads
    conv_ch = 2 * k_heads * dk + v_heads * dk
    main_w = conv_ch + v_heads * dk
    qkv = gdn_conv_proj(hn, w_in[:, :conv_ch].astype(BF16), conv_w, seq)
    z = matmul(hn, w_in[:, conv_ch:main_w].astype(BF16), BF16, name="gdn_gate_proj")
    pad = jnp.zeros((d, LANES - v_heads), F32)
    w_ba = jnp.concatenate([w_in[:, main_w:main_w + v_heads], pad, w_in[:, main_w + v_heads:], pad], axis=1)
    row_pad = jnp.zeros((LANES - v_heads,), F32)
    alog_row = jnp.concatenate([a_log, row_pad]).reshape(1, LANES)
    dtb_row = jnp.concatenate([dt_bias, row_pad]).reshape(1, LANES)
    beta_all, gc_all, gc_t = gdn_gates(hn, w_ba.astype(BF16), alog_row, dtb_row)

    rows = min(PREP_ROWS, seq)
    nr = seq // rows
    ncr = rows // CHUNK
    hv = v_heads
    nv = v_heads // k_heads
    nchain = nv * rows // (2 * CHUNK)
    npair = rows // (2 * CHUNK)
    bh_t = lambda dt, last: jax.ShapeDtypeStruct((batch, hv, seq, last), dt)
    blk4 = lambda last: pl.BlockSpec((None, nv, rows, last), lambda b, h, i: (b, h, i, 0))
    qoff, koff, voff = 0, k_heads, 2 * k_heads // nv
    w_, qg_, u_, att_, kdt_, dec_ = pl.pallas_call(
        _gdn_prep_kernel,
        out_shape=(bh_t(BF16, dk), bh_t(BF16, dk), bh_t(F32, dk), bh_t(BF16, 2 * CHUNK),
                   jax.ShapeDtypeStruct((batch, hv, dk, seq), BF16),
                   jax.ShapeDtypeStruct((batch, hv, seq // CHUNK, dk), F32)),
        grid=(batch, k_heads, nr),
        in_specs=[pl.BlockSpec((rows, dk), lambda b, h, i: (b * nr + i, qoff + h)),
                  pl.BlockSpec((rows, dk), lambda b, h, i: (b * nr + i, koff + h)),
                  pl.BlockSpec((rows, nv * dk), lambda b, h, i: (b * nr + i, voff + h)),
                  pl.BlockSpec((rows, LANES), lambda b, h, i: (b * nr + i, 0)),
                  pl.BlockSpec((rows, LANES), lambda b, h, i: (b * nr + i, 0)),
                  pl.BlockSpec((LANES, rows), lambda b, h, i: (0, b * nr + i))],
        out_specs=(blk4(dk), blk4(dk), blk4(dk), blk4(2 * CHUNK),
                   pl.BlockSpec((None, nv, dk, rows), lambda b, h, i: (b, h, 0, i)),
                   pl.BlockSpec((None, nv, ncr, dk), lambda b, h, i: (b, h, i, 0))),
        scratch_shapes=[pltpu.VMEM((nchain, dk, dk), F32), pltpu.VMEM((nchain, dk, dk), F32),
                        pltpu.VMEM((npair, dk, dk), BF16), pltpu.VMEM((nchain, dk, dk), BF16),
                        pltpu.VMEM((nchain, dk, 2 * dk), BF16), pltpu.VMEM((nchain, dk, dk), BF16),
                        pltpu.VMEM((nchain, dk, dk), F32), pltpu.VMEM((npair, dk, dk), BF16),
                        pltpu.VMEM((nchain, dk, dk), F32)],
        compiler_params=_params("parallel", "parallel", "parallel"),
        name="gdn_prep",
    )(qkv, qkv, qkv, beta_all, gc_all, gc_t)

    rows = min(SCAN_ROWS, seq)
    nr = seq // rows
    ncr = rows // CHUNK
    group = 16
    gblk = lambda last: pl.BlockSpec((None, group, rows, last), lambda b, h, i: (b, h, i, 0))
    o = pl.pallas_call(
        functools.partial(_gdn_scan_kernel, group=group),
        out_shape=jax.ShapeDtypeStruct((m, hv * dk), BF16),
        grid=(batch, hv // group, nr),
        in_specs=[gblk(dk), gblk(dk), gblk(dk), gblk(2 * CHUNK),
                  pl.BlockSpec((None, group, dk, rows), lambda b, h, i: (b, h, 0, i)),
                  pl.BlockSpec((None, group, ncr, dk), lambda b, h, i: (b, h, i, 0)),
                  pl.BlockSpec((rows, group * dk), lambda b, h, i: (b * nr + i, h)),
                  pl.BlockSpec((1, dk), lambda b, h, i: (0, 0))],
        out_specs=pl.BlockSpec((rows, group * dk), lambda b, h, i: (b * nr + i, h)),
        scratch_shapes=[pltpu.VMEM((group, dk, dk), F32), pltpu.VMEM((group, 2 * CHUNK, dk), F32),
                        pltpu.VMEM((group, 2 * CHUNK, dk), BF16)],
        compiler_params=_params("parallel", "parallel", "arbitrary"),
        name="gdn_scan",
    )(w_, qg_, u_, att_, kdt_, dec_, z, norm_g.reshape(1, dk))
    return matmul_residual(o, w_out.astype(BF16), x, tm=512, name="gdn_out_proj")


def _rwkv_mix_kernel(x_ref, g_ref, mu_ref, o_ref, tail_ref, *, rows_per_seq):
    i = pl.program_id(0)
    x = x_ref[...]
    tm = x.shape[0]
    hn = x * lax.rsqrt(jnp.mean(x * x, axis=-1, keepdims=True) + NORM_EPS) * g_ref[...]

    @pl.when((i * tm) % rows_per_seq == 0)
    def _():
        tail_ref[...] = jnp.zeros(tail_ref.shape, F32)

    prev = jnp.where(_iota(hn.shape, 0) == 0, tail_ref[7:8, :], pltpu.roll(hn, 1, 0))
    tail_ref[...] = hn[tm - 8:]
    xx = prev - hn
    for c in range(o_ref.shape[0]):
        o_ref[c] = (hn + xx * mu_ref[c:c + 1, :]).astype(o_ref.dtype)


def rwkv_token_mix(x, norm_g, mu, seq, *, tm=256):
    m, d = x.shape
    tm = min(tm, seq)
    nmix = mu.shape[0]
    return pl.pallas_call(
        functools.partial(_rwkv_mix_kernel, rows_per_seq=seq),
        out_shape=jax.ShapeDtypeStruct((nmix, m, d), BF16),
        grid=(m // tm,),
        in_specs=[pl.BlockSpec((tm, d), lambda i: (i, 0)),
                  pl.BlockSpec((1, d), lambda i: (0, 0)),
                  pl.BlockSpec((nmix, d), lambda i: (0, 0))],
        out_specs=pl.BlockSpec((nmix, tm, d), lambda i: (0, i, 0)),
        scratch_shapes=[pltpu.VMEM((8, d), F32)],
        compiler_params=_params("arbitrary"),
        name="rwkv_token_mix",
    )(x, norm_g.reshape(1, d), mu)


def _grouped_mm_kernel(a_ref, w_ref, o_ref):
    o_ref[...] = _dot(a_ref[...], w_ref[...]).astype(o_ref.dtype)


def grouped_matmul(a, w, out_dtype, *, tm=1024, tn=1024):
    g, k, n = w.shape
    m = a.shape[1]
    tm, tn = min(tm, m), min(tn, n)
    return pl.pallas_call(
        _grouped_mm_kernel,
        out_shape=jax.ShapeDtypeStruct((g, m, n), out_dtype),
        grid=(g, n // tn, m // tm),
        in_specs=[pl.BlockSpec((None, tm, k), lambda c, j, i: (c, i, 0)),
                  pl.BlockSpec((None, k, tn), lambda c, j, i: (c, 0, j))],
        out_specs=pl.BlockSpec((None, tm, tn), lambda c, j, i: (c, i, j)),
        compiler_params=_params("parallel", "parallel", "parallel"),
        name="rwkv_rkvg_proj",
    )(a, w)


def _rwkv_lora_kernel(xw_ref, xa_ref, w1_ref, w2_ref, a1_ref, a2_ref, w0_ref, a0_ref, ld_ref, a_ref):
    hw = jnp.tanh(_dot(xw_ref[...], w1_ref[...])).astype(BF16)
    lw = w0_ref[...] + _dot(hw, w2_ref[...])
    log_w = -_softplus(-lw) - 0.5
    ld_ref[...] = -jnp.exp(log_w)
    ha = _dot(xa_ref[...], a1_ref[...]).astype(BF16)
    a_ref[...] = jax.nn.sigmoid(a0_ref[...] + _dot(ha, a2_ref[...]))


def rwkv_lora(xs, w1, w2, a1, a2, w0, a0, *, tm=512):
    _, m, d = xs.shape
    tm = min(tm, m)
    r = w1.shape[1]
    row = pl.BlockSpec((1, d), lambda i: (0, 0))
    shape = jax.ShapeDtypeStruct((m, d), F32)
    return pl.pallas_call(
        _rwkv_lora_kernel,
        out_shape=(shape, shape),
        grid=(m // tm,),
        in_specs=[pl.BlockSpec((None, tm, d), lambda i: (4, i, 0)),
                  pl.BlockSpec((None, tm, d), lambda i: (5, i, 0)),
                  pl.BlockSpec((d, r), lambda i: (0, 0)), pl.BlockSpec((r, d), lambda i: (0, 0)),
                  pl.BlockSpec((d, r), lambda i: (0, 0)), pl.BlockSpec((r, d), lambda i: (0, 0)),
                  row, row],
        out_specs=(pl.BlockSpec((tm, d), lambda i: (i, 0)), pl.BlockSpec((tm, d), lambda i: (i, 0))),
        compiler_params=_params("parallel"),
        name="rwkv_lora",
    )(xs, xs, w1, w2, a1, a2, w0.reshape(1, d), a0.reshape(1, d))


def _head_sum(x, first_head):
    tot = jnp.sum(x, axis=-1, keepdims=True)
    lo = jnp.sum(jnp.where(first_head, x, 0.0), axis=-1, keepdims=True)
    return jnp.where(first_head, lo, tot - lo)


def _rwkv_prep_kernel(r_ref, k_ref, v_ref, ld_ref, a_ref, kk_ref, ka_ref, rk_ref,
                      wt_ref, rt_ref, u_ref, aro_ref, kbt_ref, vb_ref, dec_ref, bonus_ref,
                      kkt_s, kb_s, akk_s, p_s, t_s, drow_s, *, cps):
    c = CHUNK
    n2 = 2 * c
    n_h = B_HEAD_DIM
    npairs = r_ref.shape[0] // n2
    first_head = _iota((1, LANES), 1) < n_h
    tri, strict, diag = _pair_masks()
    tri_b = tri.astype(BF16)
    eye_b = diag.astype(BF16)
    eye_f = diag.astype(F32)
    upper_half = _iota((n2, LANES), 0) < c
    pairs = [slice(n * n2, (n + 1) * n2) for n in range(npairs)]

    for n, rows in enumerate(pairs):
        r, k, v, a = r_ref[rows, :], k_ref[rows, :], v_ref[rows, :], a_ref[rows, :]
        kk_raw = k * kk_ref[...]
        kk = kk_raw * lax.rsqrt(_head_sum(kk_raw * kk_raw, first_head) + 1e-6)
        k2c = k * (1.0 + (a - 1.0) * ka_ref[...])
        bbc = kk * a
        bonus_ref[rows, :] = _head_sum(r * k2c * rk_ref[...], first_head) * v
        vb_ref[rows, :] = v.astype(BF16)
        ld = ld_ref[rows, :]
        cs = _dot_x2r(tri_b, ld)
        cs_last = jnp.where(upper_half, cs[c - 1:c, :], cs[n2 - 1:n2, :])
        w_inv = jnp.exp(-cs)
        w_end = jnp.exp(cs_last - cs)
        rt_ref[rows, :] = (r * jnp.exp(cs)).astype(BF16)
        kkt_s[n] = (kk * jnp.exp(cs - ld)).astype(BF16)
        kb_s[n] = jnp.concatenate([k2c * w_inv, bbc * w_inv], axis=0).astype(BF16)
        kw, bw = k2c * w_end, -(bbc * w_end)
        for half in range(2):
            hs = slice(half * c, (half + 1) * c)
            kb_end = jnp.concatenate([kw[hs], bw[hs]], axis=0)
            cols = slice((2 * n + half) * n2, (2 * n + half + 1) * n2)
            kbt_ref[:, cols] = kb_end.T.astype(BF16)
        blk, off = divmod(2 * n, cps)
        drow_s[blk, off:off + 1, :] = jnp.exp(cs[c - 1:c, :])
        drow_s[blk, off + 1:off + 2, :] = jnp.exp(cs[n2 - 1:n2, :])
    for blk in range(dec_ref.shape[0] // LANES):
        drow_s[blk, cps:, :] = jnp.zeros((LANES - cps, LANES), F32)
        dec_ref[blk * LANES:(blk + 1) * LANES, :] = _dot_xr(eye_b, drow_s[blk], NT)
    zero_b = jnp.zeros((n2, LANES), BF16)
    strict2 = jnp.concatenate([strict, strict], axis=1)
    tri2 = jnp.concatenate([tri, tri], axis=1)
    sign2 = jnp.where(_iota((n2, 2 * n2), 1) < n2, 1.0, -1.0)
    for n, rows in enumerate(pairs):
        kkt, rt = kkt_s[n], rt_ref[rows, :]
        lhs = jnp.concatenate([jnp.where(first_head, kkt, zero_b), jnp.where(first_head, rt, zero_b),
                               jnp.where(first_head, zero_b, kkt), jnp.where(first_head, zero_b, rt)], axis=0)
        gram = _dot(lhs, kb_s[n], NT)
        for hh in range(2):
            i = 2 * n + hh
            g_kk = jnp.where(strict2, gram[2 * hh * n2:(2 * hh + 1) * n2], 0.0)
            neg_a = -g_kk[:, n2:]
            p_s[i] = neg_a.astype(BF16)
            t_s[i] = eye_f + neg_a
            akk_s[i] = g_kk[:, :n2].astype(BF16)
            g_r = gram[(2 * hh + 1) * n2:(2 * hh + 2) * n2]
            aro_ref[hh, rows, :] = jnp.where(tri2, g_r * sign2, 0.0).astype(BF16)
    _inverse_stages(p_s, t_s, 2 * npairs, c)
    for n, rows in enumerate(pairs):
        av = _dot(jnp.concatenate([akk_s[2 * n], akk_s[2 * n + 1]], axis=0), vb_ref[rows, :])
        p_s[2 * n] = av[:n2].astype(BF16)
        p_s[2 * n + 1] = av[n2:].astype(BF16)
    for n, rows in enumerate(pairs):
        t0, t1 = t_s[2 * n].astype(BF16), t_s[2 * n + 1].astype(BF16)
        wt = _dot(jnp.concatenate([t0, t1], axis=0), kkt_s[n])
        wt_ref[rows, :] = jnp.where(first_head, wt[:n2], wt[n2:]).astype(BF16)
        u_ref[rows, :] = jnp.where(first_head, _dot(t0, p_s[2 * n]), _dot(t1, p_s[2 * n + 1]))


def _rwkv_scan_kernel(wt_ref, rt_ref, u_ref, aro_ref, kbt_ref, vb_ref, dec_ref, bonus_ref, gate_ref,
                      lnw_ref, lnb_ref, o_ref, s_ref, ms_s, xc_s, xp_s, *, group):
    c = CHUNK
    n2 = 2 * c
    n_h = B_HEAD_DIM
    first_head = _iota((1, LANES), 1) < n_h
    same_head = _iota((LANES, LANES), 0) // n_h == _iota((LANES, LANES), 1) // n_h
    zeros = jnp.zeros((c, LANES), BF16)

    def head_mean(x):
        return _head_sum(x, first_head) * (1.0 / n_h)

    @pl.when(pl.program_id(2) == 0)
    def _():
        s_ref[...] = jnp.zeros(s_ref.shape, F32)

    for n in range(u_ref.shape[1] // c):
        rows = slice(n * c, (n + 1) * c)
        for gi in range(group):
            lhs = jnp.concatenate([wt_ref[gi, rows, :], rt_ref[gi, rows, :]], axis=0)
            ms_s[gi] = _dot(lhs, s_ref[gi].astype(BF16))
        for gi in range(group):
            sa = (u_ref[gi, rows, :] + ms_s[gi, :c, :]).astype(BF16)
            vb = vb_ref[gi, rows, :]
            xc_s[gi] = jnp.concatenate([vb, sa], axis=0)
            xp_s[gi] = jnp.concatenate([vb, zeros, sa, zeros] if n % 2 == 0 else [zeros, vb, zeros, sa], axis=0)
        for gi in range(group):
            upd = _dot(kbt_ref[gi, :, n * n2:(n + 1) * n2], xc_s[gi])
            dec_col = jnp.broadcast_to(dec_ref[gi, :, n:n + 1], (LANES, LANES))
            s_ref[gi] = s_ref[gi] * dec_col + jnp.where(same_head, upd, 0.0)
        for gi in range(group):
            both = _dot(jnp.concatenate([aro_ref[gi, 0, rows, :], aro_ref[gi, 1, rows, :]], axis=0), xp_s[gi])
            o = ms_s[gi, c:, :] + jnp.where(first_head, both[:c], both[c:])
            mean = head_mean(o)
            dlt = o - mean
            var = head_mean(dlt * dlt)
            cols = slice(gi * LANES, (gi + 1) * LANES)
            y = dlt * lax.rsqrt(var + B_GN_EPS) * lnw_ref[:, cols] + lnb_ref[:, cols]
            y = y + bonus_ref[rows, cols]
            o_ref[rows, cols] = (y * _silu(gate_ref[rows, cols])).astype(o_ref.dtype)


def rwkv7_layer(x, norm_g, mu, w_rkvg, w0, w_w1, w_w2, a0, w_a1, w_a2, k_k, k_a, r_k, ln_w, ln_b, w_out,
                batch, seq):
    m, d = x.shape
    heads = d // B_HEAD_DIM
    pairs = heads // 2
    order = jnp.array([0, 2, 3, 5, 1, 4])
    xs = rwkv_token_mix(x, norm_g, mu[order], seq)
    rkvg = grouped_matmul(xs, w_rkvg.astype(BF16), F32)
    lora = w_w1.shape[1]
    padc = lambda w: jnp.pad(w, ((0, 0), (0, LORA_PAD - lora))).astype(BF16)
    padr = lambda w: jnp.pad(w, ((0, LORA_PAD - lora), (0, 0))).astype(BF16)
    ld, a = rwkv_lora(xs, padc(w_w1), padr(w_w2), padc(w_a1), padr(w_a2), w0, a0)

    srows = min(SCAN_ROWS, seq)
    rows = min(PREP_ROWS, seq)
    nr = seq // rows
    npair = rows // (2 * CHUNK)
    col = lambda g: pl.BlockSpec((None, rows, LANES), lambda b, p, i, g=g: (g, b * nr + i, p))
    flat = pl.BlockSpec((rows, LANES), lambda b, p, i: (b * nr + i, p))
    prow = pl.BlockSpec((1, LANES), lambda b, p, i: (0, p))
    bp = lambda rws, last, dt: jax.ShapeDtypeStruct((batch, pairs, rws, last), dt)
    pblk = lambda rws, last: pl.BlockSpec((None, None, rws, last), lambda b, p, i: (b, p, i, 0))
    wt_, rt_, u_, aro_, kbt_, vb_, dec_, bonus = pl.pallas_call(
        functools.partial(_rwkv_prep_kernel, cps=srows // CHUNK),
        out_shape=(bp(seq, LANES, BF16), bp(seq, LANES, BF16), bp(seq, LANES, F32),
                   jax.ShapeDtypeStruct((batch, pairs, 2, seq, 4 * CHUNK), BF16),
                   bp(LANES, 2 * seq, BF16), bp(seq, LANES, BF16), bp(seq // srows * LANES, LANES, F32),
                   jax.ShapeDtypeStruct((m, d), F32)),
        grid=(batch, pairs, nr),
        in_specs=[col(0), col(1), col(2), flat, flat, prow, prow, prow],
        out_specs=(pblk(rows, LANES), pblk(rows, LANES), pblk(rows, LANES),
                   pl.BlockSpec((None, None, 2, rows, 4 * CHUNK), lambda b, p, i: (b, p, 0, i, 0)),
                   pl.BlockSpec((None, None, LANES, 2 * rows), lambda b, p, i: (b, p, 0, i)),
                   pblk(rows, LANES), pblk(rows // srows * LANES, LANES), flat),
        scratch_shapes=[pltpu.VMEM((npair, LANES, LANES), BF16), pltpu.VMEM((npair, 2 * LANES, LANES), BF16),
                        pltpu.VMEM((2 * npair, LANES, LANES), BF16), pltpu.VMEM((2 * npair, LANES, LANES), BF16),
                        pltpu.VMEM((2 * npair, LANES, LANES), F32),
                        pltpu.VMEM((rows // srows, LANES, LANES), F32)],
        compiler_params=_params("parallel", "parallel", "parallel"),
        name="rwkv_prep",
    )(rkvg, rkvg, rkvg, ld, a, k_k.reshape(1, d), k_a.reshape(1, d), r_k.reshape(1, d))

    rows = srows
    nr = seq // rows
    group = 8
    gw = group * LANES
    gblk = lambda rws, last: pl.BlockSpec((None, group, rws, last), lambda b, p, i: (b, p, i, 0))
    gflat = pl.BlockSpec((rows, gw), lambda b, p, i: (b * nr + i, p))
    grow = pl.BlockSpec((1, gw), lambda b, p, i: (0, p))
    o = pl.pallas_call(
        functools.partial(_rwkv_scan_kernel, group=group),
        out_shape=jax.ShapeDtypeStruct((m, d), BF16),
        grid=(batch, pairs // group, nr),
        in_specs=[gblk(rows, LANES), gblk(rows, LANES), gblk(rows, LANES),
                  pl.BlockSpec((None, group, 2, rows, 4 * CHUNK), lambda b, p, i: (b, p, 0, i, 0)),
                  pl.BlockSpec((None, group, LANES, 2 * rows), lambda b, p, i: (b, p, 0, i)),
                  gblk(rows, LANES), gblk(LANES, LANES), gflat,
                  pl.BlockSpec((None, rows, gw), lambda b, p, i: (3, b * nr + i, p)),
                  grow, grow],
        out_specs=gflat,
        scratch_shapes=[pltpu.VMEM((group, LANES, LANES), F32), pltpu.VMEM((group, 2 * CHUNK, LANES), F32),
                        pltpu.VMEM((group, 2 * CHUNK, LANES), BF16), pltpu.VMEM((group, 4 * CHUNK, LANES), BF16)],
        compiler_params=_params("parallel", "parallel", "arbitrary"),
        name="rwkv_scan",
    )(wt_, rt_, u_, aro_, kbt_, vb_, dec_, bonus, rkvg, ln_w.reshape(1, d), ln_b.reshape(1, d))
    return matmul_residual(o, w_out.astype(BF16), x, name="rwkv_out_proj")


def kernel(x, p, positions, norm_g, pe_norm_g, pe_w_gate, pe_w_proj, final_norm_g, a_w_in, a_lam, a_subln_g, a_w_out, b_mu, b_w_rkvg, b_w0, b_w_w1, b_w_w2, b_a0, b_w_a1, b_w_a2, b_k_k, b_k_a, b_r_k, b_ln_w, b_ln_b, b_w_out, c_w_in, c_conv_w, c_A_log, c_dt_bias, c_norm_g, c_w_out):
    batch, seq, d = x.shape
    depth = p.shape[0]
    m = batch * seq
    xf = x.reshape(m, d)
    tables = rope_tables(positions)
    for i in range(depth):
        kind = i % N_MIXERS
        j = i // N_MIXERS
        if kind == 0:
            lam_init = 0.8 - 0.6 * math.exp(-0.3 * i)
            hn = rmsnorm(xf, norm_g[i], BF16)
            xf = diff_attention_layer(xf, hn, tables, a_w_in[j], a_lam[j], a_subln_g[j], a_w_out[j],
                                      batch, seq, lam_init)
        elif kind == 1:
            xf = rwkv7_layer(xf, norm_g[i], b_mu[j], b_w_rkvg[j], b_w0[j], b_w_w1[j], b_w_w2[j], b_a0[j],
                             b_w_a1[j], b_w_a2[j], b_k_k[j], b_k_a[j], b_r_k[j], b_ln_w[j], b_ln_b[j],
                             b_w_out[j], batch, seq)
        else:
            hn = rmsnorm(xf, norm_g[i], BF16)
            xf = gated_deltanet_layer(xf, hn, c_w_in[j], c_conv_w[j], c_A_log[j], c_dt_bias[j], c_norm_g[j],
                                      c_w_out[j], batch, seq)
        hn2 = rmsnorm(xf, pe_norm_g[i], BF16)
        xf = per_layer_embedding(xf, hn2, pe_w_gate[i].astype(BF16), p[i].reshape(m, -1),
                                 pe_w_proj[i].astype(BF16))
    return rmsnorm(xf, final_norm_g, F32).reshape(batch, seq, d)
```

```python
import functools
import math

import jax
import jax.numpy as jnp
from jax import lax
from jax.experimental import pallas as pl
from jax.experimental.pallas import tpu as pltpu

F32 = jnp.float32
BF16 = jnp.bfloat16

N_MIXERS = 3
NORM_EPS = 1e-6
LANES = 128
VMEM_LIMIT = 48 * 1024 * 1024

A_HEAD_DIM = 128
A_V_DIM = 2 * A_HEAD_DIM
ROT_DIM = A_HEAD_DIM // 4
ROPE_THETA = 500000.0
SUBLN_EPS = 1e-5
ATTN_BLOCK = 256
ATTN_Q_BLOCKS = 2
ONES_ROWS = 16

B_HEAD_DIM = 64
B_GN_EPS = 64e-5
LORA_PAD = 128

C_HEAD_DIM = 128
C_CONV_WIDTH = 4
CHUNK = 64
SCAN_ROWS = 512
PREP_ROWS = 1024

NN = (((1,), (0,)), ((), ()))
NT = (((1,), (1,)), ((), ()))


def _dot(a, b, dims=NN):
    return lax.dot_general(a, b, dims, preferred_element_type=F32)


def _split2(x):
    hi = x.astype(BF16)
    lo = (x - hi.astype(F32)).astype(BF16)
    return hi, lo


def _split3(x):
    hi = x.astype(BF16)
    r = x - hi.astype(F32)
    mid = r.astype(BF16)
    lo = (r - mid.astype(F32)).astype(BF16)
    return hi, mid, lo


def _dot3(a, b, dims=NN):
    ah, al = _split2(a)
    bh, bl = _split2(b)
    return _dot(ah, bh, dims) + (_dot(ah, bl, dims) + _dot(al, bh, dims))


def _dot_xl(a, b_exact, dims=NN):
    h, m, l = _split3(a)
    return _dot(h, b_exact, dims) + (_dot(m, b_exact, dims) + _dot(l, b_exact, dims))


def _dot_xr(a_exact, b, dims=NN):
    h, m, l = _split3(b)
    return _dot(a_exact, h, dims) + (_dot(a_exact, m, dims) + _dot(a_exact, l, dims))


def _iota(shape, dim):
    return lax.broadcasted_iota(jnp.int32, shape, dim)


def _silu(x):
    return x * jax.nn.sigmoid(x)


def _softplus(x):
    return jnp.maximum(x, 0.0) + jnp.log(1.0 + jnp.exp(-jnp.abs(x)))


def _params(*sem):
    return pltpu.CompilerParams(dimension_semantics=sem, vmem_limit_bytes=VMEM_LIMIT)


def _inv_unit_lower(a, nilpotency):
    n = a.shape[0]
    eye = (_iota((n, n), 0) == _iota((n, n), 1)).astype(F32)
    p = -a
    t = eye + p
    for _ in range(int(math.log2(nilpotency)) - 1):
        p = _dot3(p, p)
        t = t + _dot3(t, p)
    return t


def _rmsnorm_kernel(x_ref, g_ref, o_ref, *, eps):
    x = x_ref[...]
    y = x * lax.rsqrt(jnp.mean(x * x, axis=-1, keepdims=True) + eps)
    o_ref[...] = (y * g_ref[...]).astype(o_ref.dtype)


def rmsnorm(x, g, out_dtype, *, eps=NORM_EPS, tm=512):
    m, d = x.shape
    tm = min(tm, m)
    return pl.pallas_call(
        functools.partial(_rmsnorm_kernel, eps=eps),
        out_shape=jax.ShapeDtypeStruct((m, d), out_dtype),
        grid=(m // tm,),
        in_specs=[pl.BlockSpec((tm, d), lambda i: (i, 0)), pl.BlockSpec((1, d), lambda i: (0, 0))],
        out_specs=pl.BlockSpec((tm, d), lambda i: (i, 0)),
        compiler_params=_params("parallel"),
        name="rmsnorm",
    )(x, g.reshape(1, d))


def _mm_kernel(a_ref, w_ref, *rest, epilogue):
    o_ref = rest[-1]
    acc = _dot(a_ref[...], w_ref[...])
    if epilogue is not None:
        acc = epilogue(acc, *rest[:-1])
    o_ref[...] = acc.astype(o_ref.dtype)


def matmul(a, w, out_dtype, *, tm=1024, tn=1024, extra=(), extra_specs=(), epilogue=None, name="matmul"):
    m, k = a.shape
    n = w.shape[1]
    tm, tn = min(tm, m), min(tn, n)
    return pl.pallas_call(
        functools.partial(_mm_kernel, epilogue=epilogue),
        out_shape=jax.ShapeDtypeStruct((m, n), out_dtype),
        grid=(n // tn, m // tm),
        in_specs=[pl.BlockSpec((tm, k), lambda j, i: (i, 0)),
                  pl.BlockSpec((k, tn), lambda j, i: (0, j))] + list(extra_specs),
        out_specs=pl.BlockSpec((tm, tn), lambda j, i: (i, j)),
        compiler_params=_params("parallel", "parallel"),
        name=name,
    )(a, w, *extra)


def matmul_residual(a, w, res, *, tm=1024, tn=1024, name="matmul_residual"):
    tm, tn = min(tm, a.shape[0]), min(tn, w.shape[1])
    return matmul(a, w, F32, tm=tm, tn=tn, extra=(res,),
                  extra_specs=(pl.BlockSpec((tm, tn), lambda j, i: (i, j)),),
                  epilogue=lambda acc, r_ref: r_ref[...] + acc, name=name)


def _ple_kernel(h_ref, wg_ref, p_ref, wp_ref, x_ref, o_ref):
    gate = jax.nn.sigmoid(_dot(h_ref[...], wg_ref[...]))
    proj = _dot(p_ref[...].astype(BF16), wp_ref[...])
    o_ref[...] = x_ref[...] + gate * proj


def per_layer_embedding(x, hn, w_gate, p, w_proj, *, tm=1024, tn=1024):
    m, d = x.shape
    pd = p.shape[1]
    tm, tn = min(tm, m), min(tn, d)
    return pl.pallas_call(
        _ple_kernel,
        out_shape=jax.ShapeDtypeStruct((m, d), F32),
        grid=(d // tn, m // tm),
        in_specs=[pl.BlockSpec((tm, d), lambda j, i: (i, 0)),
                  pl.BlockSpec((d, tn), lambda j, i: (0, j)),
                  pl.BlockSpec((tm, pd), lambda j, i: (i, 0)),
                  pl.BlockSpec((pd, tn), lambda j, i: (0, j)),
                  pl.BlockSpec((tm, tn), lambda j, i: (i, j))],
        out_specs=pl.BlockSpec((tm, tn), lambda j, i: (i, j)),
        compiler_params=_params("parallel", "parallel"),
        name="per_layer_embedding",
    )(hn, w_gate, p, w_proj, x)


def _rope_table_kernel(pos_ref, freq_ref, cos_ref, sin_lo_ref, sin_hi_ref):
    half = ROT_DIM // 2
    ang = pos_ref[...].astype(F32) * freq_ref[...]
    lane = _iota(ang.shape, 1)
    c, s = jnp.cos(ang), jnp.sin(ang)
    cos_ref[...] = jnp.where(lane < ROT_DIM, c, 1.0)
    sin_lo_ref[...] = jnp.where(lane < half, -s, 0.0)
    sin_hi_ref[...] = jnp.where((lane >= half) & (lane < ROT_DIM), s, 0.0)


def rope_tables(positions, *, tm=1024):
    m = positions.size
    tm = min(tm, m)
    inv_freq = ROPE_THETA ** (-jnp.arange(0, ROT_DIM, 2, dtype=F32) / ROT_DIM)
    freq_row = jnp.concatenate([inv_freq, inv_freq, jnp.zeros((LANES - ROT_DIM,), F32)]).reshape(1, LANES)
    shape = jax.ShapeDtypeStruct((m, LANES), F32)
    spec = pl.BlockSpec((tm, LANES), lambda i: (i, 0))
    return pl.pallas_call(
        _rope_table_kernel,
        out_shape=(shape, shape, shape),
        grid=(m // tm,),
        in_specs=[pl.BlockSpec((tm, 1), lambda i: (i, 0)), pl.BlockSpec((1, LANES), lambda i: (0, 0))],
        out_specs=(spec, spec, spec),
        compiler_params=_params("parallel"),
        name="rope_tables",
    )(positions.reshape(m, 1), freq_row)


def _attn_qk_kernel(a_ref, w_ref, cos_ref, sin_lo_ref, sin_hi_ref, o_ref, *, n_q_blocks, sub):
    j = pl.program_id(0)
    half = ROT_DIM // 2
    scale = jnp.where(j < n_q_blocks, A_HEAD_DIM ** -0.5, 1.0).astype(F32)
    cos, sin_lo, sin_hi = cos_ref[...], sin_lo_ref[...], sin_hi_ref[...]
    a = a_ref[...]

    def epilogue(acc, c0):
        for g in range(acc.shape[1] // LANES):
            x = acc[:, g * LANES:(g + 1) * LANES]
            y = x * cos + pltpu.roll(x, LANES - half, 1) * sin_lo + pltpu.roll(x, half, 1) * sin_hi
            o_ref[:, c0 + g * LANES:c0 + (g + 1) * LANES] = (y * scale).astype(o_ref.dtype)

    starts = list(range(0, w_ref.shape[1], sub))
    acc = _dot(a, w_ref[:, starts[0]:starts[0] + sub])
    for c, c0 in enumerate(starts):
        nxt = _dot(a, w_ref[:, starts[c + 1]:starts[c + 1] + sub]) if c + 1 < len(starts) else None
        epilogue(acc, c0)
        acc = nxt


def attn_qk_proj(hn, w_qk, tables, *, tm=1024, tn=2048, sub=256):
    m, k = hn.shape
    n = w_qk.shape[1]
    tm, tn = min(tm, m), min(tn, n // 2)
    tspec = pl.BlockSpec((tm, LANES), lambda j, i: (i, 0))
    return pl.pallas_call(
        functools.partial(_attn_qk_kernel, n_q_blocks=n // 2 // tn, sub=min(sub, tn)),
        out_shape=jax.ShapeDtypeStruct((m, n), BF16),
        grid=(n // tn, m // tm),
        in_specs=[pl.BlockSpec((tm, k), lambda j, i: (i, 0)),
                  pl.BlockSpec((k, tn), lambda j, i: (0, j)), tspec, tspec, tspec],
        out_specs=pl.BlockSpec((tm, tn), lambda j, i: (i, j)),
        compiler_params=_params("parallel", "parallel"),
        name="attn_qk_proj",
    )(hn, w_qk, *tables)


def _attn_vt_kernel(w_ref, a_ref, o_ref):
    o_ref[...] = _dot(w_ref[...], a_ref[...], NT).astype(o_ref.dtype)


def attn_v_proj_t(hn, w_t, batch, seq, blk, *, tn=1024):
    m, k = hn.shape
    n = w_t.shape[0]
    tn = min(tn, n)
    nk = seq // blk
    return pl.pallas_call(
        _attn_vt_kernel,
        out_shape=jax.ShapeDtypeStruct((batch, nk, n, blk), BF16),
        grid=(n // tn, m // blk),
        in_specs=[pl.BlockSpec((tn, k), lambda j, i: (j, 0)),
                  pl.BlockSpec((blk, k), lambda j, i: (i, 0))],
        out_specs=pl.BlockSpec((None, None, tn, blk), lambda j, i: (i // nk, i % nk, j, 0)),
        compiler_params=_params("parallel", "parallel"),
        name="attn_v_proj_t",
    )(w_t, hn)


def _diff_attn_kernel(lam_ref, q_ref, k_ref, vt_ref, z_ref, g_ref, o_ref, m_ref, acc_ref, s_ref,
                      *, bq, bk, lam_init):
    i = pl.program_id(2)
    lam = lam_ref[...]
    lam_full = (jnp.exp(jnp.sum(lam[0:1] * lam[1:2], axis=-1, keepdims=True))
                - jnp.exp(jnp.sum(lam[2:3] * lam[3:4], axis=-1, keepdims=True)) + lam_init)
    m_ref[...] = jnp.full(m_ref.shape, -jnp.inf, F32)
    acc_ref[...] = jnp.zeros(acc_ref.shape, F32)
    q = q_ref[...]
    d = A_HEAD_DIM
    w = A_V_DIM
    ones = jnp.ones((ONES_ROWS, bk), BF16)

    def scores(j, slot, q_lo=0):
        start = pl.multiple_of(j * bk, bk)
        kb = k_ref[pl.ds(start, bk), :]
        for c in range(2):
            s_ref[slot, c, :, q_lo:] = _dot(kb[:, c * d:(c + 1) * d], q[q_lo:, c * d:(c + 1) * d], NT)

    def absorb(j, slot, masked, q_lo=0):
        vt = jnp.concatenate([vt_ref[j], ones], axis=0)
        for c in range(2):
            s = s_ref[slot, c, :, q_lo:]
            if masked:
                kv_pos = j * bk + _iota(s.shape, 0)
                q_pos = i * bq + q_lo + _iota(s.shape, 1)
                s = jnp.where(kv_pos <= q_pos, s, -jnp.inf)
            m_prev = m_ref[c, :, q_lo:]
            m_new = jnp.maximum(m_prev, jnp.max(s, axis=0, keepdims=True))
            alpha = jnp.exp(m_prev - m_new)
            p = jnp.exp(s - m_new)
            acc_ref[c, :, q_lo:] = alpha * acc_ref[c, :, q_lo:] + _dot(vt, p.astype(BF16))
            m_ref[c, :, q_lo:] = m_new

    r = bq // bk
    first_masked = r * i
    scores(0, 0)

    def body(jj, carry):
        scores(2 * jj + 1, 1)
        absorb(2 * jj, 0, False)
        scores(2 * jj + 2, 0)
        absorb(2 * jj + 1, 1, False)
        return carry

    lax.fori_loop(0, first_masked // 2, body, 0)
    for t in range(0, r, 2):
        scores(first_masked + t + 1, 1, (t + 1) * bk)
        absorb(first_masked + t, 0, True, t * bk)
        if t + 2 < r:
            scores(first_masked + t + 2, 0, (t + 2) * bk)
        absorb(first_masked + t + 1, 1, True, (t + 1) * bk)

    o = (acc_ref[0, :w, :] / acc_ref[0, w:w + 1, :]
         - lam_full * (acc_ref[1, :w, :] / acc_ref[1, w:w + 1, :]))
    o = o * lax.rsqrt(jnp.mean(o * o, axis=0, keepdims=True) + SUBLN_EPS) * g_ref[...]
    o = (o * (1.0 - lam_init)).T
    o_ref[...] = (o * _silu(z_ref[...].astype(F32))).astype(o_ref.dtype)


def diff_attention_core(qk, z, vt, lam, subln_g, batch, seq, heads, lam_init):
    m = qk.shape[0]
    bk = vt.shape[-1]
    bq = ATTN_Q_BLOCKS * bk
    nq = seq // bq
    nk = seq // bk
    w = A_V_DIM
    return pl.pallas_call(
        functools.partial(_diff_attn_kernel, bq=bq, bk=bk, lam_init=lam_init),
        out_shape=jax.ShapeDtypeStruct((m, heads * w), BF16),
        grid=(batch, heads, nq),
        in_specs=[pl.BlockSpec((4, A_HEAD_DIM), lambda b, h, i: (0, 0)),
                  pl.BlockSpec((bq, w), lambda b, h, i: (b * nq + i, h)),
                  pl.BlockSpec((seq, w), lambda b, h, i: (b, heads + h)),
                  pl.BlockSpec((None, nk, w, bk), lambda b, h, i: (b, 0, h, 0)),
                  pl.BlockSpec((bq, w), lambda b, h, i: (b * nq + i, h)),
                  pl.BlockSpec((w, 1), lambda b, h, i: (0, 0))],
        out_specs=pl.BlockSpec((bq, w), lambda b, h, i: (b * nq + i, h)),
        scratch_shapes=[pltpu.VMEM((2, 1, bq), F32), pltpu.VMEM((2, w + ONES_ROWS, bq), F32),
                        pltpu.VMEM((2, 2, bk, bq), F32)],
        compiler_params=_params("parallel", "parallel", "parallel"),
        name="diff_attention",
    )(lam, qk, qk, vt, z, subln_g.reshape(w, 1))


def diff_attention_layer(x, hn, tables, w_in, lam, subln_g, w_out, batch, seq, lam_init):
    d = x.shape[1]
    heads = d // A_V_DIM
    qk_w = heads * 2 * A_HEAD_DIM
    v_w = heads * A_V_DIM
    blk = min(ATTN_BLOCK, seq // ATTN_Q_BLOCKS)
    w_vt = w_in[:, 2 * qk_w:2 * qk_w + v_w].T.astype(BF16)
    qk = attn_qk_proj(hn, w_in[:, :2 * qk_w].astype(BF16), tables)
    z = matmul(hn, w_in[:, 2 * qk_w + v_w:].astype(BF16), BF16, name="attn_gate_proj")
    vt = attn_v_proj_t(hn, w_vt, batch, seq, blk)
    o = diff_attention_core(qk, z, vt, lam, subln_g, batch, seq, heads, lam_init)
    return matmul_residual(o, w_out.astype(BF16), x, name="attn_out_proj")


def _gdn_conv_kernel(a_ref, w_ref, cw_ref, o_ref, tail_ref, *, rows_per_seq, sub):
    i = pl.program_id(1)
    tm = a_ref.shape[0]

    @pl.when((i * tm) % rows_per_seq == 0)
    def _():
        tail_ref[...] = jnp.zeros(tail_ref.shape, F32)

    a = a_ref[...]
    sub_iota = _iota((8, sub), 0)
    last = C_CONV_WIDTH - 1

    def epilogue(acc, cols):
        tail = tail_ref[:, cols]

        def shifted(s):
            xs = pltpu.roll(acc, s, 0)
            head = jnp.where(sub_iota < s, pltpu.roll(tail, s, 0), xs[:8])
            return jnp.concatenate([head, xs[8:]], axis=0)

        cw = cw_ref[:, cols]
        y = shifted(last) * cw[0:1]
        for t in range(1, last):
            y = y + shifted(last - t) * cw[t:t + 1]
        y = y + acc * cw[last:last + 1]
        tail_ref[:, cols] = acc[tm - 8:]
        o_ref[:, cols] = _silu(y).astype(o_ref.dtype)

    blocks = [slice(c0, c0 + sub) for c0 in range(0, w_ref.shape[1], sub)]
    acc = _dot(a, w_ref[:, blocks[0]])
    for c, cols in enumerate(blocks):
        nxt = _dot(a, w_ref[:, blocks[c + 1]]) if c + 1 < len(blocks) else None
        epilogue(acc, cols)
        acc = nxt


def gdn_conv_proj(hn, w, conv_w, seq, *, tm=1024, tn=2048, sub=256):
    m, k = hn.shape
    n = w.shape[1]
    tm, tn = min(tm, seq), min(tn, n)
    return pl.pallas_call(
        functools.partial(_gdn_conv_kernel, rows_per_seq=seq, sub=min(sub, tn)),
        out_shape=jax.ShapeDtypeStruct((m, n), BF16),
        grid=(n // tn, m // tm),
        in_specs=[pl.BlockSpec((tm, k), lambda j, i: (i, 0)),
                  pl.BlockSpec((k, tn), lambda j, i: (0, j)),
                  pl.BlockSpec((C_CONV_WIDTH, tn), lambda j, i: (0, j))],
        out_specs=pl.BlockSpec((tm, tn), lambda j, i: (i, j)),
        scratch_shapes=[pltpu.VMEM((8, tn), F32)],
        compiler_params=_params("arbitrary", "arbitrary"),
        name="gdn_conv_proj",
    )(hn, w, conv_w)


def _pair_masks():
    n = 2 * CHUNK
    r, c = _iota((n, n), 0), _iota((n, n), 1)
    same = (r // CHUNK) == (c // CHUNK)
    return same & (r >= c), same & (r > c), r == c


def _dot_x2l(a, b_exact, dims=NN):
    h, l = _split2(a)
    return _dot(h, b_exact, dims) + _dot(l, b_exact, dims)


def _dot_x2r(a_exact, b, dims=NN):
    h, l = _split2(b)
    return _dot(a_exact, h, dims) + _dot(a_exact, l, dims)


def _inverse_stages(p_ref, t_ref, count, nilpotency):
    for _ in range(int(math.log2(nilpotency)) - 1):
        for i in range(count):
            p_ref[i] = _dot(p_ref[i], p_ref[i]).astype(BF16)
        for i in range(count):
            t = t_ref[i]
            t_ref[i] = t + _dot(t.astype(BF16), p_ref[i])


def _gdn_gates_kernel(a_ref, w_ref, alog_ref, dtb_ref, beta_ref, gc_ref, gct_ref):
    acc = _dot(a_ref[...], w_ref[...])
    beta_ref[...] = jax.nn.sigmoid(acc[:, :LANES])
    g = -jnp.exp(alog_ref[...]) * _softplus(acc[:, LANES:] + dtb_ref[...])
    tri, _, diag = _pair_masks()
    tri_b, eye_b = tri.astype(BF16), diag.astype(BF16)
    n2 = 2 * CHUNK
    for n in range(g.shape[0] // n2):
        rows = slice(n * n2, (n + 1) * n2)
        gc = _dot_xr(tri_b, g[rows])
        gc_ref[rows, :] = gc
        gct_ref[:, rows] = _dot_xr(eye_b, gc, NT)


def gdn_gates(hn, w_ba, alog_row, dtb_row, *, tm=512):
    m, k = hn.shape
    tm = min(tm, m)
    row = pl.BlockSpec((1, LANES), lambda i: (0, 0))
    out = pl.BlockSpec((tm, LANES), lambda i: (i, 0))
    return pl.pallas_call(
        _gdn_gates_kernel,
        out_shape=(jax.ShapeDtypeStruct((m, LANES), F32), jax.ShapeDtypeStruct((m, LANES), F32),
                   jax.ShapeDtypeStruct((LANES, m), F32)),
        grid=(m // tm,),
        in_specs=[pl.BlockSpec((tm, k), lambda i: (i, 0)), pl.BlockSpec((k, 2 * LANES), lambda i: (0, 0)), row, row],
        out_specs=(out, out, pl.BlockSpec((LANES, tm), lambda i: (0, i))),
        compiler_params=_params("parallel"),
        name="gdn_gates",
    )(hn, w_ba, alog_row, dtb_row)


def _gdn_prep_kernel(q_ref, k_ref, v_ref, beta_ref, gc_ref, gct_ref,
                     w_ref, qg_ref, u_ref, att_ref, kdt_ref, dec_ref,
                     g_s, dec_s, kn_s, kb_s, rhs_s, p_s, t_s, qn_s, kdec_s):
    kh = pl.program_id(1)
    c = CHUNK
    n2 = 2 * c
    dk = C_HEAD_DIM
    nv = w_ref.shape[0]
    npairs = q_ref.shape[0] // n2
    tri, strict, diag = _pair_masks()
    eye_f = diag.astype(F32)
    pairs = [slice(n * n2, (n + 1) * n2) for n in range(npairs)]
    lane = _iota((n2, LANES), 1)

    for n, rows in enumerate(pairs):
        for e in range(nv):
            h = kh * nv + e
            gc = jnp.broadcast_to(jnp.sum(jnp.where(lane == h, gc_ref[rows, :], 0.0), axis=-1, keepdims=True),
                                  (n2, dk))
            grp = gct_ref[pl.ds(pl.multiple_of((h // 8) * 8, 8), 8), rows]
            own_row = jnp.sum(jnp.where(_iota(grp.shape, 0) == h % 8, grp, 0.0), axis=0, keepdims=True)
            gc_row = jnp.broadcast_to(own_row, (n2, n2))
            g_s[n * nv + e] = gc
            dec_s[n * nv + e] = jnp.where(tri, jnp.exp(jnp.where(tri, gc - gc_row, 0.0)), 0.0)
    slab = 32
    lane_slab = _iota((slab, LANES), 1)
    for n, rows in enumerate(pairs):
        for r0 in range(0, n2, slab):
            rs = slice(r0, r0 + slab)
            gs = slice(n * n2 + r0, n * n2 + r0 + slab)
            last = c - 1 if r0 < c else n2 - 1
            qf = q_ref[gs, :].astype(F32)
            kf = k_ref[gs, :].astype(F32)
            qn = qf * lax.rsqrt(jnp.sum(qf * qf, axis=-1, keepdims=True) + 1e-6) * (dk ** -0.5)
            kn = kf * lax.rsqrt(jnp.sum(kf * kf, axis=-1, keepdims=True) + 1e-6)
            kn_s[n, rs, :] = kn.astype(BF16)
            qn_s[n, rs, :] = qn.astype(BF16)
            for e in range(nv):
                i = n * nv + e
                gc = g_s[i, rs, :]
                vf = v_ref[gs, e * dk:(e + 1) * dk].astype(F32)
                beta = jnp.sum(jnp.where(lane_slab == kh * nv + e, beta_ref[gs, :], 0.0), axis=-1, keepdims=True)
                egc = jnp.exp(gc)
                kb = kn * beta
                kb_s[i, rs, :] = kb.astype(BF16)
                rhs_s[i, rs, :dk] = (vf * beta).astype(BF16)
                rhs_s[i, rs, dk:] = (kb * egc).astype(BF16)
                kdec_s[i, rs, :] = kn * jnp.exp(g_s[i, last:last + 1, :] - gc)
                qg_ref[e, gs, :] = (qn * egc).astype(BF16)
        for e in range(nv):
            i = n * nv + e
            dec_ref[e, 2 * n:2 * n + 1, :] = jnp.exp(g_s[i, c - 1:c, :])
            dec_ref[e, 2 * n + 1:2 * n + 2, :] = jnp.exp(g_s[i, n2 - 1:n2, :])
    for n, rows in enumerate(pairs):
        raw = _dot(qn_s[n], kn_s[n], NT)
        for e in range(nv):
            att_ref[e, rows, :] = (raw * dec_s[n * nv + e]).astype(BF16)
            kdt_ref[e, :, rows] = kdec_s[n * nv + e].T.astype(BF16)
    for n, rows in enumerate(pairs):
        for e in range(nv):
            i = n * nv + e
            neg_a = jnp.where(strict, -(_dot(kb_s[i], kn_s[n], NT) * dec_s[i]), 0.0)
            p_s[i] = neg_a.astype(BF16)
            t_s[i] = eye_f + neg_a
    _inverse_stages(p_s, t_s, npairs * nv, c)
    for n, rows in enumerate(pairs):
        for e in range(nv):
            sol = _dot(t_s[n * nv + e].astype(BF16), rhs_s[n * nv + e])
            u_ref[e, rows, :] = sol[:, :dk]
            w_ref[e, rows, :] = sol[:, dk:].astype(BF16)


def _gdn_scan_kernel(w_ref, qg_ref, u_ref, att_ref, kdt_ref, dec_ref, z_ref, g_ref, o_ref,
                     s_ref, ms_s, vp_s, *, group):
    c = CHUNK
    dk = C_HEAD_DIM
    zeros = jnp.zeros((c, dk), BF16)

    @pl.when(pl.program_id(2) == 0)
    def _():
        s_ref[...] = jnp.zeros(s_ref.shape, F32)

    for n in range(w_ref.shape[1] // c):
        rows = slice(n * c, (n + 1) * c)
        pair = slice((n // 2) * 2 * c, (n // 2 + 1) * 2 * c)
        for gi in range(group):
            lhs = jnp.concatenate([w_ref[gi, rows, :], qg_ref[gi, rows, :]], axis=0)
            ms_s[gi] = _dot(lhs, s_ref[gi].astype(BF16))
        for gi in range(group):
            v_new = (u_ref[gi, rows, :] - ms_s[gi, :c, :]).astype(BF16)
            vp_s[gi] = jnp.concatenate([v_new, zeros] if n % 2 == 0 else [zeros, v_new], axis=0)
        for gi in range(group):
            s_ref[gi] = s_ref[gi] * dec_ref[gi, n:n + 1, :] + _dot(kdt_ref[gi, :, pair], vp_s[gi])
        for gi in range(group):
            o = ms_s[gi, c:, :] + _dot(att_ref[gi, rows, :], vp_s[gi])
            o = o * lax.rsqrt(jnp.mean(o * o, axis=-1, keepdims=True) + NORM_EPS) * g_ref[...]
            z = z_ref[rows, gi * dk:(gi + 1) * dk].astype(F32)
            o_ref[rows, gi * dk:(gi + 1) * dk] = (o * _silu(z)).astype(o_ref.dtype)


def gated_deltanet_layer(x, hn, w_in, conv_w, a_log, dt_bias, norm_g, w_out, batch, seq):
    m, d = x.shape
    dk = C_HEAD_DIM
    k_heads = d // dk
    v_heads = 2 * k_heads
    conv_ch = 2 * k_heads * dk + v_heads * dk
    main_w = conv_ch + v_heads * dk
    qkv = gdn_conv_proj(hn, w_in[:, :conv_ch].astype(BF16), conv_w, seq)
    z = matmul(hn, w_in[:, conv_ch:main_w].astype(BF16), BF16, name="gdn_gate_proj")
    pad = jnp.zeros((d, LANES - v_heads), F32)
    w_ba = jnp.concatenate([w_in[:, main_w:main_w + v_heads], pad, w_in[:, main_w + v_heads:], pad], axis=1)
    row_pad = jnp.zeros((LANES - v_heads,), F32)
    alog_row = jnp.concatenate([a_log, row_pad]).reshape(1, LANES)
    dtb_row = jnp.concatenate([dt_bias, row_pad]).reshape(1, LANES)
    beta_all, gc_all, gc_t = gdn_gates(hn, w_ba.astype(BF16), alog_row, dtb_row)

    rows = min(PREP_ROWS, seq)
    nr = seq // rows
    ncr = rows // CHUNK
    hv = v_heads
    nv = v_heads // k_heads
    nchain = nv * rows // (2 * CHUNK)
    npair = rows // (2 * CHUNK)
    bh_t = lambda dt, last: jax.ShapeDtypeStruct((batch, hv, seq, last), dt)
    blk4 = lambda last: pl.BlockSpec((None, nv, rows, last), lambda b, h, i: (b, h, i, 0))
    qoff, koff, voff = 0, k_heads, 2 * k_heads // nv
    w_, qg_, u_, att_, kdt_, dec_ = pl.pallas_call(
        _gdn_prep_kernel,
        out_shape=(bh_t(BF16, dk), bh_t(BF16, dk), bh_t(F32, dk), bh_t(BF16, 2 * CHUNK),
                   jax.ShapeDtypeStruct((batch, hv, dk, seq), BF16),
                   jax.ShapeDtypeStruct((batch, hv, seq // CHUNK, dk), F32)),
        grid=(batch, k_heads, nr),
        in_specs=[pl.BlockSpec((rows, dk), lambda b, h, i: (b * nr + i, qoff + h)),
                  pl.BlockSpec((rows, dk), lambda b, h, i: (b * nr + i, koff + h)),
                  pl.BlockSpec((rows, nv * dk), lambda b, h, i: (b * nr + i, voff + h)),
                  pl.BlockSpec((rows, LANES), lambda b, h, i: (b * nr + i, 0)),
                  pl.BlockSpec((rows, LANES), lambda b, h, i: (b * nr + i, 0)),
                  pl.BlockSpec((LANES, rows), lambda b, h, i: (0, b * nr + i))],
        out_specs=(blk4(dk), blk4(dk), blk4(dk), blk4(2 * CHUNK),
                   pl.BlockSpec((None, nv, dk, rows), lambda b, h, i: (b, h, 0, i)),
                   pl.BlockSpec((None, nv, ncr, dk), lambda b, h, i: (b, h, i, 0))),
        scratch_shapes=[pltpu.VMEM((nchain, dk, dk), F32), pltpu.VMEM((nchain, dk, dk), F32),
                        pltpu.VMEM((npair, dk, dk), BF16), pltpu.VMEM((nchain, dk, dk), BF16),
                        pltpu.VMEM((nchain, dk, 2 * dk), BF16), pltpu.VMEM((nchain, dk, dk), BF16),
                        pltpu.VMEM((nchain, dk, dk), F32), pltpu.VMEM((npair, dk, dk), BF16),
                        pltpu.VMEM((nchain, dk, dk), F32)],
        compiler_params=_params("parallel", "parallel", "parallel"),
        name="gdn_prep",
    )(qkv, qkv, qkv, beta_all, gc_all, gc_t)

    rows = min(SCAN_ROWS, seq)
    nr = seq // rows
    ncr = rows // CHUNK
    group = 16
    gblk = lambda last: pl.BlockSpec((None, group, rows, last), lambda b, h, i: (b, h, i, 0))
    o = pl.pallas_call(
        functools.partial(_gdn_scan_kernel, group=group),
        out_shape=jax.ShapeDtypeStruct((m, hv * dk), BF16),
        grid=(batch, hv // group, nr),
        in_specs=[gblk(dk), gblk(dk), gblk(dk), gblk(2 * CHUNK),
                  pl.BlockSpec((None, group, dk, rows), lambda b, h, i: (b, h, 0, i)),
                  pl.BlockSpec((None, group, ncr, dk), lambda b, h, i: (b, h, i, 0)),
                  pl.BlockSpec((rows, group * dk), lambda b, h, i: (b * nr + i, h)),
                  pl.BlockSpec((1, dk), lambda b, h, i: (0, 0))],
        out_specs=pl.BlockSpec((rows, group * dk), lambda b, h, i: (b * nr + i, h)),
        scratch_shapes=[pltpu.VMEM((group, dk, dk), F32), pltpu.VMEM((group, 2 * CHUNK, dk), F32),
                        pltpu.VMEM((group, 2 * CHUNK, dk), BF16)],
        compiler_params=_params("parallel", "parallel", "arbitrary"),
        name="gdn_scan",
    )(w_, qg_, u_, att_, kdt_, dec_, z, norm_g.reshape(1, dk))
    return matmul_residual(o, w_out.astype(BF16), x, tm=512, name="gdn_out_proj")


def _rwkv_mix_kernel(x_ref, g_ref, mu_ref, o_ref, tail_ref, *, rows_per_seq):
    i = pl.program_id(0)
    x = x_ref[...]
    tm = x.shape[0]
    hn = x * lax.rsqrt(jnp.mean(x * x, axis=-1, keepdims=True) + NORM_EPS) * g_ref[...]

    @pl.when((i * tm) % rows_per_seq == 0)
    def _():
        tail_ref[...] = jnp.zeros(tail_ref.shape, F32)

    prev = jnp.where(_iota(hn.shape, 0) == 0, tail_ref[7:8, :], pltpu.roll(hn, 1, 0))
    tail_ref[...] = hn[tm - 8:]
    xx = prev - hn
    for c in range(o_ref.shape[0]):
        o_ref[c] = (hn + xx * mu_ref[c:c + 1, :]).astype(o_ref.dtype)


def rwkv_token_mix(x, norm_g, mu, seq, *, tm=256):
    m, d = x.shape
    tm = min(tm, seq)
    nmix = mu.shape[0]
    return pl.pallas_call(
        functools.partial(_rwkv_mix_kernel, rows_per_seq=seq),
        out_shape=jax.ShapeDtypeStruct((nmix, m, d), BF16),
        grid=(m // tm,),
        in_specs=[pl.BlockSpec((tm, d), lambda i: (i, 0)),
                  pl.BlockSpec((1, d), lambda i: (0, 0)),
                  pl.BlockSpec((nmix, d), lambda i: (0, 0))],
        out_specs=pl.BlockSpec((nmix, tm, d), lambda i: (0, i, 0)),
        scratch_shapes=[pltpu.VMEM((8, d), F32)],
        compiler_params=_params("arbitrary"),
        name="rwkv_token_mix",
    )(x, norm_g.reshape(1, d), mu)


def _grouped_mm_kernel(a_ref, w_ref, o_ref):
    o_ref[...] = _dot(a_ref[...], w_ref[...]).astype(o_ref.dtype)


def grouped_matmul(a, w, out_dtype, *, tm=1024, tn=1024):
    g, k, n = w.shape
    m = a.shape[1]
    tm, tn = min(tm, m), min(tn, n)
    return pl.pallas_call(
        _grouped_mm_kernel,
        out_shape=jax.ShapeDtypeStruct((g, m, n), out_dtype),
        grid=(g, n // tn, m // tm),
        in_specs=[pl.BlockSpec((None, tm, k), lambda c, j, i: (c, i, 0)),
                  pl.BlockSpec((None, k, tn), lambda c, j, i: (c, 0, j))],
        out_specs=pl.BlockSpec((None, tm, tn), lambda c, j, i: (c, i, j)),
        compiler_params=_params("parallel", "parallel", "parallel"),
        name="rwkv_rkvg_proj",
    )(a, w)


def _rwkv_lora_kernel(xw_ref, xa_ref, w1_ref, w2_ref, a1_ref, a2_ref, w0_ref, a0_ref, ld_ref, a_ref):
    hw = jnp.tanh(_dot(xw_ref[...], w1_ref[...])).astype(BF16)
    lw = w0_ref[...] + _dot(hw, w2_ref[...])
    log_w = -_softplus(-lw) - 0.5
    ld_ref[...] = -jnp.exp(log_w)
    ha = _dot(xa_ref[...], a1_ref[...]).astype(BF16)
    a_ref[...] = jax.nn.sigmoid(a0_ref[...] + _dot(ha, a2_ref[...]))


def rwkv_lora(xs, w1, w2, a1, a2, w0, a0, *, tm=512):
    _, m, d = xs.shape
    tm = min(tm, m)
    r = w1.shape[1]
    row = pl.BlockSpec((1, d), lambda i: (0, 0))
    shape = jax.ShapeDtypeStruct((m, d), F32)
    return pl.pallas_call(
        _rwkv_lora_kernel,
        out_shape=(shape, shape),
        grid=(m // tm,),
        in_specs=[pl.BlockSpec((None, tm, d), lambda i: (4, i, 0)),
                  pl.BlockSpec((None, tm, d), lambda i: (5, i, 0)),
                  pl.BlockSpec((d, r), lambda i: (0, 0)), pl.BlockSpec((r, d), lambda i: (0, 0)),
                  pl.BlockSpec((d, r), lambda i: (0, 0)), pl.BlockSpec((r, d), lambda i: (0, 0)),
                  row, row],
        out_specs=(pl.BlockSpec((tm, d), lambda i: (i, 0)), pl.BlockSpec((tm, d), lambda i: (i, 0))),
        compiler_params=_params("parallel"),
        name="rwkv_lora",
    )(xs, xs, w1, w2, a1, a2, w0.reshape(1, d), a0.reshape(1, d))


def _head_sum(x, first_head):
    tot = jnp.sum(x, axis=-1, keepdims=True)
    lo = jnp.sum(jnp.where(first_head, x, 0.0), axis=-1, keepdims=True)
    return jnp.where(first_head, lo, tot - lo)


def _rwkv_prep_kernel(r_ref, k_ref, v_ref, ld_ref, a_ref, kk_ref, ka_ref, rk_ref,
                      wt_ref, rt_ref, u_ref, aro_ref, kbt_ref, vb_ref, dec_ref, bonus_ref,
                      kkt_s, kb_s, akk_s, p_s, t_s, drow_s, *, cps):
    c = CHUNK
    n2 = 2 * c
    n_h = B_HEAD_DIM
    npairs = r_ref.shape[0] // n2
    first_head = _iota((1, LANES), 1) < n_h
    tri, strict, diag = _pair_masks()
    tri_b = tri.astype(BF16)
    eye_b = diag.astype(BF16)
    eye_f = diag.astype(F32)
    upper_half = _iota((n2, LANES), 0) < c
    pairs = [slice(n * n2, (n + 1) * n2) for n in range(npairs)]

    for n, rows in enumerate(pairs):
        r, k, v, a = r_ref[rows, :], k_ref[rows, :], v_ref[rows, :], a_ref[rows, :]
        kk_raw = k * kk_ref[...]
        kk = kk_raw * lax.rsqrt(_head_sum(kk_raw * kk_raw, first_head) + 1e-6)
        k2c = k * (1.0 + (a - 1.0) * ka_ref[...])
        bbc = kk * a
        bonus_ref[rows, :] = _head_sum(r * k2c * rk_ref[...], first_head) * v
        vb_ref[rows, :] = v.astype(BF16)
        ld = ld_ref[rows, :]
        cs = _dot_x2r(tri_b, ld)
        cs_last = jnp.where(upper_half, cs[c - 1:c, :], cs[n2 - 1:n2, :])
        w_inv = jnp.exp(-cs)
        w_end = jnp.exp(cs_last - cs)
        rt_ref[rows, :] = (r * jnp.exp(cs)).astype(BF16)
        kkt_s[n] = (kk * jnp.exp(cs - ld)).astype(BF16)
        kb_s[n] = jnp.concatenate([k2c * w_inv, bbc * w_inv], axis=0).astype(BF16)
        kw, bw = k2c * w_end, -(bbc * w_end)
        for half in range(2):
            hs = slice(half * c, (half + 1) * c)
            kb_end = jnp.concatenate([kw[hs], bw[hs]], axis=0)
            cols = slice((2 * n + half) * n2, (2 * n + half + 1) * n2)
            kbt_ref[:, cols] = kb_end.T.astype(BF16)
        blk, off = divmod(2 * n, cps)
        drow_s[blk, off:off + 1, :] = jnp.exp(cs[c - 1:c, :])
        drow_s[blk, off + 1:off + 2, :] = jnp.exp(cs[n2 - 1:n2, :])
    for blk in range(dec_ref.shape[0] // LANES):
        drow_s[blk, cps:, :] = jnp.zeros((LANES - cps, LANES), F32)
        dec_ref[blk * LANES:(blk + 1) * LANES, :] = _dot_xr(eye_b, drow_s[blk], NT)
    zero_b = jnp.zeros((n2, LANES), BF16)
    strict2 = jnp.concatenate([strict, strict], axis=1)
    tri2 = jnp.concatenate([tri, tri], axis=1)
    sign2 = jnp.where(_iota((n2, 2 * n2), 1) < n2, 1.0, -1.0)
    for n, rows in enumerate(pairs):
        kkt, rt = kkt_s[n], rt_ref[rows, :]
        lhs = jnp.concatenate([jnp.where(first_head, kkt, zero_b), jnp.where(first_head, rt, zero_b),
                               jnp.where(first_head, zero_b, kkt), jnp.where(first_head, zero_b, rt)], axis=0)
        gram = _dot(lhs, kb_s[n], NT)
        for hh in range(2):
            i = 2 * n + hh
            g_kk = jnp.where(strict2, gram[2 * hh * n2:(2 * hh + 1) * n2], 0.0)
            neg_a = -g_kk[:, n2:]
            p_s[i] = neg_a.astype(BF16)
            t_s[i] = eye_f + neg_a
            akk_s[i] = g_kk[:, :n2].astype(BF16)
            g_r = gram[(2 * hh + 1) * n2:(2 * hh + 2) * n2]
            aro_ref[hh, rows, :] = jnp.where(tri2, g_r * sign2, 0.0).astype(BF16)
    _inverse_stages(p_s, t_s, 2 * npairs, c)
    for n, rows in enumerate(pairs):
        av = _dot(jnp.concatenate([akk_s[2 * n], akk_s[2 * n + 1]], axis=0), vb_ref[rows, :])
        p_s[2 * n] = av[:n2].astype(BF16)
        p_s[2 * n + 1] = av[n2:].astype(BF16)
    for n, rows in enumerate(pairs):
        t0, t1 = t_s[2 * n].astype(BF16), t_s[2 * n + 1].astype(BF16)
        wt = _dot(jnp.concatenate([t0, t1], axis=0), kkt_s[n])
        wt_ref[rows, :] = jnp.where(first_head, wt[:n2], wt[n2:]).astype(BF16)
        u_ref[rows, :] = jnp.where(first_head, _dot(t0, p_s[2 * n]), _dot(t1, p_s[2 * n + 1]))


def _rwkv_scan_kernel(wt_ref, rt_ref, u_ref, aro_ref, kbt_ref, vb_ref, dec_ref, bonus_ref, gate_ref,
                      lnw_ref, lnb_ref, o_ref, s_ref, ms_s, xc_s, xp_s, *, group):
    c = CHUNK
    n2 = 2 * c
    n_h = B_HEAD_DIM
    first_head = _iota((1, LANES), 1) < n_h
    same_head = _iota((LANES, LANES), 0) // n_h == _iota((LANES, LANES), 1) // n_h
    zeros = jnp.zeros((c, LANES), BF16)

    def head_mean(x):
        return _head_sum(x, first_head) * (1.0 / n_h)

    @pl.when(pl.program_id(2) == 0)
    def _():
        s_ref[...] = jnp.zeros(s_ref.shape, F32)

    for n in range(u_ref.shape[1] // c):
        rows = slice(n * c, (n + 1) * c)
        for gi in range(group):
            lhs = jnp.concatenate([wt_ref[gi, rows, :], rt_ref[gi, rows, :]], axis=0)
            ms_s[gi] = _dot(lhs, s_ref[gi].astype(BF16))
        for gi in range(group):
            sa = (u_ref[gi, rows, :] + ms_s[gi, :c, :]).astype(BF16)
            vb = vb_ref[gi, rows, :]
            xc_s[gi] = jnp.concatenate([vb, sa], axis=0)
            xp_s[gi] = jnp.concatenate([vb, zeros, sa, zeros] if n % 2 == 0 else [zeros, vb, zeros, sa], axis=0)
        for gi in range(group):
            upd = _dot(kbt_ref[gi, :, n * n2:(n + 1) * n2], xc_s[gi])
            dec_col = jnp.broadcast_to(dec_ref[gi, :, n:n + 1], (LANES, LANES))
            s_ref[gi] = s_ref[gi] * dec_col + jnp.where(same_head, upd, 0.0)
        for gi in range(group):
            both = _dot(jnp.concatenate([aro_ref[gi, 0, rows, :], aro_ref[gi, 1, rows, :]], axis=0), xp_s[gi])
            o = ms_s[gi, c:, :] + jnp.where(first_head, both[:c], both[c:])
            mean = head_mean(o)
            dlt = o - mean
            var = head_mean(dlt * dlt)
            cols = slice(gi * LANES, (gi + 1) * LANES)
            y = dlt * lax.rsqrt(var + B_GN_EPS) * lnw_ref[:, cols] + lnb_ref[:, cols]
            y = y + bonus_ref[rows, cols]
            o_ref[rows, cols] = (y * _silu(gate_ref[rows, cols])).astype(o_ref.dtype)


def rwkv7_layer(x, norm_g, mu, w_rkvg, w0, w_w1, w_w2, a0, w_a1, w_a2, k_k, k_a, r_k, ln_w, ln_b, w_out,
                batch, seq):
    m, d = x.shape
    heads = d // B_HEAD_DIM
    pairs = heads // 2
    order = jnp.array([0, 2, 3, 5, 1, 4])
    xs = rwkv_token_mix(x, norm_g, mu[order], seq)
    rkvg = grouped_matmul(xs, w_rkvg.astype(BF16), F32)
    lora = w_w1.shape[1]
    padc = lambda w: jnp.pad(w, ((0, 0), (0, LORA_PAD - lora))).astype(BF16)
    padr = lambda w: jnp.pad(w, ((0, LORA_PAD - lora), (0, 0))).astype(BF16)
    ld, a = rwkv_lora(xs, padc(w_w1), padr(w_w2), padc(w_a1), padr(w_a2), w0, a0)

    srows = min(SCAN_ROWS, seq)
    rows = min(PREP_ROWS, seq)
    nr = seq // rows
    npair = rows // (2 * CHUNK)
    col = lambda g: pl.BlockSpec((None, rows, LANES), lambda b, p, i, g=g: (g, b * nr + i, p))
    flat = pl.BlockSpec((rows, LANES), lambda b, p, i: (b * nr + i, p))
    prow = pl.BlockSpec((1, LANES), lambda b, p, i: (0, p))
    bp = lambda rws, last, dt: jax.ShapeDtypeStruct((batch, pairs, rws, last), dt)
    pblk = lambda rws, last: pl.BlockSpec((None, None, rws, last), lambda b, p, i: (b, p, i, 0))
    wt_, rt_, u_, aro_, kbt_, vb_, dec_, bonus = pl.pallas_call(
        functools.partial(_rwkv_prep_kernel, cps=srows // CHUNK),
        out_shape=(bp(seq, LANES, BF16), bp(seq, LANES, BF16), bp(seq, LANES, F32),
                   jax.ShapeDtypeStruct((batch, pairs, 2, seq, 4 * CHUNK), BF16),
                   bp(LANES, 2 * seq, BF16), bp(seq, LANES, BF16), bp(seq // srows * LANES, LANES, F32),
                   jax.ShapeDtypeStruct((m, d), F32)),
        grid=(batch, pairs, nr),
        in_specs=[col(0), col(1), col(2), flat, flat, prow, prow, prow],
        out_specs=(pblk(rows, LANES), pblk(rows, LANES), pblk(rows, LANES),
                   pl.BlockSpec((None, None, 2, rows, 4 * CHUNK), lambda b, p, i: (b, p, 0, i, 0)),
                   pl.BlockSpec((None, None, LANES, 2 * rows), lambda b, p, i: (b, p, 0, i)),
                   pblk(rows, LANES), pblk(rows // srows * LANES, LANES), flat),
        scratch_shapes=[pltpu.VMEM((npair, LANES, LANES), BF16), pltpu.VMEM((npair, 2 * LANES, LANES), BF16),
                        pltpu.VMEM((2 * npair, LANES, LANES), BF16), pltpu.VMEM((2 * npair, LANES, LANES), BF16),
                        pltpu.VMEM((2 * npair, LANES, LANES), F32),
                        pltpu.VMEM((rows // srows, LANES, LANES), F32)],
        compiler_params=_params("parallel", "parallel", "parallel"),
        name="rwkv_prep",
    )(rkvg, rkvg, rkvg, ld, a, k_k.reshape(1, d), k_a.reshape(1, d), r_k.reshape(1, d))

    rows = srows
    nr = seq // rows
    group = 8
    gw = group * LANES
    gblk = lambda rws, last: pl.BlockSpec((None, group, rws, last), lambda b, p, i: (b, p, i, 0))
    gflat = pl.BlockSpec((rows, gw), lambda b, p, i: (b * nr + i, p))
    grow = pl.BlockSpec((1, gw), lambda b, p, i: (0, p))
    o = pl.pallas_call(
        functools.partial(_rwkv_scan_kernel, group=group),
        out_shape=jax.ShapeDtypeStruct((m, d), BF16),
        grid=(batch, pairs // group, nr),
        in_specs=[gblk(rows, LANES), gblk(rows, LANES), gblk(rows, LANES),
                  pl.BlockSpec((None, group, 2, rows, 4 * CHUNK), lambda b, p, i: (b, p, 0, i, 0)),
                  pl.BlockSpec((None, group, LANES, 2 * rows), lambda b, p, i: (b, p, 0, i)),
                  gblk(rows, LANES), gblk(LANES, LANES), gflat,
                  pl.BlockSpec((None, rows, gw), lambda b, p, i: (3, b * nr + i, p)),
                  grow, grow],
        out_specs=gflat,
        scratch_shapes=[pltpu.VMEM((group, LANES, LANES), F32), pltpu.VMEM((group, 2 * CHUNK, LANES), F32),
                        pltpu.VMEM((group, 2 * CHUNK, LANES), BF16), pltpu.VMEM((group, 4 * CHUNK, LANES), BF16)],
        compiler_params=_params("parallel", "parallel", "arbitrary"),
        name="rwkv_scan",
    )(wt_, rt_, u_, aro_, kbt_, vb_, dec_, bonus, rkvg, ln_w.reshape(1, d), ln_b.reshape(1, d))
    return matmul_residual(o, w_out.astype(BF16), x, name="rwkv_out_proj")


def kernel(x, p, positions, norm_g, pe_norm_g, pe_w_gate, pe_w_proj, final_norm_g, a_w_in, a_lam, a_subln_g, a_w_out, b_mu, b_w_rkvg, b_w0, b_w_w1, b_w_w2, b_a0, b_w_a1, b_w_a2, b_k_k, b_k_a, b_r_k, b_ln_w, b_ln_b, b_w_out, c_w_in, c_conv_w, c_A_log, c_dt_bias, c_norm_g, c_w_out):
    batch, seq, d = x.shape
    depth = p.shape[0]
    m = batch * seq
    xf = x.reshape(m, d)
    tables = rope_tables(positions)
    for i in range(depth):
        kind = i % N_MIXERS
        j = i // N_MIXERS
        if kind == 0:
            lam_init = 0.8 - 0.6 * math.exp(-0.3 * i)
            hn = rmsnorm(xf, norm_g[i], BF16)
            xf = diff_attention_layer(xf, hn, tables, a_w_in[j], a_lam[j], a_subln_g[j], a_w_out[j],
                                      batch, seq, lam_init)
        elif kind == 1:
            xf = rwkv7_layer(xf, norm_g[i], b_mu[j], b_w_rkvg[j], b_w0[j], b_w_w1[j], b_w_w2[j], b_a0[j],
                             b_w_a1[j], b_w_a2[j], b_k_k[j], b_k_a[j], b_r_k[j], b_ln_w[j], b_ln_b[j],
                             b_w_out[j], batch, seq)
        else:
            hn = rmsnorm(xf, norm_g[i], BF16)
            xf = gated_deltanet_layer(xf, hn, c_w_in[j], c_conv_w[j], c_A_log[j], c_dt_bias[j], c_norm_g[j],
                                      c_w_out[j], batch, seq)
        hn2 = rmsnorm(xf, pe_norm_g[i], BF16)
        xf = per_layer_embedding(xf, hn2, pe_w_gate[i].astype(BF16), p[i].reshape(m, -1),
                                 pe_w_proj[i].astype(BF16))
    return rmsnorm(xf, final_norm_g, F32).reshape(batch, seq, d)
```

```python
import functools
import math

import jax
import jax.numpy as jnp
from jax import lax
from jax.experimental import pallas as pl
from jax.experimental.pallas import tpu as pltpu

F32 = jnp.float32
BF16 = jnp.bfloat16

N_MIXERS = 3
NORM_EPS = 1e-6
LANES = 128
VMEM_LIMIT = 48 * 1024 * 1024

A_HEAD_DIM = 128
A_V_DIM = 2 * A_HEAD_DIM
ROT_DIM = A_HEAD_DIM // 4
ROPE_THETA = 500000.0
SUBLN_EPS = 1e-5
ATTN_BLOCK = 256
ATTN_Q_BLOCKS = 2
ONES_ROWS = 16

B_HEAD_DIM = 64
B_GN_EPS = 64e-5
LORA_PAD = 128

C_HEAD_DIM = 128
C_CONV_WIDTH = 4
CHUNK = 64
SCAN_ROWS = 512
PREP_ROWS = 1024
SLAB_ROWS = 32
GDN_SCAN_HEADS = 16
RWKV_SCAN_PAIRS = 8

MM_ROWS = 1024
MM_COLS = 1024
ROW_TILE = 512
EPI_COLS = 2048
EPI_SUB = 256

NN = (((1,), (0,)), ((), ()))
NT = (((1,), (1,)), ((), ()))


def _dot(a, b, dims=NN):
    return lax.dot_general(a, b, dims, preferred_element_type=F32)


def _split2(x):
    hi = x.astype(BF16)
    lo = (x - hi.astype(F32)).astype(BF16)
    return hi, lo


def _split3(x):
    hi = x.astype(BF16)
    r = x - hi.astype(F32)
    mid = r.astype(BF16)
    lo = (r - mid.astype(F32)).astype(BF16)
    return hi, mid, lo


def _dot_xr(a_exact, b, dims=NN):
    h, m, l = _split3(b)
    return _dot(a_exact, h, dims) + (_dot(a_exact, m, dims) + _dot(a_exact, l, dims))


def _dot_x2r(a_exact, b, dims=NN):
    h, l = _split2(b)
    return _dot(a_exact, h, dims) + _dot(a_exact, l, dims)


def _iota(shape, dim):
    return lax.broadcasted_iota(jnp.int32, shape, dim)


def _silu(x):
    return x * jax.nn.sigmoid(x)


def _softplus(x):
    return jnp.maximum(x, 0.0) + jnp.log(1.0 + jnp.exp(-jnp.abs(x)))


def _params(*sem):
    return pltpu.CompilerParams(dimension_semantics=sem, vmem_limit_bytes=VMEM_LIMIT)


def _rmsnorm_kernel(x_ref, g_ref, o_ref, *, eps):
    x = x_ref[...]
    y = x * lax.rsqrt(jnp.mean(x * x, axis=-1, keepdims=True) + eps)
    o_ref[...] = (y * g_ref[...]).astype(o_ref.dtype)


def rmsnorm(x, g, out_dtype, *, eps=NORM_EPS, tm=ROW_TILE):
    m, d = x.shape
    tm = min(tm, m)
    return pl.pallas_call(
        functools.partial(_rmsnorm_kernel, eps=eps),
        out_shape=jax.ShapeDtypeStruct((m, d), out_dtype),
        grid=(m // tm,),
        in_specs=[pl.BlockSpec((tm, d), lambda i: (i, 0)), pl.BlockSpec((1, d), lambda i: (0, 0))],
        out_specs=pl.BlockSpec((tm, d), lambda i: (i, 0)),
        compiler_params=_params("parallel"),
        name="rmsnorm",
    )(x, g.reshape(1, d))


def _mm_kernel(a_ref, w_ref, *rest, epilogue):
    o_ref = rest[-1]
    acc = _dot(a_ref[...], w_ref[...])
    if epilogue is not None:
        acc = epilogue(acc, *rest[:-1])
    o_ref[...] = acc.astype(o_ref.dtype)


def matmul(a, w, out_dtype, *, tm=MM_ROWS, tn=MM_COLS, extra=(), extra_specs=(), epilogue=None, name="matmul"):
    m, k = a.shape
    n = w.shape[1]
    tm, tn = min(tm, m), min(tn, n)
    return pl.pallas_call(
        functools.partial(_mm_kernel, epilogue=epilogue),
        out_shape=jax.ShapeDtypeStruct((m, n), out_dtype),
        grid=(n // tn, m // tm),
        in_specs=[pl.BlockSpec((tm, k), lambda j, i: (i, 0)),
                  pl.BlockSpec((k, tn), lambda j, i: (0, j))] + list(extra_specs),
        out_specs=pl.BlockSpec((tm, tn), lambda j, i: (i, j)),
        compiler_params=_params("parallel", "parallel"),
        name=name,
    )(a, w, *extra)


def matmul_residual(a, w, res, *, tm=MM_ROWS, tn=MM_COLS, name="matmul_residual"):
    tm, tn = min(tm, a.shape[0]), min(tn, w.shape[1])
    return matmul(a, w, F32, tm=tm, tn=tn, extra=(res,),
                  extra_specs=(pl.BlockSpec((tm, tn), lambda j, i: (i, j)),),
                  epilogue=lambda acc, r_ref: r_ref[...] + acc, name=name)


def _ple_kernel(h_ref, wg_ref, p_ref, wp_ref, x_ref, o_ref):
    gate = jax.nn.sigmoid(_dot(h_ref[...], wg_ref[...]))
    proj = _dot(p_ref[...].astype(BF16), wp_ref[...])
    o_ref[...] = x_ref[...] + gate * proj


def per_layer_embedding(x, hn, w_gate, p, w_proj, *, tm=MM_ROWS, tn=MM_COLS):
    m, d = x.shape
    pd = p.shape[1]
    tm, tn = min(tm, m), min(tn, d)
    return pl.pallas_call(
        _ple_kernel,
        out_shape=jax.ShapeDtypeStruct((m, d), F32),
        grid=(d // tn, m // tm),
        in_specs=[pl.BlockSpec((tm, d), lambda j, i: (i, 0)),
                  pl.BlockSpec((d, tn), lambda j, i: (0, j)),
                  pl.BlockSpec((tm, pd), lambda j, i: (i, 0)),
                  pl.BlockSpec((pd, tn), lambda j, i: (0, j)),
                  pl.BlockSpec((tm, tn), lambda j, i: (i, j))],
        out_specs=pl.BlockSpec((tm, tn), lambda j, i: (i, j)),
        compiler_params=_params("parallel", "parallel"),
        name="per_layer_embedding",
    )(hn, w_gate, p, w_proj, x)


def _rope_table_kernel(pos_ref, freq_ref, cos_ref, sin_lo_ref, sin_hi_ref):
    half = ROT_DIM // 2
    ang = pos_ref[...].astype(F32) * freq_ref[...]
    lane = _iota(ang.shape, 1)
    c, s = jnp.cos(ang), jnp.sin(ang)
    cos_ref[...] = jnp.where(lane < ROT_DIM, c, 1.0)
    sin_lo_ref[...] = jnp.where(lane < half, -s, 0.0)
    sin_hi_ref[...] = jnp.where((lane >= half) & (lane < ROT_DIM), s, 0.0)


def rope_tables(positions, *, tm=MM_ROWS):
    m = positions.size
    tm = min(tm, m)
    inv_freq = ROPE_THETA ** (-jnp.arange(0, ROT_DIM, 2, dtype=F32) / ROT_DIM)
    freq_row = jnp.concatenate([inv_freq, inv_freq, jnp.zeros((LANES - ROT_DIM,), F32)]).reshape(1, LANES)
    shape = jax.ShapeDtypeStruct((m, LANES), F32)
    spec = pl.BlockSpec((tm, LANES), lambda i: (i, 0))
    return pl.pallas_call(
        _rope_table_kernel,
        out_shape=(shape, shape, shape),
        grid=(m // tm,),
        in_specs=[pl.BlockSpec((tm, 1), lambda i: (i, 0)), pl.BlockSpec((1, LANES), lambda i: (0, 0))],
        out_specs=(spec, spec, spec),
        compiler_params=_params("parallel"),
        name="rope_tables",
    )(positions.reshape(m, 1), freq_row)


def _attn_qk_kernel(a_ref, w_ref, cos_ref, sin_lo_ref, sin_hi_ref, o_ref, *, n_q_blocks, sub):
    j = pl.program_id(0)
    half = ROT_DIM // 2
    scale = jnp.where(j < n_q_blocks, A_HEAD_DIM ** -0.5, 1.0).astype(F32)
    cos, sin_lo, sin_hi = cos_ref[...], sin_lo_ref[...], sin_hi_ref[...]
    a = a_ref[...]

    def epilogue(acc, c0):
        for g in range(acc.shape[1] // LANES):
            x = acc[:, g * LANES:(g + 1) * LANES]
            y = x * cos + pltpu.roll(x, LANES - half, 1) * sin_lo + pltpu.roll(x, half, 1) * sin_hi
            o_ref[:, c0 + g * LANES:c0 + (g + 1) * LANES] = (y * scale).astype(o_ref.dtype)

    starts = list(range(0, w_ref.shape[1], sub))
    acc = _dot(a, w_ref[:, starts[0]:starts[0] + sub])
    for c, c0 in enumerate(starts):
        nxt = _dot(a, w_ref[:, starts[c + 1]:starts[c + 1] + sub]) if c + 1 < len(starts) else None
        epilogue(acc, c0)
        acc = nxt


def attn_qk_proj(hn, w_qk, tables, *, tm=MM_ROWS, tn=EPI_COLS, sub=EPI_SUB):
    m, k = hn.shape
    n = w_qk.shape[1]
    tm, tn = min(tm, m), min(tn, n // 2)
    tspec = pl.BlockSpec((tm, LANES), lambda j, i: (i, 0))
    return pl.pallas_call(
        functools.partial(_attn_qk_kernel, n_q_blocks=n // 2 // tn, sub=min(sub, tn)),
        out_shape=jax.ShapeDtypeStruct((m, n), BF16),
        grid=(n // tn, m // tm),
        in_specs=[pl.BlockSpec((tm, k), lambda j, i: (i, 0)),
                  pl.BlockSpec((k, tn), lambda j, i: (0, j)), tspec, tspec, tspec],
        out_specs=pl.BlockSpec((tm, tn), lambda j, i: (i, j)),
        compiler_params=_params("parallel", "parallel"),
        name="attn_qk_proj",
    )(hn, w_qk, *tables)


def _attn_vt_kernel(w_ref, a_ref, o_ref):
    o_ref[...] = _dot(w_ref[...], a_ref[...], NT).astype(o_ref.dtype)


def attn_v_proj_t(hn, w_t, batch, seq, blk, *, tn=MM_COLS):
    m, k = hn.shape
    n = w_t.shape[0]
    tn = min(tn, n)
    nk = seq // blk
    return pl.pallas_call(
        _attn_vt_kernel,
        out_shape=jax.ShapeDtypeStruct((batch, nk, n, blk), BF16),
        grid=(n // tn, m // blk),
        in_specs=[pl.BlockSpec((tn, k), lambda j, i: (j, 0)),
                  pl.BlockSpec((blk, k), lambda j, i: (i, 0))],
        out_specs=pl.BlockSpec((None, None, tn, blk), lambda j, i: (i // nk, i % nk, j, 0)),
        compiler_params=_params("parallel", "parallel"),
        name="attn_v_proj_t",
    )(w_t, hn)


def _diff_attn_kernel(lam_ref, q_ref, k_ref, vt_ref, z_ref, g_ref, o_ref, m_ref, acc_ref, s_ref,
                      *, bq, bk, lam_init):
    i = pl.program_id(2)
    lam = lam_ref[...]
    lam_full = (jnp.exp(jnp.sum(lam[0:1] * lam[1:2], axis=-1, keepdims=True))
                - jnp.exp(jnp.sum(lam[2:3] * lam[3:4], axis=-1, keepdims=True)) + lam_init)
    m_ref[...] = jnp.full(m_ref.shape, -jnp.inf, F32)
    acc_ref[...] = jnp.zeros(acc_ref.shape, F32)
    q = q_ref[...]
    d = A_HEAD_DIM
    w = A_V_DIM
    ones = jnp.ones((ONES_ROWS, bk), BF16)

    def scores(j, slot, q_lo=0):
        start = pl.multiple_of(j * bk, bk)
        kb = k_ref[pl.ds(start, bk), :]
        for c in range(2):
            s_ref[slot, c, :, q_lo:] = _dot(kb[:, c * d:(c + 1) * d], q[q_lo:, c * d:(c + 1) * d], NT)

    def absorb(j, slot, masked, q_lo=0):
        vt = jnp.concatenate([vt_ref[j], ones], axis=0)
        for c in range(2):
            s = s_ref[slot, c, :, q_lo:]
            if masked:
                kv_pos = j * bk + _iota(s.shape, 0)
                q_pos = i * bq + q_lo + _iota(s.shape, 1)
                s = jnp.where(kv_pos <= q_pos, s, -jnp.inf)
            m_prev = m_ref[c, :, q_lo:]
            m_new = jnp.maximum(m_prev, jnp.max(s, axis=0, keepdims=True))
            alpha = jnp.exp(m_prev - m_new)
            p = jnp.exp(s - m_new)
            acc_ref[c, :, q_lo:] = alpha * acc_ref[c, :, q_lo:] + _dot(vt, p.astype(BF16))
            m_ref[c, :, q_lo:] = m_new

    r = bq // bk
    first_masked = r * i
    scores(0, 0)

    def body(jj, carry):
        scores(2 * jj + 1, 1)
        absorb(2 * jj, 0, False)
        scores(2 * jj + 2, 0)
        absorb(2 * jj + 1, 1, False)
        return carry

    lax.fori_loop(0, first_masked // 2, body, 0)
    for t in range(0, r, 2):
        scores(first_masked + t + 1, 1, (t + 1) * bk)
        absorb(first_masked + t, 0, True, t * bk)
        if t + 2 < r:
            scores(first_masked + t + 2, 0, (t + 2) * bk)
        absorb(first_masked + t + 1, 1, True, (t + 1) * bk)

    o = (acc_ref[0, :w, :] / acc_ref[0, w:w + 1, :]
         - lam_full * (acc_ref[1, :w, :] / acc_ref[1, w:w + 1, :]))
    o = o * lax.rsqrt(jnp.mean(o * o, axis=0, keepdims=True) + SUBLN_EPS) * g_ref[...]
    o = (o * (1.0 - lam_init)).T
    o_ref[...] = (o * _silu(z_ref[...].astype(F32))).astype(o_ref.dtype)


def diff_attention_core(qk, z, vt, lam, subln_g, batch, seq, heads, lam_init):
    m = qk.shape[0]
    bk = vt.shape[-1]
    bq = ATTN_Q_BLOCKS * bk
    nq = seq // bq
    nk = seq // bk
    w = A_V_DIM
    return pl.pallas_call(
        functools.partial(_diff_attn_kernel, bq=bq, bk=bk, lam_init=lam_init),
        out_shape=jax.ShapeDtypeStruct((m, heads * w), BF16),
        grid=(batch, heads, nq),
        in_specs=[pl.BlockSpec((4, A_HEAD_DIM), lambda b, h, i: (0, 0)),
                  pl.BlockSpec((bq, w), lambda b, h, i: (b * nq + i, h)),
                  pl.BlockSpec((seq, w), lambda b, h, i: (b, heads + h)),
                  pl.BlockSpec((None, nk, w, bk), lambda b, h, i: (b, 0, h, 0)),
                  pl.BlockSpec((bq, w), lambda b, h, i: (b * nq + i, h)),
                  pl.BlockSpec((w, 1), lambda b, h, i: (0, 0))],
        out_specs=pl.BlockSpec((bq, w), lambda b, h, i: (b * nq + i, h)),
        scratch_shapes=[pltpu.VMEM((2, 1, bq), F32), pltpu.VMEM((2, w + ONES_ROWS, bq), F32),
                        pltpu.VMEM((2, 2, bk, bq), F32)],
        compiler_params=_params("parallel", "parallel", "parallel"),
        name="diff_attention",
    )(lam, qk, qk, vt, z, subln_g.reshape(w, 1))


def diff_attention_layer(x, hn, tables, w_in, lam, subln_g, w_out, batch, seq, lam_init):
    d = x.shape[1]
    heads = d // A_V_DIM
    qk_w = heads * 2 * A_HEAD_DIM
    v_w = heads * A_V_DIM
    blk = min(ATTN_BLOCK, seq // ATTN_Q_BLOCKS)
    w_vt = w_in[:, 2 * qk_w:2 * qk_w + v_w].T.astype(BF16)
    qk = attn_qk_proj(hn, w_in[:, :2 * qk_w].astype(BF16), tables)
    z = matmul(hn, w_in[:, 2 * qk_w + v_w:].astype(BF16), BF16, name="attn_gate_proj")
    vt = attn_v_proj_t(hn, w_vt, batch, seq, blk)
    o = diff_attention_core(qk, z, vt, lam, subln_g, batch, seq, heads, lam_init)
    return matmul_residual(o, w_out.astype(BF16), x, name="attn_out_proj")


def _gdn_conv_kernel(a_ref, w_ref, cw_ref, o_ref, tail_ref, *, rows_per_seq, sub):
    i = pl.program_id(1)
    tm = a_ref.shape[0]

    @pl.when((i * tm) % rows_per_seq == 0)
    def _():
        tail_ref[...] = jnp.zeros(tail_ref.shape, F32)

    a = a_ref[...]
    sub_iota = _iota((8, sub), 0)
    last = C_CONV_WIDTH - 1

    def epilogue(acc, cols):
        tail = tail_ref[:, cols]

        def shifted(s):
            xs = pltpu.roll(acc, s, 0)
            head = jnp.where(sub_iota < s, pltpu.roll(tail, s, 0), xs[:8])
            return jnp.concatenate([head, xs[8:]], axis=0)

        cw = cw_ref[:, cols]
        y = shifted(last) * cw[0:1]
        for t in range(1, last):
            y = y + shifted(last - t) * cw[t:t + 1]
        y = y + acc * cw[last:last + 1]
        tail_ref[:, cols] = acc[tm - 8:]
        o_ref[:, cols] = _silu(y).astype(o_ref.dtype)

    blocks = [slice(c0, c0 + sub) for c0 in range(0, w_ref.shape[1], sub)]
    acc = _dot(a, w_ref[:, blocks[0]])
    for c, cols in enumerate(blocks):
        nxt = _dot(a, w_ref[:, blocks[c + 1]]) if c + 1 < len(blocks) else None
        epilogue(acc, cols)
        acc = nxt


def gdn_conv_proj(hn, w, conv_w, seq, *, tm=MM_ROWS, tn=EPI_COLS, sub=EPI_SUB):
    m, k = hn.shape
    n = w.shape[1]
    tm, tn = min(tm, seq), min(tn, n)
    return pl.pallas_call(
        functools.partial(_gdn_conv_kernel, rows_per_seq=seq, sub=min(sub, tn)),
        out_shape=jax.ShapeDtypeStruct((m, n), BF16),
        grid=(n // tn, m // tm),
        in_specs=[pl.BlockSpec((tm, k), lambda j, i: (i, 0)),
                  pl.BlockSpec((k, tn), lambda j, i: (0, j)),
                  pl.BlockSpec((C_CONV_WIDTH, tn), lambda j, i: (0, j))],
        out_specs=pl.BlockSpec((tm, tn), lambda j, i: (i, j)),
        scratch_shapes=[pltpu.VMEM((8, tn), F32)],
        compiler_params=_params("arbitrary", "arbitrary"),
        name="gdn_conv_proj",
    )(hn, w, conv_w)


def _pair_masks():
    n = 2 * CHUNK
    r, c = _iota((n, n), 0), _iota((n, n), 1)
    same = (r // CHUNK) == (c // CHUNK)
    return same & (r >= c), same & (r > c), r == c


def _inverse_stages(p_ref, t_ref, count, nilpotency):
    for _ in range(int(math.log2(nilpotency)) - 1):
        for i in range(count):
            p_ref[i] = _dot(p_ref[i], p_ref[i]).astype(BF16)
        for i in range(count):
            t = t_ref[i]
            t_ref[i] = t + _dot(t.astype(BF16), p_ref[i])


def _gdn_gates_kernel(a_ref, w_ref, alog_ref, dtb_ref, beta_ref, gc_ref, gct_ref):
    acc = _dot(a_ref[...], w_ref[...])
    beta_ref[...] = jax.nn.sigmoid(acc[:, :LANES])
    g = -jnp.exp(alog_ref[...]) * _softplus(acc[:, LANES:] + dtb_ref[...])
    tri, _, diag = _pair_masks()
    tri_b, eye_b = tri.astype(BF16), diag.astype(BF16)
    n2 = 2 * CHUNK
    for n in range(g.shape[0] // n2):
        rows = slice(n * n2, (n + 1) * n2)
        gc = _dot_xr(tri_b, g[rows])
        gc_ref[rows, :] = gc
        gct_ref[:, rows] = _dot_xr(eye_b, gc, NT)


def gdn_gates(hn, w_ba, alog_row, dtb_row, *, tm=ROW_TILE):
    m, k = hn.shape
    tm = min(tm, m)
    row = pl.BlockSpec((1, LANES), lambda i: (0, 0))
    out = pl.BlockSpec((tm, LANES), lambda i: (i, 0))
    return pl.pallas_call(
        _gdn_gates_kernel,
        out_shape=(jax.ShapeDtypeStruct((m, LANES), F32), jax.ShapeDtypeStruct((m, LANES), F32),
                   jax.ShapeDtypeStruct((LANES, m), F32)),
        grid=(m // tm,),
        in_specs=[pl.BlockSpec((tm, k), lambda i: (i, 0)), pl.BlockSpec((k, 2 * LANES), lambda i: (0, 0)), row, row],
        out_specs=(out, out, pl.BlockSpec((LANES, tm), lambda i: (0, i))),
        compiler_params=_params("parallel"),
        name="gdn_gates",
    )(hn, w_ba, alog_row, dtb_row)


def _gdn_prep_kernel(q_ref, k_ref, v_ref, beta_ref, gc_ref, gct_ref,
                     w_ref, qg_ref, u_ref, att_ref, kdt_ref, dec_ref,
                     g_s, dec_s, kn_s, kb_s, rhs_s, p_s, t_s, qn_s, kdec_s):
    kh = pl.program_id(1)
    c = CHUNK
    n2 = 2 * c
    dk = C_HEAD_DIM
    nv = w_ref.shape[0]
    npairs = q_ref.shape[0] // n2
    tri, strict, diag = _pair_masks()
    eye_f = diag.astype(F32)
    pairs = [slice(n * n2, (n + 1) * n2) for n in range(npairs)]
    lane = _iota((n2, LANES), 1)

    for n, rows in enumerate(pairs):
        for e in range(nv):
            h = kh * nv + e
            gc = jnp.broadcast_to(jnp.sum(jnp.where(lane == h, gc_ref[rows, :], 0.0), axis=-1, keepdims=True),
                                  (n2, dk))
            grp = gct_ref[pl.ds(pl.multiple_of((h // 8) * 8, 8), 8), rows]
            own_row = jnp.sum(jnp.where(_iota(grp.shape, 0) == h % 8, grp, 0.0), axis=0, keepdims=True)
            gc_row = jnp.broadcast_to(own_row, (n2, n2))
            g_s[n * nv + e] = gc
            dec_s[n * nv + e] = jnp.where(tri, jnp.exp(jnp.where(tri, gc - gc_row, 0.0)), 0.0)
    slab = SLAB_ROWS
    lane_slab = _iota((slab, LANES), 1)
    for n, rows in enumerate(pairs):
        for r0 in range(0, n2, slab):
            rs = slice(r0, r0 + slab)
            gs = slice(n * n2 + r0, n * n2 + r0 + slab)
            last = c - 1 if r0 < c else n2 - 1
            qf = q_ref[gs, :].astype(F32)
            kf = k_ref[gs, :].astype(F32)
            qn = qf * lax.rsqrt(jnp.sum(qf * qf, axis=-1, keepdims=True) + 1e-6) * (dk ** -0.5)
            kn = kf * lax.rsqrt(jnp.sum(kf * kf, axis=-1, keepdims=True) + 1e-6)
            kn_s[n, rs, :] = kn.astype(BF16)
            qn_s[n, rs, :] = qn.astype(BF16)
            for e in range(nv):
                i = n * nv + e
                gc = g_s[i, rs, :]
                vf = v_ref[gs, e * dk:(e + 1) * dk].astype(F32)
                beta = jnp.sum(jnp.where(lane_slab == kh * nv + e, beta_ref[gs, :], 0.0), axis=-1, keepdims=True)
                egc = jnp.exp(gc)
                kb = kn * beta
                kb_s[i, rs, :] = kb.astype(BF16)
                rhs_s[i, rs, :dk] = (vf * beta).astype(BF16)
                rhs_s[i, rs, dk:] = (kb * egc).astype(BF16)
                kdec_s[i, rs, :] = kn * jnp.exp(g_s[i, last:last + 1, :] - gc)
                qg_ref[e, gs, :] = (qn * egc).astype(BF16)
        for e in range(nv):
            i = n * nv + e
            dec_ref[e, 2 * n:2 * n + 1, :] = jnp.exp(g_s[i, c - 1:c, :])
            dec_ref[e, 2 * n + 1:2 * n + 2, :] = jnp.exp(g_s[i, n2 - 1:n2, :])
    for n, rows in enumerate(pairs):
        raw = _dot(qn_s[n], kn_s[n], NT)
        for e in range(nv):
            att_ref[e, rows, :] = (raw * dec_s[n * nv + e]).astype(BF16)
            kdt_ref[e, :, rows] = kdec_s[n * nv + e].T.astype(BF16)
    for n, rows in enumerate(pairs):
        for e in range(nv):
            i = n * nv + e
            neg_a = jnp.where(strict, -(_dot(kb_s[i], kn_s[n], NT) * dec_s[i]), 0.0)
            p_s[i] = neg_a.astype(BF16)
            t_s[i] = eye_f + neg_a
    _inverse_stages(p_s, t_s, npairs * nv, c)
    for n, rows in enumerate(pairs):
        for e in range(nv):
            sol = _dot(t_s[n * nv + e].astype(BF16), rhs_s[n * nv + e])
            u_ref[e, rows, :] = sol[:, :dk]
            w_ref[e, rows, :] = sol[:, dk:].astype(BF16)


def _gdn_scan_kernel(w_ref, qg_ref, u_ref, att_ref, kdt_ref, dec_ref, z_ref, g_ref, o_ref,
                     s_ref, ms_s, vp_s, *, group):
    c = CHUNK
    dk = C_HEAD_DIM
    zeros = jnp.zeros((c, dk), BF16)

    @pl.when(pl.program_id(2) == 0)
    def _():
        s_ref[...] = jnp.zeros(s_ref.shape, F32)

    for n in range(w_ref.shape[1] // c):
        rows = slice(n * c, (n + 1) * c)
        pair = slice((n // 2) * 2 * c, (n // 2 + 1) * 2 * c)
        for gi in range(group):
            lhs = jnp.concatenate([w_ref[gi, rows, :], qg_ref[gi, rows, :]], axis=0)
            ms_s[gi] = _dot(lhs, s_ref[gi].astype(BF16))
        for gi in range(group):
            v_new = (u_ref[gi, rows, :] - ms_s[gi, :c, :]).astype(BF16)
            vp_s[gi] = jnp.concatenate([v_new, zeros] if n % 2 == 0 else [zeros, v_new], axis=0)
        for gi in range(group):
            s_ref[gi] = s_ref[gi] * dec_ref[gi, n:n + 1, :] + _dot(kdt_ref[gi, :, pair], vp_s[gi])
        for gi in range(group):
            o = ms_s[gi, c:, :] + _dot(att_ref[gi, rows, :], vp_s[gi])
            o = o * lax.rsqrt(jnp.mean(o * o, axis=-1, keepdims=True) + NORM_EPS) * g_ref[...]
            z = z_ref[rows, gi * dk:(gi + 1) * dk].astype(F32)
            o_ref[rows, gi * dk:(gi + 1) * dk] = (o * _silu(z)).astype(o_ref.dtype)


def gated_deltanet_layer(x, hn, w_in, conv_w, a_log, dt_bias, norm_g, w_out, batch, seq):
    m, d = x.shape
    dk = C_HEAD_DIM
    k_heads = d // dk
    v_heads = 2 * k_heads
    conv_ch = 2 * k_heads * dk + v_heads * dk
    main_w = conv_ch + v_heads * dk
    qkv = gdn_conv_proj(hn, w_in[:, :conv_ch].astype(BF16), conv_w, seq)
    z = matmul(hn, w_in[:, conv_ch:main_w].astype(BF16), BF16, name="gdn_gate_proj")
    pad = jnp.zeros((d, LANES - v_heads), F32)
    w_ba = jnp.concatenate([w_in[:, main_w:main_w + v_heads], pad, w_in[:, main_w + v_heads:], pad], axis=1)
    row_pad = jnp.zeros((LANES - v_heads,), F32)
    alog_row = jnp.concatenate([a_log, row_pad]).reshape(1, LANES)
    dtb_row = jnp.concatenate([dt_bias, row_pad]).reshape(1, LANES)
    beta_all, gc_all, gc_t = gdn_gates(hn, w_ba.astype(BF16), alog_row, dtb_row)

    rows = min(PREP_ROWS, seq)
    nr = seq // rows
    ncr = rows // CHUNK
    hv = v_heads
    nv = v_heads // k_heads
    nchain = nv * rows // (2 * CHUNK)
    npair = rows // (2 * CHUNK)
    bh_t = lambda dt, last: jax.ShapeDtypeStruct((batch, hv, seq, last), dt)
    blk4 = lambda last: pl.BlockSpec((None, nv, rows, last), lambda b, h, i: (b, h, i, 0))
    qoff, koff, voff = 0, k_heads, 2 * k_heads // nv
    w_, qg_, u_, att_, kdt_, dec_ = pl.pallas_call(
        _gdn_prep_kernel,
        out_shape=(bh_t(BF16, dk), bh_t(BF16, dk), bh_t(F32, dk), bh_t(BF16, 2 * CHUNK),
                   jax.ShapeDtypeStruct((batch, hv, dk, seq), BF16),
                   jax.ShapeDtypeStruct((batch, hv, seq // CHUNK, dk), F32)),
        grid=(batch, k_heads, nr),
        in_specs=[pl.BlockSpec((rows, dk), lambda b, h, i: (b * nr + i, qoff + h)),
                  pl.BlockSpec((rows, dk), lambda b, h, i: (b * nr + i, koff + h)),
                  pl.BlockSpec((rows, nv * dk), lambda b, h, i: (b * nr + i, voff + h)),
                  pl.BlockSpec((rows, LANES), lambda b, h, i: (b * nr + i, 0)),
                  pl.BlockSpec((rows, LANES), lambda b, h, i: (b * nr + i, 0)),
                  pl.BlockSpec((LANES, rows), lambda b, h, i: (0, b * nr + i))],
        out_specs=(blk4(dk), blk4(dk), blk4(dk), blk4(2 * CHUNK),
                   pl.BlockSpec((None, nv, dk, rows), lambda b, h, i: (b, h, 0, i)),
                   pl.BlockSpec((None, nv, ncr, dk), lambda b, h, i: (b, h, i, 0))),
        scratch_shapes=[pltpu.VMEM((nchain, dk, dk), F32), pltpu.VMEM((nchain, dk, dk), F32),
                        pltpu.VMEM((npair, dk, dk), BF16), pltpu.VMEM((nchain, dk, dk), BF16),
                        pltpu.VMEM((nchain, dk, 2 * dk), BF16), pltpu.VMEM((nchain, dk, dk), BF16),
                        pltpu.VMEM((nchain, dk, dk), F32), pltpu.VMEM((npair, dk, dk), BF16),
                        pltpu.VMEM((nchain, dk, dk), F32)],
        compiler_params=_params("parallel", "parallel", "parallel"),
        name="gdn_prep",
    )(qkv, qkv, qkv, beta_all, gc_all, gc_t)

    rows = min(SCAN_ROWS, seq)
    nr = seq // rows
    ncr = rows // CHUNK
    group = GDN_SCAN_HEADS
    gblk = lambda last: pl.BlockSpec((None, group, rows, last), lambda b, h, i: (b, h, i, 0))
    o = pl.pallas_call(
        functools.partial(_gdn_scan_kernel, group=group),
        out_shape=jax.ShapeDtypeStruct((m, hv * dk), BF16),
        grid=(batch, hv // group, nr),
        in_specs=[gblk(dk), gblk(dk), gblk(dk), gblk(2 * CHUNK),
                  pl.BlockSpec((None, group, dk, rows), lambda b, h, i: (b, h, 0, i)),
                  pl.BlockSpec((None, group, ncr, dk), lambda b, h, i: (b, h, i, 0)),
                  pl.BlockSpec((rows, group * dk), lambda b, h, i: (b * nr + i, h)),
                  pl.BlockSpec((1, dk), lambda b, h, i: (0, 0))],
        out_specs=pl.BlockSpec((rows, group * dk), lambda b, h, i: (b * nr + i, h)),
        scratch_shapes=[pltpu.VMEM((group, dk, dk), F32), pltpu.VMEM((group, 2 * CHUNK, dk), F32),
                        pltpu.VMEM((group, 2 * CHUNK, dk), BF16)],
        compiler_params=_params("parallel", "parallel", "arbitrary"),
        name="gdn_scan",
    )(w_, qg_, u_, att_, kdt_, dec_, z, norm_g.reshape(1, dk))
    return matmul_residual(o, w_out.astype(BF16), x, tm=MM_ROWS // 2, name="gdn_out_proj")


def _rwkv_mix_kernel(x_ref, g_ref, mu_ref, o_ref, tail_ref, *, rows_per_seq):
    i = pl.program_id(0)
    x = x_ref[...]
    tm = x.shape[0]
    hn = x * lax.rsqrt(jnp.mean(x * x, axis=-1, keepdims=True) + NORM_EPS) * g_ref[...]

    @pl.when((i * tm) % rows_per_seq == 0)
    def _():
        tail_ref[...] = jnp.zeros(tail_ref.shape, F32)

    prev = jnp.where(_iota(hn.shape, 0) == 0, tail_ref[7:8, :], pltpu.roll(hn, 1, 0))
    tail_ref[...] = hn[tm - 8:]
    xx = prev - hn
    for c in range(o_ref.shape[0]):
        o_ref[c] = (hn + xx * mu_ref[c:c + 1, :]).astype(o_ref.dtype)


def rwkv_token_mix(x, norm_g, mu, seq, *, tm=ROW_TILE // 2):
    m, d = x.shape
    tm = min(tm, seq)
    nmix = mu.shape[0]
    return pl.pallas_call(
        functools.partial(_rwkv_mix_kernel, rows_per_seq=seq),
        out_shape=jax.ShapeDtypeStruct((nmix, m, d), BF16),
        grid=(m // tm,),
        in_specs=[pl.BlockSpec((tm, d), lambda i: (i, 0)),
                  pl.BlockSpec((1, d), lambda i: (0, 0)),
                  pl.BlockSpec((nmix, d), lambda i: (0, 0))],
        out_specs=pl.BlockSpec((nmix, tm, d), lambda i: (0, i, 0)),
        scratch_shapes=[pltpu.VMEM((8, d), F32)],
        compiler_params=_params("arbitrary"),
        name="rwkv_token_mix",
    )(x, norm_g.reshape(1, d), mu)


def _grouped_mm_kernel(a_ref, w_ref, o_ref):
    o_ref[...] = _dot(a_ref[...], w_ref[...]).astype(o_ref.dtype)


def grouped_matmul(a, w, out_dtype, *, tm=MM_ROWS, tn=MM_COLS):
    g, k, n = w.shape
    m = a.shape[1]
    tm, tn = min(tm, m), min(tn, n)
    return pl.pallas_call(
        _grouped_mm_kernel,
        out_shape=jax.ShapeDtypeStruct((g, m, n), out_dtype),
        grid=(g, n // tn, m // tm),
        in_specs=[pl.BlockSpec((None, tm, k), lambda c, j, i: (c, i, 0)),
                  pl.BlockSpec((None, k, tn), lambda c, j, i: (c, 0, j))],
        out_specs=pl.BlockSpec((None, tm, tn), lambda c, j, i: (c, i, j)),
        compiler_params=_params("parallel", "parallel", "parallel"),
        name="rwkv_rkvg_proj",
    )(a, w)


def _rwkv_lora_kernel(xw_ref, xa_ref, w1_ref, w2_ref, a1_ref, a2_ref, w0_ref, a0_ref, ld_ref, a_ref):
    hw = jnp.tanh(_dot(xw_ref[...], w1_ref[...])).astype(BF16)
    lw = w0_ref[...] + _dot(hw, w2_ref[...])
    log_w = -_softplus(-lw) - 0.5
    ld_ref[...] = -jnp.exp(log_w)
    ha = _dot(xa_ref[...], a1_ref[...]).astype(BF16)
    a_ref[...] = jax.nn.sigmoid(a0_ref[...] + _dot(ha, a2_ref[...]))


def rwkv_lora(xs, w1, w2, a1, a2, w0, a0, *, tm=ROW_TILE):
    _, m, d = xs.shape
    tm = min(tm, m)
    r = w1.shape[1]
    row = pl.BlockSpec((1, d), lambda i: (0, 0))
    shape = jax.ShapeDtypeStruct((m, d), F32)
    return pl.pallas_call(
        _rwkv_lora_kernel,
        out_shape=(shape, shape),
        grid=(m // tm,),
        in_specs=[pl.BlockSpec((None, tm, d), lambda i: (4, i, 0)),
                  pl.BlockSpec((None, tm, d), lambda i: (5, i, 0)),
                  pl.BlockSpec((d, r), lambda i: (0, 0)), pl.BlockSpec((r, d), lambda i: (0, 0)),
                  pl.BlockSpec((d, r), lambda i: (0, 0)), pl.BlockSpec((r, d), lambda i: (0, 0)),
                  row, row],
        out_specs=(pl.BlockSpec((tm, d), lambda i: (i, 0)), pl.BlockSpec((tm, d), lambda i: (i, 0))),
        compiler_params=_params("parallel"),
        name="rwkv_lora",
    )(xs, xs, w1, w2, a1, a2, w0.reshape(1, d), a0.reshape(1, d))


def _head_sum(x, first_head):
    tot = jnp.sum(x, axis=-1, keepdims=True)
    lo = jnp.sum(jnp.where(first_head, x, 0.0), axis=-1, keepdims=True)
    return jnp.where(first_head, lo, tot - lo)


def _rwkv_prep_kernel(r_ref, k_ref, v_ref, ld_ref, a_ref, kk_ref, ka_ref, rk_ref,
                      wt_ref, rt_ref, u_ref, aro_ref, kbt_ref, vb_ref, dec_ref, bonus_ref,
                      kkt_s, kb_s, akk_s, p_s, t_s, drow_s, *, cps):
    c = CHUNK
    n2 = 2 * c
    n_h = B_HEAD_DIM
    npairs = r_ref.shape[0] // n2
    first_head = _iota((1, LANES), 1) < n_h
    tri, strict, diag = _pair_masks()
    tri_b = tri.astype(BF16)
    eye_b = diag.astype(BF16)
    eye_f = diag.astype(F32)
    upper_half = _iota((n2, LANES), 0) < c
    pairs = [slice(n * n2, (n + 1) * n2) for n in range(npairs)]

    for n, rows in enumerate(pairs):
        r, k, v, a = r_ref[rows, :], k_ref[rows, :], v_ref[rows, :], a_ref[rows, :]
        kk_raw = k * kk_ref[...]
        kk = kk_raw * lax.rsqrt(_head_sum(kk_raw * kk_raw, first_head) + 1e-6)
        k2c = k * (1.0 + (a - 1.0) * ka_ref[...])
        bbc = kk * a
        bonus_ref[rows, :] = _head_sum(r * k2c * rk_ref[...], first_head) * v
        vb_ref[rows, :] = v.astype(BF16)
        ld = ld_ref[rows, :]
        cs = _dot_x2r(tri_b, ld)
        cs_last = jnp.where(upper_half, cs[c - 1:c, :], cs[n2 - 1:n2, :])
        w_inv = jnp.exp(-cs)
        w_end = jnp.exp(cs_last - cs)
        rt_ref[rows, :] = (r * jnp.exp(cs)).astype(BF16)
        kkt_s[n] = (kk * jnp.exp(cs - ld)).astype(BF16)
        kb_s[n] = jnp.concatenate([k2c * w_inv, bbc * w_inv], axis=0).astype(BF16)
        kw, bw = k2c * w_end, -(bbc * w_end)
        for half in range(2):
            hs = slice(half * c, (half + 1) * c)
            kb_end = jnp.concatenate([kw[hs], bw[hs]], axis=0)
            cols = slice((2 * n + half) * n2, (2 * n + half + 1) * n2)
            kbt_ref[:, cols] = kb_end.T.astype(BF16)
        blk, off = divmod(2 * n, cps)
        drow_s[blk, off:off + 1, :] = jnp.exp(cs[c - 1:c, :])
        drow_s[blk, off + 1:off + 2, :] = jnp.exp(cs[n2 - 1:n2, :])
    for blk in range(dec_ref.shape[0] // LANES):
        drow_s[blk, cps:, :] = jnp.zeros((LANES - cps, LANES), F32)
        dec_ref[blk * LANES:(blk + 1) * LANES, :] = _dot_xr(eye_b, drow_s[blk], NT)
    zero_b = jnp.zeros((n2, LANES), BF16)
    strict2 = jnp.concatenate([strict, strict], axis=1)
    tri2 = jnp.concatenate([tri, tri], axis=1)
    sign2 = jnp.where(_iota((n2, 2 * n2), 1) < n2, 1.0, -1.0)
    for n, rows in enumerate(pairs):
        kkt, rt = kkt_s[n], rt_ref[rows, :]
        lhs = jnp.concatenate([jnp.where(first_head, kkt, zero_b), jnp.where(first_head, rt, zero_b),
                               jnp.where(first_head, zero_b, kkt), jnp.where(first_head, zero_b, rt)], axis=0)
        gram = _dot(lhs, kb_s[n], NT)
        for hh in range(2):
            i = 2 * n + hh
            g_kk = jnp.where(strict2, gram[2 * hh * n2:(2 * hh + 1) * n2], 0.0)
            neg_a = -g_kk[:, n2:]
            p_s[i] = neg_a.astype(BF16)
            t_s[i] = eye_f + neg_a
            akk_s[i] = g_kk[:, :n2].astype(BF16)
            g_r = gram[(2 * hh + 1) * n2:(2 * hh + 2) * n2]
            aro_ref[hh, rows, :] = jnp.where(tri2, g_r * sign2, 0.0).astype(BF16)
    _inverse_stages(p_s, t_s, 2 * npairs, c)
    for n, rows in enumerate(pairs):
        av = _dot(jnp.concatenate([akk_s[2 * n], akk_s[2 * n + 1]], axis=0), vb_ref[rows, :])
        p_s[2 * n] = av[:n2].astype(BF16)
        p_s[2 * n + 1] = av[n2:].astype(BF16)
    for n, rows in enumerate(pairs):
        t0, t1 = t_s[2 * n].astype(BF16), t_s[2 * n + 1].astype(BF16)
        wt = _dot(jnp.concatenate([t0, t1], axis=0), kkt_s[n])
        wt_ref[rows, :] = jnp.where(first_head, wt[:n2], wt[n2:]).astype(BF16)
        u_ref[rows, :] = jnp.where(first_head, _dot(t0, p_s[2 * n]), _dot(t1, p_s[2 * n + 1]))


def _rwkv_scan_kernel(wt_ref, rt_ref, u_ref, aro_ref, kbt_ref, vb_ref, dec_ref, bonus_ref, gate_ref,
                      lnw_ref, lnb_ref, o_ref, s_ref, ms_s, xc_s, xp_s, *, group):
    c = CHUNK
    n2 = 2 * c
    n_h = B_HEAD_DIM
    first_head = _iota((1, LANES), 1) < n_h
    same_head = _iota((LANES, LANES), 0) // n_h == _iota((LANES, LANES), 1) // n_h
    zeros = jnp.zeros((c, LANES), BF16)

    def head_mean(x):
        return _head_sum(x, first_head) * (1.0 / n_h)

    @pl.when(pl.program_id(2) == 0)
    def _():
        s_ref[...] = jnp.zeros(s_ref.shape, F32)

    for n in range(u_ref.shape[1] // c):
        rows = slice(n * c, (n + 1) * c)
        for gi in range(group):
            lhs = jnp.concatenate([wt_ref[gi, rows, :], rt_ref[gi, rows, :]], axis=0)
            ms_s[gi] = _dot(lhs, s_ref[gi].astype(BF16))
        for gi in range(group):
            sa = (u_ref[gi, rows, :] + ms_s[gi, :c, :]).astype(BF16)
            vb = vb_ref[gi, rows, :]
            xc_s[gi] = jnp.concatenate([vb, sa], axis=0)
            xp_s[gi] = jnp.concatenate([vb, zeros, sa, zeros] if n % 2 == 0 else [zeros, vb, zeros, sa], axis=0)
        for gi in range(group):
            upd = _dot(kbt_ref[gi, :, n * n2:(n + 1) * n2], xc_s[gi])
            dec_col = jnp.broadcast_to(dec_ref[gi, :, n:n + 1], (LANES, LANES))
            s_ref[gi] = s_ref[gi] * dec_col + jnp.where(same_head, upd, 0.0)
        for gi in range(group):
            both = _dot(jnp.concatenate([aro_ref[gi, 0, rows, :], aro_ref[gi, 1, rows, :]], axis=0), xp_s[gi])
            o = ms_s[gi, c:, :] + jnp.where(first_head, both[:c], both[c:])
            mean = head_mean(o)
            dlt = o - mean
            var = head_mean(dlt * dlt)
            cols = slice(gi * LANES, (gi + 1) * LANES)
            y = dlt * lax.rsqrt(var + B_GN_EPS) * lnw_ref[:, cols] + lnb_ref[:, cols]
            y = y + bonus_ref[rows, cols]
            o_ref[rows, cols] = (y * _silu(gate_ref[rows, cols])).astype(o_ref.dtype)


def rwkv7_layer(x, norm_g, mu, w_rkvg, w0, w_w1, w_w2, a0, w_a1, w_a2, k_k, k_a, r_k, ln_w, ln_b, w_out,
                batch, seq):
    m, d = x.shape
    heads = d // B_HEAD_DIM
    pairs = heads // 2
    order = jnp.array([0, 2, 3, 5, 1, 4])
    xs = rwkv_token_mix(x, norm_g, mu[order], seq)
    rkvg = grouped_matmul(xs, w_rkvg.astype(BF16), F32)
    lora = w_w1.shape[1]
    padc = lambda w: jnp.pad(w, ((0, 0), (0, LORA_PAD - lora))).astype(BF16)
    padr = lambda w: jnp.pad(w, ((0, LORA_PAD - lora), (0, 0))).astype(BF16)
    ld, a = rwkv_lora(xs, padc(w_w1), padr(w_w2), padc(w_a1), padr(w_a2), w0, a0)

    srows = min(SCAN_ROWS, seq)
    rows = min(PREP_ROWS, seq)
    nr = seq // rows
    npair = rows // (2 * CHUNK)
    col = lambda g: pl.BlockSpec((None, rows, LANES), lambda b, p, i, g=g: (g, b * nr + i, p))
    flat = pl.BlockSpec((rows, LANES), lambda b, p, i: (b * nr + i, p))
    prow = pl.BlockSpec((1, LANES), lambda b, p, i: (0, p))
    bp = lambda rws, last, dt: jax.ShapeDtypeStruct((batch, pairs, rws, last), dt)
    pblk = lambda rws, last: pl.BlockSpec((None, None, rws, last), lambda b, p, i: (b, p, i, 0))
    wt_, rt_, u_, aro_, kbt_, vb_, dec_, bonus = pl.pallas_call(
        functools.partial(_rwkv_prep_kernel, cps=srows // CHUNK),
        out_shape=(bp(seq, LANES, BF16), bp(seq, LANES, BF16), bp(seq, LANES, F32),
                   jax.ShapeDtypeStruct((batch, pairs, 2, seq, 4 * CHUNK), BF16),
                   bp(LANES, 2 * seq, BF16), bp(seq, LANES, BF16), bp(seq // srows * LANES, LANES, F32),
                   jax.ShapeDtypeStruct((m, d), F32)),
        grid=(batch, pairs, nr),
        in_specs=[col(0), col(1), col(2), flat, flat, prow, prow, prow],
        out_specs=(pblk(rows, LANES), pblk(rows, LANES), pblk(rows, LANES),
                   pl.BlockSpec((None, None, 2, rows, 4 * CHUNK), lambda b, p, i: (b, p, 0, i, 0)),
                   pl.BlockSpec((None, None, LANES, 2 * rows), lambda b, p, i: (b, p, 0, i)),
                   pblk(rows, LANES), pblk(rows // srows * LANES, LANES), flat),
        scratch_shapes=[pltpu.VMEM((npair, LANES, LANES), BF16), pltpu.VMEM((npair, 2 * LANES, LANES), BF16),
                        pltpu.VMEM((2 * npair, LANES, LANES), BF16), pltpu.VMEM((2 * npair, LANES, LANES), BF16),
                        pltpu.VMEM((2 * npair, LANES, LANES), F32),
                        pltpu.VMEM((rows // srows, LANES, LANES), F32)],
        compiler_params=_params("parallel", "parallel", "parallel"),
        name="rwkv_prep",
    )(rkvg, rkvg, rkvg, ld, a, k_k.reshape(1, d), k_a.reshape(1, d), r_k.reshape(1, d))

    rows = srows
    nr = seq // rows
    group = RWKV_SCAN_PAIRS
    gw = group * LANES
    gblk = lambda rws, last: pl.BlockSpec((None, group, rws, last), lambda b, p, i: (b, p, i, 0))
    gflat = pl.BlockSpec((rows, gw), lambda b, p, i: (b * nr + i, p))
    grow = pl.BlockSpec((1, gw), lambda b, p, i: (0, p))
    o = pl.pallas_call(
        functools.partial(_rwkv_scan_kernel, group=group),
        out_shape=jax.ShapeDtypeStruct((m, d), BF16),
        grid=(batch, pairs // group, nr),
        in_specs=[gblk(rows, LANES), gblk(rows, LANES), gblk(rows, LANES),
                  pl.BlockSpec((None, group, 2, rows, 4 * CHUNK), lambda b, p, i: (b, p, 0, i, 0)),
                  pl.BlockSpec((None, group, LANES, 2 * rows), lambda b, p, i: (b, p, 0, i)),
                  gblk(rows, LANES), gblk(LANES, LANES), gflat,
                  pl.BlockSpec((None, rows, gw), lambda b, p, i: (3, b * nr + i, p)),
                  grow, grow],
        out_specs=gflat,
        scratch_shapes=[pltpu.VMEM((group, LANES, LANES), F32), pltpu.VMEM((group, 2 * CHUNK, LANES), F32),
                        pltpu.VMEM((group, 2 * CHUNK, LANES), BF16), pltpu.VMEM((group, 4 * CHUNK, LANES), BF16)],
        compiler_params=_params("parallel", "parallel", "arbitrary"),
        name="rwkv_scan",
    )(wt_, rt_, u_, aro_, kbt_, vb_, dec_, bonus, rkvg, ln_w.reshape(1, d), ln_b.reshape(1, d))
    return matmul_residual(o, w_out.astype(BF16), x, name="rwkv_out_proj")


def kernel(x, p, positions, norm_g, pe_norm_g, pe_w_gate, pe_w_proj, final_norm_g, a_w_in, a_lam, a_subln_g, a_w_out, b_mu, b_w_rkvg, b_w0, b_w_w1, b_w_w2, b_a0, b_w_a1, b_w_a2, b_k_k, b_k_a, b_r_k, b_ln_w, b_ln_b, b_w_out, c_w_in, c_conv_w, c_A_log, c_dt_bias, c_norm_g, c_w_out):
    batch, seq, d = x.shape
    depth = p.shape[0]
    m = batch * seq
    xf = x.reshape(m, d)
    tables = rope_tables(positions)
    for i in range(depth):
        kind = i % N_MIXERS
        j = i // N_MIXERS
        if kind == 0:
            lam_init = 0.8 - 0.6 * math.exp(-0.3 * i)
            hn = rmsnorm(xf, norm_g[i], BF16)
            xf = diff_attention_layer(xf, hn, tables, a_w_in[j], a_lam[j], a_subln_g[j], a_w_out[j],
                                      batch, seq, lam_init)
        elif kind == 1:
            xf = rwkv7_layer(xf, norm_g[i], b_mu[j], b_w_rkvg[j], b_w0[j], b_w_w1[j], b_w_w2[j], b_a0[j],
                             b_w_a1[j], b_w_a2[j], b_k_k[j], b_k_a[j], b_r_k[j], b_ln_w[j], b_ln_b[j],
                             b_w_out[j], batch, seq)
        else:
            hn = rmsnorm(xf, norm_g[i], BF16)
            xf = gated_deltanet_layer(xf, hn, c_w_in[j], c_conv_w[j], c_A_log[j], c_dt_bias[j], c_norm_g[j],
                                      c_w_out[j], batch, seq)
        hn2 = rmsnorm(xf, pe_norm_g[i], BF16)
        xf = per_layer_embedding(xf, hn2, pe_w_gate[i].astype(BF16), p[i].reshape(m, -1),
                                 pe_w_proj[i].astype(BF16))
    return rmsnorm(xf, final_norm_g, F32).reshape(batch, seq, d)
```

```python
import functools
import math

import jax
import jax.numpy as jnp
from jax import lax
from jax.experimental import pallas as pl
from jax.experimental.pallas import tpu as pltpu

F32 = jnp.float32
BF16 = jnp.bfloat16

N_MIXERS = 3
NORM_EPS = 1e-6
LANES = 128
VMEM_LIMIT = 48 * 1024 * 1024

A_HEAD_DIM = 128
A_V_DIM = 2 * A_HEAD_DIM
ROT_DIM = A_HEAD_DIM // 4
ROPE_THETA = 500000.0
SUBLN_EPS = 1e-5
ATTN_BLOCK = 512
ATTN_Q_BLOCKS = 1
ONES_ROWS = 16

B_HEAD_DIM = 64
B_GN_EPS = 64e-5
LORA_PAD = 128

C_HEAD_DIM = 128
C_CONV_WIDTH = 4
CHUNK = 64
SCAN_ROWS = 512
PREP_ROWS = 1024
SLAB_ROWS = 32
GDN_SCAN_HEADS = 16
RWKV_SCAN_PAIRS = 8

MM_ROWS = 1024
MM_COLS = 1024
ROW_TILE = 512
EPI_COLS = 2048
EPI_SUB = 256

NN = (((1,), (0,)), ((), ()))
NT = (((1,), (1,)), ((), ()))


def _dot(a, b, dims=NN):
    return lax.dot_general(a, b, dims, preferred_element_type=F32)


def _split2(x):
    hi = x.astype(BF16)
    lo = (x - hi.astype(F32)).astype(BF16)
    return hi, lo


def _split3(x):
    hi = x.astype(BF16)
    r = x - hi.astype(F32)
    mid = r.astype(BF16)
    lo = (r - mid.astype(F32)).astype(BF16)
    return hi, mid, lo


def _dot_xr(a_exact, b, dims=NN):
    h, m, l = _split3(b)
    return _dot(a_exact, h, dims) + (_dot(a_exact, m, dims) + _dot(a_exact, l, dims))


def _dot_x2r(a_exact, b, dims=NN):
    h, l = _split2(b)
    return _dot(a_exact, h, dims) + _dot(a_exact, l, dims)


def _iota(shape, dim):
    return lax.broadcasted_iota(jnp.int32, shape, dim)


def _silu(x):
    return x * jax.nn.sigmoid(x)


def _softplus(x):
    return jnp.maximum(x, 0.0) + jnp.log(1.0 + jnp.exp(-jnp.abs(x)))


def _params(*sem):
    return pltpu.CompilerParams(dimension_semantics=sem, vmem_limit_bytes=VMEM_LIMIT)


def _rmsnorm_kernel(x_ref, g_ref, o_ref, *, eps):
    x = x_ref[...]
    y = x * lax.rsqrt(jnp.mean(x * x, axis=-1, keepdims=True) + eps)
    o_ref[...] = (y * g_ref[...]).astype(o_ref.dtype)


def rmsnorm(x, g, out_dtype, *, eps=NORM_EPS, tm=ROW_TILE):
    m, d = x.shape
    tm = min(tm, m)
    return pl.pallas_call(
        functools.partial(_rmsnorm_kernel, eps=eps),
        out_shape=jax.ShapeDtypeStruct((m, d), out_dtype),
        grid=(m // tm,),
        in_specs=[pl.BlockSpec((tm, d), lambda i: (i, 0)), pl.BlockSpec((1, d), lambda i: (0, 0))],
        out_specs=pl.BlockSpec((tm, d), lambda i: (i, 0)),
        compiler_params=_params("parallel"),
        name="rmsnorm",
    )(x, g.reshape(1, d))


def _mm_kernel(a_ref, w_ref, *rest, epilogue):
    o_ref = rest[-1]
    acc = _dot(a_ref[...], w_ref[...])
    if epilogue is not None:
        acc = epilogue(acc, *rest[:-1])
    o_ref[...] = acc.astype(o_ref.dtype)


def matmul(a, w, out_dtype, *, tm=MM_ROWS, tn=MM_COLS, extra=(), extra_specs=(), epilogue=None, name="matmul"):
    m, k = a.shape
    n = w.shape[1]
    tm, tn = min(tm, m), min(tn, n)
    return pl.pallas_call(
        functools.partial(_mm_kernel, epilogue=epilogue),
        out_shape=jax.ShapeDtypeStruct((m, n), out_dtype),
        grid=(n // tn, m // tm),
        in_specs=[pl.BlockSpec((tm, k), lambda j, i: (i, 0)),
                  pl.BlockSpec((k, tn), lambda j, i: (0, j))] + list(extra_specs),
        out_specs=pl.BlockSpec((tm, tn), lambda j, i: (i, j)),
        compiler_params=_params("parallel", "parallel"),
        name=name,
    )(a, w, *extra)


def matmul_residual(a, w, res, *, tm=MM_ROWS, tn=MM_COLS, name="matmul_residual"):
    tm, tn = min(tm, a.shape[0]), min(tn, w.shape[1])
    return matmul(a, w, F32, tm=tm, tn=tn, extra=(res,),
                  extra_specs=(pl.BlockSpec((tm, tn), lambda j, i: (i, j)),),
                  epilogue=lambda acc, r_ref: r_ref[...] + acc, name=name)


def _ple_kernel(h_ref, wg_ref, p_ref, wp_ref, x_ref, o_ref):
    gate = jax.nn.sigmoid(_dot(h_ref[...], wg_ref[...]))
    proj = _dot(p_ref[...].astype(BF16), wp_ref[...])
    o_ref[...] = x_ref[...] + gate * proj


def per_layer_embedding(x, hn, w_gate, p, w_proj, *, tm=MM_ROWS, tn=MM_COLS):
    m, d = x.shape
    pd = p.shape[1]
    tm, tn = min(tm, m), min(tn, d)
    return pl.pallas_call(
        _ple_kernel,
        out_shape=jax.ShapeDtypeStruct((m, d), F32),
        grid=(d // tn, m // tm),
        in_specs=[pl.BlockSpec((tm, d), lambda j, i: (i, 0)),
                  pl.BlockSpec((d, tn), lambda j, i: (0, j)),
                  pl.BlockSpec((tm, pd), lambda j, i: (i, 0)),
                  pl.BlockSpec((pd, tn), lambda j, i: (0, j)),
                  pl.BlockSpec((tm, tn), lambda j, i: (i, j))],
        out_specs=pl.BlockSpec((tm, tn), lambda j, i: (i, j)),
        compiler_params=_params("parallel", "parallel"),
        name="per_layer_embedding",
    )(hn, w_gate, p, w_proj, x)


def _rope_table_kernel(pos_ref, freq_ref, cos_ref, sin_lo_ref, sin_hi_ref):
    half = ROT_DIM // 2
    ang = pos_ref[...].astype(F32) * freq_ref[...]
    lane = _iota(ang.shape, 1)
    c, s = jnp.cos(ang), jnp.sin(ang)
    cos_ref[...] = jnp.where(lane < ROT_DIM, c, 1.0)
    sin_lo_ref[...] = jnp.where(lane < half, -s, 0.0)
    sin_hi_ref[...] = jnp.where((lane >= half) & (lane < ROT_DIM), s, 0.0)


def rope_tables(positions, *, tm=MM_ROWS):
    m = positions.size
    tm = min(tm, m)
    inv_freq = ROPE_THETA ** (-jnp.arange(0, ROT_DIM, 2, dtype=F32) / ROT_DIM)
    freq_row = jnp.concatenate([inv_freq, inv_freq, jnp.zeros((LANES - ROT_DIM,), F32)]).reshape(1, LANES)
    shape = jax.ShapeDtypeStruct((m, LANES), F32)
    spec = pl.BlockSpec((tm, LANES), lambda i: (i, 0))
    return pl.pallas_call(
        _rope_table_kernel,
        out_shape=(shape, shape, shape),
        grid=(m // tm,),
        in_specs=[pl.BlockSpec((tm, 1), lambda i: (i, 0)), pl.BlockSpec((1, LANES), lambda i: (0, 0))],
        out_specs=(spec, spec, spec),
        compiler_params=_params("parallel"),
        name="rope_tables",
    )(positions.reshape(m, 1), freq_row)


def _attn_qk_kernel(a_ref, w_ref, cos_ref, sin_lo_ref, sin_hi_ref, o_ref, *, n_q_blocks, sub):
    j = pl.program_id(0)
    half = ROT_DIM // 2
    scale = jnp.where(j < n_q_blocks, A_HEAD_DIM ** -0.5, 1.0).astype(F32)
    cos, sin_lo, sin_hi = cos_ref[...], sin_lo_ref[...], sin_hi_ref[...]
    a = a_ref[...]

    def epilogue(acc, c0):
        for g in range(acc.shape[1] // LANES):
            x = acc[:, g * LANES:(g + 1) * LANES]
            y = x * cos + pltpu.roll(x, LANES - half, 1) * sin_lo + pltpu.roll(x, half, 1) * sin_hi
            o_ref[:, c0 + g * LANES:c0 + (g + 1) * LANES] = (y * scale).astype(o_ref.dtype)

    starts = list(range(0, w_ref.shape[1], sub))
    acc = _dot(a, w_ref[:, starts[0]:starts[0] + sub])
    for c, c0 in enumerate(starts):
        nxt = _dot(a, w_ref[:, starts[c + 1]:starts[c + 1] + sub]) if c + 1 < len(starts) else None
        epilogue(acc, c0)
        acc = nxt


def attn_qk_proj(hn, w_qk, tables, *, tm=MM_ROWS, tn=EPI_COLS, sub=EPI_SUB):
    m, k = hn.shape
    n = w_qk.shape[1]
    tm, tn = min(tm, m), min(tn, n // 2)
    tspec = pl.BlockSpec((tm, LANES), lambda j, i: (i, 0))
    return pl.pallas_call(
        functools.partial(_attn_qk_kernel, n_q_blocks=n // 2 // tn, sub=min(sub, tn)),
        out_shape=jax.ShapeDtypeStruct((m, n), BF16),
        grid=(n // tn, m // tm),
        in_specs=[pl.BlockSpec((tm, k), lambda j, i: (i, 0)),
                  pl.BlockSpec((k, tn), lambda j, i: (0, j)), tspec, tspec, tspec],
        out_specs=pl.BlockSpec((tm, tn), lambda j, i: (i, j)),
        compiler_params=_params("parallel", "parallel"),
        name="attn_qk_proj",
    )(hn, w_qk, *tables)


def _attn_vt_kernel(w_ref, a_ref, o_ref):
    o_ref[...] = _dot(w_ref[...], a_ref[...], NT).astype(o_ref.dtype)


def attn_v_proj_t(hn, w_t, batch, seq, blk, *, tn=MM_COLS):
    m, k = hn.shape
    n = w_t.shape[0]
    tn = min(tn, n)
    nk = seq // blk
    return pl.pallas_call(
        _attn_vt_kernel,
        out_shape=jax.ShapeDtypeStruct((batch, nk, n, blk), BF16),
        grid=(n // tn, m // blk),
        in_specs=[pl.BlockSpec((tn, k), lambda j, i: (j, 0)),
                  pl.BlockSpec((blk, k), lambda j, i: (i, 0))],
        out_specs=pl.BlockSpec((None, None, tn, blk), lambda j, i: (i // nk, i % nk, j, 0)),
        compiler_params=_params("parallel", "parallel"),
        name="attn_v_proj_t",
    )(w_t, hn)


def _diff_attn_kernel(lam_ref, q_ref, k_ref, vt_ref, z_ref, g_ref, o_ref, m_ref, acc_ref, s_ref,
                      *, bq, bk, lam_init):
    i = pl.program_id(2)
    lam = lam_ref[...]
    lam_full = (jnp.exp(jnp.sum(lam[0:1] * lam[1:2], axis=-1, keepdims=True))
                - jnp.exp(jnp.sum(lam[2:3] * lam[3:4], axis=-1, keepdims=True)) + lam_init)
    m_ref[...] = jnp.full(m_ref.shape, -jnp.inf, F32)
    acc_ref[...] = jnp.zeros(acc_ref.shape, F32)
    q = q_ref[...]
    d = A_HEAD_DIM
    w = A_V_DIM
    ones = jnp.ones((ONES_ROWS, bk), BF16)

    def scores(j, slot, q_lo=0):
        start = pl.multiple_of(j * bk, bk)
        kb = k_ref[pl.ds(start, bk), :]
        for c in range(2):
            s_ref[slot, c, :, q_lo:] = _dot(kb[:, c * d:(c + 1) * d], q[q_lo:, c * d:(c + 1) * d], NT)

    def absorb(j, slot, masked, q_lo=0):
        vt = jnp.concatenate([vt_ref[j], ones], axis=0)
        for c in range(2):
            s = s_ref[slot, c, :, q_lo:]
            if masked:
                kv_pos = j * bk + _iota(s.shape, 0)
                q_pos = i * bq + q_lo + _iota(s.shape, 1)
                s = jnp.where(kv_pos <= q_pos, s, -jnp.inf)
            m_prev = m_ref[c, :, q_lo:]
            m_new = jnp.maximum(m_prev, jnp.max(s, axis=0, keepdims=True))
            alpha = jnp.exp(m_prev - m_new)
            p = jnp.exp(s - m_new)
            acc_ref[c, :, q_lo:] = alpha * acc_ref[c, :, q_lo:] + _dot(vt, p.astype(BF16))
            m_ref[c, :, q_lo:] = m_new

    r = bq // bk
    first_masked = r * i
    scores(0, 0)

    def body(jj, carry):
        scores(2 * jj + 1, 1)
        absorb(2 * jj, 0, False)
        scores(2 * jj + 2, 0)
        absorb(2 * jj + 1, 1, False)
        return carry

    lax.fori_loop(0, first_masked // 2, body, 0)

    def tail(odd):
        base = first_masked - odd
        left = [(base + n, n % 2, n >= odd, max(n - odd, 0) * bk) for n in range(odd + r)]
        for n, (j, slot, masked, q_lo) in enumerate(left):
            if n + 1 < len(left):
                nj, nslot, _, nq_lo = left[n + 1]
                scores(nj, nslot, nq_lo)
            absorb(j, slot, masked, q_lo)

    if r % 2 == 0:
        tail(0)
    else:
        pl.when(first_masked % 2 == 0)(lambda: tail(0))
        pl.when(first_masked % 2 == 1)(lambda: tail(1))

    o = (acc_ref[0, :w, :] / acc_ref[0, w:w + 1, :]
         - lam_full * (acc_ref[1, :w, :] / acc_ref[1, w:w + 1, :]))
    o = o * lax.rsqrt(jnp.mean(o * o, axis=0, keepdims=True) + SUBLN_EPS) * g_ref[...]
    o = (o * (1.0 - lam_init)).T
    o_ref[...] = (o * _silu(z_ref[...].astype(F32))).astype(o_ref.dtype)


def diff_attention_core(qk, z, vt, lam, subln_g, batch, seq, heads, lam_init):
    m = qk.shape[0]
    bk = vt.shape[-1]
    bq = ATTN_Q_BLOCKS * bk
    nq = seq // bq
    nk = seq // bk
    w = A_V_DIM
    return pl.pallas_call(
        functools.partial(_diff_attn_kernel, bq=bq, bk=bk, lam_init=lam_init),
        out_shape=jax.ShapeDtypeStruct((m, heads * w), BF16),
        grid=(batch, heads, nq),
        in_specs=[pl.BlockSpec((4, A_HEAD_DIM), lambda b, h, i: (0, 0)),
                  pl.BlockSpec((bq, w), lambda b, h, i: (b * nq + i, h)),
                  pl.BlockSpec((seq, w), lambda b, h, i: (b, heads + h)),
                  pl.BlockSpec((None, nk, w, bk), lambda b, h, i: (b, 0, h, 0)),
                  pl.BlockSpec((bq, w), lambda b, h, i: (b * nq + i, h)),
                  pl.BlockSpec((w, 1), lambda b, h, i: (0, 0))],
        out_specs=pl.BlockSpec((bq, w), lambda b, h, i: (b * nq + i, h)),
        scratch_shapes=[pltpu.VMEM((2, 1, bq), F32), pltpu.VMEM((2, w + ONES_ROWS, bq), F32),
                        pltpu.VMEM((2, 2, bk, bq), F32)],
        compiler_params=_params("parallel", "parallel", "parallel"),
        name="diff_attention",
    )(lam, qk, qk, vt, z, subln_g.reshape(w, 1))


def diff_attention_layer(x, hn, tables, w_in, lam, subln_g, w_out, batch, seq, lam_init):
    d = x.shape[1]
    heads = d // A_V_DIM
    qk_w = heads * 2 * A_HEAD_DIM
    v_w = heads * A_V_DIM
    blk = min(ATTN_BLOCK, seq // ATTN_Q_BLOCKS)
    w_vt = w_in[:, 2 * qk_w:2 * qk_w + v_w].T.astype(BF16)
    qk = attn_qk_proj(hn, w_in[:, :2 * qk_w].astype(BF16), tables)
    z = matmul(hn, w_in[:, 2 * qk_w + v_w:].astype(BF16), BF16, name="attn_gate_proj")
    vt = attn_v_proj_t(hn, w_vt, batch, seq, blk)
    o = diff_attention_core(qk, z, vt, lam, subln_g, batch, seq, heads, lam_init)
    return matmul_residual(o, w_out.astype(BF16), x, name="attn_out_proj")


def _gdn_conv_kernel(a_ref, w_ref, cw_ref, o_ref, tail_ref, *, rows_per_seq, sub):
    i = pl.program_id(1)
    tm = a_ref.shape[0]

    @pl.when((i * tm) % rows_per_seq == 0)
    def _():
        tail_ref[...] = jnp.zeros(tail_ref.shape, F32)

    a = a_ref[...]
    sub_iota = _iota((8, sub), 0)
    last = C_CONV_WIDTH - 1

    def epilogue(acc, cols):
        tail = tail_ref[:, cols]

        def shifted(s):
            xs = pltpu.roll(acc, s, 0)
            head = jnp.where(sub_iota < s, pltpu.roll(tail, s, 0), xs[:8])
            return jnp.concatenate([head, xs[8:]], axis=0)

        cw = cw_ref[:, cols]
        y = shifted(last) * cw[0:1]
        for t in range(1, last):
            y = y + shifted(last - t) * cw[t:t + 1]
        y = y + acc * cw[last:last + 1]
        tail_ref[:, cols] = acc[tm - 8:]
        o_ref[:, cols] = _silu(y).astype(o_ref.dtype)

    blocks = [slice(c0, c0 + sub) for c0 in range(0, w_ref.shape[1], sub)]
    acc = _dot(a, w_ref[:, blocks[0]])
    for c, cols in enumerate(blocks):
        nxt = _dot(a, w_ref[:, blocks[c + 1]]) if c + 1 < len(blocks) else None
        epilogue(acc, cols)
        acc = nxt


def gdn_conv_proj(hn, w, conv_w, seq, *, tm=MM_ROWS, tn=EPI_COLS, sub=EPI_SUB):
    m, k = hn.shape
    n = w.shape[1]
    tm, tn = min(tm, seq), min(tn, n)
    return pl.pallas_call(
        functools.partial(_gdn_conv_kernel, rows_per_seq=seq, sub=min(sub, tn)),
        out_shape=jax.ShapeDtypeStruct((m, n), BF16),
        grid=(n // tn, m // tm),
        in_specs=[pl.BlockSpec((tm, k), lambda j, i: (i, 0)),
                  pl.BlockSpec((k, tn), lambda j, i: (0, j)),
                  pl.BlockSpec((C_CONV_WIDTH, tn), lambda j, i: (0, j))],
        out_specs=pl.BlockSpec((tm, tn), lambda j, i: (i, j)),
        scratch_shapes=[pltpu.VMEM((8, tn), F32)],
        compiler_params=_params("arbitrary", "arbitrary"),
        name="gdn_conv_proj",
    )(hn, w, conv_w)


def _pair_masks():
    n = 2 * CHUNK
    r, c = _iota((n, n), 0), _iota((n, n), 1)
    same = (r // CHUNK) == (c // CHUNK)
    return same & (r >= c), same & (r > c), r == c


def _inverse_stages(p_ref, t_ref, count, nilpotency):
    for _ in range(int(math.log2(nilpotency)) - 1):
        for i in range(count):
            p_ref[i] = _dot(p_ref[i], p_ref[i]).astype(BF16)
        for i in range(count):
            t = t_ref[i]
            t_ref[i] = t + _dot(t.astype(BF16), p_ref[i])


def _gdn_gates_kernel(a_ref, w_ref, alog_ref, dtb_ref, beta_ref, gc_ref, gct_ref):
    acc = _dot(a_ref[...], w_ref[...])
    beta_ref[...] = jax.nn.sigmoid(acc[:, :LANES])
    g = -jnp.exp(alog_ref[...]) * _softplus(acc[:, LANES:] + dtb_ref[...])
    tri, _, diag = _pair_masks()
    tri_b, eye_b = tri.astype(BF16), diag.astype(BF16)
    n2 = 2 * CHUNK
    for n in range(g.shape[0] // n2):
        rows = slice(n * n2, (n + 1) * n2)
        gc = _dot_xr(tri_b, g[rows])
        gc_ref[rows, :] = gc
        gct_ref[:, rows] = _dot_xr(eye_b, gc, NT)


def gdn_gates(hn, w_ba, alog_row, dtb_row, *, tm=ROW_TILE):
    m, k = hn.shape
    tm = min(tm, m)
    row = pl.BlockSpec((1, LANES), lambda i: (0, 0))
    out = pl.BlockSpec((tm, LANES), lambda i: (i, 0))
    return pl.pallas_call(
        _gdn_gates_kernel,
        out_shape=(jax.ShapeDtypeStruct((m, LANES), F32), jax.ShapeDtypeStruct((m, LANES), F32),
                   jax.ShapeDtypeStruct((LANES, m), F32)),
        grid=(m // tm,),
        in_specs=[pl.BlockSpec((tm, k), lambda i: (i, 0)), pl.BlockSpec((k, 2 * LANES), lambda i: (0, 0)), row, row],
        out_specs=(out, out, pl.BlockSpec((LANES, tm), lambda i: (0, i))),
        compiler_params=_params("parallel"),
        name="gdn_gates",
    )(hn, w_ba, alog_row, dtb_row)


def _gdn_prep_kernel(q_ref, k_ref, v_ref, beta_ref, gc_ref, gct_ref,
                     w_ref, qg_ref, u_ref, att_ref, kdt_ref, dec_ref,
                     g_s, dec_s, kn_s, kb_s, rhs_s, p_s, t_s, qn_s, kdec_s):
    kh = pl.program_id(1)
    c = CHUNK
    n2 = 2 * c
    dk = C_HEAD_DIM
    nv = w_ref.shape[0]
    npairs = q_ref.shape[0] // n2
    tri, strict, diag = _pair_masks()
    eye_f = diag.astype(F32)
    pairs = [slice(n * n2, (n + 1) * n2) for n in range(npairs)]
    lane = _iota((n2, LANES), 1)

    for n, rows in enumerate(pairs):
        for e in range(nv):
            h = kh * nv + e
            gc = jnp.broadcast_to(jnp.sum(jnp.where(lane == h, gc_ref[rows, :], 0.0), axis=-1, keepdims=True),
                                  (n2, dk))
            grp = gct_ref[pl.ds(pl.multiple_of((h // 8) * 8, 8), 8), rows]
            own_row = jnp.sum(jnp.where(_iota(grp.shape, 0) == h % 8, grp, 0.0), axis=0, keepdims=True)
            gc_row = jnp.broadcast_to(own_row, (n2, n2))
            g_s[n * nv + e] = gc
            dec_s[n * nv + e] = jnp.where(tri, jnp.exp(jnp.where(tri, gc - gc_row, 0.0)), 0.0)
    slab = SLAB_ROWS
    lane_slab = _iota((slab, LANES), 1)
    for n, rows in enumerate(pairs):
        for r0 in range(0, n2, slab):
            rs = slice(r0, r0 + slab)
            gs = slice(n * n2 + r0, n * n2 + r0 + slab)
            last = c - 1 if r0 < c else n2 - 1
            qf = q_ref[gs, :].astype(F32)
            kf = k_ref[gs, :].astype(F32)
            qn = qf * lax.rsqrt(jnp.sum(qf * qf, axis=-1, keepdims=True) + 1e-6) * (dk ** -0.5)
            kn = kf * lax.rsqrt(jnp.sum(kf * kf, axis=-1, keepdims=True) + 1e-6)
            kn_s[n, rs, :] = kn.astype(BF16)
            qn_s[n, rs, :] = qn.astype(BF16)
            for e in range(nv):
                i = n * nv + e
                gc = g_s[i, rs, :]
                vf = v_ref[gs, e * dk:(e + 1) * dk].astype(F32)
                beta = jnp.sum(jnp.where(lane_slab == kh * nv + e, beta_ref[gs, :], 0.0), axis=-1, keepdims=True)
                egc = jnp.exp(gc)
                kb = kn * beta
                kb_s[i, rs, :] = kb.astype(BF16)
                rhs_s[i, rs, :dk] = (vf * beta).astype(BF16)
                rhs_s[i, rs, dk:] = (kb * egc).astype(BF16)
                kdec_s[i, rs, :] = kn * jnp.exp(g_s[i, last:last + 1, :] - gc)
                qg_ref[e, gs, :] = (qn * egc).astype(BF16)
        for e in range(nv):
            i = n * nv + e
            dec_ref[e, 2 * n:2 * n + 1, :] = jnp.exp(g_s[i, c - 1:c, :])
            dec_ref[e, 2 * n + 1:2 * n + 2, :] = jnp.exp(g_s[i, n2 - 1:n2, :])
    for n, rows in enumerate(pairs):
        raw = _dot(qn_s[n], kn_s[n], NT)
        for e in range(nv):
            att_ref[e, rows, :] = (raw * dec_s[n * nv + e]).astype(BF16)
            kdt_ref[e, :, rows] = kdec_s[n * nv + e].T.astype(BF16)
    for n, rows in enumerate(pairs):
        for e in range(nv):
            i = n * nv + e
            neg_a = jnp.where(strict, -(_dot(kb_s[i], kn_s[n], NT) * dec_s[i]), 0.0)
            p_s[i] = neg_a.astype(BF16)
            t_s[i] = eye_f + neg_a
    _inverse_stages(p_s, t_s, npairs * nv, c)
    for n, rows in enumerate(pairs):
        for e in range(nv):
            sol = _dot(t_s[n * nv + e].astype(BF16), rhs_s[n * nv + e])
            u_ref[e, rows, :] = sol[:, :dk]
            w_ref[e, rows, :] = sol[:, dk:].astype(BF16)


def _gdn_scan_kernel(w_ref, qg_ref, u_ref, att_ref, kdt_ref, dec_ref, z_ref, g_ref, o_ref,
                     s_ref, ms_s, vp_s, *, group):
    c = CHUNK
    dk = C_HEAD_DIM
    zeros = jnp.zeros((c, dk), BF16)

    @pl.when(pl.program_id(2) == 0)
    def _():
        s_ref[...] = jnp.zeros(s_ref.shape, F32)

    for n in range(w_ref.shape[1] // c):
        rows = slice(n * c, (n + 1) * c)
        pair = slice((n // 2) * 2 * c, (n // 2 + 1) * 2 * c)
        for gi in range(group):
            lhs = jnp.concatenate([w_ref[gi, rows, :], qg_ref[gi, rows, :]], axis=0)
            ms_s[gi] = _dot(lhs, s_ref[gi].astype(BF16))
        for gi in range(group):
            v_new = (u_ref[gi, rows, :] - ms_s[gi, :c, :]).astype(BF16)
            vp_s[gi] = jnp.concatenate([v_new, zeros] if n % 2 == 0 else [zeros, v_new], axis=0)
        for gi in range(group):
            s_ref[gi] = s_ref[gi] * dec_ref[gi, n:n + 1, :] + _dot(kdt_ref[gi, :, pair], vp_s[gi])
        for gi in range(group):
            o = ms_s[gi, c:, :] + _dot(att_ref[gi, rows, :], vp_s[gi])
            o = o * lax.rsqrt(jnp.mean(o * o, axis=-1, keepdims=True) + NORM_EPS) * g_ref[...]
            z = z_ref[rows, gi * dk:(gi + 1) * dk].astype(F32)
            o_ref[rows, gi * dk:(gi + 1) * dk] = (o * _silu(z)).astype(o_ref.dtype)


def gated_deltanet_layer(x, hn, w_in, conv_w, a_log, dt_bias, norm_g, w_out, batch, seq):
    m, d = x.shape
    dk = C_HEAD_DIM
    k_heads = d // dk
    v_heads = 2 * k_heads
    conv_ch = 2 * k_heads * dk + v_heads * dk
    main_w = conv_ch + v_heads * dk
    qkv = gdn_conv_proj(hn, w_in[:, :conv_ch].astype(BF16), conv_w, seq)
    z = matmul(hn, w_in[:, conv_ch:main_w].astype(BF16), BF16, name="gdn_gate_proj")
    pad = jnp.zeros((d, LANES - v_heads), F32)
    w_ba = jnp.concatenate([w_in[:, main_w:main_w + v_heads], pad, w_in[:, main_w + v_heads:], pad], axis=1)
    row_pad = jnp.zeros((LANES - v_heads,), F32)
    alog_row = jnp.concatenate([a_log, row_pad]).reshape(1, LANES)
    dtb_row = jnp.concatenate([dt_bias, row_pad]).reshape(1, LANES)
    beta_all, gc_all, gc_t = gdn_gates(hn, w_ba.astype(BF16), alog_row, dtb_row)

    rows = min(PREP_ROWS, seq)
    nr = seq // rows
    ncr = rows // CHUNK
    hv = v_heads
    nv = v_heads // k_heads
    nchain = nv * rows // (2 * CHUNK)
    npair = rows // (2 * CHUNK)
    bh_t = lambda dt, last: jax.ShapeDtypeStruct((batch, hv, seq, last), dt)
    blk4 = lambda last: pl.BlockSpec((None, nv, rows, last), lambda b, h, i: (b, h, i, 0))
    qoff, koff, voff = 0, k_heads, 2 * k_heads // nv
    w_, qg_, u_, att_, kdt_, dec_ = pl.pallas_call(
        _gdn_prep_kernel,
        out_shape=(bh_t(BF16, dk), bh_t(BF16, dk), bh_t(F32, dk), bh_t(BF16, 2 * CHUNK),
                   jax.ShapeDtypeStruct((batch, hv, dk, seq), BF16),
                   jax.ShapeDtypeStruct((batch, hv, seq // CHUNK, dk), F32)),
        grid=(batch, k_heads, nr),
        in_specs=[pl.BlockSpec((rows, dk), lambda b, h, i: (b * nr + i, qoff + h)),
                  pl.BlockSpec((rows, dk), lambda b, h, i: (b * nr + i, koff + h)),
                  pl.BlockSpec((rows, nv * dk), lambda b, h, i: (b * nr + i, voff + h)),
                  pl.BlockSpec((rows, LANES), lambda b, h, i: (b * nr + i, 0)),
                  pl.BlockSpec((rows, LANES), lambda b, h, i: (b * nr + i, 0)),
                  pl.BlockSpec((LANES, rows), lambda b, h, i: (0, b * nr + i))],
        out_specs=(blk4(dk), blk4(dk), blk4(dk), blk4(2 * CHUNK),
                   pl.BlockSpec((None, nv, dk, rows), lambda b, h, i: (b, h, 0, i)),
                   pl.BlockSpec((None, nv, ncr, dk), lambda b, h, i: (b, h, i, 0))),
        scratch_shapes=[pltpu.VMEM((nchain, dk, dk), F32), pltpu.VMEM((nchain, dk, dk), F32),
                        pltpu.VMEM((npair, dk, dk), BF16), pltpu.VMEM((nchain, dk, dk), BF16),
                        pltpu.VMEM((nchain, dk, 2 * dk), BF16), pltpu.VMEM((nchain, dk, dk), BF16),
                        pltpu.VMEM((nchain, dk, dk), F32), pltpu.VMEM((npair, dk, dk), BF16),
                        pltpu.VMEM((nchain, dk, dk), F32)],
        compiler_params=_params("parallel", "parallel", "parallel"),
        name="gdn_prep",
    )(qkv, qkv, qkv, beta_all, gc_all, gc_t)

    rows = min(SCAN_ROWS, seq)
    nr = seq // rows
    ncr = rows // CHUNK
    group = GDN_SCAN_HEADS
    gblk = lambda last: pl.BlockSpec((None, group, rows, last), lambda b, h, i: (b, h, i, 0))
    o = pl.pallas_call(
        functools.partial(_gdn_scan_kernel, group=group),
        out_shape=jax.ShapeDtypeStruct((m, hv * dk), BF16),
        grid=(batch, hv // group, nr),
        in_specs=[gblk(dk), gblk(dk), gblk(dk), gblk(2 * CHUNK),
                  pl.BlockSpec((None, group, dk, rows), lambda b, h, i: (b, h, 0, i)),
                  pl.BlockSpec((None, group, ncr, dk), lambda b, h, i: (b, h, i, 0)),
                  pl.BlockSpec((rows, group * dk), lambda b, h, i: (b * nr + i, h)),
                  pl.BlockSpec((1, dk), lambda b, h, i: (0, 0))],
        out_specs=pl.BlockSpec((rows, group * dk), lambda b, h, i: (b * nr + i, h)),
        scratch_shapes=[pltpu.VMEM((group, dk, dk), F32), pltpu.VMEM((group, 2 * CHUNK, dk), F32),
                        pltpu.VMEM((group, 2 * CHUNK, dk), BF16)],
        compiler_params=_params("parallel", "parallel", "arbitrary"),
        name="gdn_scan",
    )(w_, qg_, u_, att_, kdt_, dec_, z, norm_g.reshape(1, dk))
    return matmul_residual(o, w_out.astype(BF16), x, tm=MM_ROWS // 2, name="gdn_out_proj")


def _rwkv_mix_kernel(x_ref, g_ref, mu_ref, o_ref, tail_ref, *, rows_per_seq):
    i = pl.program_id(0)
    x = x_ref[...]
    tm = x.shape[0]
    hn = x * lax.rsqrt(jnp.mean(x * x, axis=-1, keepdims=True) + NORM_EPS) * g_ref[...]

    @pl.when((i * tm) % rows_per_seq == 0)
    def _():
        tail_ref[...] = jnp.zeros(tail_ref.shape, F32)

    prev = jnp.where(_iota(hn.shape, 0) == 0, tail_ref[7:8, :], pltpu.roll(hn, 1, 0))
    tail_ref[...] = hn[tm - 8:]
    xx = prev - hn
    for c in range(o_ref.shape[0]):
        o_ref[c] = (hn + xx * mu_ref[c:c + 1, :]).astype(o_ref.dtype)


def rwkv_token_mix(x, norm_g, mu, seq, *, tm=ROW_TILE // 2):
    m, d = x.shape
    tm = min(tm, seq)
    nmix = mu.shape[0]
    return pl.pallas_call(
        functools.partial(_rwkv_mix_kernel, rows_per_seq=seq),
        out_shape=jax.ShapeDtypeStruct((nmix, m, d), BF16),
        grid=(m // tm,),
        in_specs=[pl.BlockSpec((tm, d), lambda i: (i, 0)),
                  pl.BlockSpec((1, d), lambda i: (0, 0)),
                  pl.BlockSpec((nmix, d), lambda i: (0, 0))],
        out_specs=pl.BlockSpec((nmix, tm, d), lambda i: (0, i, 0)),
        scratch_shapes=[pltpu.VMEM((8, d), F32)],
        compiler_params=_params("arbitrary"),
        name="rwkv_token_mix",
    )(x, norm_g.reshape(1, d), mu)


def _grouped_mm_kernel(a_ref, w_ref, o_ref):
    o_ref[...] = _dot(a_ref[...], w_ref[...]).astype(o_ref.dtype)


def grouped_matmul(a, w, out_dtype, *, tm=MM_ROWS, tn=MM_COLS):
    g, k, n = w.shape
    m = a.shape[1]
    tm, tn = min(tm, m), min(tn, n)
    return pl.pallas_call(
        _grouped_mm_kernel,
        out_shape=jax.ShapeDtypeStruct((g, m, n), out_dtype),
        grid=(g, n // tn, m // tm),
        in_specs=[pl.BlockSpec((None, tm, k), lambda c, j, i: (c, i, 0)),
                  pl.BlockSpec((None, k, tn), lambda c, j, i: (c, 0, j))],
        out_specs=pl.BlockSpec((None, tm, tn), lambda c, j, i: (c, i, j)),
        compiler_params=_params("parallel", "parallel", "parallel"),
        name="rwkv_rkvg_proj",
    )(a, w)


def _rwkv_lora_kernel(xw_ref, xa_ref, w1_ref, w2_ref, a1_ref, a2_ref, w0_ref, a0_ref, ld_ref, a_ref):
    hw = jnp.tanh(_dot(xw_ref[...], w1_ref[...])).astype(BF16)
    lw = w0_ref[...] + _dot(hw, w2_ref[...])
    log_w = -_softplus(-lw) - 0.5
    ld_ref[...] = -jnp.exp(log_w)
    ha = _dot(xa_ref[...], a1_ref[...]).astype(BF16)
    a_ref[...] = jax.nn.sigmoid(a0_ref[...] + _dot(ha, a2_ref[...]))


def rwkv_lora(xs, w1, w2, a1, a2, w0, a0, *, tm=ROW_TILE):
    _, m, d = xs.shape
    tm = min(tm, m)
    r = w1.shape[1]
    row = pl.BlockSpec((1, d), lambda i: (0, 0))
    shape = jax.ShapeDtypeStruct((m, d), F32)
    return pl.pallas_call(
        _rwkv_lora_kernel,
        out_shape=(shape, shape),
        grid=(m // tm,),
        in_specs=[pl.BlockSpec((None, tm, d), lambda i: (4, i, 0)),
                  pl.BlockSpec((None, tm, d), lambda i: (5, i, 0)),
                  pl.BlockSpec((d, r), lambda i: (0, 0)), pl.BlockSpec((r, d), lambda i: (0, 0)),
                  pl.BlockSpec((d, r), lambda i: (0, 0)), pl.BlockSpec((r, d), lambda i: (0, 0)),
                  row, row],
        out_specs=(pl.BlockSpec((tm, d), lambda i: (i, 0)), pl.BlockSpec((tm, d), lambda i: (i, 0))),
        compiler_params=_params("parallel"),
        name="rwkv_lora",
    )(xs, xs, w1, w2, a1, a2, w0.reshape(1, d), a0.reshape(1, d))


def _head_sum(x, first_head):
    tot = jnp.sum(x, axis=-1, keepdims=True)
    lo = jnp.sum(jnp.where(first_head, x, 0.0), axis=-1, keepdims=True)
    return jnp.where(first_head, lo, tot - lo)


def _rwkv_prep_kernel(r_ref, k_ref, v_ref, ld_ref, a_ref, kk_ref, ka_ref, rk_ref,
                      wt_ref, rt_ref, u_ref, aro_ref, kbt_ref, vb_ref, dec_ref, bonus_ref,
                      kkt_s, kb_s, akk_s, p_s, t_s, drow_s, *, cps):
    c = CHUNK
    n2 = 2 * c
    n_h = B_HEAD_DIM
    npairs = r_ref.shape[0] // n2
    first_head = _iota((1, LANES), 1) < n_h
    tri, strict, diag = _pair_masks()
    tri_b = tri.astype(BF16)
    eye_b = diag.astype(BF16)
    eye_f = diag.astype(F32)
    upper_half = _iota((n2, LANES), 0) < c
    pairs = [slice(n * n2, (n + 1) * n2) for n in range(npairs)]

    for n, rows in enumerate(pairs):
        r, k, v, a = r_ref[rows, :], k_ref[rows, :], v_ref[rows, :], a_ref[rows, :]
        kk_raw = k * kk_ref[...]
        kk = kk_raw * lax.rsqrt(_head_sum(kk_raw * kk_raw, first_head) + 1e-6)
        k2c = k * (1.0 + (a - 1.0) * ka_ref[...])
        bbc = kk * a
        bonus_ref[rows, :] = _head_sum(r * k2c * rk_ref[...], first_head) * v
        vb_ref[rows, :] = v.astype(BF16)
        ld = ld_ref[rows, :]
        cs = _dot_x2r(tri_b, ld)
        cs_last = jnp.where(upper_half, cs[c - 1:c, :], cs[n2 - 1:n2, :])
        w_inv = jnp.exp(-cs)
        w_end = jnp.exp(cs_last - cs)
        rt_ref[rows, :] = (r * jnp.exp(cs)).astype(BF16)
        kkt_s[n] = (kk * jnp.exp(cs - ld)).astype(BF16)
        kb_s[n] = jnp.concatenate([k2c * w_inv, bbc * w_inv], axis=0).astype(BF16)
        kw, bw = k2c * w_end, -(bbc * w_end)
        for half in range(2):
            hs = slice(half * c, (half + 1) * c)
            kb_end = jnp.concatenate([kw[hs], bw[hs]], axis=0)
            cols = slice((2 * n + half) * n2, (2 * n + half + 1) * n2)
            kbt_ref[:, cols] = kb_end.T.astype(BF16)
        blk, off = divmod(2 * n, cps)
        drow_s[blk, off:off + 1, :] = jnp.exp(cs[c - 1:c, :])
        drow_s[blk, off + 1:off + 2, :] = jnp.exp(cs[n2 - 1:n2, :])
    for blk in range(dec_ref.shape[0] // LANES):
        drow_s[blk, cps:, :] = jnp.zeros((LANES - cps, LANES), F32)
        dec_ref[blk * LANES:(blk + 1) * LANES, :] = _dot_xr(eye_b, drow_s[blk], NT)
    zero_b = jnp.zeros((n2, LANES), BF16)
    strict2 = jnp.concatenate([strict, strict], axis=1)
    tri2 = jnp.concatenate([tri, tri], axis=1)
    sign2 = jnp.where(_iota((n2, 2 * n2), 1) < n2, 1.0, -1.0)
    for n, rows in enumerate(pairs):
        kkt, rt = kkt_s[n], rt_ref[rows, :]
        lhs = jnp.concatenate([jnp.where(first_head, kkt, zero_b), jnp.where(first_head, rt, zero_b),
                               jnp.where(first_head, zero_b, kkt), jnp.where(first_head, zero_b, rt)], axis=0)
        gram = _dot(lhs, kb_s[n], NT)
        for hh in range(2):
            i = 2 * n + hh
            g_kk = jnp.where(strict2, gram[2 * hh * n2:(2 * hh + 1) * n2], 0.0)
            neg_a = -g_kk[:, n2:]
            p_s[i] = neg_a.astype(BF16)
            t_s[i] = eye_f + neg_a
            akk_s[i] = g_kk[:, :n2].astype(BF16)
            g_r = gram[(2 * hh + 1) * n2:(2 * hh + 2) * n2]
            aro_ref[hh, rows, :] = jnp.where(tri2, g_r * sign2, 0.0).astype(BF16)
    _inverse_stages(p_s, t_s, 2 * npairs, c)
    for n, rows in enumerate(pairs):
        av = _dot(jnp.concatenate([akk_s[2 * n], akk_s[2 * n + 1]], axis=0), vb_ref[rows, :])
        p_s[2 * n] = av[:n2].astype(BF16)
        p_s[2 * n + 1] = av[n2:].astype(BF16)
    for n, rows in enumerate(pairs):
        t0, t1 = t_s[2 * n].astype(BF16), t_s[2 * n + 1].astype(BF16)
        wt = _dot(jnp.concatenate([t0, t1], axis=0), kkt_s[n])
        wt_ref[rows, :] = jnp.where(first_head, wt[:n2], wt[n2:]).astype(BF16)
        u_ref[rows, :] = jnp.where(first_head, _dot(t0, p_s[2 * n]), _dot(t1, p_s[2 * n + 1]))


def _rwkv_scan_kernel(wt_ref, rt_ref, u_ref, aro_ref, kbt_ref, vb_ref, dec_ref, bonus_ref, gate_ref,
                      lnw_ref, lnb_ref, o_ref, s_ref, ms_s, xc_s, xp_s, *, group):
    c = CHUNK
    n2 = 2 * c
    n_h = B_HEAD_DIM
    first_head = _iota((1, LANES), 1) < n_h
    same_head = _iota((LANES, LANES), 0) // n_h == _iota((LANES, LANES), 1) // n_h
    zeros = jnp.zeros((c, LANES), BF16)

    def head_mean(x):
        return _head_sum(x, first_head) * (1.0 / n_h)

    @pl.when(pl.program_id(2) == 0)
    def _():
        s_ref[...] = jnp.zeros(s_ref.shape, F32)

    for n in range(u_ref.shape[1] // c):
        rows = slice(n * c, (n + 1) * c)
        for gi in range(group):
            lhs = jnp.concatenate([wt_ref[gi, rows, :], rt_ref[gi, rows, :]], axis=0)
            ms_s[gi] = _dot(lhs, s_ref[gi].astype(BF16))
        for gi in range(group):
            sa = (u_ref[gi, rows, :] + ms_s[gi, :c, :]).astype(BF16)
            vb = vb_ref[gi, rows, :]
            xc_s[gi] = jnp.concatenate([vb, sa], axis=0)
            xp_s[gi] = jnp.concatenate([vb, zeros, sa, zeros] if n % 2 == 0 else [zeros, vb, zeros, sa], axis=0)
        for gi in range(group):
            upd = _dot(kbt_ref[gi, :, n * n2:(n + 1) * n2], xc_s[gi])
            dec_col = jnp.broadcast_to(dec_ref[gi, :, n:n + 1], (LANES, LANES))
            s_ref[gi] = s_ref[gi] * dec_col + jnp.where(same_head, upd, 0.0)
        for gi in range(group):
            both = _dot(jnp.concatenate([aro_ref[gi, 0, rows, :], aro_ref[gi, 1, rows, :]], axis=0), xp_s[gi])
            o = ms_s[gi, c:, :] + jnp.where(first_head, both[:c], both[c:])
            mean = head_mean(o)
            dlt = o - mean
            var = head_mean(dlt * dlt)
            cols = slice(gi * LANES, (gi + 1) * LANES)
            y = dlt * lax.rsqrt(var + B_GN_EPS) * lnw_ref[:, cols] + lnb_ref[:, cols]
            y = y + bonus_ref[rows, cols]
            o_ref[rows, cols] = (y * _silu(gate_ref[rows, cols])).astype(o_ref.dtype)


def rwkv7_layer(x, norm_g, mu, w_rkvg, w0, w_w1, w_w2, a0, w_a1, w_a2, k_k, k_a, r_k, ln_w, ln_b, w_out,
                batch, seq):
    m, d = x.shape
    heads = d // B_HEAD_DIM
    pairs = heads // 2
    order = jnp.array([0, 2, 3, 5, 1, 4])
    xs = rwkv_token_mix(x, norm_g, mu[order], seq)
    rkvg = grouped_matmul(xs, w_rkvg.astype(BF16), F32)
    lora = w_w1.shape[1]
    padc = lambda w: jnp.pad(w, ((0, 0), (0, LORA_PAD - lora))).astype(BF16)
    padr = lambda w: jnp.pad(w, ((0, LORA_PAD - lora), (0, 0))).astype(BF16)
    ld, a = rwkv_lora(xs, padc(w_w1), padr(w_w2), padc(w_a1), padr(w_a2), w0, a0)

    srows = min(SCAN_ROWS, seq)
    rows = min(PREP_ROWS, seq)
    nr = seq // rows
    npair = rows // (2 * CHUNK)
    col = lambda g: pl.BlockSpec((None, rows, LANES), lambda b, p, i, g=g: (g, b * nr + i, p))
    flat = pl.BlockSpec((rows, LANES), lambda b, p, i: (b * nr + i, p))
    prow = pl.BlockSpec((1, LANES), lambda b, p, i: (0, p))
    bp = lambda rws, last, dt: jax.ShapeDtypeStruct((batch, pairs, rws, last), dt)
    pblk = lambda rws, last: pl.BlockSpec((None, None, rws, last), lambda b, p, i: (b, p, i, 0))
    wt_, rt_, u_, aro_, kbt_, vb_, dec_, bonus = pl.pallas_call(
        functools.partial(_rwkv_prep_kernel, cps=srows // CHUNK),
        out_shape=(bp(seq, LANES, BF16), bp(seq, LANES, BF16), bp(seq, LANES, F32),
                   jax.ShapeDtypeStruct((batch, pairs, 2, seq, 4 * CHUNK), BF16),
                   bp(LANES, 2 * seq, BF16), bp(seq, LANES, BF16), bp(seq // srows * LANES, LANES, F32),
                   jax.ShapeDtypeStruct((m, d), F32)),
        grid=(batch, pairs, nr),
        in_specs=[col(0), col(1), col(2), flat, flat, prow, prow, prow],
        out_specs=(pblk(rows, LANES), pblk(rows, LANES), pblk(rows, LANES),
                   pl.BlockSpec((None, None, 2, rows, 4 * CHUNK), lambda b, p, i: (b, p, 0, i, 0)),
                   pl.BlockSpec((None, None, LANES, 2 * rows), lambda b, p, i: (b, p, 0, i)),
                   pblk(rows, LANES), pblk(rows // srows * LANES, LANES), flat),
        scratch_shapes=[pltpu.VMEM((npair, LANES, LANES), BF16), pltpu.VMEM((npair, 2 * LANES, LANES), BF16),
                        pltpu.VMEM((2 * npair, LANES, LANES), BF16), pltpu.VMEM((2 * npair, LANES, LANES), BF16),
                        pltpu.VMEM((2 * npair, LANES, LANES), F32),
                        pltpu.VMEM((rows // srows, LANES, LANES), F32)],
        compiler_params=_params("parallel", "parallel", "parallel"),
        name="rwkv_prep",
    )(rkvg, rkvg, rkvg, ld, a, k_k.reshape(1, d), k_a.reshape(1, d), r_k.reshape(1, d))

    rows = srows
    nr = seq // rows
    group = RWKV_SCAN_PAIRS
    gw = group * LANES
    gblk = lambda rws, last: pl.BlockSpec((None, group, rws, last), lambda b, p, i: (b, p, i, 0))
    gflat = pl.BlockSpec((rows, gw), lambda b, p, i: (b * nr + i, p))
    grow = pl.BlockSpec((1, gw), lambda b, p, i: (0, p))
    o = pl.pallas_call(
        functools.partial(_rwkv_scan_kernel, group=group),
        out_shape=jax.ShapeDtypeStruct((m, d), BF16),
        grid=(batch, pairs // group, nr),
        in_specs=[gblk(rows, LANES), gblk(rows, LANES), gblk(rows, LANES),
                  pl.BlockSpec((None, group, 2, rows, 4 * CHUNK), lambda b, p, i: (b, p, 0, i, 0)),
                  pl.BlockSpec((None, group, LANES, 2 * rows), lambda b, p, i: (b, p, 0, i)),
                  gblk(rows, LANES), gblk(LANES, LANES), gflat,
                  pl.BlockSpec((None, rows, gw), lambda b, p, i: (3, b * nr + i, p)),
                  grow, grow],
        out_specs=gflat,
        scratch_shapes=[pltpu.VMEM((group, LANES, LANES), F32), pltpu.VMEM((group, 2 * CHUNK, LANES), F32),
                        pltpu.VMEM((group, 2 * CHUNK, LANES), BF16), pltpu.VMEM((group, 4 * CHUNK, LANES), BF16)],
        compiler_params=_params("parallel", "parallel", "arbitrary"),
        name="rwkv_scan",
    )(wt_, rt_, u_, aro_, kbt_, vb_, dec_, bonus, rkvg, ln_w.reshape(1, d), ln_b.reshape(1, d))
    return matmul_residual(o, w_out.astype(BF16), x, name="rwkv_out_proj")


def kernel(x, p, positions, norm_g, pe_norm_g, pe_w_gate, pe_w_proj, final_norm_g, a_w_in, a_lam, a_subln_g, a_w_out, b_mu, b_w_rkvg, b_w0, b_w_w1, b_w_w2, b_a0, b_w_a1, b_w_a2, b_k_k, b_k_a, b_r_k, b_ln_w, b_ln_b, b_w_out, c_w_in, c_conv_w, c_A_log, c_dt_bias, c_norm_g, c_w_out):
    batch, seq, d = x.shape
    depth = p.shape[0]
    m = batch * seq
    xf = x.reshape(m, d)
    tables = rope_tables(positions)
    for i in range(depth):
        kind = i % N_MIXERS
        j = i // N_MIXERS
        if kind == 0:
            lam_init = 0.8 - 0.6 * math.exp(-0.3 * i)
            hn = rmsnorm(xf, norm_g[i], BF16)
            xf = diff_attention_layer(xf, hn, tables, a_w_in[j], a_lam[j], a_subln_g[j], a_w_out[j],
                                      batch, seq, lam_init)
        elif kind == 1:
            xf = rwkv7_layer(xf, norm_g[i], b_mu[j], b_w_rkvg[j], b_w0[j], b_w_w1[j], b_w_w2[j], b_a0[j],
                             b_w_a1[j], b_w_a2[j], b_k_k[j], b_k_a[j], b_r_k[j], b_ln_w[j], b_ln_b[j],
                             b_w_out[j], batch, seq)
        else:
            hn = rmsnorm(xf, norm_g[i], BF16)
            xf = gated_deltanet_layer(xf, hn, c_w_in[j], c_conv_w[j], c_A_log[j], c_dt_bias[j], c_norm_g[j],
                                      c_w_out[j], batch, seq)
        hn2 = rmsnorm(xf, pe_norm_g[i], BF16)
        xf = per_layer_embedding(xf, hn2, pe_w_gate[i].astype(BF16), p[i].reshape(m, -1),
                                 pe_w_proj[i].astype(BF16))
    return rmsnorm(xf, final_norm_g, F32).reshape(batch, seq, d)
```

```python
import functools
import math

import jax
import jax.numpy as jnp
from jax import lax
from jax.experimental import pallas as pl
from jax.experimental.pallas import tpu as pltpu

F32 = jnp.float32
BF16 = jnp.bfloat16

N_MIXERS = 3
NORM_EPS = 1e-6
LANES = 128
VMEM_LIMIT = 48 * 1024 * 1024

A_HEAD_DIM = 128
A_V_DIM = 2 * A_HEAD_DIM
ROT_DIM = A_HEAD_DIM // 4
ROPE_THETA = 500000.0
SUBLN_EPS = 1e-5
ATTN_BLOCK = 512
ATTN_Q_BLOCKS = 1
ONES_ROWS = 16

B_HEAD_DIM = 64
B_GN_EPS = 64e-5
LORA_PAD = 128

C_HEAD_DIM = 128
C_CONV_WIDTH = 4
CHUNK = 64
SCAN_ROWS = 512
PREP_ROWS = 1024
SLAB_ROWS = 32
GDN_SCAN_HEADS = 16
RWKV_SCAN_PAIRS = 8

MM_ROWS = 1024
MM_COLS = 1024
ROW_TILE = 512
EPI_COLS = 2048
EPI_SUB = 256

NN = (((1,), (0,)), ((), ()))
NT = (((1,), (1,)), ((), ()))


def _dot(a, b, dims=NN):
    return lax.dot_general(a, b, dims, preferred_element_type=F32)


def _split2(x):
    hi = x.astype(BF16)
    lo = (x - hi.astype(F32)).astype(BF16)
    return hi, lo


def _split3(x):
    hi = x.astype(BF16)
    r = x - hi.astype(F32)
    mid = r.astype(BF16)
    lo = (r - mid.astype(F32)).astype(BF16)
    return hi, mid, lo


def _dot_xr(a_exact, b, dims=NN):
    h, m, l = _split3(b)
    return _dot(a_exact, h, dims) + (_dot(a_exact, m, dims) + _dot(a_exact, l, dims))


def _dot_x2r(a_exact, b, dims=NN):
    h, l = _split2(b)
    return _dot(a_exact, h, dims) + _dot(a_exact, l, dims)


def _iota(shape, dim):
    return lax.broadcasted_iota(jnp.int32, shape, dim)


def _silu(x):
    return x * jax.nn.sigmoid(x)


def _softplus(x):
    return jnp.maximum(x, 0.0) + jnp.log(1.0 + jnp.exp(-jnp.abs(x)))


def _params(*sem):
    return pltpu.CompilerParams(dimension_semantics=sem, vmem_limit_bytes=VMEM_LIMIT)


def _rmsnorm_kernel(x_ref, g_ref, o_ref, *, eps):
    x = x_ref[...]
    y = x * lax.rsqrt(jnp.mean(x * x, axis=-1, keepdims=True) + eps)
    o_ref[...] = (y * g_ref[...]).astype(o_ref.dtype)


def rmsnorm(x, g, out_dtype, *, eps=NORM_EPS, tm=ROW_TILE):
    m, d = x.shape
    tm = min(tm, m)
    return pl.pallas_call(
        functools.partial(_rmsnorm_kernel, eps=eps),
        out_shape=jax.ShapeDtypeStruct((m, d), out_dtype),
        grid=(m // tm,),
        in_specs=[pl.BlockSpec((tm, d), lambda i: (i, 0)), pl.BlockSpec((1, d), lambda i: (0, 0))],
        out_specs=pl.BlockSpec((tm, d), lambda i: (i, 0)),
        compiler_params=_params("parallel"),
        name="rmsnorm",
    )(x, g.reshape(1, d))


def _mm_kernel(a_ref, w_ref, *rest, epilogue):
    o_ref = rest[-1]
    acc = _dot(a_ref[...], w_ref[...])
    if epilogue is not None:
        acc = epilogue(acc, *rest[:-1])
    o_ref[...] = acc.astype(o_ref.dtype)


def matmul(a, w, out_dtype, *, tm=MM_ROWS, tn=MM_COLS, extra=(), extra_specs=(), epilogue=None, name="matmul"):
    m, k = a.shape
    n = w.shape[1]
    tm, tn = min(tm, m), min(tn, n)
    return pl.pallas_call(
        functools.partial(_mm_kernel, epilogue=epilogue),
        out_shape=jax.ShapeDtypeStruct((m, n), out_dtype),
        grid=(n // tn, m // tm),
        in_specs=[pl.BlockSpec((tm, k), lambda j, i: (i, 0)),
                  pl.BlockSpec((k, tn), lambda j, i: (0, j))] + list(extra_specs),
        out_specs=pl.BlockSpec((tm, tn), lambda j, i: (i, j)),
        compiler_params=_params("parallel", "parallel"),
        name=name,
    )(a, w, *extra)


def matmul_residual(a, w, res, *, tm=MM_ROWS, tn=MM_COLS, name="matmul_residual"):
    tm, tn = min(tm, a.shape[0]), min(tn, w.shape[1])
    return matmul(a, w, F32, tm=tm, tn=tn, extra=(res,),
                  extra_specs=(pl.BlockSpec((tm, tn), lambda j, i: (i, j)),),
                  epilogue=lambda acc, r_ref: r_ref[...] + acc, name=name)


def _ple_kernel(h_ref, wg_ref, p_ref, wp_ref, x_ref, o_ref):
    gate = jax.nn.sigmoid(_dot(h_ref[...], wg_ref[...]))
    proj = _dot(p_ref[...].astype(BF16), wp_ref[...])
    o_ref[...] = x_ref[...] + gate * proj


def per_layer_embedding(x, hn, w_gate, p, w_proj, *, tm=MM_ROWS, tn=MM_COLS):
    m, d = x.shape
    pd = p.shape[1]
    tm, tn = min(tm, m), min(tn, d)
    return pl.pallas_call(
        _ple_kernel,
        out_shape=jax.ShapeDtypeStruct((m, d), F32),
        grid=(d // tn, m // tm),
        in_specs=[pl.BlockSpec((tm, d), lambda j, i: (i, 0)),
                  pl.BlockSpec((d, tn), lambda j, i: (0, j)),
                  pl.BlockSpec((tm, pd), lambda j, i: (i, 0)),
                  pl.BlockSpec((pd, tn), lambda j, i: (0, j)),
                  pl.BlockSpec((tm, tn), lambda j, i: (i, j))],
        out_specs=pl.BlockSpec((tm, tn), lambda j, i: (i, j)),
        compiler_params=_params("parallel", "parallel"),
        name="per_layer_embedding",
    )(hn, w_gate, p, w_proj, x)


def _rope_table_kernel(pos_ref, freq_ref, cos_ref, sin_lo_ref, sin_hi_ref):
    half = ROT_DIM // 2
    ang = pos_ref[...].astype(F32) * freq_ref[...]
    lane = _iota(ang.shape, 1)
    c, s = jnp.cos(ang), jnp.sin(ang)
    cos_ref[...] = jnp.where(lane < ROT_DIM, c, 1.0)
    sin_lo_ref[...] = jnp.where(lane < half, -s, 0.0)
    sin_hi_ref[...] = jnp.where((lane >= half) & (lane < ROT_DIM), s, 0.0)


def rope_tables(positions, *, tm=MM_ROWS):
    m = positions.size
    tm = min(tm, m)
    inv_freq = ROPE_THETA ** (-jnp.arange(0, ROT_DIM, 2, dtype=F32) / ROT_DIM)
    freq_row = jnp.concatenate([inv_freq, inv_freq, jnp.zeros((LANES - ROT_DIM,), F32)]).reshape(1, LANES)
    shape = jax.ShapeDtypeStruct((m, LANES), F32)
    spec = pl.BlockSpec((tm, LANES), lambda i: (i, 0))
    return pl.pallas_call(
        _rope_table_kernel,
        out_shape=(shape, shape, shape),
        grid=(m // tm,),
        in_specs=[pl.BlockSpec((tm, 1), lambda i: (i, 0)), pl.BlockSpec((1, LANES), lambda i: (0, 0))],
        out_specs=(spec, spec, spec),
        compiler_params=_params("parallel"),
        name="rope_tables",
    )(positions.reshape(m, 1), freq_row)


def _attn_qk_kernel(a_ref, w_ref, cos_ref, sin_lo_ref, sin_hi_ref, o_ref, *, n_q_blocks, sub):
    j = pl.program_id(0)
    half = ROT_DIM // 2
    scale = jnp.where(j < n_q_blocks, A_HEAD_DIM ** -0.5, 1.0).astype(F32)
    cos, sin_lo, sin_hi = cos_ref[...], sin_lo_ref[...], sin_hi_ref[...]
    a = a_ref[...]

    def epilogue(acc, c0):
        for g in range(acc.shape[1] // LANES):
            x = acc[:, g * LANES:(g + 1) * LANES]
            y = x * cos + pltpu.roll(x, LANES - half, 1) * sin_lo + pltpu.roll(x, half, 1) * sin_hi
            o_ref[:, c0 + g * LANES:c0 + (g + 1) * LANES] = (y * scale).astype(o_ref.dtype)

    starts = list(range(0, w_ref.shape[1], sub))
    acc = _dot(a, w_ref[:, starts[0]:starts[0] + sub])
    for c, c0 in enumerate(starts):
        nxt = _dot(a, w_ref[:, starts[c + 1]:starts[c + 1] + sub]) if c + 1 < len(starts) else None
        epilogue(acc, c0)
        acc = nxt


def attn_qk_proj(hn, w_qk, tables, *, tm=MM_ROWS, tn=EPI_COLS, sub=EPI_SUB):
    m, k = hn.shape
    n = w_qk.shape[1]
    tm, tn = min(tm, m), min(tn, n // 2)
    tspec = pl.BlockSpec((tm, LANES), lambda j, i: (i, 0))
    return pl.pallas_call(
        functools.partial(_attn_qk_kernel, n_q_blocks=n // 2 // tn, sub=min(sub, tn)),
        out_shape=jax.ShapeDtypeStruct((m, n), BF16),
        grid=(n // tn, m // tm),
        in_specs=[pl.BlockSpec((tm, k), lambda j, i: (i, 0)),
                  pl.BlockSpec((k, tn), lambda j, i: (0, j)), tspec, tspec, tspec],
        out_specs=pl.BlockSpec((tm, tn), lambda j, i: (i, j)),
        compiler_params=_params("parallel", "parallel"),
        name="attn_qk_proj",
    )(hn, w_qk, *tables)


def _attn_vt_kernel(w_ref, a_ref, o_ref):
    o_ref[...] = _dot(w_ref[...], a_ref[...], NT).astype(o_ref.dtype)


def attn_v_proj_t(hn, w_t, batch, seq, blk, *, tn=MM_COLS):
    m, k = hn.shape
    n = w_t.shape[0]
    tn = min(tn, n)
    nk = seq // blk
    return pl.pallas_call(
        _attn_vt_kernel,
        out_shape=jax.ShapeDtypeStruct((batch, nk, n, blk), BF16),
        grid=(n // tn, m // blk),
        in_specs=[pl.BlockSpec((tn, k), lambda j, i: (j, 0)),
                  pl.BlockSpec((blk, k), lambda j, i: (i, 0))],
        out_specs=pl.BlockSpec((None, None, tn, blk), lambda j, i: (i // nk, i % nk, j, 0)),
        compiler_params=_params("parallel", "parallel"),
        name="attn_v_proj_t",
    )(w_t, hn)


def _diff_attn_kernel(lam_ref, q_ref, k_ref, vt_ref, z_ref, g_ref, o_ref, m_ref, acc_ref, s_ref,
                      *, bq, bk, lam_init):
    i = pl.program_id(2)
    lam = lam_ref[...]
    lam_full = (jnp.exp(jnp.sum(lam[0:1] * lam[1:2], axis=-1, keepdims=True))
                - jnp.exp(jnp.sum(lam[2:3] * lam[3:4], axis=-1, keepdims=True)) + lam_init)
    m_ref[...] = jnp.full(m_ref.shape, -jnp.inf, F32)
    acc_ref[...] = jnp.zeros(acc_ref.shape, F32)
    q = q_ref[...]
    d = A_HEAD_DIM
    w = A_V_DIM
    ones = jnp.ones((ONES_ROWS, bk), BF16)

    def scores(j, slot, q_lo=0):
        start = pl.multiple_of(j * bk, bk)
        kb = k_ref[pl.ds(start, bk), :]
        for c in range(2):
            s_ref[slot, c, :, q_lo:] = _dot(kb[:, c * d:(c + 1) * d], q[q_lo:, c * d:(c + 1) * d], NT)

    def absorb(j, slot, masked, q_lo=0):
        vt = jnp.concatenate([vt_ref[j], ones], axis=0)
        for c in range(2):
            s = s_ref[slot, c, :, q_lo:]
            if masked:
                kv_pos = j * bk + _iota(s.shape, 0)
                q_pos = i * bq + q_lo + _iota(s.shape, 1)
                s = jnp.where(kv_pos <= q_pos, s, -jnp.inf)
            m_prev = m_ref[c, :, q_lo:]
            m_new = jnp.maximum(m_prev, jnp.max(s, axis=0, keepdims=True))
            alpha = jnp.exp(m_prev - m_new)
            p = jnp.exp(s - m_new)
            acc_ref[c, :, q_lo:] = alpha * acc_ref[c, :, q_lo:] + _dot(vt, p.astype(BF16))
            m_ref[c, :, q_lo:] = m_new

    r = bq // bk
    first_masked = r * i
    scores(0, 0)

    def body(jj, carry):
        scores(2 * jj + 1, 1)
        absorb(2 * jj, 0, False)
        scores(2 * jj + 2, 0)
        absorb(2 * jj + 1, 1, False)
        return carry

    lax.fori_loop(0, first_masked // 2, body, 0)

    def tail(odd):
        base = first_masked - odd
        left = [(base + n, n % 2, n >= odd, max(n - odd, 0) * bk) for n in range(odd + r)]
        for n, (j, slot, masked, q_lo) in enumerate(left):
            if n + 1 < len(left):
                nj, nslot, _, nq_lo = left[n + 1]
                scores(nj, nslot, nq_lo)
            absorb(j, slot, masked, q_lo)

    if r % 2 == 0:
        tail(0)
    else:
        pl.when(first_masked % 2 == 0)(lambda: tail(0))
        pl.when(first_masked % 2 == 1)(lambda: tail(1))

    o = (acc_ref[0, :w, :] / acc_ref[0, w:w + 1, :]
         - lam_full * (acc_ref[1, :w, :] / acc_ref[1, w:w + 1, :]))
    o = o * lax.rsqrt(jnp.mean(o * o, axis=0, keepdims=True) + SUBLN_EPS) * g_ref[...]
    o = (o * (1.0 - lam_init)).T
    o_ref[...] = (o * _silu(z_ref[...].astype(F32))).astype(o_ref.dtype)


def diff_attention_core(qk, z, vt, lam, subln_g, batch, seq, heads, lam_init):
    m = qk.shape[0]
    bk = vt.shape[-1]
    bq = ATTN_Q_BLOCKS * bk
    nq = seq // bq
    nk = seq // bk
    w = A_V_DIM
    return pl.pallas_call(
        functools.partial(_diff_attn_kernel, bq=bq, bk=bk, lam_init=lam_init),
        out_shape=jax.ShapeDtypeStruct((m, heads * w), BF16),
        grid=(batch, heads, nq),
        in_specs=[pl.BlockSpec((4, A_HEAD_DIM), lambda b, h, i: (0, 0)),
                  pl.BlockSpec((bq, w), lambda b, h, i: (b * nq + i, h)),
                  pl.BlockSpec((seq, w), lambda b, h, i: (b, heads + h)),
                  pl.BlockSpec((None, nk, w, bk), lambda b, h, i: (b, 0, h, 0)),
                  pl.BlockSpec((bq, w), lambda b, h, i: (b * nq + i, h)),
                  pl.BlockSpec((w, 1), lambda b, h, i: (0, 0))],
        out_specs=pl.BlockSpec((bq, w), lambda b, h, i: (b * nq + i, h)),
        scratch_shapes=[pltpu.VMEM((2, 1, bq), F32), pltpu.VMEM((2, w + ONES_ROWS, bq), F32),
                        pltpu.VMEM((2, 2, bk, bq), F32)],
        compiler_params=_params("parallel", "parallel", "parallel"),
        name="diff_attention",
    )(lam, qk, qk, vt, z, subln_g.reshape(w, 1))


def diff_attention_layer(x, hn, tables, w_in, lam, subln_g, w_out, batch, seq, lam_init):
    d = x.shape[1]
    heads = d // A_V_DIM
    qk_w = heads * 2 * A_HEAD_DIM
    v_w = heads * A_V_DIM
    blk = min(ATTN_BLOCK, seq // ATTN_Q_BLOCKS)
    w_vt = w_in[:, 2 * qk_w:2 * qk_w + v_w].T.astype(BF16)
    qk = attn_qk_proj(hn, w_in[:, :2 * qk_w].astype(BF16), tables)
    z = matmul(hn, w_in[:, 2 * qk_w + v_w:].astype(BF16), BF16, name="attn_gate_proj")
    vt = attn_v_proj_t(hn, w_vt, batch, seq, blk)
    o = diff_attention_core(qk, z, vt, lam, subln_g, batch, seq, heads, lam_init)
    return matmul_residual(o, w_out.astype(BF16), x, name="attn_out_proj")


def _gdn_conv_kernel(a_ref, w_ref, cw_ref, o_ref, tail_ref, *, rows_per_seq, sub):
    i = pl.program_id(1)
    tm = a_ref.shape[0]

    @pl.when((i * tm) % rows_per_seq == 0)
    def _():
        tail_ref[...] = jnp.zeros(tail_ref.shape, F32)

    a = a_ref[...]
    sub_iota = _iota((8, sub), 0)
    last = C_CONV_WIDTH - 1

    def epilogue(acc, cols):
        tail = tail_ref[:, cols]

        def shifted(s):
            xs = pltpu.roll(acc, s, 0)
            head = jnp.where(sub_iota < s, pltpu.roll(tail, s, 0), xs[:8])
            return jnp.concatenate([head, xs[8:]], axis=0)

        cw = cw_ref[:, cols]
        y = shifted(last) * cw[0:1]
        for t in range(1, last):
            y = y + shifted(last - t) * cw[t:t + 1]
        y = y + acc * cw[last:last + 1]
        tail_ref[:, cols] = acc[tm - 8:]
        o_ref[:, cols] = _silu(y).astype(o_ref.dtype)

    blocks = [slice(c0, c0 + sub) for c0 in range(0, w_ref.shape[1], sub)]
    acc = _dot(a, w_ref[:, blocks[0]])
    for c, cols in enumerate(blocks):
        nxt = _dot(a, w_ref[:, blocks[c + 1]]) if c + 1 < len(blocks) else None
        epilogue(acc, cols)
        acc = nxt


def gdn_conv_proj(hn, w, conv_w, seq, *, tm=MM_ROWS, tn=EPI_COLS, sub=EPI_SUB):
    m, k = hn.shape
    n = w.shape[1]
    tm, tn = min(tm, seq), min(tn, n)
    return pl.pallas_call(
        functools.partial(_gdn_conv_kernel, rows_per_seq=seq, sub=min(sub, tn)),
        out_shape=jax.ShapeDtypeStruct((m, n), BF16),
        grid=(n // tn, m // tm),
        in_specs=[pl.BlockSpec((tm, k), lambda j, i: (i, 0)),
                  pl.BlockSpec((k, tn), lambda j, i: (0, j)),
                  pl.BlockSpec((C_CONV_WIDTH, tn), lambda j, i: (0, j))],
        out_specs=pl.BlockSpec((tm, tn), lambda j, i: (i, j)),
        scratch_shapes=[pltpu.VMEM((8, tn), F32)],
        compiler_params=_params("arbitrary", "arbitrary"),
        name="gdn_conv_proj",
    )(hn, w, conv_w)


def _pair_masks():
    n = 2 * CHUNK
    r, c = _iota((n, n), 0), _iota((n, n), 1)
    same = (r // CHUNK) == (c // CHUNK)
    return same & (r >= c), same & (r > c), r == c


def _inverse_stages(p_ref, t_ref, count, nilpotency):
    for _ in range(int(math.log2(nilpotency)) - 1):
        for i in range(count):
            p_ref[i] = _dot(p_ref[i], p_ref[i]).astype(BF16)
        for i in range(count):
            t = t_ref[i]
            t_ref[i] = t + _dot(t.astype(BF16), p_ref[i])


def _gdn_gates_kernel(a_ref, w_ref, alog_ref, dtb_ref, beta_ref, gc_ref, gct_ref):
    acc = _dot(a_ref[...], w_ref[...])
    beta_ref[...] = jax.nn.sigmoid(acc[:, :LANES])
    g = -jnp.exp(alog_ref[...]) * _softplus(acc[:, LANES:] + dtb_ref[...])
    tri, _, diag = _pair_masks()
    tri_b, eye_b = tri.astype(BF16), diag.astype(BF16)
    n2 = 2 * CHUNK
    for n in range(g.shape[0] // n2):
        rows = slice(n * n2, (n + 1) * n2)
        gc = _dot_xr(tri_b, g[rows])
        gc_ref[rows, :] = gc
        gct_ref[:, rows] = _dot_xr(eye_b, gc, NT)


def gdn_gates(hn, w_ba, alog_row, dtb_row, *, tm=ROW_TILE):
    m, k = hn.shape
    tm = min(tm, m)
    row = pl.BlockSpec((1, LANES), lambda i: (0, 0))
    out = pl.BlockSpec((tm, LANES), lambda i: (i, 0))
    return pl.pallas_call(
        _gdn_gates_kernel,
        out_shape=(jax.ShapeDtypeStruct((m, LANES), F32), jax.ShapeDtypeStruct((m, LANES), F32),
                   jax.ShapeDtypeStruct((LANES, m), F32)),
        grid=(m // tm,),
        in_specs=[pl.BlockSpec((tm, k), lambda i: (i, 0)), pl.BlockSpec((k, 2 * LANES), lambda i: (0, 0)), row, row],
        out_specs=(out, out, pl.BlockSpec((LANES, tm), lambda i: (0, i))),
        compiler_params=_params("parallel"),
        name="gdn_gates",
    )(hn, w_ba, alog_row, dtb_row)


def _gdn_prep_kernel(q_ref, k_ref, v_ref, beta_ref, gc_ref, gct_ref,
                     w_ref, qg_ref, u_ref, att_ref, kdt_ref, dec_ref,
                     g_s, dec_s, kn_s, kb_s, rhs_s, p_s, t_s, qn_s, kdec_s):
    kh = pl.program_id(1)
    c = CHUNK
    n2 = 2 * c
    dk = C_HEAD_DIM
    nv = w_ref.shape[0]
    npairs = q_ref.shape[0] // n2
    tri, strict, diag = _pair_masks()
    eye_f = diag.astype(F32)
    pairs = [slice(n * n2, (n + 1) * n2) for n in range(npairs)]
    lane = _iota((n2, LANES), 1)

    for n, rows in enumerate(pairs):
        for e in range(nv):
            h = kh * nv + e
            gc = jnp.broadcast_to(jnp.sum(jnp.where(lane == h, gc_ref[rows, :], 0.0), axis=-1, keepdims=True),
                                  (n2, dk))
            grp = gct_ref[pl.ds(pl.multiple_of((h // 8) * 8, 8), 8), rows]
            own_row = jnp.sum(jnp.where(_iota(grp.shape, 0) == h % 8, grp, 0.0), axis=0, keepdims=True)
            gc_row = jnp.broadcast_to(own_row, (n2, n2))
            g_s[n * nv + e] = gc
            dec_s[n * nv + e] = jnp.where(tri, jnp.exp(jnp.where(tri, gc - gc_row, 0.0)), 0.0)
    slab = SLAB_ROWS
    lane_slab = _iota((slab, LANES), 1)
    for n, rows in enumerate(pairs):
        for r0 in range(0, n2, slab):
            rs = slice(r0, r0 + slab)
            gs = slice(n * n2 + r0, n * n2 + r0 + slab)
            last = c - 1 if r0 < c else n2 - 1
            qf = q_ref[gs, :].astype(F32)
            kf = k_ref[gs, :].astype(F32)
            qn = qf * lax.rsqrt(jnp.sum(qf * qf, axis=-1, keepdims=True) + 1e-6) * (dk ** -0.5)
            kn = kf * lax.rsqrt(jnp.sum(kf * kf, axis=-1, keepdims=True) + 1e-6)
            kn_s[n, rs, :] = kn.astype(BF16)
            qn_s[n, rs, :] = qn.astype(BF16)
            for e in range(nv):
                i = n * nv + e
                gc = g_s[i, rs, :]
                vf = v_ref[gs, e * dk:(e + 1) * dk].astype(F32)
                beta = jnp.sum(jnp.where(lane_slab == kh * nv + e, beta_ref[gs, :], 0.0), axis=-1, keepdims=True)
                egc = jnp.exp(gc)
                kb = kn * beta
                kb_s[i, rs, :] = kb.astype(BF16)
                rhs_s[i, rs, :dk] = (vf * beta).astype(BF16)
                rhs_s[i, rs, dk:] = (kb * egc).astype(BF16)
                kdec_s[i, rs, :] = kn * jnp.exp(g_s[i, last:last + 1, :] - gc)
                qg_ref[e, gs, :] = (qn * egc).astype(BF16)
        for e in range(nv):
            i = n * nv + e
            dec_ref[e, 2 * n:2 * n + 1, :] = jnp.exp(g_s[i, c - 1:c, :])
            dec_ref[e, 2 * n + 1:2 * n + 2, :] = jnp.exp(g_s[i, n2 - 1:n2, :])
    for n, rows in enumerate(pairs):
        raw = _dot(qn_s[n], kn_s[n], NT)
        for e in range(nv):
            att_ref[e, rows, :] = (raw * dec_s[n * nv + e]).astype(BF16)
            kdt_ref[e, :, rows] = kdec_s[n * nv + e].T.astype(BF16)
    for n, rows in enumerate(pairs):
        for e in range(nv):
            i = n * nv + e
            neg_a = jnp.where(strict, -(_dot(kb_s[i], kn_s[n], NT) * dec_s[i]), 0.0)
            p_s[i] = neg_a.astype(BF16)
            t_s[i] = eye_f + neg_a
    _inverse_stages(p_s, t_s, npairs * nv, c)
    for n, rows in enumerate(pairs):
        for e in range(nv):
            sol = _dot(t_s[n * nv + e].astype(BF16), rhs_s[n * nv + e])
            u_ref[e, rows, :] = sol[:, :dk].astype(u_ref.dtype)
            w_ref[e, rows, :] = sol[:, dk:].astype(BF16)


def _gdn_scan_kernel(w_ref, qg_ref, u_ref, att_ref, kdt_ref, dec_ref, z_ref, g_ref, o_ref,
                     s_ref, ms_s, vp_s, *, group):
    c = CHUNK
    dk = C_HEAD_DIM
    zeros = jnp.zeros((c, dk), BF16)

    @pl.when(pl.program_id(2) == 0)
    def _():
        s_ref[...] = jnp.zeros(s_ref.shape, F32)

    for n in range(w_ref.shape[1] // c):
        rows = slice(n * c, (n + 1) * c)
        pair = slice((n // 2) * 2 * c, (n // 2 + 1) * 2 * c)
        for gi in range(group):
            lhs = jnp.concatenate([w_ref[gi, rows, :], qg_ref[gi, rows, :]], axis=0)
            ms_s[gi] = _dot(lhs, s_ref[gi].astype(BF16))
        for gi in range(group):
            v_new = (u_ref[gi, rows, :] - ms_s[gi, :c, :]).astype(BF16)
            vp_s[gi] = jnp.concatenate([v_new, zeros] if n % 2 == 0 else [zeros, v_new], axis=0)
        for gi in range(group):
            s_ref[gi] = s_ref[gi] * dec_ref[gi, n:n + 1, :] + _dot(kdt_ref[gi, :, pair], vp_s[gi])
        for gi in range(group):
            o = ms_s[gi, c:, :] + _dot(att_ref[gi, rows, :], vp_s[gi])
            o = o * lax.rsqrt(jnp.mean(o * o, axis=-1, keepdims=True) + NORM_EPS) * g_ref[...]
            z = z_ref[rows, gi * dk:(gi + 1) * dk].astype(F32)
            o_ref[rows, gi * dk:(gi + 1) * dk] = (o * _silu(z)).astype(o_ref.dtype)


def gated_deltanet_layer(x, hn, w_in, conv_w, a_log, dt_bias, norm_g, w_out, batch, seq):
    m, d = x.shape
    dk = C_HEAD_DIM
    k_heads = d // dk
    v_heads = 2 * k_heads
    conv_ch = 2 * k_heads * dk + v_heads * dk
    main_w = conv_ch + v_heads * dk
    qkv = gdn_conv_proj(hn, w_in[:, :conv_ch].astype(BF16), conv_w, seq)
    z = matmul(hn, w_in[:, conv_ch:main_w].astype(BF16), BF16, name="gdn_gate_proj")
    pad = jnp.zeros((d, LANES - v_heads), F32)
    w_ba = jnp.concatenate([w_in[:, main_w:main_w + v_heads], pad, w_in[:, main_w + v_heads:], pad], axis=1)
    row_pad = jnp.zeros((LANES - v_heads,), F32)
    alog_row = jnp.concatenate([a_log, row_pad]).reshape(1, LANES)
    dtb_row = jnp.concatenate([dt_bias, row_pad]).reshape(1, LANES)
    beta_all, gc_all, gc_t = gdn_gates(hn, w_ba.astype(BF16), alog_row, dtb_row)

    rows = min(PREP_ROWS, seq)
    nr = seq // rows
    ncr = rows // CHUNK
    hv = v_heads
    nv = v_heads // k_heads
    nchain = nv * rows // (2 * CHUNK)
    npair = rows // (2 * CHUNK)
    bh_t = lambda dt, last: jax.ShapeDtypeStruct((batch, hv, seq, last), dt)
    blk4 = lambda last: pl.BlockSpec((None, nv, rows, last), lambda b, h, i: (b, h, i, 0))
    qoff, koff, voff = 0, k_heads, 2 * k_heads // nv
    w_, qg_, u_, att_, kdt_, dec_ = pl.pallas_call(
        _gdn_prep_kernel,
        out_shape=(bh_t(BF16, dk), bh_t(BF16, dk), bh_t(BF16, dk), bh_t(BF16, 2 * CHUNK),
                   jax.ShapeDtypeStruct((batch, hv, dk, seq), BF16),
                   jax.ShapeDtypeStruct((batch, hv, seq // CHUNK, dk), F32)),
        grid=(batch, k_heads, nr),
        in_specs=[pl.BlockSpec((rows, dk), lambda b, h, i: (b * nr + i, qoff + h)),
                  pl.BlockSpec((rows, dk), lambda b, h, i: (b * nr + i, koff + h)),
                  pl.BlockSpec((rows, nv * dk), lambda b, h, i: (b * nr + i, voff + h)),
                  pl.BlockSpec((rows, LANES), lambda b, h, i: (b * nr + i, 0)),
                  pl.BlockSpec((rows, LANES), lambda b, h, i: (b * nr + i, 0)),
                  pl.BlockSpec((LANES, rows), lambda b, h, i: (0, b * nr + i))],
        out_specs=(blk4(dk), blk4(dk), blk4(dk), blk4(2 * CHUNK),
                   pl.BlockSpec((None, nv, dk, rows), lambda b, h, i: (b, h, 0, i)),
                   pl.BlockSpec((None, nv, ncr, dk), lambda b, h, i: (b, h, i, 0))),
        scratch_shapes=[pltpu.VMEM((nchain, dk, dk), F32), pltpu.VMEM((nchain, dk, dk), F32),
                        pltpu.VMEM((npair, dk, dk), BF16), pltpu.VMEM((nchain, dk, dk), BF16),
                        pltpu.VMEM((nchain, dk, 2 * dk), BF16), pltpu.VMEM((nchain, dk, dk), BF16),
                        pltpu.VMEM((nchain, dk, dk), F32), pltpu.VMEM((npair, dk, dk), BF16),
                        pltpu.VMEM((nchain, dk, dk), F32)],
        compiler_params=_params("parallel", "parallel", "parallel"),
        name="gdn_prep",
    )(qkv, qkv, qkv, beta_all, gc_all, gc_t)

    rows = min(SCAN_ROWS, seq)
    nr = seq // rows
    ncr = rows // CHUNK
    group = GDN_SCAN_HEADS
    gblk = lambda last: pl.BlockSpec((None, group, rows, last), lambda b, h, i: (b, h, i, 0))
    o = pl.pallas_call(
        functools.partial(_gdn_scan_kernel, group=group),
        out_shape=jax.ShapeDtypeStruct((m, hv * dk), BF16),
        grid=(batch, hv // group, nr),
        in_specs=[gblk(dk), gblk(dk), gblk(dk), gblk(2 * CHUNK),
                  pl.BlockSpec((None, group, dk, rows), lambda b, h, i: (b, h, 0, i)),
                  pl.BlockSpec((None, group, ncr, dk), lambda b, h, i: (b, h, i, 0)),
                  pl.BlockSpec((rows, group * dk), lambda b, h, i: (b * nr + i, h)),
                  pl.BlockSpec((1, dk), lambda b, h, i: (0, 0))],
        out_specs=pl.BlockSpec((rows, group * dk), lambda b, h, i: (b * nr + i, h)),
        scratch_shapes=[pltpu.VMEM((group, dk, dk), F32), pltpu.VMEM((group, 2 * CHUNK, dk), F32),
                        pltpu.VMEM((group, 2 * CHUNK, dk), BF16)],
        compiler_params=_params("parallel", "parallel", "arbitrary"),
        name="gdn_scan",
    )(w_, qg_, u_, att_, kdt_, dec_, z, norm_g.reshape(1, dk))
    return matmul_residual(o, w_out.astype(BF16), x, tm=MM_ROWS // 2, name="gdn_out_proj")


def _rwkv_mix_kernel(x_ref, g_ref, mu_ref, o_ref, tail_ref, *, rows_per_seq):
    i = pl.program_id(0)
    x = x_ref[...]
    tm = x.shape[0]
    hn = x * lax.rsqrt(jnp.mean(x * x, axis=-1, keepdims=True) + NORM_EPS) * g_ref[...]

    @pl.when((i * tm) % rows_per_seq == 0)
    def _():
        tail_ref[...] = jnp.zeros(tail_ref.shape, F32)

    prev = jnp.where(_iota(hn.shape, 0) == 0, tail_ref[7:8, :], pltpu.roll(hn, 1, 0))
    tail_ref[...] = hn[tm - 8:]
    xx = prev - hn
    for c in range(o_ref.shape[0]):
        o_ref[c] = (hn + xx * mu_ref[c:c + 1, :]).astype(o_ref.dtype)


def rwkv_token_mix(x, norm_g, mu, seq, *, tm=ROW_TILE // 2):
    m, d = x.shape
    tm = min(tm, seq)
    nmix = mu.shape[0]
    return pl.pallas_call(
        functools.partial(_rwkv_mix_kernel, rows_per_seq=seq),
        out_shape=jax.ShapeDtypeStruct((nmix, m, d), BF16),
        grid=(m // tm,),
        in_specs=[pl.BlockSpec((tm, d), lambda i: (i, 0)),
                  pl.BlockSpec((1, d), lambda i: (0, 0)),
                  pl.BlockSpec((nmix, d), lambda i: (0, 0))],
        out_specs=pl.BlockSpec((nmix, tm, d), lambda i: (0, i, 0)),
        scratch_shapes=[pltpu.VMEM((8, d), F32)],
        compiler_params=_params("arbitrary"),
        name="rwkv_token_mix",
    )(x, norm_g.reshape(1, d), mu)


def _grouped_mm_kernel(a_ref, w_ref, o_ref):
    o_ref[...] = _dot(a_ref[...], w_ref[...]).astype(o_ref.dtype)


def grouped_matmul(a, w, out_dtype, *, tm=MM_ROWS, tn=MM_COLS):
    g, k, n = w.shape
    m = a.shape[1]
    tm, tn = min(tm, m), min(tn, n)
    return pl.pallas_call(
        _grouped_mm_kernel,
        out_shape=jax.ShapeDtypeStruct((g, m, n), out_dtype),
        grid=(g, n // tn, m // tm),
        in_specs=[pl.BlockSpec((None, tm, k), lambda c, j, i: (c, i, 0)),
                  pl.BlockSpec((None, k, tn), lambda c, j, i: (c, 0, j))],
        out_specs=pl.BlockSpec((None, tm, tn), lambda c, j, i: (c, i, j)),
        compiler_params=_params("parallel", "parallel", "parallel"),
        name="rwkv_rkvg_proj",
    )(a, w)


def _rwkv_lora_kernel(xw_ref, xa_ref, w1_ref, w2_ref, a1_ref, a2_ref, w0_ref, a0_ref, ld_ref, a_ref):
    hw = jnp.tanh(_dot(xw_ref[...], w1_ref[...])).astype(BF16)
    lw = w0_ref[...] + _dot(hw, w2_ref[...])
    log_w = -_softplus(-lw) - 0.5
    ld_ref[...] = -jnp.exp(log_w)
    ha = _dot(xa_ref[...], a1_ref[...]).astype(BF16)
    a_ref[...] = jax.nn.sigmoid(a0_ref[...] + _dot(ha, a2_ref[...]))


def rwkv_lora(xs, w1, w2, a1, a2, w0, a0, *, tm=ROW_TILE):
    _, m, d = xs.shape
    tm = min(tm, m)
    r = w1.shape[1]
    row = pl.BlockSpec((1, d), lambda i: (0, 0))
    shape = jax.ShapeDtypeStruct((m, d), F32)
    return pl.pallas_call(
        _rwkv_lora_kernel,
        out_shape=(shape, shape),
        grid=(m // tm,),
        in_specs=[pl.BlockSpec((None, tm, d), lambda i: (4, i, 0)),
                  pl.BlockSpec((None, tm, d), lambda i: (5, i, 0)),
                  pl.BlockSpec((d, r), lambda i: (0, 0)), pl.BlockSpec((r, d), lambda i: (0, 0)),
                  pl.BlockSpec((d, r), lambda i: (0, 0)), pl.BlockSpec((r, d), lambda i: (0, 0)),
                  row, row],
        out_specs=(pl.BlockSpec((tm, d), lambda i: (i, 0)), pl.BlockSpec((tm, d), lambda i: (i, 0))),
        compiler_params=_params("parallel"),
        name="rwkv_lora",
    )(xs, xs, w1, w2, a1, a2, w0.reshape(1, d), a0.reshape(1, d))


def _head_sum(x, first_head):
    lo = jnp.sum(jnp.where(first_head, x, 0.0), axis=-1, keepdims=True)
    hi = jnp.sum(jnp.where(first_head, 0.0, x), axis=-1, keepdims=True)
    return jnp.where(first_head, lo, hi)


def _rwkv_prep_kernel(r_ref, k_ref, v_ref, ld_ref, a_ref, kk_ref, ka_ref, rk_ref,
                      wt_ref, rt_ref, u_ref, aro_ref, kbt_ref, vb_ref, dec_ref, bonus_ref,
                      kkt_s, kb_s, akk_s, p_s, t_s, drow_s, *, cps):
    c = CHUNK
    n2 = 2 * c
    n_h = B_HEAD_DIM
    npairs = r_ref.shape[0] // n2
    first_head = _iota((1, LANES), 1) < n_h
    tri, strict, diag = _pair_masks()
    tri_b = tri.astype(BF16)
    eye_b = diag.astype(BF16)
    eye_f = diag.astype(F32)
    upper_half = _iota((n2, LANES), 0) < c
    pairs = [slice(n * n2, (n + 1) * n2) for n in range(npairs)]

    for n, rows in enumerate(pairs):
        r, k, v, a = r_ref[rows, :], k_ref[rows, :], v_ref[rows, :], a_ref[rows, :]
        kk_raw = k * kk_ref[...]
        kk = kk_raw * lax.rsqrt(_head_sum(kk_raw * kk_raw, first_head) + 1e-6)
        k2c = k * (1.0 + (a - 1.0) * ka_ref[...])
        bbc = kk * a
        bonus_ref[rows, :] = (_head_sum(r * k2c * rk_ref[...], first_head) * v).astype(bonus_ref.dtype)
        vb_ref[rows, :] = v.astype(BF16)
        ld = ld_ref[rows, :]
        cs = _dot_x2r(tri_b, ld)
        cs_last = jnp.where(upper_half, cs[c - 1:c, :], cs[n2 - 1:n2, :])
        w_inv = jnp.exp(-cs)
        w_end = jnp.exp(cs_last - cs)
        rt_ref[rows, :] = (r * jnp.exp(cs)).astype(BF16)
        kkt_s[n] = (kk * jnp.exp(cs - ld)).astype(BF16)
        kb_s[n] = jnp.concatenate([k2c * w_inv, bbc * w_inv], axis=0).astype(BF16)
        kw, bw = k2c * w_end, -(bbc * w_end)
        for half in range(2):
            hs = slice(half * c, (half + 1) * c)
            kb_end = jnp.concatenate([kw[hs], bw[hs]], axis=0)
            cols = slice((2 * n + half) * n2, (2 * n + half + 1) * n2)
            kbt_ref[:, cols] = kb_end.T.astype(BF16)
        blk, off = divmod(2 * n, cps)
        drow_s[blk, off:off + 1, :] = jnp.exp(cs[c - 1:c, :])
        drow_s[blk, off + 1:off + 2, :] = jnp.exp(cs[n2 - 1:n2, :])
    for blk in range(dec_ref.shape[0] // LANES):
        drow_s[blk, cps:, :] = jnp.zeros((LANES - cps, LANES), F32)
        dec_ref[blk * LANES:(blk + 1) * LANES, :] = _dot_xr(eye_b, drow_s[blk], NT)
    zero_b = jnp.zeros((n2, LANES), BF16)
    strict2 = jnp.concatenate([strict, strict], axis=1)
    tri2 = jnp.concatenate([tri, tri], axis=1)
    sign2 = jnp.where(_iota((n2, 2 * n2), 1) < n2, 1.0, -1.0)
    for n, rows in enumerate(pairs):
        kkt, rt = kkt_s[n], rt_ref[rows, :]
        lhs = jnp.concatenate([jnp.where(first_head, kkt, zero_b), jnp.where(first_head, rt, zero_b),
                               jnp.where(first_head, zero_b, kkt), jnp.where(first_head, zero_b, rt)], axis=0)
        gram = _dot(lhs, kb_s[n], NT)
        for hh in range(2):
            i = 2 * n + hh
            g_kk = jnp.where(strict2, gram[2 * hh * n2:(2 * hh + 1) * n2], 0.0)
            neg_a = -g_kk[:, n2:]
            p_s[i] = neg_a.astype(BF16)
            t_s[i] = eye_f + neg_a
            akk_s[i] = g_kk[:, :n2].astype(BF16)
            g_r = gram[(2 * hh + 1) * n2:(2 * hh + 2) * n2]
            aro_ref[hh, rows, :] = jnp.where(tri2, g_r * sign2, 0.0).astype(BF16)
    _inverse_stages(p_s, t_s, 2 * npairs, c)
    for n, rows in enumerate(pairs):
        av = _dot(jnp.concatenate([akk_s[2 * n], akk_s[2 * n + 1]], axis=0), vb_ref[rows, :])
        p_s[2 * n] = av[:n2].astype(BF16)
        p_s[2 * n + 1] = av[n2:].astype(BF16)
    for n, rows in enumerate(pairs):
        t0, t1 = t_s[2 * n].astype(BF16), t_s[2 * n + 1].astype(BF16)
        wt = _dot(jnp.concatenate([t0, t1], axis=0), kkt_s[n])
        wt_ref[rows, :] = jnp.where(first_head, wt[:n2], wt[n2:]).astype(BF16)
        u_ref[rows, :] = jnp.where(first_head, _dot(t0, p_s[2 * n]), _dot(t1, p_s[2 * n + 1])).astype(u_ref.dtype)


def _rwkv_scan_kernel(wt_ref, rt_ref, u_ref, aro_ref, kbt_ref, vb_ref, dec_ref, bonus_ref, gate_ref,
                      lnw_ref, lnb_ref, o_ref, s_ref, ms_s, xc_s, xp_s, *, group):
    c = CHUNK
    n2 = 2 * c
    n_h = B_HEAD_DIM
    first_head = _iota((1, LANES), 1) < n_h
    same_head = _iota((LANES, LANES), 0) // n_h == _iota((LANES, LANES), 1) // n_h
    zeros = jnp.zeros((c, LANES), BF16)

    def head_mean(x):
        return _head_sum(x, first_head) * (1.0 / n_h)

    @pl.when(pl.program_id(2) == 0)
    def _():
        s_ref[...] = jnp.zeros(s_ref.shape, F32)

    for n in range(u_ref.shape[1] // c):
        rows = slice(n * c, (n + 1) * c)
        for gi in range(group):
            lhs = jnp.concatenate([wt_ref[gi, rows, :], rt_ref[gi, rows, :]], axis=0)
            ms_s[gi] = _dot(lhs, s_ref[gi].astype(BF16))
        for gi in range(group):
            sa = (u_ref[gi, rows, :] + ms_s[gi, :c, :]).astype(BF16)
            vb = vb_ref[gi, rows, :]
            xc_s[gi] = jnp.concatenate([vb, sa], axis=0)
            xp_s[gi] = jnp.concatenate([vb, zeros, sa, zeros] if n % 2 == 0 else [zeros, vb, zeros, sa], axis=0)
        for gi in range(group):
            upd = _dot(kbt_ref[gi, :, n * n2:(n + 1) * n2], xc_s[gi])
            dec_col = jnp.broadcast_to(dec_ref[gi, :, n:n + 1], (LANES, LANES))
            s_ref[gi] = s_ref[gi] * dec_col + jnp.where(same_head, upd, 0.0)
        for gi in range(group):
            both = _dot(jnp.concatenate([aro_ref[gi, 0, rows, :], aro_ref[gi, 1, rows, :]], axis=0), xp_s[gi])
            o = ms_s[gi, c:, :] + jnp.where(first_head, both[:c], both[c:])
            mean = head_mean(o)
            dlt = o - mean
            var = head_mean(dlt * dlt)
            cols = slice(gi * LANES, (gi + 1) * LANES)
            y = dlt * lax.rsqrt(var + B_GN_EPS) * lnw_ref[:, cols] + lnb_ref[:, cols]
            y = y + bonus_ref[rows, cols]
            o_ref[rows, cols] = (y * _silu(gate_ref[rows, cols])).astype(o_ref.dtype)


def rwkv7_layer(x, norm_g, mu, w_rkvg, w0, w_w1, w_w2, a0, w_a1, w_a2, k_k, k_a, r_k, ln_w, ln_b, w_out,
                batch, seq):
    m, d = x.shape
    heads = d // B_HEAD_DIM
    pairs = heads // 2
    order = jnp.array([0, 2, 3, 5, 1, 4])
    xs = rwkv_token_mix(x, norm_g, mu[order], seq)
    rkvg = grouped_matmul(xs, w_rkvg.astype(BF16), F32)
    lora = w_w1.shape[1]
    padc = lambda w: jnp.pad(w, ((0, 0), (0, LORA_PAD - lora))).astype(BF16)
    padr = lambda w: jnp.pad(w, ((0, LORA_PAD - lora), (0, 0))).astype(BF16)
    ld, a = rwkv_lora(xs, padc(w_w1), padr(w_w2), padc(w_a1), padr(w_a2), w0, a0)

    srows = min(SCAN_ROWS, seq)
    rows = min(PREP_ROWS, seq)
    nr = seq // rows
    npair = rows // (2 * CHUNK)
    col = lambda g: pl.BlockSpec((None, rows, LANES), lambda b, p, i, g=g: (g, b * nr + i, p))
    flat = pl.BlockSpec((rows, LANES), lambda b, p, i: (b * nr + i, p))
    prow = pl.BlockSpec((1, LANES), lambda b, p, i: (0, p))
    bp = lambda rws, last, dt: jax.ShapeDtypeStruct((batch, pairs, rws, last), dt)
    pblk = lambda rws, last: pl.BlockSpec((None, None, rws, last), lambda b, p, i: (b, p, i, 0))
    wt_, rt_, u_, aro_, kbt_, vb_, dec_, bonus = pl.pallas_call(
        functools.partial(_rwkv_prep_kernel, cps=srows // CHUNK),
        out_shape=(bp(seq, LANES, BF16), bp(seq, LANES, BF16), bp(seq, LANES, BF16),
                   jax.ShapeDtypeStruct((batch, pairs, 2, seq, 4 * CHUNK), BF16),
                   bp(LANES, 2 * seq, BF16), bp(seq, LANES, BF16), bp(seq // srows * LANES, LANES, F32),
                   jax.ShapeDtypeStruct((m, d), BF16)),
        grid=(batch, pairs, nr),
        in_specs=[col(0), col(1), col(2), flat, flat, prow, prow, prow],
        out_specs=(pblk(rows, LANES), pblk(rows, LANES), pblk(rows, LANES),
                   pl.BlockSpec((None, None, 2, rows, 4 * CHUNK), lambda b, p, i: (b, p, 0, i, 0)),
                   pl.BlockSpec((None, None, LANES, 2 * rows), lambda b, p, i: (b, p, 0, i)),
                   pblk(rows, LANES), pblk(rows // srows * LANES, LANES), flat),
        scratch_shapes=[pltpu.VMEM((npair, LANES, LANES), BF16), pltpu.VMEM((npair, 2 * LANES, LANES), BF16),
                        pltpu.VMEM((2 * npair, LANES, LANES), BF16), pltpu.VMEM((2 * npair, LANES, LANES), BF16),
                        pltpu.VMEM((2 * npair, LANES, LANES), F32),
                        pltpu.VMEM((rows // srows, LANES, LANES), F32)],
        compiler_params=_params("parallel", "parallel", "parallel"),
        name="rwkv_prep",
    )(rkvg, rkvg, rkvg, ld, a, k_k.reshape(1, d), k_a.reshape(1, d), r_k.reshape(1, d))

    rows = srows
    nr = seq // rows
    group = RWKV_SCAN_PAIRS
    gw = group * LANES
    gblk = lambda rws, last: pl.BlockSpec((None, group, rws, last), lambda b, p, i: (b, p, i, 0))
    gflat = pl.BlockSpec((rows, gw), lambda b, p, i: (b * nr + i, p))
    grow = pl.BlockSpec((1, gw), lambda b, p, i: (0, p))
    o = pl.pallas_call(
        functools.partial(_rwkv_scan_kernel, group=group),
        out_shape=jax.ShapeDtypeStruct((m, d), BF16),
        grid=(batch, pairs // group, nr),
        in_specs=[gblk(rows, LANES), gblk(rows, LANES), gblk(rows, LANES),
                  pl.BlockSpec((None, group, 2, rows, 4 * CHUNK), lambda b, p, i: (b, p, 0, i, 0)),
                  pl.BlockSpec((None, group, LANES, 2 * rows), lambda b, p, i: (b, p, 0, i)),
                  gblk(rows, LANES), gblk(LANES, LANES), gflat,
                  pl.BlockSpec((None, rows, gw), lambda b, p, i: (3, b * nr + i, p)),
                  grow, grow],
        out_specs=gflat,
        scratch_shapes=[pltpu.VMEM((group, LANES, LANES), F32), pltpu.VMEM((group, 2 * CHUNK, LANES), F32),
                        pltpu.VMEM((group, 2 * CHUNK, LANES), BF16), pltpu.VMEM((group, 4 * CHUNK, LANES), BF16)],
        compiler_params=_params("parallel", "parallel", "arbitrary"),
        name="rwkv_scan",
    )(wt_, rt_, u_, aro_, kbt_, vb_, dec_, bonus, rkvg, ln_w.reshape(1, d), ln_b.reshape(1, d))
    return matmul_residual(o, w_out.astype(BF16), x, name="rwkv_out_proj")


def kernel(x, p, positions, norm_g, pe_norm_g, pe_w_gate, pe_w_proj, final_norm_g, a_w_in, a_lam, a_subln_g, a_w_out, b_mu, b_w_rkvg, b_w0, b_w_w1, b_w_w2, b_a0, b_w_a1, b_w_a2, b_k_k, b_k_a, b_r_k, b_ln_w, b_ln_b, b_w_out, c_w_in, c_conv_w, c_A_log, c_dt_bias, c_norm_g, c_w_out):
    batch, seq, d = x.shape
    depth = p.shape[0]
    m = batch * seq
    xf = x.reshape(m, d)
    tables = rope_tables(positions)
    for i in range(depth):
        kind = i % N_MIXERS
        j = i // N_MIXERS
        if kind == 0:
            lam_init = 0.8 - 0.6 * math.exp(-0.3 * i)
            hn = rmsnorm(xf, norm_g[i], BF16)
            xf = diff_attention_layer(xf, hn, tables, a_w_in[j], a_lam[j], a_subln_g[j], a_w_out[j],
                                      batch, seq, lam_init)
        elif kind == 1:
            xf = rwkv7_layer(xf, norm_g[i], b_mu[j], b_w_rkvg[j], b_w0[j], b_w_w1[j], b_w_w2[j], b_a0[j],
                             b_w_a1[j], b_w_a2[j], b_k_k[j], b_k_a[j], b_r_k[j], b_ln_w[j], b_ln_b[j],
                             b_w_out[j], batch, seq)
        else:
            hn = rmsnorm(xf, norm_g[i], BF16)
            xf = gated_deltanet_layer(xf, hn, c_w_in[j], c_conv_w[j], c_A_log[j], c_dt_bias[j], c_norm_g[j],
                                      c_w_out[j], batch, seq)
        hn2 = rmsnorm(xf, pe_norm_g[i], BF16)
        xf = per_layer_embedding(xf, hn2, pe_w_gate[i].astype(BF16), p[i].reshape(m, -1),
                                 pe_w_proj[i].astype(BF16))
    return rmsnorm(xf, final_norm_g, F32).reshape(batch, seq, d)
```

```python
import functools
import math

import jax
import jax.numpy as jnp
from jax import lax
from jax.experimental import pallas as pl
from jax.experimental.pallas import tpu as pltpu

F32 = jnp.float32
BF16 = jnp.bfloat16

N_MIXERS = 3
NORM_EPS = 1e-6
LANES = 128
VMEM_LIMIT = 48 * 1024 * 1024

A_HEAD_DIM = 128
A_V_DIM = 2 * A_HEAD_DIM
ROT_DIM = A_HEAD_DIM // 4
ROPE_THETA = 500000.0
SUBLN_EPS = 1e-5
ATTN_BLOCK = 512
ATTN_Q_BLOCKS = 1
ONES_ROWS = 16

B_HEAD_DIM = 64
B_GN_EPS = 64e-5
LORA_PAD = 128

C_HEAD_DIM = 128
C_CONV_WIDTH = 4
CHUNK = 64
SCAN_ROWS = 512
PREP_ROWS = 1024
SLAB_ROWS = 32
GDN_SCAN_HEADS = 16
RWKV_SCAN_PAIRS = 8

MM_ROWS = 1024
MM_COLS = 1024
ROW_TILE = 512
EPI_COLS = 2048
EPI_SUB = 256

NN = (((1,), (0,)), ((), ()))
NT = (((1,), (1,)), ((), ()))


def _dot(a, b, dims=NN):
    return lax.dot_general(a, b, dims, preferred_element_type=F32)


def _split2(x):
    hi = x.astype(BF16)
    lo = (x - hi.astype(F32)).astype(BF16)
    return hi, lo


def _split3(x):
    hi = x.astype(BF16)
    r = x - hi.astype(F32)
    mid = r.astype(BF16)
    lo = (r - mid.astype(F32)).astype(BF16)
    return hi, mid, lo


def _dot_xr(a_exact, b, dims=NN):
    h, m, l = _split3(b)
    return _dot(a_exact, h, dims) + (_dot(a_exact, m, dims) + _dot(a_exact, l, dims))


def _dot_x2r(a_exact, b, dims=NN):
    h, l = _split2(b)
    return _dot(a_exact, h, dims) + _dot(a_exact, l, dims)


def _iota(shape, dim):
    return lax.broadcasted_iota(jnp.int32, shape, dim)


def _silu(x):
    return x * jax.nn.sigmoid(x)


def _softplus(x):
    return jnp.maximum(x, 0.0) + jnp.log(1.0 + jnp.exp(-jnp.abs(x)))


def _params(*sem):
    return pltpu.CompilerParams(dimension_semantics=sem, vmem_limit_bytes=VMEM_LIMIT)


def _rmsnorm_kernel(x_ref, g_ref, o_ref, *, eps):
    x = x_ref[...]
    y = x * lax.rsqrt(jnp.mean(x * x, axis=-1, keepdims=True) + eps)
    o_ref[...] = (y * g_ref[...]).astype(o_ref.dtype)


def rmsnorm(x, g, out_dtype, *, eps=NORM_EPS, tm=ROW_TILE):
    m, d = x.shape
    tm = min(tm, m)
    return pl.pallas_call(
        functools.partial(_rmsnorm_kernel, eps=eps),
        out_shape=jax.ShapeDtypeStruct((m, d), out_dtype),
        grid=(m // tm,),
        in_specs=[pl.BlockSpec((tm, d), lambda i: (i, 0)), pl.BlockSpec((1, d), lambda i: (0, 0))],
        out_specs=pl.BlockSpec((tm, d), lambda i: (i, 0)),
        compiler_params=_params("parallel"),
        name="rmsnorm",
    )(x, g.reshape(1, d))


def _mm_kernel(a_ref, w_ref, *rest, epilogue):
    o_ref = rest[-1]
    acc = _dot(a_ref[...], w_ref[...])
    if epilogue is not None:
        acc = epilogue(acc, *rest[:-1])
    o_ref[...] = acc.astype(o_ref.dtype)


def matmul(a, w, out_dtype, *, tm=MM_ROWS, tn=MM_COLS, extra=(), extra_specs=(), epilogue=None, name="matmul"):
    m, k = a.shape
    n = w.shape[1]
    tm, tn = min(tm, m), min(tn, n)
    return pl.pallas_call(
        functools.partial(_mm_kernel, epilogue=epilogue),
        out_shape=jax.ShapeDtypeStruct((m, n), out_dtype),
        grid=(n // tn, m // tm),
        in_specs=[pl.BlockSpec((tm, k), lambda j, i: (i, 0)),
                  pl.BlockSpec((k, tn), lambda j, i: (0, j))] + list(extra_specs),
        out_specs=pl.BlockSpec((tm, tn), lambda j, i: (i, j)),
        compiler_params=_params("parallel", "parallel"),
        name=name,
    )(a, w, *extra)


def _out_proj_kernel(a_ref, w_ref, res_ref, x_ref, xb_ref, ssq_ref):
    x = res_ref[...] + _dot(a_ref[...], w_ref[...])
    x_ref[...] = x
    xb_ref[...] = x.astype(BF16)
    ssq_ref[...] = jnp.sum(x * x, axis=-1, keepdims=True)


def matmul_residual(a, w, res, *, tm=MM_ROWS, tn=MM_COLS, name="matmul_residual"):
    m, k = a.shape
    n = w.shape[1]
    tm, tn = min(tm, m), min(tn, n)
    blk = pl.BlockSpec((tm, tn), lambda j, i: (i, j))
    return pl.pallas_call(
        _out_proj_kernel,
        out_shape=(jax.ShapeDtypeStruct((m, n), F32), jax.ShapeDtypeStruct((m, n), BF16),
                   jax.ShapeDtypeStruct((n // tn, m, 1), F32)),
        grid=(n // tn, m // tm),
        in_specs=[pl.BlockSpec((tm, k), lambda j, i: (i, 0)), pl.BlockSpec((k, tn), lambda j, i: (0, j)), blk],
        out_specs=(blk, blk, pl.BlockSpec((None, tm, 1), lambda j, i: (j, i, 0))),
        compiler_params=_params("parallel", "parallel"),
        name=name,
    )(a, w, res)


def _ple_kernel(xb_ref, ssq_ref, wg_ref, p_ref, wp_ref, x_ref, o_ref, *, eps):
    d = xb_ref.shape[1]
    inv_rms = lax.rsqrt(jnp.sum(ssq_ref[...], axis=0) / d + eps)
    gate = jax.nn.sigmoid(inv_rms * _dot(xb_ref[...], wg_ref[...]))
    proj = _dot(p_ref[...].astype(BF16), wp_ref[...])
    o_ref[...] = x_ref[...] + gate * proj


def per_layer_embedding(x, xb, ssq, norm_g, w_gate, p, w_proj, *, tm=MM_ROWS, tn=MM_COLS):
    m, d = x.shape
    pd = p.shape[1]
    nparts = ssq.shape[0]
    tm, tn = min(tm, m), min(tn, d)
    wg = (norm_g[:, None] * w_gate).astype(BF16)
    return pl.pallas_call(
        functools.partial(_ple_kernel, eps=NORM_EPS),
        out_shape=jax.ShapeDtypeStruct((m, d), F32),
        grid=(d // tn, m // tm),
        in_specs=[pl.BlockSpec((tm, d), lambda j, i: (i, 0)),
                  pl.BlockSpec((nparts, tm, 1), lambda j, i: (0, i, 0)),
                  pl.BlockSpec((d, tn), lambda j, i: (0, j)),
                  pl.BlockSpec((tm, pd), lambda j, i: (i, 0)),
                  pl.BlockSpec((pd, tn), lambda j, i: (0, j)),
                  pl.BlockSpec((tm, tn), lambda j, i: (i, j))],
        out_specs=pl.BlockSpec((tm, tn), lambda j, i: (i, j)),
        compiler_params=_params("parallel", "parallel"),
        name="per_layer_embedding",
    )(xb, ssq, wg, p, w_proj.astype(BF16), x)


def _rope_table_kernel(pos_ref, freq_ref, cos_ref, sin_lo_ref, sin_hi_ref):
    half = ROT_DIM // 2
    ang = pos_ref[...].astype(F32) * freq_ref[...]
    lane = _iota(ang.shape, 1)
    c, s = jnp.cos(ang), jnp.sin(ang)
    cos_ref[...] = jnp.where(lane < ROT_DIM, c, 1.0)
    sin_lo_ref[...] = jnp.where(lane < half, -s, 0.0)
    sin_hi_ref[...] = jnp.where((lane >= half) & (lane < ROT_DIM), s, 0.0)


def rope_tables(positions, *, tm=MM_ROWS):
    m = positions.size
    tm = min(tm, m)
    inv_freq = ROPE_THETA ** (-jnp.arange(0, ROT_DIM, 2, dtype=F32) / ROT_DIM)
    freq_row = jnp.concatenate([inv_freq, inv_freq, jnp.zeros((LANES - ROT_DIM,), F32)]).reshape(1, LANES)
    shape = jax.ShapeDtypeStruct((m, LANES), F32)
    spec = pl.BlockSpec((tm, LANES), lambda i: (i, 0))
    return pl.pallas_call(
        _rope_table_kernel,
        out_shape=(shape, shape, shape),
        grid=(m // tm,),
        in_specs=[pl.BlockSpec((tm, 1), lambda i: (i, 0)), pl.BlockSpec((1, LANES), lambda i: (0, 0))],
        out_specs=(spec, spec, spec),
        compiler_params=_params("parallel"),
        name="rope_tables",
    )(positions.reshape(m, 1), freq_row)


def _attn_qk_kernel(a_ref, w_ref, cos_ref, sin_lo_ref, sin_hi_ref, o_ref, *, n_q_blocks, sub):
    j = pl.program_id(0)
    half = ROT_DIM // 2
    scale = jnp.where(j < n_q_blocks, A_HEAD_DIM ** -0.5, 1.0).astype(F32)
    cos, sin_lo, sin_hi = cos_ref[...], sin_lo_ref[...], sin_hi_ref[...]
    a = a_ref[...]

    def epilogue(acc, c0):
        for g in range(acc.shape[1] // LANES):
            x = acc[:, g * LANES:(g + 1) * LANES]
            y = x * cos + pltpu.roll(x, LANES - half, 1) * sin_lo + pltpu.roll(x, half, 1) * sin_hi
            o_ref[:, c0 + g * LANES:c0 + (g + 1) * LANES] = (y * scale).astype(o_ref.dtype)

    starts = list(range(0, w_ref.shape[1], sub))
    acc = _dot(a, w_ref[:, starts[0]:starts[0] + sub])
    for c, c0 in enumerate(starts):
        nxt = _dot(a, w_ref[:, starts[c + 1]:starts[c + 1] + sub]) if c + 1 < len(starts) else None
        epilogue(acc, c0)
        acc = nxt


def attn_qk_proj(hn, w_qk, tables, *, tm=MM_ROWS, tn=EPI_COLS, sub=EPI_SUB):
    m, k = hn.shape
    n = w_qk.shape[1]
    tm, tn = min(tm, m), min(tn, n // 2)
    tspec = pl.BlockSpec((tm, LANES), lambda j, i: (i, 0))
    return pl.pallas_call(
        functools.partial(_attn_qk_kernel, n_q_blocks=n // 2 // tn, sub=min(sub, tn)),
        out_shape=jax.ShapeDtypeStruct((m, n), BF16),
        grid=(n // tn, m // tm),
        in_specs=[pl.BlockSpec((tm, k), lambda j, i: (i, 0)),
                  pl.BlockSpec((k, tn), lambda j, i: (0, j)), tspec, tspec, tspec],
        out_specs=pl.BlockSpec((tm, tn), lambda j, i: (i, j)),
        compiler_params=_params("parallel", "parallel"),
        name="attn_qk_proj",
    )(hn, w_qk, *tables)


def _attn_vt_kernel(w_ref, a_ref, o_ref):
    o_ref[...] = _dot(w_ref[...], a_ref[...], NT).astype(o_ref.dtype)


def attn_v_proj_t(hn, w_t, batch, seq, blk, *, tn=MM_COLS):
    m, k = hn.shape
    n = w_t.shape[0]
    tn = min(tn, n)
    nk = seq // blk
    return pl.pallas_call(
        _attn_vt_kernel,
        out_shape=jax.ShapeDtypeStruct((batch, nk, n, blk), BF16),
        grid=(n // tn, m // blk),
        in_specs=[pl.BlockSpec((tn, k), lambda j, i: (j, 0)),
                  pl.BlockSpec((blk, k), lambda j, i: (i, 0))],
        out_specs=pl.BlockSpec((None, None, tn, blk), lambda j, i: (i // nk, i % nk, j, 0)),
        compiler_params=_params("parallel", "parallel"),
        name="attn_v_proj_t",
    )(w_t, hn)


def _diff_attn_kernel(lam_ref, q_ref, k_ref, vt_ref, z_ref, g_ref, o_ref, m_ref, acc_ref, s_ref,
                      *, bq, bk, lam_init):
    i = pl.program_id(2)
    lam = lam_ref[...]
    lam_full = (jnp.exp(jnp.sum(lam[0:1] * lam[1:2], axis=-1, keepdims=True))
                - jnp.exp(jnp.sum(lam[2:3] * lam[3:4], axis=-1, keepdims=True)) + lam_init)
    m_ref[...] = jnp.full(m_ref.shape, -jnp.inf, F32)
    acc_ref[...] = jnp.zeros(acc_ref.shape, F32)
    q = q_ref[...]
    d = A_HEAD_DIM
    w = A_V_DIM
    ones = jnp.ones((ONES_ROWS, bk), BF16)

    def scores(j, slot, q_lo=0):
        start = pl.multiple_of(j * bk, bk)
        kb = k_ref[pl.ds(start, bk), :]
        for c in range(2):
            s_ref[slot, c, :, q_lo:] = _dot(kb[:, c * d:(c + 1) * d], q[q_lo:, c * d:(c + 1) * d], NT)

    def absorb(j, slot, masked, q_lo=0):
        vt = jnp.concatenate([vt_ref[j], ones], axis=0)
        for c in range(2):
            s = s_ref[slot, c, :, q_lo:]
            if masked:
                kv_pos = j * bk + _iota(s.shape, 0)
                q_pos = i * bq + q_lo + _iota(s.shape, 1)
                s = jnp.where(kv_pos <= q_pos, s, -jnp.inf)
            m_prev = m_ref[c, :, q_lo:]
            m_new = jnp.maximum(m_prev, jnp.max(s, axis=0, keepdims=True))
            alpha = jnp.exp(m_prev - m_new)
            p = jnp.exp(s - m_new)
            acc_ref[c, :, q_lo:] = alpha * acc_ref[c, :, q_lo:] + _dot(vt, p.astype(BF16))
            m_ref[c, :, q_lo:] = m_new

    r = bq // bk
    first_masked = r * i
    scores(0, 0)

    def body(jj, carry):
        scores(2 * jj + 1, 1)
        absorb(2 * jj, 0, False)
        scores(2 * jj + 2, 0)
        absorb(2 * jj + 1, 1, False)
        return carry

    lax.fori_loop(0, first_masked // 2, body, 0)

    def tail(odd):
        base = first_masked - odd
        left = [(base + n, n % 2, n >= odd, max(n - odd, 0) * bk) for n in range(odd + r)]
        for n, (j, slot, masked, q_lo) in enumerate(left):
            if n + 1 < len(left):
                nj, nslot, _, nq_lo = left[n + 1]
                scores(nj, nslot, nq_lo)
            absorb(j, slot, masked, q_lo)

    if r % 2 == 0:
        tail(0)
    else:
        pl.when(first_masked % 2 == 0)(lambda: tail(0))
        pl.when(first_masked % 2 == 1)(lambda: tail(1))

    o = (acc_ref[0, :w, :] / acc_ref[0, w:w + 1, :]
         - lam_full * (acc_ref[1, :w, :] / acc_ref[1, w:w + 1, :]))
    o = o * lax.rsqrt(jnp.mean(o * o, axis=0, keepdims=True) + SUBLN_EPS) * g_ref[...]
    o = (o * (1.0 - lam_init)).T
    o_ref[...] = (o * _silu(z_ref[...].astype(F32))).astype(o_ref.dtype)


def diff_attention_core(qk, z, vt, lam, subln_g, batch, seq, heads, lam_init):
    m = qk.shape[0]
    bk = vt.shape[-1]
    bq = ATTN_Q_BLOCKS * bk
    nq = seq // bq
    nk = seq // bk
    w = A_V_DIM
    return pl.pallas_call(
        functools.partial(_diff_attn_kernel, bq=bq, bk=bk, lam_init=lam_init),
        out_shape=jax.ShapeDtypeStruct((m, heads * w), BF16),
        grid=(batch, heads, nq),
        in_specs=[pl.BlockSpec((4, A_HEAD_DIM), lambda b, h, i: (0, 0)),
                  pl.BlockSpec((bq, w), lambda b, h, i: (b * nq + i, h)),
                  pl.BlockSpec((seq, w), lambda b, h, i: (b, heads + h)),
                  pl.BlockSpec((None, nk, w, bk), lambda b, h, i: (b, 0, h, 0)),
                  pl.BlockSpec((bq, w), lambda b, h, i: (b * nq + i, h)),
                  pl.BlockSpec((w, 1), lambda b, h, i: (0, 0))],
        out_specs=pl.BlockSpec((bq, w), lambda b, h, i: (b * nq + i, h)),
        scratch_shapes=[pltpu.VMEM((2, 1, bq), F32), pltpu.VMEM((2, w + ONES_ROWS, bq), F32),
                        pltpu.VMEM((2, 2, bk, bq), F32)],
        compiler_params=_params("parallel", "parallel", "parallel"),
        name="diff_attention",
    )(lam, qk, qk, vt, z, subln_g.reshape(w, 1))


def diff_attention_layer(x, hn, tables, w_in, lam, subln_g, w_out, batch, seq, lam_init):
    d = x.shape[1]
    heads = d // A_V_DIM
    qk_w = heads * 2 * A_HEAD_DIM
    v_w = heads * A_V_DIM
    blk = min(ATTN_BLOCK, seq // ATTN_Q_BLOCKS)
    w_vt = w_in[:, 2 * qk_w:2 * qk_w + v_w].T.astype(BF16)
    qk = attn_qk_proj(hn, w_in[:, :2 * qk_w].astype(BF16), tables)
    z = matmul(hn, w_in[:, 2 * qk_w + v_w:].astype(BF16), BF16, name="attn_gate_proj")
    vt = attn_v_proj_t(hn, w_vt, batch, seq, blk)
    o = diff_attention_core(qk, z, vt, lam, subln_g, batch, seq, heads, lam_init)
    return matmul_residual(o, w_out.astype(BF16), x, name="attn_out_proj")


def _gdn_conv_kernel(a_ref, w_ref, cw_ref, o_ref, tail_ref, *, rows_per_seq, sub):
    i = pl.program_id(1)
    tm = a_ref.shape[0]

    @pl.when((i * tm) % rows_per_seq == 0)
    def _():
        tail_ref[...] = jnp.zeros(tail_ref.shape, F32)

    a = a_ref[...]
    sub_iota = _iota((8, sub), 0)
    last = C_CONV_WIDTH - 1

    def epilogue(acc, cols):
        tail = tail_ref[:, cols]

        def shifted(s):
            xs = pltpu.roll(acc, s, 0)
            head = jnp.where(sub_iota < s, pltpu.roll(tail, s, 0), xs[:8])
            return jnp.concatenate([head, xs[8:]], axis=0)

        cw = cw_ref[:, cols]
        y = shifted(last) * cw[0:1]
        for t in range(1, last):
            y = y + shifted(last - t) * cw[t:t + 1]
        y = y + acc * cw[last:last + 1]
        tail_ref[:, cols] = acc[tm - 8:]
        o_ref[:, cols] = _silu(y).astype(o_ref.dtype)

    blocks = [slice(c0, c0 + sub) for c0 in range(0, w_ref.shape[1], sub)]
    acc = _dot(a, w_ref[:, blocks[0]])
    for c, cols in enumerate(blocks):
        nxt = _dot(a, w_ref[:, blocks[c + 1]]) if c + 1 < len(blocks) else None
        epilogue(acc, cols)
        acc = nxt


def gdn_conv_proj(hn, w, conv_w, seq, *, tm=MM_ROWS, tn=EPI_COLS, sub=EPI_SUB):
    m, k = hn.shape
    n = w.shape[1]
    tm, tn = min(tm, seq), min(tn, n)
    return pl.pallas_call(
        functools.partial(_gdn_conv_kernel, rows_per_seq=seq, sub=min(sub, tn)),
        out_shape=jax.ShapeDtypeStruct((m, n), BF16),
        grid=(n // tn, m // tm),
        in_specs=[pl.BlockSpec((tm, k), lambda j, i: (i, 0)),
                  pl.BlockSpec((k, tn), lambda j, i: (0, j)),
                  pl.BlockSpec((C_CONV_WIDTH, tn), lambda j, i: (0, j))],
        out_specs=pl.BlockSpec((tm, tn), lambda j, i: (i, j)),
        scratch_shapes=[pltpu.VMEM((8, tn), F32)],
        compiler_params=_params("arbitrary", "arbitrary"),
        name="gdn_conv_proj",
    )(hn, w, conv_w)


def _pair_masks():
    n = 2 * CHUNK
    r, c = _iota((n, n), 0), _iota((n, n), 1)
    same = (r // CHUNK) == (c // CHUNK)
    return same & (r >= c), same & (r > c), r == c


def _inverse_stages(p_ref, t_ref, count, nilpotency):
    for _ in range(int(math.log2(nilpotency)) - 1):
        for i in range(count):
            p_ref[i] = _dot(p_ref[i], p_ref[i]).astype(BF16)
        for i in range(count):
            t = t_ref[i]
            t_ref[i] = t + _dot(t.astype(BF16), p_ref[i])


def _gdn_gates_kernel(a_ref, w_ref, alog_ref, dtb_ref, beta_ref, gc_ref, gct_ref):
    acc = _dot(a_ref[...], w_ref[...])
    beta_ref[...] = jax.nn.sigmoid(acc[:, :LANES])
    g = -jnp.exp(alog_ref[...]) * _softplus(acc[:, LANES:] + dtb_ref[...])
    tri, _, diag = _pair_masks()
    tri_b, eye_b = tri.astype(BF16), diag.astype(BF16)
    n2 = 2 * CHUNK
    for n in range(g.shape[0] // n2):
        rows = slice(n * n2, (n + 1) * n2)
        gc = _dot_xr(tri_b, g[rows])
        gc_ref[rows, :] = gc
        gct_ref[:, rows] = _dot_xr(eye_b, gc, NT)


def gdn_gates(hn, w_ba, alog_row, dtb_row, *, tm=ROW_TILE):
    m, k = hn.shape
    tm = min(tm, m)
    row = pl.BlockSpec((1, LANES), lambda i: (0, 0))
    out = pl.BlockSpec((tm, LANES), lambda i: (i, 0))
    return pl.pallas_call(
        _gdn_gates_kernel,
        out_shape=(jax.ShapeDtypeStruct((m, LANES), F32), jax.ShapeDtypeStruct((m, LANES), F32),
                   jax.ShapeDtypeStruct((LANES, m), F32)),
        grid=(m // tm,),
        in_specs=[pl.BlockSpec((tm, k), lambda i: (i, 0)), pl.BlockSpec((k, 2 * LANES), lambda i: (0, 0)), row, row],
        out_specs=(out, out, pl.BlockSpec((LANES, tm), lambda i: (0, i))),
        compiler_params=_params("parallel"),
        name="gdn_gates",
    )(hn, w_ba, alog_row, dtb_row)


def _gdn_prep_kernel(q_ref, k_ref, v_ref, beta_ref, gc_ref, gct_ref,
                     w_ref, qg_ref, u_ref, att_ref, kdt_ref, dec_ref,
                     g_s, dec_s, kn_s, kb_s, rhs_s, p_s, t_s, qn_s, kdec_s):
    kh = pl.program_id(1)
    c = CHUNK
    n2 = 2 * c
    dk = C_HEAD_DIM
    nv = w_ref.shape[0]
    npairs = q_ref.shape[0] // n2
    tri, strict, diag = _pair_masks()
    eye_f = diag.astype(F32)
    pairs = [slice(n * n2, (n + 1) * n2) for n in range(npairs)]
    lane = _iota((n2, LANES), 1)

    for n, rows in enumerate(pairs):
        for e in range(nv):
            h = kh * nv + e
            gc = jnp.broadcast_to(jnp.sum(jnp.where(lane == h, gc_ref[rows, :], 0.0), axis=-1, keepdims=True),
                                  (n2, dk))
            grp = gct_ref[pl.ds(pl.multiple_of((h // 8) * 8, 8), 8), rows]
            own_row = jnp.sum(jnp.where(_iota(grp.shape, 0) == h % 8, grp, 0.0), axis=0, keepdims=True)
            gc_row = jnp.broadcast_to(own_row, (n2, n2))
            g_s[n * nv + e] = gc
            dec_s[n * nv + e] = jnp.where(tri, jnp.exp(jnp.where(tri, gc - gc_row, 0.0)), 0.0)
    slab = SLAB_ROWS
    lane_slab = _iota((slab, LANES), 1)
    for n, rows in enumerate(pairs):
        for r0 in range(0, n2, slab):
            rs = slice(r0, r0 + slab)
            gs = slice(n * n2 + r0, n * n2 + r0 + slab)
            last = c - 1 if r0 < c else n2 - 1
            qf = q_ref[gs, :].astype(F32)
            kf = k_ref[gs, :].astype(F32)
            qn = qf * lax.rsqrt(jnp.sum(qf * qf, axis=-1, keepdims=True) + 1e-6) * (dk ** -0.5)
            kn = kf * lax.rsqrt(jnp.sum(kf * kf, axis=-1, keepdims=True) + 1e-6)
            kn_s[n, rs, :] = kn.astype(BF16)
            qn_s[n, rs, :] = qn.astype(BF16)
            for e in range(nv):
                i = n * nv + e
                gc = g_s[i, rs, :]
                vf = v_ref[gs, e * dk:(e + 1) * dk].astype(F32)
                beta = jnp.sum(jnp.where(lane_slab == kh * nv + e, beta_ref[gs, :], 0.0), axis=-1, keepdims=True)
                egc = jnp.exp(gc)
                kb = kn * beta
                kb_s[i, rs, :] = kb.astype(BF16)
                rhs_s[i, rs, :dk] = (vf * beta).astype(BF16)
                rhs_s[i, rs, dk:] = (kb * egc).astype(BF16)
                kdec_s[i, rs, :] = kn * jnp.exp(g_s[i, last:last + 1, :] - gc)
                qg_ref[e, gs, :] = (qn * egc).astype(BF16)
        for e in range(nv):
            i = n * nv + e
            dec_ref[e, 2 * n:2 * n + 1, :] = jnp.exp(g_s[i, c - 1:c, :])
            dec_ref[e, 2 * n + 1:2 * n + 2, :] = jnp.exp(g_s[i, n2 - 1:n2, :])
    for n, rows in enumerate(pairs):
        raw = _dot(qn_s[n], kn_s[n], NT)
        for e in range(nv):
            att_ref[e, rows, :] = (raw * dec_s[n * nv + e]).astype(BF16)
            kdt_ref[e, :, rows] = kdec_s[n * nv + e].T.astype(BF16)
    for n, rows in enumerate(pairs):
        for e in range(nv):
            i = n * nv + e
            neg_a = jnp.where(strict, -(_dot(kb_s[i], kn_s[n], NT) * dec_s[i]), 0.0)
            p_s[i] = neg_a.astype(BF16)
            t_s[i] = eye_f + neg_a
    _inverse_stages(p_s, t_s, npairs * nv, c)
    for n, rows in enumerate(pairs):
        for e in range(nv):
            sol = _dot(t_s[n * nv + e].astype(BF16), rhs_s[n * nv + e])
            u_ref[e, rows, :] = sol[:, :dk].astype(u_ref.dtype)
            w_ref[e, rows, :] = sol[:, dk:].astype(BF16)


def _gdn_scan_kernel(w_ref, qg_ref, u_ref, att_ref, kdt_ref, dec_ref, z_ref, g_ref, o_ref,
                     s_ref, ms_s, vp_s, *, group):
    c = CHUNK
    dk = C_HEAD_DIM
    zeros = jnp.zeros((c, dk), BF16)

    @pl.when(pl.program_id(2) == 0)
    def _():
        s_ref[...] = jnp.zeros(s_ref.shape, F32)

    for n in range(w_ref.shape[1] // c):
        rows = slice(n * c, (n + 1) * c)
        pair = slice((n // 2) * 2 * c, (n // 2 + 1) * 2 * c)
        for gi in range(group):
            lhs = jnp.concatenate([w_ref[gi, rows, :], qg_ref[gi, rows, :]], axis=0)
            ms_s[gi] = _dot(lhs, s_ref[gi].astype(BF16))
        for gi in range(group):
            v_new = (u_ref[gi, rows, :] - ms_s[gi, :c, :]).astype(BF16)
            vp_s[gi] = jnp.concatenate([v_new, zeros] if n % 2 == 0 else [zeros, v_new], axis=0)
        for gi in range(group):
            s_ref[gi] = s_ref[gi] * dec_ref[gi, n:n + 1, :] + _dot(kdt_ref[gi, :, pair], vp_s[gi])
        for gi in range(group):
            o = ms_s[gi, c:, :] + _dot(att_ref[gi, rows, :], vp_s[gi])
            o = o * lax.rsqrt(jnp.mean(o * o, axis=-1, keepdims=True) + NORM_EPS) * g_ref[...]
            z = z_ref[rows, gi * dk:(gi + 1) * dk].astype(F32)
            o_ref[rows, gi * dk:(gi + 1) * dk] = (o * _silu(z)).astype(o_ref.dtype)


def gated_deltanet_layer(x, hn, w_in, conv_w, a_log, dt_bias, norm_g, w_out, batch, seq):
    m, d = x.shape
    dk = C_HEAD_DIM
    k_heads = d // dk
    v_heads = 2 * k_heads
    conv_ch = 2 * k_heads * dk + v_heads * dk
    main_w = conv_ch + v_heads * dk
    qkv = gdn_conv_proj(hn, w_in[:, :conv_ch].astype(BF16), conv_w, seq)
    z = matmul(hn, w_in[:, conv_ch:main_w].astype(BF16), BF16, name="gdn_gate_proj")
    pad = jnp.zeros((d, LANES - v_heads), F32)
    w_ba = jnp.concatenate([w_in[:, main_w:main_w + v_heads], pad, w_in[:, main_w + v_heads:], pad], axis=1)
    row_pad = jnp.zeros((LANES - v_heads,), F32)
    alog_row = jnp.concatenate([a_log, row_pad]).reshape(1, LANES)
    dtb_row = jnp.concatenate([dt_bias, row_pad]).reshape(1, LANES)
    beta_all, gc_all, gc_t = gdn_gates(hn, w_ba.astype(BF16), alog_row, dtb_row)

    rows = min(PREP_ROWS, seq)
    nr = seq // rows
    ncr = rows // CHUNK
    hv = v_heads
    nv = v_heads // k_heads
    nchain = nv * rows // (2 * CHUNK)
    npair = rows // (2 * CHUNK)
    bh_t = lambda dt, last: jax.ShapeDtypeStruct((batch, hv, seq, last), dt)
    blk4 = lambda last: pl.BlockSpec((None, nv, rows, last), lambda b, h, i: (b, h, i, 0))
    qoff, koff, voff = 0, k_heads, 2 * k_heads // nv
    w_, qg_, u_, att_, kdt_, dec_ = pl.pallas_call(
        _gdn_prep_kernel,
        out_shape=(bh_t(BF16, dk), bh_t(BF16, dk), bh_t(BF16, dk), bh_t(BF16, 2 * CHUNK),
                   jax.ShapeDtypeStruct((batch, hv, dk, seq), BF16),
                   jax.ShapeDtypeStruct((batch, hv, seq // CHUNK, dk), F32)),
        grid=(batch, k_heads, nr),
        in_specs=[pl.BlockSpec((rows, dk), lambda b, h, i: (b * nr + i, qoff + h)),
                  pl.BlockSpec((rows, dk), lambda b, h, i: (b * nr + i, koff + h)),
                  pl.BlockSpec((rows, nv * dk), lambda b, h, i: (b * nr + i, voff + h)),
                  pl.BlockSpec((rows, LANES), lambda b, h, i: (b * nr + i, 0)),
                  pl.BlockSpec((rows, LANES), lambda b, h, i: (b * nr + i, 0)),
                  pl.BlockSpec((LANES, rows), lambda b, h, i: (0, b * nr + i))],
        out_specs=(blk4(dk), blk4(dk), blk4(dk), blk4(2 * CHUNK),
                   pl.BlockSpec((None, nv, dk, rows), lambda b, h, i: (b, h, 0, i)),
                   pl.BlockSpec((None, nv, ncr, dk), lambda b, h, i: (b, h, i, 0))),
        scratch_shapes=[pltpu.VMEM((nchain, dk, dk), F32), pltpu.VMEM((nchain, dk, dk), F32),
                        pltpu.VMEM((npair, dk, dk), BF16), pltpu.VMEM((nchain, dk, dk), BF16),
                        pltpu.VMEM((nchain, dk, 2 * dk), BF16), pltpu.VMEM((nchain, dk, dk), BF16),
                        pltpu.VMEM((nchain, dk, dk), F32), pltpu.VMEM((npair, dk, dk), BF16),
                        pltpu.VMEM((nchain, dk, dk), F32)],
        compiler_params=_params("parallel", "parallel", "parallel"),
        name="gdn_prep",
    )(qkv, qkv, qkv, beta_all, gc_all, gc_t)

    rows = min(SCAN_ROWS, seq)
    nr = seq // rows
    ncr = rows // CHUNK
    group = GDN_SCAN_HEADS
    gblk = lambda last: pl.BlockSpec((None, group, rows, last), lambda b, h, i: (b, h, i, 0))
    o = pl.pallas_call(
        functools.partial(_gdn_scan_kernel, group=group),
        out_shape=jax.ShapeDtypeStruct((m, hv * dk), BF16),
        grid=(batch, hv // group, nr),
        in_specs=[gblk(dk), gblk(dk), gblk(dk), gblk(2 * CHUNK),
                  pl.BlockSpec((None, group, dk, rows), lambda b, h, i: (b, h, 0, i)),
                  pl.BlockSpec((None, group, ncr, dk), lambda b, h, i: (b, h, i, 0)),
                  pl.BlockSpec((rows, group * dk), lambda b, h, i: (b * nr + i, h)),
                  pl.BlockSpec((1, dk), lambda b, h, i: (0, 0))],
        out_specs=pl.BlockSpec((rows, group * dk), lambda b, h, i: (b * nr + i, h)),
        scratch_shapes=[pltpu.VMEM((group, dk, dk), F32), pltpu.VMEM((group, 2 * CHUNK, dk), F32),
                        pltpu.VMEM((group, 2 * CHUNK, dk), BF16)],
        compiler_params=_params("parallel", "parallel", "arbitrary"),
        name="gdn_scan",
    )(w_, qg_, u_, att_, kdt_, dec_, z, norm_g.reshape(1, dk))
    return matmul_residual(o, w_out.astype(BF16), x, tm=MM_ROWS // 2, name="gdn_out_proj")


def _rwkv_mix_kernel(x_ref, g_ref, mu_ref, o_ref, tail_ref, *, rows_per_seq):
    i = pl.program_id(0)
    x = x_ref[...]
    tm = x.shape[0]
    hn = x * lax.rsqrt(jnp.mean(x * x, axis=-1, keepdims=True) + NORM_EPS) * g_ref[...]

    @pl.when((i * tm) % rows_per_seq == 0)
    def _():
        tail_ref[...] = jnp.zeros(tail_ref.shape, F32)

    prev = jnp.where(_iota(hn.shape, 0) == 0, tail_ref[7:8, :], pltpu.roll(hn, 1, 0))
    tail_ref[...] = hn[tm - 8:]
    xx = prev - hn
    for c in range(o_ref.shape[0]):
        o_ref[c] = (hn + xx * mu_ref[c:c + 1, :]).astype(o_ref.dtype)


def rwkv_token_mix(x, norm_g, mu, seq, *, tm=ROW_TILE // 2):
    m, d = x.shape
    tm = min(tm, seq)
    nmix = mu.shape[0]
    return pl.pallas_call(
        functools.partial(_rwkv_mix_kernel, rows_per_seq=seq),
        out_shape=jax.ShapeDtypeStruct((nmix, m, d), BF16),
        grid=(m // tm,),
        in_specs=[pl.BlockSpec((tm, d), lambda i: (i, 0)),
                  pl.BlockSpec((1, d), lambda i: (0, 0)),
                  pl.BlockSpec((nmix, d), lambda i: (0, 0))],
        out_specs=pl.BlockSpec((nmix, tm, d), lambda i: (0, i, 0)),
        scratch_shapes=[pltpu.VMEM((8, d), F32)],
        compiler_params=_params("arbitrary"),
        name="rwkv_token_mix",
    )(x, norm_g.reshape(1, d), mu)


def _grouped_mm_kernel(a_ref, w_ref, o_ref):
    o_ref[...] = _dot(a_ref[...], w_ref[...]).astype(o_ref.dtype)


def grouped_matmul(a, w, out_dtype, *, tm=MM_ROWS, tn=MM_COLS):
    g, k, n = w.shape
    m = a.shape[1]
    tm, tn = min(tm, m), min(tn, n)
    return pl.pallas_call(
        _grouped_mm_kernel,
        out_shape=jax.ShapeDtypeStruct((g, m, n), out_dtype),
        grid=(g, n // tn, m // tm),
        in_specs=[pl.BlockSpec((None, tm, k), lambda c, j, i: (c, i, 0)),
                  pl.BlockSpec((None, k, tn), lambda c, j, i: (c, 0, j))],
        out_specs=pl.BlockSpec((None, tm, tn), lambda c, j, i: (c, i, j)),
        compiler_params=_params("parallel", "parallel", "parallel"),
        name="rwkv_rkvg_proj",
    )(a, w)


def _rwkv_lora_kernel(xw_ref, xa_ref, w1_ref, w2_ref, a1_ref, a2_ref, w0_ref, a0_ref, ld_ref, a_ref):
    hw = jnp.tanh(_dot(xw_ref[...], w1_ref[...])).astype(BF16)
    lw = w0_ref[...] + _dot(hw, w2_ref[...])
    log_w = -_softplus(-lw) - 0.5
    ld_ref[...] = -jnp.exp(log_w)
    ha = _dot(xa_ref[...], a1_ref[...]).astype(BF16)
    a_ref[...] = jax.nn.sigmoid(a0_ref[...] + _dot(ha, a2_ref[...]))


def rwkv_lora(xs, w1, w2, a1, a2, w0, a0, *, tm=ROW_TILE):
    _, m, d = xs.shape
    tm = min(tm, m)
    r = w1.shape[1]
    row = pl.BlockSpec((1, d), lambda i: (0, 0))
    shape = jax.ShapeDtypeStruct((m, d), F32)
    return pl.pallas_call(
        _rwkv_lora_kernel,
        out_shape=(shape, shape),
        grid=(m // tm,),
        in_specs=[pl.BlockSpec((None, tm, d), lambda i: (4, i, 0)),
                  pl.BlockSpec((None, tm, d), lambda i: (5, i, 0)),
                  pl.BlockSpec((d, r), lambda i: (0, 0)), pl.BlockSpec((r, d), lambda i: (0, 0)),
                  pl.BlockSpec((d, r), lambda i: (0, 0)), pl.BlockSpec((r, d), lambda i: (0, 0)),
                  row, row],
        out_specs=(pl.BlockSpec((tm, d), lambda i: (i, 0)), pl.BlockSpec((tm, d), lambda i: (i, 0))),
        compiler_params=_params("parallel"),
        name="rwkv_lora",
    )(xs, xs, w1, w2, a1, a2, w0.reshape(1, d), a0.reshape(1, d))


def _head_sum(x, first_head):
    lo = jnp.sum(jnp.where(first_head, x, 0.0), axis=-1, keepdims=True)
    hi = jnp.sum(jnp.where(first_head, 0.0, x), axis=-1, keepdims=True)
    return jnp.where(first_head, lo, hi)


def _rwkv_prep_kernel(r_ref, k_ref, v_ref, ld_ref, a_ref, kk_ref, ka_ref, rk_ref,
                      wt_ref, rt_ref, u_ref, aro_ref, kbt_ref, vb_ref, dec_ref, bonus_ref,
                      kkt_s, kb_s, akk_s, p_s, t_s, drow_s, *, cps):
    c = CHUNK
    n2 = 2 * c
    n_h = B_HEAD_DIM
    npairs = r_ref.shape[0] // n2
    first_head = _iota((1, LANES), 1) < n_h
    tri, strict, diag = _pair_masks()
    tri_b = tri.astype(BF16)
    eye_b = diag.astype(BF16)
    eye_f = diag.astype(F32)
    upper_half = _iota((n2, LANES), 0) < c
    pairs = [slice(n * n2, (n + 1) * n2) for n in range(npairs)]

    for n, rows in enumerate(pairs):
        r, k, v, a = r_ref[rows, :], k_ref[rows, :], v_ref[rows, :], a_ref[rows, :]
        kk_raw = k * kk_ref[...]
        kk = kk_raw * lax.rsqrt(_head_sum(kk_raw * kk_raw, first_head) + 1e-6)
        k2c = k * (1.0 + (a - 1.0) * ka_ref[...])
        bbc = kk * a
        bonus_ref[rows, :] = (_head_sum(r * k2c * rk_ref[...], first_head) * v).astype(bonus_ref.dtype)
        vb_ref[rows, :] = v.astype(BF16)
        ld = ld_ref[rows, :]
        cs = _dot_x2r(tri_b, ld)
        cs_last = jnp.where(upper_half, cs[c - 1:c, :], cs[n2 - 1:n2, :])
        w_inv = jnp.exp(-cs)
        w_end = jnp.exp(cs_last - cs)
        rt_ref[rows, :] = (r * jnp.exp(cs)).astype(BF16)
        kkt_s[n] = (kk * jnp.exp(cs - ld)).astype(BF16)
        kb_s[n] = jnp.concatenate([k2c * w_inv, bbc * w_inv], axis=0).astype(BF16)
        kw, bw = k2c * w_end, -(bbc * w_end)
        for half in range(2):
            hs = slice(half * c, (half + 1) * c)
            kb_end = jnp.concatenate([kw[hs], bw[hs]], axis=0)
            cols = slice((2 * n + half) * n2, (2 * n + half + 1) * n2)
            kbt_ref[:, cols] = kb_end.T.astype(BF16)
        blk, off = divmod(2 * n, cps)
        drow_s[blk, off:off + 1, :] = jnp.exp(cs[c - 1:c, :])
        drow_s[blk, off + 1:off + 2, :] = jnp.exp(cs[n2 - 1:n2, :])
    for blk in range(dec_ref.shape[0] // LANES):
        drow_s[blk, cps:, :] = jnp.zeros((LANES - cps, LANES), F32)
        dec_ref[blk * LANES:(blk + 1) * LANES, :] = _dot_xr(eye_b, drow_s[blk], NT)
    zero_b = jnp.zeros((n2, LANES), BF16)
    strict2 = jnp.concatenate([strict, strict], axis=1)
    tri2 = jnp.concatenate([tri, tri], axis=1)
    sign2 = jnp.where(_iota((n2, 2 * n2), 1) < n2, 1.0, -1.0)
    for n, rows in enumerate(pairs):
        kkt, rt = kkt_s[n], rt_ref[rows, :]
        lhs = jnp.concatenate([jnp.where(first_head, kkt, zero_b), jnp.where(first_head, rt, zero_b),
                               jnp.where(first_head, zero_b, kkt), jnp.where(first_head, zero_b, rt)], axis=0)
        gram = _dot(lhs, kb_s[n], NT)
        for hh in range(2):
            i = 2 * n + hh
            g_kk = jnp.where(strict2, gram[2 * hh * n2:(2 * hh + 1) * n2], 0.0)
            neg_a = -g_kk[:, n2:]
            p_s[i] = neg_a.astype(BF16)
            t_s[i] = eye_f + neg_a
            akk_s[i] = g_kk[:, :n2].astype(BF16)
            g_r = gram[(2 * hh + 1) * n2:(2 * hh + 2) * n2]
            aro_ref[hh, rows, :] = jnp.where(tri2, g_r * sign2, 0.0).astype(BF16)
    _inverse_stages(p_s, t_s, 2 * npairs, c)
    for n, rows in enumerate(pairs):
        av = _dot(jnp.concatenate([akk_s[2 * n], akk_s[2 * n + 1]], axis=0), vb_ref[rows, :])
        p_s[2 * n] = av[:n2].astype(BF16)
        p_s[2 * n + 1] = av[n2:].astype(BF16)
    for n, rows in enumerate(pairs):
        t0, t1 = t_s[2 * n].astype(BF16), t_s[2 * n + 1].astype(BF16)
        wt = _dot(jnp.concatenate([t0, t1], axis=0), kkt_s[n])
        wt_ref[rows, :] = jnp.where(first_head, wt[:n2], wt[n2:]).astype(BF16)
        u_ref[rows, :] = jnp.where(first_head, _dot(t0, p_s[2 * n]), _dot(t1, p_s[2 * n + 1])).astype(u_ref.dtype)


def _rwkv_scan_kernel(wt_ref, rt_ref, u_ref, aro_ref, kbt_ref, vb_ref, dec_ref, bonus_ref, gate_ref,
                      lnw_ref, lnb_ref, o_ref, s_ref, ms_s, xc_s, xp_s, *, group):
    c = CHUNK
    n2 = 2 * c
    n_h = B_HEAD_DIM
    first_head = _iota((1, LANES), 1) < n_h
    same_head = _iota((LANES, LANES), 0) // n_h == _iota((LANES, LANES), 1) // n_h
    zeros = jnp.zeros((c, LANES), BF16)

    def head_mean(x):
        return _head_sum(x, first_head) * (1.0 / n_h)

    @pl.when(pl.program_id(2) == 0)
    def _():
        s_ref[...] = jnp.zeros(s_ref.shape, F32)

    for n in range(u_ref.shape[1] // c):
        rows = slice(n * c, (n + 1) * c)
        for gi in range(group):
            lhs = jnp.concatenate([wt_ref[gi, rows, :], rt_ref[gi, rows, :]], axis=0)
            ms_s[gi] = _dot(lhs, s_ref[gi].astype(BF16))
        for gi in range(group):
            sa = (u_ref[gi, rows, :] + ms_s[gi, :c, :]).astype(BF16)
            vb = vb_ref[gi, rows, :]
            xc_s[gi] = jnp.concatenate([vb, sa], axis=0)
            xp_s[gi] = jnp.concatenate([vb, zeros, sa, zeros] if n % 2 == 0 else [zeros, vb, zeros, sa], axis=0)
        for gi in range(group):
            upd = _dot(kbt_ref[gi, :, n * n2:(n + 1) * n2], xc_s[gi])
            dec_col = jnp.broadcast_to(dec_ref[gi, :, n:n + 1], (LANES, LANES))
            s_ref[gi] = s_ref[gi] * dec_col + jnp.where(same_head, upd, 0.0)
        for gi in range(group):
            both = _dot(jnp.concatenate([aro_ref[gi, 0, rows, :], aro_ref[gi, 1, rows, :]], axis=0), xp_s[gi])
            o = ms_s[gi, c:, :] + jnp.where(first_head, both[:c], both[c:])
            mean = head_mean(o)
            dlt = o - mean
            var = head_mean(dlt * dlt)
            cols = slice(gi * LANES, (gi + 1) * LANES)
            y = dlt * lax.rsqrt(var + B_GN_EPS) * lnw_ref[:, cols] + lnb_ref[:, cols]
            y = y + bonus_ref[rows, cols]
            o_ref[rows, cols] = (y * _silu(gate_ref[rows, cols])).astype(o_ref.dtype)


def rwkv7_layer(x, norm_g, mu, w_rkvg, w0, w_w1, w_w2, a0, w_a1, w_a2, k_k, k_a, r_k, ln_w, ln_b, w_out,
                batch, seq):
    m, d = x.shape
    heads = d // B_HEAD_DIM
    pairs = heads // 2
    order = jnp.array([0, 2, 3, 5, 1, 4])
    xs = rwkv_token_mix(x, norm_g, mu[order], seq)
    rkvg = grouped_matmul(xs, w_rkvg.astype(BF16), F32)
    lora = w_w1.shape[1]
    padc = lambda w: jnp.pad(w, ((0, 0), (0, LORA_PAD - lora))).astype(BF16)
    padr = lambda w: jnp.pad(w, ((0, LORA_PAD - lora), (0, 0))).astype(BF16)
    ld, a = rwkv_lora(xs, padc(w_w1), padr(w_w2), padc(w_a1), padr(w_a2), w0, a0)

    srows = min(SCAN_ROWS, seq)
    rows = min(PREP_ROWS, seq)
    nr = seq // rows
    npair = rows // (2 * CHUNK)
    col = lambda g: pl.BlockSpec((None, rows, LANES), lambda b, p, i, g=g: (g, b * nr + i, p))
    flat = pl.BlockSpec((rows, LANES), lambda b, p, i: (b * nr + i, p))
    prow = pl.BlockSpec((1, LANES), lambda b, p, i: (0, p))
    bp = lambda rws, last, dt: jax.ShapeDtypeStruct((batch, pairs, rws, last), dt)
    pblk = lambda rws, last: pl.BlockSpec((None, None, rws, last), lambda b, p, i: (b, p, i, 0))
    wt_, rt_, u_, aro_, kbt_, vb_, dec_, bonus = pl.pallas_call(
        functools.partial(_rwkv_prep_kernel, cps=srows // CHUNK),
        out_shape=(bp(seq, LANES, BF16), bp(seq, LANES, BF16), bp(seq, LANES, BF16),
                   jax.ShapeDtypeStruct((batch, pairs, 2, seq, 4 * CHUNK), BF16),
                   bp(LANES, 2 * seq, BF16), bp(seq, LANES, BF16), bp(seq // srows * LANES, LANES, F32),
                   jax.ShapeDtypeStruct((m, d), BF16)),
        grid=(batch, pairs, nr),
        in_specs=[col(0), col(1), col(2), flat, flat, prow, prow, prow],
        out_specs=(pblk(rows, LANES), pblk(rows, LANES), pblk(rows, LANES),
                   pl.BlockSpec((None, None, 2, rows, 4 * CHUNK), lambda b, p, i: (b, p, 0, i, 0)),
                   pl.BlockSpec((None, None, LANES, 2 * rows), lambda b, p, i: (b, p, 0, i)),
                   pblk(rows, LANES), pblk(rows // srows * LANES, LANES), flat),
        scratch_shapes=[pltpu.VMEM((npair, LANES, LANES), BF16), pltpu.VMEM((npair, 2 * LANES, LANES), BF16),
                        pltpu.VMEM((2 * npair, LANES, LANES), BF16), pltpu.VMEM((2 * npair, LANES, LANES), BF16),
                        pltpu.VMEM((2 * npair, LANES, LANES), F32),
                        pltpu.VMEM((rows // srows, LANES, LANES), F32)],
        compiler_params=_params("parallel", "parallel", "parallel"),
        name="rwkv_prep",
    )(rkvg, rkvg, rkvg, ld, a, k_k.reshape(1, d), k_a.reshape(1, d), r_k.reshape(1, d))

    rows = srows
    nr = seq // rows
    group = RWKV_SCAN_PAIRS
    gw = group * LANES
    gblk = lambda rws, last: pl.BlockSpec((None, group, rws, last), lambda b, p, i: (b, p, i, 0))
    gflat = pl.BlockSpec((rows, gw), lambda b, p, i: (b * nr + i, p))
    grow = pl.BlockSpec((1, gw), lambda b, p, i: (0, p))
    o = pl.pallas_call(
        functools.partial(_rwkv_scan_kernel, group=group),
        out_shape=jax.ShapeDtypeStruct((m, d), BF16),
        grid=(batch, pairs // group, nr),
        in_specs=[gblk(rows, LANES), gblk(rows, LANES), gblk(rows, LANES),
                  pl.BlockSpec((None, group, 2, rows, 4 * CHUNK), lambda b, p, i: (b, p, 0, i, 0)),
                  pl.BlockSpec((None, group, LANES, 2 * rows), lambda b, p, i: (b, p, 0, i)),
                  gblk(rows, LANES), gblk(LANES, LANES), gflat,
                  pl.BlockSpec((None, rows, gw), lambda b, p, i: (3, b * nr + i, p)),
                  grow, grow],
        out_specs=gflat,
        scratch_shapes=[pltpu.VMEM((group, LANES, LANES), F32), pltpu.VMEM((group, 2 * CHUNK, LANES), F32),
                        pltpu.VMEM((group, 2 * CHUNK, LANES), BF16), pltpu.VMEM((group, 4 * CHUNK, LANES), BF16)],
        compiler_params=_params("parallel", "parallel", "arbitrary"),
        name="rwkv_scan",
    )(wt_, rt_, u_, aro_, kbt_, vb_, dec_, bonus, rkvg, ln_w.reshape(1, d), ln_b.reshape(1, d))
    return matmul_residual(o, w_out.astype(BF16), x, name="rwkv_out_proj")


def kernel(x, p, positions, norm_g, pe_norm_g, pe_w_gate, pe_w_proj, final_norm_g, a_w_in, a_lam, a_subln_g, a_w_out, b_mu, b_w_rkvg, b_w0, b_w_w1, b_w_w2, b_a0, b_w_a1, b_w_a2, b_k_k, b_k_a, b_r_k, b_ln_w, b_ln_b, b_w_out, c_w_in, c_conv_w, c_A_log, c_dt_bias, c_norm_g, c_w_out):
    batch, seq, d = x.shape
    depth = p.shape[0]
    m = batch * seq
    xf = x.reshape(m, d)
    tables = rope_tables(positions)
    for i in range(depth):
        kind = i % N_MIXERS
        j = i // N_MIXERS
        if kind == 0:
            lam_init = 0.8 - 0.6 * math.exp(-0.3 * i)
            hn = rmsnorm(xf, norm_g[i], BF16)
            xf, xb, ssq = diff_attention_layer(xf, hn, tables, a_w_in[j], a_lam[j], a_subln_g[j], a_w_out[j],
                                      batch, seq, lam_init)
        elif kind == 1:
            xf, xb, ssq = rwkv7_layer(xf, norm_g[i], b_mu[j], b_w_rkvg[j], b_w0[j], b_w_w1[j], b_w_w2[j], b_a0[j],
                             b_w_a1[j], b_w_a2[j], b_k_k[j], b_k_a[j], b_r_k[j], b_ln_w[j], b_ln_b[j],
                             b_w_out[j], batch, seq)
        else:
            hn = rmsnorm(xf, norm_g[i], BF16)
            xf, xb, ssq = gated_deltanet_layer(xf, hn, c_w_in[j], c_conv_w[j], c_A_log[j], c_dt_bias[j], c_norm_g[j],
                                      c_w_out[j], batch, seq)
        xf = per_layer_embedding(xf, xb, ssq, pe_norm_g[i], pe_w_gate[i], p[i].reshape(m, -1), pe_w_proj[i])
    return rmsnorm(xf, final_norm_g, F32).reshape(batch, seq, d)
```

```python
import functools
import math

import jax
import jax.numpy as jnp
from jax import lax
from jax.experimental import pallas as pl
from jax.experimental.pallas import tpu as pltpu

F32 = jnp.float32
BF16 = jnp.bfloat16

N_MIXERS = 3
NORM_EPS = 1e-6
LANES = 128
VMEM_LIMIT = 48 * 1024 * 1024

A_HEAD_DIM = 128
A_V_DIM = 2 * A_HEAD_DIM
ROT_DIM = A_HEAD_DIM // 4
ROPE_THETA = 500000.0
SUBLN_EPS = 1e-5
ATTN_BLOCK = 512
ATTN_Q_BLOCKS = 1
ONES_ROWS = 16

B_HEAD_DIM = 64
B_GN_EPS = 64e-5
LORA_PAD = 128

C_HEAD_DIM = 128
C_CONV_WIDTH = 4
CHUNK = 64
SCAN_ROWS = 512
PREP_ROWS = 1024
SLAB_ROWS = 32
GDN_SCAN_HEADS = 16
RWKV_SCAN_PAIRS = 8

MM_ROWS = 1024
MM_COLS = 1024
ROW_TILE = 512
EPI_COLS = 2048
EPI_SUB = 256

NN = (((1,), (0,)), ((), ()))
NT = (((1,), (1,)), ((), ()))


def _dot(a, b, dims=NN):
    return lax.dot_general(a, b, dims, preferred_element_type=F32)


def _split2(x):
    hi = x.astype(BF16)
    lo = (x - hi.astype(F32)).astype(BF16)
    return hi, lo


def _split3(x):
    hi = x.astype(BF16)
    r = x - hi.astype(F32)
    mid = r.astype(BF16)
    lo = (r - mid.astype(F32)).astype(BF16)
    return hi, mid, lo


def _dot_xr(a_exact, b, dims=NN):
    h, m, l = _split3(b)
    return _dot(a_exact, h, dims) + (_dot(a_exact, m, dims) + _dot(a_exact, l, dims))


def _dot_x2r(a_exact, b, dims=NN):
    h, l = _split2(b)
    return _dot(a_exact, h, dims) + _dot(a_exact, l, dims)


def _iota(shape, dim):
    return lax.broadcasted_iota(jnp.int32, shape, dim)


def _silu(x):
    return x * jax.nn.sigmoid(x)


def _softplus(x):
    return jnp.maximum(x, 0.0) + jnp.log(1.0 + jnp.exp(-jnp.abs(x)))


def _params(*sem):
    return pltpu.CompilerParams(dimension_semantics=sem, vmem_limit_bytes=VMEM_LIMIT)


def _rmsnorm_kernel(x_ref, g_ref, o_ref, *, eps):
    x = x_ref[...]
    y = x * lax.rsqrt(jnp.mean(x * x, axis=-1, keepdims=True) + eps)
    o_ref[...] = (y * g_ref[...]).astype(o_ref.dtype)


def rmsnorm(x, g, out_dtype, *, eps=NORM_EPS, tm=ROW_TILE):
    m, d = x.shape
    tm = min(tm, m)
    return pl.pallas_call(
        functools.partial(_rmsnorm_kernel, eps=eps),
        out_shape=jax.ShapeDtypeStruct((m, d), out_dtype),
        grid=(m // tm,),
        in_specs=[pl.BlockSpec((tm, d), lambda i: (i, 0)), pl.BlockSpec((1, d), lambda i: (0, 0))],
        out_specs=pl.BlockSpec((tm, d), lambda i: (i, 0)),
        compiler_params=_params("parallel"),
        name="rmsnorm",
    )(x, g.reshape(1, d))


def _mm_kernel(a_ref, w_ref, *rest, epilogue):
    o_ref = rest[-1]
    acc = _dot(a_ref[...], w_ref[...])
    if epilogue is not None:
        acc = epilogue(acc, *rest[:-1])
    o_ref[...] = acc.astype(o_ref.dtype)


def matmul(a, w, out_dtype, *, tm=MM_ROWS, tn=MM_COLS, extra=(), extra_specs=(), epilogue=None, name="matmul"):
    m, k = a.shape
    n = w.shape[1]
    tm, tn = min(tm, m), min(tn, n)
    return pl.pallas_call(
        functools.partial(_mm_kernel, epilogue=epilogue),
        out_shape=jax.ShapeDtypeStruct((m, n), out_dtype),
        grid=(n // tn, m // tm),
        in_specs=[pl.BlockSpec((tm, k), lambda j, i: (i, 0)),
                  pl.BlockSpec((k, tn), lambda j, i: (0, j))] + list(extra_specs),
        out_specs=pl.BlockSpec((tm, tn), lambda j, i: (i, j)),
        compiler_params=_params("parallel", "parallel"),
        name=name,
    )(a, w, *extra)


def _out_proj_kernel(a_ref, w_ref, res_ref, x_ref, xb_ref, ssq_ref):
    x = res_ref[...] + _dot(a_ref[...], w_ref[...])
    x_ref[...] = x
    xb_ref[...] = x.astype(BF16)
    ssq_ref[...] = jnp.sum(x * x, axis=-1, keepdims=True)


def matmul_residual(a, w, res, *, tm=MM_ROWS, tn=MM_COLS, name="matmul_residual"):
    m, k = a.shape
    n = w.shape[1]
    tm, tn = min(tm, m), min(tn, n)
    blk = pl.BlockSpec((tm, tn), lambda j, i: (i, j))
    return pl.pallas_call(
        _out_proj_kernel,
        out_shape=(jax.ShapeDtypeStruct((m, n), F32), jax.ShapeDtypeStruct((m, n), BF16),
                   jax.ShapeDtypeStruct((n // tn, m, 1), F32)),
        grid=(n // tn, m // tm),
        in_specs=[pl.BlockSpec((tm, k), lambda j, i: (i, 0)), pl.BlockSpec((k, tn), lambda j, i: (0, j)), blk],
        out_specs=(blk, blk, pl.BlockSpec((None, tm, 1), lambda j, i: (j, i, 0))),
        compiler_params=_params("parallel", "parallel"),
        name=name,
    )(a, w, res)


def _ple_kernel(xb_ref, ssq_ref, wg_ref, p_ref, wp_ref, x_ref, o_ref, *, eps):
    d = xb_ref.shape[1]
    inv_rms = lax.rsqrt(jnp.sum(ssq_ref[...], axis=0) / d + eps)
    gate = jax.nn.sigmoid(inv_rms * _dot(xb_ref[...], wg_ref[...]))
    proj = _dot(p_ref[...].astype(BF16), wp_ref[...])
    o_ref[...] = x_ref[...] + gate * proj


def per_layer_embedding(x, xb, ssq, norm_g, w_gate, p, w_proj, *, tm=MM_ROWS, tn=MM_COLS):
    m, d = x.shape
    pd = p.shape[1]
    nparts = ssq.shape[0]
    tm, tn = min(tm, m), min(tn, d)
    wg = (norm_g[:, None] * w_gate).astype(BF16)
    return pl.pallas_call(
        functools.partial(_ple_kernel, eps=NORM_EPS),
        out_shape=jax.ShapeDtypeStruct((m, d), F32),
        grid=(d // tn, m // tm),
        in_specs=[pl.BlockSpec((tm, d), lambda j, i: (i, 0)),
                  pl.BlockSpec((nparts, tm, 1), lambda j, i: (0, i, 0)),
                  pl.BlockSpec((d, tn), lambda j, i: (0, j)),
                  pl.BlockSpec((tm, pd), lambda j, i: (i, 0)),
                  pl.BlockSpec((pd, tn), lambda j, i: (0, j)),
                  pl.BlockSpec((tm, tn), lambda j, i: (i, j))],
        out_specs=pl.BlockSpec((tm, tn), lambda j, i: (i, j)),
        compiler_params=_params("parallel", "parallel"),
        name="per_layer_embedding",
    )(xb, ssq, wg, p, w_proj.astype(BF16), x)


def _rope_table_kernel(pos_ref, freq_ref, cos_ref, sin_lo_ref, sin_hi_ref):
    half = ROT_DIM // 2
    ang = pos_ref[...].astype(F32) * freq_ref[...]
    lane = _iota(ang.shape, 1)
    c, s = jnp.cos(ang), jnp.sin(ang)
    cos_ref[...] = jnp.where(lane < ROT_DIM, c, 1.0)
    sin_lo_ref[...] = jnp.where(lane < half, -s, 0.0)
    sin_hi_ref[...] = jnp.where((lane >= half) & (lane < ROT_DIM), s, 0.0)


def rope_tables(positions, *, tm=MM_ROWS):
    m = positions.size
    tm = min(tm, m)
    inv_freq = ROPE_THETA ** (-jnp.arange(0, ROT_DIM, 2, dtype=F32) / ROT_DIM)
    freq_row = jnp.concatenate([inv_freq, inv_freq, jnp.zeros((LANES - ROT_DIM,), F32)]).reshape(1, LANES)
    shape = jax.ShapeDtypeStruct((m, LANES), F32)
    spec = pl.BlockSpec((tm, LANES), lambda i: (i, 0))
    return pl.pallas_call(
        _rope_table_kernel,
        out_shape=(shape, shape, shape),
        grid=(m // tm,),
        in_specs=[pl.BlockSpec((tm, 1), lambda i: (i, 0)), pl.BlockSpec((1, LANES), lambda i: (0, 0))],
        out_specs=(spec, spec, spec),
        compiler_params=_params("parallel"),
        name="rope_tables",
    )(positions.reshape(m, 1), freq_row)


def _attn_qk_kernel(a_ref, w_ref, cos_ref, sin_lo_ref, sin_hi_ref, o_ref, *, n_q_blocks, sub):
    j = pl.program_id(0)
    half = ROT_DIM // 2
    scale = jnp.where(j < n_q_blocks, A_HEAD_DIM ** -0.5, 1.0).astype(F32)
    cos, sin_lo, sin_hi = cos_ref[...], sin_lo_ref[...], sin_hi_ref[...]
    a = a_ref[...]

    def epilogue(acc, c0):
        for g in range(acc.shape[1] // LANES):
            x = acc[:, g * LANES:(g + 1) * LANES]
            y = x * cos + pltpu.roll(x, LANES - half, 1) * sin_lo + pltpu.roll(x, half, 1) * sin_hi
            o_ref[:, c0 + g * LANES:c0 + (g + 1) * LANES] = (y * scale).astype(o_ref.dtype)

    starts = list(range(0, w_ref.shape[1], sub))
    acc = _dot(a, w_ref[:, starts[0]:starts[0] + sub])
    for c, c0 in enumerate(starts):
        nxt = _dot(a, w_ref[:, starts[c + 1]:starts[c + 1] + sub]) if c + 1 < len(starts) else None
        epilogue(acc, c0)
        acc = nxt


def attn_qk_proj(hn, w_qk, tables, *, tm=MM_ROWS, tn=EPI_COLS, sub=EPI_SUB):
    m, k = hn.shape
    n = w_qk.shape[1]
    tm, tn = min(tm, m), min(tn, n // 2)
    tspec = pl.BlockSpec((tm, LANES), lambda j, i: (i, 0))
    return pl.pallas_call(
        functools.partial(_attn_qk_kernel, n_q_blocks=n // 2 // tn, sub=min(sub, tn)),
        out_shape=jax.ShapeDtypeStruct((m, n), BF16),
        grid=(n // tn, m // tm),
        in_specs=[pl.BlockSpec((tm, k), lambda j, i: (i, 0)),
                  pl.BlockSpec((k, tn), lambda j, i: (0, j)), tspec, tspec, tspec],
        out_specs=pl.BlockSpec((tm, tn), lambda j, i: (i, j)),
        compiler_params=_params("parallel", "parallel"),
        name="attn_qk_proj",
    )(hn, w_qk, *tables)


def _attn_vt_kernel(w_ref, a_ref, o_ref):
    o_ref[...] = _dot(w_ref[...], a_ref[...], NT).astype(o_ref.dtype)


def attn_v_proj_t(hn, w_t, batch, seq, blk, *, tn=MM_COLS):
    m, k = hn.shape
    n = w_t.shape[0]
    tn = min(tn, n)
    nk = seq // blk
    return pl.pallas_call(
        _attn_vt_kernel,
        out_shape=jax.ShapeDtypeStruct((batch, nk, n, blk), BF16),
        grid=(n // tn, m // blk),
        in_specs=[pl.BlockSpec((tn, k), lambda j, i: (j, 0)),
                  pl.BlockSpec((blk, k), lambda j, i: (i, 0))],
        out_specs=pl.BlockSpec((None, None, tn, blk), lambda j, i: (i // nk, i % nk, j, 0)),
        compiler_params=_params("parallel", "parallel"),
        name="attn_v_proj_t",
    )(w_t, hn)


def _diff_attn_kernel(lam_ref, q_ref, k_ref, vt_ref, z_ref, g_ref, o_ref, m_ref, acc_ref, s_ref,
                      *, bq, bk, lam_init):
    i = pl.program_id(2)
    lam = lam_ref[...]
    lam_full = (jnp.exp(jnp.sum(lam[0:1] * lam[1:2], axis=-1, keepdims=True))
                - jnp.exp(jnp.sum(lam[2:3] * lam[3:4], axis=-1, keepdims=True)) + lam_init)
    m_ref[...] = jnp.full(m_ref.shape, -jnp.inf, F32)
    acc_ref[...] = jnp.zeros(acc_ref.shape, F32)
    q = q_ref[...]
    d = A_HEAD_DIM
    w = A_V_DIM
    ones = jnp.ones((ONES_ROWS, bk), BF16)

    def scores(j, slot, q_lo=0):
        start = pl.multiple_of(j * bk, bk)
        kb = k_ref[pl.ds(start, bk), :]
        for c in range(2):
            s_ref[slot, c, :, q_lo:] = _dot(kb[:, c * d:(c + 1) * d], q[q_lo:, c * d:(c + 1) * d], NT)

    def absorb(j, slot, masked, q_lo=0, kv_lo=0, kv_len=bk):
        kv = slice(kv_lo, kv_lo + kv_len)
        vt = jnp.concatenate([vt_ref[j][:, kv], ones[:, :kv_len]], axis=0)
        for c in range(2):
            s = s_ref[slot, c, kv, q_lo:]
            if masked:
                kv_pos = j * bk + kv_lo + _iota(s.shape, 0)
                q_pos = i * bq + q_lo + _iota(s.shape, 1)
                s = jnp.where(kv_pos <= q_pos, s, -jnp.inf)
            m_prev = m_ref[c, :, q_lo:]
            m_new = jnp.maximum(m_prev, jnp.max(s, axis=0, keepdims=True))
            alpha = jnp.exp(m_prev - m_new)
            p = jnp.exp(s - m_new)
            acc_ref[c, :, q_lo:] = alpha * acc_ref[c, :, q_lo:] + _dot(vt, p.astype(BF16))
            m_ref[c, :, q_lo:] = m_new

    def absorb_straddling(j, slot, q_lo):
        half = bk // 2
        absorb(j, slot, True, q_lo, 0, half)
        absorb(j, slot, True, q_lo + half, half, half)

    r = bq // bk
    first_masked = r * i
    scores(0, 0)

    def body(jj, carry):
        scores(2 * jj + 1, 1)
        absorb(2 * jj, 0, False)
        scores(2 * jj + 2, 0)
        absorb(2 * jj + 1, 1, False)
        return carry

    lax.fori_loop(0, first_masked // 2, body, 0)

    def tail(odd):
        base = first_masked - odd
        left = [(base + n, n % 2, n >= odd, max(n - odd, 0) * bk) for n in range(odd + r)]
        for n, (j, slot, masked, q_lo) in enumerate(left):
            if n + 1 < len(left):
                nj, nslot, _, nq_lo = left[n + 1]
                scores(nj, nslot, nq_lo)
            if masked:
                absorb_straddling(j, slot, q_lo)
            else:
                absorb(j, slot, False)

    if r % 2 == 0:
        tail(0)
    else:
        pl.when(first_masked % 2 == 0)(lambda: tail(0))
        pl.when(first_masked % 2 == 1)(lambda: tail(1))

    o = (acc_ref[0, :w, :] / acc_ref[0, w:w + 1, :]
         - lam_full * (acc_ref[1, :w, :] / acc_ref[1, w:w + 1, :]))
    o = o * lax.rsqrt(jnp.mean(o * o, axis=0, keepdims=True) + SUBLN_EPS) * g_ref[...]
    o = (o * (1.0 - lam_init)).T
    o_ref[...] = (o * _silu(z_ref[...].astype(F32))).astype(o_ref.dtype)


def diff_attention_core(qk, z, vt, lam, subln_g, batch, seq, heads, lam_init):
    m = qk.shape[0]
    bk = vt.shape[-1]
    bq = ATTN_Q_BLOCKS * bk
    nq = seq // bq
    nk = seq // bk
    w = A_V_DIM
    return pl.pallas_call(
        functools.partial(_diff_attn_kernel, bq=bq, bk=bk, lam_init=lam_init),
        out_shape=jax.ShapeDtypeStruct((m, heads * w), BF16),
        grid=(batch, heads, nq),
        in_specs=[pl.BlockSpec((4, A_HEAD_DIM), lambda b, h, i: (0, 0)),
                  pl.BlockSpec((bq, w), lambda b, h, i: (b * nq + i, h)),
                  pl.BlockSpec((seq, w), lambda b, h, i: (b, heads + h)),
                  pl.BlockSpec((None, nk, w, bk), lambda b, h, i: (b, 0, h, 0)),
                  pl.BlockSpec((bq, w), lambda b, h, i: (b * nq + i, h)),
                  pl.BlockSpec((w, 1), lambda b, h, i: (0, 0))],
        out_specs=pl.BlockSpec((bq, w), lambda b, h, i: (b * nq + i, h)),
        scratch_shapes=[pltpu.VMEM((2, 1, bq), F32), pltpu.VMEM((2, w + ONES_ROWS, bq), F32),
                        pltpu.VMEM((2, 2, bk, bq), F32)],
        compiler_params=_params("parallel", "parallel", "parallel"),
        name="diff_attention",
    )(lam, qk, qk, vt, z, subln_g.reshape(w, 1))


def diff_attention_layer(x, hn, tables, w_in, lam, subln_g, w_out, batch, seq, lam_init):
    d = x.shape[1]
    heads = d // A_V_DIM
    qk_w = heads * 2 * A_HEAD_DIM
    v_w = heads * A_V_DIM
    blk = min(ATTN_BLOCK, seq // ATTN_Q_BLOCKS)
    w_vt = w_in[:, 2 * qk_w:2 * qk_w + v_w].T.astype(BF16)
    qk = attn_qk_proj(hn, w_in[:, :2 * qk_w].astype(BF16), tables)
    z = matmul(hn, w_in[:, 2 * qk_w + v_w:].astype(BF16), BF16, name="attn_gate_proj")
    vt = attn_v_proj_t(hn, w_vt, batch, seq, blk)
    o = diff_attention_core(qk, z, vt, lam, subln_g, batch, seq, heads, lam_init)
    return matmul_residual(o, w_out.astype(BF16), x, name="attn_out_proj")


def _gdn_conv_kernel(a_ref, w_ref, cw_ref, o_ref, tail_ref, *, rows_per_seq, sub):
    i = pl.program_id(1)
    tm = a_ref.shape[0]

    @pl.when((i * tm) % rows_per_seq == 0)
    def _():
        tail_ref[...] = jnp.zeros(tail_ref.shape, F32)

    a = a_ref[...]
    sub_iota = _iota((8, sub), 0)
    last = C_CONV_WIDTH - 1

    def epilogue(acc, cols):
        tail = tail_ref[:, cols]

        def shifted(s):
            xs = pltpu.roll(acc, s, 0)
            head = jnp.where(sub_iota < s, pltpu.roll(tail, s, 0), xs[:8])
            return jnp.concatenate([head, xs[8:]], axis=0)

        cw = cw_ref[:, cols]
        y = shifted(last) * cw[0:1]
        for t in range(1, last):
            y = y + shifted(last - t) * cw[t:t + 1]
        y = y + acc * cw[last:last + 1]
        tail_ref[:, cols] = acc[tm - 8:]
        o_ref[:, cols] = _silu(y).astype(o_ref.dtype)

    blocks = [slice(c0, c0 + sub) for c0 in range(0, w_ref.shape[1], sub)]
    acc = _dot(a, w_ref[:, blocks[0]])
    for c, cols in enumerate(blocks):
        nxt = _dot(a, w_ref[:, blocks[c + 1]]) if c + 1 < len(blocks) else None
        epilogue(acc, cols)
        acc = nxt


def gdn_conv_proj(hn, w, conv_w, seq, *, tm=MM_ROWS, tn=EPI_COLS, sub=EPI_SUB):
    m, k = hn.shape
    n = w.shape[1]
    tm, tn = min(tm, seq), min(tn, n)
    return pl.pallas_call(
        functools.partial(_gdn_conv_kernel, rows_per_seq=seq, sub=min(sub, tn)),
        out_shape=jax.ShapeDtypeStruct((m, n), BF16),
        grid=(n // tn, m // tm),
        in_specs=[pl.BlockSpec((tm, k), lambda j, i: (i, 0)),
                  pl.BlockSpec((k, tn), lambda j, i: (0, j)),
                  pl.BlockSpec((C_CONV_WIDTH, tn), lambda j, i: (0, j))],
        out_specs=pl.BlockSpec((tm, tn), lambda j, i: (i, j)),
        scratch_shapes=[pltpu.VMEM((8, tn), F32)],
        compiler_params=_params("arbitrary", "arbitrary"),
        name="gdn_conv_proj",
    )(hn, w, conv_w)


def _pair_masks():
    n = 2 * CHUNK
    r, c = _iota((n, n), 0), _iota((n, n), 1)
    same = (r // CHUNK) == (c // CHUNK)
    return same & (r >= c), same & (r > c), r == c


def _inverse_stages(p_ref, t_ref, count, nilpotency):
    for _ in range(int(math.log2(nilpotency)) - 1):
        for i in range(count):
            p_ref[i] = _dot(p_ref[i], p_ref[i]).astype(BF16)
        for i in range(count):
            t = t_ref[i]
            t_ref[i] = t + _dot(t.astype(BF16), p_ref[i])


def _gdn_gates_kernel(a_ref, w_ref, alog_ref, dtb_ref, beta_ref, gc_ref, gct_ref):
    acc = _dot(a_ref[...], w_ref[...])
    beta_ref[...] = jax.nn.sigmoid(acc[:, :LANES])
    g = -jnp.exp(alog_ref[...]) * _softplus(acc[:, LANES:] + dtb_ref[...])
    tri, _, diag = _pair_masks()
    tri_b, eye_b = tri.astype(BF16), diag.astype(BF16)
    n2 = 2 * CHUNK
    for n in range(g.shape[0] // n2):
        rows = slice(n * n2, (n + 1) * n2)
        gc = _dot_xr(tri_b, g[rows])
        gc_ref[rows, :] = gc
        gct_ref[:, rows] = _dot_xr(eye_b, gc, NT)


def gdn_gates(hn, w_ba, alog_row, dtb_row, *, tm=ROW_TILE):
    m, k = hn.shape
    tm = min(tm, m)
    row = pl.BlockSpec((1, LANES), lambda i: (0, 0))
    out = pl.BlockSpec((tm, LANES), lambda i: (i, 0))
    return pl.pallas_call(
        _gdn_gates_kernel,
        out_shape=(jax.ShapeDtypeStruct((m, LANES), F32), jax.ShapeDtypeStruct((m, LANES), F32),
                   jax.ShapeDtypeStruct((LANES, m), F32)),
        grid=(m // tm,),
        in_specs=[pl.BlockSpec((tm, k), lambda i: (i, 0)), pl.BlockSpec((k, 2 * LANES), lambda i: (0, 0)), row, row],
        out_specs=(out, out, pl.BlockSpec((LANES, tm), lambda i: (0, i))),
        compiler_params=_params("parallel"),
        name="gdn_gates",
    )(hn, w_ba, alog_row, dtb_row)


def _gdn_prep_kernel(q_ref, k_ref, v_ref, beta_ref, gc_ref, gct_ref,
                     w_ref, qg_ref, u_ref, att_ref, kdt_ref, dec_ref,
                     g_s, dec_s, kn_s, kb_s, rhs_s, p_s, t_s, qn_s, kdec_s):
    kh = pl.program_id(1)
    c = CHUNK
    n2 = 2 * c
    dk = C_HEAD_DIM
    nv = w_ref.shape[0]
    npairs = q_ref.shape[0] // n2
    tri, strict, diag = _pair_masks()
    eye_f = diag.astype(F32)
    pairs = [slice(n * n2, (n + 1) * n2) for n in range(npairs)]
    lane = _iota((n2, LANES), 1)

    for n, rows in enumerate(pairs):
        for e in range(nv):
            h = kh * nv + e
            gc = jnp.broadcast_to(jnp.sum(jnp.where(lane == h, gc_ref[rows, :], 0.0), axis=-1, keepdims=True),
                                  (n2, dk))
            grp = gct_ref[pl.ds(pl.multiple_of((h // 8) * 8, 8), 8), rows]
            own_row = jnp.sum(jnp.where(_iota(grp.shape, 0) == h % 8, grp, 0.0), axis=0, keepdims=True)
            gc_row = jnp.broadcast_to(own_row, (n2, n2))
            g_s[n * nv + e] = gc
            dec_s[n * nv + e] = jnp.where(tri, jnp.exp(jnp.where(tri, gc - gc_row, 0.0)), 0.0)
    slab = SLAB_ROWS
    lane_slab = _iota((slab, LANES), 1)
    for n, rows in enumerate(pairs):
        for r0 in range(0, n2, slab):
            rs = slice(r0, r0 + slab)
            gs = slice(n * n2 + r0, n * n2 + r0 + slab)
            last = c - 1 if r0 < c else n2 - 1
            qf = q_ref[gs, :].astype(F32)
            kf = k_ref[gs, :].astype(F32)
            qn = qf * lax.rsqrt(jnp.sum(qf * qf, axis=-1, keepdims=True) + 1e-6) * (dk ** -0.5)
            kn = kf * lax.rsqrt(jnp.sum(kf * kf, axis=-1, keepdims=True) + 1e-6)
            kn_s[n, rs, :] = kn.astype(BF16)
            qn_s[n, rs, :] = qn.astype(BF16)
            for e in range(nv):
                i = n * nv + e
                gc = g_s[i, rs, :]
                vf = v_ref[gs, e * dk:(e + 1) * dk].astype(F32)
                beta = jnp.sum(jnp.where(lane_slab == kh * nv + e, beta_ref[gs, :], 0.0), axis=-1, keepdims=True)
                egc = jnp.exp(gc)
                kb = kn * beta
                kb_s[i, rs, :] = kb.astype(BF16)
                rhs_s[i, rs, :dk] = (vf * beta).astype(BF16)
                rhs_s[i, rs, dk:] = (kb * egc).astype(BF16)
                kdec_s[i, rs, :] = kn * jnp.exp(g_s[i, last:last + 1, :] - gc)
                qg_ref[e, gs, :] = (qn * egc).astype(BF16)
        for e in range(nv):
            i = n * nv + e
            dec_ref[e, 2 * n:2 * n + 1, :] = jnp.exp(g_s[i, c - 1:c, :])
            dec_ref[e, 2 * n + 1:2 * n + 2, :] = jnp.exp(g_s[i, n2 - 1:n2, :])
    for n, rows in enumerate(pairs):
        raw = _dot(qn_s[n], kn_s[n], NT)
        for e in range(nv):
            att_ref[e, rows, :] = (raw * dec_s[n * nv + e]).astype(BF16)
            kdt_ref[e, :, rows] = kdec_s[n * nv + e].T.astype(BF16)
    for n, rows in enumerate(pairs):
        for e in range(nv):
            i = n * nv + e
            neg_a = jnp.where(strict, -(_dot(kb_s[i], kn_s[n], NT) * dec_s[i]), 0.0)
            p_s[i] = neg_a.astype(BF16)
            t_s[i] = eye_f + neg_a
    _inverse_stages(p_s, t_s, npairs * nv, c)
    for n, rows in enumerate(pairs):
        for e in range(nv):
            sol = _dot(t_s[n * nv + e].astype(BF16), rhs_s[n * nv + e])
            u_ref[e, rows, :] = sol[:, :dk].astype(u_ref.dtype)
            w_ref[e, rows, :] = sol[:, dk:].astype(BF16)


def _gdn_scan_kernel(w_ref, qg_ref, u_ref, att_ref, kdt_ref, dec_ref, z_ref, g_ref, o_ref,
                     s_ref, ms_s, vp_s, *, group):
    c = CHUNK
    dk = C_HEAD_DIM
    zeros = jnp.zeros((c, dk), BF16)

    @pl.when(pl.program_id(2) == 0)
    def _():
        s_ref[...] = jnp.zeros(s_ref.shape, F32)

    for n in range(w_ref.shape[1] // c):
        rows = slice(n * c, (n + 1) * c)
        pair = slice((n // 2) * 2 * c, (n // 2 + 1) * 2 * c)
        for gi in range(group):
            lhs = jnp.concatenate([w_ref[gi, rows, :], qg_ref[gi, rows, :]], axis=0)
            ms_s[gi] = _dot(lhs, s_ref[gi].astype(BF16))
        for gi in range(group):
            v_new = (u_ref[gi, rows, :] - ms_s[gi, :c, :]).astype(BF16)
            vp_s[gi] = jnp.concatenate([v_new, zeros] if n % 2 == 0 else [zeros, v_new], axis=0)
        for gi in range(group):
            s_ref[gi] = s_ref[gi] * dec_ref[gi, n:n + 1, :] + _dot(kdt_ref[gi, :, pair], vp_s[gi])
        for gi in range(group):
            o = ms_s[gi, c:, :] + _dot(att_ref[gi, rows, :], vp_s[gi])
            o = o * lax.rsqrt(jnp.mean(o * o, axis=-1, keepdims=True) + NORM_EPS) * g_ref[...]
            z = z_ref[rows, gi * dk:(gi + 1) * dk].astype(F32)
            o_ref[rows, gi * dk:(gi + 1) * dk] = (o * _silu(z)).astype(o_ref.dtype)


def gated_deltanet_layer(x, hn, w_in, conv_w, a_log, dt_bias, norm_g, w_out, batch, seq):
    m, d = x.shape
    dk = C_HEAD_DIM
    k_heads = d // dk
    v_heads = 2 * k_heads
    conv_ch = 2 * k_heads * dk + v_heads * dk
    main_w = conv_ch + v_heads * dk
    qkv = gdn_conv_proj(hn, w_in[:, :conv_ch].astype(BF16), conv_w, seq)
    z = matmul(hn, w_in[:, conv_ch:main_w].astype(BF16), BF16, name="gdn_gate_proj")
    pad = jnp.zeros((d, LANES - v_heads), F32)
    w_ba = jnp.concatenate([w_in[:, main_w:main_w + v_heads], pad, w_in[:, main_w + v_heads:], pad], axis=1)
    row_pad = jnp.zeros((LANES - v_heads,), F32)
    alog_row = jnp.concatenate([a_log, row_pad]).reshape(1, LANES)
    dtb_row = jnp.concatenate([dt_bias, row_pad]).reshape(1, LANES)
    beta_all, gc_all, gc_t = gdn_gates(hn, w_ba.astype(BF16), alog_row, dtb_row)

    rows = min(PREP_ROWS, seq)
    nr = seq // rows
    ncr = rows // CHUNK
    hv = v_heads
    nv = v_heads // k_heads
    nchain = nv * rows // (2 * CHUNK)
    npair = rows // (2 * CHUNK)
    bh_t = lambda dt, last: jax.ShapeDtypeStruct((batch, hv, seq, last), dt)
    blk4 = lambda last: pl.BlockSpec((None, nv, rows, last), lambda b, h, i: (b, h, i, 0))
    qoff, koff, voff = 0, k_heads, 2 * k_heads // nv
    w_, qg_, u_, att_, kdt_, dec_ = pl.pallas_call(
        _gdn_prep_kernel,
        out_shape=(bh_t(BF16, dk), bh_t(BF16, dk), bh_t(BF16, dk), bh_t(BF16, 2 * CHUNK),
                   jax.ShapeDtypeStruct((batch, hv, dk, seq), BF16),
                   jax.ShapeDtypeStruct((batch, hv, seq // CHUNK, dk), F32)),
        grid=(batch, k_heads, nr),
        in_specs=[pl.BlockSpec((rows, dk), lambda b, h, i: (b * nr + i, qoff + h)),
                  pl.BlockSpec((rows, dk), lambda b, h, i: (b * nr + i, koff + h)),
                  pl.BlockSpec((rows, nv * dk), lambda b, h, i: (b * nr + i, voff + h)),
                  pl.BlockSpec((rows, LANES), lambda b, h, i: (b * nr + i, 0)),
                  pl.BlockSpec((rows, LANES), lambda b, h, i: (b * nr + i, 0)),
                  pl.BlockSpec((LANES, rows), lambda b, h, i: (0, b * nr + i))],
        out_specs=(blk4(dk), blk4(dk), blk4(dk), blk4(2 * CHUNK),
                   pl.BlockSpec((None, nv, dk, rows), lambda b, h, i: (b, h, 0, i)),
                   pl.BlockSpec((None, nv, ncr, dk), lambda b, h, i: (b, h, i, 0))),
        scratch_shapes=[pltpu.VMEM((nchain, dk, dk), F32), pltpu.VMEM((nchain, dk, dk), F32),
                        pltpu.VMEM((npair, dk, dk), BF16), pltpu.VMEM((nchain, dk, dk), BF16),
                        pltpu.VMEM((nchain, dk, 2 * dk), BF16), pltpu.VMEM((nchain, dk, dk), BF16),
                        pltpu.VMEM((nchain, dk, dk), F32), pltpu.VMEM((npair, dk, dk), BF16),
                        pltpu.VMEM((nchain, dk, dk), F32)],
        compiler_params=_params("parallel", "parallel", "parallel"),
        name="gdn_prep",
    )(qkv, qkv, qkv, beta_all, gc_all, gc_t)

    rows = min(SCAN_ROWS, seq)
    nr = seq // rows
    ncr = rows // CHUNK
    group = GDN_SCAN_HEADS
    gblk = lambda last: pl.BlockSpec((None, group, rows, last), lambda b, h, i: (b, h, i, 0))
    o = pl.pallas_call(
        functools.partial(_gdn_scan_kernel, group=group),
        out_shape=jax.ShapeDtypeStruct((m, hv * dk), BF16),
        grid=(batch, hv // group, nr),
        in_specs=[gblk(dk), gblk(dk), gblk(dk), gblk(2 * CHUNK),
                  pl.BlockSpec((None, group, dk, rows), lambda b, h, i: (b, h, 0, i)),
                  pl.BlockSpec((None, group, ncr, dk), lambda b, h, i: (b, h, i, 0)),
                  pl.BlockSpec((rows, group * dk), lambda b, h, i: (b * nr + i, h)),
                  pl.BlockSpec((1, dk), lambda b, h, i: (0, 0))],
        out_specs=pl.BlockSpec((rows, group * dk), lambda b, h, i: (b * nr + i, h)),
        scratch_shapes=[pltpu.VMEM((group, dk, dk), F32), pltpu.VMEM((group, 2 * CHUNK, dk), F32),
                        pltpu.VMEM((group, 2 * CHUNK, dk), BF16)],
        compiler_params=_params("parallel", "parallel", "arbitrary"),
        name="gdn_scan",
    )(w_, qg_, u_, att_, kdt_, dec_, z, norm_g.reshape(1, dk))
    return matmul_residual(o, w_out.astype(BF16), x, tm=MM_ROWS // 2, name="gdn_out_proj")


def _rwkv_mix_kernel(x_ref, g_ref, mu_ref, o_ref, tail_ref, *, rows_per_seq):
    i = pl.program_id(0)
    x = x_ref[...]
    tm = x.shape[0]
    hn = x * lax.rsqrt(jnp.mean(x * x, axis=-1, keepdims=True) + NORM_EPS) * g_ref[...]

    @pl.when((i * tm) % rows_per_seq == 0)
    def _():
        tail_ref[...] = jnp.zeros(tail_ref.shape, F32)

    prev = jnp.where(_iota(hn.shape, 0) == 0, tail_ref[7:8, :], pltpu.roll(hn, 1, 0))
    tail_ref[...] = hn[tm - 8:]
    xx = prev - hn
    for c in range(o_ref.shape[0]):
        o_ref[c] = (hn + xx * mu_ref[c:c + 1, :]).astype(o_ref.dtype)


def rwkv_token_mix(x, norm_g, mu, seq, *, tm=ROW_TILE // 2):
    m, d = x.shape
    tm = min(tm, seq)
    nmix = mu.shape[0]
    return pl.pallas_call(
        functools.partial(_rwkv_mix_kernel, rows_per_seq=seq),
        out_shape=jax.ShapeDtypeStruct((nmix, m, d), BF16),
        grid=(m // tm,),
        in_specs=[pl.BlockSpec((tm, d), lambda i: (i, 0)),
                  pl.BlockSpec((1, d), lambda i: (0, 0)),
                  pl.BlockSpec((nmix, d), lambda i: (0, 0))],
        out_specs=pl.BlockSpec((nmix, tm, d), lambda i: (0, i, 0)),
        scratch_shapes=[pltpu.VMEM((8, d), F32)],
        compiler_params=_params("arbitrary"),
        name="rwkv_token_mix",
    )(x, norm_g.reshape(1, d), mu)


def _grouped_mm_kernel(a_ref, w_ref, o_ref):
    o_ref[...] = _dot(a_ref[...], w_ref[...]).astype(o_ref.dtype)


def grouped_matmul(a, w, out_dtype, *, tm=MM_ROWS, tn=MM_COLS):
    g, k, n = w.shape
    m = a.shape[1]
    tm, tn = min(tm, m), min(tn, n)
    return pl.pallas_call(
        _grouped_mm_kernel,
        out_shape=jax.ShapeDtypeStruct((g, m, n), out_dtype),
        grid=(g, n // tn, m // tm),
        in_specs=[pl.BlockSpec((None, tm, k), lambda c, j, i: (c, i, 0)),
                  pl.BlockSpec((None, k, tn), lambda c, j, i: (c, 0, j))],
        out_specs=pl.BlockSpec((None, tm, tn), lambda c, j, i: (c, i, j)),
        compiler_params=_params("parallel", "parallel", "parallel"),
        name="rwkv_rkvg_proj",
    )(a, w)


def _rwkv_lora_kernel(xw_ref, xa_ref, w1_ref, w2_ref, a1_ref, a2_ref, w0_ref, a0_ref, ld_ref, a_ref):
    hw = jnp.tanh(_dot(xw_ref[...], w1_ref[...])).astype(BF16)
    lw = w0_ref[...] + _dot(hw, w2_ref[...])
    log_w = -_softplus(-lw) - 0.5
    ld_ref[...] = -jnp.exp(log_w)
    ha = _dot(xa_ref[...], a1_ref[...]).astype(BF16)
    a_ref[...] = jax.nn.sigmoid(a0_ref[...] + _dot(ha, a2_ref[...]))


def rwkv_lora(xs, w1, w2, a1, a2, w0, a0, *, tm=ROW_TILE):
    _, m, d = xs.shape
    tm = min(tm, m)
    r = w1.shape[1]
    row = pl.BlockSpec((1, d), lambda i: (0, 0))
    shape = jax.ShapeDtypeStruct((m, d), F32)
    return pl.pallas_call(
        _rwkv_lora_kernel,
        out_shape=(shape, shape),
        grid=(m // tm,),
        in_specs=[pl.BlockSpec((None, tm, d), lambda i: (4, i, 0)),
                  pl.BlockSpec((None, tm, d), lambda i: (5, i, 0)),
                  pl.BlockSpec((d, r), lambda i: (0, 0)), pl.BlockSpec((r, d), lambda i: (0, 0)),
                  pl.BlockSpec((d, r), lambda i: (0, 0)), pl.BlockSpec((r, d), lambda i: (0, 0)),
                  row, row],
        out_specs=(pl.BlockSpec((tm, d), lambda i: (i, 0)), pl.BlockSpec((tm, d), lambda i: (i, 0))),
        compiler_params=_params("parallel"),
        name="rwkv_lora",
    )(xs, xs, w1, w2, a1, a2, w0.reshape(1, d), a0.reshape(1, d))


def _head_sum(x, first_head):
    lo = jnp.sum(jnp.where(first_head, x, 0.0), axis=-1, keepdims=True)
    hi = jnp.sum(jnp.where(first_head, 0.0, x), axis=-1, keepdims=True)
    return jnp.where(first_head, lo, hi)


def _rwkv_prep_kernel(r_ref, k_ref, v_ref, ld_ref, a_ref, kk_ref, ka_ref, rk_ref,
                      wt_ref, rt_ref, u_ref, aro_ref, kbt_ref, vb_ref, dec_ref, bonus_ref,
                      kkt_s, kb_s, akk_s, p_s, t_s, drow_s, *, cps):
    c = CHUNK
    n2 = 2 * c
    n_h = B_HEAD_DIM
    npairs = r_ref.shape[0] // n2
    first_head = _iota((1, LANES), 1) < n_h
    tri, strict, diag = _pair_masks()
    tri_b = tri.astype(BF16)
    eye_b = diag.astype(BF16)
    eye_f = diag.astype(F32)
    upper_half = _iota((n2, LANES), 0) < c
    pairs = [slice(n * n2, (n + 1) * n2) for n in range(npairs)]

    for n, rows in enumerate(pairs):
        r, k, v, a = r_ref[rows, :], k_ref[rows, :], v_ref[rows, :], a_ref[rows, :]
        kk_raw = k * kk_ref[...]
        kk = kk_raw * lax.rsqrt(_head_sum(kk_raw * kk_raw, first_head) + 1e-6)
        k2c = k * (1.0 + (a - 1.0) * ka_ref[...])
        bbc = kk * a
        bonus_ref[rows, :] = (_head_sum(r * k2c * rk_ref[...], first_head) * v).astype(bonus_ref.dtype)
        vb_ref[rows, :] = v.astype(BF16)
        ld = ld_ref[rows, :]
        cs = _dot_x2r(tri_b, ld)
        cs_last = jnp.where(upper_half, cs[c - 1:c, :], cs[n2 - 1:n2, :])
        w_inv = jnp.exp(-cs)
        w_end = jnp.exp(cs_last - cs)
        rt_ref[rows, :] = (r * jnp.exp(cs)).astype(BF16)
        kkt_s[n] = (kk * jnp.exp(cs - ld)).astype(BF16)
        kb_s[n] = jnp.concatenate([k2c * w_inv, bbc * w_inv], axis=0).astype(BF16)
        kw, bw = k2c * w_end, -(bbc * w_end)
        for half in range(2):
            hs = slice(half * c, (half + 1) * c)
            kb_end = jnp.concatenate([kw[hs], bw[hs]], axis=0)
            cols = slice((2 * n + half) * n2, (2 * n + half + 1) * n2)
            kbt_ref[:, cols] = kb_end.T.astype(BF16)
        blk, off = divmod(2 * n, cps)
        drow_s[blk, off:off + 1, :] = jnp.exp(cs[c - 1:c, :])
        drow_s[blk, off + 1:off + 2, :] = jnp.exp(cs[n2 - 1:n2, :])
    for blk in range(dec_ref.shape[0] // LANES):
        drow_s[blk, cps:, :] = jnp.zeros((LANES - cps, LANES), F32)
        dec_ref[blk * LANES:(blk + 1) * LANES, :] = _dot_xr(eye_b, drow_s[blk], NT)
    zero_b = jnp.zeros((n2, LANES), BF16)
    strict2 = jnp.concatenate([strict, strict], axis=1)
    tri2 = jnp.concatenate([tri, tri], axis=1)
    sign2 = jnp.where(_iota((n2, 2 * n2), 1) < n2, 1.0, -1.0)
    for n, rows in enumerate(pairs):
        kkt, rt = kkt_s[n], rt_ref[rows, :]
        lhs = jnp.concatenate([jnp.where(first_head, kkt, zero_b), jnp.where(first_head, rt, zero_b),
                               jnp.where(first_head, zero_b, kkt), jnp.where(first_head, zero_b, rt)], axis=0)
        gram = _dot(lhs, kb_s[n], NT)
        for hh in range(2):
            i = 2 * n + hh
            g_kk = jnp.where(strict2, gram[2 * hh * n2:(2 * hh + 1) * n2], 0.0)
            neg_a = -g_kk[:, n2:]
            p_s[i] = neg_a.astype(BF16)
            t_s[i] = eye_f + neg_a
            akk_s[i] = g_kk[:, :n2].astype(BF16)
            g_r = gram[(2 * hh + 1) * n2:(2 * hh + 2) * n2]
            aro_ref[hh, rows, :] = jnp.where(tri2, g_r * sign2, 0.0).astype(BF16)
    _inverse_stages(p_s, t_s, 2 * npairs, c)
    for n, rows in enumerate(pairs):
        av = _dot(jnp.concatenate([akk_s[2 * n], akk_s[2 * n + 1]], axis=0), vb_ref[rows, :])
        p_s[2 * n] = av[:n2].astype(BF16)
        p_s[2 * n + 1] = av[n2:].astype(BF16)
    for n, rows in enumerate(pairs):
        t0, t1 = t_s[2 * n].astype(BF16), t_s[2 * n + 1].astype(BF16)
        wt = _dot(jnp.concatenate([t0, t1], axis=0), kkt_s[n])
        wt_ref[rows, :] = jnp.where(first_head, wt[:n2], wt[n2:]).astype(BF16)
        u_ref[rows, :] = jnp.where(first_head, _dot(t0, p_s[2 * n]), _dot(t1, p_s[2 * n + 1])).astype(u_ref.dtype)


def _rwkv_scan_kernel(wt_ref, rt_ref, u_ref, aro_ref, kbt_ref, vb_ref, dec_ref, bonus_ref, gate_ref,
                      lnw_ref, lnb_ref, o_ref, s_ref, ms_s, xc_s, xp_s, *, group):
    c = CHUNK
    n2 = 2 * c
    n_h = B_HEAD_DIM
    first_head = _iota((1, LANES), 1) < n_h
    same_head = _iota((LANES, LANES), 0) // n_h == _iota((LANES, LANES), 1) // n_h
    zeros = jnp.zeros((c, LANES), BF16)

    def head_mean(x):
        return _head_sum(x, first_head) * (1.0 / n_h)

    @pl.when(pl.program_id(2) == 0)
    def _():
        s_ref[...] = jnp.zeros(s_ref.shape, F32)

    for n in range(u_ref.shape[1] // c):
        rows = slice(n * c, (n + 1) * c)
        for gi in range(group):
            lhs = jnp.concatenate([wt_ref[gi, rows, :], rt_ref[gi, rows, :]], axis=0)
            ms_s[gi] = _dot(lhs, s_ref[gi].astype(BF16))
        for gi in range(group):
            sa = (u_ref[gi, rows, :] + ms_s[gi, :c, :]).astype(BF16)
            vb = vb_ref[gi, rows, :]
            xc_s[gi] = jnp.concatenate([vb, sa], axis=0)
            xp_s[gi] = jnp.concatenate([vb, zeros, sa, zeros] if n % 2 == 0 else [zeros, vb, zeros, sa], axis=0)
        for gi in range(group):
            upd = _dot(kbt_ref[gi, :, n * n2:(n + 1) * n2], xc_s[gi])
            dec_col = jnp.broadcast_to(dec_ref[gi, :, n:n + 1], (LANES, LANES))
            s_ref[gi] = s_ref[gi] * dec_col + jnp.where(same_head, upd, 0.0)
        for gi in range(group):
            both = _dot(jnp.concatenate([aro_ref[gi, 0, rows, :], aro_ref[gi, 1, rows, :]], axis=0), xp_s[gi])
            o = ms_s[gi, c:, :] + jnp.where(first_head, both[:c], both[c:])
            mean = head_mean(o)
            dlt = o - mean
            var = head_mean(dlt * dlt)
            cols = slice(gi * LANES, (gi + 1) * LANES)
            y = dlt * lax.rsqrt(var + B_GN_EPS) * lnw_ref[:, cols] + lnb_ref[:, cols]
            y = y + bonus_ref[rows, cols]
            o_ref[rows, cols] = (y * _silu(gate_ref[rows, cols])).astype(o_ref.dtype)


def rwkv7_layer(x, norm_g, mu, w_rkvg, w0, w_w1, w_w2, a0, w_a1, w_a2, k_k, k_a, r_k, ln_w, ln_b, w_out,
                batch, seq):
    m, d = x.shape
    heads = d // B_HEAD_DIM
    pairs = heads // 2
    order = jnp.array([0, 2, 3, 5, 1, 4])
    xs = rwkv_token_mix(x, norm_g, mu[order], seq)
    rkvg = grouped_matmul(xs, w_rkvg.astype(BF16), F32)
    lora = w_w1.shape[1]
    padc = lambda w: jnp.pad(w, ((0, 0), (0, LORA_PAD - lora))).astype(BF16)
    padr = lambda w: jnp.pad(w, ((0, LORA_PAD - lora), (0, 0))).astype(BF16)
    ld, a = rwkv_lora(xs, padc(w_w1), padr(w_w2), padc(w_a1), padr(w_a2), w0, a0)

    srows = min(SCAN_ROWS, seq)
    rows = min(PREP_ROWS, seq)
    nr = seq // rows
    npair = rows // (2 * CHUNK)
    col = lambda g: pl.BlockSpec((None, rows, LANES), lambda b, p, i, g=g: (g, b * nr + i, p))
    flat = pl.BlockSpec((rows, LANES), lambda b, p, i: (b * nr + i, p))
    prow = pl.BlockSpec((1, LANES), lambda b, p, i: (0, p))
    bp = lambda rws, last, dt: jax.ShapeDtypeStruct((batch, pairs, rws, last), dt)
    pblk = lambda rws, last: pl.BlockSpec((None, None, rws, last), lambda b, p, i: (b, p, i, 0))
    wt_, rt_, u_, aro_, kbt_, vb_, dec_, bonus = pl.pallas_call(
        functools.partial(_rwkv_prep_kernel, cps=srows // CHUNK),
        out_shape=(bp(seq, LANES, BF16), bp(seq, LANES, BF16), bp(seq, LANES, BF16),
                   jax.ShapeDtypeStruct((batch, pairs, 2, seq, 4 * CHUNK), BF16),
                   bp(LANES, 2 * seq, BF16), bp(seq, LANES, BF16), bp(seq // srows * LANES, LANES, F32),
                   jax.ShapeDtypeStruct((m, d), BF16)),
        grid=(batch, pairs, nr),
        in_specs=[col(0), col(1), col(2), flat, flat, prow, prow, prow],
        out_specs=(pblk(rows, LANES), pblk(rows, LANES), pblk(rows, LANES),
                   pl.BlockSpec((None, None, 2, rows, 4 * CHUNK), lambda b, p, i: (b, p, 0, i, 0)),
                   pl.BlockSpec((None, None, LANES, 2 * rows), lambda b, p, i: (b, p, 0, i)),
                   pblk(rows, LANES), pblk(rows // srows * LANES, LANES), flat),
        scratch_shapes=[pltpu.VMEM((npair, LANES, LANES), BF16), pltpu.VMEM((npair, 2 * LANES, LANES), BF16),
                        pltpu.VMEM((2 * npair, LANES, LANES), BF16), pltpu.VMEM((2 * npair, LANES, LANES), BF16),
                        pltpu.VMEM((2 * npair, LANES, LANES), F32),
                        pltpu.VMEM((rows // srows, LANES, LANES), F32)],
        compiler_params=_params("parallel", "parallel", "parallel"),
        name="rwkv_prep",
    )(rkvg, rkvg, rkvg, ld, a, k_k.reshape(1, d), k_a.reshape(1, d), r_k.reshape(1, d))

    rows = srows
    nr = seq // rows
    group = RWKV_SCAN_PAIRS
    gw = group * LANES
    gblk = lambda rws, last: pl.BlockSpec((None, group, rws, last), lambda b, p, i: (b, p, i, 0))
    gflat = pl.BlockSpec((rows, gw), lambda b, p, i: (b * nr + i, p))
    grow = pl.BlockSpec((1, gw), lambda b, p, i: (0, p))
    o = pl.pallas_call(
        functools.partial(_rwkv_scan_kernel, group=group),
        out_shape=jax.ShapeDtypeStruct((m, d), BF16),
        grid=(batch, pairs // group, nr),
        in_specs=[gblk(rows, LANES), gblk(rows, LANES), gblk(rows, LANES),
                  pl.BlockSpec((None, group, 2, rows, 4 * CHUNK), lambda b, p, i: (b, p, 0, i, 0)),
                  pl.BlockSpec((None, group, LANES, 2 * rows), lambda b, p, i: (b, p, 0, i)),
                  gblk(rows, LANES), gblk(LANES, LANES), gflat,
                  pl.BlockSpec((None, rows, gw), lambda b, p, i: (3, b * nr + i, p)),
                  grow, grow],
        out_specs=gflat,
        scratch_shapes=[pltpu.VMEM((group, LANES, LANES), F32), pltpu.VMEM((group, 2 * CHUNK, LANES), F32),
                        pltpu.VMEM((group, 2 * CHUNK, LANES), BF16), pltpu.VMEM((group, 4 * CHUNK, LANES), BF16)],
        compiler_params=_params("parallel", "parallel", "arbitrary"),
        name="rwkv_scan",
    )(wt_, rt_, u_, aro_, kbt_, vb_, dec_, bonus, rkvg, ln_w.reshape(1, d), ln_b.reshape(1, d))
    return matmul_residual(o, w_out.astype(BF16), x, name="rwkv_out_proj")


def kernel(x, p, positions, norm_g, pe_norm_g, pe_w_gate, pe_w_proj, final_norm_g, a_w_in, a_lam, a_subln_g, a_w_out, b_mu, b_w_rkvg, b_w0, b_w_w1, b_w_w2, b_a0, b_w_a1, b_w_a2, b_k_k, b_k_a, b_r_k, b_ln_w, b_ln_b, b_w_out, c_w_in, c_conv_w, c_A_log, c_dt_bias, c_norm_g, c_w_out):
    batch, seq, d = x.shape
    depth = p.shape[0]
    m = batch * seq
    xf = x.reshape(m, d)
    tables = rope_tables(positions)
    for i in range(depth):
        kind = i % N_MIXERS
        j = i // N_MIXERS
        if kind == 0:
            lam_init = 0.8 - 0.6 * math.exp(-0.3 * i)
            hn = rmsnorm(xf, norm_g[i], BF16)
            xf, xb, ssq = diff_attention_layer(xf, hn, tables, a_w_in[j], a_lam[j], a_subln_g[j], a_w_out[j],
                                      batch, seq, lam_init)
        elif kind == 1:
            xf, xb, ssq = rwkv7_layer(xf, norm_g[i], b_mu[j], b_w_rkvg[j], b_w0[j], b_w_w1[j], b_w_w2[j], b_a0[j],
                             b_w_a1[j], b_w_a2[j], b_k_k[j], b_k_a[j], b_r_k[j], b_ln_w[j], b_ln_b[j],
                             b_w_out[j], batch, seq)
        else:
            hn = rmsnorm(xf, norm_g[i], BF16)
            xf, xb, ssq = gated_deltanet_layer(xf, hn, c_w_in[j], c_conv_w[j], c_A_log[j], c_dt_bias[j], c_norm_g[j],
                                      c_w_out[j], batch, seq)
        xf = per_layer_embedding(xf, xb, ssq, pe_norm_g[i], pe_w_gate[i], p[i].reshape(m, -1), pe_w_proj[i])
    return rmsnorm(xf, final_norm_g, F32).reshape(batch, seq, d)
```

```python
import functools
import math

import jax
import jax.numpy as jnp
from jax import lax
from jax.experimental import pallas as pl
from jax.experimental.pallas import tpu as pltpu

F32 = jnp.float32
BF16 = jnp.bfloat16

N_MIXERS = 3
NORM_EPS = 1e-6
LANES = 128
VMEM_LIMIT = 48 * 1024 * 1024

A_HEAD_DIM = 128
A_V_DIM = 2 * A_HEAD_DIM
ROT_DIM = A_HEAD_DIM // 4
ROPE_THETA = 500000.0
SUBLN_EPS = 1e-5
ATTN_BLOCK = 512
ATTN_Q_BLOCKS = 1
ONES_ROWS = 16

B_HEAD_DIM = 64
B_GN_EPS = 64e-5
LORA_PAD = 128

C_HEAD_DIM = 128
C_CONV_WIDTH = 4
CHUNK = 64
SCAN_ROWS = 512
PREP_ROWS = 1024
SLAB_ROWS = 32
GDN_SCAN_HEADS = 16
RWKV_SCAN_PAIRS = 8

MM_ROWS = 1024
MM_COLS = 1024
ROW_TILE = 512
EPI_COLS = 2048
EPI_SUB = 256

NN = (((1,), (0,)), ((), ()))
NT = (((1,), (1,)), ((), ()))


def _dot(a, b, dims=NN):
    return lax.dot_general(a, b, dims, preferred_element_type=F32)


def _split2(x):
    hi = x.astype(BF16)
    lo = (x - hi.astype(F32)).astype(BF16)
    return hi, lo


def _split3(x):
    hi = x.astype(BF16)
    r = x - hi.astype(F32)
    mid = r.astype(BF16)
    lo = (r - mid.astype(F32)).astype(BF16)
    return hi, mid, lo


def _dot_xr(a_exact, b, dims=NN):
    h, m, l = _split3(b)
    return _dot(a_exact, h, dims) + (_dot(a_exact, m, dims) + _dot(a_exact, l, dims))


def _dot_x2r(a_exact, b, dims=NN):
    h, l = _split2(b)
    return _dot(a_exact, h, dims) + _dot(a_exact, l, dims)


def _iota(shape, dim):
    return lax.broadcasted_iota(jnp.int32, shape, dim)


def _silu(x):
    return x * jax.nn.sigmoid(x)


def _softplus(x):
    return jnp.maximum(x, 0.0) + jnp.log(1.0 + jnp.exp(-jnp.abs(x)))


def _params(*sem):
    return pltpu.CompilerParams(dimension_semantics=sem, vmem_limit_bytes=VMEM_LIMIT)


def _rmsnorm_kernel(x_ref, g_ref, o_ref, *, eps):
    x = x_ref[...]
    y = x * lax.rsqrt(jnp.mean(x * x, axis=-1, keepdims=True) + eps)
    o_ref[...] = (y * g_ref[...]).astype(o_ref.dtype)


def rmsnorm(x, g, out_dtype, *, eps=NORM_EPS, tm=ROW_TILE):
    m, d = x.shape
    tm = min(tm, m)
    return pl.pallas_call(
        functools.partial(_rmsnorm_kernel, eps=eps),
        out_shape=jax.ShapeDtypeStruct((m, d), out_dtype),
        grid=(m // tm,),
        in_specs=[pl.BlockSpec((tm, d), lambda i: (i, 0)), pl.BlockSpec((1, d), lambda i: (0, 0))],
        out_specs=pl.BlockSpec((tm, d), lambda i: (i, 0)),
        compiler_params=_params("parallel"),
        name="rmsnorm",
    )(x, g.reshape(1, d))


def _mm_kernel(a_ref, w_ref, *rest, epilogue):
    o_ref = rest[-1]
    acc = _dot(a_ref[...], w_ref[...])
    if epilogue is not None:
        acc = epilogue(acc, *rest[:-1])
    o_ref[...] = acc.astype(o_ref.dtype)


def matmul(a, w, out_dtype, *, tm=MM_ROWS, tn=MM_COLS, extra=(), extra_specs=(), epilogue=None, name="matmul"):
    m, k = a.shape
    n = w.shape[1]
    tm, tn = min(tm, m), min(tn, n)
    return pl.pallas_call(
        functools.partial(_mm_kernel, epilogue=epilogue),
        out_shape=jax.ShapeDtypeStruct((m, n), out_dtype),
        grid=(n // tn, m // tm),
        in_specs=[pl.BlockSpec((tm, k), lambda j, i: (i, 0)),
                  pl.BlockSpec((k, tn), lambda j, i: (0, j))] + list(extra_specs),
        out_specs=pl.BlockSpec((tm, tn), lambda j, i: (i, j)),
        compiler_params=_params("parallel", "parallel"),
        name=name,
    )(a, w, *extra)


def _out_proj_kernel(a_ref, w_ref, res_ref, x_ref, xb_ref, ssq_ref):
    x = res_ref[...] + _dot(a_ref[...], w_ref[...])
    x_ref[...] = x
    xb_ref[...] = x.astype(BF16)
    ssq_ref[...] = jnp.sum(x * x, axis=-1, keepdims=True)


def matmul_residual(a, w, res, *, tm=MM_ROWS, tn=MM_COLS, name="matmul_residual"):
    m, k = a.shape
    n = w.shape[1]
    tm, tn = min(tm, m), min(tn, n)
    blk = pl.BlockSpec((tm, tn), lambda j, i: (i, j))
    return pl.pallas_call(
        _out_proj_kernel,
        out_shape=(jax.ShapeDtypeStruct((m, n), F32), jax.ShapeDtypeStruct((m, n), BF16),
                   jax.ShapeDtypeStruct((n // tn, m, 1), F32)),
        grid=(n // tn, m // tm),
        in_specs=[pl.BlockSpec((tm, k), lambda j, i: (i, 0)), pl.BlockSpec((k, tn), lambda j, i: (0, j)), blk],
        out_specs=(blk, blk, pl.BlockSpec((None, tm, 1), lambda j, i: (j, i, 0))),
        compiler_params=_params("parallel", "parallel"),
        name=name,
    )(a, w, res)


def _ple_kernel(xb_ref, ssq_ref, wg_ref, p_ref, wp_ref, x_ref, o_ref, *, eps):
    d = xb_ref.shape[1]
    inv_rms = lax.rsqrt(jnp.sum(ssq_ref[...], axis=0) / d + eps)
    gate = jax.nn.sigmoid(inv_rms * _dot(xb_ref[...], wg_ref[...]))
    proj = _dot(p_ref[...].astype(BF16), wp_ref[...])
    o_ref[...] = x_ref[...] + gate * proj


def per_layer_embedding(x, xb, ssq, norm_g, w_gate, p, w_proj, *, tm=MM_ROWS, tn=MM_COLS):
    m, d = x.shape
    pd = p.shape[1]
    nparts = ssq.shape[0]
    tm, tn = min(tm, m), min(tn, d)
    wg = (norm_g[:, None] * w_gate).astype(BF16)
    return pl.pallas_call(
        functools.partial(_ple_kernel, eps=NORM_EPS),
        out_shape=jax.ShapeDtypeStruct((m, d), F32),
        grid=(d // tn, m // tm),
        in_specs=[pl.BlockSpec((tm, d), lambda j, i: (i, 0)),
                  pl.BlockSpec((nparts, tm, 1), lambda j, i: (0, i, 0)),
                  pl.BlockSpec((d, tn), lambda j, i: (0, j)),
                  pl.BlockSpec((tm, pd), lambda j, i: (i, 0)),
                  pl.BlockSpec((pd, tn), lambda j, i: (0, j)),
                  pl.BlockSpec((tm, tn), lambda j, i: (i, j))],
        out_specs=pl.BlockSpec((tm, tn), lambda j, i: (i, j)),
        compiler_params=_params("parallel", "parallel"),
        name="per_layer_embedding",
    )(xb, ssq, wg, p, w_proj.astype(BF16), x)


def _rope_table_kernel(pos_ref, freq_ref, cos_ref, sin_lo_ref, sin_hi_ref):
    half = ROT_DIM // 2
    ang = pos_ref[...].astype(F32) * freq_ref[...]
    lane = _iota(ang.shape, 1)
    c, s = jnp.cos(ang), jnp.sin(ang)
    cos_ref[...] = jnp.where(lane < ROT_DIM, c, 1.0)
    sin_lo_ref[...] = jnp.where(lane < half, -s, 0.0)
    sin_hi_ref[...] = jnp.where((lane >= half) & (lane < ROT_DIM), s, 0.0)


def rope_tables(positions, *, tm=MM_ROWS):
    m = positions.size
    tm = min(tm, m)
    inv_freq = ROPE_THETA ** (-jnp.arange(0, ROT_DIM, 2, dtype=F32) / ROT_DIM)
    freq_row = jnp.concatenate([inv_freq, inv_freq, jnp.zeros((LANES - ROT_DIM,), F32)]).reshape(1, LANES)
    shape = jax.ShapeDtypeStruct((m, LANES), F32)
    spec = pl.BlockSpec((tm, LANES), lambda i: (i, 0))
    return pl.pallas_call(
        _rope_table_kernel,
        out_shape=(shape, shape, shape),
        grid=(m // tm,),
        in_specs=[pl.BlockSpec((tm, 1), lambda i: (i, 0)), pl.BlockSpec((1, LANES), lambda i: (0, 0))],
        out_specs=(spec, spec, spec),
        compiler_params=_params("parallel"),
        name="rope_tables",
    )(positions.reshape(m, 1), freq_row)


def _attn_qk_kernel(a_ref, w_ref, cos_ref, sin_lo_ref, sin_hi_ref, o_ref, *, n_q_blocks, sub):
    j = pl.program_id(0)
    half = ROT_DIM // 2
    scale = jnp.where(j < n_q_blocks, A_HEAD_DIM ** -0.5, 1.0).astype(F32)
    cos, sin_lo, sin_hi = cos_ref[...], sin_lo_ref[...], sin_hi_ref[...]
    a = a_ref[...]

    def epilogue(acc, c0):
        for g in range(acc.shape[1] // LANES):
            x = acc[:, g * LANES:(g + 1) * LANES]
            y = x * cos + pltpu.roll(x, LANES - half, 1) * sin_lo + pltpu.roll(x, half, 1) * sin_hi
            o_ref[:, c0 + g * LANES:c0 + (g + 1) * LANES] = (y * scale).astype(o_ref.dtype)

    starts = list(range(0, w_ref.shape[1], sub))
    acc = _dot(a, w_ref[:, starts[0]:starts[0] + sub])
    for c, c0 in enumerate(starts):
        nxt = _dot(a, w_ref[:, starts[c + 1]:starts[c + 1] + sub]) if c + 1 < len(starts) else None
        epilogue(acc, c0)
        acc = nxt


def attn_qk_proj(hn, w_qk, tables, *, tm=MM_ROWS, tn=EPI_COLS, sub=EPI_SUB):
    m, k = hn.shape
    n = w_qk.shape[1]
    tm, tn = min(tm, m), min(tn, n // 2)
    tspec = pl.BlockSpec((tm, LANES), lambda j, i: (i, 0))
    return pl.pallas_call(
        functools.partial(_attn_qk_kernel, n_q_blocks=n // 2 // tn, sub=min(sub, tn)),
        out_shape=jax.ShapeDtypeStruct((m, n), BF16),
        grid=(n // tn, m // tm),
        in_specs=[pl.BlockSpec((tm, k), lambda j, i: (i, 0)),
                  pl.BlockSpec((k, tn), lambda j, i: (0, j)), tspec, tspec, tspec],
        out_specs=pl.BlockSpec((tm, tn), lambda j, i: (i, j)),
        compiler_params=_params("parallel", "parallel"),
        name="attn_qk_proj",
    )(hn, w_qk, *tables)


def _attn_vt_kernel(w_ref, a_ref, o_ref):
    o_ref[...] = _dot(w_ref[...], a_ref[...], NT).astype(o_ref.dtype)


def attn_v_proj_t(hn, w_t, batch, seq, blk, *, tn=MM_COLS):
    m, k = hn.shape
    n = w_t.shape[0]
    tn = min(tn, n)
    nk = seq // blk
    return pl.pallas_call(
        _attn_vt_kernel,
        out_shape=jax.ShapeDtypeStruct((batch, nk, n, blk), BF16),
        grid=(n // tn, m // blk),
        in_specs=[pl.BlockSpec((tn, k), lambda j, i: (j, 0)),
                  pl.BlockSpec((blk, k), lambda j, i: (i, 0))],
        out_specs=pl.BlockSpec((None, None, tn, blk), lambda j, i: (i // nk, i % nk, j, 0)),
        compiler_params=_params("parallel", "parallel"),
        name="attn_v_proj_t",
    )(w_t, hn)


def _diff_attn_kernel(lam_ref, q_ref, k_ref, vt_ref, z_ref, g_ref, o_ref, m_ref, acc_ref, s_ref,
                      *, bq, bk, lam_init):
    i = pl.program_id(2)
    lam = lam_ref[...]
    lam_full = (jnp.exp(jnp.sum(lam[0:1] * lam[1:2], axis=-1, keepdims=True))
                - jnp.exp(jnp.sum(lam[2:3] * lam[3:4], axis=-1, keepdims=True)) + lam_init)
    m_ref[...] = jnp.full(m_ref.shape, -jnp.inf, F32)
    acc_ref[...] = jnp.zeros(acc_ref.shape, F32)
    q = q_ref[...]
    d = A_HEAD_DIM
    w = A_V_DIM
    ones = jnp.ones((ONES_ROWS, bk), BF16)

    def scores(j, slot, q_lo=0):
        start = pl.multiple_of(j * bk, bk)
        kb = k_ref[pl.ds(start, bk), :]
        for c in range(2):
            s_ref[slot, c, :, q_lo:] = _dot(kb[:, c * d:(c + 1) * d], q[q_lo:, c * d:(c + 1) * d], NT)

    def absorb(j, slot, masked, q_lo=0):
        vt = jnp.concatenate([vt_ref[j], ones], axis=0)
        for c in range(2):
            s = s_ref[slot, c, :, q_lo:]
            if masked:
                kv_pos = j * bk + _iota(s.shape, 0)
                q_pos = i * bq + q_lo + _iota(s.shape, 1)
                s = jnp.where(kv_pos <= q_pos, s, -jnp.inf)
            m_prev = m_ref[c, :, q_lo:]
            m_new = jnp.maximum(m_prev, jnp.max(s, axis=0, keepdims=True))
            alpha = jnp.exp(m_prev - m_new)
            p = jnp.exp(s - m_new)
            acc_ref[c, :, q_lo:] = alpha * acc_ref[c, :, q_lo:] + _dot(vt, p.astype(BF16))
            m_ref[c, :, q_lo:] = m_new

    r = bq // bk
    first_masked = r * i
    scores(0, 0)

    def body(jj, carry):
        scores(2 * jj + 1, 1)
        absorb(2 * jj, 0, False)
        scores(2 * jj + 2, 0)
        absorb(2 * jj + 1, 1, False)
        return carry

    lax.fori_loop(0, first_masked // 2, body, 0)

    def tail(odd):
        base = first_masked - odd
        left = [(base + n, n % 2, n >= odd, max(n - odd, 0) * bk) for n in range(odd + r)]
        for n, (j, slot, masked, q_lo) in enumerate(left):
            if n + 1 < len(left):
                nj, nslot, _, nq_lo = left[n + 1]
                scores(nj, nslot, nq_lo)
            absorb(j, slot, masked, q_lo)

    if r % 2 == 0:
        tail(0)
    else:
        pl.when(first_masked % 2 == 0)(lambda: tail(0))
        pl.when(first_masked % 2 == 1)(lambda: tail(1))

    o = (acc_ref[0, :w, :] / acc_ref[0, w:w + 1, :]
         - lam_full * (acc_ref[1, :w, :] / acc_ref[1, w:w + 1, :]))
    o = o * lax.rsqrt(jnp.mean(o * o, axis=0, keepdims=True) + SUBLN_EPS) * g_ref[...]
    o = (o * (1.0 - lam_init)).T
    o_ref[...] = (o * _silu(z_ref[...].astype(F32))).astype(o_ref.dtype)


def diff_attention_core(qk, z, vt, lam, subln_g, batch, seq, heads, lam_init):
    m = qk.shape[0]
    bk = vt.shape[-1]
    bq = ATTN_Q_BLOCKS * bk
    nq = seq // bq
    nk = seq // bk
    w = A_V_DIM
    return pl.pallas_call(
        functools.partial(_diff_attn_kernel, bq=bq, bk=bk, lam_init=lam_init),
        out_shape=jax.ShapeDtypeStruct((m, heads * w), BF16),
        grid=(batch, heads, nq),
        in_specs=[pl.BlockSpec((4, A_HEAD_DIM), lambda b, h, i: (0, 0)),
                  pl.BlockSpec((bq, w), lambda b, h, i: (b * nq + i, h)),
                  pl.BlockSpec((seq, w), lambda b, h, i: (b, heads + h)),
                  pl.BlockSpec((None, nk, w, bk), lambda b, h, i: (b, 0, h, 0)),
                  pl.BlockSpec((bq, w), lambda b, h, i: (b * nq + i, h)),
                  pl.BlockSpec((w, 1), lambda b, h, i: (0, 0))],
        out_specs=pl.BlockSpec((bq, w), lambda b, h, i: (b * nq + i, h)),
        scratch_shapes=[pltpu.VMEM((2, 1, bq), F32), pltpu.VMEM((2, w + ONES_ROWS, bq), F32),
                        pltpu.VMEM((2, 2, bk, bq), F32)],
        compiler_params=_params("parallel", "parallel", "parallel"),
        name="diff_attention",
    )(lam, qk, qk, vt, z, subln_g.reshape(w, 1))


def diff_attention_layer(x, hn, tables, w_in, lam, subln_g, w_out, batch, seq, lam_init):
    d = x.shape[1]
    heads = d // A_V_DIM
    qk_w = heads * 2 * A_HEAD_DIM
    v_w = heads * A_V_DIM
    blk = min(ATTN_BLOCK, seq // ATTN_Q_BLOCKS)
    w_vt = w_in[:, 2 * qk_w:2 * qk_w + v_w].T.astype(BF16)
    qk = attn_qk_proj(hn, w_in[:, :2 * qk_w].astype(BF16), tables)
    z = matmul(hn, w_in[:, 2 * qk_w + v_w:].astype(BF16), BF16, name="attn_gate_proj")
    vt = attn_v_proj_t(hn, w_vt, batch, seq, blk)
    o = diff_attention_core(qk, z, vt, lam, subln_g, batch, seq, heads, lam_init)
    return matmul_residual(o, w_out.astype(BF16), x, name="attn_out_proj")


def _gdn_conv_kernel(a_ref, w_ref, cw_ref, o_ref, tail_ref, *, rows_per_seq, sub):
    i = pl.program_id(1)
    tm = a_ref.shape[0]

    @pl.when((i * tm) % rows_per_seq == 0)
    def _():
        tail_ref[...] = jnp.zeros(tail_ref.shape, F32)

    a = a_ref[...]
    sub_iota = _iota((8, sub), 0)
    last = C_CONV_WIDTH - 1

    def epilogue(acc, cols):
        tail = tail_ref[:, cols]

        def shifted(s):
            xs = pltpu.roll(acc, s, 0)
            head = jnp.where(sub_iota < s, pltpu.roll(tail, s, 0), xs[:8])
            return jnp.concatenate([head, xs[8:]], axis=0)

        cw = cw_ref[:, cols]
        y = shifted(last) * cw[0:1]
        for t in range(1, last):
            y = y + shifted(last - t) * cw[t:t + 1]
        y = y + acc * cw[last:last + 1]
        tail_ref[:, cols] = acc[tm - 8:]
        o_ref[:, cols] = _silu(y).astype(o_ref.dtype)

    blocks = [slice(c0, c0 + sub) for c0 in range(0, w_ref.shape[1], sub)]
    acc = _dot(a, w_ref[:, blocks[0]])
    for c, cols in enumerate(blocks):
        nxt = _dot(a, w_ref[:, blocks[c + 1]]) if c + 1 < len(blocks) else None
        epilogue(acc, cols)
        acc = nxt


def gdn_conv_proj(hn, w, conv_w, seq, *, tm=MM_ROWS, tn=EPI_COLS, sub=EPI_SUB):
    m, k = hn.shape
    n = w.shape[1]
    tm, tn = min(tm, seq), min(tn, n)
    return pl.pallas_call(
        functools.partial(_gdn_conv_kernel, rows_per_seq=seq, sub=min(sub, tn)),
        out_shape=jax.ShapeDtypeStruct((m, n), BF16),
        grid=(n // tn, m // tm),
        in_specs=[pl.BlockSpec((tm, k), lambda j, i: (i, 0)),
                  pl.BlockSpec((k, tn), lambda j, i: (0, j)),
                  pl.BlockSpec((C_CONV_WIDTH, tn), lambda j, i: (0, j))],
        out_specs=pl.BlockSpec((tm, tn), lambda j, i: (i, j)),
        scratch_shapes=[pltpu.VMEM((8, tn), F32)],
        compiler_params=_params("arbitrary", "arbitrary"),
        name="gdn_conv_proj",
    )(hn, w, conv_w)


def _pair_masks():
    n = 2 * CHUNK
    r, c = _iota((n, n), 0), _iota((n, n), 1)
    same = (r // CHUNK) == (c // CHUNK)
    return same & (r >= c), same & (r > c), r == c


def _inverse_stages(p_ref, t_ref, count, nilpotency):
    for _ in range(int(math.log2(nilpotency)) - 1):
        for i in range(count):
            p_ref[i] = _dot(p_ref[i], p_ref[i]).astype(BF16)
        for i in range(count):
            t = t_ref[i]
            t_ref[i] = t + _dot(t.astype(BF16), p_ref[i])


def _gdn_gates_kernel(a_ref, w_ref, alog_ref, dtb_ref, beta_ref, gc_ref, gct_ref):
    acc = _dot(a_ref[...], w_ref[...])
    beta_ref[...] = jax.nn.sigmoid(acc[:, :LANES])
    g = -jnp.exp(alog_ref[...]) * _softplus(acc[:, LANES:] + dtb_ref[...])
    tri, _, diag = _pair_masks()
    tri_b, eye_b = tri.astype(BF16), diag.astype(BF16)
    n2 = 2 * CHUNK
    for n in range(g.shape[0] // n2):
        rows = slice(n * n2, (n + 1) * n2)
        gc = _dot_xr(tri_b, g[rows])
        gc_ref[rows, :] = gc
        gct_ref[:, rows] = _dot_xr(eye_b, gc, NT)


def gdn_gates(hn, w_ba, alog_row, dtb_row, *, tm=ROW_TILE):
    m, k = hn.shape
    tm = min(tm, m)
    row = pl.BlockSpec((1, LANES), lambda i: (0, 0))
    out = pl.BlockSpec((tm, LANES), lambda i: (i, 0))
    return pl.pallas_call(
        _gdn_gates_kernel,
        out_shape=(jax.ShapeDtypeStruct((m, LANES), F32), jax.ShapeDtypeStruct((m, LANES), F32),
                   jax.ShapeDtypeStruct((LANES, m), F32)),
        grid=(m // tm,),
        in_specs=[pl.BlockSpec((tm, k), lambda i: (i, 0)), pl.BlockSpec((k, 2 * LANES), lambda i: (0, 0)), row, row],
        out_specs=(out, out, pl.BlockSpec((LANES, tm), lambda i: (0, i))),
        compiler_params=_params("parallel"),
        name="gdn_gates",
    )(hn, w_ba, alog_row, dtb_row)


def _gdn_prep_kernel(q_ref, k_ref, v_ref, beta_ref, gc_ref, gct_ref,
                     w_ref, qg_ref, u_ref, att_ref, kdt_ref, dec_ref,
                     g_s, dec_s, kn_s, kb_s, rhs_s, p_s, t_s, qn_s, kdec_s):
    kh = pl.program_id(1)
    c = CHUNK
    n2 = 2 * c
    dk = C_HEAD_DIM
    nv = w_ref.shape[0]
    npairs = q_ref.shape[0] // n2
    tri, strict, diag = _pair_masks()
    eye_f = diag.astype(F32)
    pairs = [slice(n * n2, (n + 1) * n2) for n in range(npairs)]
    lane = _iota((n2, LANES), 1)

    for n, rows in enumerate(pairs):
        for e in range(nv):
            h = kh * nv + e
            gc = jnp.broadcast_to(jnp.sum(jnp.where(lane == h, gc_ref[rows, :], 0.0), axis=-1, keepdims=True),
                                  (n2, dk))
            grp = gct_ref[pl.ds(pl.multiple_of((h // 8) * 8, 8), 8), rows]
            own_row = jnp.sum(jnp.where(_iota(grp.shape, 0) == h % 8, grp, 0.0), axis=0, keepdims=True)
            gc_row = jnp.broadcast_to(own_row, (n2, n2))
            g_s[n * nv + e] = gc
            dec_s[n * nv + e] = jnp.where(tri, jnp.exp(jnp.where(tri, gc - gc_row, 0.0)), 0.0)
    slab = SLAB_ROWS
    lane_slab = _iota((slab, LANES), 1)
    for n, rows in enumerate(pairs):
        for r0 in range(0, n2, slab):
            rs = slice(r0, r0 + slab)
            gs = slice(n * n2 + r0, n * n2 + r0 + slab)
            last = c - 1 if r0 < c else n2 - 1
            qf = q_ref[gs, :].astype(F32)
            kf = k_ref[gs, :].astype(F32)
            qn = qf * lax.rsqrt(jnp.sum(qf * qf, axis=-1, keepdims=True) + 1e-6) * (dk ** -0.5)
            kn = kf * lax.rsqrt(jnp.sum(kf * kf, axis=-1, keepdims=True) + 1e-6)
            kn_s[n, rs, :] = kn.astype(BF16)
            qn_s[n, rs, :] = qn.astype(BF16)
            for e in range(nv):
                i = n * nv + e
                gc = g_s[i, rs, :]
                vf = v_ref[gs, e * dk:(e + 1) * dk].astype(F32)
                beta = jnp.sum(jnp.where(lane_slab == kh * nv + e, beta_ref[gs, :], 0.0), axis=-1, keepdims=True)
                egc = jnp.exp(gc)
                kb = kn * beta
                kb_s[i, rs, :] = kb.astype(BF16)
                rhs_s[i, rs, :dk] = (vf * beta).astype(BF16)
                rhs_s[i, rs, dk:] = (kb * egc).astype(BF16)
                kdec_s[i, rs, :] = kn * jnp.exp(g_s[i, last:last + 1, :] - gc)
                qg_ref[e, gs, :] = (qn * egc).astype(BF16)
        for e in range(nv):
            i = n * nv + e
            dec_ref[e, 2 * n:2 * n + 1, :] = jnp.exp(g_s[i, c - 1:c, :])
            dec_ref[e, 2 * n + 1:2 * n + 2, :] = jnp.exp(g_s[i, n2 - 1:n2, :])
    for n, rows in enumerate(pairs):
        raw = _dot(qn_s[n], kn_s[n], NT)
        for e in range(nv):
            att_ref[e, rows, :] = (raw * dec_s[n * nv + e]).astype(BF16)
            kdt_ref[e, :, rows] = kdec_s[n * nv + e].T.astype(BF16)
    for n, rows in enumerate(pairs):
        for e in range(nv):
            i = n * nv + e
            neg_a = jnp.where(strict, -(_dot(kb_s[i], kn_s[n], NT) * dec_s[i]), 0.0)
            p_s[i] = neg_a.astype(BF16)
            t_s[i] = eye_f + neg_a
    _inverse_stages(p_s, t_s, npairs * nv, c)
    for n, rows in enumerate(pairs):
        for e in range(nv):
            sol = _dot(t_s[n * nv + e].astype(BF16), rhs_s[n * nv + e])
            u_ref[e, rows, :] = sol[:, :dk].astype(u_ref.dtype)
            w_ref[e, rows, :] = sol[:, dk:].astype(BF16)


def _gdn_scan_kernel(w_ref, qg_ref, u_ref, att_ref, kdt_ref, dec_ref, z_ref, g_ref, o_ref,
                     s_ref, ms_s, vp_s, *, group):
    c = CHUNK
    dk = C_HEAD_DIM
    zeros = jnp.zeros((c, dk), BF16)

    @pl.when(pl.program_id(2) == 0)
    def _():
        s_ref[...] = jnp.zeros(s_ref.shape, F32)

    for n in range(w_ref.shape[1] // c):
        rows = slice(n * c, (n + 1) * c)
        pair = slice((n // 2) * 2 * c, (n // 2 + 1) * 2 * c)
        for gi in range(group):
            lhs = jnp.concatenate([w_ref[gi, rows, :], qg_ref[gi, rows, :]], axis=0)
            ms_s[gi] = _dot(lhs, s_ref[gi].astype(BF16))
        for gi in range(group):
            v_new = (u_ref[gi, rows, :] - ms_s[gi, :c, :]).astype(BF16)
            vp_s[gi] = jnp.concatenate([v_new, zeros] if n % 2 == 0 else [zeros, v_new], axis=0)
        for gi in range(group):
            s_ref[gi] = s_ref[gi] * dec_ref[gi, n:n + 1, :] + _dot(kdt_ref[gi, :, pair], vp_s[gi])
        for gi in range(group):
            o = ms_s[gi, c:, :] + _dot(att_ref[gi, rows, :], vp_s[gi])
            o = o * lax.rsqrt(jnp.mean(o * o, axis=-1, keepdims=True) + NORM_EPS) * g_ref[...]
            z = z_ref[rows, gi * dk:(gi + 1) * dk].astype(F32)
            o_ref[rows, gi * dk:(gi + 1) * dk] = (o * _silu(z)).astype(o_ref.dtype)


def gated_deltanet_layer(x, hn, w_in, conv_w, a_log, dt_bias, norm_g, w_out, batch, seq):
    m, d = x.shape
    dk = C_HEAD_DIM
    k_heads = d // dk
    v_heads = 2 * k_heads
    conv_ch = 2 * k_heads * dk + v_heads * dk
    main_w = conv_ch + v_heads * dk
    qkv = gdn_conv_proj(hn, w_in[:, :conv_ch].astype(BF16), conv_w, seq)
    z = matmul(hn, w_in[:, conv_ch:main_w].astype(BF16), BF16, name="gdn_gate_proj")
    pad = jnp.zeros((d, LANES - v_heads), F32)
    w_ba = jnp.concatenate([w_in[:, main_w:main_w + v_heads], pad, w_in[:, main_w + v_heads:], pad], axis=1)
    row_pad = jnp.zeros((LANES - v_heads,), F32)
    alog_row = jnp.concatenate([a_log, row_pad]).reshape(1, LANES)
    dtb_row = jnp.concatenate([dt_bias, row_pad]).reshape(1, LANES)
    beta_all, gc_all, gc_t = gdn_gates(hn, w_ba.astype(BF16), alog_row, dtb_row)

    rows = min(PREP_ROWS, seq)
    nr = seq // rows
    ncr = rows // CHUNK
    hv = v_heads
    nv = v_heads // k_heads
    nchain = nv * rows // (2 * CHUNK)
    npair = rows // (2 * CHUNK)
    bh_t = lambda dt, last: jax.ShapeDtypeStruct((batch, hv, seq, last), dt)
    blk4 = lambda last: pl.BlockSpec((None, nv, rows, last), lambda b, h, i: (b, h, i, 0))
    qoff, koff, voff = 0, k_heads, 2 * k_heads // nv
    w_, qg_, u_, att_, kdt_, dec_ = pl.pallas_call(
        _gdn_prep_kernel,
        out_shape=(bh_t(BF16, dk), bh_t(BF16, dk), bh_t(BF16, dk), bh_t(BF16, 2 * CHUNK),
                   jax.ShapeDtypeStruct((batch, hv, dk, seq), BF16),
                   jax.ShapeDtypeStruct((batch, hv, seq // CHUNK, dk), F32)),
        grid=(batch, k_heads, nr),
        in_specs=[pl.BlockSpec((rows, dk), lambda b, h, i: (b * nr + i, qoff + h)),
                  pl.BlockSpec((rows, dk), lambda b, h, i: (b * nr + i, koff + h)),
                  pl.BlockSpec((rows, nv * dk), lambda b, h, i: (b * nr + i, voff + h)),
                  pl.BlockSpec((rows, LANES), lambda b, h, i: (b * nr + i, 0)),
                  pl.BlockSpec((rows, LANES), lambda b, h, i: (b * nr + i, 0)),
                  pl.BlockSpec((LANES, rows), lambda b, h, i: (0, b * nr + i))],
        out_specs=(blk4(dk), blk4(dk), blk4(dk), blk4(2 * CHUNK),
                   pl.BlockSpec((None, nv, dk, rows), lambda b, h, i: (b, h, 0, i)),
                   pl.BlockSpec((None, nv, ncr, dk), lambda b, h, i: (b, h, i, 0))),
        scratch_shapes=[pltpu.VMEM((nchain, dk, dk), F32), pltpu.VMEM((nchain, dk, dk), F32),
                        pltpu.VMEM((npair, dk, dk), BF16), pltpu.VMEM((nchain, dk, dk), BF16),
                        pltpu.VMEM((nchain, dk, 2 * dk), BF16), pltpu.VMEM((nchain, dk, dk), BF16),
                        pltpu.VMEM((nchain, dk, dk), F32), pltpu.VMEM((npair, dk, dk), BF16),
                        pltpu.VMEM((nchain, dk, dk), F32)],
        compiler_params=_params("parallel", "parallel", "parallel"),
        name="gdn_prep",
    )(qkv, qkv, qkv, beta_all, gc_all, gc_t)

    rows = min(SCAN_ROWS, seq)
    nr = seq // rows
    ncr = rows // CHUNK
    group = GDN_SCAN_HEADS
    gblk = lambda last: pl.BlockSpec((None, group, rows, last), lambda b, h, i: (b, h, i, 0))
    o = pl.pallas_call(
        functools.partial(_gdn_scan_kernel, group=group),
        out_shape=jax.ShapeDtypeStruct((m, hv * dk), BF16),
        grid=(batch, hv // group, nr),
        in_specs=[gblk(dk), gblk(dk), gblk(dk), gblk(2 * CHUNK),
                  pl.BlockSpec((None, group, dk, rows), lambda b, h, i: (b, h, 0, i)),
                  pl.BlockSpec((None, group, ncr, dk), lambda b, h, i: (b, h, i, 0)),
                  pl.BlockSpec((rows, group * dk), lambda b, h, i: (b * nr + i, h)),
                  pl.BlockSpec((1, dk), lambda b, h, i: (0, 0))],
        out_specs=pl.BlockSpec((rows, group * dk), lambda b, h, i: (b * nr + i, h)),
        scratch_shapes=[pltpu.VMEM((group, dk, dk), F32), pltpu.VMEM((group, 2 * CHUNK, dk), F32),
                        pltpu.VMEM((group, 2 * CHUNK, dk), BF16)],
        compiler_params=_params("parallel", "parallel", "arbitrary"),
        name="gdn_scan",
    )(w_, qg_, u_, att_, kdt_, dec_, z, norm_g.reshape(1, dk))
    return matmul_residual(o, w_out.astype(BF16), x, tm=MM_ROWS // 2, name="gdn_out_proj")


def _rwkv_mix_kernel(x_ref, g_ref, mu_ref, w1_ref, a1_ref, o_ref, hw_ref, ha_ref, tail_ref, *, rows_per_seq):
    i = pl.program_id(0)
    x = x_ref[...]
    tm = x.shape[0]
    hn = x * lax.rsqrt(jnp.mean(x * x, axis=-1, keepdims=True) + NORM_EPS) * g_ref[...]

    @pl.when((i * tm) % rows_per_seq == 0)
    def _():
        tail_ref[...] = jnp.zeros(tail_ref.shape, F32)

    prev = jnp.where(_iota(hn.shape, 0) == 0, tail_ref[7:8, :], pltpu.roll(hn, 1, 0))
    tail_ref[...] = hn[tm - 8:]
    xx = prev - hn
    nproj = o_ref.shape[0]
    for c in range(nproj):
        o_ref[c] = (hn + xx * mu_ref[c:c + 1, :]).astype(o_ref.dtype)
    hw_ref[...] = _dot((hn + xx * mu_ref[nproj:nproj + 1, :]).astype(BF16), w1_ref[...])
    ha_ref[...] = _dot((hn + xx * mu_ref[nproj + 1:nproj + 2, :]).astype(BF16), a1_ref[...])


def rwkv_token_mix(x, norm_g, mu, w1, a1, seq, *, tm=ROW_TILE // 2):
    m, d = x.shape
    tm = min(tm, seq)
    nmix = mu.shape[0]
    nproj = nmix - 2
    r = w1.shape[1]
    low = jax.ShapeDtypeStruct((m, r), F32)
    return pl.pallas_call(
        functools.partial(_rwkv_mix_kernel, rows_per_seq=seq),
        out_shape=(jax.ShapeDtypeStruct((nproj, m, d), BF16), low, low),
        grid=(m // tm,),
        in_specs=[pl.BlockSpec((tm, d), lambda i: (i, 0)),
                  pl.BlockSpec((1, d), lambda i: (0, 0)),
                  pl.BlockSpec((nmix, d), lambda i: (0, 0)),
                  pl.BlockSpec((d, r), lambda i: (0, 0)), pl.BlockSpec((d, r), lambda i: (0, 0))],
        out_specs=(pl.BlockSpec((nproj, tm, d), lambda i: (0, i, 0)),
                   pl.BlockSpec((tm, r), lambda i: (i, 0)), pl.BlockSpec((tm, r), lambda i: (i, 0))),
        scratch_shapes=[pltpu.VMEM((8, d), F32)],
        compiler_params=_params("arbitrary"),
        name="rwkv_token_mix",
    )(x, norm_g.reshape(1, d), mu, w1, a1)


def _grouped_mm_kernel(a_ref, w_ref, o_ref):
    o_ref[...] = _dot(a_ref[...], w_ref[...]).astype(o_ref.dtype)


def grouped_matmul(a, w, out_dtype, *, tm=MM_ROWS, tn=MM_COLS):
    g, k, n = w.shape
    m = a.shape[1]
    tm, tn = min(tm, m), min(tn, n)
    return pl.pallas_call(
        _grouped_mm_kernel,
        out_shape=jax.ShapeDtypeStruct((g, m, n), out_dtype),
        grid=(g, n // tn, m // tm),
        in_specs=[pl.BlockSpec((None, tm, k), lambda c, j, i: (c, i, 0)),
                  pl.BlockSpec((None, k, tn), lambda c, j, i: (c, 0, j))],
        out_specs=pl.BlockSpec((None, tm, tn), lambda c, j, i: (c, i, j)),
        compiler_params=_params("parallel", "parallel", "parallel"),
        name="rwkv_rkvg_proj",
    )(a, w)


def _rwkv_lora_kernel(hw_ref, ha_ref, w2_ref, a2_ref, w0_ref, a0_ref, ld_ref, a_ref):
    lw = w0_ref[...] + _dot(jnp.tanh(hw_ref[...]).astype(BF16), w2_ref[...])
    log_w = -_softplus(-lw) - 0.5
    ld_ref[...] = -jnp.exp(log_w)
    a_ref[...] = jax.nn.sigmoid(a0_ref[...] + _dot(ha_ref[...].astype(BF16), a2_ref[...]))


def rwkv_lora(hw, ha, w2, a2, w0, a0, *, tm=ROW_TILE):
    m, r = hw.shape
    d = w2.shape[1]
    tm = min(tm, m)
    row = pl.BlockSpec((1, d), lambda i: (0, 0))
    low = pl.BlockSpec((tm, r), lambda i: (i, 0))
    shape = jax.ShapeDtypeStruct((m, d), F32)
    return pl.pallas_call(
        _rwkv_lora_kernel,
        out_shape=(shape, shape),
        grid=(m // tm,),
        in_specs=[low, low, pl.BlockSpec((r, d), lambda i: (0, 0)), pl.BlockSpec((r, d), lambda i: (0, 0)),
                  row, row],
        out_specs=(pl.BlockSpec((tm, d), lambda i: (i, 0)), pl.BlockSpec((tm, d), lambda i: (i, 0))),
        compiler_params=_params("parallel"),
        name="rwkv_lora",
    )(hw, ha, w2, a2, w0.reshape(1, d), a0.reshape(1, d))


def _head_sum(x, first_head):
    lo = jnp.sum(jnp.where(first_head, x, 0.0), axis=-1, keepdims=True)
    hi = jnp.sum(jnp.where(first_head, 0.0, x), axis=-1, keepdims=True)
    return jnp.where(first_head, lo, hi)


def _rwkv_prep_kernel(r_ref, k_ref, v_ref, ld_ref, a_ref, kk_ref, ka_ref, rk_ref,
                      wt_ref, rt_ref, u_ref, aro_ref, kbt_ref, vb_ref, dec_ref, bonus_ref,
                      kkt_s, kb_s, akk_s, p_s, t_s, drow_s, *, cps):
    c = CHUNK
    n2 = 2 * c
    n_h = B_HEAD_DIM
    npairs = r_ref.shape[0] // n2
    first_head = _iota((1, LANES), 1) < n_h
    tri, strict, diag = _pair_masks()
    tri_b = tri.astype(BF16)
    eye_b = diag.astype(BF16)
    eye_f = diag.astype(F32)
    upper_half = _iota((n2, LANES), 0) < c
    pairs = [slice(n * n2, (n + 1) * n2) for n in range(npairs)]

    for n, rows in enumerate(pairs):
        r, k, v, a = r_ref[rows, :], k_ref[rows, :], v_ref[rows, :], a_ref[rows, :]
        kk_raw = k * kk_ref[...]
        kk = kk_raw * lax.rsqrt(_head_sum(kk_raw * kk_raw, first_head) + 1e-6)
        k2c = k * (1.0 + (a - 1.0) * ka_ref[...])
        bbc = kk * a
        bonus_ref[rows, :] = (_head_sum(r * k2c * rk_ref[...], first_head) * v).astype(bonus_ref.dtype)
        vb_ref[rows, :] = v.astype(BF16)
        ld = ld_ref[rows, :]
        cs = _dot_x2r(tri_b, ld)
        cs_last = jnp.where(upper_half, cs[c - 1:c, :], cs[n2 - 1:n2, :])
        w_inv = jnp.exp(-cs)
        w_end = jnp.exp(cs_last - cs)
        rt_ref[rows, :] = (r * jnp.exp(cs)).astype(BF16)
        kkt_s[n] = (kk * jnp.exp(cs - ld)).astype(BF16)
        kb_s[n] = jnp.concatenate([k2c * w_inv, bbc * w_inv], axis=0).astype(BF16)
        kw, bw = k2c * w_end, -(bbc * w_end)
        for half in range(2):
            hs = slice(half * c, (half + 1) * c)
            kb_end = jnp.concatenate([kw[hs], bw[hs]], axis=0)
            cols = slice((2 * n + half) * n2, (2 * n + half + 1) * n2)
            kbt_ref[:, cols] = kb_end.T.astype(BF16)
        blk, off = divmod(2 * n, cps)
        drow_s[blk, off:off + 1, :] = jnp.exp(cs[c - 1:c, :])
        drow_s[blk, off + 1:off + 2, :] = jnp.exp(cs[n2 - 1:n2, :])
    for blk in range(dec_ref.shape[0] // LANES):
        drow_s[blk, cps:, :] = jnp.zeros((LANES - cps, LANES), F32)
        dec_ref[blk * LANES:(blk + 1) * LANES, :] = _dot_xr(eye_b, drow_s[blk], NT)
    zero_b = jnp.zeros((n2, LANES), BF16)
    strict2 = jnp.concatenate([strict, strict], axis=1)
    tri2 = jnp.concatenate([tri, tri], axis=1)
    sign2 = jnp.where(_iota((n2, 2 * n2), 1) < n2, 1.0, -1.0)
    for n, rows in enumerate(pairs):
        kkt, rt = kkt_s[n], rt_ref[rows, :]
        lhs = jnp.concatenate([jnp.where(first_head, kkt, zero_b), jnp.where(first_head, rt, zero_b),
                               jnp.where(first_head, zero_b, kkt), jnp.where(first_head, zero_b, rt)], axis=0)
        gram = _dot(lhs, kb_s[n], NT)
        for hh in range(2):
            i = 2 * n + hh
            g_kk = jnp.where(strict2, gram[2 * hh * n2:(2 * hh + 1) * n2], 0.0)
            neg_a = -g_kk[:, n2:]
            p_s[i] = neg_a.astype(BF16)
            t_s[i] = eye_f + neg_a
            akk_s[i] = g_kk[:, :n2].astype(BF16)
            g_r = gram[(2 * hh + 1) * n2:(2 * hh + 2) * n2]
            aro_ref[hh, rows, :] = jnp.where(tri2, g_r * sign2, 0.0).astype(BF16)
    _inverse_stages(p_s, t_s, 2 * npairs, c)
    for n, rows in enumerate(pairs):
        av = _dot(jnp.concatenate([akk_s[2 * n], akk_s[2 * n + 1]], axis=0), vb_ref[rows, :])
        p_s[2 * n] = av[:n2].astype(BF16)
        p_s[2 * n + 1] = av[n2:].astype(BF16)
    for n, rows in enumerate(pairs):
        t0, t1 = t_s[2 * n].astype(BF16), t_s[2 * n + 1].astype(BF16)
        wt = _dot(jnp.concatenate([t0, t1], axis=0), kkt_s[n])
        wt_ref[rows, :] = jnp.where(first_head, wt[:n2], wt[n2:]).astype(BF16)
        u_ref[rows, :] = jnp.where(first_head, _dot(t0, p_s[2 * n]), _dot(t1, p_s[2 * n + 1])).astype(u_ref.dtype)


def _rwkv_scan_kernel(wt_ref, rt_ref, u_ref, aro_ref, kbt_ref, vb_ref, dec_ref, bonus_ref, gate_ref,
                      lnw_ref, lnb_ref, o_ref, s_ref, ms_s, xc_s, xp_s, *, group):
    c = CHUNK
    n2 = 2 * c
    n_h = B_HEAD_DIM
    first_head = _iota((1, LANES), 1) < n_h
    same_head = _iota((LANES, LANES), 0) // n_h == _iota((LANES, LANES), 1) // n_h
    zeros = jnp.zeros((c, LANES), BF16)

    def head_mean(x):
        return _head_sum(x, first_head) * (1.0 / n_h)

    @pl.when(pl.program_id(2) == 0)
    def _():
        s_ref[...] = jnp.zeros(s_ref.shape, F32)

    for n in range(u_ref.shape[1] // c):
        rows = slice(n * c, (n + 1) * c)
        for gi in range(group):
            lhs = jnp.concatenate([wt_ref[gi, rows, :], rt_ref[gi, rows, :]], axis=0)
            ms_s[gi] = _dot(lhs, s_ref[gi].astype(BF16))
        for gi in range(group):
            sa = (u_ref[gi, rows, :] + ms_s[gi, :c, :]).astype(BF16)
            vb = vb_ref[gi, rows, :]
            xc_s[gi] = jnp.concatenate([vb, sa], axis=0)
            xp_s[gi] = jnp.concatenate([vb, zeros, sa, zeros] if n % 2 == 0 else [zeros, vb, zeros, sa], axis=0)
        for gi in range(group):
            upd = _dot(kbt_ref[gi, :, n * n2:(n + 1) * n2], xc_s[gi])
            dec_col = jnp.broadcast_to(dec_ref[gi, :, n:n + 1], (LANES, LANES))
            s_ref[gi] = s_ref[gi] * dec_col + jnp.where(same_head, upd, 0.0)
        for gi in range(group):
            both = _dot(jnp.concatenate([aro_ref[gi, 0, rows, :], aro_ref[gi, 1, rows, :]], axis=0), xp_s[gi])
            o = ms_s[gi, c:, :] + jnp.where(first_head, both[:c], both[c:])
            mean = head_mean(o)
            dlt = o - mean
            var = head_mean(dlt * dlt)
            cols = slice(gi * LANES, (gi + 1) * LANES)
            y = dlt * lax.rsqrt(var + B_GN_EPS) * lnw_ref[:, cols] + lnb_ref[:, cols]
            y = y + bonus_ref[rows, cols]
            o_ref[rows, cols] = (y * _silu(gate_ref[rows, cols])).astype(o_ref.dtype)


def rwkv7_layer(x, norm_g, mu, w_rkvg, w0, w_w1, w_w2, a0, w_a1, w_a2, k_k, k_a, r_k, ln_w, ln_b, w_out,
                batch, seq):
    m, d = x.shape
    heads = d // B_HEAD_DIM
    pairs = heads // 2
    order = jnp.array([0, 2, 3, 5, 1, 4])
    lora = w_w1.shape[1]
    padc = lambda w: jnp.pad(w, ((0, 0), (0, LORA_PAD - lora))).astype(BF16)
    padr = lambda w: jnp.pad(w, ((0, LORA_PAD - lora), (0, 0))).astype(BF16)
    xs, hw, ha = rwkv_token_mix(x, norm_g, mu[order], padc(w_w1), padc(w_a1), seq)
    rkvg = grouped_matmul(xs, w_rkvg.astype(BF16), F32)
    ld, a = rwkv_lora(hw, ha, padr(w_w2), padr(w_a2), w0, a0)

    srows = min(SCAN_ROWS, seq)
    rows = min(PREP_ROWS, seq)
    nr = seq // rows
    npair = rows // (2 * CHUNK)
    col = lambda g: pl.BlockSpec((None, rows, LANES), lambda b, p, i, g=g: (g, b * nr + i, p))
    flat = pl.BlockSpec((rows, LANES), lambda b, p, i: (b * nr + i, p))
    prow = pl.BlockSpec((1, LANES), lambda b, p, i: (0, p))
    bp = lambda rws, last, dt: jax.ShapeDtypeStruct((batch, pairs, rws, last), dt)
    pblk = lambda rws, last: pl.BlockSpec((None, None, rws, last), lambda b, p, i: (b, p, i, 0))
    wt_, rt_, u_, aro_, kbt_, vb_, dec_, bonus = pl.pallas_call(
        functools.partial(_rwkv_prep_kernel, cps=srows // CHUNK),
        out_shape=(bp(seq, LANES, BF16), bp(seq, LANES, BF16), bp(seq, LANES, BF16),
                   jax.ShapeDtypeStruct((batch, pairs, 2, seq, 4 * CHUNK), BF16),
                   bp(LANES, 2 * seq, BF16), bp(seq, LANES, BF16), bp(seq // srows * LANES, LANES, F32),
                   jax.ShapeDtypeStruct((m, d), BF16)),
        grid=(batch, pairs, nr),
        in_specs=[col(0), col(1), col(2), flat, flat, prow, prow, prow],
        out_specs=(pblk(rows, LANES), pblk(rows, LANES), pblk(rows, LANES),
                   pl.BlockSpec((None, None, 2, rows, 4 * CHUNK), lambda b, p, i: (b, p, 0, i, 0)),
                   pl.BlockSpec((None, None, LANES, 2 * rows), lambda b, p, i: (b, p, 0, i)),
                   pblk(rows, LANES), pblk(rows // srows * LANES, LANES), flat),
        scratch_shapes=[pltpu.VMEM((npair, LANES, LANES), BF16), pltpu.VMEM((npair, 2 * LANES, LANES), BF16),
                        pltpu.VMEM((2 * npair, LANES, LANES), BF16), pltpu.VMEM((2 * npair, LANES, LANES), BF16),
                        pltpu.VMEM((2 * npair, LANES, LANES), F32),
                        pltpu.VMEM((rows // srows, LANES, LANES), F32)],
        compiler_params=_params("parallel", "parallel", "parallel"),
        name="rwkv_prep",
    )(rkvg, rkvg, rkvg, ld, a, k_k.reshape(1, d), k_a.reshape(1, d), r_k.reshape(1, d))

    rows = srows
    nr = seq // rows
    group = RWKV_SCAN_PAIRS
    gw = group * LANES
    gblk = lambda rws, last: pl.BlockSpec((None, group, rws, last), lambda b, p, i: (b, p, i, 0))
    gflat = pl.BlockSpec((rows, gw), lambda b, p, i: (b * nr + i, p))
    grow = pl.BlockSpec((1, gw), lambda b, p, i: (0, p))
    o = pl.pallas_call(
        functools.partial(_rwkv_scan_kernel, group=group),
        out_shape=jax.ShapeDtypeStruct((m, d), BF16),
        grid=(batch, pairs // group, nr),
        in_specs=[gblk(rows, LANES), gblk(rows, LANES), gblk(rows, LANES),
                  pl.BlockSpec((None, group, 2, rows, 4 * CHUNK), lambda b, p, i: (b, p, 0, i, 0)),
                  pl.BlockSpec((None, group, LANES, 2 * rows), lambda b, p, i: (b, p, 0, i)),
                  gblk(rows, LANES), gblk(LANES, LANES), gflat,
                  pl.BlockSpec((None, rows, gw), lambda b, p, i: (3, b * nr + i, p)),
                  grow, grow],
        out_specs=gflat,
        scratch_shapes=[pltpu.VMEM((group, LANES, LANES), F32), pltpu.VMEM((group, 2 * CHUNK, LANES), F32),
                        pltpu.VMEM((group, 2 * CHUNK, LANES), BF16), pltpu.VMEM((group, 4 * CHUNK, LANES), BF16)],
        compiler_params=_params("parallel", "parallel", "arbitrary"),
        name="rwkv_scan",
    )(wt_, rt_, u_, aro_, kbt_, vb_, dec_, bonus, rkvg, ln_w.reshape(1, d), ln_b.reshape(1, d))
    return matmul_residual(o, w_out.astype(BF16), x, name="rwkv_out_proj")


def kernel(x, p, positions, norm_g, pe_norm_g, pe_w_gate, pe_w_proj, final_norm_g, a_w_in, a_lam, a_subln_g, a_w_out, b_mu, b_w_rkvg, b_w0, b_w_w1, b_w_w2, b_a0, b_w_a1, b_w_a2, b_k_k, b_k_a, b_r_k, b_ln_w, b_ln_b, b_w_out, c_w_in, c_conv_w, c_A_log, c_dt_bias, c_norm_g, c_w_out):
    batch, seq, d = x.shape
    depth = p.shape[0]
    m = batch * seq
    xf = x.reshape(m, d)
    tables = rope_tables(positions)
    for i in range(depth):
        kind = i % N_MIXERS
        j = i // N_MIXERS
        if kind == 0:
            lam_init = 0.8 - 0.6 * math.exp(-0.3 * i)
            hn = rmsnorm(xf, norm_g[i], BF16)
            xf, xb, ssq = diff_attention_layer(xf, hn, tables, a_w_in[j], a_lam[j], a_subln_g[j], a_w_out[j],
                                      batch, seq, lam_init)
        elif kind == 1:
            xf, xb, ssq = rwkv7_layer(xf, norm_g[i], b_mu[j], b_w_rkvg[j], b_w0[j], b_w_w1[j], b_w_w2[j], b_a0[j],
                             b_w_a1[j], b_w_a2[j], b_k_k[j], b_k_a[j], b_r_k[j], b_ln_w[j], b_ln_b[j],
                             b_w_out[j], batch, seq)
        else:
            hn = rmsnorm(xf, norm_g[i], BF16)
            xf, xb, ssq = gated_deltanet_layer(xf, hn, c_w_in[j], c_conv_w[j], c_A_log[j], c_dt_bias[j], c_norm_g[j],
                                      c_w_out[j], batch, seq)
        xf = per_layer_embedding(xf, xb, ssq, pe_norm_g[i], pe_w_gate[i], p[i].reshape(m, -1), pe_w_proj[i])
    return rmsnorm(xf, final_norm_g, F32).reshape(batch, seq, d)
```

```python
import functools
import math

import jax
import jax.numpy as jnp
from jax import lax
from jax.experimental import pallas as pl
from jax.experimental.pallas import tpu as pltpu

F32 = jnp.float32
BF16 = jnp.bfloat16

N_MIXERS = 3
NORM_EPS = 1e-6
LANES = 128
VMEM_LIMIT = 48 * 1024 * 1024

A_HEAD_DIM = 128
A_V_DIM = 2 * A_HEAD_DIM
ROT_DIM = A_HEAD_DIM // 4
ROPE_THETA = 500000.0
SUBLN_EPS = 1e-5
ATTN_BLOCK = 512
ATTN_Q_BLOCKS = 1
ONES_ROWS = 16

B_HEAD_DIM = 64
B_GN_EPS = 64e-5
LORA_PAD = 128

C_HEAD_DIM = 128
C_CONV_WIDTH = 4
CHUNK = 64
SCAN_ROWS = 512
PREP_ROWS = 1024
SLAB_ROWS = 32
GDN_SCAN_HEADS = 16
RWKV_SCAN_PAIRS = 8

MM_ROWS = 1024
MM_COLS = 1024
ROW_TILE = 1024
EPI_COLS = 2048
EPI_SUB = 256

NN = (((1,), (0,)), ((), ()))
NT = (((1,), (1,)), ((), ()))


def _dot(a, b, dims=NN):
    return lax.dot_general(a, b, dims, preferred_element_type=F32)


def _split2(x):
    hi = x.astype(BF16)
    lo = (x - hi.astype(F32)).astype(BF16)
    return hi, lo


def _split3(x):
    hi = x.astype(BF16)
    r = x - hi.astype(F32)
    mid = r.astype(BF16)
    lo = (r - mid.astype(F32)).astype(BF16)
    return hi, mid, lo


def _dot_xr(a_exact, b, dims=NN):
    h, m, l = _split3(b)
    return _dot(a_exact, h, dims) + (_dot(a_exact, m, dims) + _dot(a_exact, l, dims))


def _dot_x2r(a_exact, b, dims=NN):
    h, l = _split2(b)
    return _dot(a_exact, h, dims) + _dot(a_exact, l, dims)


def _iota(shape, dim):
    return lax.broadcasted_iota(jnp.int32, shape, dim)


def _silu(x):
    return x * jax.nn.sigmoid(x)


def _softplus(x):
    return jnp.maximum(x, 0.0) + jnp.log(1.0 + jnp.exp(-jnp.abs(x)))


def _params(*sem):
    return pltpu.CompilerParams(dimension_semantics=sem, vmem_limit_bytes=VMEM_LIMIT)


def _rmsnorm_kernel(x_ref, g_ref, o_ref, *, eps):
    x = x_ref[...]
    y = x * lax.rsqrt(jnp.mean(x * x, axis=-1, keepdims=True) + eps)
    o_ref[...] = (y * g_ref[...]).astype(o_ref.dtype)


def rmsnorm(x, g, out_dtype, *, eps=NORM_EPS, tm=ROW_TILE):
    m, d = x.shape
    tm = min(tm, m)
    return pl.pallas_call(
        functools.partial(_rmsnorm_kernel, eps=eps),
        out_shape=jax.ShapeDtypeStruct((m, d), out_dtype),
        grid=(m // tm,),
        in_specs=[pl.BlockSpec((tm, d), lambda i: (i, 0)), pl.BlockSpec((1, d), lambda i: (0, 0))],
        out_specs=pl.BlockSpec((tm, d), lambda i: (i, 0)),
        compiler_params=_params("parallel"),
        name="rmsnorm",
    )(x, g.reshape(1, d))


def _mm_kernel(a_ref, w_ref, *rest, epilogue):
    o_ref = rest[-1]
    acc = _dot(a_ref[...], w_ref[...])
    if epilogue is not None:
        acc = epilogue(acc, *rest[:-1])
    o_ref[...] = acc.astype(o_ref.dtype)


def matmul(a, w, out_dtype, *, tm=MM_ROWS, tn=MM_COLS, extra=(), extra_specs=(), epilogue=None, name="matmul"):
    m, k = a.shape
    n = w.shape[1]
    tm, tn = min(tm, m), min(tn, n)
    return pl.pallas_call(
        functools.partial(_mm_kernel, epilogue=epilogue),
        out_shape=jax.ShapeDtypeStruct((m, n), out_dtype),
        grid=(n // tn, m // tm),
        in_specs=[pl.BlockSpec((tm, k), lambda j, i: (i, 0)),
                  pl.BlockSpec((k, tn), lambda j, i: (0, j))] + list(extra_specs),
        out_specs=pl.BlockSpec((tm, tn), lambda j, i: (i, j)),
        compiler_params=_params("parallel", "parallel"),
        name=name,
    )(a, w, *extra)


def _out_proj_kernel(a_ref, w_ref, res_ref, x_ref, xb_ref, ssq_ref):
    x = res_ref[...] + _dot(a_ref[...], w_ref[...])
    x_ref[...] = x
    xb_ref[...] = x.astype(BF16)
    ssq_ref[...] = jnp.sum(x * x, axis=-1, keepdims=True)


def matmul_residual(a, w, res, *, tm=MM_ROWS, tn=MM_COLS, name="matmul_residual"):
    m, k = a.shape
    n = w.shape[1]
    tm, tn = min(tm, m), min(tn, n)
    blk = pl.BlockSpec((tm, tn), lambda j, i: (i, j))
    return pl.pallas_call(
        _out_proj_kernel,
        out_shape=(jax.ShapeDtypeStruct((m, n), F32), jax.ShapeDtypeStruct((m, n), BF16),
                   jax.ShapeDtypeStruct((n // tn, m, 1), F32)),
        grid=(n // tn, m // tm),
        in_specs=[pl.BlockSpec((tm, k), lambda j, i: (i, 0)), pl.BlockSpec((k, tn), lambda j, i: (0, j)), blk],
        out_specs=(blk, blk, pl.BlockSpec((None, tm, 1), lambda j, i: (j, i, 0))),
        compiler_params=_params("parallel", "parallel"),
        name=name,
    )(a, w, res)


def _ple_kernel(xb_ref, ssq_ref, wg_ref, p_ref, wp_ref, x_ref, o_ref, *, eps):
    d = xb_ref.shape[1]
    inv_rms = lax.rsqrt(jnp.sum(ssq_ref[...], axis=0) / d + eps)
    gate = jax.nn.sigmoid(inv_rms * _dot(xb_ref[...], wg_ref[...]))
    proj = _dot(p_ref[...].astype(BF16), wp_ref[...])
    o_ref[...] = x_ref[...] + gate * proj


def per_layer_embedding(x, xb, ssq, norm_g, w_gate, p, w_proj, *, tm=MM_ROWS, tn=MM_COLS):
    m, d = x.shape
    pd = p.shape[1]
    nparts = ssq.shape[0]
    tm, tn = min(tm, m), min(tn, d)
    wg = (norm_g[:, None] * w_gate).astype(BF16)
    return pl.pallas_call(
        functools.partial(_ple_kernel, eps=NORM_EPS),
        out_shape=jax.ShapeDtypeStruct((m, d), F32),
        grid=(d // tn, m // tm),
        in_specs=[pl.BlockSpec((tm, d), lambda j, i: (i, 0)),
                  pl.BlockSpec((nparts, tm, 1), lambda j, i: (0, i, 0)),
                  pl.BlockSpec((d, tn), lambda j, i: (0, j)),
                  pl.BlockSpec((tm, pd), lambda j, i: (i, 0)),
                  pl.BlockSpec((pd, tn), lambda j, i: (0, j)),
                  pl.BlockSpec((tm, tn), lambda j, i: (i, j))],
        out_specs=pl.BlockSpec((tm, tn), lambda j, i: (i, j)),
        compiler_params=_params("parallel", "parallel"),
        name="per_layer_embedding",
    )(xb, ssq, wg, p, w_proj.astype(BF16), x)


def _rope_table_kernel(pos_ref, freq_ref, cos_ref, sin_lo_ref, sin_hi_ref):
    half = ROT_DIM // 2
    ang = pos_ref[...].astype(F32) * freq_ref[...]
    lane = _iota(ang.shape, 1)
    c, s = jnp.cos(ang), jnp.sin(ang)
    cos_ref[...] = jnp.where(lane < ROT_DIM, c, 1.0)
    sin_lo_ref[...] = jnp.where(lane < half, -s, 0.0)
    sin_hi_ref[...] = jnp.where((lane >= half) & (lane < ROT_DIM), s, 0.0)


def rope_tables(positions, *, tm=MM_ROWS):
    m = positions.size
    tm = min(tm, m)
    inv_freq = ROPE_THETA ** (-jnp.arange(0, ROT_DIM, 2, dtype=F32) / ROT_DIM)
    freq_row = jnp.concatenate([inv_freq, inv_freq, jnp.zeros((LANES - ROT_DIM,), F32)]).reshape(1, LANES)
    shape = jax.ShapeDtypeStruct((m, LANES), F32)
    spec = pl.BlockSpec((tm, LANES), lambda i: (i, 0))
    return pl.pallas_call(
        _rope_table_kernel,
        out_shape=(shape, shape, shape),
        grid=(m // tm,),
        in_specs=[pl.BlockSpec((tm, 1), lambda i: (i, 0)), pl.BlockSpec((1, LANES), lambda i: (0, 0))],
        out_specs=(spec, spec, spec),
        compiler_params=_params("parallel"),
        name="rope_tables",
    )(positions.reshape(m, 1), freq_row)


def _attn_qk_kernel(a_ref, w_ref, cos_ref, sin_lo_ref, sin_hi_ref, o_ref, *, n_q_blocks, sub):
    j = pl.program_id(0)
    half = ROT_DIM // 2
    scale = jnp.where(j < n_q_blocks, A_HEAD_DIM ** -0.5, 1.0).astype(F32)
    cos, sin_lo, sin_hi = cos_ref[...], sin_lo_ref[...], sin_hi_ref[...]
    a = a_ref[...]

    def epilogue(acc, c0):
        for g in range(acc.shape[1] // LANES):
            x = acc[:, g * LANES:(g + 1) * LANES]
            y = x * cos + pltpu.roll(x, LANES - half, 1) * sin_lo + pltpu.roll(x, half, 1) * sin_hi
            o_ref[:, c0 + g * LANES:c0 + (g + 1) * LANES] = (y * scale).astype(o_ref.dtype)

    starts = list(range(0, w_ref.shape[1], sub))
    acc = _dot(a, w_ref[:, starts[0]:starts[0] + sub])
    for c, c0 in enumerate(starts):
        nxt = _dot(a, w_ref[:, starts[c + 1]:starts[c + 1] + sub]) if c + 1 < len(starts) else None
        epilogue(acc, c0)
        acc = nxt


def attn_qk_proj(hn, w_qk, tables, *, tm=MM_ROWS, tn=EPI_COLS, sub=EPI_SUB):
    m, k = hn.shape
    n = w_qk.shape[1]
    tm, tn = min(tm, m), min(tn, n // 2)
    tspec = pl.BlockSpec((tm, LANES), lambda j, i: (i, 0))
    return pl.pallas_call(
        functools.partial(_attn_qk_kernel, n_q_blocks=n // 2 // tn, sub=min(sub, tn)),
        out_shape=jax.ShapeDtypeStruct((m, n), BF16),
        grid=(n // tn, m // tm),
        in_specs=[pl.BlockSpec((tm, k), lambda j, i: (i, 0)),
                  pl.BlockSpec((k, tn), lambda j, i: (0, j)), tspec, tspec, tspec],
        out_specs=pl.BlockSpec((tm, tn), lambda j, i: (i, j)),
        compiler_params=_params("parallel", "parallel"),
        name="attn_qk_proj",
    )(hn, w_qk, *tables)


def _attn_vt_kernel(w_ref, a_ref, o_ref):
    o_ref[...] = _dot(w_ref[...], a_ref[...], NT).astype(o_ref.dtype)


def attn_v_proj_t(hn, w_t, batch, seq, blk, *, tn=MM_COLS):
    m, k = hn.shape
    n = w_t.shape[0]
    tn = min(tn, n)
    nk = seq // blk
    return pl.pallas_call(
        _attn_vt_kernel,
        out_shape=jax.ShapeDtypeStruct((batch, nk, n, blk), BF16),
        grid=(n // tn, m // blk),
        in_specs=[pl.BlockSpec((tn, k), lambda j, i: (j, 0)),
                  pl.BlockSpec((blk, k), lambda j, i: (i, 0))],
        out_specs=pl.BlockSpec((None, None, tn, blk), lambda j, i: (i // nk, i % nk, j, 0)),
        compiler_params=_params("parallel", "parallel"),
        name="attn_v_proj_t",
    )(w_t, hn)


def _diff_attn_kernel(lam_ref, q_ref, k_ref, vt_ref, z_ref, g_ref, o_ref, m_ref, acc_ref, s_ref,
                      *, bq, bk, lam_init):
    i = pl.program_id(2)
    lam = lam_ref[...]
    lam_full = (jnp.exp(jnp.sum(lam[0:1] * lam[1:2], axis=-1, keepdims=True))
                - jnp.exp(jnp.sum(lam[2:3] * lam[3:4], axis=-1, keepdims=True)) + lam_init)
    m_ref[...] = jnp.full(m_ref.shape, -jnp.inf, F32)
    acc_ref[...] = jnp.zeros(acc_ref.shape, F32)
    q = q_ref[...]
    d = A_HEAD_DIM
    w = A_V_DIM
    ones = jnp.ones((ONES_ROWS, bk), BF16)

    def scores(j, slot, q_lo=0):
        start = pl.multiple_of(j * bk, bk)
        kb = k_ref[pl.ds(start, bk), :]
        for c in range(2):
            s_ref[slot, c, :, q_lo:] = _dot(kb[:, c * d:(c + 1) * d], q[q_lo:, c * d:(c + 1) * d], NT)

    def absorb(j, slot, masked, q_lo=0):
        vt = jnp.concatenate([vt_ref[j], ones], axis=0)
        for c in range(2):
            s = s_ref[slot, c, :, q_lo:]
            if masked:
                kv_pos = j * bk + _iota(s.shape, 0)
                q_pos = i * bq + q_lo + _iota(s.shape, 1)
                s = jnp.where(kv_pos <= q_pos, s, -jnp.inf)
            m_prev = m_ref[c, :, q_lo:]
            m_new = jnp.maximum(m_prev, jnp.max(s, axis=0, keepdims=True))
            alpha = jnp.exp(m_prev - m_new)
            p = jnp.exp(s - m_new)
            acc_ref[c, :, q_lo:] = alpha * acc_ref[c, :, q_lo:] + _dot(vt, p.astype(BF16))
            m_ref[c, :, q_lo:] = m_new

    r = bq // bk
    first_masked = r * i
    scores(0, 0)

    def body(jj, carry):
        scores(2 * jj + 1, 1)
        absorb(2 * jj, 0, False)
        scores(2 * jj + 2, 0)
        absorb(2 * jj + 1, 1, False)
        return carry

    lax.fori_loop(0, first_masked // 2, body, 0)

    def tail(odd):
        base = first_masked - odd
        left = [(base + n, n % 2, n >= odd, max(n - odd, 0) * bk) for n in range(odd + r)]
        for n, (j, slot, masked, q_lo) in enumerate(left):
            if n + 1 < len(left):
                nj, nslot, _, nq_lo = left[n + 1]
                scores(nj, nslot, nq_lo)
            absorb(j, slot, masked, q_lo)

    if r % 2 == 0:
        tail(0)
    else:
        pl.when(first_masked % 2 == 0)(lambda: tail(0))
        pl.when(first_masked % 2 == 1)(lambda: tail(1))

    o = (acc_ref[0, :w, :] / acc_ref[0, w:w + 1, :]
         - lam_full * (acc_ref[1, :w, :] / acc_ref[1, w:w + 1, :]))
    o = o * lax.rsqrt(jnp.mean(o * o, axis=0, keepdims=True) + SUBLN_EPS) * g_ref[...]
    o = (o * (1.0 - lam_init)).T
    o_ref[...] = (o * _silu(z_ref[...].astype(F32))).astype(o_ref.dtype)


def diff_attention_core(qk, z, vt, lam, subln_g, batch, seq, heads, lam_init):
    m = qk.shape[0]
    bk = vt.shape[-1]
    bq = ATTN_Q_BLOCKS * bk
    nq = seq // bq
    nk = seq // bk
    w = A_V_DIM
    return pl.pallas_call(
        functools.partial(_diff_attn_kernel, bq=bq, bk=bk, lam_init=lam_init),
        out_shape=jax.ShapeDtypeStruct((m, heads * w), BF16),
        grid=(batch, heads, nq),
        in_specs=[pl.BlockSpec((4, A_HEAD_DIM), lambda b, h, i: (0, 0)),
                  pl.BlockSpec((bq, w), lambda b, h, i: (b * nq + i, h)),
                  pl.BlockSpec((seq, w), lambda b, h, i: (b, heads + h)),
                  pl.BlockSpec((None, nk, w, bk), lambda b, h, i: (b, 0, h, 0)),
                  pl.BlockSpec((bq, w), lambda b, h, i: (b * nq + i, h)),
                  pl.BlockSpec((w, 1), lambda b, h, i: (0, 0))],
        out_specs=pl.BlockSpec((bq, w), lambda b, h, i: (b * nq + i, h)),
        scratch_shapes=[pltpu.VMEM((2, 1, bq), F32), pltpu.VMEM((2, w + ONES_ROWS, bq), F32),
                        pltpu.VMEM((2, 2, bk, bq), F32)],
        compiler_params=_params("parallel", "parallel", "parallel"),
        name="diff_attention",
    )(lam, qk, qk, vt, z, subln_g.reshape(w, 1))


def diff_attention_layer(x, hn, tables, w_in, lam, subln_g, w_out, batch, seq, lam_init):
    d = x.shape[1]
    heads = d // A_V_DIM
    qk_w = heads * 2 * A_HEAD_DIM
    v_w = heads * A_V_DIM
    blk = min(ATTN_BLOCK, seq // ATTN_Q_BLOCKS)
    w_vt = w_in[:, 2 * qk_w:2 * qk_w + v_w].T.astype(BF16)
    qk = attn_qk_proj(hn, w_in[:, :2 * qk_w].astype(BF16), tables)
    z = matmul(hn, w_in[:, 2 * qk_w + v_w:].astype(BF16), BF16, name="attn_gate_proj")
    vt = attn_v_proj_t(hn, w_vt, batch, seq, blk)
    o = diff_attention_core(qk, z, vt, lam, subln_g, batch, seq, heads, lam_init)
    return matmul_residual(o, w_out.astype(BF16), x, name="attn_out_proj")


def _gdn_conv_kernel(a_ref, w_ref, cw_ref, o_ref, tail_ref, *, rows_per_seq, sub):
    i = pl.program_id(1)
    tm = a_ref.shape[0]

    @pl.when((i * tm) % rows_per_seq == 0)
    def _():
        tail_ref[...] = jnp.zeros(tail_ref.shape, F32)

    a = a_ref[...]
    sub_iota = _iota((8, sub), 0)
    last = C_CONV_WIDTH - 1

    def epilogue(acc, cols):
        tail = tail_ref[:, cols]

        def shifted(s):
            xs = pltpu.roll(acc, s, 0)
            head = jnp.where(sub_iota < s, pltpu.roll(tail, s, 0), xs[:8])
            return jnp.concatenate([head, xs[8:]], axis=0)

        cw = cw_ref[:, cols]
        y = shifted(last) * cw[0:1]
        for t in range(1, last):
            y = y + shifted(last - t) * cw[t:t + 1]
        y = y + acc * cw[last:last + 1]
        tail_ref[:, cols] = acc[tm - 8:]
        o_ref[:, cols] = _silu(y).astype(o_ref.dtype)

    blocks = [slice(c0, c0 + sub) for c0 in range(0, w_ref.shape[1], sub)]
    acc = _dot(a, w_ref[:, blocks[0]])
    for c, cols in enumerate(blocks):
        nxt = _dot(a, w_ref[:, blocks[c + 1]]) if c + 1 < len(blocks) else None
        epilogue(acc, cols)
        acc = nxt


def gdn_conv_proj(hn, w, conv_w, seq, *, tm=MM_ROWS, tn=EPI_COLS, sub=EPI_SUB):
    m, k = hn.shape
    n = w.shape[1]
    tm, tn = min(tm, seq), min(tn, n)
    return pl.pallas_call(
        functools.partial(_gdn_conv_kernel, rows_per_seq=seq, sub=min(sub, tn)),
        out_shape=jax.ShapeDtypeStruct((m, n), BF16),
        grid=(n // tn, m // tm),
        in_specs=[pl.BlockSpec((tm, k), lambda j, i: (i, 0)),
                  pl.BlockSpec((k, tn), lambda j, i: (0, j)),
                  pl.BlockSpec((C_CONV_WIDTH, tn), lambda j, i: (0, j))],
        out_specs=pl.BlockSpec((tm, tn), lambda j, i: (i, j)),
        scratch_shapes=[pltpu.VMEM((8, tn), F32)],
        compiler_params=_params("arbitrary", "arbitrary"),
        name="gdn_conv_proj",
    )(hn, w, conv_w)


def _pair_masks():
    n = 2 * CHUNK
    r, c = _iota((n, n), 0), _iota((n, n), 1)
    same = (r // CHUNK) == (c // CHUNK)
    return same & (r >= c), same & (r > c), r == c


def _inverse_stages(p_ref, t_ref, count, nilpotency):
    for _ in range(int(math.log2(nilpotency)) - 1):
        for i in range(count):
            p_ref[i] = _dot(p_ref[i], p_ref[i]).astype(BF16)
        for i in range(count):
            t = t_ref[i]
            t_ref[i] = t + _dot(t.astype(BF16), p_ref[i])


def _gdn_gates_kernel(a_ref, w_ref, alog_ref, dtb_ref, beta_ref, gc_ref, gct_ref):
    acc = _dot(a_ref[...], w_ref[...])
    beta_ref[...] = jax.nn.sigmoid(acc[:, :LANES])
    g = -jnp.exp(alog_ref[...]) * _softplus(acc[:, LANES:] + dtb_ref[...])
    tri, _, diag = _pair_masks()
    tri_b, eye_b = tri.astype(BF16), diag.astype(BF16)
    n2 = 2 * CHUNK
    for n in range(g.shape[0] // n2):
        rows = slice(n * n2, (n + 1) * n2)
        gc = _dot_xr(tri_b, g[rows])
        gc_ref[rows, :] = gc
        gct_ref[:, rows] = _dot_xr(eye_b, gc, NT)


def gdn_gates(hn, w_ba, alog_row, dtb_row, *, tm=ROW_TILE):
    m, k = hn.shape
    tm = min(tm, m)
    row = pl.BlockSpec((1, LANES), lambda i: (0, 0))
    out = pl.BlockSpec((tm, LANES), lambda i: (i, 0))
    return pl.pallas_call(
        _gdn_gates_kernel,
        out_shape=(jax.ShapeDtypeStruct((m, LANES), F32), jax.ShapeDtypeStruct((m, LANES), F32),
                   jax.ShapeDtypeStruct((LANES, m), F32)),
        grid=(m // tm,),
        in_specs=[pl.BlockSpec((tm, k), lambda i: (i, 0)), pl.BlockSpec((k, 2 * LANES), lambda i: (0, 0)), row, row],
        out_specs=(out, out, pl.BlockSpec((LANES, tm), lambda i: (0, i))),
        compiler_params=_params("parallel"),
        name="gdn_gates",
    )(hn, w_ba, alog_row, dtb_row)


def _gdn_prep_kernel(q_ref, k_ref, v_ref, beta_ref, gc_ref, gct_ref,
                     w_ref, qg_ref, u_ref, att_ref, kdt_ref, dec_ref,
                     g_s, dec_s, kn_s, kb_s, rhs_s, p_s, t_s, qn_s, kdec_s):
    kh = pl.program_id(1)
    c = CHUNK
    n2 = 2 * c
    dk = C_HEAD_DIM
    nv = w_ref.shape[0]
    npairs = q_ref.shape[0] // n2
    tri, strict, diag = _pair_masks()
    eye_f = diag.astype(F32)
    pairs = [slice(n * n2, (n + 1) * n2) for n in range(npairs)]
    lane = _iota((n2, LANES), 1)

    for n, rows in enumerate(pairs):
        for e in range(nv):
            h = kh * nv + e
            gc = jnp.broadcast_to(jnp.sum(jnp.where(lane == h, gc_ref[rows, :], 0.0), axis=-1, keepdims=True),
                                  (n2, dk))
            grp = gct_ref[pl.ds(pl.multiple_of((h // 8) * 8, 8), 8), rows]
            own_row = jnp.sum(jnp.where(_iota(grp.shape, 0) == h % 8, grp, 0.0), axis=0, keepdims=True)
            gc_row = jnp.broadcast_to(own_row, (n2, n2))
            g_s[n * nv + e] = gc
            dec_s[n * nv + e] = jnp.where(tri, jnp.exp(jnp.where(tri, gc - gc_row, 0.0)), 0.0)
    slab = SLAB_ROWS
    lane_slab = _iota((slab, LANES), 1)
    for n, rows in enumerate(pairs):
        for r0 in range(0, n2, slab):
            rs = slice(r0, r0 + slab)
            gs = slice(n * n2 + r0, n * n2 + r0 + slab)
            last = c - 1 if r0 < c else n2 - 1
            qf = q_ref[gs, :].astype(F32)
            kf = k_ref[gs, :].astype(F32)
            qn = qf * lax.rsqrt(jnp.sum(qf * qf, axis=-1, keepdims=True) + 1e-6) * (dk ** -0.5)
            kn = kf * lax.rsqrt(jnp.sum(kf * kf, axis=-1, keepdims=True) + 1e-6)
            kn_s[n, rs, :] = kn.astype(BF16)
            qn_s[n, rs, :] = qn.astype(BF16)
            for e in range(nv):
                i = n * nv + e
                gc = g_s[i, rs, :]
                vf = v_ref[gs, e * dk:(e + 1) * dk].astype(F32)
                beta = jnp.sum(jnp.where(lane_slab == kh * nv + e, beta_ref[gs, :], 0.0), axis=-1, keepdims=True)
                egc = jnp.exp(gc)
                kb = kn * beta
                kb_s[i, rs, :] = kb.astype(BF16)
                rhs_s[i, rs, :dk] = (vf * beta).astype(BF16)
                rhs_s[i, rs, dk:] = (kb * egc).astype(BF16)
                kdec_s[i, rs, :] = kn * jnp.exp(g_s[i, last:last + 1, :] - gc)
                qg_ref[e, gs, :] = (qn * egc).astype(BF16)
        for e in range(nv):
            i = n * nv + e
            dec_ref[e, 2 * n:2 * n + 1, :] = jnp.exp(g_s[i, c - 1:c, :])
            dec_ref[e, 2 * n + 1:2 * n + 2, :] = jnp.exp(g_s[i, n2 - 1:n2, :])
    for n, rows in enumerate(pairs):
        raw = _dot(qn_s[n], kn_s[n], NT)
        for e in range(nv):
            att_ref[e, rows, :] = (raw * dec_s[n * nv + e]).astype(BF16)
            kdt_ref[e, :, rows] = kdec_s[n * nv + e].T.astype(BF16)
    for n, rows in enumerate(pairs):
        for e in range(nv):
            i = n * nv + e
            neg_a = jnp.where(strict, -(_dot(kb_s[i], kn_s[n], NT) * dec_s[i]), 0.0)
            p_s[i] = neg_a.astype(BF16)
            t_s[i] = eye_f + neg_a
    _inverse_stages(p_s, t_s, npairs * nv, c)
    for n, rows in enumerate(pairs):
        for e in range(nv):
            sol = _dot(t_s[n * nv + e].astype(BF16), rhs_s[n * nv + e])
            u_ref[e, rows, :] = sol[:, :dk].astype(u_ref.dtype)
            w_ref[e, rows, :] = sol[:, dk:].astype(BF16)


def _gdn_scan_kernel(w_ref, qg_ref, u_ref, att_ref, kdt_ref, dec_ref, z_ref, g_ref, o_ref,
                     s_ref, ms_s, vp_s, *, group):
    c = CHUNK
    dk = C_HEAD_DIM
    zeros = jnp.zeros((c, dk), BF16)

    @pl.when(pl.program_id(2) == 0)
    def _():
        s_ref[...] = jnp.zeros(s_ref.shape, F32)

    for n in range(w_ref.shape[1] // c):
        rows = slice(n * c, (n + 1) * c)
        pair = slice((n // 2) * 2 * c, (n // 2 + 1) * 2 * c)
        for gi in range(group):
            lhs = jnp.concatenate([w_ref[gi, rows, :], qg_ref[gi, rows, :]], axis=0)
            ms_s[gi] = _dot(lhs, s_ref[gi].astype(BF16))
        for gi in range(group):
            v_new = (u_ref[gi, rows, :] - ms_s[gi, :c, :]).astype(BF16)
            vp_s[gi] = jnp.concatenate([v_new, zeros] if n % 2 == 0 else [zeros, v_new], axis=0)
        for gi in range(group):
            s_ref[gi] = s_ref[gi] * dec_ref[gi, n:n + 1, :] + _dot(kdt_ref[gi, :, pair], vp_s[gi])
        for gi in range(group):
            o = ms_s[gi, c:, :] + _dot(att_ref[gi, rows, :], vp_s[gi])
            o = o * lax.rsqrt(jnp.mean(o * o, axis=-1, keepdims=True) + NORM_EPS) * g_ref[...]
            z = z_ref[rows, gi * dk:(gi + 1) * dk].astype(F32)
            o_ref[rows, gi * dk:(gi + 1) * dk] = (o * _silu(z)).astype(o_ref.dtype)


def gated_deltanet_layer(x, hn, w_in, conv_w, a_log, dt_bias, norm_g, w_out, batch, seq):
    m, d = x.shape
    dk = C_HEAD_DIM
    k_heads = d // dk
    v_heads = 2 * k_heads
    conv_ch = 2 * k_heads * dk + v_heads * dk
    main_w = conv_ch + v_heads * dk
    qkv = gdn_conv_proj(hn, w_in[:, :conv_ch].astype(BF16), conv_w, seq)
    z = matmul(hn, w_in[:, conv_ch:main_w].astype(BF16), BF16, name="gdn_gate_proj")
    pad = jnp.zeros((d, LANES - v_heads), F32)
    w_ba = jnp.concatenate([w_in[:, main_w:main_w + v_heads], pad, w_in[:, main_w + v_heads:], pad], axis=1)
    row_pad = jnp.zeros((LANES - v_heads,), F32)
    alog_row = jnp.concatenate([a_log, row_pad]).reshape(1, LANES)
    dtb_row = jnp.concatenate([dt_bias, row_pad]).reshape(1, LANES)
    beta_all, gc_all, gc_t = gdn_gates(hn, w_ba.astype(BF16), alog_row, dtb_row)

    rows = min(PREP_ROWS, seq)
    nr = seq // rows
    ncr = rows // CHUNK
    hv = v_heads
    nv = v_heads // k_heads
    nchain = nv * rows // (2 * CHUNK)
    npair = rows // (2 * CHUNK)
    bh_t = lambda dt, last: jax.ShapeDtypeStruct((batch, hv, seq, last), dt)
    blk4 = lambda last: pl.BlockSpec((None, nv, rows, last), lambda b, h, i: (b, h, i, 0))
    qoff, koff, voff = 0, k_heads, 2 * k_heads // nv
    w_, qg_, u_, att_, kdt_, dec_ = pl.pallas_call(
        _gdn_prep_kernel,
        out_shape=(bh_t(BF16, dk), bh_t(BF16, dk), bh_t(BF16, dk), bh_t(BF16, 2 * CHUNK),
                   jax.ShapeDtypeStruct((batch, hv, dk, seq), BF16),
                   jax.ShapeDtypeStruct((batch, hv, seq // CHUNK, dk), F32)),
        grid=(batch, k_heads, nr),
        in_specs=[pl.BlockSpec((rows, dk), lambda b, h, i: (b * nr + i, qoff + h)),
                  pl.BlockSpec((rows, dk), lambda b, h, i: (b * nr + i, koff + h)),
                  pl.BlockSpec((rows, nv * dk), lambda b, h, i: (b * nr + i, voff + h)),
                  pl.BlockSpec((rows, LANES), lambda b, h, i: (b * nr + i, 0)),
                  pl.BlockSpec((rows, LANES), lambda b, h, i: (b * nr + i, 0)),
                  pl.BlockSpec((LANES, rows), lambda b, h, i: (0, b * nr + i))],
        out_specs=(blk4(dk), blk4(dk), blk4(dk), blk4(2 * CHUNK),
                   pl.BlockSpec((None, nv, dk, rows), lambda b, h, i: (b, h, 0, i)),
                   pl.BlockSpec((None, nv, ncr, dk), lambda b, h, i: (b, h, i, 0))),
        scratch_shapes=[pltpu.VMEM((nchain, dk, dk), F32), pltpu.VMEM((nchain, dk, dk), F32),
                        pltpu.VMEM((npair, dk, dk), BF16), pltpu.VMEM((nchain, dk, dk), BF16),
                        pltpu.VMEM((nchain, dk, 2 * dk), BF16), pltpu.VMEM((nchain, dk, dk), BF16),
                        pltpu.VMEM((nchain, dk, dk), F32), pltpu.VMEM((npair, dk, dk), BF16),
                        pltpu.VMEM((nchain, dk, dk), F32)],
        compiler_params=_params("parallel", "parallel", "parallel"),
        name="gdn_prep",
    )(qkv, qkv, qkv, beta_all, gc_all, gc_t)

    rows = min(SCAN_ROWS, seq)
    nr = seq // rows
    ncr = rows // CHUNK
    group = GDN_SCAN_HEADS
    gblk = lambda last: pl.BlockSpec((None, group, rows, last), lambda b, h, i: (b, h, i, 0))
    o = pl.pallas_call(
        functools.partial(_gdn_scan_kernel, group=group),
        out_shape=jax.ShapeDtypeStruct((m, hv * dk), BF16),
        grid=(batch, hv // group, nr),
        in_specs=[gblk(dk), gblk(dk), gblk(dk), gblk(2 * CHUNK),
                  pl.BlockSpec((None, group, dk, rows), lambda b, h, i: (b, h, 0, i)),
                  pl.BlockSpec((None, group, ncr, dk), lambda b, h, i: (b, h, i, 0)),
                  pl.BlockSpec((rows, group * dk), lambda b, h, i: (b * nr + i, h)),
                  pl.BlockSpec((1, dk), lambda b, h, i: (0, 0))],
        out_specs=pl.BlockSpec((rows, group * dk), lambda b, h, i: (b * nr + i, h)),
        scratch_shapes=[pltpu.VMEM((group, dk, dk), F32), pltpu.VMEM((group, 2 * CHUNK, dk), F32),
                        pltpu.VMEM((group, 2 * CHUNK, dk), BF16)],
        compiler_params=_params("parallel", "parallel", "arbitrary"),
        name="gdn_scan",
    )(w_, qg_, u_, att_, kdt_, dec_, z, norm_g.reshape(1, dk))
    return matmul_residual(o, w_out.astype(BF16), x, tm=MM_ROWS // 2, name="gdn_out_proj")


def _rwkv_mix_kernel(x_ref, g_ref, mu_ref, w1_ref, a1_ref, o_ref, hw_ref, ha_ref, tail_ref, *, rows_per_seq):
    i = pl.program_id(0)
    x = x_ref[...]
    tm = x.shape[0]
    hn = x * lax.rsqrt(jnp.mean(x * x, axis=-1, keepdims=True) + NORM_EPS) * g_ref[...]

    @pl.when((i * tm) % rows_per_seq == 0)
    def _():
        tail_ref[...] = jnp.zeros(tail_ref.shape, F32)

    prev = jnp.where(_iota(hn.shape, 0) == 0, tail_ref[7:8, :], pltpu.roll(hn, 1, 0))
    tail_ref[...] = hn[tm - 8:]
    xx = prev - hn
    nproj = o_ref.shape[0]
    for c in range(nproj):
        o_ref[c] = (hn + xx * mu_ref[c:c + 1, :]).astype(o_ref.dtype)
    hw_ref[...] = _dot((hn + xx * mu_ref[nproj:nproj + 1, :]).astype(BF16), w1_ref[...])
    ha_ref[...] = _dot((hn + xx * mu_ref[nproj + 1:nproj + 2, :]).astype(BF16), a1_ref[...])


def rwkv_token_mix(x, norm_g, mu, w1, a1, seq, *, tm=ROW_TILE // 2):
    m, d = x.shape
    tm = min(tm, seq)
    nmix = mu.shape[0]
    nproj = nmix - 2
    r = w1.shape[1]
    low = jax.ShapeDtypeStruct((m, r), F32)
    return pl.pallas_call(
        functools.partial(_rwkv_mix_kernel, rows_per_seq=seq),
        out_shape=(jax.ShapeDtypeStruct((nproj, m, d), BF16), low, low),
        grid=(m // tm,),
        in_specs=[pl.BlockSpec((tm, d), lambda i: (i, 0)),
                  pl.BlockSpec((1, d), lambda i: (0, 0)),
                  pl.BlockSpec((nmix, d), lambda i: (0, 0)),
                  pl.BlockSpec((d, r), lambda i: (0, 0)), pl.BlockSpec((d, r), lambda i: (0, 0))],
        out_specs=(pl.BlockSpec((nproj, tm, d), lambda i: (0, i, 0)),
                   pl.BlockSpec((tm, r), lambda i: (i, 0)), pl.BlockSpec((tm, r), lambda i: (i, 0))),
        scratch_shapes=[pltpu.VMEM((8, d), F32)],
        compiler_params=_params("arbitrary"),
        name="rwkv_token_mix",
    )(x, norm_g.reshape(1, d), mu, w1, a1)


def _grouped_mm_kernel(a_ref, w_ref, o_ref):
    o_ref[...] = _dot(a_ref[...], w_ref[...]).astype(o_ref.dtype)


def grouped_matmul(a, w, out_dtype, *, tm=MM_ROWS, tn=MM_COLS):
    g, k, n = w.shape
    m = a.shape[1]
    tm, tn = min(tm, m), min(tn, n)
    return pl.pallas_call(
        _grouped_mm_kernel,
        out_shape=jax.ShapeDtypeStruct((g, m, n), out_dtype),
        grid=(g, n // tn, m // tm),
        in_specs=[pl.BlockSpec((None, tm, k), lambda c, j, i: (c, i, 0)),
                  pl.BlockSpec((None, k, tn), lambda c, j, i: (c, 0, j))],
        out_specs=pl.BlockSpec((None, tm, tn), lambda c, j, i: (c, i, j)),
        compiler_params=_params("parallel", "parallel", "parallel"),
        name="rwkv_rkvg_proj",
    )(a, w)


def _rwkv_lora_kernel(hw_ref, ha_ref, w2_ref, a2_ref, w0_ref, a0_ref, ld_ref, a_ref):
    lw = w0_ref[...] + _dot(jnp.tanh(hw_ref[...]).astype(BF16), w2_ref[...])
    log_w = -_softplus(-lw) - 0.5
    ld_ref[...] = -jnp.exp(log_w)
    a_ref[...] = jax.nn.sigmoid(a0_ref[...] + _dot(ha_ref[...].astype(BF16), a2_ref[...]))


def rwkv_lora(hw, ha, w2, a2, w0, a0, *, tm=ROW_TILE):
    m, r = hw.shape
    d = w2.shape[1]
    tm = min(tm, m)
    row = pl.BlockSpec((1, d), lambda i: (0, 0))
    low = pl.BlockSpec((tm, r), lambda i: (i, 0))
    shape = jax.ShapeDtypeStruct((m, d), F32)
    return pl.pallas_call(
        _rwkv_lora_kernel,
        out_shape=(shape, shape),
        grid=(m // tm,),
        in_specs=[low, low, pl.BlockSpec((r, d), lambda i: (0, 0)), pl.BlockSpec((r, d), lambda i: (0, 0)),
                  row, row],
        out_specs=(pl.BlockSpec((tm, d), lambda i: (i, 0)), pl.BlockSpec((tm, d), lambda i: (i, 0))),
        compiler_params=_params("parallel"),
        name="rwkv_lora",
    )(hw, ha, w2, a2, w0.reshape(1, d), a0.reshape(1, d))


def _head_sum(x, first_head):
    lo = jnp.sum(jnp.where(first_head, x, 0.0), axis=-1, keepdims=True)
    hi = jnp.sum(jnp.where(first_head, 0.0, x), axis=-1, keepdims=True)
    return jnp.where(first_head, lo, hi)


def _rwkv_prep_kernel(r_ref, k_ref, v_ref, ld_ref, a_ref, kk_ref, ka_ref, rk_ref,
                      wt_ref, rt_ref, u_ref, aro_ref, kbt_ref, vb_ref, dec_ref, bonus_ref,
                      kkt_s, kb_s, akk_s, p_s, t_s, drow_s, *, cps):
    c = CHUNK
    n2 = 2 * c
    n_h = B_HEAD_DIM
    npairs = r_ref.shape[0] // n2
    first_head = _iota((1, LANES), 1) < n_h
    tri, strict, diag = _pair_masks()
    tri_b = tri.astype(BF16)
    eye_b = diag.astype(BF16)
    eye_f = diag.astype(F32)
    upper_half = _iota((n2, LANES), 0) < c
    pairs = [slice(n * n2, (n + 1) * n2) for n in range(npairs)]

    for n, rows in enumerate(pairs):
        r, k, v, a = r_ref[rows, :], k_ref[rows, :], v_ref[rows, :], a_ref[rows, :]
        kk_raw = k * kk_ref[...]
        kk = kk_raw * lax.rsqrt(_head_sum(kk_raw * kk_raw, first_head) + 1e-6)
        k2c = k * (1.0 + (a - 1.0) * ka_ref[...])
        bbc = kk * a
        bonus_ref[rows, :] = (_head_sum(r * k2c * rk_ref[...], first_head) * v).astype(bonus_ref.dtype)
        vb_ref[rows, :] = v.astype(BF16)
        ld = ld_ref[rows, :]
        cs = _dot_x2r(tri_b, ld)
        cs_last = jnp.where(upper_half, cs[c - 1:c, :], cs[n2 - 1:n2, :])
        w_inv = jnp.exp(-cs)
        w_end = jnp.exp(cs_last - cs)
        rt_ref[rows, :] = (r * jnp.exp(cs)).astype(BF16)
        kkt_s[n] = (kk * jnp.exp(cs - ld)).astype(BF16)
        kb_s[n] = jnp.concatenate([k2c * w_inv, bbc * w_inv], axis=0).astype(BF16)
        kw, bw = k2c * w_end, -(bbc * w_end)
        for half in range(2):
            hs = slice(half * c, (half + 1) * c)
            kb_end = jnp.concatenate([kw[hs], bw[hs]], axis=0)
            cols = slice((2 * n + half) * n2, (2 * n + half + 1) * n2)
            kbt_ref[:, cols] = kb_end.T.astype(BF16)
        blk, off = divmod(2 * n, cps)
        drow_s[blk, off:off + 1, :] = jnp.exp(cs[c - 1:c, :])
        drow_s[blk, off + 1:off + 2, :] = jnp.exp(cs[n2 - 1:n2, :])
    for blk in range(dec_ref.shape[0] // LANES):
        drow_s[blk, cps:, :] = jnp.zeros((LANES - cps, LANES), F32)
        dec_ref[blk * LANES:(blk + 1) * LANES, :] = _dot_xr(eye_b, drow_s[blk], NT)
    zero_b = jnp.zeros((n2, LANES), BF16)
    strict2 = jnp.concatenate([strict, strict], axis=1)
    tri2 = jnp.concatenate([tri, tri], axis=1)
    sign2 = jnp.where(_iota((n2, 2 * n2), 1) < n2, 1.0, -1.0)
    for n, rows in enumerate(pairs):
        kkt, rt = kkt_s[n], rt_ref[rows, :]
        lhs = jnp.concatenate([jnp.where(first_head, kkt, zero_b), jnp.where(first_head, rt, zero_b),
                               jnp.where(first_head, zero_b, kkt), jnp.where(first_head, zero_b, rt)], axis=0)
        gram = _dot(lhs, kb_s[n], NT)
        for hh in range(2):
            i = 2 * n + hh
            g_kk = jnp.where(strict2, gram[2 * hh * n2:(2 * hh + 1) * n2], 0.0)
            neg_a = -g_kk[:, n2:]
            p_s[i] = neg_a.astype(BF16)
            t_s[i] = eye_f + neg_a
            akk_s[i] = g_kk[:, :n2].astype(BF16)
            g_r = gram[(2 * hh + 1) * n2:(2 * hh + 2) * n2]
            aro_ref[hh, rows, :] = jnp.where(tri2, g_r * sign2, 0.0).astype(BF16)
    _inverse_stages(p_s, t_s, 2 * npairs, c)
    for n, rows in enumerate(pairs):
        av = _dot(jnp.concatenate([akk_s[2 * n], akk_s[2 * n + 1]], axis=0), vb_ref[rows, :])
        p_s[2 * n] = av[:n2].astype(BF16)
        p_s[2 * n + 1] = av[n2:].astype(BF16)
    for n, rows in enumerate(pairs):
        t0, t1 = t_s[2 * n].astype(BF16), t_s[2 * n + 1].astype(BF16)
        wt = _dot(jnp.concatenate([t0, t1], axis=0), kkt_s[n])
        wt_ref[rows, :] = jnp.where(first_head, wt[:n2], wt[n2:]).astype(BF16)
        u_ref[rows, :] = jnp.where(first_head, _dot(t0, p_s[2 * n]), _dot(t1, p_s[2 * n + 1])).astype(u_ref.dtype)


def _rwkv_scan_kernel(wt_ref, rt_ref, u_ref, aro_ref, kbt_ref, vb_ref, dec_ref, bonus_ref, gate_ref,
                      lnw_ref, lnb_ref, o_ref, s_ref, ms_s, xc_s, xp_s, *, group):
    c = CHUNK
    n2 = 2 * c
    n_h = B_HEAD_DIM
    first_head = _iota((1, LANES), 1) < n_h
    same_head = _iota((LANES, LANES), 0) // n_h == _iota((LANES, LANES), 1) // n_h
    zeros = jnp.zeros((c, LANES), BF16)

    def head_mean(x):
        return _head_sum(x, first_head) * (1.0 / n_h)

    @pl.when(pl.program_id(2) == 0)
    def _():
        s_ref[...] = jnp.zeros(s_ref.shape, F32)

    for n in range(u_ref.shape[1] // c):
        rows = slice(n * c, (n + 1) * c)
        for gi in range(group):
            lhs = jnp.concatenate([wt_ref[gi, rows, :], rt_ref[gi, rows, :]], axis=0)
            ms_s[gi] = _dot(lhs, s_ref[gi].astype(BF16))
        for gi in range(group):
            sa = (u_ref[gi, rows, :] + ms_s[gi, :c, :]).astype(BF16)
            vb = vb_ref[gi, rows, :]
            xc_s[gi] = jnp.concatenate([vb, sa], axis=0)
            xp_s[gi] = jnp.concatenate([vb, zeros, sa, zeros] if n % 2 == 0 else [zeros, vb, zeros, sa], axis=0)
        for gi in range(group):
            upd = _dot(kbt_ref[gi, :, n * n2:(n + 1) * n2], xc_s[gi])
            dec_col = jnp.broadcast_to(dec_ref[gi, :, n:n + 1], (LANES, LANES))
            s_ref[gi] = s_ref[gi] * dec_col + jnp.where(same_head, upd, 0.0)
        for gi in range(group):
            both = _dot(jnp.concatenate([aro_ref[gi, 0, rows, :], aro_ref[gi, 1, rows, :]], axis=0), xp_s[gi])
            o = ms_s[gi, c:, :] + jnp.where(first_head, both[:c], both[c:])
            mean = head_mean(o)
            dlt = o - mean
            var = head_mean(dlt * dlt)
            cols = slice(gi * LANES, (gi + 1) * LANES)
            y = dlt * lax.rsqrt(var + B_GN_EPS) * lnw_ref[:, cols] + lnb_ref[:, cols]
            y = y + bonus_ref[rows, cols]
            o_ref[rows, cols] = (y * _silu(gate_ref[rows, cols])).astype(o_ref.dtype)


def rwkv7_layer(x, norm_g, mu, w_rkvg, w0, w_w1, w_w2, a0, w_a1, w_a2, k_k, k_a, r_k, ln_w, ln_b, w_out,
                batch, seq):
    m, d = x.shape
    heads = d // B_HEAD_DIM
    pairs = heads // 2
    order = jnp.array([0, 2, 3, 5, 1, 4])
    lora = w_w1.shape[1]
    padc = lambda w: jnp.pad(w, ((0, 0), (0, LORA_PAD - lora))).astype(BF16)
    padr = lambda w: jnp.pad(w, ((0, LORA_PAD - lora), (0, 0))).astype(BF16)
    xs, hw, ha = rwkv_token_mix(x, norm_g, mu[order], padc(w_w1), padc(w_a1), seq)
    rkvg = grouped_matmul(xs, w_rkvg.astype(BF16), F32)
    ld, a = rwkv_lora(hw, ha, padr(w_w2), padr(w_a2), w0, a0)

    srows = min(SCAN_ROWS, seq)
    rows = min(PREP_ROWS, seq)
    nr = seq // rows
    npair = rows // (2 * CHUNK)
    col = lambda g: pl.BlockSpec((None, rows, LANES), lambda b, p, i, g=g: (g, b * nr + i, p))
    flat = pl.BlockSpec((rows, LANES), lambda b, p, i: (b * nr + i, p))
    prow = pl.BlockSpec((1, LANES), lambda b, p, i: (0, p))
    bp = lambda rws, last, dt: jax.ShapeDtypeStruct((batch, pairs, rws, last), dt)
    pblk = lambda rws, last: pl.BlockSpec((None, None, rws, last), lambda b, p, i: (b, p, i, 0))
    wt_, rt_, u_, aro_, kbt_, vb_, dec_, bonus = pl.pallas_call(
        functools.partial(_rwkv_prep_kernel, cps=srows // CHUNK),
        out_shape=(bp(seq, LANES, BF16), bp(seq, LANES, BF16), bp(seq, LANES, BF16),
                   jax.ShapeDtypeStruct((batch, pairs, 2, seq, 4 * CHUNK), BF16),
                   bp(LANES, 2 * seq, BF16), bp(seq, LANES, BF16), bp(seq // srows * LANES, LANES, F32),
                   jax.ShapeDtypeStruct((m, d), BF16)),
        grid=(batch, pairs, nr),
        in_specs=[col(0), col(1), col(2), flat, flat, prow, prow, prow],
        out_specs=(pblk(rows, LANES), pblk(rows, LANES), pblk(rows, LANES),
                   pl.BlockSpec((None, None, 2, rows, 4 * CHUNK), lambda b, p, i: (b, p, 0, i, 0)),
                   pl.BlockSpec((None, None, LANES, 2 * rows), lambda b, p, i: (b, p, 0, i)),
                   pblk(rows, LANES), pblk(rows // srows * LANES, LANES), flat),
        scratch_shapes=[pltpu.VMEM((npair, LANES, LANES), BF16), pltpu.VMEM((npair, 2 * LANES, LANES), BF16),
                        pltpu.VMEM((2 * npair, LANES, LANES), BF16), pltpu.VMEM((2 * npair, LANES, LANES), BF16),
                        pltpu.VMEM((2 * npair, LANES, LANES), F32),
                        pltpu.VMEM((rows // srows, LANES, LANES), F32)],
        compiler_params=_params("parallel", "parallel", "parallel"),
        name="rwkv_prep",
    )(rkvg, rkvg, rkvg, ld, a, k_k.reshape(1, d), k_a.reshape(1, d), r_k.reshape(1, d))

    rows = srows
    nr = seq // rows
    group = RWKV_SCAN_PAIRS
    gw = group * LANES
    gblk = lambda rws, last: pl.BlockSpec((None, group, rws, last), lambda b, p, i: (b, p, i, 0))
    gflat = pl.BlockSpec((rows, gw), lambda b, p, i: (b * nr + i, p))
    grow = pl.BlockSpec((1, gw), lambda b, p, i: (0, p))
    o = pl.pallas_call(
        functools.partial(_rwkv_scan_kernel, group=group),
        out_shape=jax.ShapeDtypeStruct((m, d), BF16),
        grid=(batch, pairs // group, nr),
        in_specs=[gblk(rows, LANES), gblk(rows, LANES), gblk(rows, LANES),
                  pl.BlockSpec((None, group, 2, rows, 4 * CHUNK), lambda b, p, i: (b, p, 0, i, 0)),
                  pl.BlockSpec((None, group, LANES, 2 * rows), lambda b, p, i: (b, p, 0, i)),
                  gblk(rows, LANES), gblk(LANES, LANES), gflat,
                  pl.BlockSpec((None, rows, gw), lambda b, p, i: (3, b * nr + i, p)),
                  grow, grow],
        out_specs=gflat,
        scratch_shapes=[pltpu.VMEM((group, LANES, LANES), F32), pltpu.VMEM((group, 2 * CHUNK, LANES), F32),
                        pltpu.VMEM((group, 2 * CHUNK, LANES), BF16), pltpu.VMEM((group, 4 * CHUNK, LANES), BF16)],
        compiler_params=_params("parallel", "parallel", "arbitrary"),
        name="rwkv_scan",
    )(wt_, rt_, u_, aro_, kbt_, vb_, dec_, bonus, rkvg, ln_w.reshape(1, d), ln_b.reshape(1, d))
    return matmul_residual(o, w_out.astype(BF16), x, name="rwkv_out_proj")


def kernel(x, p, positions, norm_g, pe_norm_g, pe_w_gate, pe_w_proj, final_norm_g, a_w_in, a_lam, a_subln_g, a_w_out, b_mu, b_w_rkvg, b_w0, b_w_w1, b_w_w2, b_a0, b_w_a1, b_w_a2, b_k_k, b_k_a, b_r_k, b_ln_w, b_ln_b, b_w_out, c_w_in, c_conv_w, c_A_log, c_dt_bias, c_norm_g, c_w_out):
    batch, seq, d = x.shape
    depth = p.shape[0]
    m = batch * seq
    xf = x.reshape(m, d)
    tables = rope_tables(positions)
    for i in range(depth):
        kind = i % N_MIXERS
        j = i // N_MIXERS
        if kind == 0:
            lam_init = 0.8 - 0.6 * math.exp(-0.3 * i)
            hn = rmsnorm(xf, norm_g[i], BF16)
            xf, xb, ssq = diff_attention_layer(xf, hn, tables, a_w_in[j], a_lam[j], a_subln_g[j], a_w_out[j],
                                      batch, seq, lam_init)
        elif kind == 1:
            xf, xb, ssq = rwkv7_layer(xf, norm_g[i], b_mu[j], b_w_rkvg[j], b_w0[j], b_w_w1[j], b_w_w2[j], b_a0[j],
                             b_w_a1[j], b_w_a2[j], b_k_k[j], b_k_a[j], b_r_k[j], b_ln_w[j], b_ln_b[j],
                             b_w_out[j], batch, seq)
        else:
            hn = rmsnorm(xf, norm_g[i], BF16)
            xf, xb, ssq = gated_deltanet_layer(xf, hn, c_w_in[j], c_conv_w[j], c_A_log[j], c_dt_bias[j], c_norm_g[j],
                                      c_w_out[j], batch, seq)
        xf = per_layer_embedding(xf, xb, ssq, pe_norm_g[i], pe_w_gate[i], p[i].reshape(m, -1), pe_w_proj[i])
    return rmsnorm(xf, final_norm_g, F32).reshape(batch, seq, d)
```

```python
import functools
import math

import jax
import jax.numpy as jnp
from jax import lax
from jax.experimental import pallas as pl
from jax.experimental.pallas import tpu as pltpu

F32 = jnp.float32
BF16 = jnp.bfloat16

N_MIXERS = 3
NORM_EPS = 1e-6
LANES = 128
VMEM_LIMIT = 48 * 1024 * 1024

A_HEAD_DIM = 128
A_V_DIM = 2 * A_HEAD_DIM
ROT_DIM = A_HEAD_DIM // 4
ROPE_THETA = 500000.0
SUBLN_EPS = 1e-5
ATTN_BLOCK = 512
ATTN_Q_BLOCKS = 1
ONES_ROWS = 16

B_HEAD_DIM = 64
B_GN_EPS = 64e-5
LORA_PAD = 128

C_HEAD_DIM = 128
C_CONV_WIDTH = 4
CHUNK = 64
SCAN_ROWS = 512
PREP_ROWS = 1024
SLAB_ROWS = 32
GDN_SCAN_HEADS = 16
RWKV_SCAN_PAIRS = 8

MM_ROWS = 1024
MM_COLS = 1024
ROW_TILE = 1024
EPI_COLS = 2048
EPI_SUB = 256

NN = (((1,), (0,)), ((), ()))
NT = (((1,), (1,)), ((), ()))


def _dot(a, b, dims=NN):
    return lax.dot_general(a, b, dims, preferred_element_type=F32)


def _split2(x):
    hi = x.astype(BF16)
    lo = (x - hi.astype(F32)).astype(BF16)
    return hi, lo


def _split3(x):
    hi = x.astype(BF16)
    r = x - hi.astype(F32)
    mid = r.astype(BF16)
    lo = (r - mid.astype(F32)).astype(BF16)
    return hi, mid, lo


def _dot_xr(a_exact, b, dims=NN):
    h, m, l = _split3(b)
    return _dot(a_exact, h, dims) + (_dot(a_exact, m, dims) + _dot(a_exact, l, dims))


def _dot_x2r(a_exact, b, dims=NN):
    h, l = _split2(b)
    return _dot(a_exact, h, dims) + _dot(a_exact, l, dims)


def _iota(shape, dim):
    return lax.broadcasted_iota(jnp.int32, shape, dim)


def _silu(x):
    return x * jax.nn.sigmoid(x)


def _softplus(x):
    return jnp.maximum(x, 0.0) + jnp.log(1.0 + jnp.exp(-jnp.abs(x)))


def _params(*sem, fuse_inputs=None):
    return pltpu.CompilerParams(dimension_semantics=sem, vmem_limit_bytes=VMEM_LIMIT, allow_input_fusion=fuse_inputs)


def _rmsnorm_kernel(x_ref, g_ref, o_ref, *, eps):
    x = x_ref[...]
    y = x * lax.rsqrt(jnp.mean(x * x, axis=-1, keepdims=True) + eps)
    o_ref[...] = (y * g_ref[...]).astype(o_ref.dtype)


def rmsnorm(x, g, out_dtype, *, eps=NORM_EPS, tm=ROW_TILE):
    m, d = x.shape
    tm = min(tm, m)
    return pl.pallas_call(
        functools.partial(_rmsnorm_kernel, eps=eps),
        out_shape=jax.ShapeDtypeStruct((m, d), out_dtype),
        grid=(m // tm,),
        in_specs=[pl.BlockSpec((tm, d), lambda i: (i, 0)), pl.BlockSpec((1, d), lambda i: (0, 0))],
        out_specs=pl.BlockSpec((tm, d), lambda i: (i, 0)),
        compiler_params=_params("parallel"),
        name="rmsnorm",
    )(x, g.reshape(1, d))


def _mm_kernel(a_ref, w_ref, *rest, epilogue):
    o_ref = rest[-1]
    acc = _dot(a_ref[...], w_ref[...])
    if epilogue is not None:
        acc = epilogue(acc, *rest[:-1])
    o_ref[...] = acc.astype(o_ref.dtype)


def matmul(a, w, out_dtype, *, tm=MM_ROWS, tn=MM_COLS, extra=(), extra_specs=(), epilogue=None, name="matmul"):
    m, k = a.shape
    n = w.shape[1]
    tm, tn = min(tm, m), min(tn, n)
    return pl.pallas_call(
        functools.partial(_mm_kernel, epilogue=epilogue),
        out_shape=jax.ShapeDtypeStruct((m, n), out_dtype),
        grid=(n // tn, m // tm),
        in_specs=[pl.BlockSpec((tm, k), lambda j, i: (i, 0)),
                  pl.BlockSpec((k, tn), lambda j, i: (0, j))] + list(extra_specs),
        out_specs=pl.BlockSpec((tm, tn), lambda j, i: (i, j)),
        compiler_params=_params("parallel", "parallel", fuse_inputs=[False, True] + [False] * len(extra)),
        name=name,
    )(a, w, *extra)


def _out_proj_kernel(a_ref, w_ref, res_ref, x_ref, xb_ref, ssq_ref):
    x = res_ref[...] + _dot(a_ref[...], w_ref[...])
    x_ref[...] = x
    xb_ref[...] = x.astype(BF16)
    ssq_ref[...] = jnp.sum(x * x, axis=-1, keepdims=True)


def matmul_residual(a, w, res, *, tm=MM_ROWS, tn=MM_COLS, name="matmul_residual"):
    m, k = a.shape
    n = w.shape[1]
    tm, tn = min(tm, m), min(tn, n)
    blk = pl.BlockSpec((tm, tn), lambda j, i: (i, j))
    return pl.pallas_call(
        _out_proj_kernel,
        out_shape=(jax.ShapeDtypeStruct((m, n), F32), jax.ShapeDtypeStruct((m, n), BF16),
                   jax.ShapeDtypeStruct((n // tn, m, 1), F32)),
        grid=(n // tn, m // tm),
        in_specs=[pl.BlockSpec((tm, k), lambda j, i: (i, 0)), pl.BlockSpec((k, tn), lambda j, i: (0, j)), blk],
        out_specs=(blk, blk, pl.BlockSpec((None, tm, 1), lambda j, i: (j, i, 0))),
        compiler_params=_params("parallel", "parallel", fuse_inputs=[False, True, False]),
        name=name,
    )(a, w, res)


def _ple_kernel(xb_ref, ssq_ref, wg_ref, p_ref, wp_ref, x_ref, o_ref, *, eps):
    d = xb_ref.shape[1]
    inv_rms = lax.rsqrt(jnp.sum(ssq_ref[...], axis=0) / d + eps)
    gate = jax.nn.sigmoid(inv_rms * _dot(xb_ref[...], wg_ref[...]))
    proj = _dot(p_ref[...].astype(BF16), wp_ref[...])
    o_ref[...] = x_ref[...] + gate * proj


def per_layer_embedding(x, xb, ssq, norm_g, w_gate, p, w_proj, *, tm=MM_ROWS, tn=MM_COLS):
    m, d = x.shape
    pd = p.shape[1]
    nparts = ssq.shape[0]
    tm, tn = min(tm, m), min(tn, d)
    wg = (norm_g[:, None] * w_gate).astype(BF16)
    return pl.pallas_call(
        functools.partial(_ple_kernel, eps=NORM_EPS),
        out_shape=jax.ShapeDtypeStruct((m, d), F32),
        grid=(d // tn, m // tm),
        in_specs=[pl.BlockSpec((tm, d), lambda j, i: (i, 0)),
                  pl.BlockSpec((nparts, tm, 1), lambda j, i: (0, i, 0)),
                  pl.BlockSpec((d, tn), lambda j, i: (0, j)),
                  pl.BlockSpec((tm, pd), lambda j, i: (i, 0)),
                  pl.BlockSpec((pd, tn), lambda j, i: (0, j)),
                  pl.BlockSpec((tm, tn), lambda j, i: (i, j))],
        out_specs=pl.BlockSpec((tm, tn), lambda j, i: (i, j)),
        compiler_params=_params("parallel", "parallel"),
        name="per_layer_embedding",
    )(xb, ssq, wg, p, w_proj.astype(BF16), x)


def _rope_table_kernel(pos_ref, freq_ref, cos_ref, sin_lo_ref, sin_hi_ref):
    half = ROT_DIM // 2
    ang = pos_ref[...].astype(F32) * freq_ref[...]
    lane = _iota(ang.shape, 1)
    c, s = jnp.cos(ang), jnp.sin(ang)
    cos_ref[...] = jnp.where(lane < ROT_DIM, c, 1.0)
    sin_lo_ref[...] = jnp.where(lane < half, -s, 0.0)
    sin_hi_ref[...] = jnp.where((lane >= half) & (lane < ROT_DIM), s, 0.0)


def rope_tables(positions, *, tm=MM_ROWS):
    m = positions.size
    tm = min(tm, m)
    inv_freq = ROPE_THETA ** (-jnp.arange(0, ROT_DIM, 2, dtype=F32) / ROT_DIM)
    freq_row = jnp.concatenate([inv_freq, inv_freq, jnp.zeros((LANES - ROT_DIM,), F32)]).reshape(1, LANES)
    shape = jax.ShapeDtypeStruct((m, LANES), F32)
    spec = pl.BlockSpec((tm, LANES), lambda i: (i, 0))
    return pl.pallas_call(
        _rope_table_kernel,
        out_shape=(shape, shape, shape),
        grid=(m // tm,),
        in_specs=[pl.BlockSpec((tm, 1), lambda i: (i, 0)), pl.BlockSpec((1, LANES), lambda i: (0, 0))],
        out_specs=(spec, spec, spec),
        compiler_params=_params("parallel"),
        name="rope_tables",
    )(positions.reshape(m, 1), freq_row)


def _attn_qk_kernel(a_ref, w_ref, cos_ref, sin_lo_ref, sin_hi_ref, o_ref, *, n_q_blocks, sub):
    j = pl.program_id(0)
    half = ROT_DIM // 2
    scale = jnp.where(j < n_q_blocks, A_HEAD_DIM ** -0.5, 1.0).astype(F32)
    cos, sin_lo, sin_hi = cos_ref[...], sin_lo_ref[...], sin_hi_ref[...]
    a = a_ref[...]

    def epilogue(acc, c0):
        for g in range(acc.shape[1] // LANES):
            x = acc[:, g * LANES:(g + 1) * LANES]
            y = x * cos + pltpu.roll(x, LANES - half, 1) * sin_lo + pltpu.roll(x, half, 1) * sin_hi
            o_ref[:, c0 + g * LANES:c0 + (g + 1) * LANES] = (y * scale).astype(o_ref.dtype)

    starts = list(range(0, w_ref.shape[1], sub))
    acc = _dot(a, w_ref[:, starts[0]:starts[0] + sub])
    for c, c0 in enumerate(starts):
        nxt = _dot(a, w_ref[:, starts[c + 1]:starts[c + 1] + sub]) if c + 1 < len(starts) else None
        epilogue(acc, c0)
        acc = nxt


def attn_qk_proj(hn, w_qk, tables, *, tm=MM_ROWS, tn=EPI_COLS, sub=EPI_SUB):
    m, k = hn.shape
    n = w_qk.shape[1]
    tm, tn = min(tm, m), min(tn, n // 2)
    tspec = pl.BlockSpec((tm, LANES), lambda j, i: (i, 0))
    return pl.pallas_call(
        functools.partial(_attn_qk_kernel, n_q_blocks=n // 2 // tn, sub=min(sub, tn)),
        out_shape=jax.ShapeDtypeStruct((m, n), BF16),
        grid=(n // tn, m // tm),
        in_specs=[pl.BlockSpec((tm, k), lambda j, i: (i, 0)),
                  pl.BlockSpec((k, tn), lambda j, i: (0, j)), tspec, tspec, tspec],
        out_specs=pl.BlockSpec((tm, tn), lambda j, i: (i, j)),
        compiler_params=_params("parallel", "parallel"),
        name="attn_qk_proj",
    )(hn, w_qk, *tables)


def _attn_vt_kernel(w_ref, a_ref, o_ref):
    o_ref[...] = _dot(w_ref[...], a_ref[...], NT).astype(o_ref.dtype)


def attn_v_proj_t(hn, w_t, batch, seq, blk, *, tn=MM_COLS):
    m, k = hn.shape
    n = w_t.shape[0]
    tn = min(tn, n)
    nk = seq // blk
    return pl.pallas_call(
        _attn_vt_kernel,
        out_shape=jax.ShapeDtypeStruct((batch, nk, n, blk), BF16),
        grid=(n // tn, m // blk),
        in_specs=[pl.BlockSpec((tn, k), lambda j, i: (j, 0)),
                  pl.BlockSpec((blk, k), lambda j, i: (i, 0))],
        out_specs=pl.BlockSpec((None, None, tn, blk), lambda j, i: (i // nk, i % nk, j, 0)),
        compiler_params=_params("parallel", "parallel"),
        name="attn_v_proj_t",
    )(w_t, hn)


def _diff_attn_kernel(lam_ref, q_ref, k_ref, vt_ref, z_ref, g_ref, o_ref, m_ref, acc_ref, s_ref,
                      *, bq, bk, lam_init):
    i = pl.program_id(2)
    lam = lam_ref[...]
    lam_full = (jnp.exp(jnp.sum(lam[0:1] * lam[1:2], axis=-1, keepdims=True))
                - jnp.exp(jnp.sum(lam[2:3] * lam[3:4], axis=-1, keepdims=True)) + lam_init)
    m_ref[...] = jnp.full(m_ref.shape, -jnp.inf, F32)
    acc_ref[...] = jnp.zeros(acc_ref.shape, F32)
    q = q_ref[...]
    d = A_HEAD_DIM
    w = A_V_DIM
    ones = jnp.ones((ONES_ROWS, bk), BF16)

    def scores(j, slot, q_lo=0):
        start = pl.multiple_of(j * bk, bk)
        kb = k_ref[pl.ds(start, bk), :]
        for c in range(2):
            s_ref[slot, c, :, q_lo:] = _dot(kb[:, c * d:(c + 1) * d], q[q_lo:, c * d:(c + 1) * d], NT)

    def absorb(j, slot, masked, q_lo=0):
        vt = jnp.concatenate([vt_ref[j], ones], axis=0)
        for c in range(2):
            s = s_ref[slot, c, :, q_lo:]
            if masked:
                kv_pos = j * bk + _iota(s.shape, 0)
                q_pos = i * bq + q_lo + _iota(s.shape, 1)
                s = jnp.where(kv_pos <= q_pos, s, -jnp.inf)
            m_prev = m_ref[c, :, q_lo:]
            m_new = jnp.maximum(m_prev, jnp.max(s, axis=0, keepdims=True))
            alpha = jnp.exp(m_prev - m_new)
            p = jnp.exp(s - m_new)
            acc_ref[c, :, q_lo:] = alpha * acc_ref[c, :, q_lo:] + _dot(vt, p.astype(BF16))
            m_ref[c, :, q_lo:] = m_new

    r = bq // bk
    first_masked = r * i
    scores(0, 0)

    def body(jj, carry):
        scores(2 * jj + 1, 1)
        absorb(2 * jj, 0, False)
        scores(2 * jj + 2, 0)
        absorb(2 * jj + 1, 1, False)
        return carry

    lax.fori_loop(0, first_masked // 2, body, 0)

    def tail(odd):
        base = first_masked - odd
        left = [(base + n, n % 2, n >= odd, max(n - odd, 0) * bk) for n in range(odd + r)]
        for n, (j, slot, masked, q_lo) in enumerate(left):
            if n + 1 < len(left):
                nj, nslot, _, nq_lo = left[n + 1]
                scores(nj, nslot, nq_lo)
            absorb(j, slot, masked, q_lo)

    if r % 2 == 0:
        tail(0)
    else:
        pl.when(first_masked % 2 == 0)(lambda: tail(0))
        pl.when(first_masked % 2 == 1)(lambda: tail(1))

    o = (acc_ref[0, :w, :] / acc_ref[0, w:w + 1, :]
         - lam_full * (acc_ref[1, :w, :] / acc_ref[1, w:w + 1, :]))
    o = o * lax.rsqrt(jnp.mean(o * o, axis=0, keepdims=True) + SUBLN_EPS) * g_ref[...]
    o = (o * (1.0 - lam_init)).T
    o_ref[...] = (o * _silu(z_ref[...].astype(F32))).astype(o_ref.dtype)


def diff_attention_core(qk, z, vt, lam, subln_g, batch, seq, heads, lam_init):
    m = qk.shape[0]
    bk = vt.shape[-1]
    bq = ATTN_Q_BLOCKS * bk
    nq = seq // bq
    nk = seq // bk
    w = A_V_DIM
    return pl.pallas_call(
        functools.partial(_diff_attn_kernel, bq=bq, bk=bk, lam_init=lam_init),
        out_shape=jax.ShapeDtypeStruct((m, heads * w), BF16),
        grid=(batch, heads, nq),
        in_specs=[pl.BlockSpec((4, A_HEAD_DIM), lambda b, h, i: (0, 0)),
                  pl.BlockSpec((bq, w), lambda b, h, i: (b * nq + i, h)),
                  pl.BlockSpec((seq, w), lambda b, h, i: (b, heads + h)),
                  pl.BlockSpec((None, nk, w, bk), lambda b, h, i: (b, 0, h, 0)),
                  pl.BlockSpec((bq, w), lambda b, h, i: (b * nq + i, h)),
                  pl.BlockSpec((w, 1), lambda b, h, i: (0, 0))],
        out_specs=pl.BlockSpec((bq, w), lambda b, h, i: (b * nq + i, h)),
        scratch_shapes=[pltpu.VMEM((2, 1, bq), F32), pltpu.VMEM((2, w + ONES_ROWS, bq), F32),
                        pltpu.VMEM((2, 2, bk, bq), F32)],
        compiler_params=_params("parallel", "parallel", "parallel"),
        name="diff_attention",
    )(lam, qk, qk, vt, z, subln_g.reshape(w, 1))


def diff_attention_layer(x, hn, tables, w_in, lam, subln_g, w_out, batch, seq, lam_init):
    d = x.shape[1]
    heads = d // A_V_DIM
    qk_w = heads * 2 * A_HEAD_DIM
    v_w = heads * A_V_DIM
    blk = min(ATTN_BLOCK, seq // ATTN_Q_BLOCKS)
    w_vt = w_in[:, 2 * qk_w:2 * qk_w + v_w].T.astype(BF16)
    qk = attn_qk_proj(hn, w_in[:, :2 * qk_w].astype(BF16), tables)
    z = matmul(hn, w_in[:, 2 * qk_w + v_w:].astype(BF16), BF16, name="attn_gate_proj")
    vt = attn_v_proj_t(hn, w_vt, batch, seq, blk)
    o = diff_attention_core(qk, z, vt, lam, subln_g, batch, seq, heads, lam_init)
    return matmul_residual(o, w_out.astype(BF16), x, name="attn_out_proj")


def _gdn_conv_kernel(a_ref, w_ref, cw_ref, o_ref, tail_ref, *, rows_per_seq, sub):
    i = pl.program_id(1)
    tm = a_ref.shape[0]

    @pl.when((i * tm) % rows_per_seq == 0)
    def _():
        tail_ref[...] = jnp.zeros(tail_ref.shape, F32)

    a = a_ref[...]
    sub_iota = _iota((8, sub), 0)
    last = C_CONV_WIDTH - 1

    def epilogue(acc, cols):
        tail = tail_ref[:, cols]

        def shifted(s):
            xs = pltpu.roll(acc, s, 0)
            head = jnp.where(sub_iota < s, pltpu.roll(tail, s, 0), xs[:8])
            return jnp.concatenate([head, xs[8:]], axis=0)

        cw = cw_ref[:, cols]
        y = shifted(last) * cw[0:1]
        for t in range(1, last):
            y = y + shifted(last - t) * cw[t:t + 1]
        y = y + acc * cw[last:last + 1]
        tail_ref[:, cols] = acc[tm - 8:]
        o_ref[:, cols] = _silu(y).astype(o_ref.dtype)

    blocks = [slice(c0, c0 + sub) for c0 in range(0, w_ref.shape[1], sub)]
    acc = _dot(a, w_ref[:, blocks[0]])
    for c, cols in enumerate(blocks):
        nxt = _dot(a, w_ref[:, blocks[c + 1]]) if c + 1 < len(blocks) else None
        epilogue(acc, cols)
        acc = nxt


def gdn_conv_proj(hn, w, conv_w, seq, *, tm=MM_ROWS, tn=EPI_COLS, sub=EPI_SUB):
    m, k = hn.shape
    n = w.shape[1]
    tm, tn = min(tm, seq), min(tn, n)
    return pl.pallas_call(
        functools.partial(_gdn_conv_kernel, rows_per_seq=seq, sub=min(sub, tn)),
        out_shape=jax.ShapeDtypeStruct((m, n), BF16),
        grid=(n // tn, m // tm),
        in_specs=[pl.BlockSpec((tm, k), lambda j, i: (i, 0)),
                  pl.BlockSpec((k, tn), lambda j, i: (0, j)),
                  pl.BlockSpec((C_CONV_WIDTH, tn), lambda j, i: (0, j))],
        out_specs=pl.BlockSpec((tm, tn), lambda j, i: (i, j)),
        scratch_shapes=[pltpu.VMEM((8, tn), F32)],
        compiler_params=_params("arbitrary", "arbitrary"),
        name="gdn_conv_proj",
    )(hn, w, conv_w)


def _pair_masks():
    n = 2 * CHUNK
    r, c = _iota((n, n), 0), _iota((n, n), 1)
    same = (r // CHUNK) == (c // CHUNK)
    return same & (r >= c), same & (r > c), r == c


def _inverse_stages(p_ref, t_ref, count, nilpotency):
    for _ in range(int(math.log2(nilpotency)) - 1):
        for i in range(count):
            p_ref[i] = _dot(p_ref[i], p_ref[i]).astype(BF16)
        for i in range(count):
            t = t_ref[i]
            t_ref[i] = t + _dot(t.astype(BF16), p_ref[i])


def _gdn_gates_kernel(a_ref, w_ref, alog_ref, dtb_ref, beta_ref, gc_ref, gct_ref):
    acc = _dot(a_ref[...], w_ref[...])
    beta_ref[...] = jax.nn.sigmoid(acc[:, :LANES])
    g = -jnp.exp(alog_ref[...]) * _softplus(acc[:, LANES:] + dtb_ref[...])
    tri, _, diag = _pair_masks()
    tri_b, eye_b = tri.astype(BF16), diag.astype(BF16)
    n2 = 2 * CHUNK
    for n in range(g.shape[0] // n2):
        rows = slice(n * n2, (n + 1) * n2)
        gc = _dot_xr(tri_b, g[rows])
        gc_ref[rows, :] = gc
        gct_ref[:, rows] = _dot_xr(eye_b, gc, NT)


def gdn_gates(hn, w_ba, alog_row, dtb_row, *, tm=ROW_TILE):
    m, k = hn.shape
    tm = min(tm, m)
    row = pl.BlockSpec((1, LANES), lambda i: (0, 0))
    out = pl.BlockSpec((tm, LANES), lambda i: (i, 0))
    return pl.pallas_call(
        _gdn_gates_kernel,
        out_shape=(jax.ShapeDtypeStruct((m, LANES), F32), jax.ShapeDtypeStruct((m, LANES), F32),
                   jax.ShapeDtypeStruct((LANES, m), F32)),
        grid=(m // tm,),
        in_specs=[pl.BlockSpec((tm, k), lambda i: (i, 0)), pl.BlockSpec((k, 2 * LANES), lambda i: (0, 0)), row, row],
        out_specs=(out, out, pl.BlockSpec((LANES, tm), lambda i: (0, i))),
        compiler_params=_params("parallel"),
        name="gdn_gates",
    )(hn, w_ba, alog_row, dtb_row)


def _gdn_prep_kernel(q_ref, k_ref, v_ref, beta_ref, gc_ref, gct_ref,
                     w_ref, qg_ref, u_ref, att_ref, kdt_ref, dec_ref,
                     g_s, dec_s, kn_s, kb_s, rhs_s, p_s, t_s, qn_s, kdec_s):
    kh = pl.program_id(1)
    c = CHUNK
    n2 = 2 * c
    dk = C_HEAD_DIM
    nv = w_ref.shape[0]
    npairs = q_ref.shape[0] // n2
    tri, strict, diag = _pair_masks()
    eye_f = diag.astype(F32)
    pairs = [slice(n * n2, (n + 1) * n2) for n in range(npairs)]
    lane = _iota((n2, LANES), 1)

    for n, rows in enumerate(pairs):
        for e in range(nv):
            h = kh * nv + e
            gc = jnp.broadcast_to(jnp.sum(jnp.where(lane == h, gc_ref[rows, :], 0.0), axis=-1, keepdims=True),
                                  (n2, dk))
            grp = gct_ref[pl.ds(pl.multiple_of((h // 8) * 8, 8), 8), rows]
            own_row = jnp.sum(jnp.where(_iota(grp.shape, 0) == h % 8, grp, 0.0), axis=0, keepdims=True)
            gc_row = jnp.broadcast_to(own_row, (n2, n2))
            g_s[n * nv + e] = gc
            dec_s[n * nv + e] = jnp.where(tri, jnp.exp(jnp.where(tri, gc - gc_row, 0.0)), 0.0)
    slab = SLAB_ROWS
    lane_slab = _iota((slab, LANES), 1)
    for n, rows in enumerate(pairs):
        for r0 in range(0, n2, slab):
            rs = slice(r0, r0 + slab)
            gs = slice(n * n2 + r0, n * n2 + r0 + slab)
            last = c - 1 if r0 < c else n2 - 1
            qf = q_ref[gs, :].astype(F32)
            kf = k_ref[gs, :].astype(F32)
            qn = qf * lax.rsqrt(jnp.sum(qf * qf, axis=-1, keepdims=True) + 1e-6) * (dk ** -0.5)
            kn = kf * lax.rsqrt(jnp.sum(kf * kf, axis=-1, keepdims=True) + 1e-6)
            kn_s[n, rs, :] = kn.astype(BF16)
            qn_s[n, rs, :] = qn.astype(BF16)
            for e in range(nv):
                i = n * nv + e
                gc = g_s[i, rs, :]
                vf = v_ref[gs, e * dk:(e + 1) * dk].astype(F32)
                beta = jnp.sum(jnp.where(lane_slab == kh * nv + e, beta_ref[gs, :], 0.0), axis=-1, keepdims=True)
                egc = jnp.exp(gc)
                kb = kn * beta
                kb_s[i, rs, :] = kb.astype(BF16)
                rhs_s[i, rs, :dk] = (vf * beta).astype(BF16)
                rhs_s[i, rs, dk:] = (kb * egc).astype(BF16)
                kdec_s[i, rs, :] = kn * jnp.exp(g_s[i, last:last + 1, :] - gc)
                qg_ref[e, gs, :] = (qn * egc).astype(BF16)
        for e in range(nv):
            i = n * nv + e
            dec_ref[e, 2 * n:2 * n + 1, :] = jnp.exp(g_s[i, c - 1:c, :])
            dec_ref[e, 2 * n + 1:2 * n + 2, :] = jnp.exp(g_s[i, n2 - 1:n2, :])
    for n, rows in enumerate(pairs):
        raw = _dot(qn_s[n], kn_s[n], NT)
        for e in range(nv):
            att_ref[e, rows, :] = (raw * dec_s[n * nv + e]).astype(BF16)
            kdt_ref[e, :, rows] = kdec_s[n * nv + e].T.astype(BF16)
    for n, rows in enumerate(pairs):
        for e in range(nv):
            i = n * nv + e
            neg_a = jnp.where(strict, -(_dot(kb_s[i], kn_s[n], NT) * dec_s[i]), 0.0)
            p_s[i] = neg_a.astype(BF16)
            t_s[i] = eye_f + neg_a
    _inverse_stages(p_s, t_s, npairs * nv, c)
    for n, rows in enumerate(pairs):
        for e in range(nv):
            sol = _dot(t_s[n * nv + e].astype(BF16), rhs_s[n * nv + e])
            u_ref[e, rows, :] = sol[:, :dk].astype(u_ref.dtype)
            w_ref[e, rows, :] = sol[:, dk:].astype(BF16)


def _gdn_scan_kernel(w_ref, qg_ref, u_ref, att_ref, kdt_ref, dec_ref, z_ref, g_ref, o_ref,
                     s_ref, ms_s, vp_s, *, group):
    c = CHUNK
    dk = C_HEAD_DIM
    zeros = jnp.zeros((c, dk), BF16)

    @pl.when(pl.program_id(2) == 0)
    def _():
        s_ref[...] = jnp.zeros(s_ref.shape, F32)

    for n in range(w_ref.shape[1] // c):
        rows = slice(n * c, (n + 1) * c)
        pair = slice((n // 2) * 2 * c, (n // 2 + 1) * 2 * c)
        for gi in range(group):
            lhs = jnp.concatenate([w_ref[gi, rows, :], qg_ref[gi, rows, :]], axis=0)
            ms_s[gi] = _dot(lhs, s_ref[gi].astype(BF16))
        for gi in range(group):
            v_new = (u_ref[gi, rows, :] - ms_s[gi, :c, :]).astype(BF16)
            vp_s[gi] = jnp.concatenate([v_new, zeros] if n % 2 == 0 else [zeros, v_new], axis=0)
        for gi in range(group):
            s_ref[gi] = s_ref[gi] * dec_ref[gi, n:n + 1, :] + _dot(kdt_ref[gi, :, pair], vp_s[gi])
        for gi in range(group):
            o = ms_s[gi, c:, :] + _dot(att_ref[gi, rows, :], vp_s[gi])
            o = o * lax.rsqrt(jnp.mean(o * o, axis=-1, keepdims=True) + NORM_EPS) * g_ref[...]
            z = z_ref[rows, gi * dk:(gi + 1) * dk].astype(F32)
            o_ref[rows, gi * dk:(gi + 1) * dk] = (o * _silu(z)).astype(o_ref.dtype)


def gated_deltanet_layer(x, hn, w_in, conv_w, a_log, dt_bias, norm_g, w_out, batch, seq):
    m, d = x.shape
    dk = C_HEAD_DIM
    k_heads = d // dk
    v_heads = 2 * k_heads
    conv_ch = 2 * k_heads * dk + v_heads * dk
    main_w = conv_ch + v_heads * dk
    qkv = gdn_conv_proj(hn, w_in[:, :conv_ch].astype(BF16), conv_w, seq)
    z = matmul(hn, w_in[:, conv_ch:main_w].astype(BF16), BF16, name="gdn_gate_proj")
    pad = jnp.zeros((d, LANES - v_heads), F32)
    w_ba = jnp.concatenate([w_in[:, main_w:main_w + v_heads], pad, w_in[:, main_w + v_heads:], pad], axis=1)
    row_pad = jnp.zeros((LANES - v_heads,), F32)
    alog_row = jnp.concatenate([a_log, row_pad]).reshape(1, LANES)
    dtb_row = jnp.concatenate([dt_bias, row_pad]).reshape(1, LANES)
    beta_all, gc_all, gc_t = gdn_gates(hn, w_ba.astype(BF16), alog_row, dtb_row)

    rows = min(PREP_ROWS, seq)
    nr = seq // rows
    ncr = rows // CHUNK
    hv = v_heads
    nv = v_heads // k_heads
    nchain = nv * rows // (2 * CHUNK)
    npair = rows // (2 * CHUNK)
    bh_t = lambda dt, last: jax.ShapeDtypeStruct((batch, hv, seq, last), dt)
    blk4 = lambda last: pl.BlockSpec((None, nv, rows, last), lambda b, h, i: (b, h, i, 0))
    qoff, koff, voff = 0, k_heads, 2 * k_heads // nv
    w_, qg_, u_, att_, kdt_, dec_ = pl.pallas_call(
        _gdn_prep_kernel,
        out_shape=(bh_t(BF16, dk), bh_t(BF16, dk), bh_t(BF16, dk), bh_t(BF16, 2 * CHUNK),
                   jax.ShapeDtypeStruct((batch, hv, dk, seq), BF16),
                   jax.ShapeDtypeStruct((batch, hv, seq // CHUNK, dk), F32)),
        grid=(batch, k_heads, nr),
        in_specs=[pl.BlockSpec((rows, dk), lambda b, h, i: (b * nr + i, qoff + h)),
                  pl.BlockSpec((rows, dk), lambda b, h, i: (b * nr + i, koff + h)),
                  pl.BlockSpec((rows, nv * dk), lambda b, h, i: (b * nr + i, voff + h)),
                  pl.BlockSpec((rows, LANES), lambda b, h, i: (b * nr + i, 0)),
                  pl.BlockSpec((rows, LANES), lambda b, h, i: (b * nr + i, 0)),
                  pl.BlockSpec((LANES, rows), lambda b, h, i: (0, b * nr + i))],
        out_specs=(blk4(dk), blk4(dk), blk4(dk), blk4(2 * CHUNK),
                   pl.BlockSpec((None, nv, dk, rows), lambda b, h, i: (b, h, 0, i)),
                   pl.BlockSpec((None, nv, ncr, dk), lambda b, h, i: (b, h, i, 0))),
        scratch_shapes=[pltpu.VMEM((nchain, dk, dk), F32), pltpu.VMEM((nchain, dk, dk), F32),
                        pltpu.VMEM((npair, dk, dk), BF16), pltpu.VMEM((nchain, dk, dk), BF16),
                        pltpu.VMEM((nchain, dk, 2 * dk), BF16), pltpu.VMEM((nchain, dk, dk), BF16),
                        pltpu.VMEM((nchain, dk, dk), F32), pltpu.VMEM((npair, dk, dk), BF16),
                        pltpu.VMEM((nchain, dk, dk), F32)],
        compiler_params=_params("parallel", "parallel", "parallel"),
        name="gdn_prep",
    )(qkv, qkv, qkv, beta_all, gc_all, gc_t)

    rows = min(SCAN_ROWS, seq)
    nr = seq // rows
    ncr = rows // CHUNK
    group = GDN_SCAN_HEADS
    gblk = lambda last: pl.BlockSpec((None, group, rows, last), lambda b, h, i: (b, h, i, 0))
    o = pl.pallas_call(
        functools.partial(_gdn_scan_kernel, group=group),
        out_shape=jax.ShapeDtypeStruct((m, hv * dk), BF16),
        grid=(batch, hv // group, nr),
        in_specs=[gblk(dk), gblk(dk), gblk(dk), gblk(2 * CHUNK),
                  pl.BlockSpec((None, group, dk, rows), lambda b, h, i: (b, h, 0, i)),
                  pl.BlockSpec((None, group, ncr, dk), lambda b, h, i: (b, h, i, 0)),
                  pl.BlockSpec((rows, group * dk), lambda b, h, i: (b * nr + i, h)),
                  pl.BlockSpec((1, dk), lambda b, h, i: (0, 0))],
        out_specs=pl.BlockSpec((rows, group * dk), lambda b, h, i: (b * nr + i, h)),
        scratch_shapes=[pltpu.VMEM((group, dk, dk), F32), pltpu.VMEM((group, 2 * CHUNK, dk), F32),
                        pltpu.VMEM((group, 2 * CHUNK, dk), BF16)],
        compiler_params=_params("parallel", "parallel", "arbitrary"),
        name="gdn_scan",
    )(w_, qg_, u_, att_, kdt_, dec_, z, norm_g.reshape(1, dk))
    return matmul_residual(o, w_out.astype(BF16), x, tm=MM_ROWS // 2, name="gdn_out_proj")


def _rwkv_mix_kernel(x_ref, g_ref, mu_ref, w1_ref, a1_ref, o_ref, hw_ref, ha_ref, tail_ref, *, rows_per_seq):
    i = pl.program_id(0)
    x = x_ref[...]
    tm = x.shape[0]
    hn = x * lax.rsqrt(jnp.mean(x * x, axis=-1, keepdims=True) + NORM_EPS) * g_ref[...]

    @pl.when((i * tm) % rows_per_seq == 0)
    def _():
        tail_ref[...] = jnp.zeros(tail_ref.shape, F32)

    prev = jnp.where(_iota(hn.shape, 0) == 0, tail_ref[7:8, :], pltpu.roll(hn, 1, 0))
    tail_ref[...] = hn[tm - 8:]
    xx = prev - hn
    nproj = o_ref.shape[0]
    for c in range(nproj):
        o_ref[c] = (hn + xx * mu_ref[c:c + 1, :]).astype(o_ref.dtype)
    hw_ref[...] = _dot((hn + xx * mu_ref[nproj:nproj + 1, :]).astype(BF16), w1_ref[...])
    ha_ref[...] = _dot((hn + xx * mu_ref[nproj + 1:nproj + 2, :]).astype(BF16), a1_ref[...])


def rwkv_token_mix(x, norm_g, mu, w1, a1, seq, *, tm=ROW_TILE // 2):
    m, d = x.shape
    tm = min(tm, seq)
    nmix = mu.shape[0]
    nproj = nmix - 2
    r = w1.shape[1]
    low = jax.ShapeDtypeStruct((m, r), F32)
    return pl.pallas_call(
        functools.partial(_rwkv_mix_kernel, rows_per_seq=seq),
        out_shape=(jax.ShapeDtypeStruct((nproj, m, d), BF16), low, low),
        grid=(m // tm,),
        in_specs=[pl.BlockSpec((tm, d), lambda i: (i, 0)),
                  pl.BlockSpec((1, d), lambda i: (0, 0)),
                  pl.BlockSpec((nmix, d), lambda i: (0, 0)),
                  pl.BlockSpec((d, r), lambda i: (0, 0)), pl.BlockSpec((d, r), lambda i: (0, 0))],
        out_specs=(pl.BlockSpec((nproj, tm, d), lambda i: (0, i, 0)),
                   pl.BlockSpec((tm, r), lambda i: (i, 0)), pl.BlockSpec((tm, r), lambda i: (i, 0))),
        scratch_shapes=[pltpu.VMEM((8, d), F32)],
        compiler_params=_params("arbitrary"),
        name="rwkv_token_mix",
    )(x, norm_g.reshape(1, d), mu, w1, a1)


def _grouped_mm_kernel(a_ref, w_ref, o_ref):
    o_ref[...] = _dot(a_ref[...], w_ref[...]).astype(o_ref.dtype)


def grouped_matmul(a, w, out_dtype, *, tm=MM_ROWS, tn=MM_COLS):
    g, k, n = w.shape
    m = a.shape[1]
    tm, tn = min(tm, m), min(tn, n)
    return pl.pallas_call(
        _grouped_mm_kernel,
        out_shape=jax.ShapeDtypeStruct((g, m, n), out_dtype),
        grid=(g, n // tn, m // tm),
        in_specs=[pl.BlockSpec((None, tm, k), lambda c, j, i: (c, i, 0)),
                  pl.BlockSpec((None, k, tn), lambda c, j, i: (c, 0, j))],
        out_specs=pl.BlockSpec((None, tm, tn), lambda c, j, i: (c, i, j)),
        compiler_params=_params("parallel", "parallel", "parallel"),
        name="rwkv_rkvg_proj",
    )(a, w)


def _rwkv_lora_kernel(hw_ref, ha_ref, w2_ref, a2_ref, w0_ref, a0_ref, ld_ref, a_ref):
    lw = w0_ref[...] + _dot(jnp.tanh(hw_ref[...]).astype(BF16), w2_ref[...])
    log_w = -_softplus(-lw) - 0.5
    ld_ref[...] = -jnp.exp(log_w)
    a_ref[...] = jax.nn.sigmoid(a0_ref[...] + _dot(ha_ref[...].astype(BF16), a2_ref[...]))


def rwkv_lora(hw, ha, w2, a2, w0, a0, *, tm=ROW_TILE):
    m, r = hw.shape
    d = w2.shape[1]
    tm = min(tm, m)
    row = pl.BlockSpec((1, d), lambda i: (0, 0))
    low = pl.BlockSpec((tm, r), lambda i: (i, 0))
    shape = jax.ShapeDtypeStruct((m, d), F32)
    return pl.pallas_call(
        _rwkv_lora_kernel,
        out_shape=(shape, shape),
        grid=(m // tm,),
        in_specs=[low, low, pl.BlockSpec((r, d), lambda i: (0, 0)), pl.BlockSpec((r, d), lambda i: (0, 0)),
                  row, row],
        out_specs=(pl.BlockSpec((tm, d), lambda i: (i, 0)), pl.BlockSpec((tm, d), lambda i: (i, 0))),
        compiler_params=_params("parallel"),
        name="rwkv_lora",
    )(hw, ha, w2, a2, w0.reshape(1, d), a0.reshape(1, d))


def _head_sum(x, first_head):
    lo = jnp.sum(jnp.where(first_head, x, 0.0), axis=-1, keepdims=True)
    hi = jnp.sum(jnp.where(first_head, 0.0, x), axis=-1, keepdims=True)
    return jnp.where(first_head, lo, hi)


def _rwkv_prep_kernel(r_ref, k_ref, v_ref, ld_ref, a_ref, kk_ref, ka_ref, rk_ref,
                      wt_ref, rt_ref, u_ref, aro_ref, kbt_ref, vb_ref, dec_ref, bonus_ref,
                      kkt_s, kb_s, akk_s, p_s, t_s, drow_s, *, cps):
    c = CHUNK
    n2 = 2 * c
    n_h = B_HEAD_DIM
    npairs = r_ref.shape[0] // n2
    first_head = _iota((1, LANES), 1) < n_h
    tri, strict, diag = _pair_masks()
    tri_b = tri.astype(BF16)
    eye_b = diag.astype(BF16)
    eye_f = diag.astype(F32)
    upper_half = _iota((n2, LANES), 0) < c
    pairs = [slice(n * n2, (n + 1) * n2) for n in range(npairs)]

    for n, rows in enumerate(pairs):
        r, k, v, a = r_ref[rows, :], k_ref[rows, :], v_ref[rows, :], a_ref[rows, :]
        kk_raw = k * kk_ref[...]
        kk = kk_raw * lax.rsqrt(_head_sum(kk_raw * kk_raw, first_head) + 1e-6)
        k2c = k * (1.0 + (a - 1.0) * ka_ref[...])
        bbc = kk * a
        bonus_ref[rows, :] = (_head_sum(r * k2c * rk_ref[...], first_head) * v).astype(bonus_ref.dtype)
        vb_ref[rows, :] = v.astype(BF16)
        ld = ld_ref[rows, :]
        cs = _dot_x2r(tri_b, ld)
        cs_last = jnp.where(upper_half, cs[c - 1:c, :], cs[n2 - 1:n2, :])
        w_inv = jnp.exp(-cs)
        w_end = jnp.exp(cs_last - cs)
        rt_ref[rows, :] = (r * jnp.exp(cs)).astype(BF16)
        kkt_s[n] = (kk * jnp.exp(cs - ld)).astype(BF16)
        kb_s[n] = jnp.concatenate([k2c * w_inv, bbc * w_inv], axis=0).astype(BF16)
        kw, bw = k2c * w_end, -(bbc * w_end)
        for half in range(2):
            hs = slice(half * c, (half + 1) * c)
            kb_end = jnp.concatenate([kw[hs], bw[hs]], axis=0)
            cols = slice((2 * n + half) * n2, (2 * n + half + 1) * n2)
            kbt_ref[:, cols] = kb_end.T.astype(BF16)
        blk, off = divmod(2 * n, cps)
        drow_s[blk, off:off + 1, :] = jnp.exp(cs[c - 1:c, :])
        drow_s[blk, off + 1:off + 2, :] = jnp.exp(cs[n2 - 1:n2, :])
    for blk in range(dec_ref.shape[0] // LANES):
        drow_s[blk, cps:, :] = jnp.zeros((LANES - cps, LANES), F32)
        dec_ref[blk * LANES:(blk + 1) * LANES, :] = _dot_xr(eye_b, drow_s[blk], NT)
    zero_b = jnp.zeros((n2, LANES), BF16)
    strict2 = jnp.concatenate([strict, strict], axis=1)
    tri2 = jnp.concatenate([tri, tri], axis=1)
    sign2 = jnp.where(_iota((n2, 2 * n2), 1) < n2, 1.0, -1.0)
    for n, rows in enumerate(pairs):
        kkt, rt = kkt_s[n], rt_ref[rows, :]
        lhs = jnp.concatenate([jnp.where(first_head, kkt, zero_b), jnp.where(first_head, rt, zero_b),
                               jnp.where(first_head, zero_b, kkt), jnp.where(first_head, zero_b, rt)], axis=0)
        gram = _dot(lhs, kb_s[n], NT)
        for hh in range(2):
            i = 2 * n + hh
            g_kk = jnp.where(strict2, gram[2 * hh * n2:(2 * hh + 1) * n2], 0.0)
            neg_a = -g_kk[:, n2:]
            p_s[i] = neg_a.astype(BF16)
            t_s[i] = eye_f + neg_a
            akk_s[i] = g_kk[:, :n2].astype(BF16)
            g_r = gram[(2 * hh + 1) * n2:(2 * hh + 2) * n2]
            aro_ref[hh, rows, :] = jnp.where(tri2, g_r * sign2, 0.0).astype(BF16)
    _inverse_stages(p_s, t_s, 2 * npairs, c)
    for n, rows in enumerate(pairs):
        av = _dot(jnp.concatenate([akk_s[2 * n], akk_s[2 * n + 1]], axis=0), vb_ref[rows, :])
        p_s[2 * n] = av[:n2].astype(BF16)
        p_s[2 * n + 1] = av[n2:].astype(BF16)
    for n, rows in enumerate(pairs):
        t0, t1 = t_s[2 * n].astype(BF16), t_s[2 * n + 1].astype(BF16)
        wt = _dot(jnp.concatenate([t0, t1], axis=0), kkt_s[n])
        wt_ref[rows, :] = jnp.where(first_head, wt[:n2], wt[n2:]).astype(BF16)
        u_ref[rows, :] = jnp.where(first_head, _dot(t0, p_s[2 * n]), _dot(t1, p_s[2 * n + 1])).astype(u_ref.dtype)


def _rwkv_scan_kernel(wt_ref, rt_ref, u_ref, aro_ref, kbt_ref, vb_ref, dec_ref, bonus_ref, gate_ref,
                      lnw_ref, lnb_ref, o_ref, s_ref, ms_s, xc_s, xp_s, *, group):
    c = CHUNK
    n2 = 2 * c
    n_h = B_HEAD_DIM
    first_head = _iota((1, LANES), 1) < n_h
    same_head = _iota((LANES, LANES), 0) // n_h == _iota((LANES, LANES), 1) // n_h
    zeros = jnp.zeros((c, LANES), BF16)

    def head_mean(x):
        return _head_sum(x, first_head) * (1.0 / n_h)

    @pl.when(pl.program_id(2) == 0)
    def _():
        s_ref[...] = jnp.zeros(s_ref.shape, F32)

    for n in range(u_ref.shape[1] // c):
        rows = slice(n * c, (n + 1) * c)
        for gi in range(group):
            lhs = jnp.concatenate([wt_ref[gi, rows, :], rt_ref[gi, rows, :]], axis=0)
            ms_s[gi] = _dot(lhs, s_ref[gi].astype(BF16))
        for gi in range(group):
            sa = (u_ref[gi, rows, :] + ms_s[gi, :c, :]).astype(BF16)
            vb = vb_ref[gi, rows, :]
            xc_s[gi] = jnp.concatenate([vb, sa], axis=0)
            xp_s[gi] = jnp.concatenate([vb, zeros, sa, zeros] if n % 2 == 0 else [zeros, vb, zeros, sa], axis=0)
        for gi in range(group):
            upd = _dot(kbt_ref[gi, :, n * n2:(n + 1) * n2], xc_s[gi])
            dec_col = jnp.broadcast_to(dec_ref[gi, :, n:n + 1], (LANES, LANES))
            s_ref[gi] = s_ref[gi] * dec_col + jnp.where(same_head, upd, 0.0)
        for gi in range(group):
            both = _dot(jnp.concatenate([aro_ref[gi, 0, rows, :], aro_ref[gi, 1, rows, :]], axis=0), xp_s[gi])
            o = ms_s[gi, c:, :] + jnp.where(first_head, both[:c], both[c:])
            mean = head_mean(o)
            dlt = o - mean
            var = head_mean(dlt * dlt)
            cols = slice(gi * LANES, (gi + 1) * LANES)
            y = dlt * lax.rsqrt(var + B_GN_EPS) * lnw_ref[:, cols] + lnb_ref[:, cols]
            y = y + bonus_ref[rows, cols]
            o_ref[rows, cols] = (y * _silu(gate_ref[rows, cols])).astype(o_ref.dtype)


def rwkv7_layer(x, norm_g, mu, w_rkvg, w0, w_w1, w_w2, a0, w_a1, w_a2, k_k, k_a, r_k, ln_w, ln_b, w_out,
                batch, seq):
    m, d = x.shape
    heads = d // B_HEAD_DIM
    pairs = heads // 2
    order = jnp.array([0, 2, 3, 5, 1, 4])
    lora = w_w1.shape[1]
    padc = lambda w: jnp.pad(w, ((0, 0), (0, LORA_PAD - lora))).astype(BF16)
    padr = lambda w: jnp.pad(w, ((0, LORA_PAD - lora), (0, 0))).astype(BF16)
    xs, hw, ha = rwkv_token_mix(x, norm_g, mu[order], padc(w_w1), padc(w_a1), seq)
    rkvg = grouped_matmul(xs, w_rkvg.astype(BF16), F32)
    ld, a = rwkv_lora(hw, ha, padr(w_w2), padr(w_a2), w0, a0)

    srows = min(SCAN_ROWS, seq)
    rows = min(PREP_ROWS, seq)
    nr = seq // rows
    npair = rows // (2 * CHUNK)
    col = lambda g: pl.BlockSpec((None, rows, LANES), lambda b, p, i, g=g: (g, b * nr + i, p))
    flat = pl.BlockSpec((rows, LANES), lambda b, p, i: (b * nr + i, p))
    prow = pl.BlockSpec((1, LANES), lambda b, p, i: (0, p))
    bp = lambda rws, last, dt: jax.ShapeDtypeStruct((batch, pairs, rws, last), dt)
    pblk = lambda rws, last: pl.BlockSpec((None, None, rws, last), lambda b, p, i: (b, p, i, 0))
    wt_, rt_, u_, aro_, kbt_, vb_, dec_, bonus = pl.pallas_call(
        functools.partial(_rwkv_prep_kernel, cps=srows // CHUNK),
        out_shape=(bp(seq, LANES, BF16), bp(seq, LANES, BF16), bp(seq, LANES, BF16),
                   jax.ShapeDtypeStruct((batch, pairs, 2, seq, 4 * CHUNK), BF16),
                   bp(LANES, 2 * seq, BF16), bp(seq, LANES, BF16), bp(seq // srows * LANES, LANES, F32),
                   jax.ShapeDtypeStruct((m, d), BF16)),
        grid=(batch, pairs, nr),
        in_specs=[col(0), col(1), col(2), flat, flat, prow, prow, prow],
        out_specs=(pblk(rows, LANES), pblk(rows, LANES), pblk(rows, LANES),
                   pl.BlockSpec((None, None, 2, rows, 4 * CHUNK), lambda b, p, i: (b, p, 0, i, 0)),
                   pl.BlockSpec((None, None, LANES, 2 * rows), lambda b, p, i: (b, p, 0, i)),
                   pblk(rows, LANES), pblk(rows // srows * LANES, LANES), flat),
        scratch_shapes=[pltpu.VMEM((npair, LANES, LANES), BF16), pltpu.VMEM((npair, 2 * LANES, LANES), BF16),
                        pltpu.VMEM((2 * npair, LANES, LANES), BF16), pltpu.VMEM((2 * npair, LANES, LANES), BF16),
                        pltpu.VMEM((2 * npair, LANES, LANES), F32),
                        pltpu.VMEM((rows // srows, LANES, LANES), F32)],
        compiler_params=_params("parallel", "parallel", "parallel"),
        name="rwkv_prep",
    )(rkvg, rkvg, rkvg, ld, a, k_k.reshape(1, d), k_a.reshape(1, d), r_k.reshape(1, d))

    rows = srows
    nr = seq // rows
    group = RWKV_SCAN_PAIRS
    gw = group * LANES
    gblk = lambda rws, last: pl.BlockSpec((None, group, rws, last), lambda b, p, i: (b, p, i, 0))
    gflat = pl.BlockSpec((rows, gw), lambda b, p, i: (b * nr + i, p))
    grow = pl.BlockSpec((1, gw), lambda b, p, i: (0, p))
    o = pl.pallas_call(
        functools.partial(_rwkv_scan_kernel, group=group),
        out_shape=jax.ShapeDtypeStruct((m, d), BF16),
        grid=(batch, pairs // group, nr),
        in_specs=[gblk(rows, LANES), gblk(rows, LANES), gblk(rows, LANES),
                  pl.BlockSpec((None, group, 2, rows, 4 * CHUNK), lambda b, p, i: (b, p, 0, i, 0)),
                  pl.BlockSpec((None, group, LANES, 2 * rows), lambda b, p, i: (b, p, 0, i)),
                  gblk(rows, LANES), gblk(LANES, LANES), gflat,
                  pl.BlockSpec((None, rows, gw), lambda b, p, i: (3, b * nr + i, p)),
                  grow, grow],
        out_specs=gflat,
        scratch_shapes=[pltpu.VMEM((group, LANES, LANES), F32), pltpu.VMEM((group, 2 * CHUNK, LANES), F32),
                        pltpu.VMEM((group, 2 * CHUNK, LANES), BF16), pltpu.VMEM((group, 4 * CHUNK, LANES), BF16)],
        compiler_params=_params("parallel", "parallel", "arbitrary"),
        name="rwkv_scan",
    )(wt_, rt_, u_, aro_, kbt_, vb_, dec_, bonus, rkvg, ln_w.reshape(1, d), ln_b.reshape(1, d))
    return matmul_residual(o, w_out.astype(BF16), x, name="rwkv_out_proj")


def kernel(x, p, positions, norm_g, pe_norm_g, pe_w_gate, pe_w_proj, final_norm_g, a_w_in, a_lam, a_subln_g, a_w_out, b_mu, b_w_rkvg, b_w0, b_w_w1, b_w_w2, b_a0, b_w_a1, b_w_a2, b_k_k, b_k_a, b_r_k, b_ln_w, b_ln_b, b_w_out, c_w_in, c_conv_w, c_A_log, c_dt_bias, c_norm_g, c_w_out):
    batch, seq, d = x.shape
    depth = p.shape[0]
    m = batch * seq
    xf = x.reshape(m, d)
    tables = rope_tables(positions)
    for i in range(depth):
        kind = i % N_MIXERS
        j = i // N_MIXERS
        if kind == 0:
            lam_init = 0.8 - 0.6 * math.exp(-0.3 * i)
            hn = rmsnorm(xf, norm_g[i], BF16)
            xf, xb, ssq = diff_attention_layer(xf, hn, tables, a_w_in[j], a_lam[j], a_subln_g[j], a_w_out[j],
                                      batch, seq, lam_init)
        elif kind == 1:
            xf, xb, ssq = rwkv7_layer(xf, norm_g[i], b_mu[j], b_w_rkvg[j], b_w0[j], b_w_w1[j], b_w_w2[j], b_a0[j],
                             b_w_a1[j], b_w_a2[j], b_k_k[j], b_k_a[j], b_r_k[j], b_ln_w[j], b_ln_b[j],
                             b_w_out[j], batch, seq)
        else:
            hn = rmsnorm(xf, norm_g[i], BF16)
            xf, xb, ssq = gated_deltanet_layer(xf, hn, c_w_in[j], c_conv_w[j], c_A_log[j], c_dt_bias[j], c_norm_g[j],
                                      c_w_out[j], batch, seq)
        xf = per_layer_embedding(xf, xb, ssq, pe_norm_g[i], pe_w_gate[i], p[i].reshape(m, -1), pe_w_proj[i])
    return rmsnorm(xf, final_norm_g, F32).reshape(batch, seq, d)
```
